```python
import math
import jax, jax.numpy as jnp
from jax import lax
import numpy as np

D_MODEL = 1024
BATCH = 32
SEQ = 2048
DEPTH = 1

N_MEM = 256
D_MIX = D_MODEL
CHUNK = 128
GM_GROUPS = 4
GM_WIDTH = D_MIX // 2
GM_DIM = GM_WIDTH // GM_GROUPS
SB_WIDTH = D_MIX - GM_WIDTH
SB_HEADS = 8
SB_HEAD_DIM = SB_WIDTH // SB_HEADS
Q_BLOCK = 128
IN_COLS = 2 * GM_WIDTH + 3 * SB_WIDTH
X_HEADS = 4
X_HEAD_DIM = D_MODEL // X_HEADS
D_FF = 4 * D_MODEL
EPS = 1e-6

kernel_name = "hybrid_gmlp_stickbreaking_memxattn_block"


def rms_norm(x, g):
    xf = x.astype(jnp.float32)
    y = xf * lax.rsqrt(jnp.mean(xf * xf, axis=-1, keepdims=True) + EPS)
    return (y * g.astype(jnp.float32)).astype(x.dtype)


def head_rms(x):
    xf = x.astype(jnp.float32)
    return (xf * lax.rsqrt(jnp.mean(xf * xf, axis=-1, keepdims=True) + EPS)).astype(x.dtype)


def spatial_gating(u, v, v_norm_g, w_spatial, b_spatial):
    b, s, _ = u.shape
    n_chunks = s // CHUNK
    u = u.reshape(b, n_chunks, CHUNK, GM_GROUPS, GM_DIM)
    v = v.reshape(b, n_chunks, CHUNK, GM_GROUPS, GM_DIM)
    v = rms_norm(v, v_norm_g.reshape(GM_GROUPS, GM_DIM))
    mask = jnp.tril(jnp.ones((CHUNK, CHUNK), dtype=bool))
    w = jnp.where(mask[None], w_spatial, jnp.zeros_like(w_spatial))
    mixed = jnp.einsum('gts,bcsgd->bctgd', w, v) + b_spatial.T[None, None, :, :, None]
    return u * mixed


def stick_breaking_attention(q, k, v):
    s_len = q.shape[2]
    scale = 1.0 / math.sqrt(SB_HEAD_DIM)
    outs = []
    for blk in range(s_len // Q_BLOCK):
        q0 = blk * Q_BLOCK
        k_end = q0 + Q_BLOCK
        qb = q[:, :, q0:k_end]
        kb = k[:, :, :k_end]
        vb = v[:, :, :k_end]
        z = jnp.einsum('bhtd,bhsd->bhts', qb, kb, preferred_element_type=jnp.float32) * scale
        t_idx = q0 + jnp.arange(Q_BLOCK)[:, None]
        s_idx = jnp.arange(k_end)[None, :]
        causal = s_idx < t_idx
        log_beta = jax.nn.log_sigmoid(z)
        log_1m = jnp.where(causal, jax.nn.log_sigmoid(-z), 0.0)
        cum = jnp.cumsum(log_1m, axis=-1)
        total = cum[..., -1:]
        log_a = log_beta + (total - cum)
        a = jnp.where(causal, jnp.exp(log_a), 0.0)
        outs.append(jnp.einsum('bhts,bhsd->bhtd', a.astype(vb.dtype), vb))
    return jnp.concatenate(outs, axis=2)


def mixer(xn, w_in, gm_v_norm_g, w_spatial, b_spatial, head_norm_g, w_out):
    b, s, _ = xn.shape
    proj = xn @ w_in
    u = jax.nn.gelu(proj[..., :GM_WIDTH])
    gv = jax.nn.gelu(proj[..., GM_WIDTH:2 * GM_WIDTH])
    qkv = proj[..., 2 * GM_WIDTH:].reshape(b, s, 3, SB_HEADS, SB_HEAD_DIM)
    q = qkv[:, :, 0].transpose(0, 2, 1, 3)
    k = qkv[:, :, 1].transpose(0, 2, 1, 3)
    v = qkv[:, :, 2].transpose(0, 2, 1, 3)
    a_out = head_rms(spatial_gating(u, gv, gm_v_norm_g, w_spatial, b_spatial)).reshape(b, s, GM_WIDTH)
    sb = stick_breaking_attention(q, k, v).transpose(0, 2, 1, 3)
    b_out = head_rms(sb).reshape(b, s, SB_WIDTH)
    merged = jnp.concatenate([a_out, b_out], axis=-1) * head_norm_g
    return merged @ w_out


def cross_attention(hn, mem_n, w_cq, w_ckv, w_co):
    b, s, _ = hn.shape
    m = mem_n.shape[1]
    q = (hn @ w_cq).reshape(b, s, X_HEADS, X_HEAD_DIM)
    kv = (mem_n @ w_ckv).reshape(b, m, 2, X_HEADS, X_HEAD_DIM)
    k, v = kv[:, :, 0], kv[:, :, 1]
    scores = jnp.einsum('bshd,bmhd->bhsm', q, k, preferred_element_type=jnp.float32) / math.sqrt(X_HEAD_DIM)
    p = jax.nn.softmax(scores, axis=-1)
    o = jnp.einsum('bhsm,bmhd->bshd', p.astype(v.dtype), v).reshape(b, s, D_MODEL)
    return o @ w_co


def sq_relu_mlp(hn, w_ff1, w_ff2):
    h = jax.nn.relu(hn @ w_ff1)
    return (h * h) @ w_ff2


def _fwd_setup_inputs(seed: int = 0) -> dict:
    key = jax.random.key(seed)
    ks = jax.random.split(key, 20)
    f32 = jnp.float32

    def nrm(k, shape, scale):
        return jax.random.normal(k, shape, f32) * scale

    def gain(k, shape):
        return 1.0 + 0.02 * jax.random.normal(k, shape, f32)

    L = DEPTH
    return {
        "x": jax.random.normal(ks[0], (BATCH, SEQ, D_MODEL), f32),
        "mem": jax.random.normal(ks[1], (BATCH, N_MEM, D_MODEL), f32),
        "norm_mix_g": gain(ks[2], (L, D_MODEL)),
        "w_in": nrm(ks[3], (L, D_MODEL, IN_COLS), D_MODEL ** -0.5),
        "gm_v_norm_g": gain(ks[4], (L, GM_WIDTH)),
        "w_spatial": nrm(ks[5], (L, GM_GROUPS, CHUNK, CHUNK), 0.5 * CHUNK ** -0.5),
        "b_spatial": 1.0 + 0.01 * jax.random.normal(ks[6], (L, GM_GROUPS, CHUNK), f32),
        "head_norm_g": gain(ks[7], (L, D_MIX)),
        "w_out": nrm(ks[8], (L, D_MIX, D_MODEL), D_MIX ** -0.5),
        "norm_cross_g": gain(ks[9], (L, D_MODEL)),
        "norm_mem_g": gain(ks[10], (L, D_MODEL)),
        "w_cq": nrm(ks[11], (L, D_MODEL, D_MODEL), D_MODEL ** -0.5),
        "w_ckv": nrm(ks[12], (L, D_MODEL, 2 * D_MODEL), D_MODEL ** -0.5),
        "w_co": nrm(ks[13], (L, D_MODEL, D_MODEL), D_MODEL ** -0.5),
        "norm_ffn_g": gain(ks[14], (L, D_MODEL)),
        "w_ff1": nrm(ks[15], (L, D_MODEL, D_FF), D_MODEL ** -0.5),
        "w_ff2": nrm(ks[16], (L, D_FF, D_MODEL), D_FF ** -0.5),
        "norm_final_g": gain(ks[17], (D_MODEL,)),
    }


def _fwd_reference(x, mem, norm_mix_g, w_in, gm_v_norm_g, w_spatial, b_spatial, head_norm_g, w_out,
              norm_cross_g, norm_mem_g, w_cq, w_ckv, w_co, norm_ffn_g, w_ff1, w_ff2, norm_final_g):
    h = x
    for l in range(DEPTH):
        xn = rms_norm(h, norm_mix_g[l])
        h = h + mixer(xn, w_in[l], gm_v_norm_g[l], w_spatial[l], b_spatial[l], head_norm_g[l], w_out[l])
        hn = rms_norm(h, norm_cross_g[l])
        mem_n = rms_norm(mem, norm_mem_g[l])
        h = h + cross_attention(hn, mem_n, w_cq[l], w_ckv[l], w_co[l])
        hn = rms_norm(h, norm_ffn_g[l])
        h = h + sq_relu_mlp(hn, w_ff1[l], w_ff2[l])
    return rms_norm(h, norm_final_g)


import jax as _jax
import jax.numpy as _jnp

TWIN_FORMAT = 'train_step'
FWD_PARAMS = ['x', 'mem', 'norm_mix_g', 'w_in', 'gm_v_norm_g', 'w_spatial', 'b_spatial', 'head_norm_g', 'w_out', 'norm_cross_g', 'norm_mem_g', 'w_cq', 'w_ckv', 'w_co', 'norm_ffn_g', 'w_ff1', 'w_ff2', 'norm_final_g']
TWIN_WEIGHTS = ['norm_mix_g', 'w_in', 'gm_v_norm_g', 'w_spatial', 'b_spatial', 'head_norm_g', 'w_out', 'norm_cross_g', 'norm_mem_g', 'w_cq', 'w_ckv', 'w_co', 'norm_ffn_g', 'w_ff1', 'w_ff2', 'norm_final_g']
TWIN_DIFF_INPUT = 'x'
TWIN_INPUTS = ['x', 'mem', 'norm_mix_g', 'w_in', 'gm_v_norm_g', 'w_spatial', 'b_spatial', 'head_norm_g', 'w_out', 'norm_cross_g', 'norm_mem_g', 'w_cq', 'w_ckv', 'w_co', 'norm_ffn_g', 'w_ff1', 'w_ff2', 'norm_final_g', 'loss_target', 'm_norm_mix_g', 'm_w_in', 'm_gm_v_norm_g', 'm_w_spatial', 'm_b_spatial', 'm_head_norm_g', 'm_w_out', 'm_norm_cross_g', 'm_norm_mem_g', 'm_w_cq', 'm_w_ckv', 'm_w_co', 'm_norm_ffn_g', 'm_w_ff1', 'm_w_ff2', 'm_norm_final_g', 'v_norm_mix_g', 'v_w_in', 'v_gm_v_norm_g', 'v_w_spatial', 'v_b_spatial', 'v_head_norm_g', 'v_w_out', 'v_norm_cross_g', 'v_norm_mem_g', 'v_w_cq', 'v_w_ckv', 'v_w_co', 'v_norm_ffn_g', 'v_w_ff1', 'v_w_ff2', 'v_norm_final_g']
TWIN_OUTPUTS = ['loss', 'grad_x', 'grad_norm_mix_g', 'grad_w_in', 'grad_gm_v_norm_g', 'grad_w_spatial', 'grad_b_spatial', 'grad_head_norm_g', 'grad_w_out', 'grad_norm_cross_g', 'grad_norm_mem_g', 'grad_w_cq', 'grad_w_ckv', 'grad_w_co', 'grad_norm_ffn_g', 'grad_w_ff1', 'grad_w_ff2', 'grad_norm_final_g', 'delta_norm_mix_g', 'delta_w_in', 'delta_gm_v_norm_g', 'delta_w_spatial', 'delta_b_spatial', 'delta_head_norm_g', 'delta_w_out', 'delta_norm_cross_g', 'delta_norm_mem_g', 'delta_w_cq', 'delta_w_ckv', 'delta_w_co', 'delta_norm_ffn_g', 'delta_w_ff1', 'delta_w_ff2', 'delta_norm_final_g', 'new_m_norm_mix_g', 'new_m_w_in', 'new_m_gm_v_norm_g', 'new_m_w_spatial', 'new_m_b_spatial', 'new_m_head_norm_g', 'new_m_w_out', 'new_m_norm_cross_g', 'new_m_norm_mem_g', 'new_m_w_cq', 'new_m_w_ckv', 'new_m_w_co', 'new_m_norm_ffn_g', 'new_m_w_ff1', 'new_m_w_ff2', 'new_m_norm_final_g', 'new_v_norm_mix_g', 'new_v_w_in', 'new_v_gm_v_norm_g', 'new_v_w_spatial', 'new_v_b_spatial', 'new_v_head_norm_g', 'new_v_w_out', 'new_v_norm_cross_g', 'new_v_norm_mem_g', 'new_v_w_cq', 'new_v_w_ckv', 'new_v_w_co', 'new_v_norm_ffn_g', 'new_v_w_ff1', 'new_v_w_ff2', 'new_v_norm_final_g']
TWIN_LEAF_KINDS = {'loss': 'loss', 'grad_x': 'grad_x', 'grad_norm_mix_g': 'grad_w', 'grad_w_in': 'grad_w', 'grad_gm_v_norm_g': 'grad_w', 'grad_w_spatial': 'grad_w', 'grad_b_spatial': 'grad_w', 'grad_head_norm_g': 'grad_w', 'grad_w_out': 'grad_w', 'grad_norm_cross_g': 'grad_w', 'grad_norm_mem_g': 'grad_w', 'grad_w_cq': 'grad_w', 'grad_w_ckv': 'grad_w', 'grad_w_co': 'grad_w', 'grad_norm_ffn_g': 'grad_w', 'grad_w_ff1': 'grad_w', 'grad_w_ff2': 'grad_w', 'grad_norm_final_g': 'grad_w', 'delta_norm_mix_g': 'delta_w', 'delta_w_in': 'delta_w', 'delta_gm_v_norm_g': 'delta_w', 'delta_w_spatial': 'delta_w', 'delta_b_spatial': 'delta_w', 'delta_head_norm_g': 'delta_w', 'delta_w_out': 'delta_w', 'delta_norm_cross_g': 'delta_w', 'delta_norm_mem_g': 'delta_w', 'delta_w_cq': 'delta_w', 'delta_w_ckv': 'delta_w', 'delta_w_co': 'delta_w', 'delta_norm_ffn_g': 'delta_w', 'delta_w_ff1': 'delta_w', 'delta_w_ff2': 'delta_w', 'delta_norm_final_g': 'delta_w', 'new_m_norm_mix_g': 'new_m', 'new_m_w_in': 'new_m', 'new_m_gm_v_norm_g': 'new_m', 'new_m_w_spatial': 'new_m', 'new_m_b_spatial': 'new_m', 'new_m_head_norm_g': 'new_m', 'new_m_w_out': 'new_m', 'new_m_norm_cross_g': 'new_m', 'new_m_norm_mem_g': 'new_m', 'new_m_w_cq': 'new_m', 'new_m_w_ckv': 'new_m', 'new_m_w_co': 'new_m', 'new_m_norm_ffn_g': 'new_m', 'new_m_w_ff1': 'new_m', 'new_m_w_ff2': 'new_m', 'new_m_norm_final_g': 'new_m', 'new_v_norm_mix_g': 'new_v', 'new_v_w_in': 'new_v', 'new_v_gm_v_norm_g': 'new_v', 'new_v_w_spatial': 'new_v', 'new_v_b_spatial': 'new_v', 'new_v_head_norm_g': 'new_v', 'new_v_w_out': 'new_v', 'new_v_norm_cross_g': 'new_v', 'new_v_norm_mem_g': 'new_v', 'new_v_w_cq': 'new_v', 'new_v_w_ckv': 'new_v', 'new_v_w_co': 'new_v', 'new_v_norm_ffn_g': 'new_v', 'new_v_w_ff1': 'new_v', 'new_v_w_ff2': 'new_v', 'new_v_norm_final_g': 'new_v'}


def _forward(args):
    return _fwd_reference(*[args[k] for k in FWD_PARAMS])


def _output_shape():
    out = _jax.eval_shape(lambda: _forward(_fwd_setup_inputs(0)))
    return out.shape, out.dtype

N_MICROBATCH = 1
ADAM_LR = 0.001
ADAM_B1 = 0.9
ADAM_B2 = 0.999
ADAM_EPS = 1e-08
ADAM_WD = 0.01
ADAM_STEP = 10
PER_EXAMPLE_BATCH_AXIS = {'x': 0, 'mem': 0, 'loss_target': 0}
SHARED_INPUTS = []
_WEIGHT_DTYPES = {'norm_mix_g': _jnp.float32, 'w_in': _jnp.float32, 'gm_v_norm_g': _jnp.float32, 'w_spatial': _jnp.float32, 'b_spatial': _jnp.float32, 'head_norm_g': _jnp.float32, 'w_out': _jnp.float32, 'norm_cross_g': _jnp.float32, 'norm_mem_g': _jnp.float32, 'w_cq': _jnp.float32, 'w_ckv': _jnp.float32, 'w_co': _jnp.float32, 'norm_ffn_g': _jnp.float32, 'w_ff1': _jnp.float32, 'w_ff2': _jnp.float32, 'norm_final_g': _jnp.float32}
MOMENT_SCALE = {'norm_mix_g': 2.274243e-01, 'w_in': 1.388109e-01, 'gm_v_norm_g': 6.817023e-02, 'w_spatial': 1.213475e-01, 'b_spatial': 6.509032e-02, 'head_norm_g': 1.895012e-01, 'w_out': 1.973228e-01, 'norm_cross_g': 2.071194e-02, 'norm_mem_g': 3.033148e-02, 'w_cq': 1.972453e-02, 'w_ckv': 2.035658e-02, 'w_co': 2.087570e-02, 'norm_ffn_g': 2.126073e-01, 'w_ff1': 9.566438e-02, 'w_ff2': 1.917267e-01, 'norm_final_g': 6.459111e+01}


def _to_microbatches(a, axis):
    t = _jnp.moveaxis(a, axis, 0)
    t = t.reshape((N_MICROBATCH, t.shape[0] // N_MICROBATCH) + t.shape[1:])
    return _jnp.moveaxis(t, 1, axis + 1)


def setup_inputs(seed: int = 0) -> dict:
    inp = _fwd_setup_inputs(seed)
    key = _jax.random.fold_in(_jax.random.key(seed), 7919)
    shape, _ = _output_shape()
    out = dict(inp)
    out["loss_target"] = _jax.random.normal(_jax.random.fold_in(key, 0), shape, _jnp.float32)
    for i, name in enumerate(TWIN_WEIGHTS):
        w = inp[name].astype(_jnp.float32)
        if MOMENT_SCALE is None:
            s = _jnp.sqrt(_jnp.mean(_jnp.square(w)) + 1e-30)
        else:
            s = MOMENT_SCALE[name]
        km, kv = _jax.random.split(_jax.random.fold_in(key, i + 1))
        out[name] = w
        out["m_" + name] = s * _jax.random.normal(km, w.shape, _jnp.float32)
        out["v_" + name] = (s * s) * _jax.random.uniform(kv, w.shape, _jnp.float32, 0.5, 1.5)
    if N_MICROBATCH > 1:
        for name, axis in PER_EXAMPLE_BATCH_AXIS.items():
            out[name] = _to_microbatches(out[name], axis)
    return {'x': out['x'], 'mem': out['mem'], 'norm_mix_g': out['norm_mix_g'], 'w_in': out['w_in'], 'gm_v_norm_g': out['gm_v_norm_g'], 'w_spatial': out['w_spatial'], 'b_spatial': out['b_spatial'], 'head_norm_g': out['head_norm_g'], 'w_out': out['w_out'], 'norm_cross_g': out['norm_cross_g'], 'norm_mem_g': out['norm_mem_g'], 'w_cq': out['w_cq'], 'w_ckv': out['w_ckv'], 'w_co': out['w_co'], 'norm_ffn_g': out['norm_ffn_g'], 'w_ff1': out['w_ff1'], 'w_ff2': out['w_ff2'], 'norm_final_g': out['norm_final_g'], 'loss_target': out['loss_target'], 'm_norm_mix_g': out['m_norm_mix_g'], 'm_w_in': out['m_w_in'], 'm_gm_v_norm_g': out['m_gm_v_norm_g'], 'm_w_spatial': out['m_w_spatial'], 'm_b_spatial': out['m_b_spatial'], 'm_head_norm_g': out['m_head_norm_g'], 'm_w_out': out['m_w_out'], 'm_norm_cross_g': out['m_norm_cross_g'], 'm_norm_mem_g': out['m_norm_mem_g'], 'm_w_cq': out['m_w_cq'], 'm_w_ckv': out['m_w_ckv'], 'm_w_co': out['m_w_co'], 'm_norm_ffn_g': out['m_norm_ffn_g'], 'm_w_ff1': out['m_w_ff1'], 'm_w_ff2': out['m_w_ff2'], 'm_norm_final_g': out['m_norm_final_g'], 'v_norm_mix_g': out['v_norm_mix_g'], 'v_w_in': out['v_w_in'], 'v_gm_v_norm_g': out['v_gm_v_norm_g'], 'v_w_spatial': out['v_w_spatial'], 'v_b_spatial': out['v_b_spatial'], 'v_head_norm_g': out['v_head_norm_g'], 'v_w_out': out['v_w_out'], 'v_norm_cross_g': out['v_norm_cross_g'], 'v_norm_mem_g': out['v_norm_mem_g'], 'v_w_cq': out['v_w_cq'], 'v_w_ckv': out['v_w_ckv'], 'v_w_co': out['v_w_co'], 'v_norm_ffn_g': out['v_norm_ffn_g'], 'v_w_ff1': out['v_w_ff1'], 'v_w_ff2': out['v_w_ff2'], 'v_norm_final_g': out['v_norm_final_g']}


def _loss(weights, diff, rest, loss_target):
    with _jax.named_scope("forward"):
        args = {**rest, TWIN_DIFF_INPUT: diff, **{k: w.astype(_WEIGHT_DTYPES[k]) for k, w in weights.items()}}
        y = _forward(args)
    with _jax.named_scope("loss_head"):
        err = _jnp.square(y.astype(_jnp.float32) - loss_target)
        return 0.5 * _jnp.sum(_jnp.mean(err, axis=-1)) if err.ndim else 0.5 * err


def _adamw(w, g, m, v):
    m = ADAM_B1 * m + (1.0 - ADAM_B1) * g
    v = ADAM_B2 * v + (1.0 - ADAM_B2) * _jnp.square(g)
    m_hat = m / (1.0 - ADAM_B1 ** ADAM_STEP)
    v_hat = v / (1.0 - ADAM_B2 ** ADAM_STEP)
    delta = -ADAM_LR * (m_hat / (_jnp.sqrt(v_hat) + ADAM_EPS) + ADAM_WD * w)
    return delta, m, v


def reference(x, mem, norm_mix_g, w_in, gm_v_norm_g, w_spatial, b_spatial, head_norm_g, w_out, norm_cross_g, norm_mem_g, w_cq, w_ckv, w_co, norm_ffn_g, w_ff1, w_ff2, norm_final_g, loss_target, m_norm_mix_g, m_w_in, m_gm_v_norm_g, m_w_spatial, m_b_spatial, m_head_norm_g, m_w_out, m_norm_cross_g, m_norm_mem_g, m_w_cq, m_w_ckv, m_w_co, m_norm_ffn_g, m_w_ff1, m_w_ff2, m_norm_final_g, v_norm_mix_g, v_w_in, v_gm_v_norm_g, v_w_spatial, v_b_spatial, v_head_norm_g, v_w_out, v_norm_cross_g, v_norm_mem_g, v_w_cq, v_w_ckv, v_w_co, v_norm_ffn_g, v_w_ff1, v_w_ff2, v_norm_final_g):
    given = dict(x=x, mem=mem, norm_mix_g=norm_mix_g, w_in=w_in, gm_v_norm_g=gm_v_norm_g, w_spatial=w_spatial, b_spatial=b_spatial, head_norm_g=head_norm_g, w_out=w_out, norm_cross_g=norm_cross_g, norm_mem_g=norm_mem_g, w_cq=w_cq, w_ckv=w_ckv, w_co=w_co, norm_ffn_g=norm_ffn_g, w_ff1=w_ff1, w_ff2=w_ff2, norm_final_g=norm_final_g, loss_target=loss_target, m_norm_mix_g=m_norm_mix_g, m_w_in=m_w_in, m_gm_v_norm_g=m_gm_v_norm_g, m_w_spatial=m_w_spatial, m_b_spatial=m_b_spatial, m_head_norm_g=m_head_norm_g, m_w_out=m_w_out, m_norm_cross_g=m_norm_cross_g, m_norm_mem_g=m_norm_mem_g, m_w_cq=m_w_cq, m_w_ckv=m_w_ckv, m_w_co=m_w_co, m_norm_ffn_g=m_norm_ffn_g, m_w_ff1=m_w_ff1, m_w_ff2=m_w_ff2, m_norm_final_g=m_norm_final_g, v_norm_mix_g=v_norm_mix_g, v_w_in=v_w_in, v_gm_v_norm_g=v_gm_v_norm_g, v_w_spatial=v_w_spatial, v_b_spatial=v_b_spatial, v_head_norm_g=v_head_norm_g, v_w_out=v_w_out, v_norm_cross_g=v_norm_cross_g, v_norm_mem_g=v_norm_mem_g, v_w_cq=v_w_cq, v_w_ckv=v_w_ckv, v_w_co=v_w_co, v_norm_ffn_g=v_norm_ffn_g, v_w_ff1=v_w_ff1, v_w_ff2=v_w_ff2, v_norm_final_g=v_norm_final_g)
    weights = {n: given[n] for n in TWIN_WEIGHTS}
    shared = {n: given[n] for n in SHARED_INPUTS}
    per_example = {n: given[n] for n in ['x', 'mem']}
    grad_fn = _jax.value_and_grad(_loss, argnums=(0, 1))

    def one_microbatch(ex, loss_target):
        ex = dict(ex)
        diff = ex.pop(TWIN_DIFF_INPUT)
        return grad_fn(weights, diff, {**shared, **ex}, loss_target)

    if N_MICROBATCH == 1:
        loss, (grad_w, grad_x) = one_microbatch(per_example, given["loss_target"])
    else:
        def body(carry, xs):
            loss_sum, grad_sum = carry
            l_k, (gw_k, gx_k) = one_microbatch(xs[0], xs[1])
            with _jax.named_scope("update"):
                return (loss_sum + l_k, _jax.tree.map(_jnp.add, grad_sum, gw_k)), gx_k

        init = (_jnp.zeros((), _jnp.float32), _jax.tree.map(_jnp.zeros_like, weights))
        (loss, grad_w), grad_x = _jax.lax.scan(body, init, (per_example, given["loss_target"]))
    with _jax.named_scope("update"):
        delta_w, new_m, new_v = {}, {}, {}
        for n in TWIN_WEIGHTS:
            delta_w[n], new_m[n], new_v[n] = _adamw(weights[n], grad_w[n], given["m_" + n], given["v_" + n])
    return (loss, grad_x, *[grad_w[n] for n in TWIN_WEIGHTS], *[delta_w[n] for n in TWIN_WEIGHTS],
            *[new_m[n] for n in TWIN_WEIGHTS], *[new_v[n] for n in TWIN_WEIGHTS])
```

```python
import functools
import math

import jax
import jax.numpy as jnp
from jax import lax
from jax.experimental import pallas as pl
from jax.experimental.pallas import tpu as pltpu

F32 = jnp.float32
BF = jnp.bfloat16

EPS = 1e-6
D_MODEL = 1024
CHUNK = 128
GM_GROUPS = 4
GM_WIDTH = 512
SB_WIDTH = 512
HEAD_LANES = 64
SB_SCALE = 0.125
X_HEADS = 4
X_HEAD_DIM = 256
N_MEM = 256
D_FF = 4096
IN_COLS = 2560
N_CHIPS = 4
N_DEV = 8

ADAM_LR = 0.001
ADAM_B1 = 0.9
ADAM_B2 = 0.999
ADAM_EPS = 1e-08
ADAM_WD = 0.01
ADAM_STEP = 10

V7X_VMEM_BYTES = 64 * 1024 * 1024
MESH = pl.DeviceIdType.MESH
ANY = pl.BlockSpec(memory_space=pl.ANY)

GELU_C = math.sqrt(2.0 / math.pi)
GELU_A = 0.044715


def _params(vmem_mb, sem=None):
    assert vmem_mb * 1024 * 1024 <= V7X_VMEM_BYTES
    return pltpu.CompilerParams(vmem_limit_bytes=vmem_mb * 1024 * 1024, dimension_semantics=sem)


def _dot(a, b):
    return jnp.dot(a, b, preferred_element_type=F32)


def _dot_bt(a, b):
    return lax.dot_general(a, b, (((1,), (1,)), ((), ())), preferred_element_type=F32)


def _dot_at(a, b):
    return lax.dot_general(a, b, (((0,), (0,)), ((), ())), preferred_element_type=F32)


def _gelu(x):
    t = jnp.tanh(GELU_C * (x + GELU_A * x * x * x))
    return 0.5 * x * (1.0 + t)


def _gelu_grad(x):
    x2 = x * x
    t = jnp.tanh(GELU_C * (x + GELU_A * x2 * x))
    return 0.5 * (1.0 + t) + 0.5 * x * (1.0 - t * t) * (GELU_C * (1.0 + 3.0 * GELU_A * x2))


def _rs(x):
    return lax.rsqrt(jnp.mean(x * x, axis=-1, keepdims=True) + EPS)


def _rms_bwd(dxn, xhat, r, g):
    dxh = dxn * g
    dx = r * (dxh - xhat * jnp.mean(dxh * xhat, axis=-1, keepdims=True))
    return dx, dxn * xhat


def _norm_matmul(x, g, w, tm, name):
    t, d = x.shape
    n = w.shape[1]
    tm = min(tm, t)

    def body(x_ref, g_ref, w_ref, out_ref, xn_ref):
        xv = x_ref[...]
        xn = (xv * _rs(xv) * g_ref[...]).astype(BF)
        xn_ref[...] = xn
        out_ref[...] = _dot(xn, w_ref[...]).astype(out_ref.dtype)

    return pl.pallas_call(
        body, name=name, grid=(t // tm,),
        in_specs=[pl.BlockSpec((tm, d), lambda i: (i, 0)), pl.BlockSpec((1, d), lambda i: (0, 0)),
                  pl.BlockSpec((d, n), lambda i: (0, 0))],
        out_specs=[pl.BlockSpec((tm, n), lambda i: (i, 0)), pl.BlockSpec((tm, d), lambda i: (i, 0))],
        out_shape=[jax.ShapeDtypeStruct((t, n), BF), jax.ShapeDtypeStruct((t, d), BF)],
        compiler_params=_params(48, ("arbitrary",)),
    )(x, g, w)


def _wgrad(a, g, tn, tk, name, square_a=False, col_shards=1):
    t, m = a.shape
    n = g.shape[1]
    tk = min(tk, t)
    tm = min(m, 1024)
    ns = n // col_shards
    assert ns % tn == 0 and m % tm == 0
    per = ns // tn
    nk = t // tk

    def body(a_ref, g_ref, o_ref):
        k = pl.program_id(2)

        @pl.when(k == 0)
        def _():
            o_ref[...] = jnp.zeros_like(o_ref)

        av = a_ref[...]
        if square_a:
            af = av.astype(F32)
            av = af * af
        o_ref[...] += _dot_at(av.astype(BF), g_ref[...].astype(BF))

    return pl.pallas_call(
        body, name=name, grid=(m // tm, n // tn, nk),
        in_specs=[pl.BlockSpec((tk, tm), lambda i, j, k: (k, i)), pl.BlockSpec((tk, tn), lambda i, j, k: (k, j))],
        out_specs=pl.BlockSpec((None, tm, tn), lambda i, j, k: (j // per, i, j % per)),
        out_shape=jax.ShapeDtypeStruct((col_shards, m, ns), F32),
        compiler_params=_params(48, ("arbitrary", "arbitrary", "arbitrary")),
    )(a, g)


def _matmul_bt(a, w, tm, name):
    t, n = a.shape
    k = w.shape[0]
    tm = min(tm, t)

    def body(a_ref, w_ref, o_ref):
        o_ref[...] = _dot_bt(a_ref[...].astype(BF), w_ref[...]).astype(o_ref.dtype)

    return pl.pallas_call(
        body, name=name, grid=(t // tm,),
        in_specs=[pl.BlockSpec((tm, n), lambda i: (i, 0)), pl.BlockSpec((k, n), lambda i: (0, 0))],
        out_specs=pl.BlockSpec((tm, k), lambda i: (i, 0)),
        out_shape=jax.ShapeDtypeStruct((t, k), BF),
        compiler_params=_params(32, ("arbitrary",)),
    )(a, w)


def _gmlp_fwd(proj, gg, wt, bb, hg, tm):
    t = proj.shape[0]
    tm = min(tm, t)

    def body(u_ref, v_ref, gg_ref, wt_ref, bb_ref, hg_ref, out_ref):
        for cc in range(tm // CHUNK):
            rows = slice(cc * CHUNK, (cc + 1) * CHUNK)
            for g in range(GM_GROUPS):
                cols = slice(g * 128, (g + 1) * 128)
                u = _gelu(u_ref[rows, cols].astype(F32))
                gv = _gelu(v_ref[rows, cols].astype(F32))
                vn = gv * _rs(gv) * gg_ref[:, cols]
                mixed = _dot(wt_ref[g], vn.astype(BF)) + bb_ref[g]
                a = u * mixed
                out_ref[rows, cols] = (a * _rs(a) * hg_ref[:, cols]).astype(BF)

    return pl.pallas_call(
        body, name="gmlp_fwd", grid=(t // tm,),
        in_specs=[pl.BlockSpec((tm, 512), lambda i: (i, 0)), pl.BlockSpec((tm, 512), lambda i: (i, 1)),
                  pl.BlockSpec((1, 512), lambda i: (0, 0)), pl.BlockSpec((4, 128, 128), lambda i: (0, 0, 0)),
                  pl.BlockSpec((4, 128, 128), lambda i: (0, 0, 0)), pl.BlockSpec((1, 512), lambda i: (0, 0))],
        out_specs=pl.BlockSpec((tm, 512), lambda i: (i, 0)),
        out_shape=jax.ShapeDtypeStruct((t, 512), BF),
        compiler_params=_params(32, ("arbitrary",)),
    )(proj, proj, gg, wt, bb, hg)


def _gmlp_bwd(proj, dmerged, gg, wt, wtt, bb, hg, tm):
    t = proj.shape[0]
    tm = min(tm, t)
    nsteps = t // tm

    def body(u_ref, v_ref, dm_ref, gg_ref, wt_ref, wtt_ref, bb_ref, hg_ref,
             dp_ref, dw_ref, db_ref, dgg_ref, dhg_ref):
        i = pl.program_id(0)

        @pl.when(i == 0)
        def _():
            dw_ref[...] = jnp.zeros_like(dw_ref)
            db_ref[...] = jnp.zeros_like(db_ref)
            dgg_ref[...] = jnp.zeros_like(dgg_ref)
            dhg_ref[...] = jnp.zeros_like(dhg_ref)

        for cc in range(tm // CHUNK):
            rows = slice(cc * CHUNK, (cc + 1) * CHUNK)
            for g in range(GM_GROUPS):
                cols = slice(g * 128, (g + 1) * 128)
                up = u_ref[rows, cols].astype(F32)
                gp = v_ref[rows, cols].astype(F32)
                u = _gelu(up)
                gv = _gelu(gp)
                rv = _rs(gv)
                gvh = gv * rv
                ggv = gg_ref[:, cols]
                vnb = (gvh * ggv).astype(BF)
                mixed = _dot(wt_ref[g], vnb) + bb_ref[g]
                a = u * mixed
                ra = _rs(a)
                ah = a * ra
                dm = dm_ref[rows, cols].astype(F32)
                dhg_ref[:, cols] += jnp.sum(dm * ah, axis=0, keepdims=True)
                dah = dm * hg_ref[:, cols]
                da = ra * (dah - ah * jnp.mean(dah * ah, axis=-1, keepdims=True))
                du = da * mixed
                dmix = da * u
                db_ref[g] += dmix
                dmb = dmix.astype(BF)
                dw_ref[g] += _dot_bt(dmb, vnb)
                dvn = _dot(wtt_ref[g], dmb)
                dgg_ref[:, cols] += jnp.sum(dvn * gvh, axis=0, keepdims=True)
                dgh = dvn * ggv
                dgv = rv * (dgh - gvh * jnp.mean(dgh * gvh, axis=-1, keepdims=True))
                dp_ref[rows, cols] = (du * _gelu_grad(up)).astype(BF)
                dp_ref[rows, 512 + g * 128:512 + (g + 1) * 128] = (dgv * _gelu_grad(gp)).astype(BF)

        @pl.when(i == nsteps - 1)
        def _():
            r = lax.broadcasted_iota(jnp.int32, (CHUNK, CHUNK), 0)
            c = lax.broadcasted_iota(jnp.int32, (CHUNK, CHUNK), 1)
            for g in range(GM_GROUPS):
                dw_ref[g] = jnp.where(c <= r, dw_ref[g], 0.0)
                db_ref[g] = jnp.broadcast_to(jnp.sum(db_ref[g], axis=-1, keepdims=True), (CHUNK, CHUNK))

    small = lambda shape: pl.BlockSpec(shape, lambda i: (0,) * len(shape))
    return pl.pallas_call(
        body, name="gmlp_bwd", grid=(nsteps,),
        in_specs=[pl.BlockSpec((tm, 512), lambda i: (i, 0)), pl.BlockSpec((tm, 512), lambda i: (i, 1)),
                  pl.BlockSpec((tm, 512), lambda i: (i, 0)), small((1, 512)), small((4, 128, 128)),
                  small((4, 128, 128)), small((4, 128, 128)), small((1, 512))],
        out_specs=[pl.BlockSpec((tm, 1024), lambda i: (i, 0)), small((4, 128, 128)), small((4, 128, 128)),
                   small((1, 512)), small((1, 512))],
        out_shape=[jax.ShapeDtypeStruct((t, 1024), BF), jax.ShapeDtypeStruct((4, 128, 128), F32),
                   jax.ShapeDtypeStruct((4, 128, 128), F32), jax.ShapeDtypeStruct((1, 512), F32),
                   jax.ShapeDtypeStruct((1, 512), F32)],
        compiler_params=_params(32, ("arbitrary",)),
    )(proj, proj, dmerged, gg, wt, wtt, bb, hg)


def _log_sig_pair(z):
    sp = jnp.log(1.0 + jnp.exp(-jnp.abs(z)))
    return -(jnp.maximum(z, 0.0) + sp), jnp.minimum(z, 0.0) - sp


def _hi_lo(m):
    hi = m.astype(BF)
    return hi, (m - hi.astype(F32)).astype(BF)


def _head_sums(x, h0):
    s0 = jnp.sum(jnp.where(h0, x, 0.0), axis=-1, keepdims=True)
    s1 = jnp.sum(jnp.where(h0, 0.0, x), axis=-1, keepdims=True)
    return jnp.where(h0, s0, s1)


def _sb_fwd(proj, hg, nb, s, tq):
    t = nb * s
    tq = min(tq, s)
    nq = s // tq

    def body(q_ref, k_ref, v_ref, hg_ref, o_ref, tot_ref, mb_ref, acc0, acc1, c0, c1):
        i = pl.program_id(2)
        lane = lax.broadcasted_iota(jnp.int32, (tq, 128), 1)
        h0 = lane < HEAD_LANES
        qs = q_ref[...] * SB_SCALE
        zero = jnp.zeros_like(qs)
        q0 = jnp.where(h0, qs, zero)
        q1 = jnp.where(h0, zero, qs)
        r = lax.broadcasted_iota(jnp.int32, (tq, tq), 0)
        c = lax.broadcasted_iota(jnp.int32, (tq, tq), 1)
        tri_gt = (r > c).astype(BF)
        causal = c < r
        for ref in (acc0, acc1, c0, c1):
            ref[...] = jnp.zeros_like(ref)

        def block(j, masked):
            start = pl.multiple_of(j * tq, tq)
            kj = k_ref[pl.ds(start, tq), :]
            vj = v_ref[pl.ds(start, tq), :]
            for qh, acc, cr in ((q0, acc0, c0), (q1, acc1, c1)):
                z = _dot_bt(qh, kj)
                m, l = _log_sig_pair(z)
                if masked:
                    m = jnp.where(causal, m, 0.0)
                mh, ml = _hi_lo(m)
                w = _dot(mh, tri_gt) + _dot(ml, tri_gt)
                a = jnp.exp(l + w + cr[...])
                if masked:
                    a = jnp.where(causal, a, 0.0)
                acc[...] += _dot(a.astype(BF), vj)
                cr[...] += jnp.sum(m, axis=-1, keepdims=True)

        block(i, True)

        def step(jj, carry):
            block(i - 1 - jj, False)
            return carry

        lax.fori_loop(0, i, step, 0)

        o = jnp.where(h0, acc0[...], acc1[...])
        o_ref[...] = o
        tot_ref[...] = jnp.where(h0, c0[...], c1[...])
        ro = lax.rsqrt(_head_sums(o * o, h0) * (1.0 / HEAD_LANES) + EPS)
        mb_ref[...] = (o * ro * hg_ref[...]).astype(BF)

    blk = lambda col0: pl.BlockSpec((tq, 128), lambda b, hp, i: (b * nq + i, col0 + hp))
    seq = lambda col0: pl.BlockSpec((s, 128), lambda b, hp, i: (b, col0 + hp))
    return pl.pallas_call(
        body, name="sb_fwd", grid=(nb, 4, nq),
        in_specs=[blk(8), seq(12), seq(16), pl.BlockSpec((1, 128), lambda b, hp, i: (0, 4 + hp))],
        out_specs=[blk(0), blk(0), blk(0)],
        out_shape=[jax.ShapeDtypeStruct((t, 512), F32), jax.ShapeDtypeStruct((t, 512), F32),
                   jax.ShapeDtypeStruct((t, 512), BF)],
        scratch_shapes=[pltpu.VMEM((tq, 128), F32), pltpu.VMEM((tq, 128), F32),
                        pltpu.VMEM((tq, 1), F32), pltpu.VMEM((tq, 1), F32)],
        compiler_params=_params(32, ("arbitrary", "arbitrary", "arbitrary")),
    )(proj, proj, proj, hg)


def _sb_bwd(proj, o_sb, tot, dmerged, hg, nb, s, tq):
    t = nb * s
    tq = min(tq, s)
    nq = s // tq

    def body(q_ref, k_ref, v_ref, o_ref, tot_ref, dm_ref, hg_ref,
             dq_ref, dk_ref, dv_ref, dhg_ref, dk_acc, dv_acc, dq0, dq1, cm0, cm1, cg0, cg1):
        i = pl.program_id(2)
        lane = lax.broadcasted_iota(jnp.int32, (tq, 128), 1)
        h0 = lane < HEAD_LANES
        qs = q_ref[...] * SB_SCALE
        zero = jnp.zeros_like(qs)
        q0 = jnp.where(h0, qs, zero)
        q1 = jnp.where(h0, zero, qs)
        qstack = jnp.concatenate([q0, q1], axis=0)
        r = lax.broadcasted_iota(jnp.int32, (tq, tq), 0)
        c = lax.broadcasted_iota(jnp.int32, (tq, tq), 1)
        tri_le = (r <= c).astype(BF)
        tri_lt = (r < c).astype(BF)
        causal = c < r

        @pl.when(i == 0)
        def _():
            dk_acc[...] = jnp.zeros_like(dk_acc)
            dv_acc[...] = jnp.zeros_like(dv_acc)
            dhg_ref[...] = jnp.zeros_like(dhg_ref)

        for ref in (dq0, dq1, cm0, cm1, cg0, cg1):
            ref[...] = jnp.zeros_like(ref)

        o = o_ref[...]
        ro = lax.rsqrt(_head_sums(o * o, h0) * (1.0 / HEAD_LANES) + EPS)
        oh = o * ro
        dm = dm_ref[...].astype(F32)
        dhg_ref[...] += jnp.sum(dm * oh, axis=0, keepdims=True)
        doh = dm * hg_ref[...]
        do = (ro * (doh - oh * (_head_sums(doh * oh, h0) * (1.0 / HEAD_LANES)))).astype(BF)
        zb = jnp.zeros_like(do)
        do0 = jnp.where(h0, do, zb)
        do1 = jnp.where(h0, zb, do)
        dostack = jnp.concatenate([do0, do1], axis=0)
        tots = (tot_ref[:, 0:1], tot_ref[:, HEAD_LANES:HEAD_LANES + 1])

        def block(j, masked):
            start = pl.multiple_of(j * tq, tq)
            kj = k_ref[pl.ds(start, tq), :]
            vj = v_ref[pl.ds(start, tq), :]
            dzs, abs_ = [], []
            for qh, doh_, dqr, cm, cg, th in ((q0, do0, dq0, cm0, cg0, tots[0]), (q1, do1, dq1, cm1, cg1, tots[1])):
                z = _dot_bt(qh, kj)
                m, l = _log_sig_pair(z)
                if masked:
                    m = jnp.where(causal, m, 0.0)
                mh, ml = _hi_lo(m)
                cum = _dot(mh, tri_le) + _dot(ml, tri_le)
                a = jnp.exp(l + (th - cm[...] - cum))
                if masked:
                    a = jnp.where(causal, a, 0.0)
                da = _dot_bt(doh_, vj)
                gm = a * da
                pp = cg[...] + _dot(gm.astype(BF), tri_lt)
                dz = gm - jnp.exp(l) * (gm + pp)
                if masked:
                    dz = jnp.where(causal, dz, 0.0)
                dzb = dz.astype(BF)
                dqr[...] += _dot(dzb, kj)
                dzs.append(dzb)
                abs_.append(a.astype(BF))
                cm[...] += jnp.sum(m, axis=-1, keepdims=True)
                cg[...] += jnp.sum(gm, axis=-1, keepdims=True)
            dk_acc[pl.ds(start, tq), :] += _dot_at(jnp.concatenate(dzs, axis=0), qstack)
            dv_acc[pl.ds(start, tq), :] += _dot_at(jnp.concatenate(abs_, axis=0), dostack)

        def step(j, carry):
            block(j, False)
            return carry

        lax.fori_loop(0, i, step, 0)
        block(i, True)

        dq_ref[...] = (jnp.where(h0, dq0[...], dq1[...]) * SB_SCALE).astype(BF)

        @pl.when(i == nq - 1)
        def _():
            dk_ref[...] = dk_acc[...].astype(BF)
            dv_ref[...] = dv_acc[...].astype(BF)

    blk = lambda col0: pl.BlockSpec((tq, 128), lambda b, hp, i: (b * nq + i, col0 + hp))
    seq = lambda col0: pl.BlockSpec((s, 128), lambda b, hp, i: (b, col0 + hp))
    return pl.pallas_call(
        body, name="sb_bwd", grid=(nb, 4, nq),
        in_specs=[blk(8), seq(12), seq(16), blk(0), blk(0), blk(4),
                  pl.BlockSpec((1, 128), lambda b, hp, i: (0, 4 + hp))],
        out_specs=[blk(0), seq(0), seq(0), pl.BlockSpec((None, 1, 128), lambda b, hp, i: (b, 0, hp))],
        out_shape=[jax.ShapeDtypeStruct((t, 512), BF), jax.ShapeDtypeStruct((t, 512), BF),
                   jax.ShapeDtypeStruct((t, 512), BF), jax.ShapeDtypeStruct((nb, 1, 512), F32)],
        scratch_shapes=[pltpu.VMEM((s, 128), F32), pltpu.VMEM((s, 128), F32),
                        pltpu.VMEM((tq, 128), F32), pltpu.VMEM((tq, 128), F32),
                        pltpu.VMEM((tq, 1), F32), pltpu.VMEM((tq, 1), F32),
                        pltpu.VMEM((tq, 1), F32), pltpu.VMEM((tq, 1), F32)],
        compiler_params=_params(40, ("arbitrary", "arbitrary", "arbitrary")),
    )(proj, proj, proj, o_sb, tot, dmerged, hg)


def _softmax_rows(sc):
    e = jnp.exp(sc - jnp.max(sc, axis=-1, keepdims=True))
    return e / jnp.sum(e, axis=-1, keepdims=True)


def _mix_cross_fwd(x, ma, mb, w_out, gc, w_cq, kv, w_co, s, tm):
    t, d = x.shape
    tm = min(tm, s)
    per = s // tm
    inv = 1.0 / math.sqrt(X_HEAD_DIM)

    def body(x_ref, ma_ref, mb_ref, wo_ref, gc_ref, wq_ref, kv_ref, wc_ref, h1_ref, h2_ref, hn_ref, oc_ref):
        h1 = x_ref[...] + _dot(ma_ref[...], wo_ref[0:512, :]) + _dot(mb_ref[...], wo_ref[512:1024, :])
        h1_ref[...] = h1
        hn = (h1 * _rs(h1) * gc_ref[...]).astype(BF)
        hn_ref[...] = hn
        qc = _dot(hn, wq_ref[...]).astype(BF)
        for h in range(X_HEADS):
            cols = slice(h * X_HEAD_DIM, (h + 1) * X_HEAD_DIM)
            kh = kv_ref[:, h * X_HEAD_DIM:(h + 1) * X_HEAD_DIM]
            vh = kv_ref[:, d + h * X_HEAD_DIM:d + (h + 1) * X_HEAD_DIM]
            p = _softmax_rows(_dot_bt(qc[:, cols], kh) * inv)
            oc_ref[:, cols] = _dot(p.astype(BF), vh).astype(BF)
        h2_ref[...] = h1 + _dot(oc_ref[...], wc_ref[...])

    row = lambda width: pl.BlockSpec((tm, width), lambda i: (i, 0))
    full = lambda a, b: pl.BlockSpec((a, b), lambda i: (0, 0))
    return pl.pallas_call(
        body, name="mix_cross_fwd", grid=(t // tm,),
        in_specs=[row(d), row(512), row(512), full(d, d), full(1, d), full(d, d),
                  pl.BlockSpec((N_MEM, 2 * d), lambda i: (i // per, 0)), full(d, d)],
        out_specs=[row(d), row(d), row(d), row(d)],
        out_shape=[jax.ShapeDtypeStruct((t, d), F32), jax.ShapeDtypeStruct((t, d), F32),
                   jax.ShapeDtypeStruct((t, d), BF), jax.ShapeDtypeStruct((t, d), BF)],
        compiler_params=_params(48, ("arbitrary",)),
    )(x, ma, mb, w_out, gc, w_cq, kv, w_co)


def _cross_bwd(dh2, h1, gc, w_cq, kv, w_co, s, tm):
    t, d = dh2.shape
    tm = min(tm, s)
    per = s // tm
    nb = t // s
    inv = 1.0 / math.sqrt(X_HEAD_DIM)

    def body(dh2_ref, h1_ref, gc_ref, wq_ref, kv_ref, wc_ref, dh1_ref, dqc_ref, dkv_ref, dgc_ref):
        i = pl.program_id(0)

        @pl.when(i == 0)
        def _():
            dgc_ref[...] = jnp.zeros_like(dgc_ref)

        @pl.when(i % per == 0)
        def _():
            dkv_ref[...] = jnp.zeros_like(dkv_ref)

        dh2 = dh2_ref[...]
        h1 = h1_ref[...]
        r = _rs(h1)
        h1h = h1 * r
        gcv = gc_ref[...]
        hn = (h1h * gcv).astype(BF)
        qc = _dot(hn, wq_ref[...]).astype(BF)
        do = _dot_bt(dh2.astype(BF), wc_ref[...]).astype(BF)
        for h in range(X_HEADS):
            cols = slice(h * X_HEAD_DIM, (h + 1) * X_HEAD_DIM)
            vcols = slice(d + h * X_HEAD_DIM, d + (h + 1) * X_HEAD_DIM)
            kh = kv_ref[:, cols]
            vh = kv_ref[:, vcols]
            p = _softmax_rows(_dot_bt(qc[:, cols], kh) * inv)
            dp = _dot_bt(do[:, cols], vh)
            ds = (p * (dp - jnp.sum(dp * p, axis=-1, keepdims=True)) * inv).astype(BF)
            dqc_ref[:, cols] = _dot(ds, kh).astype(BF)
            dkv_ref[:, cols] += _dot_at(ds, qc[:, cols])
            dkv_ref[:, vcols] += _dot_at(p.astype(BF), do[:, cols])
        dhn = _dot_bt(dqc_ref[...], wq_ref[...])
        dx, dg = _rms_bwd(dhn, h1h, r, gcv)
        dh1_ref[...] = dh2 + dx
        dgc_ref[...] += jnp.sum(dg, axis=0, keepdims=True)

    row = lambda width: pl.BlockSpec((tm, width), lambda i: (i, 0))
    full = lambda a, b: pl.BlockSpec((a, b), lambda i: (0, 0))
    kvspec = pl.BlockSpec((N_MEM, 2 * d), lambda i: (i // per, 0))
    return pl.pallas_call(
        body, name="cross_bwd", grid=(t // tm,),
        in_specs=[row(d), row(d), full(1, d), full(d, d), kvspec, full(d, d)],
        out_specs=[row(d), row(d), kvspec, full(1, d)],
        out_shape=[jax.ShapeDtypeStruct((t, d), F32), jax.ShapeDtypeStruct((t, d), BF),
                   jax.ShapeDtypeStruct((nb * N_MEM, 2 * d), F32), jax.ShapeDtypeStruct((1, d), F32)],
        compiler_params=_params(48, ("arbitrary",)),
    )(dh2, h1, gc, w_cq, kv, w_co)


def _mem_bwd(mem, gm, dkv, w_ckv, tm):
    t, d = mem.shape
    tm = min(tm, t)

    def body(mem_ref, dkv_ref, w_ref, dg_ref):
        @pl.when(pl.program_id(0) == 0)
        def _():
            dg_ref[...] = jnp.zeros_like(dg_ref)

        mv = mem_ref[...]
        dmn = _dot_bt(dkv_ref[...].astype(BF), w_ref[...])
        dg_ref[...] += jnp.sum(dmn * (mv * _rs(mv)), axis=0, keepdims=True)

    del gm
    return pl.pallas_call(
        body, name="mem_bwd", grid=(t // tm,),
        in_specs=[pl.BlockSpec((tm, d), lambda i: (i, 0)), pl.BlockSpec((tm, 2 * d), lambda i: (i, 0)),
                  pl.BlockSpec((d, 2 * d), lambda i: (0, 0))],
        out_specs=pl.BlockSpec((1, d), lambda i: (0, 0)),
        out_shape=jax.ShapeDtypeStruct((1, d), F32),
        compiler_params=_params(32, ("arbitrary",)),
    )(mem, dkv, w_ckv)


def _ffn_loss_fwd(h2, gf, w1, w2, gl, target, tm):
    t, d = h2.shape
    tm = min(tm, t)

    def body(h2_ref, gf_ref, w1_ref, w2_ref, gl_ref, tg_ref, hn_ref, f_ref, dh3_ref, dgl_ref, loss_ref):
        @pl.when(pl.program_id(0) == 0)
        def _():
            dgl_ref[...] = jnp.zeros_like(dgl_ref)
            loss_ref[...] = jnp.zeros_like(loss_ref)

        h2 = h2_ref[...]
        hn = (h2 * _rs(h2) * gf_ref[...]).astype(BF)
        hn_ref[...] = hn
        h3 = h2
        for c in range(4):
            f = jnp.maximum(_dot(hn, w1_ref[c]), 0.0)
            f_ref[:, c * 1024:(c + 1) * 1024] = f.astype(BF)
            h3 = h3 + _dot((f * f).astype(BF), w2_ref[c])
        r3 = _rs(h3)
        yh = h3 * r3
        glv = gl_ref[...]
        e = yh * glv - tg_ref[...]
        loss_ref[...] += 0.5 * jnp.sum(jnp.sum(e * e, axis=-1, keepdims=True) * (1.0 / d), axis=0, keepdims=True)
        dy = e * (1.0 / d)
        dx, dg = _rms_bwd(dy, yh, r3, glv)
        dh3_ref[...] = dx
        dgl_ref[...] += jnp.sum(dg, axis=0, keepdims=True)

    row = lambda width: pl.BlockSpec((tm, width), lambda i: (i, 0))
    return pl.pallas_call(
        body, name="ffn_loss_fwd", grid=(t // tm,),
        in_specs=[row(d), pl.BlockSpec((1, d), lambda i: (0, 0)), pl.BlockSpec((4, d, 1024), lambda i: (0, 0, 0)),
                  pl.BlockSpec((4, 1024, d), lambda i: (0, 0, 0)), pl.BlockSpec((1, d), lambda i: (0, 0)), row(d)],
        out_specs=[row(d), row(D_FF), row(d), pl.BlockSpec((1, d), lambda i: (0, 0)),
                   pl.BlockSpec((1, 1), lambda i: (0, 0))],
        out_shape=[jax.ShapeDtypeStruct((t, d), BF), jax.ShapeDtypeStruct((t, D_FF), BF),
                   jax.ShapeDtypeStruct((t, d), F32), jax.ShapeDtypeStruct((1, d), F32),
                   jax.ShapeDtypeStruct((1, 1), F32)],
        compiler_params=_params(56, ("arbitrary",)),
    )(h2, gf, w1, w2, gl, target)


def _ffn_bwd(dh3, f, h2, gf, w1, w2, tm):
    t, d = h2.shape
    tm = min(tm, t)

    def body(dh3_ref, f_ref, h2_ref, gf_ref, w1_ref, w2_ref, dh2_ref, dpre_ref, dgf_ref):
        @pl.when(pl.program_id(0) == 0)
        def _():
            dgf_ref[...] = jnp.zeros_like(dgf_ref)

        dh3 = dh3_ref[...]
        dh3b = dh3.astype(BF)
        dhn = jnp.zeros((tm, d), F32)
        for c in range(4):
            cols = slice(c * 1024, (c + 1) * 1024)
            dpre = (_dot_bt(dh3b, w2_ref[c]) * (2.0 * f_ref[:, cols].astype(F32))).astype(BF)
            dpre_ref[:, cols] = dpre
            dhn = dhn + _dot_bt(dpre, w1_ref[c])
        h2 = h2_ref[...]
        r = _rs(h2)
        dx, dg = _rms_bwd(dhn, h2 * r, r, gf_ref[...])
        dh2_ref[...] = dh3 + dx
        dgf_ref[...] += jnp.sum(dg, axis=0, keepdims=True)

    row = lambda width: pl.BlockSpec((tm, width), lambda i: (i, 0))
    return pl.pallas_call(
        body, name="ffn_bwd", grid=(t // tm,),
        in_specs=[row(d), row(D_FF), row(d), pl.BlockSpec((1, d), lambda i: (0, 0)),
                  pl.BlockSpec((4, d, 1024), lambda i: (0, 0, 0)), pl.BlockSpec((4, 1024, d), lambda i: (0, 0, 0))],
        out_specs=[row(d), row(D_FF), pl.BlockSpec((1, d), lambda i: (0, 0))],
        out_shape=[jax.ShapeDtypeStruct((t, d), F32), jax.ShapeDtypeStruct((t, D_FF), BF),
                   jax.ShapeDtypeStruct((1, d), F32)],
        compiler_params=_params(56, ("arbitrary",)),
    )(dh3, f, h2, gf, w1, w2)


def _in_bwd(dproj, dh1, x, g, w_in, tm):
    t, d = x.shape
    n = w_in.shape[1]
    tm = min(tm, t)

    def body(dp_ref, dh1_ref, x_ref, g_ref, w_ref, dx_ref, dg_ref):
        @pl.when(pl.program_id(0) == 0)
        def _():
            dg_ref[...] = jnp.zeros_like(dg_ref)

        dxn = _dot_bt(dp_ref[...], w_ref[...])
        xv = x_ref[...]
        r = _rs(xv)
        dx, dg = _rms_bwd(dxn, xv * r, r, g_ref[...])
        dx_ref[...] = dh1_ref[...] + dx
        dg_ref[...] += jnp.sum(dg, axis=0, keepdims=True)

    row = lambda width: pl.BlockSpec((tm, width), lambda i: (i, 0))
    return pl.pallas_call(
        body, name="in_bwd", grid=(t // tm,),
        in_specs=[row(n), row(d), row(d), pl.BlockSpec((1, d), lambda i: (0, 0)),
                  pl.BlockSpec((d, n), lambda i: (0, 0))],
        out_specs=[row(d), pl.BlockSpec((1, d), lambda i: (0, 0))],
        out_shape=[jax.ShapeDtypeStruct((t, d), F32), jax.ShapeDtypeStruct((1, d), F32)],
        compiler_params=_params(48, ("arbitrary",)),
    )(dproj, dh1, x, g, w_in)


def _local_step(x, mem, target, small, big, nb, s, tq=256):
    d = D_MODEL
    g_mix, g_v, w_sp, b_sp, g_head, g_cross, g_mem, g_ffn, g_fin = (
        small[k] for k in ("norm_mix_g", "gm_v_norm_g", "w_spatial", "b_spatial", "head_norm_g", "norm_cross_g",
                           "norm_mem_g", "norm_ffn_g", "norm_final_g"))
    tri = jnp.tril(jnp.ones((CHUNK, CHUNK), dtype=bool))
    w_sp_m = jnp.where(tri[None], w_sp, 0.0)
    wt = w_sp_m.astype(BF)
    wtt = jnp.swapaxes(w_sp_m, 1, 2).astype(BF)
    bb = jnp.broadcast_to(b_sp[:, :, None], (GM_GROUPS, CHUNK, CHUNK))
    hg_a = g_head[:, :GM_WIDTH]
    w1c = big["w_ff1"].reshape(d, 4, 1024).transpose(1, 0, 2)
    w2c = big["w_ff2"].reshape(4, 1024, d)

    proj, xn = _norm_matmul(x, g_mix, big["w_in"], 512, "in_proj")
    ma = _gmlp_fwd(proj, g_v, wt, bb, hg_a, 512)
    o_sb, tot, mb = _sb_fwd(proj, g_head, nb, s, tq)
    kv, memn = _norm_matmul(mem, g_mem, big["w_ckv"], 512, "mem_proj")
    h1, h2, hn, oc = _mix_cross_fwd(x, ma, mb, big["w_out"], g_cross, big["w_cq"], kv, big["w_co"], s, 512)
    hn2, f, dh3, d_fin, loss = _ffn_loss_fwd(h2, g_ffn, w1c, w2c, g_fin, target, 256)

    dh2, dpre, d_ffn = _ffn_bwd(dh3, f, h2, g_ffn, w1c, w2c, 256)
    gw_ff2 = _wgrad(f, dh3, 1024, 512, "wgrad_ff2", square_a=True)
    gw_ff1 = _wgrad(hn2, dpre, 1024, 512, "wgrad_ff1", col_shards=4)
    dh1, dqc, dkv, d_cross = _cross_bwd(dh2, h1, g_cross, big["w_cq"], kv, big["w_co"], s, 512)
    gw_co = _wgrad(oc, dh2, 1024, 512, "wgrad_co")
    gw_cq = _wgrad(hn, dqc, 1024, 512, "wgrad_cq")
    gw_ckv = _wgrad(memn, dkv, 512, 512, "wgrad_ckv", col_shards=4)
    d_mem = _mem_bwd(mem, g_mem, dkv, big["w_ckv"], 512)
    dmerged = _matmul_bt(dh1, big["w_out"], 512, "out_bwd")
    merged = jnp.concatenate([ma, mb], axis=1)
    gw_out = _wgrad(merged, dh1, 1024, 512, "wgrad_out")
    dugv, d_wsp, d_bb, d_gv, d_hga = _gmlp_bwd(proj, dmerged, g_v, wt, wtt, bb, hg_a, 512)
    dq, dk, dv, d_hgb = _sb_bwd(proj, o_sb, tot, dmerged, g_head, nb, s, tq)
    dproj = jnp.concatenate([dugv, dq, dk, dv], axis=1)
    grad_x, d_mix = _in_bwd(dproj, dh1, x, g_mix, big["w_in"], 512)
    gw_in = _wgrad(xn, dproj, 640, 512, "wgrad_in", col_shards=4)

    gsmall = {
        "norm_mix_g": d_mix, "gm_v_norm_g": d_gv, "w_spatial": d_wsp, "b_spatial": d_bb[:, :, 0],
        "head_norm_g": jnp.concatenate([d_hga, jnp.sum(d_hgb, axis=0)], axis=1), "norm_cross_g": d_cross,
        "norm_mem_g": d_mem, "norm_ffn_g": d_ffn, "norm_final_g": d_fin,
    }
    slab = lambda a: a.reshape(N_DEV, -1, a.shape[-1])
    gbig = {"w_in": slab(gw_in), "w_out": slab(gw_out), "w_cq": slab(gw_cq), "w_ckv": slab(gw_ckv),
            "w_co": slab(gw_co), "w_ff1": slab(gw_ff1), "w_ff2": slab(gw_ff2)}
    return loss, grad_x, gsmall, gbig


BIG = ("w_in", "w_out", "w_cq", "w_ckv", "w_co", "w_ff1", "w_ff2")
SMALL = ("norm_mix_g", "gm_v_norm_g", "w_spatial", "b_spatial", "head_norm_g", "norm_cross_g", "norm_mem_g",
         "norm_ffn_g", "norm_final_g")


def _other_chips(x, y):
    return ((1 - x, y), (x, 1 - y), (1 - x, 1 - y))


def _all_gather_weights(shards):
    n = len(shards)

    def body(*refs):
        ins, outs = refs[:n], refs[n:2 * n]
        send_sems, recv_sems, loc_sems = refs[2 * n:]
        x, y, c = lax.axis_index("x"), lax.axis_index("y"), lax.axis_index("c")
        q = 2 * x + y
        copies = []
        for w in range(n):
            lc = pltpu.make_async_copy(ins[w], outs[w].at[q], loc_sems.at[w])
            lc.start()
            copies.append(lc)
            for k, (px, py) in enumerate(_other_chips(x, y)):
                cp = pltpu.make_async_remote_copy(
                    src_ref=ins[w], dst_ref=outs[w].at[q], send_sem=send_sems.at[3 * w + k],
                    recv_sem=recv_sems.at[3 * w + k], device_id=(px, py, c), device_id_type=MESH)
                cp.start()
                copies.append(cp)
        for cp in copies:
            cp.wait()

    return pl.pallas_call(
        body, name="all_gather_weights",
        in_specs=[ANY] * n, out_specs=[ANY] * n,
        out_shape=[jax.ShapeDtypeStruct((N_CHIPS,) + a.shape, a.dtype) for a in shards],
        scratch_shapes=[pltpu.SemaphoreType.DMA((3 * n,)), pltpu.SemaphoreType.DMA((3 * n,)),
                        pltpu.SemaphoreType.DMA((n,))],
    )(*shards)


def _sibling_exchange(slabs):
    n = len(slabs)

    def body(*refs):
        ins, outs = refs[:n], refs[n:2 * n]
        send_sems, recv_sems = refs[2 * n:]
        x, y, c = lax.axis_index("x"), lax.axis_index("y"), lax.axis_index("c")
        copies = []
        for w in range(n):
            for p in range(N_CHIPS):
                cp = pltpu.make_async_remote_copy(
                    src_ref=ins[w].at[2 * p + (1 - c)], dst_ref=outs[w].at[p], send_sem=send_sems.at[4 * w + p],
                    recv_sem=recv_sems.at[4 * w + p], device_id=(x, y, 1 - c), device_id_type=MESH)
                cp.start()
                copies.append(cp)
        for cp in copies:
            cp.wait()

    return pl.pallas_call(
        body, name="grad_sibling_exchange",
        in_specs=[ANY] * n, out_specs=[ANY] * n,
        out_shape=[jax.ShapeDtypeStruct((N_CHIPS,) + a.shape[1:], a.dtype) for a in slabs],
        scratch_shapes=[pltpu.SemaphoreType.DMA((4 * n,)), pltpu.SemaphoreType.DMA((4 * n,))],
    )(*slabs)


def _chip_sum(slabs, recv, c_idx, name):
    _, r, cw = slabs.shape
    tr = min(r, 256)

    def body(c_ref, a_ref, b_ref, o_ref):
        del c_ref
        o_ref[...] = (a_ref[...] + b_ref[...]).astype(BF)

    return pl.pallas_call(
        body, name=name,
        grid_spec=pltpu.PrefetchScalarGridSpec(
            num_scalar_prefetch=1, grid=(N_CHIPS, r // tr),
            in_specs=[pl.BlockSpec((None, tr, cw), lambda p, i, c_ref: (2 * p + c_ref[0], i, 0)),
                      pl.BlockSpec((None, tr, cw), lambda p, i, c_ref: (p, i, 0))],
            out_specs=pl.BlockSpec((None, tr, cw), lambda p, i, c_ref: (p, i, 0))),
        out_shape=jax.ShapeDtypeStruct((N_CHIPS, r, cw), BF),
        compiler_params=_params(32, ("arbitrary", "arbitrary")),
    )(c_idx, slabs, recv)


def _chip_exchange(sums):
    n = len(sums)

    def body(*refs):
        ins, outs = refs[:n], refs[n:2 * n]
        send_sems, recv_sems, loc_sems = refs[2 * n:]
        x, y, c = lax.axis_index("x"), lax.axis_index("y"), lax.axis_index("c")
        q = 2 * x + y
        copies = []
        for w in range(n):
            lc = pltpu.make_async_copy(ins[w].at[q], outs[w].at[q], loc_sems.at[w])
            lc.start()
            copies.append(lc)
            for k, (px, py) in enumerate(_other_chips(x, y)):
                cp = pltpu.make_async_remote_copy(
                    src_ref=ins[w].at[2 * px + py], dst_ref=outs[w].at[q], send_sem=send_sems.at[3 * w + k],
                    recv_sem=recv_sems.at[3 * w + k], device_id=(px, py, c), device_id_type=MESH)
                cp.start()
                copies.append(cp)
        for cp in copies:
            cp.wait()

    return pl.pallas_call(
        body, name="grad_chip_exchange",
        in_specs=[ANY] * n, out_specs=[ANY] * n,
        out_shape=[jax.ShapeDtypeStruct(a.shape, a.dtype) for a in sums],
        scratch_shapes=[pltpu.SemaphoreType.DMA((3 * n,)), pltpu.SemaphoreType.DMA((3 * n,)),
                        pltpu.SemaphoreType.DMA((n,))],
    )(*sums)


def _sum4(parts, name):
    _, r, cw = parts.shape
    tr = min(r, 256)

    def body(a_ref, o_ref):
        o_ref[...] = ((a_ref[0].astype(F32) + a_ref[1].astype(F32)) + a_ref[2].astype(F32)) + a_ref[3].astype(F32)

    return pl.pallas_call(
        body, name=name, grid=(r // tr,),
        in_specs=[pl.BlockSpec((N_CHIPS, tr, cw), lambda i: (0, i, 0))],
        out_specs=pl.BlockSpec((tr, cw), lambda i: (i, 0)),
        out_shape=jax.ShapeDtypeStruct((r, cw), F32),
        compiler_params=_params(32, ("arbitrary",)),
    )(parts)


def _half_exchange(halves):
    n = len(halves)

    def body(*refs):
        ins, outs = refs[:n], refs[n:2 * n]
        send_sems, recv_sems, loc_sems = refs[2 * n:]
        x, y, c = lax.axis_index("x"), lax.axis_index("y"), lax.axis_index("c")
        copies = []
        for w in range(n):
            lc = pltpu.make_async_copy(ins[w], outs[w].at[c], loc_sems.at[w])
            lc.start()
            copies.append(lc)
            cp = pltpu.make_async_remote_copy(
                src_ref=ins[w], dst_ref=outs[w].at[c], send_sem=send_sems.at[w], recv_sem=recv_sems.at[w],
                device_id=(x, y, 1 - c), device_id_type=MESH)
            cp.start()
            copies.append(cp)
        for cp in copies:
            cp.wait()

    return pl.pallas_call(
        body, name="grad_half_exchange",
        in_specs=[ANY] * n, out_specs=[ANY] * n,
        out_shape=[jax.ShapeDtypeStruct((2,) + a.shape, a.dtype) for a in halves],
        scratch_shapes=[pltpu.SemaphoreType.DMA((n,)), pltpu.SemaphoreType.DMA((n,)), pltpu.SemaphoreType.DMA((n,))],
    )(*halves)


def _small_all_reduce(packed):
    rows = packed.shape[0]

    def body(in_ref, out_ref, buf, send_sems, recv_sems):
        x, y, c = lax.axis_index("x"), lax.axis_index("y"), lax.axis_index("c")
        me = 4 * x + 2 * y + c
        buf[me] = in_ref[...]
        copies = []
        for k in range(1, N_DEV):
            bx, by, bc = (k >> 2) & 1, (k >> 1) & 1, k & 1
            peer = (x ^ bx, y ^ by, c ^ bc)
            cp = pltpu.make_async_remote_copy(
                src_ref=in_ref, dst_ref=buf.at[me], send_sem=send_sems.at[k - 1], recv_sem=recv_sems.at[k - 1],
                device_id=peer, device_id_type=MESH)
            cp.start()
            copies.append(cp)
        for cp in copies:
            cp.wait()
        acc = buf[0]
        for dev in range(1, N_DEV):
            acc = acc + buf[dev]
        out_ref[...] = acc

    return pl.pallas_call(
        body, name="small_all_reduce",
        in_specs=[pl.BlockSpec(memory_space=pltpu.VMEM)], out_specs=pl.BlockSpec(memory_space=pltpu.VMEM),
        out_shape=jax.ShapeDtypeStruct(packed.shape, F32),
        scratch_shapes=[pltpu.VMEM((N_DEV, rows, 128), F32), pltpu.SemaphoreType.DMA((N_DEV - 1,)),
                        pltpu.SemaphoreType.DMA((N_DEV - 1,))],
        compiler_params=_params(16),
    )(packed)


def _adamw(g, w, m, v, name):
    r, cw = g.shape
    tr = r if r <= 1024 else 256
    assert r % tr == 0

    def body(g_ref, w_ref, m_ref, v_ref, d_ref, nm_ref, nv_ref):
        gv = g_ref[...]
        nm = ADAM_B1 * m_ref[...] + (1.0 - ADAM_B1) * gv
        nv = ADAM_B2 * v_ref[...] + (1.0 - ADAM_B2) * (gv * gv)
        m_hat = nm / (1.0 - ADAM_B1 ** ADAM_STEP)
        v_hat = nv / (1.0 - ADAM_B2 ** ADAM_STEP)
        d_ref[...] = -ADAM_LR * (m_hat / (jnp.sqrt(v_hat) + ADAM_EPS) + ADAM_WD * w_ref[...])
        nm_ref[...] = nm
        nv_ref[...] = nv

    spec = pl.BlockSpec((tr, cw), lambda i: (i, 0))
    return pl.pallas_call(
        body, name=name, grid=(r // tr,),
        in_specs=[spec] * 4, out_specs=[spec] * 3,
        out_shape=[jax.ShapeDtypeStruct((r, cw), F32)] * 3,
        compiler_params=_params(32, ("arbitrary",)),
    )(g, w, m, v)


def _small_params(args):
    small = {k: args[k].reshape(1, -1) for k in SMALL}
    small["w_spatial"] = args["w_spatial"][0]
    small["b_spatial"] = args["b_spatial"][0]
    return small


def _pack(parts, rows):
    flat = jnp.concatenate([p.reshape(-1).astype(F32) for p in parts])
    return jnp.pad(flat, (0, rows * 128 - flat.shape[0])).reshape(rows, 128)


def _unpack(packed, shapes):
    flat = packed.reshape(-1)
    out, off = [], 0
    for shp in shapes:
        size = math.prod(shp)
        out.append(flat[off:off + size].reshape(shp))
        off += size
    return out


def kernel(x, mem, norm_mix_g, w_in, gm_v_norm_g, w_spatial, b_spatial, head_norm_g, w_out, norm_cross_g, norm_mem_g, w_cq, w_ckv, w_co, norm_ffn_g, w_ff1, w_ff2, norm_final_g, loss_target, m_norm_mix_g, m_w_in, m_gm_v_norm_g, m_w_spatial, m_b_spatial, m_head_norm_g, m_w_out, m_norm_cross_g, m_norm_mem_g, m_w_cq, m_w_ckv, m_w_co, m_norm_ffn_g, m_w_ff1, m_w_ff2, m_norm_final_g, v_norm_mix_g, v_w_in, v_gm_v_norm_g, v_w_spatial, v_b_spatial, v_head_norm_g, v_w_out, v_norm_cross_g, v_norm_mem_g, v_w_cq, v_w_ckv, v_w_co, v_norm_ffn_g, v_w_ff1, v_w_ff2, v_norm_final_g):
    args = dict(locals())
    d = D_MODEL
    nb, s, _ = x.shape
    c_idx = lax.axis_index("c").astype(jnp.int32).reshape(1)

    shards = [args[k][0].astype(BF) for k in BIG]
    full = dict(zip(BIG, _all_gather_weights(shards)))
    big = {
        "w_in": full["w_in"].transpose(1, 0, 2).reshape(d, IN_COLS),
        "w_out": full["w_out"].reshape(d, d),
        "w_cq": full["w_cq"].reshape(d, d),
        "w_ckv": full["w_ckv"].transpose(1, 0, 2).reshape(d, 2 * d),
        "w_co": full["w_co"].reshape(d, d),
        "w_ff1": full["w_ff1"].transpose(1, 0, 2).reshape(d, D_FF),
        "w_ff2": full["w_ff2"].reshape(D_FF, d),
    }
    small = _small_params(args)

    loss, grad_x, gsmall, gbig = _local_step(
        x.reshape(nb * s, d), mem.reshape(nb * N_MEM, d), loss_target.reshape(nb * s, d), small, big, nb, s)

    slabs = [gbig[k] for k in BIG]
    recv = _sibling_exchange(slabs)
    sums = [_chip_sum(a, b, c_idx, "chip_sum_" + k) for k, a, b in zip(BIG, slabs, recv)]
    parts = _chip_exchange(sums)
    halves = [_sum4(a, "sum4_" + k) for k, a in zip(BIG, parts)]
    both = _half_exchange(halves)

    out = {"grad_x": grad_x.reshape(nb, s, d)}
    for k, g2 in zip(BIG, both):
        shp = args[k].shape
        g = g2.reshape(shp[1], shp[2])
        dl, nm, nv = _adamw(g, args[k][0], args["m_" + k][0], args["v_" + k][0], "adamw_" + k)
        out["grad_" + k], out["delta_" + k], out["new_m_" + k], out["new_v_" + k] = (
            a.reshape(shp) for a in (g, dl, nm, nv))

    shapes = [args[k].shape for k in SMALL]
    n_small = sum(math.prod(sh) for sh in shapes)
    rows = -(-(n_small + 1) // 1024) * 8
    reduced = _small_all_reduce(_pack([gsmall[k] for k in SMALL] + [loss], rows))
    dl, nm, nv = _adamw(reduced, _pack([args[k] for k in SMALL], rows), _pack([args["m_" + k] for k in SMALL], rows),
                        _pack([args["v_" + k] for k in SMALL], rows), "adamw_small")
    for name, arr in (("grad_", reduced), ("delta_", dl), ("new_m_", nm), ("new_v_", nv)):
        for k, a in zip(SMALL, _unpack(arr, shapes)):
            out[name + k] = a
    out["loss"] = reduced.reshape(-1)[n_small]

    names = ["norm_mix_g", "w_in", "gm_v_norm_g", "w_spatial", "b_spatial", "head_norm_g", "w_out", "norm_cross_g",
             "norm_mem_g", "w_cq", "w_ckv", "w_co", "norm_ffn_g", "w_ff1", "w_ff2", "norm_final_g"]
    return (out["loss"], out["grad_x"], *[out["grad_" + k] for k in names], *[out["delta_" + k] for k in names],
            *[out["new_m_" + k] for k in names], *[out["new_v_" + k] for k in names])
```

```python
import functools
import math

import jax
import jax.numpy as jnp
from jax import lax
from jax.experimental import pallas as pl
from jax.experimental.pallas import tpu as pltpu

F32 = jnp.float32
BF = jnp.bfloat16

EPS = 1e-6
D_MODEL = 1024
CHUNK = 128
GM_GROUPS = 4
GM_WIDTH = 512
SB_WIDTH = 512
HEAD_LANES = 64
SB_SCALE = 0.125
SB_SKIP = -104.0
X_HEADS = 4
X_HEAD_DIM = 256
N_MEM = 256
D_FF = 4096
IN_COLS = 2560
N_CHIPS = 4
N_DEV = 8

ADAM_LR = 0.001
ADAM_B1 = 0.9
ADAM_B2 = 0.999
ADAM_EPS = 1e-08
ADAM_WD = 0.01
ADAM_STEP = 10

V7X_VMEM_BYTES = 64 * 1024 * 1024
MESH = pl.DeviceIdType.MESH
ANY = pl.BlockSpec(memory_space=pl.ANY)

GELU_C = math.sqrt(2.0 / math.pi)
GELU_A = 0.044715


def _params(vmem_mb, sem=None):
    assert vmem_mb * 1024 * 1024 <= V7X_VMEM_BYTES
    return pltpu.CompilerParams(vmem_limit_bytes=vmem_mb * 1024 * 1024, dimension_semantics=sem)


def _dot(a, b):
    return jnp.dot(a, b, preferred_element_type=F32)


def _dot_bt(a, b):
    return lax.dot_general(a, b, (((1,), (1,)), ((), ())), preferred_element_type=F32)


def _dot_at(a, b):
    return lax.dot_general(a, b, (((0,), (0,)), ((), ())), preferred_element_type=F32)


def _gelu(x):
    t = jnp.tanh(GELU_C * (x + GELU_A * x * x * x))
    return 0.5 * x * (1.0 + t)


def _gelu_grad(x):
    x2 = x * x
    t = jnp.tanh(GELU_C * (x + GELU_A * x2 * x))
    return 0.5 * (1.0 + t) + 0.5 * x * (1.0 - t * t) * (GELU_C * (1.0 + 3.0 * GELU_A * x2))


def _rs(x):
    return lax.rsqrt(jnp.mean(x * x, axis=-1, keepdims=True) + EPS)


def _rms_bwd(dxn, xhat, r, g):
    dxh = dxn * g
    dx = r * (dxh - xhat * jnp.mean(dxh * xhat, axis=-1, keepdims=True))
    return dx, dxn * xhat


def _norm_matmul(x, g, w, tm, name):
    t, d = x.shape
    n = w.shape[1]
    tm = min(tm, t)

    def body(x_ref, g_ref, w_ref, out_ref, xn_ref):
        xv = x_ref[...]
        xn = (xv * _rs(xv) * g_ref[...]).astype(BF)
        xn_ref[...] = xn
        out_ref[...] = _dot(xn, w_ref[...]).astype(out_ref.dtype)

    return pl.pallas_call(
        body, name=name, grid=(t // tm,),
        in_specs=[pl.BlockSpec((tm, d), lambda i: (i, 0)), pl.BlockSpec((1, d), lambda i: (0, 0)),
                  pl.BlockSpec((d, n), lambda i: (0, 0))],
        out_specs=[pl.BlockSpec((tm, n), lambda i: (i, 0)), pl.BlockSpec((tm, d), lambda i: (i, 0))],
        out_shape=[jax.ShapeDtypeStruct((t, n), BF), jax.ShapeDtypeStruct((t, d), BF)],
        compiler_params=_params(48, ("arbitrary",)),
    )(x, g, w)


def _wgrad(a, g, tn, tk, name, square_a=False, col_shards=1):
    t, m = a.shape
    n = g.shape[1]
    tk = min(tk, t)
    tm = min(m, 1024)
    ns = n // col_shards
    assert ns % tn == 0 and m % tm == 0
    per = ns // tn
    nk = t // tk

    def body(a_ref, g_ref, o_ref):
        k = pl.program_id(2)

        @pl.when(k == 0)
        def _():
            o_ref[...] = jnp.zeros_like(o_ref)

        av = a_ref[...]
        if square_a:
            af = av.astype(F32)
            av = af * af
        o_ref[...] += _dot_at(av.astype(BF), g_ref[...].astype(BF))

    return pl.pallas_call(
        body, name=name, grid=(m // tm, n // tn, nk),
        in_specs=[pl.BlockSpec((tk, tm), lambda i, j, k: (k, i)), pl.BlockSpec((tk, tn), lambda i, j, k: (k, j))],
        out_specs=pl.BlockSpec((None, tm, tn), lambda i, j, k: (j // per, i, j % per)),
        out_shape=jax.ShapeDtypeStruct((col_shards, m, ns), F32),
        compiler_params=_params(48, ("arbitrary", "arbitrary", "arbitrary")),
    )(a, g)


def _matmul_bt(a, w, tm, name):
    t, n = a.shape
    k = w.shape[0]
    tm = min(tm, t)

    def body(a_ref, w_ref, o_ref):
        o_ref[...] = _dot_bt(a_ref[...].astype(BF), w_ref[...]).astype(o_ref.dtype)

    return pl.pallas_call(
        body, name=name, grid=(t // tm,),
        in_specs=[pl.BlockSpec((tm, n), lambda i: (i, 0)), pl.BlockSpec((k, n), lambda i: (0, 0))],
        out_specs=pl.BlockSpec((tm, k), lambda i: (i, 0)),
        out_shape=jax.ShapeDtypeStruct((t, k), BF),
        compiler_params=_params(32, ("arbitrary",)),
    )(a, w)


def _gmlp_fwd(proj, gg, wt, bb, hg, tm):
    t = proj.shape[0]
    tm = min(tm, t)

    def body(u_ref, v_ref, gg_ref, wt_ref, bb_ref, hg_ref, out_ref):
        for cc in range(tm // CHUNK):
            rows = slice(cc * CHUNK, (cc + 1) * CHUNK)
            for g in range(GM_GROUPS):
                cols = slice(g * 128, (g + 1) * 128)
                u = _gelu(u_ref[rows, cols].astype(F32))
                gv = _gelu(v_ref[rows, cols].astype(F32))
                vn = gv * _rs(gv) * gg_ref[:, cols]
                mixed = _dot(wt_ref[g], vn.astype(BF)) + bb_ref[g]
                a = u * mixed
                out_ref[rows, cols] = (a * _rs(a) * hg_ref[:, cols]).astype(BF)

    return pl.pallas_call(
        body, name="gmlp_fwd", grid=(t // tm,),
        in_specs=[pl.BlockSpec((tm, 512), lambda i: (i, 0)), pl.BlockSpec((tm, 512), lambda i: (i, 1)),
                  pl.BlockSpec((1, 512), lambda i: (0, 0)), pl.BlockSpec((4, 128, 128), lambda i: (0, 0, 0)),
                  pl.BlockSpec((4, 128, 128), lambda i: (0, 0, 0)), pl.BlockSpec((1, 512), lambda i: (0, 0))],
        out_specs=pl.BlockSpec((tm, 512), lambda i: (i, 0)),
        out_shape=jax.ShapeDtypeStruct((t, 512), BF),
        compiler_params=_params(32, ("arbitrary",)),
    )(proj, proj, gg, wt, bb, hg)


def _gmlp_bwd(proj, dmerged, gg, wt, wtt, bb, hg, tm):
    t = proj.shape[0]
    tm = min(tm, t)
    nsteps = t // tm

    def body(u_ref, v_ref, dm_ref, gg_ref, wt_ref, wtt_ref, bb_ref, hg_ref,
             dp_ref, dw_ref, db_ref, dgg_ref, dhg_ref):
        i = pl.program_id(0)

        @pl.when(i == 0)
        def _():
            dw_ref[...] = jnp.zeros_like(dw_ref)
            db_ref[...] = jnp.zeros_like(db_ref)
            dgg_ref[...] = jnp.zeros_like(dgg_ref)
            dhg_ref[...] = jnp.zeros_like(dhg_ref)

        for cc in range(tm // CHUNK):
            rows = slice(cc * CHUNK, (cc + 1) * CHUNK)
            for g in range(GM_GROUPS):
                cols = slice(g * 128, (g + 1) * 128)
                up = u_ref[rows, cols].astype(F32)
                gp = v_ref[rows, cols].astype(F32)
                u = _gelu(up)
                gv = _gelu(gp)
                rv = _rs(gv)
                gvh = gv * rv
                ggv = gg_ref[:, cols]
                vnb = (gvh * ggv).astype(BF)
                mixed = _dot(wt_ref[g], vnb) + bb_ref[g]
                a = u * mixed
                ra = _rs(a)
                ah = a * ra
                dm = dm_ref[rows, cols].astype(F32)
                dhg_ref[:, cols] += jnp.sum(dm * ah, axis=0, keepdims=True)
                dah = dm * hg_ref[:, cols]
                da = ra * (dah - ah * jnp.mean(dah * ah, axis=-1, keepdims=True))
                du = da * mixed
                dmix = da * u
                db_ref[g] += dmix
                dmb = dmix.astype(BF)
                dw_ref[g] += _dot_bt(dmb, vnb)
                dvn = _dot(wtt_ref[g], dmb)
                dgg_ref[:, cols] += jnp.sum(dvn * gvh, axis=0, keepdims=True)
                dgh = dvn * ggv
                dgv = rv * (dgh - gvh * jnp.mean(dgh * gvh, axis=-1, keepdims=True))
                dp_ref[rows, cols] = (du * _gelu_grad(up)).astype(BF)
                dp_ref[rows, 512 + g * 128:512 + (g + 1) * 128] = (dgv * _gelu_grad(gp)).astype(BF)

        @pl.when(i == nsteps - 1)
        def _():
            r = lax.broadcasted_iota(jnp.int32, (CHUNK, CHUNK), 0)
            c = lax.broadcasted_iota(jnp.int32, (CHUNK, CHUNK), 1)
            for g in range(GM_GROUPS):
                dw_ref[g] = jnp.where(c <= r, dw_ref[g], 0.0)
                db_ref[g] = jnp.broadcast_to(jnp.sum(db_ref[g], axis=-1, keepdims=True), (CHUNK, CHUNK))

    small = lambda shape: pl.BlockSpec(shape, lambda i: (0,) * len(shape))
    return pl.pallas_call(
        body, name="gmlp_bwd", grid=(nsteps,),
        in_specs=[pl.BlockSpec((tm, 512), lambda i: (i, 0)), pl.BlockSpec((tm, 512), lambda i: (i, 1)),
                  pl.BlockSpec((tm, 512), lambda i: (i, 0)), small((1, 512)), small((4, 128, 128)),
                  small((4, 128, 128)), small((4, 128, 128)), small((1, 512))],
        out_specs=[pl.BlockSpec((tm, 1024), lambda i: (i, 0)), small((4, 128, 128)), small((4, 128, 128)),
                   small((1, 512)), small((1, 512))],
        out_shape=[jax.ShapeDtypeStruct((t, 1024), BF), jax.ShapeDtypeStruct((4, 128, 128), F32),
                   jax.ShapeDtypeStruct((4, 128, 128), F32), jax.ShapeDtypeStruct((1, 512), F32),
                   jax.ShapeDtypeStruct((1, 512), F32)],
        compiler_params=_params(32, ("arbitrary",)),
    )(proj, proj, dmerged, gg, wt, wtt, bb, hg)


def _log_sig_pair(z):
    sp = jnp.log(1.0 + jnp.exp(-jnp.abs(z)))
    return -(jnp.maximum(z, 0.0) + sp), jnp.minimum(z, 0.0) - sp


def _hi_lo(m):
    hi = m.astype(BF)
    return hi, (m - hi.astype(F32)).astype(BF)


def _head_sums(x, h0):
    s0 = jnp.sum(jnp.where(h0, x, 0.0), axis=-1, keepdims=True)
    s1 = jnp.sum(jnp.where(h0, 0.0, x), axis=-1, keepdims=True)
    return jnp.where(h0, s0, s1)


def _sb_setup(q_ref, tq):
    lane = lax.broadcasted_iota(jnp.int32, (tq, 128), 1)
    h0 = lane < HEAD_LANES
    qs = q_ref[...] * SB_SCALE
    zero = jnp.zeros_like(qs)
    qst = jnp.concatenate([jnp.where(h0, qs, zero), jnp.where(h0, zero, qs)], axis=0)
    r = lax.broadcasted_iota(jnp.int32, (2 * tq, tq), 0)
    c = lax.broadcasted_iota(jnp.int32, (2 * tq, tq), 1)
    causal = c < jnp.where(r >= tq, r - tq, r)
    return h0, qst, causal


def _tri(tq, op):
    return op(lax.broadcasted_iota(jnp.int32, (tq, tq), 0), lax.broadcasted_iota(jnp.int32, (tq, tq), 1)).astype(BF)


def _sb_fwd(proj, hg, nb, s, tq):
    t = nb * s
    tq = min(tq, s)
    nq = s // tq

    def body(q_ref, k_ref, v_ref, hg_ref, o_ref, tot_ref, mb_ref, nblk_ref, acc, cr):
        i = pl.program_id(2)
        h0, qst, causal = _sb_setup(q_ref, tq)
        tri_gt = _tri(tq, lambda r, c: r > c)
        acc[...] = jnp.zeros_like(acc)
        cr[...] = jnp.zeros_like(cr)

        def block(j, masked):
            start = pl.multiple_of(j * tq, tq)
            kj = k_ref[pl.ds(start, tq), :]
            vj = v_ref[pl.ds(start, tq), :]
            m, l = _log_sig_pair(_dot_bt(qst, kj))
            if masked:
                m = jnp.where(causal, m, 0.0)
            mh, ml = _hi_lo(m)
            a = jnp.exp(l + (_dot(mh, tri_gt) + _dot(ml, tri_gt)) + cr[...])
            if masked:
                a = jnp.where(causal, a, 0.0)
            acc[...] += _dot(a.astype(BF), vj)
            cnew = cr[...] + jnp.sum(m, axis=-1, keepdims=True)
            cr[...] = cnew
            return jnp.max(cnew)

        def cond(carry):
            return jnp.logical_and(carry[0] < i, carry[1] > SB_SKIP)

        def step(carry):
            return carry[0] + 1, block(i - 1 - carry[0], False)

        walked, _ = lax.while_loop(cond, step, (jnp.int32(0), block(i, True)))

        o = jnp.where(h0, acc[0:tq, :], acc[tq:2 * tq, :])
        o_ref[...] = o
        tot_ref[...] = jnp.where(h0, cr[0:tq, :], cr[tq:2 * tq, :])
        ro = lax.rsqrt(_head_sums(o * o, h0) * (1.0 / HEAD_LANES) + EPS)
        mb_ref[...] = (o * ro * hg_ref[...]).astype(BF)
        nblk_ref[...] = jnp.full((8, 128), walked.astype(F32))

    blk = lambda col0: pl.BlockSpec((tq, 128), lambda b, hp, i: (b * nq + i, col0 + hp))
    seq = lambda col0: pl.BlockSpec((s, 128), lambda b, hp, i: (b, col0 + hp))
    return pl.pallas_call(
        body, name="sb_fwd", grid=(nb, 4, nq),
        in_specs=[blk(8), seq(12), seq(16), pl.BlockSpec((1, 128), lambda b, hp, i: (0, 4 + hp))],
        out_specs=[blk(0), blk(0), blk(0), pl.BlockSpec((None, None, 8, 128), lambda b, hp, i: (b, hp, i, 0))],
        out_shape=[jax.ShapeDtypeStruct((t, 512), F32), jax.ShapeDtypeStruct((t, 512), F32),
                   jax.ShapeDtypeStruct((t, 512), BF), jax.ShapeDtypeStruct((nb, 4, nq * 8, 128), F32)],
        scratch_shapes=[pltpu.VMEM((2 * tq, 128), F32), pltpu.VMEM((2 * tq, 1), F32)],
        compiler_params=_params(32, ("arbitrary", "arbitrary", "arbitrary")),
    )(proj, proj, proj, hg)


def _sb_bwd(proj, o_sb, tot, nblk, dmerged, hg, nb, s, tq):
    t = nb * s
    tq = min(tq, s)
    nq = s // tq

    def body(q_ref, k_ref, v_ref, o_ref, tot_ref, nblk_ref, dm_ref, hg_ref,
             dq_ref, dk_ref, dv_ref, dhg_ref, dk_acc, dv_acc, dq_acc, cm, cg):
        i = pl.program_id(2)
        h0, qst, causal = _sb_setup(q_ref, tq)
        tri_le = _tri(tq, lambda r, c: r <= c)
        tri_lt = _tri(tq, lambda r, c: r < c)

        @pl.when(i == 0)
        def _():
            dk_acc[...] = jnp.zeros_like(dk_acc)
            dv_acc[...] = jnp.zeros_like(dv_acc)
            dhg_ref[...] = jnp.zeros_like(dhg_ref)

        for ref in (dq_acc, cm, cg):
            ref[...] = jnp.zeros_like(ref)

        o = o_ref[...]
        ro = lax.rsqrt(_head_sums(o * o, h0) * (1.0 / HEAD_LANES) + EPS)
        oh = o * ro
        dm = dm_ref[...].astype(F32)
        dhg_ref[...] += jnp.sum(dm * oh, axis=0, keepdims=True)
        doh = dm * hg_ref[...]
        do = (ro * (doh - oh * (_head_sums(doh * oh, h0) * (1.0 / HEAD_LANES)))).astype(BF)
        zb = jnp.zeros_like(do)
        dost = jnp.concatenate([jnp.where(h0, do, zb), jnp.where(h0, zb, do)], axis=0)
        tots = jnp.concatenate([tot_ref[:, 0:1], tot_ref[:, HEAD_LANES:HEAD_LANES + 1]], axis=0)

        def block(j, masked):
            start = pl.multiple_of(j * tq, tq)
            kj = k_ref[pl.ds(start, tq), :]
            vj = v_ref[pl.ds(start, tq), :]
            m, l = _log_sig_pair(_dot_bt(qst, kj))
            if masked:
                m = jnp.where(causal, m, 0.0)
            mh, ml = _hi_lo(m)
            a = jnp.exp(l + (tots - cm[...] - (_dot(mh, tri_le) + _dot(ml, tri_le))))
            if masked:
                a = jnp.where(causal, a, 0.0)
            gm = a * _dot_bt(dost, vj)
            pp = cg[...] + _dot(gm.astype(BF), tri_lt)
            dz = gm - jnp.exp(l) * (gm + pp)
            if masked:
                dz = jnp.where(causal, dz, 0.0)
            dzb = dz.astype(BF)
            dq_acc[...] += _dot(dzb, kj)
            dk_acc[pl.ds(start, tq), :] += _dot_at(dzb, qst)
            dv_acc[pl.ds(start, tq), :] += _dot_at(a.astype(BF), dost)
            cm[...] += jnp.sum(m, axis=-1, keepdims=True)
            cg[...] += jnp.sum(gm, axis=-1, keepdims=True)

        def step(j, carry):
            block(j, False)
            return carry

        walked = jnp.clip(jnp.max(nblk_ref[...]).astype(jnp.int32), 0, i)
        lax.fori_loop(i - walked, i, step, 0)
        block(i, True)

        dq_ref[...] = (jnp.where(h0, dq_acc[0:tq, :], dq_acc[tq:2 * tq, :]) * SB_SCALE).astype(BF)

        @pl.when(i == nq - 1)
        def _():
            dk_ref[...] = dk_acc[...].astype(BF)
            dv_ref[...] = dv_acc[...].astype(BF)

    blk = lambda col0: pl.BlockSpec((tq, 128), lambda b, hp, i: (b * nq + i, col0 + hp))
    seq = lambda col0: pl.BlockSpec((s, 128), lambda b, hp, i: (b, col0 + hp))
    return pl.pallas_call(
        body, name="sb_bwd", grid=(nb, 4, nq),
        in_specs=[blk(8), seq(12), seq(16), blk(0), blk(0),
                  pl.BlockSpec((None, None, 8, 128), lambda b, hp, i: (b, hp, i, 0)), blk(4),
                  pl.BlockSpec((1, 128), lambda b, hp, i: (0, 4 + hp))],
        out_specs=[blk(0), seq(0), seq(0), pl.BlockSpec((None, 1, 128), lambda b, hp, i: (b, 0, hp))],
        out_shape=[jax.ShapeDtypeStruct((t, 512), BF), jax.ShapeDtypeStruct((t, 512), BF),
                   jax.ShapeDtypeStruct((t, 512), BF), jax.ShapeDtypeStruct((nb, 1, 512), F32)],
        scratch_shapes=[pltpu.VMEM((s, 128), F32), pltpu.VMEM((s, 128), F32), pltpu.VMEM((2 * tq, 128), F32),
                        pltpu.VMEM((2 * tq, 1), F32), pltpu.VMEM((2 * tq, 1), F32)],
        compiler_params=_params(40, ("arbitrary", "arbitrary", "arbitrary")),
    )(proj, proj, proj, o_sb, tot, nblk, dmerged, hg)


def _softmax_rows(sc):
    e = jnp.exp(sc - jnp.max(sc, axis=-1, keepdims=True))
    return e / jnp.sum(e, axis=-1, keepdims=True)


def _mix_cross_fwd(x, ma, mb, w_out, gc, w_cq, kv, w_co, s, tm):
    t, d = x.shape
    tm = min(tm, s)
    per = s // tm
    inv = 1.0 / math.sqrt(X_HEAD_DIM)

    def body(x_ref, ma_ref, mb_ref, wo_ref, gc_ref, wq_ref, kv_ref, wc_ref, h1_ref, h2_ref, hn_ref, oc_ref):
        h1 = x_ref[...] + _dot(ma_ref[...], wo_ref[0:512, :]) + _dot(mb_ref[...], wo_ref[512:1024, :])
        h1_ref[...] = h1
        hn = (h1 * _rs(h1) * gc_ref[...]).astype(BF)
        hn_ref[...] = hn
        qc = _dot(hn, wq_ref[...]).astype(BF)
        for h in range(X_HEADS):
            cols = slice(h * X_HEAD_DIM, (h + 1) * X_HEAD_DIM)
            kh = kv_ref[:, h * X_HEAD_DIM:(h + 1) * X_HEAD_DIM]
            vh = kv_ref[:, d + h * X_HEAD_DIM:d + (h + 1) * X_HEAD_DIM]
            p = _softmax_rows(_dot_bt(qc[:, cols], kh) * inv)
            oc_ref[:, cols] = _dot(p.astype(BF), vh).astype(BF)
        h2_ref[...] = h1 + _dot(oc_ref[...], wc_ref[...])

    row = lambda width: pl.BlockSpec((tm, width), lambda i: (i, 0))
    full = lambda a, b: pl.BlockSpec((a, b), lambda i: (0, 0))
    return pl.pallas_call(
        body, name="mix_cross_fwd", grid=(t // tm,),
        in_specs=[row(d), row(512), row(512), full(d, d), full(1, d), full(d, d),
                  pl.BlockSpec((N_MEM, 2 * d), lambda i: (i // per, 0)), full(d, d)],
        out_specs=[row(d), row(d), row(d), row(d)],
        out_shape=[jax.ShapeDtypeStruct((t, d), F32), jax.ShapeDtypeStruct((t, d), F32),
                   jax.ShapeDtypeStruct((t, d), BF), jax.ShapeDtypeStruct((t, d), BF)],
        compiler_params=_params(48, ("arbitrary",)),
    )(x, ma, mb, w_out, gc, w_cq, kv, w_co)


def _cross_bwd(dh2, h1, gc, w_cq, kv, w_co, s, tm):
    t, d = dh2.shape
    tm = min(tm, s)
    per = s // tm
    nb = t // s
    inv = 1.0 / math.sqrt(X_HEAD_DIM)

    def body(dh2_ref, h1_ref, gc_ref, wq_ref, kv_ref, wc_ref, dh1_ref, dqc_ref, dkv_ref, dgc_ref):
        i = pl.program_id(0)

        @pl.when(i == 0)
        def _():
            dgc_ref[...] = jnp.zeros_like(dgc_ref)

        @pl.when(i % per == 0)
        def _():
            dkv_ref[...] = jnp.zeros_like(dkv_ref)

        dh2 = dh2_ref[...]
        h1 = h1_ref[...]
        r = _rs(h1)
        h1h = h1 * r
        gcv = gc_ref[...]
        hn = (h1h * gcv).astype(BF)
        qc = _dot(hn, wq_ref[...]).astype(BF)
        do = _dot_bt(dh2.astype(BF), wc_ref[...]).astype(BF)
        for h in range(X_HEADS):
            cols = slice(h * X_HEAD_DIM, (h + 1) * X_HEAD_DIM)
            vcols = slice(d + h * X_HEAD_DIM, d + (h + 1) * X_HEAD_DIM)
            kh = kv_ref[:, cols]
            vh = kv_ref[:, vcols]
            p = _softmax_rows(_dot_bt(qc[:, cols], kh) * inv)
            dp = _dot_bt(do[:, cols], vh)
            ds = (p * (dp - jnp.sum(dp * p, axis=-1, keepdims=True)) * inv).astype(BF)
            dqc_ref[:, cols] = _dot(ds, kh).astype(BF)
            dkv_ref[:, cols] += _dot_at(ds, qc[:, cols])
            dkv_ref[:, vcols] += _dot_at(p.astype(BF), do[:, cols])
        dhn = _dot_bt(dqc_ref[...], wq_ref[...])
        dx, dg = _rms_bwd(dhn, h1h, r, gcv)
        dh1_ref[...] = dh2 + dx
        dgc_ref[...] += jnp.sum(dg, axis=0, keepdims=True)

    row = lambda width: pl.BlockSpec((tm, width), lambda i: (i, 0))
    full = lambda a, b: pl.BlockSpec((a, b), lambda i: (0, 0))
    kvspec = pl.BlockSpec((N_MEM, 2 * d), lambda i: (i // per, 0))
    return pl.pallas_call(
        body, name="cross_bwd", grid=(t // tm,),
        in_specs=[row(d), row(d), full(1, d), full(d, d), kvspec, full(d, d)],
        out_specs=[row(d), row(d), kvspec, full(1, d)],
        out_shape=[jax.ShapeDtypeStruct((t, d), F32), jax.ShapeDtypeStruct((t, d), BF),
                   jax.ShapeDtypeStruct((nb * N_MEM, 2 * d), F32), jax.ShapeDtypeStruct((1, d), F32)],
        compiler_params=_params(48, ("arbitrary",)),
    )(dh2, h1, gc, w_cq, kv, w_co)


def _mem_bwd(mem, gm, dkv, w_ckv, tm):
    t, d = mem.shape
    tm = min(tm, t)

    def body(mem_ref, dkv_ref, w_ref, dg_ref):
        @pl.when(pl.program_id(0) == 0)
        def _():
            dg_ref[...] = jnp.zeros_like(dg_ref)

        mv = mem_ref[...]
        dmn = _dot_bt(dkv_ref[...].astype(BF), w_ref[...])
        dg_ref[...] += jnp.sum(dmn * (mv * _rs(mv)), axis=0, keepdims=True)

    del gm
    return pl.pallas_call(
        body, name="mem_bwd", grid=(t // tm,),
        in_specs=[pl.BlockSpec((tm, d), lambda i: (i, 0)), pl.BlockSpec((tm, 2 * d), lambda i: (i, 0)),
                  pl.BlockSpec((d, 2 * d), lambda i: (0, 0))],
        out_specs=pl.BlockSpec((1, d), lambda i: (0, 0)),
        out_shape=jax.ShapeDtypeStruct((1, d), F32),
        compiler_params=_params(32, ("arbitrary",)),
    )(mem, dkv, w_ckv)


def _ffn_loss_fwd(h2, gf, w1, w2, gl, target, tm):
    t, d = h2.shape
    tm = min(tm, t)

    def body(h2_ref, gf_ref, w1_ref, w2_ref, gl_ref, tg_ref, hn_ref, f_ref, dh3_ref, dgl_ref, loss_ref):
        @pl.when(pl.program_id(0) == 0)
        def _():
            dgl_ref[...] = jnp.zeros_like(dgl_ref)
            loss_ref[...] = jnp.zeros_like(loss_ref)

        h2 = h2_ref[...]
        hn = (h2 * _rs(h2) * gf_ref[...]).astype(BF)
        hn_ref[...] = hn
        h3 = h2
        for c in range(4):
            f = jnp.maximum(_dot(hn, w1_ref[c]), 0.0)
            f_ref[:, c * 1024:(c + 1) * 1024] = f.astype(BF)
            h3 = h3 + _dot((f * f).astype(BF), w2_ref[c])
        r3 = _rs(h3)
        yh = h3 * r3
        glv = gl_ref[...]
        e = yh * glv - tg_ref[...]
        loss_ref[...] += 0.5 * jnp.sum(jnp.sum(e * e, axis=-1, keepdims=True) * (1.0 / d), axis=0, keepdims=True)
        dy = e * (1.0 / d)
        dx, dg = _rms_bwd(dy, yh, r3, glv)
        dh3_ref[...] = dx
        dgl_ref[...] += jnp.sum(dg, axis=0, keepdims=True)

    row = lambda width: pl.BlockSpec((tm, width), lambda i: (i, 0))
    return pl.pallas_call(
        body, name="ffn_loss_fwd", grid=(t // tm,),
        in_specs=[row(d), pl.BlockSpec((1, d), lambda i: (0, 0)), pl.BlockSpec((4, d, 1024), lambda i: (0, 0, 0)),
                  pl.BlockSpec((4, 1024, d), lambda i: (0, 0, 0)), pl.BlockSpec((1, d), lambda i: (0, 0)), row(d)],
        out_specs=[row(d), row(D_FF), row(d), pl.BlockSpec((1, d), lambda i: (0, 0)),
                   pl.BlockSpec((1, 1), lambda i: (0, 0))],
        out_shape=[jax.ShapeDtypeStruct((t, d), BF), jax.ShapeDtypeStruct((t, D_FF), BF),
                   jax.ShapeDtypeStruct((t, d), F32), jax.ShapeDtypeStruct((1, d), F32),
                   jax.ShapeDtypeStruct((1, 1), F32)],
        compiler_params=_params(56, ("arbitrary",)),
    )(h2, gf, w1, w2, gl, target)


def _ffn_bwd(dh3, f, h2, gf, w1, w2, tm):
    t, d = h2.shape
    tm = min(tm, t)

    def body(dh3_ref, f_ref, h2_ref, gf_ref, w1_ref, w2_ref, dh2_ref, dpre_ref, dgf_ref):
        @pl.when(pl.program_id(0) == 0)
        def _():
            dgf_ref[...] = jnp.zeros_like(dgf_ref)

        dh3 = dh3_ref[...]
        dh3b = dh3.astype(BF)
        dhn = jnp.zeros((tm, d), F32)
        for c in range(4):
            cols = slice(c * 1024, (c + 1) * 1024)
            dpre = (_dot_bt(dh3b, w2_ref[c]) * (2.0 * f_ref[:, cols].astype(F32))).astype(BF)
            dpre_ref[:, cols] = dpre
            dhn = dhn + _dot_bt(dpre, w1_ref[c])
        h2 = h2_ref[...]
        r = _rs(h2)
        dx, dg = _rms_bwd(dhn, h2 * r, r, gf_ref[...])
        dh2_ref[...] = dh3 + dx
        dgf_ref[...] += jnp.sum(dg, axis=0, keepdims=True)

    row = lambda width: pl.BlockSpec((tm, width), lambda i: (i, 0))
    return pl.pallas_call(
        body, name="ffn_bwd", grid=(t // tm,),
        in_specs=[row(d), row(D_FF), row(d), pl.BlockSpec((1, d), lambda i: (0, 0)),
                  pl.BlockSpec((4, d, 1024), lambda i: (0, 0, 0)), pl.BlockSpec((4, 1024, d), lambda i: (0, 0, 0))],
        out_specs=[row(d), row(D_FF), pl.BlockSpec((1, d), lambda i: (0, 0))],
        out_shape=[jax.ShapeDtypeStruct((t, d), F32), jax.ShapeDtypeStruct((t, D_FF), BF),
                   jax.ShapeDtypeStruct((1, d), F32)],
        compiler_params=_params(56, ("arbitrary",)),
    )(dh3, f, h2, gf, w1, w2)


def _in_bwd(dproj, dh1, x, g, w_in, tm):
    t, d = x.shape
    n = w_in.shape[1]
    tm = min(tm, t)

    def body(dp_ref, dh1_ref, x_ref, g_ref, w_ref, dx_ref, dg_ref):
        @pl.when(pl.program_id(0) == 0)
        def _():
            dg_ref[...] = jnp.zeros_like(dg_ref)

        dxn = _dot_bt(dp_ref[...], w_ref[...])
        xv = x_ref[...]
        r = _rs(xv)
        dx, dg = _rms_bwd(dxn, xv * r, r, g_ref[...])
        dx_ref[...] = dh1_ref[...] + dx
        dg_ref[...] += jnp.sum(dg, axis=0, keepdims=True)

    row = lambda width: pl.BlockSpec((tm, width), lambda i: (i, 0))
    return pl.pallas_call(
        body, name="in_bwd", grid=(t // tm,),
        in_specs=[row(n), row(d), row(d), pl.BlockSpec((1, d), lambda i: (0, 0)),
                  pl.BlockSpec((d, n), lambda i: (0, 0))],
        out_specs=[row(d), pl.BlockSpec((1, d), lambda i: (0, 0))],
        out_shape=[jax.ShapeDtypeStruct((t, d), F32), jax.ShapeDtypeStruct((1, d), F32)],
        compiler_params=_params(48, ("arbitrary",)),
    )(dproj, dh1, x, g, w_in)


def _local_step(x, mem, target, small, big, nb, s, tq=256):
    d = D_MODEL
    g_mix, g_v, w_sp, b_sp, g_head, g_cross, g_mem, g_ffn, g_fin = (
        small[k] for k in ("norm_mix_g", "gm_v_norm_g", "w_spatial", "b_spatial", "head_norm_g", "norm_cross_g",
                           "norm_mem_g", "norm_ffn_g", "norm_final_g"))
    tri = jnp.tril(jnp.ones((CHUNK, CHUNK), dtype=bool))
    w_sp_m = jnp.where(tri[None], w_sp, 0.0)
    wt = w_sp_m.astype(BF)
    wtt = jnp.swapaxes(w_sp_m, 1, 2).astype(BF)
    bb = jnp.broadcast_to(b_sp[:, :, None], (GM_GROUPS, CHUNK, CHUNK))
    hg_a = g_head[:, :GM_WIDTH]
    w1c = big["w_ff1"].reshape(d, 4, 1024).transpose(1, 0, 2)
    w2c = big["w_ff2"].reshape(4, 1024, d)

    proj, xn = _norm_matmul(x, g_mix, big["w_in"], 512, "in_proj")
    ma = _gmlp_fwd(proj, g_v, wt, bb, hg_a, 512)
    o_sb, tot, mb, nblk = _sb_fwd(proj, g_head, nb, s, tq)
    kv, memn = _norm_matmul(mem, g_mem, big["w_ckv"], 512, "mem_proj")
    h1, h2, hn, oc = _mix_cross_fwd(x, ma, mb, big["w_out"], g_cross, big["w_cq"], kv, big["w_co"], s, 512)
    hn2, f, dh3, d_fin, loss = _ffn_loss_fwd(h2, g_ffn, w1c, w2c, g_fin, target, 256)

    dh2, dpre, d_ffn = _ffn_bwd(dh3, f, h2, g_ffn, w1c, w2c, 256)
    gw_ff2 = _wgrad(f, dh3, 1024, 512, "wgrad_ff2", square_a=True)
    gw_ff1 = _wgrad(hn2, dpre, 1024, 512, "wgrad_ff1", col_shards=4)
    dh1, dqc, dkv, d_cross = _cross_bwd(dh2, h1, g_cross, big["w_cq"], kv, big["w_co"], s, 512)
    gw_co = _wgrad(oc, dh2, 1024, 512, "wgrad_co")
    gw_cq = _wgrad(hn, dqc, 1024, 512, "wgrad_cq")
    gw_ckv = _wgrad(memn, dkv, 512, 512, "wgrad_ckv", col_shards=4)
    d_mem = _mem_bwd(mem, g_mem, dkv, big["w_ckv"], 512)
    dmerged = _matmul_bt(dh1, big["w_out"], 512, "out_bwd")
    merged = jnp.concatenate([ma, mb], axis=1)
    gw_out = _wgrad(merged, dh1, 1024, 512, "wgrad_out")
    dugv, d_wsp, d_bb, d_gv, d_hga = _gmlp_bwd(proj, dmerged, g_v, wt, wtt, bb, hg_a, 512)
    dq, dk, dv, d_hgb = _sb_bwd(proj, o_sb, tot, nblk, dmerged, g_head, nb, s, tq)
    dproj = jnp.concatenate([dugv, dq, dk, dv], axis=1)
    grad_x, d_mix = _in_bwd(dproj, dh1, x, g_mix, big["w_in"], 512)
    gw_in = _wgrad(xn, dproj, 640, 512, "wgrad_in", col_shards=4)

    gsmall = {
        "norm_mix_g": d_mix, "gm_v_norm_g": d_gv, "w_spatial": d_wsp, "b_spatial": d_bb[:, :, 0],
        "head_norm_g": jnp.concatenate([d_hga, jnp.sum(d_hgb, axis=0)], axis=1), "norm_cross_g": d_cross,
        "norm_mem_g": d_mem, "norm_ffn_g": d_ffn, "norm_final_g": d_fin,
    }
    slab = lambda a: a.reshape(N_DEV, -1, a.shape[-1])
    gbig = {"w_in": slab(gw_in), "w_out": slab(gw_out), "w_cq": slab(gw_cq), "w_ckv": slab(gw_ckv),
            "w_co": slab(gw_co), "w_ff1": slab(gw_ff1), "w_ff2": slab(gw_ff2)}
    return loss, grad_x, gsmall, gbig


BIG = ("w_in", "w_out", "w_cq", "w_ckv", "w_co", "w_ff1", "w_ff2")
SMALL = ("norm_mix_g", "gm_v_norm_g", "w_spatial", "b_spatial", "head_norm_g", "norm_cross_g", "norm_mem_g",
         "norm_ffn_g", "norm_final_g")


def _other_chips(x, y):
    return ((1 - x, y), (x, 1 - y), (1 - x, 1 - y))


def _local_copies_start(srcs, stages, sems):
    loads = [pltpu.make_async_copy(src, stage, sems.at[w]) for w, (src, stage) in enumerate(zip(srcs, stages))]
    for ld in loads:
        ld.start()
    return loads


def _local_copies_finish(loads, stages, dsts, sems):
    stores = []
    for w, (ld, stage, dst) in enumerate(zip(loads, stages, dsts)):
        ld.wait()
        st = pltpu.make_async_copy(stage, dst, sems.at[w])
        st.start()
        stores.append(st)
    for st in stores:
        st.wait()


def _all_gather_weights(shards):
    n = len(shards)

    def body(*refs):
        ins, outs, stages = refs[:n], refs[n:2 * n], refs[2 * n:3 * n]
        send_sems, recv_sems, ld_sems, st_sems = refs[3 * n:]
        x, y, c = lax.axis_index("x"), lax.axis_index("y"), lax.axis_index("c")
        q = 2 * x + y
        loads = _local_copies_start(ins, stages, ld_sems)
        copies = []
        for w in range(n):
            for k, (px, py) in enumerate(_other_chips(x, y)):
                cp = pltpu.make_async_remote_copy(
                    src_ref=ins[w], dst_ref=outs[w].at[q], send_sem=send_sems.at[3 * w + k],
                    recv_sem=recv_sems.at[3 * w + k], device_id=(px, py, c), device_id_type=MESH)
                cp.start()
                copies.append(cp)
        _local_copies_finish(loads, stages, [outs[w].at[q] for w in range(n)], st_sems)
        for cp in copies:
            cp.wait()

    return pl.pallas_call(
        body, name="all_gather_weights",
        in_specs=[ANY] * n, out_specs=[ANY] * n,
        out_shape=[jax.ShapeDtypeStruct((N_CHIPS,) + a.shape, a.dtype) for a in shards],
        scratch_shapes=[pltpu.VMEM(a.shape, a.dtype) for a in shards] + [
            pltpu.SemaphoreType.DMA((3 * n,)), pltpu.SemaphoreType.DMA((3 * n,)),
            pltpu.SemaphoreType.DMA((n,)), pltpu.SemaphoreType.DMA((n,))],
        compiler_params=_params(24),
    )(*shards)


def _sibling_exchange(slabs):
    n = len(slabs)

    def body(*refs):
        ins, outs = refs[:n], refs[n:2 * n]
        send_sems, recv_sems = refs[2 * n:]
        x, y, c = lax.axis_index("x"), lax.axis_index("y"), lax.axis_index("c")
        copies = []
        for w in range(n):
            for p in range(N_CHIPS):
                cp = pltpu.make_async_remote_copy(
                    src_ref=ins[w].at[2 * p + (1 - c)], dst_ref=outs[w].at[p], send_sem=send_sems.at[4 * w + p],
                    recv_sem=recv_sems.at[4 * w + p], device_id=(x, y, 1 - c), device_id_type=MESH)
                cp.start()
                copies.append(cp)
        for cp in copies:
            cp.wait()

    return pl.pallas_call(
        body, name="grad_sibling_exchange",
        in_specs=[ANY] * n, out_specs=[ANY] * n,
        out_shape=[jax.ShapeDtypeStruct((N_CHIPS,) + a.shape[1:], a.dtype) for a in slabs],
        scratch_shapes=[pltpu.SemaphoreType.DMA((4 * n,)), pltpu.SemaphoreType.DMA((4 * n,))],
    )(*slabs)


def _chip_sum(slabs, recv, c_idx, name):
    _, r, cw = slabs.shape
    tr = min(r, 256)

    def body(c_ref, a_ref, b_ref, o_ref):
        del c_ref
        o_ref[...] = (a_ref[...] + b_ref[...]).astype(BF)

    return pl.pallas_call(
        body, name=name,
        grid_spec=pltpu.PrefetchScalarGridSpec(
            num_scalar_prefetch=1, grid=(N_CHIPS, r // tr),
            in_specs=[pl.BlockSpec((None, tr, cw), lambda p, i, c_ref: (2 * p + c_ref[0], i, 0)),
                      pl.BlockSpec((None, tr, cw), lambda p, i, c_ref: (p, i, 0))],
            out_specs=pl.BlockSpec((None, tr, cw), lambda p, i, c_ref: (p, i, 0))),
        out_shape=jax.ShapeDtypeStruct((N_CHIPS, r, cw), BF),
        compiler_params=_params(32, ("arbitrary", "arbitrary")),
    )(c_idx, slabs, recv)


def _chip_exchange(sums):
    n = len(sums)

    def body(*refs):
        ins, outs, stages = refs[:n], refs[n:2 * n], refs[2 * n:3 * n]
        send_sems, recv_sems, ld_sems, st_sems = refs[3 * n:]
        x, y, c = lax.axis_index("x"), lax.axis_index("y"), lax.axis_index("c")
        q = 2 * x + y
        loads = _local_copies_start([ins[w].at[q] for w in range(n)], stages, ld_sems)
        copies = []
        for w in range(n):
            for k, (px, py) in enumerate(_other_chips(x, y)):
                cp = pltpu.make_async_remote_copy(
                    src_ref=ins[w].at[2 * px + py], dst_ref=outs[w].at[q], send_sem=send_sems.at[3 * w + k],
                    recv_sem=recv_sems.at[3 * w + k], device_id=(px, py, c), device_id_type=MESH)
                cp.start()
                copies.append(cp)
        _local_copies_finish(loads, stages, [outs[w].at[q] for w in range(n)], st_sems)
        for cp in copies:
            cp.wait()

    return pl.pallas_call(
        body, name="grad_chip_exchange",
        in_specs=[ANY] * n, out_specs=[ANY] * n,
        out_shape=[jax.ShapeDtypeStruct(a.shape, a.dtype) for a in sums],
        scratch_shapes=[pltpu.VMEM(a.shape[1:], a.dtype) for a in sums] + [
            pltpu.SemaphoreType.DMA((3 * n,)), pltpu.SemaphoreType.DMA((3 * n,)),
            pltpu.SemaphoreType.DMA((n,)), pltpu.SemaphoreType.DMA((n,))],
        compiler_params=_params(16),
    )(*sums)


def _sum4(parts, name):
    _, r, cw = parts.shape
    tr = min(r, 256)

    def body(a_ref, o_ref):
        o_ref[...] = ((a_ref[0].astype(F32) + a_ref[1].astype(F32)) + a_ref[2].astype(F32)) + a_ref[3].astype(F32)

    return pl.pallas_call(
        body, name=name, grid=(r // tr,),
        in_specs=[pl.BlockSpec((N_CHIPS, tr, cw), lambda i: (0, i, 0))],
        out_specs=pl.BlockSpec((tr, cw), lambda i: (i, 0)),
        out_shape=jax.ShapeDtypeStruct((r, cw), F32),
        compiler_params=_params(32, ("arbitrary",)),
    )(parts)


def _half_exchange(halves):
    n = len(halves)

    def body(*refs):
        ins, outs, stages = refs[:n], refs[n:2 * n], refs[2 * n:3 * n]
        send_sems, recv_sems, ld_sems, st_sems = refs[3 * n:]
        x, y, c = lax.axis_index("x"), lax.axis_index("y"), lax.axis_index("c")
        loads = _local_copies_start(ins, stages, ld_sems)
        copies = []
        for w in range(n):
            cp = pltpu.make_async_remote_copy(
                src_ref=ins[w], dst_ref=outs[w].at[c], send_sem=send_sems.at[w], recv_sem=recv_sems.at[w],
                device_id=(x, y, 1 - c), device_id_type=MESH)
            cp.start()
            copies.append(cp)
        _local_copies_finish(loads, stages, [outs[w].at[c] for w in range(n)], st_sems)
        for cp in copies:
            cp.wait()

    return pl.pallas_call(
        body, name="grad_half_exchange",
        in_specs=[ANY] * n, out_specs=[ANY] * n,
        out_shape=[jax.ShapeDtypeStruct((2,) + a.shape, a.dtype) for a in halves],
        scratch_shapes=[pltpu.VMEM(a.shape, a.dtype) for a in halves] + [
            pltpu.SemaphoreType.DMA((n,)), pltpu.SemaphoreType.DMA((n,)),
            pltpu.SemaphoreType.DMA((n,)), pltpu.SemaphoreType.DMA((n,))],
        compiler_params=_params(24),
    )(*halves)


def _small_all_reduce(packed):
    rows = packed.shape[0]

    def body(in_ref, out_ref, buf, send_sems, recv_sems):
        x, y, c = lax.axis_index("x"), lax.axis_index("y"), lax.axis_index("c")
        me = 4 * x + 2 * y + c
        buf[me] = in_ref[...]
        copies = []
        for k in range(1, N_DEV):
            bx, by, bc = (k >> 2) & 1, (k >> 1) & 1, k & 1
            peer = (x ^ bx, y ^ by, c ^ bc)
            cp = pltpu.make_async_remote_copy(
                src_ref=in_ref, dst_ref=buf.at[me], send_sem=send_sems.at[k - 1], recv_sem=recv_sems.at[k - 1],
                device_id=peer, device_id_type=MESH)
            cp.start()
            copies.append(cp)
        for cp in copies:
            cp.wait()
        acc = buf[0]
        for dev in range(1, N_DEV):
            acc = acc + buf[dev]
        out_ref[...] = acc

    return pl.pallas_call(
        body, name="small_all_reduce",
        in_specs=[pl.BlockSpec(memory_space=pltpu.VMEM)], out_specs=pl.BlockSpec(memory_space=pltpu.VMEM),
        out_shape=jax.ShapeDtypeStruct(packed.shape, F32),
        scratch_shapes=[pltpu.VMEM((N_DEV, rows, 128), F32), pltpu.SemaphoreType.DMA((N_DEV - 1,)),
                        pltpu.SemaphoreType.DMA((N_DEV - 1,))],
        compiler_params=_params(16),
    )(packed)


def _adamw(g, w, m, v, name):
    r, cw = g.shape
    tr = r if r <= 1024 else 256
    assert r % tr == 0

    def body(g_ref, w_ref, m_ref, v_ref, d_ref, nm_ref, nv_ref):
        gv = g_ref[...]
        nm = ADAM_B1 * m_ref[...] + (1.0 - ADAM_B1) * gv
        nv = ADAM_B2 * v_ref[...] + (1.0 - ADAM_B2) * (gv * gv)
        m_hat = nm / (1.0 - ADAM_B1 ** ADAM_STEP)
        v_hat = nv / (1.0 - ADAM_B2 ** ADAM_STEP)
        d_ref[...] = -ADAM_LR * (m_hat / (jnp.sqrt(v_hat) + ADAM_EPS) + ADAM_WD * w_ref[...])
        nm_ref[...] = nm
        nv_ref[...] = nv

    spec = pl.BlockSpec((tr, cw), lambda i: (i, 0))
    return pl.pallas_call(
        body, name=name, grid=(r // tr,),
        in_specs=[spec] * 4, out_specs=[spec] * 3,
        out_shape=[jax.ShapeDtypeStruct((r, cw), F32)] * 3,
        compiler_params=_params(32, ("arbitrary",)),
    )(g, w, m, v)


def _small_params(args):
    small = {k: args[k].reshape(1, -1) for k in SMALL}
    small["w_spatial"] = args["w_spatial"][0]
    small["b_spatial"] = args["b_spatial"][0]
    return small


def _pack(parts, rows):
    flat = jnp.concatenate([p.reshape(-1).astype(F32) for p in parts])
    return jnp.pad(flat, (0, rows * 128 - flat.shape[0])).reshape(rows, 128)


def _unpack(packed, shapes):
    flat = packed.reshape(-1)
    out, off = [], 0
    for shp in shapes:
        size = math.prod(shp)
        out.append(flat[off:off + size].reshape(shp))
        off += size
    return out


def kernel(x, mem, norm_mix_g, w_in, gm_v_norm_g, w_spatial, b_spatial, head_norm_g, w_out, norm_cross_g, norm_mem_g, w_cq, w_ckv, w_co, norm_ffn_g, w_ff1, w_ff2, norm_final_g, loss_target, m_norm_mix_g, m_w_in, m_gm_v_norm_g, m_w_spatial, m_b_spatial, m_head_norm_g, m_w_out, m_norm_cross_g, m_norm_mem_g, m_w_cq, m_w_ckv, m_w_co, m_norm_ffn_g, m_w_ff1, m_w_ff2, m_norm_final_g, v_norm_mix_g, v_w_in, v_gm_v_norm_g, v_w_spatial, v_b_spatial, v_head_norm_g, v_w_out, v_norm_cross_g, v_norm_mem_g, v_w_cq, v_w_ckv, v_w_co, v_norm_ffn_g, v_w_ff1, v_w_ff2, v_norm_final_g):
    args = dict(locals())
    d = D_MODEL
    nb, s, _ = x.shape
    c_idx = lax.axis_index("c").astype(jnp.int32).reshape(1)

    shards = [args[k][0].astype(BF) for k in BIG]
    full = dict(zip(BIG, _all_gather_weights(shards)))
    big = {
        "w_in": full["w_in"].transpose(1, 0, 2).reshape(d, IN_COLS),
        "w_out": full["w_out"].reshape(d, d),
        "w_cq": full["w_cq"].reshape(d, d),
        "w_ckv": full["w_ckv"].transpose(1, 0, 2).reshape(d, 2 * d),
        "w_co": full["w_co"].reshape(d, d),
        "w_ff1": full["w_ff1"].transpose(1, 0, 2).reshape(d, D_FF),
        "w_ff2": full["w_ff2"].reshape(D_FF, d),
    }
    small = _small_params(args)

    loss, grad_x, gsmall, gbig = _local_step(
        x.reshape(nb * s, d), mem.reshape(nb * N_MEM, d), loss_target.reshape(nb * s, d), small, big, nb, s)

    slabs = [gbig[k] for k in BIG]
    recv = _sibling_exchange(slabs)
    sums = [_chip_sum(a, b, c_idx, "chip_sum_" + k) for k, a, b in zip(BIG, slabs, recv)]
    parts = _chip_exchange(sums)
    halves = [_sum4(a, "sum4_" + k) for k, a in zip(BIG, parts)]
    both = _half_exchange(halves)

    out = {"grad_x": grad_x.reshape(nb, s, d)}
    for k, g2 in zip(BIG, both):
        shp = args[k].shape
        g = g2.reshape(shp[1], shp[2])
        dl, nm, nv = _adamw(g, args[k][0], args["m_" + k][0], args["v_" + k][0], "adamw_" + k)
        out["grad_" + k], out["delta_" + k], out["new_m_" + k], out["new_v_" + k] = (
            a.reshape(shp) for a in (g, dl, nm, nv))

    shapes = [args[k].shape for k in SMALL]
    n_small = sum(math.prod(sh) for sh in shapes)
    rows = -(-(n_small + 1) // 1024) * 8
    reduced = _small_all_reduce(_pack([gsmall[k] for k in SMALL] + [loss], rows))
    dl, nm, nv = _adamw(reduced, _pack([args[k] for k in SMALL], rows), _pack([args["m_" + k] for k in SMALL], rows),
                        _pack([args["v_" + k] for k in SMALL], rows), "adamw_small")
    for name, arr in (("grad_", reduced), ("delta_", dl), ("new_m_", nm), ("new_v_", nv)):
        for k, a in zip(SMALL, _unpack(arr, shapes)):
            out[name + k] = a
    out["loss"] = reduced.reshape(-1)[n_small]

    names = ["norm_mix_g", "w_in", "gm_v_norm_g", "w_spatial", "b_spatial", "head_norm_g", "w_out", "norm_cross_g",
             "norm_mem_g", "w_cq", "w_ckv", "w_co", "norm_ffn_g", "w_ff1", "w_ff2", "norm_final_g"]
    return (out["loss"], out["grad_x"], *[out["grad_" + k] for k in names], *[out["delta_" + k] for k in names],
            *[out["new_m_" + k] for k in names], *[out["new_v_" + k] for k in names])
```

```python
import functools
import math

import jax
import jax.numpy as jnp
from jax import lax
from jax.experimental import pallas as pl
from jax.experimental.pallas import tpu as pltpu

F32 = jnp.float32
BF = jnp.bfloat16

EPS = 1e-6
D_MODEL = 1024
CHUNK = 128
GM_GROUPS = 4
GM_WIDTH = 512
SB_WIDTH = 512
HEAD_LANES = 64
SB_SCALE = 0.125
SB_SKIP = -104.0
X_HEADS = 4
X_HEAD_DIM = 256
N_MEM = 256
D_FF = 4096
IN_COLS = 2560
N_CHIPS = 4
N_DEV = 8

ADAM_LR = 0.001
ADAM_B1 = 0.9
ADAM_B2 = 0.999
ADAM_EPS = 1e-08
ADAM_WD = 0.01
ADAM_STEP = 10

V7X_VMEM_BYTES = 64 * 1024 * 1024
MESH = pl.DeviceIdType.MESH
ANY = pl.BlockSpec(memory_space=pl.ANY)

GELU_C = math.sqrt(2.0 / math.pi)
GELU_A = 0.044715


def _params(vmem_mb, sem=None):
    assert vmem_mb * 1024 * 1024 <= V7X_VMEM_BYTES
    return pltpu.CompilerParams(vmem_limit_bytes=vmem_mb * 1024 * 1024, dimension_semantics=sem)


def _dot(a, b):
    return jnp.dot(a, b, preferred_element_type=F32)


def _dot_bt(a, b):
    return lax.dot_general(a, b, (((1,), (1,)), ((), ())), preferred_element_type=F32)


def _dot_at(a, b):
    return lax.dot_general(a, b, (((0,), (0,)), ((), ())), preferred_element_type=F32)


def _gelu(x):
    t = jnp.tanh(GELU_C * (x + GELU_A * x * x * x))
    return 0.5 * x * (1.0 + t)


def _gelu_grad(x):
    x2 = x * x
    t = jnp.tanh(GELU_C * (x + GELU_A * x2 * x))
    return 0.5 * (1.0 + t) + 0.5 * x * (1.0 - t * t) * (GELU_C * (1.0 + 3.0 * GELU_A * x2))


def _rs(x):
    return lax.rsqrt(jnp.mean(x * x, axis=-1, keepdims=True) + EPS)


def _rms_bwd(dxn, xhat, r, g):
    dxh = dxn * g
    dx = r * (dxh - xhat * jnp.mean(dxh * xhat, axis=-1, keepdims=True))
    return dx, dxn * xhat


def _norm_matmul(x, g, w, tm, name):
    t, d = x.shape
    n = w.shape[1]
    tm = min(tm, t)

    def body(x_ref, g_ref, w_ref, out_ref, xn_ref):
        xv = x_ref[...]
        xn = (xv * _rs(xv) * g_ref[...]).astype(BF)
        xn_ref[...] = xn
        out_ref[...] = _dot(xn, w_ref[...]).astype(out_ref.dtype)

    return pl.pallas_call(
        body, name=name, grid=(t // tm,),
        in_specs=[pl.BlockSpec((tm, d), lambda i: (i, 0)), pl.BlockSpec((1, d), lambda i: (0, 0)),
                  pl.BlockSpec((d, n), lambda i: (0, 0))],
        out_specs=[pl.BlockSpec((tm, n), lambda i: (i, 0)), pl.BlockSpec((tm, d), lambda i: (i, 0))],
        out_shape=[jax.ShapeDtypeStruct((t, n), BF), jax.ShapeDtypeStruct((t, d), BF)],
        compiler_params=_params(48, ("arbitrary",)),
    )(x, g, w)


def _wgrad(a, g, tn, tk, name, square_a=False, col_shards=1):
    t, m = a.shape
    n = g.shape[1]
    tk = min(tk, t)
    tm = min(m, 1024)
    ns = n // col_shards
    assert ns % tn == 0 and m % tm == 0
    per = ns // tn
    nk = t // tk

    def body(a_ref, g_ref, o_ref):
        k = pl.program_id(2)

        @pl.when(k == 0)
        def _():
            o_ref[...] = jnp.zeros_like(o_ref)

        av = a_ref[...]
        if square_a:
            af = av.astype(F32)
            av = af * af
        o_ref[...] += _dot_at(av.astype(BF), g_ref[...].astype(BF))

    return pl.pallas_call(
        body, name=name, grid=(m // tm, n // tn, nk),
        in_specs=[pl.BlockSpec((tk, tm), lambda i, j, k: (k, i)), pl.BlockSpec((tk, tn), lambda i, j, k: (k, j))],
        out_specs=pl.BlockSpec((None, tm, tn), lambda i, j, k: (j // per, i, j % per)),
        out_shape=jax.ShapeDtypeStruct((col_shards, m, ns), F32),
        compiler_params=_params(48, ("arbitrary", "arbitrary", "arbitrary")),
    )(a, g)


def _matmul_bt(a, w, tm, name):
    t, n = a.shape
    k = w.shape[0]
    tm = min(tm, t)

    def body(a_ref, w_ref, o_ref):
        o_ref[...] = _dot_bt(a_ref[...].astype(BF), w_ref[...]).astype(o_ref.dtype)

    return pl.pallas_call(
        body, name=name, grid=(t // tm,),
        in_specs=[pl.BlockSpec((tm, n), lambda i: (i, 0)), pl.BlockSpec((k, n), lambda i: (0, 0))],
        out_specs=pl.BlockSpec((tm, k), lambda i: (i, 0)),
        out_shape=jax.ShapeDtypeStruct((t, k), BF),
        compiler_params=_params(32, ("arbitrary",)),
    )(a, w)


def _gmlp_fwd(proj, gg, wt, bb, hg, tm):
    t = proj.shape[0]
    tm = min(tm, t)

    def body(u_ref, v_ref, gg_ref, wt_ref, bb_ref, hg_ref, out_ref):
        for cc in range(tm // CHUNK):
            rows = slice(cc * CHUNK, (cc + 1) * CHUNK)
            for g in range(GM_GROUPS):
                cols = slice(g * 128, (g + 1) * 128)
                u = _gelu(u_ref[rows, cols].astype(F32))
                gv = _gelu(v_ref[rows, cols].astype(F32))
                vn = gv * _rs(gv) * gg_ref[:, cols]
                mixed = _dot(wt_ref[g], vn.astype(BF)) + bb_ref[g]
                a = u * mixed
                out_ref[rows, cols] = (a * _rs(a) * hg_ref[:, cols]).astype(BF)

    return pl.pallas_call(
        body, name="gmlp_fwd", grid=(t // tm,),
        in_specs=[pl.BlockSpec((tm, 512), lambda i: (i, 0)), pl.BlockSpec((tm, 512), lambda i: (i, 1)),
                  pl.BlockSpec((1, 512), lambda i: (0, 0)), pl.BlockSpec((4, 128, 128), lambda i: (0, 0, 0)),
                  pl.BlockSpec((4, 128, 128), lambda i: (0, 0, 0)), pl.BlockSpec((1, 512), lambda i: (0, 0))],
        out_specs=pl.BlockSpec((tm, 512), lambda i: (i, 0)),
        out_shape=jax.ShapeDtypeStruct((t, 512), BF),
        compiler_params=_params(32, ("arbitrary",)),
    )(proj, proj, gg, wt, bb, hg)


def _gmlp_bwd(proj, dmerged, gg, wt, wtt, bb, hg, tm):
    t = proj.shape[0]
    tm = min(tm, t)
    nsteps = t // tm

    def body(u_ref, v_ref, dm_ref, gg_ref, wt_ref, wtt_ref, bb_ref, hg_ref,
             dp_ref, dw_ref, db_ref, dgg_ref, dhg_ref):
        i = pl.program_id(0)

        @pl.when(i == 0)
        def _():
            dw_ref[...] = jnp.zeros_like(dw_ref)
            db_ref[...] = jnp.zeros_like(db_ref)
            dgg_ref[...] = jnp.zeros_like(dgg_ref)
            dhg_ref[...] = jnp.zeros_like(dhg_ref)

        for cc in range(tm // CHUNK):
            rows = slice(cc * CHUNK, (cc + 1) * CHUNK)
            for g in range(GM_GROUPS):
                cols = slice(g * 128, (g + 1) * 128)
                up = u_ref[rows, cols].astype(F32)
                gp = v_ref[rows, cols].astype(F32)
                u = _gelu(up)
                gv = _gelu(gp)
                rv = _rs(gv)
                gvh = gv * rv
                ggv = gg_ref[:, cols]
                vnb = (gvh * ggv).astype(BF)
                mixed = _dot(wt_ref[g], vnb) + bb_ref[g]
                a = u * mixed
                ra = _rs(a)
                ah = a * ra
                dm = dm_ref[rows, cols].astype(F32)
                dhg_ref[:, cols] += jnp.sum(dm * ah, axis=0, keepdims=True)
                dah = dm * hg_ref[:, cols]
                da = ra * (dah - ah * jnp.mean(dah * ah, axis=-1, keepdims=True))
                du = da * mixed
                dmix = da * u
                db_ref[g] += dmix
                dmb = dmix.astype(BF)
                dw_ref[g] += _dot_bt(dmb, vnb)
                dvn = _dot(wtt_ref[g], dmb)
                dgg_ref[:, cols] += jnp.sum(dvn * gvh, axis=0, keepdims=True)
                dgh = dvn * ggv
                dgv = rv * (dgh - gvh * jnp.mean(dgh * gvh, axis=-1, keepdims=True))
                dp_ref[rows, cols] = (du * _gelu_grad(up)).astype(BF)
                dp_ref[rows, 512 + g * 128:512 + (g + 1) * 128] = (dgv * _gelu_grad(gp)).astype(BF)

        @pl.when(i == nsteps - 1)
        def _():
            r = lax.broadcasted_iota(jnp.int32, (CHUNK, CHUNK), 0)
            c = lax.broadcasted_iota(jnp.int32, (CHUNK, CHUNK), 1)
            for g in range(GM_GROUPS):
                dw_ref[g] = jnp.where(c <= r, dw_ref[g], 0.0)
                db_ref[g] = jnp.broadcast_to(jnp.sum(db_ref[g], axis=-1, keepdims=True), (CHUNK, CHUNK))

    small = lambda shape: pl.BlockSpec(shape, lambda i: (0,) * len(shape))
    return pl.pallas_call(
        body, name="gmlp_bwd", grid=(nsteps,),
        in_specs=[pl.BlockSpec((tm, 512), lambda i: (i, 0)), pl.BlockSpec((tm, 512), lambda i: (i, 1)),
                  pl.BlockSpec((tm, 512), lambda i: (i, 0)), small((1, 512)), small((4, 128, 128)),
                  small((4, 128, 128)), small((4, 128, 128)), small((1, 512))],
        out_specs=[pl.BlockSpec((tm, 1024), lambda i: (i, 0)), small((4, 128, 128)), small((4, 128, 128)),
                   small((1, 512)), small((1, 512))],
        out_shape=[jax.ShapeDtypeStruct((t, 1024), BF), jax.ShapeDtypeStruct((4, 128, 128), F32),
                   jax.ShapeDtypeStruct((4, 128, 128), F32), jax.ShapeDtypeStruct((1, 512), F32),
                   jax.ShapeDtypeStruct((1, 512), F32)],
        compiler_params=_params(32, ("arbitrary",)),
    )(proj, proj, dmerged, gg, wt, wtt, bb, hg)


def _other_chips(x, y):
    return ((1 - x, y), (x, 1 - y), (1 - x, 1 - y))


class _GatherExchange:
    def __init__(self, shards):
        n = len(shards)
        self.n = n
        self.in_arrays = list(shards)
        self.out_shape = [jax.ShapeDtypeStruct((N_CHIPS,) + a.shape, a.dtype) for a in shards]
        self.scratch_shapes = [pltpu.VMEM(a.shape, a.dtype) for a in shards] + [
            pltpu.SemaphoreType.DMA((3 * n,)), pltpu.SemaphoreType.DMA((3 * n,)),
            pltpu.SemaphoreType.DMA((n,)), pltpu.SemaphoreType.DMA((n,))]

    def _copies(self, ins, outs, scr):
        n = self.n
        stages, (send_sems, recv_sems, ld_sems, st_sems) = scr[:n], scr[n:]
        x, y, c = lax.axis_index("x"), lax.axis_index("y"), lax.axis_index("c")
        q = 2 * x + y
        loads = [pltpu.make_async_copy(ins[w], stages[w], ld_sems.at[w]) for w in range(n)]
        stores = [pltpu.make_async_copy(stages[w], outs[w].at[q], st_sems.at[w]) for w in range(n)]
        remote = [pltpu.make_async_remote_copy(
            src_ref=ins[w], dst_ref=outs[w].at[q], send_sem=send_sems.at[3 * w + k],
            recv_sem=recv_sems.at[3 * w + k], device_id=(px, py, c), device_id_type=MESH)
            for w in range(n) for k, (px, py) in enumerate(_other_chips(x, y))]
        return loads, stores, remote

    def start(self, ins, outs, scr):
        loads, stores, remote = self._copies(ins, outs, scr)
        for cp in loads + remote:
            cp.start()
        for ld, st in zip(loads, stores):
            ld.wait()
            st.start()

    def finish(self, ins, outs, scr):
        _, stores, remote = self._copies(ins, outs, scr)
        for cp in stores + remote:
            cp.wait()


class _ChipExchange:
    def __init__(self, sums):
        n = len(sums)
        self.n = n
        self.in_arrays = list(sums)
        self.out_shape = [jax.ShapeDtypeStruct(a.shape, a.dtype) for a in sums]
        self.scratch_shapes = [pltpu.SemaphoreType.DMA((3 * n,)), pltpu.SemaphoreType.DMA((3 * n,))]

    def _copies(self, ins, outs, scr):
        send_sems, recv_sems = scr
        x, y, c = lax.axis_index("x"), lax.axis_index("y"), lax.axis_index("c")
        q = 2 * x + y
        return [pltpu.make_async_remote_copy(
            src_ref=ins[w].at[2 * px + py], dst_ref=outs[w].at[q], send_sem=send_sems.at[3 * w + k],
            recv_sem=recv_sems.at[3 * w + k], device_id=(px, py, c), device_id_type=MESH)
            for w in range(self.n) for k, (px, py) in enumerate(_other_chips(x, y))]

    def start(self, ins, outs, scr):
        for cp in self._copies(ins, outs, scr):
            cp.start()

    def finish(self, ins, outs, scr):
        for cp in self._copies(ins, outs, scr):
            cp.wait()


def _run_exchange(ex, name):
    n_in, n_out = len(ex.in_arrays), len(ex.out_shape)

    def body(*refs):
        ins, outs, scr = refs[:n_in], refs[n_in:n_in + n_out], refs[n_in + n_out:]
        ex.start(ins, outs, scr)
        ex.finish(ins, outs, scr)

    return pl.pallas_call(
        body, name=name, in_specs=[ANY] * n_in, out_specs=[ANY] * n_out, out_shape=ex.out_shape,
        scratch_shapes=ex.scratch_shapes, compiler_params=_params(24),
    )(*ex.in_arrays)


class _NoExchange:
    in_arrays, out_shape, scratch_shapes = (), (), ()

    def start(self, ins, outs, scr):
        pass

    def finish(self, ins, outs, scr):
        pass


def _log_sig_pair(z):
    sp = jnp.log(1.0 + jnp.exp(-jnp.abs(z)))
    return -(jnp.maximum(z, 0.0) + sp), jnp.minimum(z, 0.0) - sp


def _hi_lo(m):
    hi = m.astype(BF)
    return hi, (m - hi.astype(F32)).astype(BF)


def _head_sums(x, h0):
    s0 = jnp.sum(jnp.where(h0, x, 0.0), axis=-1, keepdims=True)
    s1 = jnp.sum(jnp.where(h0, 0.0, x), axis=-1, keepdims=True)
    return jnp.where(h0, s0, s1)


def _sb_setup(q_ref, tq):
    lane = lax.broadcasted_iota(jnp.int32, (tq, 128), 1)
    h0 = lane < HEAD_LANES
    qs = q_ref[...] * SB_SCALE
    zero = jnp.zeros_like(qs)
    qst = jnp.concatenate([jnp.where(h0, qs, zero), jnp.where(h0, zero, qs)], axis=0)
    r = lax.broadcasted_iota(jnp.int32, (2 * tq, tq), 0)
    c = lax.broadcasted_iota(jnp.int32, (2 * tq, tq), 1)
    causal = c < jnp.where(r >= tq, r - tq, r)
    return h0, qst, causal


def _tri(tq, op):
    return op(lax.broadcasted_iota(jnp.int32, (tq, tq), 0), lax.broadcasted_iota(jnp.int32, (tq, tq), 1)).astype(BF)


def _sb_fwd(proj, hg, nb, s, tq, ride=None):
    t = nb * s
    tq = min(tq, s)
    nq = s // tq
    ride = ride or _NoExchange()
    n_in, n_out = len(ride.in_arrays), len(ride.out_shape)

    def body(*refs):
        q_ref, k_ref, v_ref, hg_ref = refs[:4]
        ride_in = refs[4:4 + n_in]
        o_ref, tot_ref, mb_ref, nblk_ref = refs[4 + n_in:8 + n_in]
        ride_out = refs[8 + n_in:8 + n_in + n_out]
        acc, cr = refs[8 + n_in + n_out:10 + n_in + n_out]
        ride_scr = refs[10 + n_in + n_out:]
        i = pl.program_id(2)
        step_id = (pl.program_id(0) * 4 + pl.program_id(1)) * nq + i

        @pl.when(step_id == 0)
        def _():
            ride.start(ride_in, ride_out, ride_scr)

        h0, qst, causal = _sb_setup(q_ref, tq)
        tri_gt = _tri(tq, lambda r, c: r > c)
        acc[...] = jnp.zeros_like(acc)
        cr[...] = jnp.zeros_like(cr)

        def block(j, masked):
            start = pl.multiple_of(j * tq, tq)
            kj = k_ref[pl.ds(start, tq), :]
            vj = v_ref[pl.ds(start, tq), :]
            m, l = _log_sig_pair(_dot_bt(qst, kj))
            if masked:
                m = jnp.where(causal, m, 0.0)
            mh, ml = _hi_lo(m)
            a = jnp.exp(l + (_dot(mh, tri_gt) + _dot(ml, tri_gt)) + cr[...])
            if masked:
                a = jnp.where(causal, a, 0.0)
            acc[...] += _dot(a.astype(BF), vj)
            cnew = cr[...] + jnp.sum(m, axis=-1, keepdims=True)
            cr[...] = cnew
            return jnp.max(cnew)

        def cond(carry):
            return jnp.logical_and(carry[0] < i, carry[1] > SB_SKIP)

        def step(carry):
            return carry[0] + 1, block(i - 1 - carry[0], False)

        walked, _ = lax.while_loop(cond, step, (jnp.int32(0), block(i, True)))

        o = jnp.where(h0, acc[0:tq, :], acc[tq:2 * tq, :])
        o_ref[...] = o
        tot_ref[...] = jnp.where(h0, cr[0:tq, :], cr[tq:2 * tq, :])
        ro = lax.rsqrt(_head_sums(o * o, h0) * (1.0 / HEAD_LANES) + EPS)
        mb_ref[...] = (o * ro * hg_ref[...]).astype(BF)
        nblk_ref[...] = jnp.full((8, 128), walked.astype(F32))

        @pl.when(step_id == nb * 4 * nq - 1)
        def _():
            ride.finish(ride_in, ride_out, ride_scr)

    blk = lambda col0: pl.BlockSpec((tq, 128), lambda b, hp, i: (b * nq + i, col0 + hp))
    seq = lambda col0: pl.BlockSpec((s, 128), lambda b, hp, i: (b, col0 + hp))
    outs = pl.pallas_call(
        body, name="sb_fwd", grid=(nb, 4, nq),
        in_specs=[blk(8), seq(12), seq(16), pl.BlockSpec((1, 128), lambda b, hp, i: (0, 4 + hp))] + [ANY] * n_in,
        out_specs=[blk(0), blk(0), blk(0),
                   pl.BlockSpec((None, None, 8, 128), lambda b, hp, i: (b, hp, i, 0))] + [ANY] * n_out,
        out_shape=[jax.ShapeDtypeStruct((t, 512), F32), jax.ShapeDtypeStruct((t, 512), F32),
                   jax.ShapeDtypeStruct((t, 512), BF), jax.ShapeDtypeStruct((nb, 4, nq * 8, 128), F32)]
        + list(ride.out_shape),
        scratch_shapes=[pltpu.VMEM((2 * tq, 128), F32), pltpu.VMEM((2 * tq, 1), F32)] + list(ride.scratch_shapes),
        compiler_params=_params(40, ("arbitrary", "arbitrary", "arbitrary")),
    )(proj, proj, proj, hg, *ride.in_arrays)
    return outs[0], outs[1], outs[2], outs[3], outs[4:]


def _sb_bwd(proj, o_sb, tot, nblk, dmerged, hg, nb, s, tq, ride=None):
    t = nb * s
    tq = min(tq, s)
    nq = s // tq
    ride = ride or _NoExchange()
    n_in, n_out = len(ride.in_arrays), len(ride.out_shape)

    def body(*refs):
        q_ref, k_ref, v_ref, o_ref, tot_ref, nblk_ref, dm_ref, hg_ref = refs[:8]
        ride_in = refs[8:8 + n_in]
        dq_ref, dk_ref, dv_ref, dhg_ref = refs[8 + n_in:12 + n_in]
        ride_out = refs[12 + n_in:12 + n_in + n_out]
        dk_acc, dv_acc, dq_acc, cm, cg = refs[12 + n_in + n_out:17 + n_in + n_out]
        ride_scr = refs[17 + n_in + n_out:]
        i = pl.program_id(2)
        step_id = (pl.program_id(0) * 4 + pl.program_id(1)) * nq + i

        @pl.when(step_id == 0)
        def _():
            ride.start(ride_in, ride_out, ride_scr)

        h0, qst, causal = _sb_setup(q_ref, tq)
        tri_le = _tri(tq, lambda r, c: r <= c)
        tri_lt = _tri(tq, lambda r, c: r < c)

        @pl.when(i == 0)
        def _():
            dk_acc[...] = jnp.zeros_like(dk_acc)
            dv_acc[...] = jnp.zeros_like(dv_acc)
            dhg_ref[...] = jnp.zeros_like(dhg_ref)

        for ref in (dq_acc, cm, cg):
            ref[...] = jnp.zeros_like(ref)

        o = o_ref[...]
        ro = lax.rsqrt(_head_sums(o * o, h0) * (1.0 / HEAD_LANES) + EPS)
        oh = o * ro
        dm = dm_ref[...].astype(F32)
        dhg_ref[...] += jnp.sum(dm * oh, axis=0, keepdims=True)
        doh = dm * hg_ref[...]
        do = (ro * (doh - oh * (_head_sums(doh * oh, h0) * (1.0 / HEAD_LANES)))).astype(BF)
        zb = jnp.zeros_like(do)
        dost = jnp.concatenate([jnp.where(h0, do, zb), jnp.where(h0, zb, do)], axis=0)
        tots = jnp.concatenate([tot_ref[:, 0:1], tot_ref[:, HEAD_LANES:HEAD_LANES + 1]], axis=0)

        def block(j, masked):
            start = pl.multiple_of(j * tq, tq)
            kj = k_ref[pl.ds(start, tq), :]
            vj = v_ref[pl.ds(start, tq), :]
            m, l = _log_sig_pair(_dot_bt(qst, kj))
            if masked:
                m = jnp.where(causal, m, 0.0)
            mh, ml = _hi_lo(m)
            a = jnp.exp(l + (tots - cm[...] - (_dot(mh, tri_le) + _dot(ml, tri_le))))
            if masked:
                a = jnp.where(causal, a, 0.0)
            gm = a * _dot_bt(dost, vj)
            pp = cg[...] + _dot(gm.astype(BF), tri_lt)
            dz = gm - jnp.exp(l) * (gm + pp)
            if masked:
                dz = jnp.where(causal, dz, 0.0)
            dzb = dz.astype(BF)
            dq_acc[...] += _dot(dzb, kj)
            dk_acc[pl.ds(start, tq), :] += _dot_at(dzb, qst)
            dv_acc[pl.ds(start, tq), :] += _dot_at(a.astype(BF), dost)
            cm[...] += jnp.sum(m, axis=-1, keepdims=True)
            cg[...] += jnp.sum(gm, axis=-1, keepdims=True)

        def step(j, carry):
            block(j, False)
            return carry

        walked = jnp.clip(jnp.max(nblk_ref[...]).astype(jnp.int32), 0, i)
        lax.fori_loop(i - walked, i, step, 0)
        block(i, True)

        dq_ref[...] = (jnp.where(h0, dq_acc[0:tq, :], dq_acc[tq:2 * tq, :]) * SB_SCALE).astype(BF)

        @pl.when(i == nq - 1)
        def _():
            dk_ref[...] = dk_acc[...].astype(BF)
            dv_ref[...] = dv_acc[...].astype(BF)

        @pl.when(step_id == nb * 4 * nq - 1)
        def _():
            ride.finish(ride_in, ride_out, ride_scr)

    blk = lambda col0: pl.BlockSpec((tq, 128), lambda b, hp, i: (b * nq + i, col0 + hp))
    seq = lambda col0: pl.BlockSpec((s, 128), lambda b, hp, i: (b, col0 + hp))
    outs = pl.pallas_call(
        body, name="sb_bwd", grid=(nb, 4, nq),
        in_specs=[blk(8), seq(12), seq(16), blk(0), blk(0),
                  pl.BlockSpec((None, None, 8, 128), lambda b, hp, i: (b, hp, i, 0)), blk(4),
                  pl.BlockSpec((1, 128), lambda b, hp, i: (0, 4 + hp))] + [ANY] * n_in,
        out_specs=[blk(0), seq(0), seq(0), pl.BlockSpec((None, 1, 128), lambda b, hp, i: (b, 0, hp))]
        + [ANY] * n_out,
        out_shape=[jax.ShapeDtypeStruct((t, 512), BF), jax.ShapeDtypeStruct((t, 512), BF),
                   jax.ShapeDtypeStruct((t, 512), BF), jax.ShapeDtypeStruct((nb, 1, 512), F32)]
        + list(ride.out_shape),
        scratch_shapes=[pltpu.VMEM((s, 128), F32), pltpu.VMEM((s, 128), F32), pltpu.VMEM((2 * tq, 128), F32),
                        pltpu.VMEM((2 * tq, 1), F32), pltpu.VMEM((2 * tq, 1), F32)] + list(ride.scratch_shapes),
        compiler_params=_params(40, ("arbitrary", "arbitrary", "arbitrary")),
    )(proj, proj, proj, o_sb, tot, nblk, dmerged, hg, *ride.in_arrays)
    return outs[0], outs[1], outs[2], outs[3], outs[4:]


def _softmax_rows(sc):
    e = jnp.exp(sc - jnp.max(sc, axis=-1, keepdims=True))
    return e / jnp.sum(e, axis=-1, keepdims=True)


def _mix_cross_fwd(x, ma, mb, w_out, gc, w_cq, kv, w_co, s, tm):
    t, d = x.shape
    tm = min(tm, s)
    per = s // tm
    inv = 1.0 / math.sqrt(X_HEAD_DIM)

    def body(x_ref, ma_ref, mb_ref, wo_ref, gc_ref, wq_ref, kv_ref, wc_ref, h1_ref, h2_ref, hn_ref, oc_ref):
        h1 = x_ref[...] + _dot(ma_ref[...], wo_ref[0:512, :]) + _dot(mb_ref[...], wo_ref[512:1024, :])
        h1_ref[...] = h1
        hn = (h1 * _rs(h1) * gc_ref[...]).astype(BF)
        hn_ref[...] = hn
        qc = _dot(hn, wq_ref[...]).astype(BF)
        for h in range(X_HEADS):
            cols = slice(h * X_HEAD_DIM, (h + 1) * X_HEAD_DIM)
            kh = kv_ref[:, h * X_HEAD_DIM:(h + 1) * X_HEAD_DIM]
            vh = kv_ref[:, d + h * X_HEAD_DIM:d + (h + 1) * X_HEAD_DIM]
            p = _softmax_rows(_dot_bt(qc[:, cols], kh) * inv)
            oc_ref[:, cols] = _dot(p.astype(BF), vh).astype(BF)
        h2_ref[...] = h1 + _dot(oc_ref[...], wc_ref[...])

    row = lambda width: pl.BlockSpec((tm, width), lambda i: (i, 0))
    full = lambda a, b: pl.BlockSpec((a, b), lambda i: (0, 0))
    return pl.pallas_call(
        body, name="mix_cross_fwd", grid=(t // tm,),
        in_specs=[row(d), row(512), row(512), full(d, d), full(1, d), full(d, d),
                  pl.BlockSpec((N_MEM, 2 * d), lambda i: (i // per, 0)), full(d, d)],
        out_specs=[row(d), row(d), row(d), row(d)],
        out_shape=[jax.ShapeDtypeStruct((t, d), F32), jax.ShapeDtypeStruct((t, d), F32),
                   jax.ShapeDtypeStruct((t, d), BF), jax.ShapeDtypeStruct((t, d), BF)],
        compiler_params=_params(48, ("arbitrary",)),
    )(x, ma, mb, w_out, gc, w_cq, kv, w_co)


def _cross_bwd(dh2, h1, gc, w_cq, kv, w_co, s, tm):
    t, d = dh2.shape
    tm = min(tm, s)
    per = s // tm
    nb = t // s
    inv = 1.0 / math.sqrt(X_HEAD_DIM)

    def body(dh2_ref, h1_ref, gc_ref, wq_ref, kv_ref, wc_ref, dh1_ref, dqc_ref, dkv_ref, dgc_ref):
        i = pl.program_id(0)

        @pl.when(i == 0)
        def _():
            dgc_ref[...] = jnp.zeros_like(dgc_ref)

        @pl.when(i % per == 0)
        def _():
            dkv_ref[...] = jnp.zeros_like(dkv_ref)

        dh2 = dh2_ref[...]
        h1 = h1_ref[...]
        r = _rs(h1)
        h1h = h1 * r
        gcv = gc_ref[...]
        hn = (h1h * gcv).astype(BF)
        qc = _dot(hn, wq_ref[...]).astype(BF)
        do = _dot_bt(dh2.astype(BF), wc_ref[...]).astype(BF)
        for h in range(X_HEADS):
            cols = slice(h * X_HEAD_DIM, (h + 1) * X_HEAD_DIM)
            vcols = slice(d + h * X_HEAD_DIM, d + (h + 1) * X_HEAD_DIM)
            kh = kv_ref[:, cols]
            vh = kv_ref[:, vcols]
            p = _softmax_rows(_dot_bt(qc[:, cols], kh) * inv)
            dp = _dot_bt(do[:, cols], vh)
            ds = (p * (dp - jnp.sum(dp * p, axis=-1, keepdims=True)) * inv).astype(BF)
            dqc_ref[:, cols] = _dot(ds, kh).astype(BF)
            dkv_ref[:, cols] += _dot_at(ds, qc[:, cols])
            dkv_ref[:, vcols] += _dot_at(p.astype(BF), do[:, cols])
        dhn = _dot_bt(dqc_ref[...], wq_ref[...])
        dx, dg = _rms_bwd(dhn, h1h, r, gcv)
        dh1_ref[...] = dh2 + dx
        dgc_ref[...] += jnp.sum(dg, axis=0, keepdims=True)

    row = lambda width: pl.BlockSpec((tm, width), lambda i: (i, 0))
    full = lambda a, b: pl.BlockSpec((a, b), lambda i: (0, 0))
    kvspec = pl.BlockSpec((N_MEM, 2 * d), lambda i: (i // per, 0))
    return pl.pallas_call(
        body, name="cross_bwd", grid=(t // tm,),
        in_specs=[row(d), row(d), full(1, d), full(d, d), kvspec, full(d, d)],
        out_specs=[row(d), row(d), kvspec, full(1, d)],
        out_shape=[jax.ShapeDtypeStruct((t, d), F32), jax.ShapeDtypeStruct((t, d), BF),
                   jax.ShapeDtypeStruct((nb * N_MEM, 2 * d), F32), jax.ShapeDtypeStruct((1, d), F32)],
        compiler_params=_params(48, ("arbitrary",)),
    )(dh2, h1, gc, w_cq, kv, w_co)


def _mem_bwd(mem, gm, dkv, w_ckv, tm):
    t, d = mem.shape
    tm = min(tm, t)

    def body(mem_ref, dkv_ref, w_ref, dg_ref):
        @pl.when(pl.program_id(0) == 0)
        def _():
            dg_ref[...] = jnp.zeros_like(dg_ref)

        mv = mem_ref[...]
        dmn = _dot_bt(dkv_ref[...].astype(BF), w_ref[...])
        dg_ref[...] += jnp.sum(dmn * (mv * _rs(mv)), axis=0, keepdims=True)

    del gm
    return pl.pallas_call(
        body, name="mem_bwd", grid=(t // tm,),
        in_specs=[pl.BlockSpec((tm, d), lambda i: (i, 0)), pl.BlockSpec((tm, 2 * d), lambda i: (i, 0)),
                  pl.BlockSpec((d, 2 * d), lambda i: (0, 0))],
        out_specs=pl.BlockSpec((1, d), lambda i: (0, 0)),
        out_shape=jax.ShapeDtypeStruct((1, d), F32),
        compiler_params=_params(32, ("arbitrary",)),
    )(mem, dkv, w_ckv)


def _ffn_loss_fwd(h2, gf, w1, w2, gl, target, tm):
    t, d = h2.shape
    tm = min(tm, t)

    def body(h2_ref, gf_ref, w1_ref, w2_ref, gl_ref, tg_ref, hn_ref, f_ref, dh3_ref, dgl_ref, loss_ref):
        @pl.when(pl.program_id(0) == 0)
        def _():
            dgl_ref[...] = jnp.zeros_like(dgl_ref)
            loss_ref[...] = jnp.zeros_like(loss_ref)

        h2 = h2_ref[...]
        hn = (h2 * _rs(h2) * gf_ref[...]).astype(BF)
        hn_ref[...] = hn
        h3 = h2
        for c in range(4):
            f = jnp.maximum(_dot(hn, w1_ref[c]), 0.0)
            f_ref[:, c * 1024:(c + 1) * 1024] = f.astype(BF)
            h3 = h3 + _dot((f * f).astype(BF), w2_ref[c])
        r3 = _rs(h3)
        yh = h3 * r3
        glv = gl_ref[...]
        e = yh * glv - tg_ref[...]
        loss_ref[...] += 0.5 * jnp.sum(jnp.sum(e * e, axis=-1, keepdims=True) * (1.0 / d), axis=0, keepdims=True)
        dy = e * (1.0 / d)
        dx, dg = _rms_bwd(dy, yh, r3, glv)
        dh3_ref[...] = dx
        dgl_ref[...] += jnp.sum(dg, axis=0, keepdims=True)

    row = lambda width: pl.BlockSpec((tm, width), lambda i: (i, 0))
    return pl.pallas_call(
        body, name="ffn_loss_fwd", grid=(t // tm,),
        in_specs=[row(d), pl.BlockSpec((1, d), lambda i: (0, 0)), pl.BlockSpec((4, d, 1024), lambda i: (0, 0, 0)),
                  pl.BlockSpec((4, 1024, d), lambda i: (0, 0, 0)), pl.BlockSpec((1, d), lambda i: (0, 0)), row(d)],
        out_specs=[row(d), row(D_FF), row(d), pl.BlockSpec((1, d), lambda i: (0, 0)),
                   pl.BlockSpec((1, 1), lambda i: (0, 0))],
        out_shape=[jax.ShapeDtypeStruct((t, d), BF), jax.ShapeDtypeStruct((t, D_FF), BF),
                   jax.ShapeDtypeStruct((t, d), F32), jax.ShapeDtypeStruct((1, d), F32),
                   jax.ShapeDtypeStruct((1, 1), F32)],
        compiler_params=_params(56, ("arbitrary",)),
    )(h2, gf, w1, w2, gl, target)


def _ffn_bwd(dh3, f, h2, gf, w1, w2, tm):
    t, d = h2.shape
    tm = min(tm, t)

    def body(dh3_ref, f_ref, h2_ref, gf_ref, w1_ref, w2_ref, dh2_ref, dpre_ref, dgf_ref):
        @pl.when(pl.program_id(0) == 0)
        def _():
            dgf_ref[...] = jnp.zeros_like(dgf_ref)

        dh3 = dh3_ref[...]
        dh3b = dh3.astype(BF)
        dhn = jnp.zeros((tm, d), F32)
        for c in range(4):
            cols = slice(c * 1024, (c + 1) * 1024)
            dpre = (_dot_bt(dh3b, w2_ref[c]) * (2.0 * f_ref[:, cols].astype(F32))).astype(BF)
            dpre_ref[:, cols] = dpre
            dhn = dhn + _dot_bt(dpre, w1_ref[c])
        h2 = h2_ref[...]
        r = _rs(h2)
        dx, dg = _rms_bwd(dhn, h2 * r, r, gf_ref[...])
        dh2_ref[...] = dh3 + dx
        dgf_ref[...] += jnp.sum(dg, axis=0, keepdims=True)

    row = lambda width: pl.BlockSpec((tm, width), lambda i: (i, 0))
    return pl.pallas_call(
        body, name="ffn_bwd", grid=(t // tm,),
        in_specs=[row(d), row(D_FF), row(d), pl.BlockSpec((1, d), lambda i: (0, 0)),
                  pl.BlockSpec((4, d, 1024), lambda i: (0, 0, 0)), pl.BlockSpec((4, 1024, d), lambda i: (0, 0, 0))],
        out_specs=[row(d), row(D_FF), pl.BlockSpec((1, d), lambda i: (0, 0))],
        out_shape=[jax.ShapeDtypeStruct((t, d), F32), jax.ShapeDtypeStruct((t, D_FF), BF),
                   jax.ShapeDtypeStruct((1, d), F32)],
        compiler_params=_params(56, ("arbitrary",)),
    )(dh3, f, h2, gf, w1, w2)


def _in_bwd(dproj, dh1, x, g, w_in, tm):
    t, d = x.shape
    n = w_in.shape[1]
    tm = min(tm, t)

    def body(dp_ref, dh1_ref, x_ref, g_ref, w_ref, dx_ref, dg_ref):
        @pl.when(pl.program_id(0) == 0)
        def _():
            dg_ref[...] = jnp.zeros_like(dg_ref)

        dxn = _dot_bt(dp_ref[...], w_ref[...])
        xv = x_ref[...]
        r = _rs(xv)
        dx, dg = _rms_bwd(dxn, xv * r, r, g_ref[...])
        dx_ref[...] = dh1_ref[...] + dx
        dg_ref[...] += jnp.sum(dg, axis=0, keepdims=True)

    row = lambda width: pl.BlockSpec((tm, width), lambda i: (i, 0))
    return pl.pallas_call(
        body, name="in_bwd", grid=(t // tm,),
        in_specs=[row(n), row(d), row(d), pl.BlockSpec((1, d), lambda i: (0, 0)),
                  pl.BlockSpec((d, n), lambda i: (0, 0))],
        out_specs=[row(d), pl.BlockSpec((1, d), lambda i: (0, 0))],
        out_shape=[jax.ShapeDtypeStruct((t, d), F32), jax.ShapeDtypeStruct((1, d), F32)],
        compiler_params=_params(48, ("arbitrary",)),
    )(dproj, dh1, x, g, w_in)


def _full_weights(gathered):
    d = D_MODEL
    out = {}
    for k, a in gathered.items():
        if k in ("w_in", "w_ckv", "w_ff1"):
            out[k] = a.transpose(1, 0, 2).reshape(d, -1)
        else:
            out[k] = a.reshape(-1, d)
    return out


def _slabs(a):
    return a.reshape(N_DEV, -1, a.shape[-1])


def _local_step(x, mem, target, small, big, nb, s, tq=256, gather_rest=None, reduce_rest=None):
    d = D_MODEL
    g_mix, g_v, w_sp, b_sp, g_head, g_cross, g_mem, g_ffn, g_fin = (
        small[k] for k in ("norm_mix_g", "gm_v_norm_g", "w_spatial", "b_spatial", "head_norm_g", "norm_cross_g",
                           "norm_mem_g", "norm_ffn_g", "norm_final_g"))
    tri = jnp.tril(jnp.ones((CHUNK, CHUNK), dtype=bool))
    w_sp_m = jnp.where(tri[None], w_sp, 0.0)
    wt = w_sp_m.astype(BF)
    wtt = jnp.swapaxes(w_sp_m, 1, 2).astype(BF)
    bb = jnp.broadcast_to(b_sp[:, :, None], (GM_GROUPS, CHUNK, CHUNK))
    hg_a = g_head[:, :GM_WIDTH]

    proj, xn = _norm_matmul(x, g_mix, big["w_in"], 512, "in_proj")
    ma = _gmlp_fwd(proj, g_v, wt, bb, hg_a, 512)
    o_sb, tot, mb, nblk, gathered = _sb_fwd(proj, g_head, nb, s, tq, ride=gather_rest)
    if gather_rest is not None:
        big = dict(big, **_full_weights(dict(zip(BIG[1:], gathered))))
    w1c = big["w_ff1"].reshape(d, 4, 1024).transpose(1, 0, 2)
    w2c = big["w_ff2"].reshape(4, 1024, d)
    kv, memn = _norm_matmul(mem, g_mem, big["w_ckv"], 512, "mem_proj")
    h1, h2, hn, oc = _mix_cross_fwd(x, ma, mb, big["w_out"], g_cross, big["w_cq"], kv, big["w_co"], s, 512)
    hn2, f, dh3, d_fin, loss = _ffn_loss_fwd(h2, g_ffn, w1c, w2c, g_fin, target, 256)

    gbig = {}
    dh2, dpre, d_ffn = _ffn_bwd(dh3, f, h2, g_ffn, w1c, w2c, 256)
    gbig["w_ff2"] = _slabs(_wgrad(f, dh3, 1024, 512, "wgrad_ff2", square_a=True))
    gbig["w_ff1"] = _slabs(_wgrad(hn2, dpre, 1024, 512, "wgrad_ff1", col_shards=4))
    dh1, dqc, dkv, d_cross = _cross_bwd(dh2, h1, g_cross, big["w_cq"], kv, big["w_co"], s, 512)
    gbig["w_co"] = _slabs(_wgrad(oc, dh2, 1024, 512, "wgrad_co"))
    gbig["w_cq"] = _slabs(_wgrad(hn, dqc, 1024, 512, "wgrad_cq"))
    gbig["w_ckv"] = _slabs(_wgrad(memn, dkv, 512, 512, "wgrad_ckv", col_shards=4))
    d_mem = _mem_bwd(mem, g_mem, dkv, big["w_ckv"], 512)
    dmerged = _matmul_bt(dh1, big["w_out"], 512, "out_bwd")
    gbig["w_out"] = _slabs(_wgrad(jnp.concatenate([ma, mb], axis=1), dh1, 1024, 512, "wgrad_out"))
    dugv, d_wsp, d_bb, d_gv, d_hga = _gmlp_bwd(proj, dmerged, g_v, wt, wtt, bb, hg_a, 512)
    ride = reduce_rest({k: gbig[k] for k in BIG[1:]}) if reduce_rest is not None else None
    dq, dk, dv, d_hgb, reduced = _sb_bwd(proj, o_sb, tot, nblk, dmerged, g_head, nb, s, tq, ride=ride)
    dproj = jnp.concatenate([dugv, dq, dk, dv], axis=1)
    grad_x, d_mix = _in_bwd(dproj, dh1, x, g_mix, big["w_in"], 512)
    gbig["w_in"] = _slabs(_wgrad(xn, dproj, 640, 512, "wgrad_in", col_shards=4))

    gsmall = {
        "norm_mix_g": d_mix, "gm_v_norm_g": d_gv, "w_spatial": d_wsp, "b_spatial": d_bb[:, :, 0],
        "head_norm_g": jnp.concatenate([d_hga, jnp.sum(d_hgb, axis=0)], axis=1), "norm_cross_g": d_cross,
        "norm_mem_g": d_mem, "norm_ffn_g": d_ffn, "norm_final_g": d_fin,
    }
    return loss, grad_x, gsmall, gbig, reduced


BIG = ("w_in", "w_out", "w_cq", "w_ckv", "w_co", "w_ff1", "w_ff2")
SMALL = ("norm_mix_g", "gm_v_norm_g", "w_spatial", "b_spatial", "head_norm_g", "norm_cross_g", "norm_mem_g",
         "norm_ffn_g", "norm_final_g")


def _local_copies_start(srcs, stages, sems):
    loads = [pltpu.make_async_copy(src, stage, sems.at[w]) for w, (src, stage) in enumerate(zip(srcs, stages))]
    for ld in loads:
        ld.start()
    return loads


def _local_copies_finish(loads, stages, dsts, sems):
    stores = []
    for w, (ld, stage, dst) in enumerate(zip(loads, stages, dsts)):
        ld.wait()
        st = pltpu.make_async_copy(stage, dst, sems.at[w])
        st.start()
        stores.append(st)
    for st in stores:
        st.wait()


def _sibling_exchange(slabs, name):
    n = len(slabs)

    def body(*refs):
        ins, outs = refs[:n], refs[n:2 * n]
        send_sems, recv_sems = refs[2 * n:]
        x, y, c = lax.axis_index("x"), lax.axis_index("y"), lax.axis_index("c")
        copies = []
        for w in range(n):
            for p in range(N_CHIPS):
                cp = pltpu.make_async_remote_copy(
                    src_ref=ins[w].at[2 * p + (1 - c)], dst_ref=outs[w].at[p], send_sem=send_sems.at[4 * w + p],
                    recv_sem=recv_sems.at[4 * w + p], device_id=(x, y, 1 - c), device_id_type=MESH)
                cp.start()
                copies.append(cp)
        for cp in copies:
            cp.wait()

    return pl.pallas_call(
        body, name=name,
        in_specs=[ANY] * n, out_specs=[ANY] * n,
        out_shape=[jax.ShapeDtypeStruct((N_CHIPS,) + a.shape[1:], a.dtype) for a in slabs],
        scratch_shapes=[pltpu.SemaphoreType.DMA((4 * n,)), pltpu.SemaphoreType.DMA((4 * n,))],
    )(*slabs)


def _chip_sum(slabs, recv, c_idx, name):
    _, r, cw = slabs.shape
    tr = min(r, 256)

    def body(c_ref, a_ref, b_ref, o_ref):
        del c_ref
        o_ref[...] = (a_ref[...] + b_ref[...]).astype(BF)

    return pl.pallas_call(
        body, name=name,
        grid_spec=pltpu.PrefetchScalarGridSpec(
            num_scalar_prefetch=1, grid=(N_CHIPS, r // tr),
            in_specs=[pl.BlockSpec((None, tr, cw), lambda p, i, c_ref: (2 * p + c_ref[0], i, 0)),
                      pl.BlockSpec((None, tr, cw), lambda p, i, c_ref: (p, i, 0))],
            out_specs=pl.BlockSpec((None, tr, cw), lambda p, i, c_ref: (p, i, 0))),
        out_shape=jax.ShapeDtypeStruct((N_CHIPS, r, cw), BF),
        compiler_params=_params(32, ("arbitrary", "arbitrary")),
    )(c_idx, slabs, recv)


def _sum4(sums, parts, q_idx, name):
    _, r, cw = parts.shape
    tr = min(r, 256)

    def body(q_ref, own_ref, a_ref, b_ref, c_ref, o_ref):
        del q_ref
        o_ref[...] = ((own_ref[...].astype(F32) + a_ref[...].astype(F32)) + b_ref[...].astype(F32)) + c_ref[
            ...].astype(F32)

    spec = lambda k: pl.BlockSpec((None, tr, cw), lambda i, q_ref: ((q_ref[0] + k) % N_CHIPS, i, 0))
    return pl.pallas_call(
        body, name=name,
        grid_spec=pltpu.PrefetchScalarGridSpec(
            num_scalar_prefetch=1, grid=(r // tr,), in_specs=[spec(0), spec(1), spec(2), spec(3)],
            out_specs=pl.BlockSpec((tr, cw), lambda i, q_ref: (i, 0))),
        out_shape=jax.ShapeDtypeStruct((r, cw), F32),
        compiler_params=_params(32, ("arbitrary",)),
    )(q_idx, sums, parts, parts, parts)


def _half_exchange(halves):
    n = len(halves)

    def body(*refs):
        ins, outs, stages = refs[:n], refs[n:2 * n], refs[2 * n:3 * n]
        send_sems, recv_sems, ld_sems, st_sems = refs[3 * n:]
        x, y, c = lax.axis_index("x"), lax.axis_index("y"), lax.axis_index("c")
        loads = _local_copies_start(ins, stages, ld_sems)
        copies = []
        for w in range(n):
            cp = pltpu.make_async_remote_copy(
                src_ref=ins[w], dst_ref=outs[w].at[c], send_sem=send_sems.at[w], recv_sem=recv_sems.at[w],
                device_id=(x, y, 1 - c), device_id_type=MESH)
            cp.start()
            copies.append(cp)
        _local_copies_finish(loads, stages, [outs[w].at[c] for w in range(n)], st_sems)
        for cp in copies:
            cp.wait()

    return pl.pallas_call(
        body, name="grad_half_exchange",
        in_specs=[ANY] * n, out_specs=[ANY] * n,
        out_shape=[jax.ShapeDtypeStruct((2,) + a.shape, a.dtype) for a in halves],
        scratch_shapes=[pltpu.VMEM(a.shape, a.dtype) for a in halves] + [
            pltpu.SemaphoreType.DMA((n,)), pltpu.SemaphoreType.DMA((n,)),
            pltpu.SemaphoreType.DMA((n,)), pltpu.SemaphoreType.DMA((n,))],
        compiler_params=_params(24),
    )(*halves)


def _small_all_reduce(packed):
    rows = packed.shape[0]

    def body(in_ref, out_ref, buf, send_sems, recv_sems):
        x, y, c = lax.axis_index("x"), lax.axis_index("y"), lax.axis_index("c")
        me = 4 * x + 2 * y + c
        buf[me] = in_ref[...]
        copies = []
        for k in range(1, N_DEV):
            bx, by, bc = (k >> 2) & 1, (k >> 1) & 1, k & 1
            peer = (x ^ bx, y ^ by, c ^ bc)
            cp = pltpu.make_async_remote_copy(
                src_ref=in_ref, dst_ref=buf.at[me], send_sem=send_sems.at[k - 1], recv_sem=recv_sems.at[k - 1],
                device_id=peer, device_id_type=MESH)
            cp.start()
            copies.append(cp)
        for cp in copies:
            cp.wait()
        acc = buf[0]
        for dev in range(1, N_DEV):
            acc = acc + buf[dev]
        out_ref[...] = acc

    return pl.pallas_call(
        body, name="small_all_reduce",
        in_specs=[pl.BlockSpec(memory_space=pltpu.VMEM)], out_specs=pl.BlockSpec(memory_space=pltpu.VMEM),
        out_shape=jax.ShapeDtypeStruct(packed.shape, F32),
        scratch_shapes=[pltpu.VMEM((N_DEV, rows, 128), F32), pltpu.SemaphoreType.DMA((N_DEV - 1,)),
                        pltpu.SemaphoreType.DMA((N_DEV - 1,))],
        compiler_params=_params(16),
    )(packed)


def _adamw(g, w, m, v, name):
    r, cw = g.shape
    tr = r if r <= 1024 else 256
    assert r % tr == 0

    def body(g_ref, w_ref, m_ref, v_ref, d_ref, nm_ref, nv_ref):
        gv = g_ref[...]
        nm = ADAM_B1 * m_ref[...] + (1.0 - ADAM_B1) * gv
        nv = ADAM_B2 * v_ref[...] + (1.0 - ADAM_B2) * (gv * gv)
        m_hat = nm / (1.0 - ADAM_B1 ** ADAM_STEP)
        v_hat = nv / (1.0 - ADAM_B2 ** ADAM_STEP)
        d_ref[...] = -ADAM_LR * (m_hat / (jnp.sqrt(v_hat) + ADAM_EPS) + ADAM_WD * w_ref[...])
        nm_ref[...] = nm
        nv_ref[...] = nv

    spec = pl.BlockSpec((tr, cw), lambda i: (i, 0))
    return pl.pallas_call(
        body, name=name, grid=(r // tr,),
        in_specs=[spec] * 4, out_specs=[spec] * 3,
        out_shape=[jax.ShapeDtypeStruct((r, cw), F32)] * 3,
        compiler_params=_params(32, ("arbitrary",)),
    )(g, w, m, v)


def _small_params(args):
    small = {k: args[k].reshape(1, -1) for k in SMALL}
    small["w_spatial"] = args["w_spatial"][0]
    small["b_spatial"] = args["b_spatial"][0]
    return small


def _pack(parts, rows):
    flat = jnp.concatenate([p.reshape(-1).astype(F32) for p in parts])
    return jnp.pad(flat, (0, rows * 128 - flat.shape[0])).reshape(rows, 128)


def _unpack(packed, shapes):
    flat = packed.reshape(-1)
    out, off = [], 0
    for shp in shapes:
        size = math.prod(shp)
        out.append(flat[off:off + size].reshape(shp))
        off += size
    return out


def kernel(x, mem, norm_mix_g, w_in, gm_v_norm_g, w_spatial, b_spatial, head_norm_g, w_out, norm_cross_g, norm_mem_g, w_cq, w_ckv, w_co, norm_ffn_g, w_ff1, w_ff2, norm_final_g, loss_target, m_norm_mix_g, m_w_in, m_gm_v_norm_g, m_w_spatial, m_b_spatial, m_head_norm_g, m_w_out, m_norm_cross_g, m_norm_mem_g, m_w_cq, m_w_ckv, m_w_co, m_norm_ffn_g, m_w_ff1, m_w_ff2, m_norm_final_g, v_norm_mix_g, v_w_in, v_gm_v_norm_g, v_w_spatial, v_b_spatial, v_head_norm_g, v_w_out, v_norm_cross_g, v_norm_mem_g, v_w_cq, v_w_ckv, v_w_co, v_norm_ffn_g, v_w_ff1, v_w_ff2, v_norm_final_g):
    args = dict(locals())
    d = D_MODEL
    nb, s, _ = x.shape
    c_idx = lax.axis_index("c").astype(jnp.int32).reshape(1)
    q_idx = (2 * lax.axis_index("x") + lax.axis_index("y")).astype(jnp.int32).reshape(1)
    rest = BIG[1:]

    shards = {k: args[k][0].astype(BF) for k in BIG}
    big = _full_weights({"w_in": _run_exchange(_GatherExchange([shards["w_in"]]), "all_gather_w_in")[0]})
    gather_rest = _GatherExchange([shards[k] for k in rest])

    sums = {}

    def chip_sums(group):
        names = list(group)
        recv = _sibling_exchange([group[k] for k in names], "grad_sibling_exchange_" + names[0])
        for k, r in zip(names, recv):
            sums[k] = _chip_sum(group[k], r, c_idx, "chip_sum_" + k)
        return _ChipExchange([sums[k] for k in names])

    loss, grad_x, gsmall, gbig, parts_rest = _local_step(
        x.reshape(nb * s, d), mem.reshape(nb * N_MEM, d), loss_target.reshape(nb * s, d), _small_params(args), big,
        nb, s, gather_rest=gather_rest, reduce_rest=chip_sums)
    parts = dict(zip(rest, parts_rest))
    parts["w_in"] = _run_exchange(chip_sums({"w_in": gbig["w_in"]}), "grad_chip_exchange_w_in")[0]
    halves = [_sum4(sums[k], parts[k], q_idx, "sum4_" + k) for k in BIG]
    both = _half_exchange(halves)

    out = {"grad_x": grad_x.reshape(nb, s, d)}
    for k, g2 in zip(BIG, both):
        shp = args[k].shape
        g = g2.reshape(shp[1], shp[2])
        dl, nm, nv = _adamw(g, args[k][0], args["m_" + k][0], args["v_" + k][0], "adamw_" + k)
        out["grad_" + k], out["delta_" + k], out["new_m_" + k], out["new_v_" + k] = (
            a.reshape(shp) for a in (g, dl, nm, nv))

    shapes = [args[k].shape for k in SMALL]
    n_small = sum(math.prod(sh) for sh in shapes)
    rows = -(-(n_small + 1) // 1024) * 8
    reduced = _small_all_reduce(_pack([gsmall[k] for k in SMALL] + [loss], rows))
    dl, nm, nv = _adamw(reduced, _pack([args[k] for k in SMALL], rows), _pack([args["m_" + k] for k in SMALL], rows),
                        _pack([args["v_" + k] for k in SMALL], rows), "adamw_small")
    for name, arr in (("grad_", reduced), ("delta_", dl), ("new_m_", nm), ("new_v_", nv)):
        for k, a in zip(SMALL, _unpack(arr, shapes)):
            out[name + k] = a
    out["loss"] = reduced.reshape(-1)[n_small]

    names = ["norm_mix_g", "w_in", "gm_v_norm_g", "w_spatial", "b_spatial", "head_norm_g", "w_out", "norm_cross_g",
             "norm_mem_g", "w_cq", "w_ckv", "w_co", "norm_ffn_g", "w_ff1", "w_ff2", "norm_final_g"]
    return (out["loss"], out["grad_x"], *[out["grad_" + k] for k in names], *[out["delta_" + k] for k in names],
            *[out["new_m_" + k] for k in names], *[out["new_v_" + k] for k in names])
```

```python
import functools
import math

import jax
import jax.numpy as jnp
from jax import lax
from jax.experimental import pallas as pl
from jax.experimental.pallas import tpu as pltpu

F32 = jnp.float32
BF = jnp.bfloat16

EPS = 1e-6
D_MODEL = 1024
CHUNK = 128
GM_GROUPS = 4
GM_WIDTH = 512
SB_WIDTH = 512
HEAD_LANES = 64
SB_SCALE = 0.125
SB_SKIP = -104.0
X_HEADS = 4
X_HEAD_DIM = 256
N_MEM = 256
D_FF = 4096
IN_COLS = 2560
N_CHIPS = 4
N_DEV = 8

ADAM_LR = 0.001
ADAM_B1 = 0.9
ADAM_B2 = 0.999
ADAM_EPS = 1e-08
ADAM_WD = 0.01
ADAM_STEP = 10

V7X_VMEM_BYTES = 64 * 1024 * 1024
MESH = pl.DeviceIdType.MESH
ANY = pl.BlockSpec(memory_space=pl.ANY)

GELU_C = math.sqrt(2.0 / math.pi)
GELU_A = 0.044715


def _params(vmem_mb, sem=None):
    assert vmem_mb * 1024 * 1024 <= V7X_VMEM_BYTES
    return pltpu.CompilerParams(vmem_limit_bytes=vmem_mb * 1024 * 1024, dimension_semantics=sem)


def _dot(a, b):
    return jnp.dot(a, b, preferred_element_type=F32)


def _dot_bt(a, b):
    return lax.dot_general(a, b, (((1,), (1,)), ((), ())), preferred_element_type=F32)


def _dot_at(a, b):
    return lax.dot_general(a, b, (((0,), (0,)), ((), ())), preferred_element_type=F32)


def _gelu(x):
    t = jnp.tanh(GELU_C * (x + GELU_A * x * x * x))
    return 0.5 * x * (1.0 + t)


def _gelu_and_grad(x):
    x2 = x * x
    t = jnp.tanh(GELU_C * (x + GELU_A * x2 * x))
    h = 0.5 * (1.0 + t)
    return x * h, h + 0.5 * x * (1.0 - t * t) * (GELU_C * (1.0 + 3.0 * GELU_A * x2))


def _rs(x):
    return lax.rsqrt(jnp.mean(x * x, axis=-1, keepdims=True) + EPS)


def _rms_bwd(dxn, xhat, r, g):
    dxh = dxn * g
    dx = r * (dxh - xhat * jnp.mean(dxh * xhat, axis=-1, keepdims=True))
    return dx, dxn * xhat


def _norm_matmul(x, g, w, tm, name):
    t, d = x.shape
    n = w.shape[1]
    tm = min(tm, t)

    def body(x_ref, g_ref, w_ref, out_ref, xn_ref):
        xv = x_ref[...]
        xn = (xv * _rs(xv) * g_ref[...]).astype(BF)
        xn_ref[...] = xn
        out_ref[...] = _dot(xn, w_ref[...]).astype(out_ref.dtype)

    return pl.pallas_call(
        body, name=name, grid=(t // tm,),
        in_specs=[pl.BlockSpec((tm, d), lambda i: (i, 0)), pl.BlockSpec((1, d), lambda i: (0, 0)),
                  pl.BlockSpec((d, n), lambda i: (0, 0))],
        out_specs=[pl.BlockSpec((tm, n), lambda i: (i, 0)), pl.BlockSpec((tm, d), lambda i: (i, 0))],
        out_shape=[jax.ShapeDtypeStruct((t, n), BF), jax.ShapeDtypeStruct((t, d), BF)],
        compiler_params=_params(48, ("arbitrary",)),
    )(x, g, w)


def _wgrad(a, g, tn, tk, name, square_a=False, col_shards=1):
    t, m = a.shape
    n = g.shape[1]
    tk = min(tk, t)
    tm = min(m, 1024)
    ns = n // col_shards
    assert ns % tn == 0 and m % tm == 0
    per = ns // tn
    nk = t // tk

    def body(a_ref, g_ref, o_ref):
        k = pl.program_id(2)

        @pl.when(k == 0)
        def _():
            o_ref[...] = jnp.zeros_like(o_ref)

        av = a_ref[...]
        if square_a:
            af = av.astype(F32)
            av = af * af
        o_ref[...] += _dot_at(av.astype(BF), g_ref[...].astype(BF))

    return pl.pallas_call(
        body, name=name, grid=(m // tm, n // tn, nk),
        in_specs=[pl.BlockSpec((tk, tm), lambda i, j, k: (k, i)), pl.BlockSpec((tk, tn), lambda i, j, k: (k, j))],
        out_specs=pl.BlockSpec((None, tm, tn), lambda i, j, k: (j // per, i, j % per)),
        out_shape=jax.ShapeDtypeStruct((col_shards, m, ns), F32),
        compiler_params=_params(48, ("arbitrary", "arbitrary", "arbitrary")),
    )(a, g)


def _matmul_bt(a, w, tm, name):
    t, n = a.shape
    k = w.shape[0]
    tm = min(tm, t)

    def body(a_ref, w_ref, o_ref):
        o_ref[...] = _dot_bt(a_ref[...].astype(BF), w_ref[...]).astype(o_ref.dtype)

    return pl.pallas_call(
        body, name=name, grid=(t // tm,),
        in_specs=[pl.BlockSpec((tm, n), lambda i: (i, 0)), pl.BlockSpec((k, n), lambda i: (0, 0))],
        out_specs=pl.BlockSpec((tm, k), lambda i: (i, 0)),
        out_shape=jax.ShapeDtypeStruct((t, k), BF),
        compiler_params=_params(32, ("arbitrary",)),
    )(a, w)


def _gmlp_fwd(proj, gg, wt, bb, hg, tm):
    t = proj.shape[0]
    tm = min(tm, t)

    def body(u_ref, v_ref, gg_ref, wt_ref, bb_ref, hg_ref, out_ref):
        for cc in range(tm // CHUNK):
            rows = slice(cc * CHUNK, (cc + 1) * CHUNK)
            for g in range(GM_GROUPS):
                cols = slice(g * 128, (g + 1) * 128)
                u = _gelu(u_ref[rows, cols].astype(F32))
                gv = _gelu(v_ref[rows, cols].astype(F32))
                vn = gv * _rs(gv) * gg_ref[:, cols]
                mixed = _dot(wt_ref[g], vn.astype(BF)) + bb_ref[g]
                a = u * mixed
                out_ref[rows, cols] = (a * _rs(a) * hg_ref[:, cols]).astype(BF)

    return pl.pallas_call(
        body, name="gmlp_fwd", grid=(t // tm,),
        in_specs=[pl.BlockSpec((tm, 512), lambda i: (i, 0)), pl.BlockSpec((tm, 512), lambda i: (i, 1)),
                  pl.BlockSpec((1, 512), lambda i: (0, 0)), pl.BlockSpec((4, 128, 128), lambda i: (0, 0, 0)),
                  pl.BlockSpec((4, 128, 128), lambda i: (0, 0, 0)), pl.BlockSpec((1, 512), lambda i: (0, 0))],
        out_specs=pl.BlockSpec((tm, 512), lambda i: (i, 0)),
        out_shape=jax.ShapeDtypeStruct((t, 512), BF),
        compiler_params=_params(32, ("arbitrary",)),
    )(proj, proj, gg, wt, bb, hg)


def _gmlp_bwd(proj, dmerged, gg, wt, wtt, bb, hg, tm, ride=None):
    t = proj.shape[0]
    tm = min(tm, t)
    nsteps = t // tm

    def body(u_ref, v_ref, dm_ref, gg_ref, wt_ref, wtt_ref, bb_ref, hg_ref,
             dp_ref, dw_ref, db_ref, dgg_ref, dhg_ref):
        i = pl.program_id(0)

        @pl.when(i == 0)
        def _():
            dw_ref[...] = jnp.zeros_like(dw_ref)
            db_ref[...] = jnp.zeros_like(db_ref)
            dgg_ref[...] = jnp.zeros_like(dgg_ref)
            dhg_ref[...] = jnp.zeros_like(dhg_ref)

        for cc in range(tm // CHUNK):
            rows = slice(cc * CHUNK, (cc + 1) * CHUNK)
            for g in range(GM_GROUPS):
                cols = slice(g * 128, (g + 1) * 128)
                up = u_ref[rows, cols].astype(F32)
                gp = v_ref[rows, cols].astype(F32)
                u, u_grad = _gelu_and_grad(up)
                gv, gv_grad = _gelu_and_grad(gp)
                rv = _rs(gv)
                gvh = gv * rv
                ggv = gg_ref[:, cols]
                vnb = (gvh * ggv).astype(BF)
                mixed = _dot(wt_ref[g], vnb) + bb_ref[g]
                a = u * mixed
                ra = _rs(a)
                ah = a * ra
                dm = dm_ref[rows, cols].astype(F32)
                dhg_ref[:, cols] += jnp.sum(dm * ah, axis=0, keepdims=True)
                dah = dm * hg_ref[:, cols]
                da = ra * (dah - ah * jnp.mean(dah * ah, axis=-1, keepdims=True))
                du = da * mixed
                dmix = da * u
                db_ref[g] += dmix
                dmb = dmix.astype(BF)
                dw_ref[g] += _dot_bt(dmb, vnb)
                dvn = _dot(wtt_ref[g], dmb)
                dgg_ref[:, cols] += jnp.sum(dvn * gvh, axis=0, keepdims=True)
                dgh = dvn * ggv
                dgv = rv * (dgh - gvh * jnp.mean(dgh * gvh, axis=-1, keepdims=True))
                dp_ref[rows, cols] = (du * u_grad).astype(BF)
                dp_ref[rows, 512 + g * 128:512 + (g + 1) * 128] = (dgv * gv_grad).astype(BF)

        @pl.when(i == nsteps - 1)
        def _():
            r = lax.broadcasted_iota(jnp.int32, (CHUNK, CHUNK), 0)
            c = lax.broadcasted_iota(jnp.int32, (CHUNK, CHUNK), 1)
            for g in range(GM_GROUPS):
                dw_ref[g] = jnp.where(c <= r, dw_ref[g], 0.0)
                db_ref[g] = jnp.broadcast_to(jnp.sum(db_ref[g], axis=-1, keepdims=True), (CHUNK, CHUNK))

    small = lambda shape: pl.BlockSpec(shape, lambda i: (0,) * len(shape))
    res, rode = _ride_call(
        body, "gmlp_bwd", (nsteps,),
        in_specs=[pl.BlockSpec((tm, 512), lambda i: (i, 0)), pl.BlockSpec((tm, 512), lambda i: (i, 1)),
                  pl.BlockSpec((tm, 512), lambda i: (i, 0)), small((1, 512)), small((4, 128, 128)),
                  small((4, 128, 128)), small((4, 128, 128)), small((1, 512))],
        out_specs=[pl.BlockSpec((tm, 1024), lambda i: (i, 0)), small((4, 128, 128)), small((4, 128, 128)),
                   small((1, 512)), small((1, 512))],
        out_shape=[jax.ShapeDtypeStruct((t, 1024), BF), jax.ShapeDtypeStruct((4, 128, 128), F32),
                   jax.ShapeDtypeStruct((4, 128, 128), F32), jax.ShapeDtypeStruct((1, 512), F32),
                   jax.ShapeDtypeStruct((1, 512), F32)],
        scratch_shapes=[], operands=(proj, proj, dmerged, gg, wt, wtt, bb, hg), vmem_mb=32, ride=ride)
    return (*res, rode)


def _other_chips(x, y):
    return ((1 - x, y), (x, 1 - y), (1 - x, 1 - y))


class _GatherExchange:
    def __init__(self, shards):
        n = len(shards)
        self.n = n
        self.in_arrays = list(shards)
        self.out_shape = [jax.ShapeDtypeStruct((N_CHIPS,) + a.shape, a.dtype) for a in shards]
        self.half_rows = [a.shape[0] // 2 for a in shards]
        sems = lambda k: pltpu.SemaphoreType.DMA((k,))
        self.scratch_shapes = [pltpu.VMEM(a.shape, a.dtype) for a in shards] + [
            sems(3 * n), sems(3 * n), sems(3 * n), sems(3 * n), sems(n), sems(n)]

    def _copies(self, ins, outs, scr):
        n = self.n
        stages, (ici_send, ici_recv, d2d_send, d2d_recv, ld_sems, st_sems) = scr[:n], scr[n:]
        x, y, c = lax.axis_index("x"), lax.axis_index("y"), lax.axis_index("c")
        q = 2 * x + y
        loads = [pltpu.make_async_copy(ins[w], stages[w], ld_sems.at[w]) for w in range(n)]
        stores = [pltpu.make_async_copy(stages[w], outs[w].at[q], st_sems.at[w]) for w in range(n)]
        ici, d2d = [], []
        for w in range(n):
            half = pl.ds(c * self.half_rows[w], self.half_rows[w])
            for k, (px, py) in enumerate(_other_chips(x, y)):
                ici.append(pltpu.make_async_remote_copy(
                    src_ref=ins[w].at[half], dst_ref=outs[w].at[q, half], send_sem=ici_send.at[3 * w + k],
                    recv_sem=ici_recv.at[3 * w + k], device_id=(px, py, c), device_id_type=MESH))
                landed = outs[w].at[2 * px + py, half]
                d2d.append(pltpu.make_async_remote_copy(
                    src_ref=landed, dst_ref=landed, send_sem=d2d_send.at[3 * w + k],
                    recv_sem=d2d_recv.at[3 * w + k], device_id=(x, y, 1 - c), device_id_type=MESH))
        return loads, stores, ici, d2d

    def start(self, ins, outs, scr):
        loads, stores, ici, _ = self._copies(ins, outs, scr)
        for cp in loads + ici:
            cp.start()
        for ld, st in zip(loads, stores):
            ld.wait()
            st.start()

    def relay(self, ins, outs, scr):
        _, _, ici, d2d = self._copies(ins, outs, scr)
        for got, fwd in zip(ici, d2d):
            got.wait_recv()
            fwd.start()

    def finish(self, ins, outs, scr):
        _, stores, ici, d2d = self._copies(ins, outs, scr)
        for cp in ici:
            cp.wait_send()
        for cp in d2d + stores:
            cp.wait()


class _SiblingExchange:
    def __init__(self, slabs):
        n = len(slabs)
        self.n = n
        self.in_arrays = list(slabs)
        self.out_shape = [jax.ShapeDtypeStruct((N_CHIPS,) + a.shape[1:], a.dtype) for a in slabs]
        self.scratch_shapes = [pltpu.SemaphoreType.DMA((4 * n,)), pltpu.SemaphoreType.DMA((4 * n,))]

    def _copies(self, ins, outs, scr):
        send_sems, recv_sems = scr
        x, y, c = lax.axis_index("x"), lax.axis_index("y"), lax.axis_index("c")
        return [pltpu.make_async_remote_copy(
            src_ref=ins[w].at[2 * p + (1 - c)], dst_ref=outs[w].at[p], send_sem=send_sems.at[4 * w + p],
            recv_sem=recv_sems.at[4 * w + p], device_id=(x, y, 1 - c), device_id_type=MESH)
            for w in range(self.n) for p in range(N_CHIPS)]

    def start(self, ins, outs, scr):
        for cp in self._copies(ins, outs, scr):
            cp.start()

    def finish(self, ins, outs, scr):
        for cp in self._copies(ins, outs, scr):
            cp.wait()


class _ChipExchange:
    def __init__(self, sums):
        n = len(sums)
        self.n = n
        self.in_arrays = list(sums)
        self.out_shape = [jax.ShapeDtypeStruct(a.shape, a.dtype) for a in sums]
        self.scratch_shapes = [pltpu.SemaphoreType.DMA((3 * n,)), pltpu.SemaphoreType.DMA((3 * n,))]

    def _copies(self, ins, outs, scr):
        send_sems, recv_sems = scr
        x, y, c = lax.axis_index("x"), lax.axis_index("y"), lax.axis_index("c")
        q = 2 * x + y
        return [pltpu.make_async_remote_copy(
            src_ref=ins[w].at[2 * px + py], dst_ref=outs[w].at[q], send_sem=send_sems.at[3 * w + k],
            recv_sem=recv_sems.at[3 * w + k], device_id=(px, py, c), device_id_type=MESH)
            for w in range(self.n) for k, (px, py) in enumerate(_other_chips(x, y))]

    def start(self, ins, outs, scr):
        for cp in self._copies(ins, outs, scr):
            cp.start()

    def finish(self, ins, outs, scr):
        for cp in self._copies(ins, outs, scr):
            cp.wait()


class _NoExchange:
    in_arrays, out_shape, scratch_shapes = (), (), ()

    def start(self, ins, outs, scr):
        pass

    def finish(self, ins, outs, scr):
        pass


def _run_exchange(ex, name):
    n_in, n_out = len(ex.in_arrays), len(ex.out_shape)

    def body(*refs):
        ins, outs, scr = refs[:n_in], refs[n_in:n_in + n_out], refs[n_in + n_out:]
        ex.start(ins, outs, scr)
        if hasattr(ex, "relay"):
            ex.relay(ins, outs, scr)
        ex.finish(ins, outs, scr)

    return pl.pallas_call(
        body, name=name, in_specs=[ANY] * n_in, out_specs=[ANY] * n_out, out_shape=ex.out_shape,
        scratch_shapes=ex.scratch_shapes, compiler_params=_params(24),
    )(*ex.in_arrays)


def _ride_call(body, name, grid, in_specs, out_specs, out_shape, scratch_shapes, operands, vmem_mb, ride=None):
    ride = ride or _NoExchange()
    ni, no, ns = len(in_specs), len(out_specs), len(scratch_shapes)
    ri, ro = len(ride.in_arrays), len(ride.out_shape)
    total = math.prod(grid)

    def wrapped(*refs):
        ins, rins = refs[:ni], refs[ni:ni + ri]
        outs, routs = refs[ni + ri:ni + ri + no], refs[ni + ri + no:ni + ri + no + ro]
        scr, rscr = refs[ni + ri + no + ro:ni + ri + no + ro + ns], refs[ni + ri + no + ro + ns:]
        step = pl.program_id(0)
        for ax in range(1, len(grid)):
            step = step * grid[ax] + pl.program_id(ax)

        @pl.when(step == 0)
        def _():
            ride.start(rins, routs, rscr)

        if hasattr(ride, "relay"):
            @pl.when(step == (3 * total) // 4)
            def _():
                ride.relay(rins, routs, rscr)

        body(*ins, *outs, *scr)

        @pl.when(step == total - 1)
        def _():
            ride.finish(rins, routs, rscr)

    res = pl.pallas_call(
        wrapped, name=name, grid=grid, in_specs=list(in_specs) + [ANY] * ri, out_specs=list(out_specs) + [ANY] * ro,
        out_shape=list(out_shape) + list(ride.out_shape),
        scratch_shapes=list(scratch_shapes) + list(ride.scratch_shapes),
        compiler_params=_params(vmem_mb, ("arbitrary",) * len(grid)),
    )(*operands, *ride.in_arrays)
    return res[:no], res[no:]


def _log_sig_pair(z):
    sp = jnp.log(1.0 + jnp.exp(-jnp.abs(z)))
    return -(jnp.maximum(z, 0.0) + sp), jnp.minimum(z, 0.0) - sp


def _hi_lo(m):
    hi = m.astype(BF)
    return hi, (m - hi.astype(F32)).astype(BF)


def _head_sums(x, h0):
    s0 = jnp.sum(jnp.where(h0, x, 0.0), axis=-1, keepdims=True)
    s1 = jnp.sum(jnp.where(h0, 0.0, x), axis=-1, keepdims=True)
    return jnp.where(h0, s0, s1)


def _sb_setup(q_ref, tq):
    lane = lax.broadcasted_iota(jnp.int32, (tq, 128), 1)
    h0 = lane < HEAD_LANES
    qs = q_ref[...] * SB_SCALE
    zero = jnp.zeros_like(qs)
    qst = jnp.concatenate([jnp.where(h0, qs, zero), jnp.where(h0, zero, qs)], axis=0)
    r = lax.broadcasted_iota(jnp.int32, (2 * tq, tq), 0)
    c = lax.broadcasted_iota(jnp.int32, (2 * tq, tq), 1)
    causal = c < jnp.where(r >= tq, r - tq, r)
    return h0, qst, causal


def _tri(tq, op):
    return op(lax.broadcasted_iota(jnp.int32, (tq, tq), 0), lax.broadcasted_iota(jnp.int32, (tq, tq), 1)).astype(BF)


def _sb_fwd(proj, hg, nb, s, tq, ride=None):
    t = nb * s
    tq = min(tq, s)
    nq = s // tq

    def body(q_ref, k_ref, v_ref, hg_ref, o_ref, tot_ref, mb_ref, nblk_ref, acc, cr):
        i = pl.program_id(2)
        h0, qst, causal = _sb_setup(q_ref, tq)
        tri_gt = _tri(tq, lambda r, c: r > c)
        acc[...] = jnp.zeros_like(acc)
        cr[...] = jnp.zeros_like(cr)

        def block(j, masked):
            start = pl.multiple_of(j * tq, tq)
            kj = k_ref[pl.ds(start, tq), :]
            vj = v_ref[pl.ds(start, tq), :]
            m, l = _log_sig_pair(_dot_bt(qst, kj))
            if masked:
                m = jnp.where(causal, m, 0.0)
            mh, ml = _hi_lo(m)
            a = jnp.exp(l + (_dot(mh, tri_gt) + _dot(ml, tri_gt)) + cr[...])
            if masked:
                a = jnp.where(causal, a, 0.0)
            acc[...] += _dot(a.astype(BF), vj)
            cnew = cr[...] + jnp.sum(m, axis=-1, keepdims=True)
            cr[...] = cnew
            return jnp.max(cnew)

        def cond(carry):
            return jnp.logical_and(carry[0] < i, carry[1] > SB_SKIP)

        def step(carry):
            return carry[0] + 1, block(i - 1 - carry[0], False)

        walked, _ = lax.while_loop(cond, step, (jnp.int32(0), block(i, True)))

        o = jnp.where(h0, acc[0:tq, :], acc[tq:2 * tq, :])
        o_ref[...] = o
        tot_ref[...] = jnp.where(h0, cr[0:tq, :], cr[tq:2 * tq, :])
        ro = lax.rsqrt(_head_sums(o * o, h0) * (1.0 / HEAD_LANES) + EPS)
        mb_ref[...] = (o * ro * hg_ref[...]).astype(BF)
        nblk_ref[...] = jnp.full((8, 128), walked.astype(F32))

    blk = lambda col0: pl.BlockSpec((tq, 128), lambda b, hp, i: (b * nq + i, col0 + hp))
    seq = lambda col0: pl.BlockSpec((s, 128), lambda b, hp, i: (b, col0 + hp))
    (o, tot, mb, nblk), rode = _ride_call(
        body, "sb_fwd", (nb, 4, nq),
        in_specs=[blk(8), seq(12), seq(16), pl.BlockSpec((1, 128), lambda b, hp, i: (0, 4 + hp))],
        out_specs=[blk(0), blk(0), blk(0), pl.BlockSpec((None, None, 8, 128), lambda b, hp, i: (b, hp, i, 0))],
        out_shape=[jax.ShapeDtypeStruct((t, 512), F32), jax.ShapeDtypeStruct((t, 512), F32),
                   jax.ShapeDtypeStruct((t, 512), BF), jax.ShapeDtypeStruct((nb, 4, nq * 8, 128), F32)],
        scratch_shapes=[pltpu.VMEM((2 * tq, 128), F32), pltpu.VMEM((2 * tq, 1), F32)],
        operands=(proj, proj, proj, hg), vmem_mb=40, ride=ride)
    return o, tot, mb, nblk, rode


def _sb_bwd(proj, o_sb, tot, nblk, dmerged, hg, nb, s, tq, ride=None):
    t = nb * s
    tq = min(tq, s)
    nq = s // tq

    def body(q_ref, k_ref, v_ref, o_ref, tot_ref, nblk_ref, dm_ref, hg_ref,
             dq_ref, dk_ref, dv_ref, dhg_ref, dk_acc, dv_acc, dq_acc, cm, cg):
        i = pl.program_id(2)
        h0, qst, causal = _sb_setup(q_ref, tq)
        tri_le = _tri(tq, lambda r, c: r <= c)
        tri_lt = _tri(tq, lambda r, c: r < c)

        @pl.when(i == 0)
        def _():
            dk_acc[...] = jnp.zeros_like(dk_acc)
            dv_acc[...] = jnp.zeros_like(dv_acc)
            dhg_ref[...] = jnp.zeros_like(dhg_ref)

        for ref in (dq_acc, cm, cg):
            ref[...] = jnp.zeros_like(ref)

        o = o_ref[...]
        ro = lax.rsqrt(_head_sums(o * o, h0) * (1.0 / HEAD_LANES) + EPS)
        oh = o * ro
        dm = dm_ref[...].astype(F32)
        dhg_ref[...] += jnp.sum(dm * oh, axis=0, keepdims=True)
        doh = dm * hg_ref[...]
        do = (ro * (doh - oh * (_head_sums(doh * oh, h0) * (1.0 / HEAD_LANES)))).astype(BF)
        zb = jnp.zeros_like(do)
        dost = jnp.concatenate([jnp.where(h0, do, zb), jnp.where(h0, zb, do)], axis=0)
        tots = jnp.concatenate([tot_ref[:, 0:1], tot_ref[:, HEAD_LANES:HEAD_LANES + 1]], axis=0)

        def block(j, masked):
            start = pl.multiple_of(j * tq, tq)
            kj = k_ref[pl.ds(start, tq), :]
            vj = v_ref[pl.ds(start, tq), :]
            m, l = _log_sig_pair(_dot_bt(qst, kj))
            if masked:
                m = jnp.where(causal, m, 0.0)
            mh, ml = _hi_lo(m)
            a = jnp.exp(l + (tots - cm[...] - (_dot(mh, tri_le) + _dot(ml, tri_le))))
            if masked:
                a = jnp.where(causal, a, 0.0)
            gm = a * _dot_bt(dost, vj)
            pp = cg[...] + _dot(gm.astype(BF), tri_lt)
            dz = gm - jnp.exp(l) * (gm + pp)
            if masked:
                dz = jnp.where(causal, dz, 0.0)
            dzb = dz.astype(BF)
            dq_acc[...] += _dot(dzb, kj)
            dk_acc[pl.ds(start, tq), :] += _dot_at(dzb, qst)
            dv_acc[pl.ds(start, tq), :] += _dot_at(a.astype(BF), dost)
            cm[...] += jnp.sum(m, axis=-1, keepdims=True)
            cg[...] += jnp.sum(gm, axis=-1, keepdims=True)

        def step(j, carry):
            block(j, False)
            return carry

        walked = jnp.clip(jnp.max(nblk_ref[...]).astype(jnp.int32), 0, i)
        lax.fori_loop(i - walked, i, step, 0)
        block(i, True)

        dq_ref[...] = (jnp.where(h0, dq_acc[0:tq, :], dq_acc[tq:2 * tq, :]) * SB_SCALE).astype(BF)

        @pl.when(i == nq - 1)
        def _():
            dk_ref[...] = dk_acc[...].astype(BF)
            dv_ref[...] = dv_acc[...].astype(BF)

    blk = lambda col0: pl.BlockSpec((tq, 128), lambda b, hp, i: (b * nq + i, col0 + hp))
    seq = lambda col0: pl.BlockSpec((s, 128), lambda b, hp, i: (b, col0 + hp))
    (dq, dk, dv, dhg), rode = _ride_call(
        body, "sb_bwd", (nb, 4, nq),
        in_specs=[blk(8), seq(12), seq(16), blk(0), blk(0),
                  pl.BlockSpec((None, None, 8, 128), lambda b, hp, i: (b, hp, i, 0)), blk(4),
                  pl.BlockSpec((1, 128), lambda b, hp, i: (0, 4 + hp))],
        out_specs=[blk(0), seq(0), seq(0), pl.BlockSpec((None, 1, 128), lambda b, hp, i: (b, 0, hp))],
        out_shape=[jax.ShapeDtypeStruct((t, 512), BF), jax.ShapeDtypeStruct((t, 512), BF),
                   jax.ShapeDtypeStruct((t, 512), BF), jax.ShapeDtypeStruct((nb, 1, 512), F32)],
        scratch_shapes=[pltpu.VMEM((s, 128), F32), pltpu.VMEM((s, 128), F32), pltpu.VMEM((2 * tq, 128), F32),
                        pltpu.VMEM((2 * tq, 1), F32), pltpu.VMEM((2 * tq, 1), F32)],
        operands=(proj, proj, proj, o_sb, tot, nblk, dmerged, hg), vmem_mb=40, ride=ride)
    return dq, dk, dv, dhg, rode


def _softmax_rows(sc):
    e = jnp.exp(sc - jnp.max(sc, axis=-1, keepdims=True))
    return e / jnp.sum(e, axis=-1, keepdims=True)


def _mix_cross_fwd(x, ma, mb, w_out, gc, w_cq, kv, w_co, s, tm):
    t, d = x.shape
    tm = min(tm, s)
    per = s // tm
    inv = 1.0 / math.sqrt(X_HEAD_DIM)

    def body(x_ref, ma_ref, mb_ref, wo_ref, gc_ref, wq_ref, kv_ref, wc_ref, h1_ref, h2_ref, hn_ref, oc_ref):
        h1 = x_ref[...] + _dot(ma_ref[...], wo_ref[0:512, :]) + _dot(mb_ref[...], wo_ref[512:1024, :])
        h1_ref[...] = h1
        hn = (h1 * _rs(h1) * gc_ref[...]).astype(BF)
        hn_ref[...] = hn
        qc = _dot(hn, wq_ref[...]).astype(BF)
        for h in range(X_HEADS):
            cols = slice(h * X_HEAD_DIM, (h + 1) * X_HEAD_DIM)
            kh = kv_ref[:, h * X_HEAD_DIM:(h + 1) * X_HEAD_DIM]
            vh = kv_ref[:, d + h * X_HEAD_DIM:d + (h + 1) * X_HEAD_DIM]
            p = _softmax_rows(_dot_bt(qc[:, cols], kh) * inv)
            oc_ref[:, cols] = _dot(p.astype(BF), vh).astype(BF)
        h2_ref[...] = h1 + _dot(oc_ref[...], wc_ref[...])

    row = lambda width: pl.BlockSpec((tm, width), lambda i: (i, 0))
    full = lambda a, b: pl.BlockSpec((a, b), lambda i: (0, 0))
    return pl.pallas_call(
        body, name="mix_cross_fwd", grid=(t // tm,),
        in_specs=[row(d), row(512), row(512), full(d, d), full(1, d), full(d, d),
                  pl.BlockSpec((N_MEM, 2 * d), lambda i: (i // per, 0)), full(d, d)],
        out_specs=[row(d), row(d), row(d), row(d)],
        out_shape=[jax.ShapeDtypeStruct((t, d), F32), jax.ShapeDtypeStruct((t, d), F32),
                   jax.ShapeDtypeStruct((t, d), BF), jax.ShapeDtypeStruct((t, d), BF)],
        compiler_params=_params(48, ("arbitrary",)),
    )(x, ma, mb, w_out, gc, w_cq, kv, w_co)


def _cross_bwd(dh2, h1, gc, w_cq, kv, w_co, s, tm):
    t, d = dh2.shape
    tm = min(tm, s)
    per = s // tm
    nb = t // s
    inv = 1.0 / math.sqrt(X_HEAD_DIM)

    def body(dh2_ref, h1_ref, gc_ref, wq_ref, kv_ref, wc_ref, dh1_ref, dqc_ref, dkv_ref, dgc_ref):
        i = pl.program_id(0)

        @pl.when(i == 0)
        def _():
            dgc_ref[...] = jnp.zeros_like(dgc_ref)

        @pl.when(i % per == 0)
        def _():
            dkv_ref[...] = jnp.zeros_like(dkv_ref)

        dh2 = dh2_ref[...]
        h1 = h1_ref[...]
        r = _rs(h1)
        h1h = h1 * r
        gcv = gc_ref[...]
        hn = (h1h * gcv).astype(BF)
        qc = _dot(hn, wq_ref[...]).astype(BF)
        do = _dot_bt(dh2.astype(BF), wc_ref[...]).astype(BF)
        for h in range(X_HEADS):
            cols = slice(h * X_HEAD_DIM, (h + 1) * X_HEAD_DIM)
            vcols = slice(d + h * X_HEAD_DIM, d + (h + 1) * X_HEAD_DIM)
            kh = kv_ref[:, cols]
            vh = kv_ref[:, vcols]
            p = _softmax_rows(_dot_bt(qc[:, cols], kh) * inv)
            dp = _dot_bt(do[:, cols], vh)
            ds = (p * (dp - jnp.sum(dp * p, axis=-1, keepdims=True)) * inv).astype(BF)
            dqc_ref[:, cols] = _dot(ds, kh).astype(BF)
            dkv_ref[:, cols] += _dot_at(ds, qc[:, cols])
            dkv_ref[:, vcols] += _dot_at(p.astype(BF), do[:, cols])
        dhn = _dot_bt(dqc_ref[...], wq_ref[...])
        dx, dg = _rms_bwd(dhn, h1h, r, gcv)
        dh1_ref[...] = dh2 + dx
        dgc_ref[...] += jnp.sum(dg, axis=0, keepdims=True)

    row = lambda width: pl.BlockSpec((tm, width), lambda i: (i, 0))
    full = lambda a, b: pl.BlockSpec((a, b), lambda i: (0, 0))
    kvspec = pl.BlockSpec((N_MEM, 2 * d), lambda i: (i // per, 0))
    return pl.pallas_call(
        body, name="cross_bwd", grid=(t // tm,),
        in_specs=[row(d), row(d), full(1, d), full(d, d), kvspec, full(d, d)],
        out_specs=[row(d), row(d), kvspec, full(1, d)],
        out_shape=[jax.ShapeDtypeStruct((t, d), F32), jax.ShapeDtypeStruct((t, d), BF),
                   jax.ShapeDtypeStruct((nb * N_MEM, 2 * d), F32), jax.ShapeDtypeStruct((1, d), F32)],
        compiler_params=_params(48, ("arbitrary",)),
    )(dh2, h1, gc, w_cq, kv, w_co)


def _mem_bwd(mem, gm, dkv, w_ckv, tm):
    t, d = mem.shape
    tm = min(tm, t)

    def body(mem_ref, dkv_ref, w_ref, dg_ref):
        @pl.when(pl.program_id(0) == 0)
        def _():
            dg_ref[...] = jnp.zeros_like(dg_ref)

        mv = mem_ref[...]
        dmn = _dot_bt(dkv_ref[...].astype(BF), w_ref[...])
        dg_ref[...] += jnp.sum(dmn * (mv * _rs(mv)), axis=0, keepdims=True)

    del gm
    return pl.pallas_call(
        body, name="mem_bwd", grid=(t // tm,),
        in_specs=[pl.BlockSpec((tm, d), lambda i: (i, 0)), pl.BlockSpec((tm, 2 * d), lambda i: (i, 0)),
                  pl.BlockSpec((d, 2 * d), lambda i: (0, 0))],
        out_specs=pl.BlockSpec((1, d), lambda i: (0, 0)),
        out_shape=jax.ShapeDtypeStruct((1, d), F32),
        compiler_params=_params(32, ("arbitrary",)),
    )(mem, dkv, w_ckv)


def _ffn_loss_fwd(h2, gf, w1, w2, gl, target, tm):
    t, d = h2.shape
    tm = min(tm, t)

    def body(h2_ref, gf_ref, w1_ref, w2_ref, gl_ref, tg_ref, hn_ref, f_ref, dh3_ref, dgl_ref, loss_ref):
        @pl.when(pl.program_id(0) == 0)
        def _():
            dgl_ref[...] = jnp.zeros_like(dgl_ref)
            loss_ref[...] = jnp.zeros_like(loss_ref)

        h2 = h2_ref[...]
        hn = (h2 * _rs(h2) * gf_ref[...]).astype(BF)
        hn_ref[...] = hn
        h3 = h2
        for c in range(4):
            f = jnp.maximum(_dot(hn, w1_ref[c]), 0.0)
            f_ref[:, c * 1024:(c + 1) * 1024] = f.astype(BF)
            h3 = h3 + _dot((f * f).astype(BF), w2_ref[c])
        r3 = _rs(h3)
        yh = h3 * r3
        glv = gl_ref[...]
        e = yh * glv - tg_ref[...]
        loss_ref[...] += 0.5 * jnp.sum(jnp.sum(e * e, axis=-1, keepdims=True) * (1.0 / d), axis=0, keepdims=True)
        dy = e * (1.0 / d)
        dx, dg = _rms_bwd(dy, yh, r3, glv)
        dh3_ref[...] = dx
        dgl_ref[...] += jnp.sum(dg, axis=0, keepdims=True)

    row = lambda width: pl.BlockSpec((tm, width), lambda i: (i, 0))
    return pl.pallas_call(
        body, name="ffn_loss_fwd", grid=(t // tm,),
        in_specs=[row(d), pl.BlockSpec((1, d), lambda i: (0, 0)), pl.BlockSpec((4, d, 1024), lambda i: (0, 0, 0)),
                  pl.BlockSpec((4, 1024, d), lambda i: (0, 0, 0)), pl.BlockSpec((1, d), lambda i: (0, 0)), row(d)],
        out_specs=[row(d), row(D_FF), row(d), pl.BlockSpec((1, d), lambda i: (0, 0)),
                   pl.BlockSpec((1, 1), lambda i: (0, 0))],
        out_shape=[jax.ShapeDtypeStruct((t, d), BF), jax.ShapeDtypeStruct((t, D_FF), BF),
                   jax.ShapeDtypeStruct((t, d), F32), jax.ShapeDtypeStruct((1, d), F32),
                   jax.ShapeDtypeStruct((1, 1), F32)],
        compiler_params=_params(56, ("arbitrary",)),
    )(h2, gf, w1, w2, gl, target)


def _ffn_bwd(dh3, f, h2, gf, w1, w2, tm):
    t, d = h2.shape
    tm = min(tm, t)

    def body(dh3_ref, f_ref, h2_ref, gf_ref, w1_ref, w2_ref, dh2_ref, dpre_ref, dgf_ref):
        @pl.when(pl.program_id(0) == 0)
        def _():
            dgf_ref[...] = jnp.zeros_like(dgf_ref)

        dh3 = dh3_ref[...]
        dh3b = dh3.astype(BF)
        dhn = jnp.zeros((tm, d), F32)
        for c in range(4):
            cols = slice(c * 1024, (c + 1) * 1024)
            dpre = (_dot_bt(dh3b, w2_ref[c]) * (2.0 * f_ref[:, cols].astype(F32))).astype(BF)
            dpre_ref[:, cols] = dpre
            dhn = dhn + _dot_bt(dpre, w1_ref[c])
        h2 = h2_ref[...]
        r = _rs(h2)
        dx, dg = _rms_bwd(dhn, h2 * r, r, gf_ref[...])
        dh2_ref[...] = dh3 + dx
        dgf_ref[...] += jnp.sum(dg, axis=0, keepdims=True)

    row = lambda width: pl.BlockSpec((tm, width), lambda i: (i, 0))
    return pl.pallas_call(
        body, name="ffn_bwd", grid=(t // tm,),
        in_specs=[row(d), row(D_FF), row(d), pl.BlockSpec((1, d), lambda i: (0, 0)),
                  pl.BlockSpec((4, d, 1024), lambda i: (0, 0, 0)), pl.BlockSpec((4, 1024, d), lambda i: (0, 0, 0))],
        out_specs=[row(d), row(D_FF), pl.BlockSpec((1, d), lambda i: (0, 0))],
        out_shape=[jax.ShapeDtypeStruct((t, d), F32), jax.ShapeDtypeStruct((t, D_FF), BF),
                   jax.ShapeDtypeStruct((1, d), F32)],
        compiler_params=_params(56, ("arbitrary",)),
    )(dh3, f, h2, gf, w1, w2)


def _in_bwd(dproj, dh1, x, g, w_in, tm, ride=None):
    t, d = x.shape
    n = w_in.shape[1]
    tm = min(tm, t)

    def body(dp_ref, dh1_ref, x_ref, g_ref, w_ref, dx_ref, dg_ref):
        @pl.when(pl.program_id(0) == 0)
        def _():
            dg_ref[...] = jnp.zeros_like(dg_ref)

        dxn = _dot_bt(dp_ref[...], w_ref[...])
        xv = x_ref[...]
        r = _rs(xv)
        dx, dg = _rms_bwd(dxn, xv * r, r, g_ref[...])
        dx_ref[...] = dh1_ref[...] + dx
        dg_ref[...] += jnp.sum(dg, axis=0, keepdims=True)

    row = lambda width: pl.BlockSpec((tm, width), lambda i: (i, 0))
    (dx, dg), rode = _ride_call(
        body, "in_bwd", (t // tm,),
        in_specs=[row(n), row(d), row(d), pl.BlockSpec((1, d), lambda i: (0, 0)),
                  pl.BlockSpec((d, n), lambda i: (0, 0))],
        out_specs=[row(d), pl.BlockSpec((1, d), lambda i: (0, 0))],
        out_shape=[jax.ShapeDtypeStruct((t, d), F32), jax.ShapeDtypeStruct((1, d), F32)],
        scratch_shapes=[], operands=(dproj, dh1, x, g, w_in), vmem_mb=48, ride=ride)
    return dx, dg, rode


class _GradReduce:
    def __init__(self, c_idx):
        self.c_idx = c_idx
        self.sums = {}

    def sibling(self, slabs):
        return _SiblingExchange(slabs)

    def chip(self, names, slabs, recv):
        for k, a, r in zip(names, slabs, recv):
            self.sums[k] = _chip_sum(a, r, self.c_idx, "chip_sum_" + k)
        return _ChipExchange([self.sums[k] for k in names])


def _full_weights(gathered):
    d = D_MODEL
    out = {}
    for k, a in gathered.items():
        if k in ("w_in", "w_ckv", "w_ff1"):
            out[k] = a.transpose(1, 0, 2).reshape(d, -1)
        else:
            out[k] = a.reshape(-1, d)
    return out


def _slabs(a):
    return a.reshape(N_DEV, -1, a.shape[-1])


def _local_step(x, mem, target, small, big, nb, s, tq=256, gather_rest=None, reduce=None):
    d = D_MODEL
    g_mix, g_v, w_sp, b_sp, g_head, g_cross, g_mem, g_ffn, g_fin = (
        small[k] for k in ("norm_mix_g", "gm_v_norm_g", "w_spatial", "b_spatial", "head_norm_g", "norm_cross_g",
                           "norm_mem_g", "norm_ffn_g", "norm_final_g"))
    tri = jnp.tril(jnp.ones((CHUNK, CHUNK), dtype=bool))
    w_sp_m = jnp.where(tri[None], w_sp, 0.0)
    wt = w_sp_m.astype(BF)
    wtt = jnp.swapaxes(w_sp_m, 1, 2).astype(BF)
    bb = jnp.broadcast_to(b_sp[:, :, None], (GM_GROUPS, CHUNK, CHUNK))
    hg_a = g_head[:, :GM_WIDTH]

    proj, xn = _norm_matmul(x, g_mix, big["w_in"], 512, "in_proj")
    ma = _gmlp_fwd(proj, g_v, wt, bb, hg_a, 512)
    o_sb, tot, mb, nblk, gathered = _sb_fwd(proj, g_head, nb, s, tq, ride=gather_rest)
    if gather_rest is not None:
        big = dict(big, **_full_weights(dict(zip(BIG[1:], gathered))))
    w1c = big["w_ff1"].reshape(d, 4, 1024).transpose(1, 0, 2)
    w2c = big["w_ff2"].reshape(4, 1024, d)
    kv, memn = _norm_matmul(mem, g_mem, big["w_ckv"], 512, "mem_proj")
    h1, h2, hn, oc = _mix_cross_fwd(x, ma, mb, big["w_out"], g_cross, big["w_cq"], kv, big["w_co"], s, 512)
    hn2, f, dh3, d_fin, loss = _ffn_loss_fwd(h2, g_ffn, w1c, w2c, g_fin, target, 256)

    gbig = {}
    dh2, dpre, d_ffn = _ffn_bwd(dh3, f, h2, g_ffn, w1c, w2c, 256)
    gbig["w_ff2"] = _slabs(_wgrad(f, dh3, 1024, 1024, "wgrad_ff2", square_a=True))
    gbig["w_ff1"] = _slabs(_wgrad(hn2, dpre, 1024, 1024, "wgrad_ff1", col_shards=4))
    dh1, dqc, dkv, d_cross = _cross_bwd(dh2, h1, g_cross, big["w_cq"], kv, big["w_co"], s, 512)
    gbig["w_co"] = _slabs(_wgrad(oc, dh2, 1024, 1024, "wgrad_co"))
    gbig["w_cq"] = _slabs(_wgrad(hn, dqc, 1024, 1024, "wgrad_cq"))
    gbig["w_ckv"] = _slabs(_wgrad(memn, dkv, 512, 1024, "wgrad_ckv", col_shards=4))
    d_mem = _mem_bwd(mem, g_mem, dkv, big["w_ckv"], 512)
    dmerged = _matmul_bt(dh1, big["w_out"], 512, "out_bwd")
    gbig["w_out"] = _slabs(_wgrad(jnp.concatenate([ma, mb], axis=1), dh1, 1024, 1024, "wgrad_out"))
    rest = BIG[1:]
    ride = reduce.sibling([gbig[k] for k in rest]) if reduce else None
    dugv, d_wsp, d_bb, d_gv, d_hga, recv = _gmlp_bwd(proj, dmerged, g_v, wt, wtt, bb, hg_a, 512, ride=ride)
    ride = reduce.chip(rest, [gbig[k] for k in rest], recv) if reduce else None
    dq, dk, dv, d_hgb, parts_rest = _sb_bwd(proj, o_sb, tot, nblk, dmerged, g_head, nb, s, tq, ride=ride)
    dproj = jnp.concatenate([dugv, dq, dk, dv], axis=1)
    gbig["w_in"] = _slabs(_wgrad(xn, dproj, 640, 1024, "wgrad_in", col_shards=4))
    ride = None
    if reduce:
        recv = _run_exchange(reduce.sibling([gbig["w_in"]]), "grad_sibling_exchange_w_in")
        ride = reduce.chip(["w_in"], [gbig["w_in"]], recv)
    grad_x, d_mix, parts_in = _in_bwd(dproj, dh1, x, g_mix, big["w_in"], 512, ride=ride)
    parts = dict(zip(BIG, list(parts_in) + list(parts_rest)))

    gsmall = {
        "norm_mix_g": d_mix, "gm_v_norm_g": d_gv, "w_spatial": d_wsp, "b_spatial": d_bb[:, :, 0],
        "head_norm_g": jnp.concatenate([d_hga, jnp.sum(d_hgb, axis=0)], axis=1), "norm_cross_g": d_cross,
        "norm_mem_g": d_mem, "norm_ffn_g": d_ffn, "norm_final_g": d_fin,
    }
    return loss, grad_x, gsmall, gbig, parts


BIG = ("w_in", "w_out", "w_cq", "w_ckv", "w_co", "w_ff1", "w_ff2")
SMALL = ("norm_mix_g", "gm_v_norm_g", "w_spatial", "b_spatial", "head_norm_g", "norm_cross_g", "norm_mem_g",
         "norm_ffn_g", "norm_final_g")


def _local_copies_start(srcs, stages, sems):
    loads = [pltpu.make_async_copy(src, stage, sems.at[w]) for w, (src, stage) in enumerate(zip(srcs, stages))]
    for ld in loads:
        ld.start()
    return loads


def _local_copies_finish(loads, stages, dsts, sems):
    stores = []
    for w, (ld, stage, dst) in enumerate(zip(loads, stages, dsts)):
        ld.wait()
        st = pltpu.make_async_copy(stage, dst, sems.at[w])
        st.start()
        stores.append(st)
    for st in stores:
        st.wait()


def _chip_sum(slabs, recv, c_idx, name):
    _, r, cw = slabs.shape
    tr = min(r, 256)

    def body(c_ref, a_ref, b_ref, o_ref):
        del c_ref
        o_ref[...] = (a_ref[...] + b_ref[...]).astype(BF)

    return pl.pallas_call(
        body, name=name,
        grid_spec=pltpu.PrefetchScalarGridSpec(
            num_scalar_prefetch=1, grid=(N_CHIPS, r // tr),
            in_specs=[pl.BlockSpec((None, tr, cw), lambda p, i, c_ref: (2 * p + c_ref[0], i, 0)),
                      pl.BlockSpec((None, tr, cw), lambda p, i, c_ref: (p, i, 0))],
            out_specs=pl.BlockSpec((None, tr, cw), lambda p, i, c_ref: (p, i, 0))),
        out_shape=jax.ShapeDtypeStruct((N_CHIPS, r, cw), BF),
        compiler_params=_params(32, ("arbitrary", "arbitrary")),
    )(c_idx, slabs, recv)


def _sum4(sums, parts, q_idx, name):
    _, r, cw = parts.shape
    tr = min(r, 256)

    def body(q_ref, own_ref, a_ref, b_ref, c_ref, o_ref):
        del q_ref
        o_ref[...] = ((own_ref[...].astype(F32) + a_ref[...].astype(F32)) + b_ref[...].astype(F32)) + c_ref[
            ...].astype(F32)

    spec = lambda k: pl.BlockSpec((None, tr, cw), lambda i, q_ref: ((q_ref[0] + k) % N_CHIPS, i, 0))
    return pl.pallas_call(
        body, name=name,
        grid_spec=pltpu.PrefetchScalarGridSpec(
            num_scalar_prefetch=1, grid=(r // tr,), in_specs=[spec(0), spec(1), spec(2), spec(3)],
            out_specs=pl.BlockSpec((tr, cw), lambda i, q_ref: (i, 0))),
        out_shape=jax.ShapeDtypeStruct((r, cw), F32),
        compiler_params=_params(32, ("arbitrary",)),
    )(q_idx, sums, parts, parts, parts)


def _half_exchange(halves):
    n = len(halves)

    def body(*refs):
        ins, outs, stages = refs[:n], refs[n:2 * n], refs[2 * n:3 * n]
        send_sems, recv_sems, ld_sems, st_sems = refs[3 * n:]
        x, y, c = lax.axis_index("x"), lax.axis_index("y"), lax.axis_index("c")
        loads = _local_copies_start(ins, stages, ld_sems)
        copies = []
        for w in range(n):
            cp = pltpu.make_async_remote_copy(
                src_ref=ins[w], dst_ref=outs[w].at[c], send_sem=send_sems.at[w], recv_sem=recv_sems.at[w],
                device_id=(x, y, 1 - c), device_id_type=MESH)
            cp.start()
            copies.append(cp)
        _local_copies_finish(loads, stages, [outs[w].at[c] for w in range(n)], st_sems)
        for cp in copies:
            cp.wait()

    return pl.pallas_call(
        body, name="grad_half_exchange",
        in_specs=[ANY] * n, out_specs=[ANY] * n,
        out_shape=[jax.ShapeDtypeStruct((2,) + a.shape, a.dtype) for a in halves],
        scratch_shapes=[pltpu.VMEM(a.shape, a.dtype) for a in halves] + [
            pltpu.SemaphoreType.DMA((n,)), pltpu.SemaphoreType.DMA((n,)),
            pltpu.SemaphoreType.DMA((n,)), pltpu.SemaphoreType.DMA((n,))],
        compiler_params=_params(24),
    )(*halves)


def _small_all_reduce(packed):
    rows = packed.shape[0]

    def body(in_ref, out_ref, buf, send_sems, recv_sems):
        x, y, c = lax.axis_index("x"), lax.axis_index("y"), lax.axis_index("c")
        me = 4 * x + 2 * y + c
        buf[me] = in_ref[...]
        copies = []
        for k in range(1, N_DEV):
            bx, by, bc = (k >> 2) & 1, (k >> 1) & 1, k & 1
            peer = (x ^ bx, y ^ by, c ^ bc)
            cp = pltpu.make_async_remote_copy(
                src_ref=in_ref, dst_ref=buf.at[me], send_sem=send_sems.at[k - 1], recv_sem=recv_sems.at[k - 1],
                device_id=peer, device_id_type=MESH)
            cp.start()
            copies.append(cp)
        for cp in copies:
            cp.wait()
        acc = buf[0]
        for dev in range(1, N_DEV):
            acc = acc + buf[dev]
        out_ref[...] = acc

    return pl.pallas_call(
        body, name="small_all_reduce",
        in_specs=[pl.BlockSpec(memory_space=pltpu.VMEM)], out_specs=pl.BlockSpec(memory_space=pltpu.VMEM),
        out_shape=jax.ShapeDtypeStruct(packed.shape, F32),
        scratch_shapes=[pltpu.VMEM((N_DEV, rows, 128), F32), pltpu.SemaphoreType.DMA((N_DEV - 1,)),
                        pltpu.SemaphoreType.DMA((N_DEV - 1,))],
        compiler_params=_params(16),
    )(packed)


def _adamw(g, w, m, v, name):
    r, cw = g.shape
    tr = r if r <= 1024 else 256
    assert r % tr == 0

    def body(g_ref, w_ref, m_ref, v_ref, d_ref, nm_ref, nv_ref):
        gv = g_ref[...]
        nm = ADAM_B1 * m_ref[...] + (1.0 - ADAM_B1) * gv
        nv = ADAM_B2 * v_ref[...] + (1.0 - ADAM_B2) * (gv * gv)
        m_hat = nm / (1.0 - ADAM_B1 ** ADAM_STEP)
        v_hat = nv / (1.0 - ADAM_B2 ** ADAM_STEP)
        d_ref[...] = -ADAM_LR * (m_hat / (jnp.sqrt(v_hat) + ADAM_EPS) + ADAM_WD * w_ref[...])
        nm_ref[...] = nm
        nv_ref[...] = nv

    spec = pl.BlockSpec((tr, cw), lambda i: (i, 0))
    return pl.pallas_call(
        body, name=name, grid=(r // tr,),
        in_specs=[spec] * 4, out_specs=[spec] * 3,
        out_shape=[jax.ShapeDtypeStruct((r, cw), F32)] * 3,
        compiler_params=_params(32, ("arbitrary",)),
    )(g, w, m, v)


def _small_params(args):
    small = {k: args[k].reshape(1, -1) for k in SMALL}
    small["w_spatial"] = args["w_spatial"][0]
    small["b_spatial"] = args["b_spatial"][0]
    return small


def _pack(parts, rows):
    flat = jnp.concatenate([p.reshape(-1).astype(F32) for p in parts])
    return jnp.pad(flat, (0, rows * 128 - flat.shape[0])).reshape(rows, 128)


def _unpack(packed, shapes):
    flat = packed.reshape(-1)
    out, off = [], 0
    for shp in shapes:
        size = math.prod(shp)
        out.append(flat[off:off + size].reshape(shp))
        off += size
    return out


def kernel(x, mem, norm_mix_g, w_in, gm_v_norm_g, w_spatial, b_spatial, head_norm_g, w_out, norm_cross_g, norm_mem_g, w_cq, w_ckv, w_co, norm_ffn_g, w_ff1, w_ff2, norm_final_g, loss_target, m_norm_mix_g, m_w_in, m_gm_v_norm_g, m_w_spatial, m_b_spatial, m_head_norm_g, m_w_out, m_norm_cross_g, m_norm_mem_g, m_w_cq, m_w_ckv, m_w_co, m_norm_ffn_g, m_w_ff1, m_w_ff2, m_norm_final_g, v_norm_mix_g, v_w_in, v_gm_v_norm_g, v_w_spatial, v_b_spatial, v_head_norm_g, v_w_out, v_norm_cross_g, v_norm_mem_g, v_w_cq, v_w_ckv, v_w_co, v_norm_ffn_g, v_w_ff1, v_w_ff2, v_norm_final_g):
    args = dict(locals())
    d = D_MODEL
    nb, s, _ = x.shape
    c_idx = lax.axis_index("c").astype(jnp.int32).reshape(1)
    q_idx = (2 * lax.axis_index("x") + lax.axis_index("y")).astype(jnp.int32).reshape(1)
    rest = BIG[1:]

    shards = {k: args[k][0].astype(BF) for k in BIG}
    big = _full_weights({"w_in": _run_exchange(_GatherExchange([shards["w_in"]]), "all_gather_w_in")[0]})
    gather_rest = _GatherExchange([shards[k] for k in rest])

    reduce = _GradReduce(c_idx)
    loss, grad_x, gsmall, _, parts = _local_step(
        x.reshape(nb * s, d), mem.reshape(nb * N_MEM, d), loss_target.reshape(nb * s, d), _small_params(args), big,
        nb, s, gather_rest=gather_rest, reduce=reduce)
    halves = [_sum4(reduce.sums[k], parts[k], q_idx, "sum4_" + k) for k in BIG]
    both = _half_exchange(halves)

    out = {"grad_x": grad_x.reshape(nb, s, d)}
    for k, g2 in zip(BIG, both):
        shp = args[k].shape
        g = g2.reshape(shp[1], shp[2])
        dl, nm, nv = _adamw(g, args[k][0], args["m_" + k][0], args["v_" + k][0], "adamw_" + k)
        out["grad_" + k], out["delta_" + k], out["new_m_" + k], out["new_v_" + k] = (
            a.reshape(shp) for a in (g, dl, nm, nv))

    shapes = [args[k].shape for k in SMALL]
    n_small = sum(math.prod(sh) for sh in shapes)
    rows = -(-(n_small + 1) // 1024) * 8
    reduced = _small_all_reduce(_pack([gsmall[k] for k in SMALL] + [loss], rows))
    dl, nm, nv = _adamw(reduced, _pack([args[k] for k in SMALL], rows), _pack([args["m_" + k] for k in SMALL], rows),
                        _pack([args["v_" + k] for k in SMALL], rows), "adamw_small")
    for name, arr in (("grad_", reduced), ("delta_", dl), ("new_m_", nm), ("new_v_", nv)):
        for k, a in zip(SMALL, _unpack(arr, shapes)):
            out[name + k] = a
    out["loss"] = reduced.reshape(-1)[n_small]

    names = ["norm_mix_g", "w_in", "gm_v_norm_g", "w_spatial", "b_spatial", "head_norm_g", "w_out", "norm_cross_g",
             "norm_mem_g", "w_cq", "w_ckv", "w_co", "norm_ffn_g", "w_ff1", "w_ff2", "norm_final_g"]
    return (out["loss"], out["grad_x"], *[out["grad_" + k] for k in names], *[out["delta_" + k] for k in names],
            *[out["new_m_" + k] for k in names], *[out["new_v_" + k] for k in names])
```

```python
import functools
import math

import jax
import jax.numpy as jnp
from jax import lax
from jax.experimental import pallas as pl
from jax.experimental.pallas import tpu as pltpu

F32 = jnp.float32
BF = jnp.bfloat16

EPS = 1e-6
D_MODEL = 1024
CHUNK = 128
GM_GROUPS = 4
GM_WIDTH = 512
SB_WIDTH = 512
HEAD_LANES = 64
SB_SCALE = 0.125
SB_SKIP = -104.0
X_HEADS = 4
X_HEAD_DIM = 256
N_MEM = 256
D_FF = 4096
IN_COLS = 2560
N_CHIPS = 4
N_DEV = 8

ADAM_LR = 0.001
ADAM_B1 = 0.9
ADAM_B2 = 0.999
ADAM_EPS = 1e-08
ADAM_WD = 0.01
ADAM_STEP = 10

V7X_VMEM_BYTES = 64 * 1024 * 1024
MESH = pl.DeviceIdType.MESH
ANY = pl.BlockSpec(memory_space=pl.ANY)

GELU_C = math.sqrt(2.0 / math.pi)
GELU_A = 0.044715


def _params(vmem_mb, sem=None):
    assert vmem_mb * 1024 * 1024 <= V7X_VMEM_BYTES
    return pltpu.CompilerParams(vmem_limit_bytes=vmem_mb * 1024 * 1024, dimension_semantics=sem)


def _dot(a, b):
    return jnp.dot(a, b, preferred_element_type=F32)


def _dot_bt(a, b):
    return lax.dot_general(a, b, (((1,), (1,)), ((), ())), preferred_element_type=F32)


def _dot_at(a, b):
    return lax.dot_general(a, b, (((0,), (0,)), ((), ())), preferred_element_type=F32)


def _gelu(x):
    t = jnp.tanh(GELU_C * (x + GELU_A * x * x * x))
    return 0.5 * x * (1.0 + t)


def _gelu_and_grad(x):
    x2 = x * x
    t = jnp.tanh(GELU_C * (x + GELU_A * x2 * x))
    h = 0.5 * (1.0 + t)
    return x * h, h + 0.5 * x * (1.0 - t * t) * (GELU_C * (1.0 + 3.0 * GELU_A * x2))


def _rs(x):
    return lax.rsqrt(jnp.mean(x * x, axis=-1, keepdims=True) + EPS)


def _rms_bwd(dxn, xhat, r, g):
    dxh = dxn * g
    dx = r * (dxh - xhat * jnp.mean(dxh * xhat, axis=-1, keepdims=True))
    return dx, dxn * xhat


def _norm_matmul(x, g, w, tm, name):
    t, d = x.shape
    n = w.shape[1]
    tm = min(tm, t)

    def body(x_ref, g_ref, w_ref, out_ref, xn_ref):
        xv = x_ref[...]
        xn = (xv * _rs(xv) * g_ref[...]).astype(BF)
        xn_ref[...] = xn
        out_ref[...] = _dot(xn, w_ref[...]).astype(out_ref.dtype)

    return pl.pallas_call(
        body, name=name, grid=(t // tm,),
        in_specs=[pl.BlockSpec((tm, d), lambda i: (i, 0)), pl.BlockSpec((1, d), lambda i: (0, 0)),
                  pl.BlockSpec((d, n), lambda i: (0, 0))],
        out_specs=[pl.BlockSpec((tm, n), lambda i: (i, 0)), pl.BlockSpec((tm, d), lambda i: (i, 0))],
        out_shape=[jax.ShapeDtypeStruct((t, n), BF), jax.ShapeDtypeStruct((t, d), BF)],
        compiler_params=_params(48, ("arbitrary",)),
    )(x, g, w)


def _wgrad(a, g, tn, tk, name, square_a=False, col_shards=1):
    t, m = a.shape
    n = g.shape[1]
    tk = min(tk, t)
    tm = min(m, 1024)
    ns = n // col_shards
    assert ns % tn == 0 and m % tm == 0
    per = ns // tn
    nk = t // tk

    def body(a_ref, g_ref, o_ref):
        k = pl.program_id(2)

        @pl.when(k == 0)
        def _():
            o_ref[...] = jnp.zeros_like(o_ref)

        av = a_ref[...]
        if square_a:
            af = av.astype(F32)
            av = af * af
        o_ref[...] += _dot_at(av.astype(BF), g_ref[...].astype(BF))

    return pl.pallas_call(
        body, name=name, grid=(m // tm, n // tn, nk),
        in_specs=[pl.BlockSpec((tk, tm), lambda i, j, k: (k, i)), pl.BlockSpec((tk, tn), lambda i, j, k: (k, j))],
        out_specs=pl.BlockSpec((None, tm, tn), lambda i, j, k: (j // per, i, j % per)),
        out_shape=jax.ShapeDtypeStruct((col_shards, m, ns), F32),
        compiler_params=_params(48, ("arbitrary", "arbitrary", "arbitrary")),
    )(a, g)


def _matmul_bt(a, w, tm, name):
    t, n = a.shape
    k = w.shape[0]
    tm = min(tm, t)

    def body(a_ref, w_ref, o_ref):
        o_ref[...] = _dot_bt(a_ref[...].astype(BF), w_ref[...]).astype(o_ref.dtype)

    return pl.pallas_call(
        body, name=name, grid=(t // tm,),
        in_specs=[pl.BlockSpec((tm, n), lambda i: (i, 0)), pl.BlockSpec((k, n), lambda i: (0, 0))],
        out_specs=pl.BlockSpec((tm, k), lambda i: (i, 0)),
        out_shape=jax.ShapeDtypeStruct((t, k), BF),
        compiler_params=_params(32, ("arbitrary",)),
    )(a, w)


def _gmlp_fwd(proj, gg, wt, bb, hg, tm):
    t = proj.shape[0]
    tm = min(tm, t)

    def body(u_ref, v_ref, gg_ref, wt_ref, bb_ref, hg_ref, out_ref):
        for cc in range(tm // CHUNK):
            rows = slice(cc * CHUNK, (cc + 1) * CHUNK)
            for g in range(GM_GROUPS):
                cols = slice(g * 128, (g + 1) * 128)
                u = _gelu(u_ref[rows, cols].astype(F32))
                gv = _gelu(v_ref[rows, cols].astype(F32))
                vn = gv * _rs(gv) * gg_ref[:, cols]
                mixed = _dot(wt_ref[g], vn.astype(BF)) + bb_ref[g]
                a = u * mixed
                out_ref[rows, cols] = (a * _rs(a) * hg_ref[:, cols]).astype(BF)

    return pl.pallas_call(
        body, name="gmlp_fwd", grid=(t // tm,),
        in_specs=[pl.BlockSpec((tm, 512), lambda i: (i, 0)), pl.BlockSpec((tm, 512), lambda i: (i, 1)),
                  pl.BlockSpec((1, 512), lambda i: (0, 0)), pl.BlockSpec((4, 128, 128), lambda i: (0, 0, 0)),
                  pl.BlockSpec((4, 128, 128), lambda i: (0, 0, 0)), pl.BlockSpec((1, 512), lambda i: (0, 0))],
        out_specs=pl.BlockSpec((tm, 512), lambda i: (i, 0)),
        out_shape=jax.ShapeDtypeStruct((t, 512), BF),
        compiler_params=_params(32, ("arbitrary",)),
    )(proj, proj, gg, wt, bb, hg)


def _gmlp_bwd(proj, dmerged, gg, wt, wtt, bb, hg, tm, ride=None):
    t = proj.shape[0]
    tm = min(tm, t)
    nsteps = t // tm

    def body(u_ref, v_ref, dm_ref, gg_ref, wt_ref, wtt_ref, bb_ref, hg_ref,
             dp_ref, dw_ref, db_ref, dgg_ref, dhg_ref):
        i = pl.program_id(0)

        @pl.when(i == 0)
        def _():
            dw_ref[...] = jnp.zeros_like(dw_ref)
            db_ref[...] = jnp.zeros_like(db_ref)
            dgg_ref[...] = jnp.zeros_like(dgg_ref)
            dhg_ref[...] = jnp.zeros_like(dhg_ref)

        for cc in range(tm // CHUNK):
            rows = slice(cc * CHUNK, (cc + 1) * CHUNK)
            for g in range(GM_GROUPS):
                cols = slice(g * 128, (g + 1) * 128)
                up = u_ref[rows, cols].astype(F32)
                gp = v_ref[rows, cols].astype(F32)
                u, u_grad = _gelu_and_grad(up)
                gv, gv_grad = _gelu_and_grad(gp)
                rv = _rs(gv)
                gvh = gv * rv
                ggv = gg_ref[:, cols]
                vnb = (gvh * ggv).astype(BF)
                mixed = _dot(wt_ref[g], vnb) + bb_ref[g]
                a = u * mixed
                ra = _rs(a)
                ah = a * ra
                dm = dm_ref[rows, cols].astype(F32)
                dhg_ref[:, cols] += jnp.sum(dm * ah, axis=0, keepdims=True)
                dah = dm * hg_ref[:, cols]
                da = ra * (dah - ah * jnp.mean(dah * ah, axis=-1, keepdims=True))
                du = da * mixed
                dmix = da * u
                db_ref[g] += dmix
                dmb = dmix.astype(BF)
                dw_ref[g] += _dot_bt(dmb, vnb)
                dvn = _dot(wtt_ref[g], dmb)
                dgg_ref[:, cols] += jnp.sum(dvn * gvh, axis=0, keepdims=True)
                dgh = dvn * ggv
                dgv = rv * (dgh - gvh * jnp.mean(dgh * gvh, axis=-1, keepdims=True))
                dp_ref[rows, cols] = (du * u_grad).astype(BF)
                dp_ref[rows, 512 + g * 128:512 + (g + 1) * 128] = (dgv * gv_grad).astype(BF)

        @pl.when(i == nsteps - 1)
        def _():
            r = lax.broadcasted_iota(jnp.int32, (CHUNK, CHUNK), 0)
            c = lax.broadcasted_iota(jnp.int32, (CHUNK, CHUNK), 1)
            for g in range(GM_GROUPS):
                dw_ref[g] = jnp.where(c <= r, dw_ref[g], 0.0)
                db_ref[g] = jnp.broadcast_to(jnp.sum(db_ref[g], axis=-1, keepdims=True), (CHUNK, CHUNK))

    small = lambda shape: pl.BlockSpec(shape, lambda i: (0,) * len(shape))
    res, rode = _ride_call(
        body, "gmlp_bwd", (nsteps,),
        in_specs=[pl.BlockSpec((tm, 512), lambda i: (i, 0)), pl.BlockSpec((tm, 512), lambda i: (i, 1)),
                  pl.BlockSpec((tm, 512), lambda i: (i, 0)), small((1, 512)), small((4, 128, 128)),
                  small((4, 128, 128)), small((4, 128, 128)), small((1, 512))],
        out_specs=[pl.BlockSpec((tm, 1024), lambda i: (i, 0)), small((4, 128, 128)), small((4, 128, 128)),
                   small((1, 512)), small((1, 512))],
        out_shape=[jax.ShapeDtypeStruct((t, 1024), BF), jax.ShapeDtypeStruct((4, 128, 128), F32),
                   jax.ShapeDtypeStruct((4, 128, 128), F32), jax.ShapeDtypeStruct((1, 512), F32),
                   jax.ShapeDtypeStruct((1, 512), F32)],
        scratch_shapes=[], operands=(proj, proj, dmerged, gg, wt, wtt, bb, hg), vmem_mb=32, ride=ride)
    return (*res, rode)


def _other_chips(x, y):
    return ((1 - x, y), (x, 1 - y), (1 - x, 1 - y))


class _GatherExchange:
    def __init__(self, shards):
        n = len(shards)
        self.n = n
        self.in_arrays = list(shards)
        self.out_shape = [jax.ShapeDtypeStruct((N_CHIPS,) + a.shape, a.dtype) for a in shards]
        self.half_rows = [a.shape[0] // 2 for a in shards]
        sems = lambda k: pltpu.SemaphoreType.DMA((k,))
        self.scratch_shapes = [pltpu.VMEM(a.shape, a.dtype) for a in shards] + [
            sems(3 * n), sems(3 * n), sems(3 * n), sems(3 * n), sems(n), sems(n)]

    def _copies(self, ins, outs, scr):
        n = self.n
        stages, (ici_send, ici_recv, d2d_send, d2d_recv, ld_sems, st_sems) = scr[:n], scr[n:]
        x, y, c = lax.axis_index("x"), lax.axis_index("y"), lax.axis_index("c")
        q = 2 * x + y
        loads = [pltpu.make_async_copy(ins[w], stages[w], ld_sems.at[w]) for w in range(n)]
        stores = [pltpu.make_async_copy(stages[w], outs[w].at[q], st_sems.at[w]) for w in range(n)]
        ici, d2d = [], []
        for w in range(n):
            half = pl.ds(c * self.half_rows[w], self.half_rows[w])
            for k, (px, py) in enumerate(_other_chips(x, y)):
                ici.append(pltpu.make_async_remote_copy(
                    src_ref=ins[w].at[half], dst_ref=outs[w].at[q, half], send_sem=ici_send.at[3 * w + k],
                    recv_sem=ici_recv.at[3 * w + k], device_id=(px, py, c), device_id_type=MESH))
                landed = outs[w].at[2 * px + py, half]
                d2d.append(pltpu.make_async_remote_copy(
                    src_ref=landed, dst_ref=landed, send_sem=d2d_send.at[3 * w + k],
                    recv_sem=d2d_recv.at[3 * w + k], device_id=(x, y, 1 - c), device_id_type=MESH))
        return loads, stores, ici, d2d

    def start(self, ins, outs, scr):
        loads, stores, ici, _ = self._copies(ins, outs, scr)
        for cp in loads + ici:
            cp.start()
        for ld, st in zip(loads, stores):
            ld.wait()
            st.start()

    def relay(self, ins, outs, scr):
        _, _, ici, d2d = self._copies(ins, outs, scr)
        for got, fwd in zip(ici, d2d):
            got.wait_recv()
            fwd.start()

    def finish(self, ins, outs, scr):
        _, stores, ici, d2d = self._copies(ins, outs, scr)
        for cp in ici:
            cp.wait_send()
        for cp in d2d + stores:
            cp.wait()


class _SiblingExchange:
    def __init__(self, slabs):
        n = len(slabs)
        self.n = n
        self.in_arrays = list(slabs)
        self.out_shape = [jax.ShapeDtypeStruct((N_CHIPS,) + a.shape[1:], a.dtype) for a in slabs]
        self.scratch_shapes = [pltpu.SemaphoreType.DMA((4 * n,)), pltpu.SemaphoreType.DMA((4 * n,))]

    def _copies(self, ins, outs, scr):
        send_sems, recv_sems = scr
        x, y, c = lax.axis_index("x"), lax.axis_index("y"), lax.axis_index("c")
        return [pltpu.make_async_remote_copy(
            src_ref=ins[w].at[2 * p + (1 - c)], dst_ref=outs[w].at[p], send_sem=send_sems.at[4 * w + p],
            recv_sem=recv_sems.at[4 * w + p], device_id=(x, y, 1 - c), device_id_type=MESH)
            for w in range(self.n) for p in range(N_CHIPS)]

    def start(self, ins, outs, scr):
        for cp in self._copies(ins, outs, scr):
            cp.start()

    def finish(self, ins, outs, scr):
        for cp in self._copies(ins, outs, scr):
            cp.wait()


class _ChipExchange:
    def __init__(self, sums):
        n = len(sums)
        self.n = n
        self.in_arrays = list(sums)
        self.out_shape = [jax.ShapeDtypeStruct(a.shape, a.dtype) for a in sums]
        self.scratch_shapes = [pltpu.SemaphoreType.DMA((3 * n,)), pltpu.SemaphoreType.DMA((3 * n,))]

    def _copies(self, ins, outs, scr):
        send_sems, recv_sems = scr
        x, y, c = lax.axis_index("x"), lax.axis_index("y"), lax.axis_index("c")
        q = 2 * x + y
        return [pltpu.make_async_remote_copy(
            src_ref=ins[w].at[2 * px + py], dst_ref=outs[w].at[q], send_sem=send_sems.at[3 * w + k],
            recv_sem=recv_sems.at[3 * w + k], device_id=(px, py, c), device_id_type=MESH)
            for w in range(self.n) for k, (px, py) in enumerate(_other_chips(x, y))]

    def start(self, ins, outs, scr):
        for cp in self._copies(ins, outs, scr):
            cp.start()

    def finish(self, ins, outs, scr):
        for cp in self._copies(ins, outs, scr):
            cp.wait()


class _NoExchange:
    in_arrays, out_shape, scratch_shapes = (), (), ()

    def start(self, ins, outs, scr):
        pass

    def finish(self, ins, outs, scr):
        pass


def _run_exchange(ex, name):
    n_in, n_out = len(ex.in_arrays), len(ex.out_shape)

    def body(*refs):
        ins, outs, scr = refs[:n_in], refs[n_in:n_in + n_out], refs[n_in + n_out:]
        ex.start(ins, outs, scr)
        if hasattr(ex, "relay"):
            ex.relay(ins, outs, scr)
        ex.finish(ins, outs, scr)

    return pl.pallas_call(
        body, name=name, in_specs=[ANY] * n_in, out_specs=[ANY] * n_out, out_shape=ex.out_shape,
        scratch_shapes=ex.scratch_shapes, compiler_params=_params(24),
    )(*ex.in_arrays)


def _ride_call(body, name, grid, in_specs, out_specs, out_shape, scratch_shapes, operands, vmem_mb, ride=None):
    ride = ride or _NoExchange()
    ni, no, ns = len(in_specs), len(out_specs), len(scratch_shapes)
    ri, ro = len(ride.in_arrays), len(ride.out_shape)
    total = math.prod(grid)

    def wrapped(*refs):
        ins, rins = refs[:ni], refs[ni:ni + ri]
        outs, routs = refs[ni + ri:ni + ri + no], refs[ni + ri + no:ni + ri + no + ro]
        scr, rscr = refs[ni + ri + no + ro:ni + ri + no + ro + ns], refs[ni + ri + no + ro + ns:]
        step = pl.program_id(0)
        for ax in range(1, len(grid)):
            step = step * grid[ax] + pl.program_id(ax)

        @pl.when(step == 0)
        def _():
            ride.start(rins, routs, rscr)

        if hasattr(ride, "relay"):
            @pl.when(step == (3 * total) // 4)
            def _():
                ride.relay(rins, routs, rscr)

        body(*ins, *outs, *scr)

        @pl.when(step == total - 1)
        def _():
            ride.finish(rins, routs, rscr)

    res = pl.pallas_call(
        wrapped, name=name, grid=grid, in_specs=list(in_specs) + [ANY] * ri, out_specs=list(out_specs) + [ANY] * ro,
        out_shape=list(out_shape) + list(ride.out_shape),
        scratch_shapes=list(scratch_shapes) + list(ride.scratch_shapes),
        compiler_params=_params(vmem_mb, ("arbitrary",) * len(grid)),
    )(*operands, *ride.in_arrays)
    return res[:no], res[no:]


def _log_sig_pair(z):
    sp = jnp.log(1.0 + jnp.exp(-jnp.abs(z)))
    return -(jnp.maximum(z, 0.0) + sp), jnp.minimum(z, 0.0) - sp


def _hi_lo(m):
    hi = m.astype(BF)
    return hi, (m - hi.astype(F32)).astype(BF)


def _head_sums(x, h0):
    s0 = jnp.sum(jnp.where(h0, x, 0.0), axis=-1, keepdims=True)
    s1 = jnp.sum(jnp.where(h0, 0.0, x), axis=-1, keepdims=True)
    return jnp.where(h0, s0, s1)


def _sb_setup(q_ref, tq):
    lane = lax.broadcasted_iota(jnp.int32, (tq, 128), 1)
    h0 = lane < HEAD_LANES
    qs = q_ref[...] * SB_SCALE
    zero = jnp.zeros_like(qs)
    qst = jnp.concatenate([jnp.where(h0, qs, zero), jnp.where(h0, zero, qs)], axis=0)
    r = lax.broadcasted_iota(jnp.int32, (2 * tq, tq), 0)
    c = lax.broadcasted_iota(jnp.int32, (2 * tq, tq), 1)
    causal = c < jnp.where(r >= tq, r - tq, r)
    return h0, qst, causal


def _tri(tq, op):
    return op(lax.broadcasted_iota(jnp.int32, (tq, tq), 0), lax.broadcasted_iota(jnp.int32, (tq, tq), 1)).astype(BF)


def _sb_fwd(proj, hg, nb, s, tq, ride=None):
    t = nb * s
    tq = min(tq, s)
    nq = s // tq

    def body(q_ref, k_ref, v_ref, hg_ref, o_ref, tot_ref, mb_ref, nblk_ref, acc, cr):
        i = pl.program_id(2)
        h0, qst, causal = _sb_setup(q_ref, tq)
        tri_gt = _tri(tq, lambda r, c: r > c)
        acc[...] = jnp.zeros_like(acc)
        cr[...] = jnp.zeros_like(cr)

        def block(j, masked):
            start = pl.multiple_of(j * tq, tq)
            kj = k_ref[pl.ds(start, tq), :]
            vj = v_ref[pl.ds(start, tq), :]
            m, l = _log_sig_pair(_dot_bt(qst, kj))
            if masked:
                m = jnp.where(causal, m, 0.0)
            mh, ml = _hi_lo(m)
            a = jnp.exp(l + (_dot(mh, tri_gt) + _dot(ml, tri_gt)) + cr[...])
            if masked:
                a = jnp.where(causal, a, 0.0)
            acc[...] += _dot(a.astype(BF), vj)
            cnew = cr[...] + jnp.sum(m, axis=-1, keepdims=True)
            cr[...] = cnew
            return jnp.max(cnew)

        def cond(carry):
            return jnp.logical_and(carry[0] < i, carry[1] > SB_SKIP)

        def step(carry):
            return carry[0] + 1, block(i - 1 - carry[0], False)

        walked, _ = lax.while_loop(cond, step, (jnp.int32(0), block(i, True)))

        o = jnp.where(h0, acc[0:tq, :], acc[tq:2 * tq, :])
        o_ref[...] = o
        tot_ref[...] = jnp.where(h0, cr[0:tq, :], cr[tq:2 * tq, :])
        ro = lax.rsqrt(_head_sums(o * o, h0) * (1.0 / HEAD_LANES) + EPS)
        mb_ref[...] = (o * ro * hg_ref[...]).astype(BF)
        nblk_ref[...] = jnp.full((8, 128), walked.astype(F32))

    blk = lambda col0: pl.BlockSpec((tq, 128), lambda b, hp, i: (b * nq + i, col0 + hp))
    seq = lambda col0: pl.BlockSpec((s, 128), lambda b, hp, i: (b, col0 + hp))
    (o, tot, mb, nblk), rode = _ride_call(
        body, "sb_fwd", (nb, 4, nq),
        in_specs=[blk(8), seq(12), seq(16), pl.BlockSpec((1, 128), lambda b, hp, i: (0, 4 + hp))],
        out_specs=[blk(0), blk(0), blk(0), pl.BlockSpec((None, None, 8, 128), lambda b, hp, i: (b, hp, i, 0))],
        out_shape=[jax.ShapeDtypeStruct((t, 512), F32), jax.ShapeDtypeStruct((t, 512), F32),
                   jax.ShapeDtypeStruct((t, 512), BF), jax.ShapeDtypeStruct((nb, 4, nq * 8, 128), F32)],
        scratch_shapes=[pltpu.VMEM((2 * tq, 128), F32), pltpu.VMEM((2 * tq, 1), F32)],
        operands=(proj, proj, proj, hg), vmem_mb=40, ride=ride)
    return o, tot, mb, nblk, rode


def _sb_bwd(proj, o_sb, tot, nblk, dmerged, hg, nb, s, tq, ride=None):
    t = nb * s
    tq = min(tq, s)
    nq = s // tq

    def body(q_ref, k_ref, v_ref, o_ref, tot_ref, nblk_ref, dm_ref, hg_ref,
             dq_ref, dk_ref, dv_ref, dhg_ref, dk_acc, dv_acc, dq_acc, cm, cg):
        i = pl.program_id(2)
        h0, qst, causal = _sb_setup(q_ref, tq)
        tri_le = _tri(tq, lambda r, c: r <= c)
        tri_lt = _tri(tq, lambda r, c: r < c)

        @pl.when(i == 0)
        def _():
            dk_acc[...] = jnp.zeros_like(dk_acc)
            dv_acc[...] = jnp.zeros_like(dv_acc)
            dhg_ref[...] = jnp.zeros_like(dhg_ref)

        for ref in (dq_acc, cm, cg):
            ref[...] = jnp.zeros_like(ref)

        o = o_ref[...]
        ro = lax.rsqrt(_head_sums(o * o, h0) * (1.0 / HEAD_LANES) + EPS)
        oh = o * ro
        dm = dm_ref[...].astype(F32)
        dhg_ref[...] += jnp.sum(dm * oh, axis=0, keepdims=True)
        doh = dm * hg_ref[...]
        do = (ro * (doh - oh * (_head_sums(doh * oh, h0) * (1.0 / HEAD_LANES)))).astype(BF)
        zb = jnp.zeros_like(do)
        dost = jnp.concatenate([jnp.where(h0, do, zb), jnp.where(h0, zb, do)], axis=0)
        tots = jnp.concatenate([tot_ref[:, 0:1], tot_ref[:, HEAD_LANES:HEAD_LANES + 1]], axis=0)

        def block(j, masked):
            start = pl.multiple_of(j * tq, tq)
            kj = k_ref[pl.ds(start, tq), :]
            vj = v_ref[pl.ds(start, tq), :]
            m, l = _log_sig_pair(_dot_bt(qst, kj))
            if masked:
                m = jnp.where(causal, m, 0.0)
            mh, ml = _hi_lo(m)
            a = jnp.exp(l + (tots - cm[...] - (_dot(mh, tri_le) + _dot(ml, tri_le))))
            if masked:
                a = jnp.where(causal, a, 0.0)
            gm = a * _dot_bt(dost, vj)
            pp = cg[...] + _dot(gm.astype(BF), tri_lt)
            dz = gm - jnp.exp(l) * (gm + pp)
            if masked:
                dz = jnp.where(causal, dz, 0.0)
            dzb = dz.astype(BF)
            dq_acc[...] += _dot(dzb, kj)
            dk_acc[pl.ds(start, tq), :] += _dot_at(dzb, qst)
            dv_acc[pl.ds(start, tq), :] += _dot_at(a.astype(BF), dost)
            cm[...] += jnp.sum(m, axis=-1, keepdims=True)
            cg[...] += jnp.sum(gm, axis=-1, keepdims=True)

        def step(j, carry):
            block(j, False)
            return carry

        walked = jnp.clip(jnp.max(nblk_ref[...]).astype(jnp.int32), 0, i)
        lax.fori_loop(i - walked, i, step, 0)
        block(i, True)

        dq_ref[...] = (jnp.where(h0, dq_acc[0:tq, :], dq_acc[tq:2 * tq, :]) * SB_SCALE).astype(BF)

        @pl.when(i == nq - 1)
        def _():
            dk_ref[...] = dk_acc[...].astype(BF)
            dv_ref[...] = dv_acc[...].astype(BF)

    blk = lambda col0: pl.BlockSpec((tq, 128), lambda b, hp, i: (b * nq + i, col0 + hp))
    seq = lambda col0: pl.BlockSpec((s, 128), lambda b, hp, i: (b, col0 + hp))
    (dq, dk, dv, dhg), rode = _ride_call(
        body, "sb_bwd", (nb, 4, nq),
        in_specs=[blk(8), seq(12), seq(16), blk(0), blk(0),
                  pl.BlockSpec((None, None, 8, 128), lambda b, hp, i: (b, hp, i, 0)), blk(4),
                  pl.BlockSpec((1, 128), lambda b, hp, i: (0, 4 + hp))],
        out_specs=[blk(0), seq(0), seq(0), pl.BlockSpec((None, 1, 128), lambda b, hp, i: (b, 0, hp))],
        out_shape=[jax.ShapeDtypeStruct((t, 512), BF), jax.ShapeDtypeStruct((t, 512), BF),
                   jax.ShapeDtypeStruct((t, 512), BF), jax.ShapeDtypeStruct((nb, 1, 512), F32)],
        scratch_shapes=[pltpu.VMEM((s, 128), F32), pltpu.VMEM((s, 128), F32), pltpu.VMEM((2 * tq, 128), F32),
                        pltpu.VMEM((2 * tq, 1), F32), pltpu.VMEM((2 * tq, 1), F32)],
        operands=(proj, proj, proj, o_sb, tot, nblk, dmerged, hg), vmem_mb=40, ride=ride)
    return dq, dk, dv, dhg, rode


def _softmax_rows(sc):
    e = jnp.exp(sc - jnp.max(sc, axis=-1, keepdims=True))
    return e / jnp.sum(e, axis=-1, keepdims=True)


def _mix_cross_fwd(x, ma, mb, w_out, gc, w_cq, kv, w_co, s, tm):
    t, d = x.shape
    tm = min(tm, s)
    per = s // tm
    inv = 1.0 / math.sqrt(X_HEAD_DIM)

    def body(x_ref, ma_ref, mb_ref, wo_ref, gc_ref, wq_ref, kv_ref, wc_ref, h1_ref, h2_ref, hn_ref, oc_ref):
        h1 = x_ref[...] + _dot(ma_ref[...], wo_ref[0:512, :]) + _dot(mb_ref[...], wo_ref[512:1024, :])
        h1_ref[...] = h1
        hn = (h1 * _rs(h1) * gc_ref[...]).astype(BF)
        hn_ref[...] = hn
        qc = _dot(hn, wq_ref[...]).astype(BF)
        for h in range(X_HEADS):
            cols = slice(h * X_HEAD_DIM, (h + 1) * X_HEAD_DIM)
            kh = kv_ref[:, h * X_HEAD_DIM:(h + 1) * X_HEAD_DIM]
            vh = kv_ref[:, d + h * X_HEAD_DIM:d + (h + 1) * X_HEAD_DIM]
            p = _softmax_rows(_dot_bt(qc[:, cols], kh) * inv)
            oc_ref[:, cols] = _dot(p.astype(BF), vh).astype(BF)
        h2_ref[...] = h1 + _dot(oc_ref[...], wc_ref[...])

    row = lambda width: pl.BlockSpec((tm, width), lambda i: (i, 0))
    full = lambda a, b: pl.BlockSpec((a, b), lambda i: (0, 0))
    return pl.pallas_call(
        body, name="mix_cross_fwd", grid=(t // tm,),
        in_specs=[row(d), row(512), row(512), full(d, d), full(1, d), full(d, d),
                  pl.BlockSpec((N_MEM, 2 * d), lambda i: (i // per, 0)), full(d, d)],
        out_specs=[row(d), row(d), row(d), row(d)],
        out_shape=[jax.ShapeDtypeStruct((t, d), F32), jax.ShapeDtypeStruct((t, d), F32),
                   jax.ShapeDtypeStruct((t, d), BF), jax.ShapeDtypeStruct((t, d), BF)],
        compiler_params=_params(48, ("arbitrary",)),
    )(x, ma, mb, w_out, gc, w_cq, kv, w_co)


def _cross_bwd(dh2, h1, gc, w_cq, kv, w_co, s, tm):
    t, d = dh2.shape
    tm = min(tm, s)
    per = s // tm
    nb = t // s
    inv = 1.0 / math.sqrt(X_HEAD_DIM)

    def body(dh2_ref, h1_ref, gc_ref, wq_ref, kv_ref, wc_ref, dh1_ref, dqc_ref, dkv_ref, dgc_ref):
        i = pl.program_id(0)

        @pl.when(i == 0)
        def _():
            dgc_ref[...] = jnp.zeros_like(dgc_ref)

        @pl.when(i % per == 0)
        def _():
            dkv_ref[...] = jnp.zeros_like(dkv_ref)

        dh2 = dh2_ref[...]
        h1 = h1_ref[...]
        r = _rs(h1)
        h1h = h1 * r
        gcv = gc_ref[...]
        hn = (h1h * gcv).astype(BF)
        qc = _dot(hn, wq_ref[...]).astype(BF)
        do = _dot_bt(dh2.astype(BF), wc_ref[...]).astype(BF)
        for h in range(X_HEADS):
            cols = slice(h * X_HEAD_DIM, (h + 1) * X_HEAD_DIM)
            vcols = slice(d + h * X_HEAD_DIM, d + (h + 1) * X_HEAD_DIM)
            kh = kv_ref[:, cols]
            vh = kv_ref[:, vcols]
            p = _softmax_rows(_dot_bt(qc[:, cols], kh) * inv)
            dp = _dot_bt(do[:, cols], vh)
            ds = (p * (dp - jnp.sum(dp * p, axis=-1, keepdims=True)) * inv).astype(BF)
            dqc_ref[:, cols] = _dot(ds, kh).astype(BF)
            dkv_ref[:, cols] += _dot_at(ds, qc[:, cols])
            dkv_ref[:, vcols] += _dot_at(p.astype(BF), do[:, cols])
        dhn = _dot_bt(dqc_ref[...], wq_ref[...])
        dx, dg = _rms_bwd(dhn, h1h, r, gcv)
        dh1_ref[...] = dh2 + dx
        dgc_ref[...] += jnp.sum(dg, axis=0, keepdims=True)

    row = lambda width: pl.BlockSpec((tm, width), lambda i: (i, 0))
    full = lambda a, b: pl.BlockSpec((a, b), lambda i: (0, 0))
    kvspec = pl.BlockSpec((N_MEM, 2 * d), lambda i: (i // per, 0))
    return pl.pallas_call(
        body, name="cross_bwd", grid=(t // tm,),
        in_specs=[row(d), row(d), full(1, d), full(d, d), kvspec, full(d, d)],
        out_specs=[row(d), row(d), kvspec, full(1, d)],
        out_shape=[jax.ShapeDtypeStruct((t, d), F32), jax.ShapeDtypeStruct((t, d), BF),
                   jax.ShapeDtypeStruct((nb * N_MEM, 2 * d), F32), jax.ShapeDtypeStruct((1, d), F32)],
        compiler_params=_params(48, ("arbitrary",)),
    )(dh2, h1, gc, w_cq, kv, w_co)


def _mem_bwd(mem, gm, dkv, w_ckv, tm):
    t, d = mem.shape
    tm = min(tm, t)

    def body(mem_ref, dkv_ref, w_ref, dg_ref):
        @pl.when(pl.program_id(0) == 0)
        def _():
            dg_ref[...] = jnp.zeros_like(dg_ref)

        mv = mem_ref[...]
        dmn = _dot_bt(dkv_ref[...].astype(BF), w_ref[...])
        dg_ref[...] += jnp.sum(dmn * (mv * _rs(mv)), axis=0, keepdims=True)

    del gm
    return pl.pallas_call(
        body, name="mem_bwd", grid=(t // tm,),
        in_specs=[pl.BlockSpec((tm, d), lambda i: (i, 0)), pl.BlockSpec((tm, 2 * d), lambda i: (i, 0)),
                  pl.BlockSpec((d, 2 * d), lambda i: (0, 0))],
        out_specs=pl.BlockSpec((1, d), lambda i: (0, 0)),
        out_shape=jax.ShapeDtypeStruct((1, d), F32),
        compiler_params=_params(32, ("arbitrary",)),
    )(mem, dkv, w_ckv)


def _ffn_loss_fwd(h2, gf, w1, w2, gl, target, tm):
    t, d = h2.shape
    tm = min(tm, t)

    def body(h2_ref, gf_ref, w1_ref, w2_ref, gl_ref, tg_ref, hn_ref, f_ref, dh3_ref, dgl_ref, loss_ref):
        @pl.when(pl.program_id(0) == 0)
        def _():
            dgl_ref[...] = jnp.zeros_like(dgl_ref)
            loss_ref[...] = jnp.zeros_like(loss_ref)

        h2 = h2_ref[...]
        hn = (h2 * _rs(h2) * gf_ref[...]).astype(BF)
        hn_ref[...] = hn
        h3 = h2
        for c in range(4):
            f = jnp.maximum(_dot(hn, w1_ref[c]), 0.0)
            f_ref[:, c * 1024:(c + 1) * 1024] = f.astype(BF)
            h3 = h3 + _dot((f * f).astype(BF), w2_ref[c])
        r3 = _rs(h3)
        yh = h3 * r3
        glv = gl_ref[...]
        e = yh * glv - tg_ref[...]
        loss_ref[...] += 0.5 * jnp.sum(jnp.sum(e * e, axis=-1, keepdims=True) * (1.0 / d), axis=0, keepdims=True)
        dy = e * (1.0 / d)
        dx, dg = _rms_bwd(dy, yh, r3, glv)
        dh3_ref[...] = dx
        dgl_ref[...] += jnp.sum(dg, axis=0, keepdims=True)

    row = lambda width: pl.BlockSpec((tm, width), lambda i: (i, 0))
    return pl.pallas_call(
        body, name="ffn_loss_fwd", grid=(t // tm,),
        in_specs=[row(d), pl.BlockSpec((1, d), lambda i: (0, 0)), pl.BlockSpec((4, d, 1024), lambda i: (0, 0, 0), pipeline_mode=pl.Buffered(1)),
                  pl.BlockSpec((4, 1024, d), lambda i: (0, 0, 0), pipeline_mode=pl.Buffered(1)),
                  pl.BlockSpec((1, d), lambda i: (0, 0)), row(d)],
        out_specs=[row(d), row(D_FF), row(d), pl.BlockSpec((1, d), lambda i: (0, 0)),
                   pl.BlockSpec((1, 1), lambda i: (0, 0))],
        out_shape=[jax.ShapeDtypeStruct((t, d), BF), jax.ShapeDtypeStruct((t, D_FF), BF),
                   jax.ShapeDtypeStruct((t, d), F32), jax.ShapeDtypeStruct((1, d), F32),
                   jax.ShapeDtypeStruct((1, 1), F32)],
        compiler_params=_params(56, ("arbitrary",)),
    )(h2, gf, w1, w2, gl, target)


def _ffn_bwd(dh3, f, h2, gf, w1, w2, tm):
    t, d = h2.shape
    tm = min(tm, t)

    def body(dh3_ref, f_ref, h2_ref, gf_ref, w1_ref, w2_ref, dh2_ref, dpre_ref, dgf_ref):
        @pl.when(pl.program_id(0) == 0)
        def _():
            dgf_ref[...] = jnp.zeros_like(dgf_ref)

        dh3 = dh3_ref[...]
        dh3b = dh3.astype(BF)
        dhn = jnp.zeros((tm, d), F32)
        for c in range(4):
            cols = slice(c * 1024, (c + 1) * 1024)
            dpre = (_dot_bt(dh3b, w2_ref[c]) * (2.0 * f_ref[:, cols].astype(F32))).astype(BF)
            dpre_ref[:, cols] = dpre
            dhn = dhn + _dot_bt(dpre, w1_ref[c])
        h2 = h2_ref[...]
        r = _rs(h2)
        dx, dg = _rms_bwd(dhn, h2 * r, r, gf_ref[...])
        dh2_ref[...] = dh3 + dx
        dgf_ref[...] += jnp.sum(dg, axis=0, keepdims=True)

    row = lambda width: pl.BlockSpec((tm, width), lambda i: (i, 0))
    return pl.pallas_call(
        body, name="ffn_bwd", grid=(t // tm,),
        in_specs=[row(d), row(D_FF), row(d), pl.BlockSpec((1, d), lambda i: (0, 0)),
                  pl.BlockSpec((4, d, 1024), lambda i: (0, 0, 0), pipeline_mode=pl.Buffered(1)),
                  pl.BlockSpec((4, 1024, d), lambda i: (0, 0, 0), pipeline_mode=pl.Buffered(1))],
        out_specs=[row(d), row(D_FF), pl.BlockSpec((1, d), lambda i: (0, 0))],
        out_shape=[jax.ShapeDtypeStruct((t, d), F32), jax.ShapeDtypeStruct((t, D_FF), BF),
                   jax.ShapeDtypeStruct((1, d), F32)],
        compiler_params=_params(56, ("arbitrary",)),
    )(dh3, f, h2, gf, w1, w2)


def _in_bwd(dproj, dh1, x, g, w_in, tm, ride=None):
    t, d = x.shape
    n = w_in.shape[1]
    tm = min(tm, t)

    def body(dp_ref, dh1_ref, x_ref, g_ref, w_ref, dx_ref, dg_ref):
        @pl.when(pl.program_id(0) == 0)
        def _():
            dg_ref[...] = jnp.zeros_like(dg_ref)

        dxn = _dot_bt(dp_ref[...], w_ref[...])
        xv = x_ref[...]
        r = _rs(xv)
        dx, dg = _rms_bwd(dxn, xv * r, r, g_ref[...])
        dx_ref[...] = dh1_ref[...] + dx
        dg_ref[...] += jnp.sum(dg, axis=0, keepdims=True)

    row = lambda width: pl.BlockSpec((tm, width), lambda i: (i, 0))
    (dx, dg), rode = _ride_call(
        body, "in_bwd", (t // tm,),
        in_specs=[row(n), row(d), row(d), pl.BlockSpec((1, d), lambda i: (0, 0)),
                  pl.BlockSpec((d, n), lambda i: (0, 0))],
        out_specs=[row(d), pl.BlockSpec((1, d), lambda i: (0, 0))],
        out_shape=[jax.ShapeDtypeStruct((t, d), F32), jax.ShapeDtypeStruct((1, d), F32)],
        scratch_shapes=[], operands=(dproj, dh1, x, g, w_in), vmem_mb=48, ride=ride)
    return dx, dg, rode


class _GradReduce:
    def __init__(self, c_idx):
        self.c_idx = c_idx
        self.sums = {}

    def sibling(self, slabs):
        return _SiblingExchange(slabs)

    def chip(self, names, slabs, recv):
        for k, a, r in zip(names, slabs, recv):
            self.sums[k] = _chip_sum(a, r, self.c_idx, "chip_sum_" + k)
        return _ChipExchange([self.sums[k] for k in names])


def _full_weights(gathered):
    d = D_MODEL
    out = {}
    for k, a in gathered.items():
        if k in ("w_in", "w_ckv", "w_ff1"):
            out[k] = a.transpose(1, 0, 2).reshape(d, -1)
        else:
            out[k] = a.reshape(-1, d)
    return out


def _slabs(a):
    return a.reshape(N_DEV, -1, a.shape[-1])


def _local_step(x, mem, target, small, big, nb, s, tq=256, gather_rest=None, reduce=None):
    d = D_MODEL
    g_mix, g_v, w_sp, b_sp, g_head, g_cross, g_mem, g_ffn, g_fin = (
        small[k] for k in ("norm_mix_g", "gm_v_norm_g", "w_spatial", "b_spatial", "head_norm_g", "norm_cross_g",
                           "norm_mem_g", "norm_ffn_g", "norm_final_g"))
    tri = jnp.tril(jnp.ones((CHUNK, CHUNK), dtype=bool))
    w_sp_m = jnp.where(tri[None], w_sp, 0.0)
    wt = w_sp_m.astype(BF)
    wtt = jnp.swapaxes(w_sp_m, 1, 2).astype(BF)
    bb = jnp.broadcast_to(b_sp[:, :, None], (GM_GROUPS, CHUNK, CHUNK))
    hg_a = g_head[:, :GM_WIDTH]

    proj, xn = _norm_matmul(x, g_mix, big["w_in"], 512, "in_proj")
    ma = _gmlp_fwd(proj, g_v, wt, bb, hg_a, 512)
    o_sb, tot, mb, nblk, gathered = _sb_fwd(proj, g_head, nb, s, tq, ride=gather_rest)
    if gather_rest is not None:
        big = dict(big, **_full_weights(dict(zip(BIG[1:], gathered))))
    w1c = big["w_ff1"].reshape(d, 4, 1024).transpose(1, 0, 2)
    w2c = big["w_ff2"].reshape(4, 1024, d)
    kv, memn = _norm_matmul(mem, g_mem, big["w_ckv"], 512, "mem_proj")
    h1, h2, hn, oc = _mix_cross_fwd(x, ma, mb, big["w_out"], g_cross, big["w_cq"], kv, big["w_co"], s, 512)
    hn2, f, dh3, d_fin, loss = _ffn_loss_fwd(h2, g_ffn, w1c, w2c, g_fin, target, 512)

    gbig = {}
    dh2, dpre, d_ffn = _ffn_bwd(dh3, f, h2, g_ffn, w1c, w2c, 512)
    gbig["w_ff2"] = _slabs(_wgrad(f, dh3, 1024, 1024, "wgrad_ff2", square_a=True))
    gbig["w_ff1"] = _slabs(_wgrad(hn2, dpre, 1024, 1024, "wgrad_ff1", col_shards=4))
    dh1, dqc, dkv, d_cross = _cross_bwd(dh2, h1, g_cross, big["w_cq"], kv, big["w_co"], s, 512)
    gbig["w_co"] = _slabs(_wgrad(oc, dh2, 1024, 1024, "wgrad_co"))
    gbig["w_cq"] = _slabs(_wgrad(hn, dqc, 1024, 1024, "wgrad_cq"))
    gbig["w_ckv"] = _slabs(_wgrad(memn, dkv, 512, 1024, "wgrad_ckv", col_shards=4))
    d_mem = _mem_bwd(mem, g_mem, dkv, big["w_ckv"], 512)
    dmerged = _matmul_bt(dh1, big["w_out"], 512, "out_bwd")
    gbig["w_out"] = _slabs(_wgrad(jnp.concatenate([ma, mb], axis=1), dh1, 1024, 1024, "wgrad_out"))
    rest = BIG[1:]
    ride = reduce.sibling([gbig[k] for k in rest]) if reduce else None
    dugv, d_wsp, d_bb, d_gv, d_hga, recv = _gmlp_bwd(proj, dmerged, g_v, wt, wtt, bb, hg_a, 512, ride=ride)
    ride = reduce.chip(rest, [gbig[k] for k in rest], recv) if reduce else None
    dq, dk, dv, d_hgb, parts_rest = _sb_bwd(proj, o_sb, tot, nblk, dmerged, g_head, nb, s, tq, ride=ride)
    dproj = jnp.concatenate([dugv, dq, dk, dv], axis=1)
    gbig["w_in"] = _slabs(_wgrad(xn, dproj, 640, 1024, "wgrad_in", col_shards=4))
    last = None
    if reduce:
        recv = _run_exchange(reduce.sibling([gbig["w_in"]]), "grad_sibling_exchange_w_in")
        last = reduce.chip(["w_in"], [gbig["w_in"]], recv)
    grad_x, d_mix, _ = _in_bwd(dproj, dh1, x, g_mix, big["w_in"], 512)
    parts = dict(zip(rest, parts_rest))

    gsmall = {
        "norm_mix_g": d_mix, "gm_v_norm_g": d_gv, "w_spatial": d_wsp, "b_spatial": d_bb[:, :, 0],
        "head_norm_g": jnp.concatenate([d_hga, jnp.sum(d_hgb, axis=0)], axis=1), "norm_cross_g": d_cross,
        "norm_mem_g": d_mem, "norm_ffn_g": d_ffn, "norm_final_g": d_fin,
    }
    return loss, grad_x, gsmall, gbig, parts, last


BIG = ("w_in", "w_out", "w_cq", "w_ckv", "w_co", "w_ff1", "w_ff2")
SMALL = ("norm_mix_g", "gm_v_norm_g", "w_spatial", "b_spatial", "head_norm_g", "norm_cross_g", "norm_mem_g",
         "norm_ffn_g", "norm_final_g")


def _local_copies_start(srcs, stages, sems):
    loads = [pltpu.make_async_copy(src, stage, sems.at[w]) for w, (src, stage) in enumerate(zip(srcs, stages))]
    for ld in loads:
        ld.start()
    return loads


def _local_copies_finish(loads, stages, dsts, sems):
    stores = []
    for w, (ld, stage, dst) in enumerate(zip(loads, stages, dsts)):
        ld.wait()
        st = pltpu.make_async_copy(stage, dst, sems.at[w])
        st.start()
        stores.append(st)
    for st in stores:
        st.wait()


def _chip_sum(slabs, recv, c_idx, name):
    _, r, cw = slabs.shape
    tr = min(r, 256)

    def body(c_ref, a_ref, b_ref, o_ref):
        del c_ref
        o_ref[...] = (a_ref[...] + b_ref[...]).astype(BF)

    return pl.pallas_call(
        body, name=name,
        grid_spec=pltpu.PrefetchScalarGridSpec(
            num_scalar_prefetch=1, grid=(N_CHIPS, r // tr),
            in_specs=[pl.BlockSpec((None, tr, cw), lambda p, i, c_ref: (2 * p + c_ref[0], i, 0)),
                      pl.BlockSpec((None, tr, cw), lambda p, i, c_ref: (p, i, 0))],
            out_specs=pl.BlockSpec((None, tr, cw), lambda p, i, c_ref: (p, i, 0))),
        out_shape=jax.ShapeDtypeStruct((N_CHIPS, r, cw), BF),
        compiler_params=_params(32, ("arbitrary", "arbitrary")),
    )(c_idx, slabs, recv)


def _sum4(sums, parts, q_idx, name):
    _, r, cw = parts.shape
    tr = min(r, 256)

    def body(q_ref, own_ref, a_ref, b_ref, c_ref, o_ref):
        del q_ref
        o_ref[...] = ((own_ref[...].astype(F32) + a_ref[...].astype(F32)) + b_ref[...].astype(F32)) + c_ref[
            ...].astype(F32)

    spec = lambda k: pl.BlockSpec((None, tr, cw), lambda i, q_ref: ((q_ref[0] + k) % N_CHIPS, i, 0))
    return pl.pallas_call(
        body, name=name,
        grid_spec=pltpu.PrefetchScalarGridSpec(
            num_scalar_prefetch=1, grid=(r // tr,), in_specs=[spec(0), spec(1), spec(2), spec(3)],
            out_specs=pl.BlockSpec((tr, cw), lambda i, q_ref: (i, 0))),
        out_shape=jax.ShapeDtypeStruct((r, cw), F32),
        compiler_params=_params(32, ("arbitrary",)),
    )(q_idx, sums, parts, parts, parts)


def _half_exchange(halves):
    n = len(halves)

    def body(*refs):
        ins, outs, stages = refs[:n], refs[n:2 * n], refs[2 * n:3 * n]
        send_sems, recv_sems, ld_sems, st_sems = refs[3 * n:]
        x, y, c = lax.axis_index("x"), lax.axis_index("y"), lax.axis_index("c")
        loads = _local_copies_start(ins, stages, ld_sems)
        copies = []
        for w in range(n):
            cp = pltpu.make_async_remote_copy(
                src_ref=ins[w], dst_ref=outs[w].at[c], send_sem=send_sems.at[w], recv_sem=recv_sems.at[w],
                device_id=(x, y, 1 - c), device_id_type=MESH)
            cp.start()
            copies.append(cp)
        _local_copies_finish(loads, stages, [outs[w].at[c] for w in range(n)], st_sems)
        for cp in copies:
            cp.wait()

    return pl.pallas_call(
        body, name="grad_half_exchange",
        in_specs=[ANY] * n, out_specs=[ANY] * n,
        out_shape=[jax.ShapeDtypeStruct((2,) + a.shape, a.dtype) for a in halves],
        scratch_shapes=[pltpu.VMEM(a.shape, a.dtype) for a in halves] + [
            pltpu.SemaphoreType.DMA((n,)), pltpu.SemaphoreType.DMA((n,)),
            pltpu.SemaphoreType.DMA((n,)), pltpu.SemaphoreType.DMA((n,))],
        compiler_params=_params(24),
    )(*halves)


def _small_all_reduce(packed, ride=None):
    rows = packed.shape[0]
    ride = ride or _NoExchange()
    ri, ro = len(ride.in_arrays), len(ride.out_shape)

    def body(*refs):
        in_ref, rins, out_ref, routs = refs[0], refs[1:1 + ri], refs[1 + ri], refs[2 + ri:2 + ri + ro]
        buf, send_sems, recv_sems = refs[2 + ri + ro:5 + ri + ro]
        rscr = refs[5 + ri + ro:]
        ride.start(rins, routs, rscr)
        x, y, c = lax.axis_index("x"), lax.axis_index("y"), lax.axis_index("c")
        me = 4 * x + 2 * y + c
        buf[me] = in_ref[...]
        copies = []
        for k in range(1, N_DEV):
            bx, by, bc = (k >> 2) & 1, (k >> 1) & 1, k & 1
            peer = (x ^ bx, y ^ by, c ^ bc)
            cp = pltpu.make_async_remote_copy(
                src_ref=in_ref, dst_ref=buf.at[me], send_sem=send_sems.at[k - 1], recv_sem=recv_sems.at[k - 1],
                device_id=peer, device_id_type=MESH)
            cp.start()
            copies.append(cp)
        for cp in copies:
            cp.wait()
        acc = buf[0]
        for dev in range(1, N_DEV):
            acc = acc + buf[dev]
        out_ref[...] = acc
        ride.finish(rins, routs, rscr)

    vmem = pl.BlockSpec(memory_space=pltpu.VMEM)
    res = pl.pallas_call(
        body, name="small_all_reduce",
        in_specs=[vmem] + [ANY] * ri, out_specs=[vmem] + [ANY] * ro,
        out_shape=[jax.ShapeDtypeStruct(packed.shape, F32)] + list(ride.out_shape),
        scratch_shapes=[pltpu.VMEM((N_DEV, rows, 128), F32), pltpu.SemaphoreType.DMA((N_DEV - 1,)),
                        pltpu.SemaphoreType.DMA((N_DEV - 1,))] + list(ride.scratch_shapes),
        compiler_params=_params(16),
    )(packed, *ride.in_arrays)
    return res[0], res[1:]


def _adamw(g, w, m, v, name):
    r, cw = g.shape
    tr = r if r <= 1024 else 256
    assert r % tr == 0

    def body(g_ref, w_ref, m_ref, v_ref, d_ref, nm_ref, nv_ref):
        gv = g_ref[...]
        nm = ADAM_B1 * m_ref[...] + (1.0 - ADAM_B1) * gv
        nv = ADAM_B2 * v_ref[...] + (1.0 - ADAM_B2) * (gv * gv)
        m_hat = nm / (1.0 - ADAM_B1 ** ADAM_STEP)
        v_hat = nv / (1.0 - ADAM_B2 ** ADAM_STEP)
        d_ref[...] = -ADAM_LR * (m_hat / (jnp.sqrt(v_hat) + ADAM_EPS) + ADAM_WD * w_ref[...])
        nm_ref[...] = nm
        nv_ref[...] = nv

    spec = pl.BlockSpec((tr, cw), lambda i: (i, 0))
    return pl.pallas_call(
        body, name=name, grid=(r // tr,),
        in_specs=[spec] * 4, out_specs=[spec] * 3,
        out_shape=[jax.ShapeDtypeStruct((r, cw), F32)] * 3,
        compiler_params=_params(32, ("arbitrary",)),
    )(g, w, m, v)


def _small_params(args):
    small = {k: args[k].reshape(1, -1) for k in SMALL}
    small["w_spatial"] = args["w_spatial"][0]
    small["b_spatial"] = args["b_spatial"][0]
    return small


def _pack(parts, rows):
    flat = jnp.concatenate([p.reshape(-1).astype(F32) for p in parts])
    return jnp.pad(flat, (0, rows * 128 - flat.shape[0])).reshape(rows, 128)


def _unpack(packed, shapes):
    flat = packed.reshape(-1)
    out, off = [], 0
    for shp in shapes:
        size = math.prod(shp)
        out.append(flat[off:off + size].reshape(shp))
        off += size
    return out


def kernel(x, mem, norm_mix_g, w_in, gm_v_norm_g, w_spatial, b_spatial, head_norm_g, w_out, norm_cross_g, norm_mem_g, w_cq, w_ckv, w_co, norm_ffn_g, w_ff1, w_ff2, norm_final_g, loss_target, m_norm_mix_g, m_w_in, m_gm_v_norm_g, m_w_spatial, m_b_spatial, m_head_norm_g, m_w_out, m_norm_cross_g, m_norm_mem_g, m_w_cq, m_w_ckv, m_w_co, m_norm_ffn_g, m_w_ff1, m_w_ff2, m_norm_final_g, v_norm_mix_g, v_w_in, v_gm_v_norm_g, v_w_spatial, v_b_spatial, v_head_norm_g, v_w_out, v_norm_cross_g, v_norm_mem_g, v_w_cq, v_w_ckv, v_w_co, v_norm_ffn_g, v_w_ff1, v_w_ff2, v_norm_final_g):
    args = dict(locals())
    d = D_MODEL
    nb, s, _ = x.shape
    c_idx = lax.axis_index("c").astype(jnp.int32).reshape(1)
    q_idx = (2 * lax.axis_index("x") + lax.axis_index("y")).astype(jnp.int32).reshape(1)
    rest = BIG[1:]

    shards = {k: args[k][0].astype(BF) for k in BIG}
    big = _full_weights({"w_in": _run_exchange(_GatherExchange([shards["w_in"]]), "all_gather_w_in")[0]})
    gather_rest = _GatherExchange([shards[k] for k in rest])

    reduce = _GradReduce(c_idx)
    loss, grad_x, gsmall, _, parts, last = _local_step(
        x.reshape(nb * s, d), mem.reshape(nb * N_MEM, d), loss_target.reshape(nb * s, d), _small_params(args), big,
        nb, s, gather_rest=gather_rest, reduce=reduce)

    shapes = [args[k].shape for k in SMALL]
    n_small = sum(math.prod(sh) for sh in shapes)
    rows = -(-(n_small + 1) // 1024) * 8
    reduced, (parts["w_in"],) = _small_all_reduce(_pack([gsmall[k] for k in SMALL] + [loss], rows), ride=last)
    halves = [_sum4(reduce.sums[k], parts[k], q_idx, "sum4_" + k) for k in BIG]
    both = _half_exchange(halves)

    out = {"grad_x": grad_x.reshape(nb, s, d)}
    for k, g2 in zip(BIG, both):
        shp = args[k].shape
        g = g2.reshape(shp[1], shp[2])
        dl, nm, nv = _adamw(g, args[k][0], args["m_" + k][0], args["v_" + k][0], "adamw_" + k)
        out["grad_" + k], out["delta_" + k], out["new_m_" + k], out["new_v_" + k] = (
            a.reshape(shp) for a in (g, dl, nm, nv))

    dl, nm, nv = _adamw(reduced, _pack([args[k] for k in SMALL], rows), _pack([args["m_" + k] for k in SMALL], rows),
                        _pack([args["v_" + k] for k in SMALL], rows), "adamw_small")
    for name, arr in (("grad_", reduced), ("delta_", dl), ("new_m_", nm), ("new_v_", nv)):
        for k, a in zip(SMALL, _unpack(arr, shapes)):
            out[name + k] = a
    out["loss"] = reduced.reshape(-1)[n_small]

    names = ["norm_mix_g", "w_in", "gm_v_norm_g", "w_spatial", "b_spatial", "head_norm_g", "w_out", "norm_cross_g",
             "norm_mem_g", "w_cq", "w_ckv", "w_co", "norm_ffn_g", "w_ff1", "w_ff2", "norm_final_g"]
    return (out["loss"], out["grad_x"], *[out["grad_" + k] for k in names], *[out["delta_" + k] for k in names],
            *[out["new_m_" + k] for k in names], *[out["new_v_" + k] for k in names])
```

```python
import functools
import math

import jax
import jax.numpy as jnp
from jax import lax
from jax.experimental import pallas as pl
from jax.experimental.pallas import tpu as pltpu

F32 = jnp.float32
BF = jnp.bfloat16

EPS = 1e-6
D_MODEL = 1024
CHUNK = 128
GM_GROUPS = 4
GM_WIDTH = 512
SB_WIDTH = 512
HEAD_LANES = 64
SB_SCALE = 0.125
SB_SKIP = -104.0
X_HEADS = 4
X_HEAD_DIM = 256
N_MEM = 256
D_FF = 4096
IN_COLS = 2560
N_CHIPS = 4
N_DEV = 8

ADAM_LR = 0.001
ADAM_B1 = 0.9
ADAM_B2 = 0.999
ADAM_EPS = 1e-08
ADAM_WD = 0.01
ADAM_STEP = 10

V7X_VMEM_BYTES = 64 * 1024 * 1024
MESH = pl.DeviceIdType.MESH
ANY = pl.BlockSpec(memory_space=pl.ANY)

GELU_C = math.sqrt(2.0 / math.pi)
GELU_A = 0.044715


def _params(vmem_mb, sem=None):
    assert vmem_mb * 1024 * 1024 <= V7X_VMEM_BYTES
    return pltpu.CompilerParams(vmem_limit_bytes=vmem_mb * 1024 * 1024, dimension_semantics=sem)


def _in_hbm(*arrays):
    return tuple(pltpu.with_memory_space_constraint(a, pltpu.HBM) for a in arrays)


def _dot(a, b):
    return jnp.dot(a, b, preferred_element_type=F32)


def _dot_bt(a, b):
    return lax.dot_general(a, b, (((1,), (1,)), ((), ())), preferred_element_type=F32)


def _dot_at(a, b):
    return lax.dot_general(a, b, (((0,), (0,)), ((), ())), preferred_element_type=F32)


def _gelu(x):
    t = jnp.tanh(GELU_C * (x + GELU_A * x * x * x))
    return 0.5 * x * (1.0 + t)


def _gelu_and_grad(x):
    x2 = x * x
    t = jnp.tanh(GELU_C * (x + GELU_A * x2 * x))
    h = 0.5 * (1.0 + t)
    return x * h, h + 0.5 * x * (1.0 - t * t) * (GELU_C * (1.0 + 3.0 * GELU_A * x2))


def _rs(x):
    return lax.rsqrt(jnp.mean(x * x, axis=-1, keepdims=True) + EPS)


def _rms_bwd(dxn, xhat, r, g):
    dxh = dxn * g
    dx = r * (dxh - xhat * jnp.mean(dxh * xhat, axis=-1, keepdims=True))
    return dx, dxn * xhat


def _norm_matmul(x, g, w, tm, name):
    t, d = x.shape
    n = w.shape[1]
    tm = min(tm, t)

    def body(x_ref, g_ref, w_ref, out_ref, xn_ref):
        xv = x_ref[...]
        xn = (xv * _rs(xv) * g_ref[...]).astype(BF)
        xn_ref[...] = xn
        out_ref[...] = _dot(xn, w_ref[...]).astype(out_ref.dtype)

    return pl.pallas_call(
        body, name=name, grid=(t // tm,),
        in_specs=[pl.BlockSpec((tm, d), lambda i: (i, 0)), pl.BlockSpec((1, d), lambda i: (0, 0)),
                  pl.BlockSpec((d, n), lambda i: (0, 0))],
        out_specs=[pl.BlockSpec((tm, n), lambda i: (i, 0)), pl.BlockSpec((tm, d), lambda i: (i, 0))],
        out_shape=[jax.ShapeDtypeStruct((t, n), BF), jax.ShapeDtypeStruct((t, d), BF)],
        compiler_params=_params(48, ("arbitrary",)),
    )(x, g, w)


def _wgrad(a, g, tn, tk, name, square_a=False, col_shards=1):
    t, m = a.shape
    n = g.shape[1]
    tk = min(tk, t)
    tm = min(m, 1024)
    ns = n // col_shards
    assert ns % tn == 0 and m % tm == 0
    per = ns // tn
    nk = t // tk

    def body(a_ref, g_ref, o_ref):
        k = pl.program_id(2)

        @pl.when(k == 0)
        def _():
            o_ref[...] = jnp.zeros_like(o_ref)

        av = a_ref[...]
        if square_a:
            af = av.astype(F32)
            av = af * af
        o_ref[...] += _dot_at(av.astype(BF), g_ref[...].astype(BF))

    return pl.pallas_call(
        body, name=name, grid=(m // tm, n // tn, nk),
        in_specs=[pl.BlockSpec((tk, tm), lambda i, j, k: (k, i)), pl.BlockSpec((tk, tn), lambda i, j, k: (k, j))],
        out_specs=pl.BlockSpec((None, tm, tn), lambda i, j, k: (j // per, i, j % per)),
        out_shape=jax.ShapeDtypeStruct((col_shards, m, ns), F32),
        compiler_params=_params(48, ("arbitrary", "arbitrary", "arbitrary")),
    )(a, g)


def _matmul_bt(a, w, tm, name):
    t, n = a.shape
    k = w.shape[0]
    tm = min(tm, t)

    def body(a_ref, w_ref, o_ref):
        o_ref[...] = _dot_bt(a_ref[...].astype(BF), w_ref[...]).astype(o_ref.dtype)

    return pl.pallas_call(
        body, name=name, grid=(t // tm,),
        in_specs=[pl.BlockSpec((tm, n), lambda i: (i, 0)), pl.BlockSpec((k, n), lambda i: (0, 0))],
        out_specs=pl.BlockSpec((tm, k), lambda i: (i, 0)),
        out_shape=jax.ShapeDtypeStruct((t, k), BF),
        compiler_params=_params(32, ("arbitrary",)),
    )(a, w)


def _gmlp_fwd(proj, gg, wt, bb, hg, tm):
    t = proj.shape[0]
    tm = min(tm, t)

    def body(u_ref, v_ref, gg_ref, wt_ref, bb_ref, hg_ref, out_ref):
        for cc in range(tm // CHUNK):
            rows = slice(cc * CHUNK, (cc + 1) * CHUNK)
            for g in range(GM_GROUPS):
                cols = slice(g * 128, (g + 1) * 128)
                u = _gelu(u_ref[rows, cols].astype(F32))
                gv = _gelu(v_ref[rows, cols].astype(F32))
                vn = gv * _rs(gv) * gg_ref[:, cols]
                mixed = _dot(wt_ref[g], vn.astype(BF)) + bb_ref[g]
                a = u * mixed
                out_ref[rows, cols] = (a * _rs(a) * hg_ref[:, cols]).astype(BF)

    return pl.pallas_call(
        body, name="gmlp_fwd", grid=(t // tm,),
        in_specs=[pl.BlockSpec((tm, 512), lambda i: (i, 0)), pl.BlockSpec((tm, 512), lambda i: (i, 1)),
                  pl.BlockSpec((1, 512), lambda i: (0, 0)), pl.BlockSpec((4, 128, 128), lambda i: (0, 0, 0)),
                  pl.BlockSpec((4, 128, 128), lambda i: (0, 0, 0)), pl.BlockSpec((1, 512), lambda i: (0, 0))],
        out_specs=pl.BlockSpec((tm, 512), lambda i: (i, 0)),
        out_shape=jax.ShapeDtypeStruct((t, 1024), BF),
        compiler_params=_params(32, ("arbitrary",)),
    )(proj, proj, gg, wt, bb, hg)


def _gmlp_bwd(proj, dmerged, gg, wt, wtt, bb, hg, tm, ride=None):
    t = proj.shape[0]
    tm = min(tm, t)
    nsteps = t // tm

    def body(u_ref, v_ref, dm_ref, gg_ref, wt_ref, wtt_ref, bb_ref, hg_ref,
             dp_ref, dw_ref, db_ref, dgg_ref, dhg_ref):
        i = pl.program_id(0)

        @pl.when(i == 0)
        def _():
            dw_ref[...] = jnp.zeros_like(dw_ref)
            db_ref[...] = jnp.zeros_like(db_ref)
            dgg_ref[...] = jnp.zeros_like(dgg_ref)
            dhg_ref[...] = jnp.zeros_like(dhg_ref)

        for cc in range(tm // CHUNK):
            rows = slice(cc * CHUNK, (cc + 1) * CHUNK)
            for g in range(GM_GROUPS):
                cols = slice(g * 128, (g + 1) * 128)
                up = u_ref[rows, cols].astype(F32)
                gp = v_ref[rows, cols].astype(F32)
                u, u_grad = _gelu_and_grad(up)
                gv, gv_grad = _gelu_and_grad(gp)
                rv = _rs(gv)
                gvh = gv * rv
                ggv = gg_ref[:, cols]
                vnb = (gvh * ggv).astype(BF)
                mixed = _dot(wt_ref[g], vnb) + bb_ref[g]
                a = u * mixed
                ra = _rs(a)
                ah = a * ra
                dm = dm_ref[rows, cols].astype(F32)
                dhg_ref[:, cols] += jnp.sum(dm * ah, axis=0, keepdims=True)
                dah = dm * hg_ref[:, cols]
                da = ra * (dah - ah * jnp.mean(dah * ah, axis=-1, keepdims=True))
                du = da * mixed
                dmix = da * u
                db_ref[g] += dmix
                dmb = dmix.astype(BF)
                dw_ref[g] += _dot_bt(dmb, vnb)
                dvn = _dot(wtt_ref[g], dmb)
                dgg_ref[:, cols] += jnp.sum(dvn * gvh, axis=0, keepdims=True)
                dgh = dvn * ggv
                dgv = rv * (dgh - gvh * jnp.mean(dgh * gvh, axis=-1, keepdims=True))
                dp_ref[rows, cols] = (du * u_grad).astype(BF)
                dp_ref[rows, 512 + g * 128:512 + (g + 1) * 128] = (dgv * gv_grad).astype(BF)

        @pl.when(i == nsteps - 1)
        def _():
            r = lax.broadcasted_iota(jnp.int32, (CHUNK, CHUNK), 0)
            c = lax.broadcasted_iota(jnp.int32, (CHUNK, CHUNK), 1)
            for g in range(GM_GROUPS):
                dw_ref[g] = jnp.where(c <= r, dw_ref[g], 0.0)
                db_ref[g] = jnp.broadcast_to(jnp.sum(db_ref[g], axis=-1, keepdims=True), (CHUNK, CHUNK))

    small = lambda shape: pl.BlockSpec(shape, lambda i: (0,) * len(shape))
    res, rode = _ride_call(
        body, "gmlp_bwd", (nsteps,),
        in_specs=[pl.BlockSpec((tm, 512), lambda i: (i, 0)), pl.BlockSpec((tm, 512), lambda i: (i, 1)),
                  pl.BlockSpec((tm, 512), lambda i: (i, 0)), small((1, 512)), small((4, 128, 128)),
                  small((4, 128, 128)), small((4, 128, 128)), small((1, 512))],
        out_specs=[pl.BlockSpec((tm, 1024), lambda i: (i, 0)), small((4, 128, 128)), small((4, 128, 128)),
                   small((1, 512)), small((1, 512))],
        out_shape=[jax.ShapeDtypeStruct((t, 1024), BF), jax.ShapeDtypeStruct((4, 128, 128), F32),
                   jax.ShapeDtypeStruct((4, 128, 128), F32), jax.ShapeDtypeStruct((1, 512), F32),
                   jax.ShapeDtypeStruct((1, 512), F32)],
        scratch_shapes=[], operands=(proj, proj, dmerged, gg, wt, wtt, bb, hg), vmem_mb=32, ride=ride)
    return (*res, rode)


def _other_chips(x, y):
    return ((1 - x, y), (x, 1 - y), (1 - x, 1 - y))


class _GatherExchange:
    def __init__(self, shards):
        n = len(shards)
        self.n = n
        self.in_arrays = list(shards)
        self.out_shape = [jax.ShapeDtypeStruct((N_CHIPS,) + a.shape, a.dtype) for a in shards]
        self.half_rows = [a.shape[0] // 2 for a in shards]
        sems = lambda k: pltpu.SemaphoreType.DMA((k,))
        self.scratch_shapes = [pltpu.VMEM(a.shape, a.dtype) for a in shards] + [
            sems(3 * n), sems(3 * n), sems(3 * n), sems(3 * n), sems(n), sems(n)]

    def _copies(self, ins, outs, scr):
        n = self.n
        stages, (ici_send, ici_recv, d2d_send, d2d_recv, ld_sems, st_sems) = scr[:n], scr[n:]
        x, y, c = lax.axis_index("x"), lax.axis_index("y"), lax.axis_index("c")
        q = 2 * x + y
        loads = [pltpu.make_async_copy(ins[w], stages[w], ld_sems.at[w]) for w in range(n)]
        stores = [pltpu.make_async_copy(stages[w], outs[w].at[q], st_sems.at[w]) for w in range(n)]
        ici, d2d = [], []
        for w in range(n):
            half = pl.ds(c * self.half_rows[w], self.half_rows[w])
            for k, (px, py) in enumerate(_other_chips(x, y)):
                ici.append(pltpu.make_async_remote_copy(
                    src_ref=ins[w].at[half], dst_ref=outs[w].at[q, half], send_sem=ici_send.at[3 * w + k],
                    recv_sem=ici_recv.at[3 * w + k], device_id=(px, py, c), device_id_type=MESH))
                landed = outs[w].at[2 * px + py, half]
                d2d.append(pltpu.make_async_remote_copy(
                    src_ref=landed, dst_ref=landed, send_sem=d2d_send.at[3 * w + k],
                    recv_sem=d2d_recv.at[3 * w + k], device_id=(x, y, 1 - c), device_id_type=MESH))
        return loads, stores, ici, d2d

    def start(self, ins, outs, scr):
        loads, stores, ici, _ = self._copies(ins, outs, scr)
        for cp in loads + ici:
            cp.start()
        for ld, st in zip(loads, stores):
            ld.wait()
            st.start()

    def relay(self, ins, outs, scr):
        _, _, ici, d2d = self._copies(ins, outs, scr)
        for got, fwd in zip(ici, d2d):
            got.wait_recv()
            fwd.start()

    def finish(self, ins, outs, scr):
        _, stores, ici, d2d = self._copies(ins, outs, scr)
        for cp in ici:
            cp.wait_send()
        for cp in d2d + stores:
            cp.wait()


class _SiblingExchange:
    def __init__(self, slabs):
        n = len(slabs)
        self.n = n
        self.in_arrays = list(slabs)
        self.out_shape = [jax.ShapeDtypeStruct((N_CHIPS,) + a.shape[1:], a.dtype) for a in slabs]
        self.scratch_shapes = [pltpu.SemaphoreType.DMA((4 * n,)), pltpu.SemaphoreType.DMA((4 * n,))]

    def _copies(self, ins, outs, scr):
        send_sems, recv_sems = scr
        x, y, c = lax.axis_index("x"), lax.axis_index("y"), lax.axis_index("c")
        return [pltpu.make_async_remote_copy(
            src_ref=ins[w].at[2 * p + (1 - c)], dst_ref=outs[w].at[p], send_sem=send_sems.at[4 * w + p],
            recv_sem=recv_sems.at[4 * w + p], device_id=(x, y, 1 - c), device_id_type=MESH)
            for w in range(self.n) for p in range(N_CHIPS)]

    def start(self, ins, outs, scr):
        for cp in self._copies(ins, outs, scr):
            cp.start()

    def finish(self, ins, outs, scr):
        for cp in self._copies(ins, outs, scr):
            cp.wait()


class _ChipExchange:
    def __init__(self, sums):
        n = len(sums)
        self.n = n
        self.in_arrays = list(sums)
        self.out_shape = [jax.ShapeDtypeStruct(a.shape, a.dtype) for a in sums]
        self.scratch_shapes = [pltpu.SemaphoreType.DMA((3 * n,)), pltpu.SemaphoreType.DMA((3 * n,))]

    def _copies(self, ins, outs, scr):
        send_sems, recv_sems = scr
        x, y, c = lax.axis_index("x"), lax.axis_index("y"), lax.axis_index("c")
        q = 2 * x + y
        return [pltpu.make_async_remote_copy(
            src_ref=ins[w].at[2 * px + py], dst_ref=outs[w].at[q], send_sem=send_sems.at[3 * w + k],
            recv_sem=recv_sems.at[3 * w + k], device_id=(px, py, c), device_id_type=MESH)
            for w in range(self.n) for k, (px, py) in enumerate(_other_chips(x, y))]

    def start(self, ins, outs, scr):
        for cp in self._copies(ins, outs, scr):
            cp.start()

    def finish(self, ins, outs, scr):
        for cp in self._copies(ins, outs, scr):
            cp.wait()


class _NoExchange:
    in_arrays, out_shape, scratch_shapes = (), (), ()

    def start(self, ins, outs, scr):
        pass

    def finish(self, ins, outs, scr):
        pass


def _run_exchange(ex, name):
    n_in, n_out = len(ex.in_arrays), len(ex.out_shape)

    def body(*refs):
        ins, outs, scr = refs[:n_in], refs[n_in:n_in + n_out], refs[n_in + n_out:]
        ex.start(ins, outs, scr)
        if hasattr(ex, "relay"):
            ex.relay(ins, outs, scr)
        ex.finish(ins, outs, scr)

    return pl.pallas_call(
        body, name=name, in_specs=[ANY] * n_in, out_specs=[ANY] * n_out, out_shape=ex.out_shape,
        scratch_shapes=ex.scratch_shapes, compiler_params=_params(24),
    )(*ex.in_arrays)


def _ride_call(body, name, grid, in_specs, out_specs, out_shape, scratch_shapes, operands, vmem_mb, ride=None,
               aliases=None):
    ride = ride or _NoExchange()
    ni, no, ns = len(in_specs), len(out_specs), len(scratch_shapes)
    ri, ro = len(ride.in_arrays), len(ride.out_shape)
    total = math.prod(grid)

    def wrapped(*refs):
        ins, rins = refs[:ni], refs[ni:ni + ri]
        outs, routs = refs[ni + ri:ni + ri + no], refs[ni + ri + no:ni + ri + no + ro]
        scr, rscr = refs[ni + ri + no + ro:ni + ri + no + ro + ns], refs[ni + ri + no + ro + ns:]
        step = pl.program_id(0)
        for ax in range(1, len(grid)):
            step = step * grid[ax] + pl.program_id(ax)

        @pl.when(step == 0)
        def _():
            ride.start(rins, routs, rscr)

        if hasattr(ride, "relay"):
            @pl.when(step == (3 * total) // 4)
            def _():
                ride.relay(rins, routs, rscr)

        body(*ins, *outs, *scr)

        @pl.when(step == total - 1)
        def _():
            ride.finish(rins, routs, rscr)

    res = pl.pallas_call(
        wrapped, name=name, grid=grid, in_specs=list(in_specs) + [ANY] * ri, out_specs=list(out_specs) + [ANY] * ro,
        out_shape=list(out_shape) + list(ride.out_shape),
        scratch_shapes=list(scratch_shapes) + list(ride.scratch_shapes), input_output_aliases=aliases or {},
        compiler_params=_params(vmem_mb, ("arbitrary",) * len(grid)),
    )(*operands, *ride.in_arrays)
    return res[:no], res[no:]


def _neg_log_sig(z):
    n = jnp.maximum(z, 0.0) + jnp.log(1.0 + jnp.exp(-jnp.abs(z)))
    return n, z - n


def _running_sums(n, tri2):
    hi = n.astype(BF)
    lo = (n - hi.astype(F32)).astype(BF)
    return _dot(jnp.concatenate([hi, lo], axis=1), tri2)


def _head_sums(x, h0):
    s0 = jnp.sum(jnp.where(h0, x, 0.0), axis=-1, keepdims=True)
    s1 = jnp.sum(jnp.where(h0, 0.0, x), axis=-1, keepdims=True)
    return jnp.where(h0, s0, s1)


def _sb_setup(q_ref, tq):
    lane = lax.broadcasted_iota(jnp.int32, (tq, 128), 1)
    h0 = lane < HEAD_LANES
    qs = q_ref[...] * SB_SCALE
    zero = jnp.zeros_like(qs)
    qst = jnp.concatenate([jnp.where(h0, qs, zero), jnp.where(h0, zero, qs)], axis=0)
    r = lax.broadcasted_iota(jnp.int32, (2 * tq, tq), 0)
    c = lax.broadcasted_iota(jnp.int32, (2 * tq, tq), 1)
    causal = c < jnp.where(r >= tq, r - tq, r)
    return h0, qst, causal


def _tri(tq, op):
    return op(lax.broadcasted_iota(jnp.int32, (tq, tq), 0), lax.broadcasted_iota(jnp.int32, (tq, tq), 1)).astype(BF)


def _sb_fwd(proj, merged, hg, nb, s, tq, ride=None):
    t = nb * s
    tq = min(tq, s)
    nq = s // tq

    def body(q_ref, k_ref, v_ref, hg_ref, merged_ref, o_ref, tot_ref, mb_ref, nblk_ref, acc, cr):
        del merged_ref
        i = pl.program_id(2)
        h0, qst, causal = _sb_setup(q_ref, tq)
        tri_gt = _tri(tq, lambda r, c: r > c)
        tri_gt = jnp.concatenate([tri_gt, tri_gt], axis=0)

        def block(j, masked, c_in):
            start = pl.multiple_of(j * tq, tq)
            kj = k_ref[pl.ds(start, tq), :]
            vj = v_ref[pl.ds(start, tq), :]
            n, l = _neg_log_sig(_dot_bt(qst, kj))
            if masked:
                n = jnp.where(causal, n, 0.0)
            a = jnp.exp(l - (_running_sums(n, tri_gt) + c_in))
            if masked:
                a = jnp.where(causal, a, 0.0)
            return _dot(a.astype(BF), vj), c_in + jnp.sum(n, axis=-1, keepdims=True)

        @pl.when(i == 0)
        def _():
            acc[...], cr[...] = block(0, True, jnp.zeros((2 * tq, 1), F32))

        @pl.when(i > 0)
        def _():
            p_diag, c_diag = block(i, True, jnp.zeros((2 * tq, 1), F32))
            p_prev, c_prev = block(i - 1, False, c_diag)
            acc[...] = p_diag + p_prev
            cr[...] = c_prev

        def cond(carry):
            return jnp.logical_and(carry[0] < i, carry[1] < -SB_SKIP)

        def step(carry):
            p, c_new = block(i - 1 - carry[0], False, cr[...])
            acc[...] += p
            cr[...] = c_new
            return carry[0] + 1, jnp.min(c_new)

        walked, _ = lax.while_loop(cond, step, (jnp.minimum(i, 1), jnp.min(cr[...])))

        o = jnp.where(h0, acc[0:tq, :], acc[tq:2 * tq, :])
        o_ref[...] = o
        tot_ref[...] = jnp.where(h0, cr[0:tq, :], cr[tq:2 * tq, :])
        ro = lax.rsqrt(_head_sums(o * o, h0) * (1.0 / HEAD_LANES) + EPS)
        mb_ref[...] = (o * ro * hg_ref[...]).astype(BF)
        nblk_ref[...] = jnp.full((8, 128), walked.astype(F32))

    blk = lambda col0: pl.BlockSpec((tq, 128), lambda b, hp, i: (b * nq + i, col0 + hp))
    seq = lambda col0: pl.BlockSpec((s, 128), lambda b, hp, i: (b, col0 + hp))
    (o, tot, mb, nblk), rode = _ride_call(
        body, "sb_fwd", (nb, 4, nq),
        in_specs=[blk(8), seq(12), seq(16), pl.BlockSpec((1, 128), lambda b, hp, i: (0, 4 + hp)), ANY],
        out_specs=[blk(0), blk(0), blk(4), pl.BlockSpec((None, None, 8, 128), lambda b, hp, i: (b, hp, i, 0))],
        out_shape=[jax.ShapeDtypeStruct((t, 512), F32), jax.ShapeDtypeStruct((t, 512), F32),
                   jax.ShapeDtypeStruct((t, 1024), BF), jax.ShapeDtypeStruct((nb, 4, nq * 8, 128), F32)],
        scratch_shapes=[pltpu.VMEM((2 * tq, 128), F32), pltpu.VMEM((2 * tq, 1), F32)],
        operands=(proj, proj, proj, hg, merged), vmem_mb=40, ride=ride, aliases={4: 2})
    return o, tot, mb, nblk, rode


def _sb_bwd(proj, o_sb, tot, nblk, dmerged, hg, nb, s, tq, ride=None):
    t = nb * s
    tq = min(tq, s)
    nq = s // tq

    def body(q_ref, k_ref, v_ref, o_ref, tot_ref, nblk_ref, dm_ref, hg_ref,
             dq_ref, dk_ref, dv_ref, dhg_ref, dk_acc, dv_acc, dq_acc, cm, cg):
        i = pl.program_id(2)
        h0, qst, causal = _sb_setup(q_ref, tq)
        tri_le = _tri(tq, lambda r, c: r <= c)
        tri_le = jnp.concatenate([tri_le, tri_le], axis=0)
        tri_lt = _tri(tq, lambda r, c: r < c)

        @pl.when(i == 0)
        def _():
            dk_acc[...] = jnp.zeros_like(dk_acc)
            dv_acc[...] = jnp.zeros_like(dv_acc)
            dhg_ref[...] = jnp.zeros_like(dhg_ref)

        for ref in (dq_acc, cm, cg):
            ref[...] = jnp.zeros_like(ref)

        o = o_ref[...]
        ro = lax.rsqrt(_head_sums(o * o, h0) * (1.0 / HEAD_LANES) + EPS)
        oh = o * ro
        dm = dm_ref[...].astype(F32)
        dhg_ref[...] += jnp.sum(dm * oh, axis=0, keepdims=True)
        doh = dm * hg_ref[...]
        do = (ro * (doh - oh * (_head_sums(doh * oh, h0) * (1.0 / HEAD_LANES)))).astype(BF)
        zb = jnp.zeros_like(do)
        dost = jnp.concatenate([jnp.where(h0, do, zb), jnp.where(h0, zb, do)], axis=0)
        tots = jnp.concatenate([tot_ref[:, 0:1], tot_ref[:, HEAD_LANES:HEAD_LANES + 1]], axis=0)

        def block(j, masked, cm_in, cg_in):
            start = pl.multiple_of(j * tq, tq)
            kj = k_ref[pl.ds(start, tq), :]
            vj = v_ref[pl.ds(start, tq), :]
            n, l = _neg_log_sig(_dot_bt(qst, kj))
            if masked:
                n = jnp.where(causal, n, 0.0)
            a = jnp.exp(l - (tots - cm_in - _running_sums(n, tri_le)))
            if masked:
                a = jnp.where(causal, a, 0.0)
            gm = a * _dot_bt(dost, vj)
            pp = cg_in + _dot(gm.astype(BF), tri_lt)
            dz = gm - jnp.exp(l) * (gm + pp)
            if masked:
                dz = jnp.where(causal, dz, 0.0)
            dzb = dz.astype(BF)
            dk_acc[pl.ds(start, tq), :] += _dot_at(dzb, qst)
            dv_acc[pl.ds(start, tq), :] += _dot_at(a.astype(BF), dost)
            return (_dot(dzb, kj), cm_in + jnp.sum(n, axis=-1, keepdims=True),
                    cg_in + jnp.sum(gm, axis=-1, keepdims=True))

        def step(j, carry):
            dq, cm[...], cg[...] = block(j, False, cm[...], cg[...])
            dq_acc[...] += dq
            return carry

        walked = jnp.clip(jnp.max(nblk_ref[...]).astype(jnp.int32), jnp.minimum(i, 1), i)
        lax.fori_loop(i - walked, i - 1, step, 0)

        @pl.when(i == 0)
        def _():
            dq_acc[...] = block(0, True, cm[...], cg[...])[0]

        @pl.when(i > 0)
        def _():
            dq_prev, cm_prev, cg_prev = block(i - 1, False, cm[...], cg[...])
            dq_acc[...] += dq_prev + block(i, True, cm_prev, cg_prev)[0]

        dq_ref[...] = (jnp.where(h0, dq_acc[0:tq, :], dq_acc[tq:2 * tq, :]) * SB_SCALE).astype(BF)

        @pl.when(i == nq - 1)
        def _():
            dk_ref[...] = dk_acc[...].astype(BF)
            dv_ref[...] = dv_acc[...].astype(BF)

    blk = lambda col0: pl.BlockSpec((tq, 128), lambda b, hp, i: (b * nq + i, col0 + hp))
    seq = lambda col0: pl.BlockSpec((s, 128), lambda b, hp, i: (b, col0 + hp))
    (dq, dk, dv, dhg), rode = _ride_call(
        body, "sb_bwd", (nb, 4, nq),
        in_specs=[blk(8), seq(12), seq(16), blk(0), blk(0),
                  pl.BlockSpec((None, None, 8, 128), lambda b, hp, i: (b, hp, i, 0)), blk(4),
                  pl.BlockSpec((1, 128), lambda b, hp, i: (0, 4 + hp))],
        out_specs=[blk(0), seq(0), seq(0), pl.BlockSpec((None, 1, 128), lambda b, hp, i: (b, 0, hp))],
        out_shape=[jax.ShapeDtypeStruct((t, 512), BF), jax.ShapeDtypeStruct((t, 512), BF),
                   jax.ShapeDtypeStruct((t, 512), BF), jax.ShapeDtypeStruct((nb, 1, 512), F32)],
        scratch_shapes=[pltpu.VMEM((s, 128), F32), pltpu.VMEM((s, 128), F32), pltpu.VMEM((2 * tq, 128), F32),
                        pltpu.VMEM((2 * tq, 1), F32), pltpu.VMEM((2 * tq, 1), F32)],
        operands=(proj, proj, proj, o_sb, tot, nblk, dmerged, hg), vmem_mb=40, ride=ride)
    return dq, dk, dv, dhg, rode


def _softmax_rows(sc):
    e = jnp.exp(sc - jnp.max(sc, axis=-1, keepdims=True))
    return e / jnp.sum(e, axis=-1, keepdims=True)


def _mix_cross_fwd(x, merged, w_out, gc, w_cq, kv, w_co, s, tm):
    t, d = x.shape
    tm = min(tm, s)
    per = s // tm
    inv = 1.0 / math.sqrt(X_HEAD_DIM)

    def body(x_ref, m_ref, wo_ref, gc_ref, wq_ref, kv_ref, wc_ref, h1_ref, h2_ref, hn_ref, oc_ref):
        h1 = x_ref[...] + _dot(m_ref[...], wo_ref[...])
        h1_ref[...] = h1
        hn = (h1 * _rs(h1) * gc_ref[...]).astype(BF)
        hn_ref[...] = hn
        qc = _dot(hn, wq_ref[...]).astype(BF)
        for h in range(X_HEADS):
            cols = slice(h * X_HEAD_DIM, (h + 1) * X_HEAD_DIM)
            kh = kv_ref[:, h * X_HEAD_DIM:(h + 1) * X_HEAD_DIM]
            vh = kv_ref[:, d + h * X_HEAD_DIM:d + (h + 1) * X_HEAD_DIM]
            p = _softmax_rows(_dot_bt(qc[:, cols], kh) * inv)
            oc_ref[:, cols] = _dot(p.astype(BF), vh).astype(BF)
        h2_ref[...] = h1 + _dot(oc_ref[...], wc_ref[...])

    row = lambda width: pl.BlockSpec((tm, width), lambda i: (i, 0))
    full = lambda a, b: pl.BlockSpec((a, b), lambda i: (0, 0))
    return pl.pallas_call(
        body, name="mix_cross_fwd", grid=(t // tm,),
        in_specs=[row(d), row(d), full(d, d), full(1, d), full(d, d),
                  pl.BlockSpec((N_MEM, 2 * d), lambda i: (i // per, 0)), full(d, d)],
        out_specs=[row(d), row(d), row(d), row(d)],
        out_shape=[jax.ShapeDtypeStruct((t, d), F32), jax.ShapeDtypeStruct((t, d), F32),
                   jax.ShapeDtypeStruct((t, d), BF), jax.ShapeDtypeStruct((t, d), BF)],
        compiler_params=_params(48, ("arbitrary",)),
    )(x, merged, w_out, gc, w_cq, kv, w_co)


def _cross_bwd(dh2, h1, gc, w_cq, kv, w_co, s, tm):
    t, d = dh2.shape
    tm = min(tm, s)
    per = s // tm
    nb = t // s
    inv = 1.0 / math.sqrt(X_HEAD_DIM)

    def body(dh2_ref, h1_ref, gc_ref, wq_ref, kv_ref, wc_ref, dh1_ref, dqc_ref, dkv_ref, dgc_ref):
        i = pl.program_id(0)

        @pl.when(i == 0)
        def _():
            dgc_ref[...] = jnp.zeros_like(dgc_ref)

        @pl.when(i % per == 0)
        def _():
            dkv_ref[...] = jnp.zeros_like(dkv_ref)

        dh2 = dh2_ref[...]
        h1 = h1_ref[...]
        r = _rs(h1)
        h1h = h1 * r
        gcv = gc_ref[...]
        hn = (h1h * gcv).astype(BF)
        qc = _dot(hn, wq_ref[...]).astype(BF)
        do = _dot_bt(dh2.astype(BF), wc_ref[...]).astype(BF)
        for h in range(X_HEADS):
            cols = slice(h * X_HEAD_DIM, (h + 1) * X_HEAD_DIM)
            vcols = slice(d + h * X_HEAD_DIM, d + (h + 1) * X_HEAD_DIM)
            kh = kv_ref[:, cols]
            vh = kv_ref[:, vcols]
            p = _softmax_rows(_dot_bt(qc[:, cols], kh) * inv)
            dp = _dot_bt(do[:, cols], vh)
            ds = (p * (dp - jnp.sum(dp * p, axis=-1, keepdims=True)) * inv).astype(BF)
            dqc_ref[:, cols] = _dot(ds, kh).astype(BF)
            dkv_ref[:, cols] += _dot_at(ds, qc[:, cols])
            dkv_ref[:, vcols] += _dot_at(p.astype(BF), do[:, cols])
        dhn = _dot_bt(dqc_ref[...], wq_ref[...])
        dx, dg = _rms_bwd(dhn, h1h, r, gcv)
        dh1_ref[...] = dh2 + dx
        dgc_ref[...] += jnp.sum(dg, axis=0, keepdims=True)

    row = lambda width: pl.BlockSpec((tm, width), lambda i: (i, 0))
    full = lambda a, b: pl.BlockSpec((a, b), lambda i: (0, 0))
    kvspec = pl.BlockSpec((N_MEM, 2 * d), lambda i: (i // per, 0))
    return pl.pallas_call(
        body, name="cross_bwd", grid=(t // tm,),
        in_specs=[row(d), row(d), full(1, d), full(d, d), kvspec, full(d, d)],
        out_specs=[row(d), row(d), kvspec, full(1, d)],
        out_shape=[jax.ShapeDtypeStruct((t, d), F32), jax.ShapeDtypeStruct((t, d), BF),
                   jax.ShapeDtypeStruct((nb * N_MEM, 2 * d), F32), jax.ShapeDtypeStruct((1, d), F32)],
        compiler_params=_params(48, ("arbitrary",)),
    )(dh2, h1, gc, w_cq, kv, w_co)


def _mem_bwd(mem, gm, dkv, w_ckv, tm):
    t, d = mem.shape
    tm = min(tm, t)

    def body(mem_ref, dkv_ref, w_ref, dg_ref):
        @pl.when(pl.program_id(0) == 0)
        def _():
            dg_ref[...] = jnp.zeros_like(dg_ref)

        mv = mem_ref[...]
        dmn = _dot_bt(dkv_ref[...].astype(BF), w_ref[...])
        dg_ref[...] += jnp.sum(dmn * (mv * _rs(mv)), axis=0, keepdims=True)

    del gm
    return pl.pallas_call(
        body, name="mem_bwd", grid=(t // tm,),
        in_specs=[pl.BlockSpec((tm, d), lambda i: (i, 0)), pl.BlockSpec((tm, 2 * d), lambda i: (i, 0)),
                  pl.BlockSpec((d, 2 * d), lambda i: (0, 0))],
        out_specs=pl.BlockSpec((1, d), lambda i: (0, 0)),
        out_shape=jax.ShapeDtypeStruct((1, d), F32),
        compiler_params=_params(32, ("arbitrary",)),
    )(mem, dkv, w_ckv)


def _ffn_loss_fwd(h2, gf, w1, w2, gl, target, tm):
    t, d = h2.shape
    tm = min(tm, t)

    def body(h2_ref, gf_ref, w1_ref, w2_ref, gl_ref, tg_ref, hn_ref, f_ref, dh3_ref, dgl_ref, loss_ref):
        @pl.when(pl.program_id(0) == 0)
        def _():
            dgl_ref[...] = jnp.zeros_like(dgl_ref)
            loss_ref[...] = jnp.zeros_like(loss_ref)

        h2 = h2_ref[...]
        hn = (h2 * _rs(h2) * gf_ref[...]).astype(BF)
        hn_ref[...] = hn
        h3 = h2
        for c in range(4):
            f = jnp.maximum(_dot(hn, w1_ref[c]), 0.0)
            f_ref[:, c * 1024:(c + 1) * 1024] = f.astype(BF)
            h3 = h3 + _dot((f * f).astype(BF), w2_ref[c])
        r3 = _rs(h3)
        yh = h3 * r3
        glv = gl_ref[...]
        e = yh * glv - tg_ref[...]
        loss_ref[...] += 0.5 * jnp.sum(jnp.sum(e * e, axis=-1, keepdims=True) * (1.0 / d), axis=0, keepdims=True)
        dy = e * (1.0 / d)
        dx, dg = _rms_bwd(dy, yh, r3, glv)
        dh3_ref[...] = dx
        dgl_ref[...] += jnp.sum(dg, axis=0, keepdims=True)

    row = lambda width: pl.BlockSpec((tm, width), lambda i: (i, 0))
    return pl.pallas_call(
        body, name="ffn_loss_fwd", grid=(t // tm,),
        in_specs=[row(d), pl.BlockSpec((1, d), lambda i: (0, 0)), pl.BlockSpec((4, d, 1024), lambda i: (0, 0, 0), pipeline_mode=pl.Buffered(1)),
                  pl.BlockSpec((4, 1024, d), lambda i: (0, 0, 0), pipeline_mode=pl.Buffered(1)),
                  pl.BlockSpec((1, d), lambda i: (0, 0)), row(d)],
        out_specs=[row(d), row(D_FF), row(d), pl.BlockSpec((1, d), lambda i: (0, 0)),
                   pl.BlockSpec((1, 1), lambda i: (0, 0))],
        out_shape=[jax.ShapeDtypeStruct((t, d), BF), jax.ShapeDtypeStruct((t, D_FF), BF),
                   jax.ShapeDtypeStruct((t, d), F32), jax.ShapeDtypeStruct((1, d), F32),
                   jax.ShapeDtypeStruct((1, 1), F32)],
        compiler_params=_params(56, ("arbitrary",)),
    )(h2, gf, w1, w2, gl, target)


def _ffn_bwd(dh3, f, h2, gf, w1, w2, tm):
    t, d = h2.shape
    tm = min(tm, t)

    def body(dh3_ref, f_ref, h2_ref, gf_ref, w1_ref, w2_ref, dh2_ref, dpre_ref, dgf_ref):
        @pl.when(pl.program_id(0) == 0)
        def _():
            dgf_ref[...] = jnp.zeros_like(dgf_ref)

        dh3 = dh3_ref[...]
        dh3b = dh3.astype(BF)
        dhn = jnp.zeros((tm, d), F32)
        for c in range(4):
            cols = slice(c * 1024, (c + 1) * 1024)
            dpre = (_dot_bt(dh3b, w2_ref[c]) * (2.0 * f_ref[:, cols].astype(F32))).astype(BF)
            dpre_ref[:, cols] = dpre
            dhn = dhn + _dot_bt(dpre, w1_ref[c])
        h2 = h2_ref[...]
        r = _rs(h2)
        dx, dg = _rms_bwd(dhn, h2 * r, r, gf_ref[...])
        dh2_ref[...] = dh3 + dx
        dgf_ref[...] += jnp.sum(dg, axis=0, keepdims=True)

    row = lambda width: pl.BlockSpec((tm, width), lambda i: (i, 0))
    return pl.pallas_call(
        body, name="ffn_bwd", grid=(t // tm,),
        in_specs=[row(d), row(D_FF), row(d), pl.BlockSpec((1, d), lambda i: (0, 0)),
                  pl.BlockSpec((4, d, 1024), lambda i: (0, 0, 0), pipeline_mode=pl.Buffered(1)),
                  pl.BlockSpec((4, 1024, d), lambda i: (0, 0, 0), pipeline_mode=pl.Buffered(1))],
        out_specs=[row(d), row(D_FF), pl.BlockSpec((1, d), lambda i: (0, 0))],
        out_shape=[jax.ShapeDtypeStruct((t, d), F32), jax.ShapeDtypeStruct((t, D_FF), BF),
                   jax.ShapeDtypeStruct((1, d), F32)],
        compiler_params=_params(56, ("arbitrary",)),
    )(dh3, f, h2, gf, w1, w2)


def _in_bwd(dproj, dh1, x, g, w_in, tm, ride=None):
    t, d = x.shape
    n = w_in.shape[1]
    tm = min(tm, t)

    def body(dp_ref, dh1_ref, x_ref, g_ref, w_ref, dx_ref, dg_ref):
        @pl.when(pl.program_id(0) == 0)
        def _():
            dg_ref[...] = jnp.zeros_like(dg_ref)

        dxn = _dot_bt(dp_ref[...], w_ref[...])
        xv = x_ref[...]
        r = _rs(xv)
        dx, dg = _rms_bwd(dxn, xv * r, r, g_ref[...])
        dx_ref[...] = dh1_ref[...] + dx
        dg_ref[...] += jnp.sum(dg, axis=0, keepdims=True)

    row = lambda width: pl.BlockSpec((tm, width), lambda i: (i, 0))
    (dx, dg), rode = _ride_call(
        body, "in_bwd", (t // tm,),
        in_specs=[row(n), row(d), row(d), pl.BlockSpec((1, d), lambda i: (0, 0)),
                  pl.BlockSpec((d, n), lambda i: (0, 0))],
        out_specs=[row(d), pl.BlockSpec((1, d), lambda i: (0, 0))],
        out_shape=[jax.ShapeDtypeStruct((t, d), F32), jax.ShapeDtypeStruct((1, d), F32)],
        scratch_shapes=[], operands=(dproj, dh1, x, g, w_in), vmem_mb=48, ride=ride)
    return dx, dg, rode


class _GradReduce:
    def __init__(self, c_idx):
        self.c_idx = c_idx
        self.sums = {}

    def sibling(self, slabs):
        return _SiblingExchange(slabs)

    def chip(self, names, slabs, recv):
        for k, a, r in zip(names, slabs, recv):
            self.sums[k] = _chip_sum(a, r, self.c_idx, "chip_sum_" + k)
        return _ChipExchange([self.sums[k] for k in names])


def _full_weights(gathered):
    d = D_MODEL
    out = {}
    for k, a in gathered.items():
        if k in ("w_in", "w_ckv", "w_ff1"):
            out[k] = a.transpose(1, 0, 2).reshape(d, -1)
        else:
            out[k] = a.reshape(-1, d)
    return out


def _slabs(a):
    return a.reshape(N_DEV, -1, a.shape[-1])


def _local_step(x, mem, target, small, big, nb, s, tq=256, gather_rest=None, reduce=None):
    d = D_MODEL
    g_mix, g_v, w_sp, b_sp, g_head, g_cross, g_mem, g_ffn, g_fin = (
        small[k] for k in ("norm_mix_g", "gm_v_norm_g", "w_spatial", "b_spatial", "head_norm_g", "norm_cross_g",
                           "norm_mem_g", "norm_ffn_g", "norm_final_g"))
    tri = jnp.tril(jnp.ones((CHUNK, CHUNK), dtype=bool))
    w_sp_m = jnp.where(tri[None], w_sp, 0.0)
    wt = w_sp_m.astype(BF)
    wtt = jnp.swapaxes(w_sp_m, 1, 2).astype(BF)
    bb = jnp.broadcast_to(b_sp[:, :, None], (GM_GROUPS, CHUNK, CHUNK))
    hg_a = g_head[:, :GM_WIDTH]

    proj, xn = _norm_matmul(x, g_mix, big["w_in"], 512, "in_proj")
    merged = _gmlp_fwd(proj, g_v, wt, bb, hg_a, 512)
    o_sb, tot, merged, nblk, gathered = _sb_fwd(proj, merged, g_head, nb, s, tq, ride=gather_rest)
    if gather_rest is not None:
        big = dict(big, **_full_weights(dict(zip(BIG[1:], gathered))))
    w1c = big["w_ff1"].reshape(d, 4, 1024).transpose(1, 0, 2)
    w2c = big["w_ff2"].reshape(4, 1024, d)
    kv, memn = _norm_matmul(mem, g_mem, big["w_ckv"], 512, "mem_proj")
    h1, h2, hn, oc = _mix_cross_fwd(x, merged, big["w_out"], g_cross, big["w_cq"], kv, big["w_co"], s, 512)
    hn2, f, dh3, d_fin, loss = _ffn_loss_fwd(h2, g_ffn, w1c, w2c, g_fin, target, 512)

    gbig = {}
    dh2, dpre, d_ffn = _ffn_bwd(dh3, f, h2, g_ffn, w1c, w2c, 512)
    gbig["w_ff2"] = _slabs(_wgrad(f, dh3, 1024, 1024, "wgrad_ff2", square_a=True))
    gbig["w_ff1"] = _slabs(_wgrad(hn2, dpre, 1024, 1024, "wgrad_ff1", col_shards=4))
    dh1, dqc, dkv, d_cross = _cross_bwd(dh2, h1, g_cross, big["w_cq"], kv, big["w_co"], s, 512)
    gbig["w_co"] = _slabs(_wgrad(oc, dh2, 1024, 1024, "wgrad_co"))
    gbig["w_cq"] = _slabs(_wgrad(hn, dqc, 1024, 1024, "wgrad_cq"))
    gbig["w_ckv"] = _slabs(_wgrad(memn, dkv, 512, 1024, "wgrad_ckv", col_shards=4))
    d_mem = _mem_bwd(mem, g_mem, dkv, big["w_ckv"], 512)
    dmerged = _matmul_bt(dh1, big["w_out"], 512, "out_bwd")
    gbig["w_out"] = _slabs(_wgrad(merged, dh1, 1024, 1024, "wgrad_out"))
    rest = BIG[1:]
    ride = reduce.sibling([gbig[k] for k in rest]) if reduce else None
    dugv, d_wsp, d_bb, d_gv, d_hga, recv = _gmlp_bwd(proj, dmerged, g_v, wt, wtt, bb, hg_a, 512, ride=ride)
    ride = reduce.chip(rest, [gbig[k] for k in rest], recv) if reduce else None
    dq, dk, dv, d_hgb, parts_rest = _sb_bwd(proj, o_sb, tot, nblk, dmerged, g_head, nb, s, tq, ride=ride)
    dproj = jnp.concatenate([dugv, dq, dk, dv], axis=1)
    gbig["w_in"] = _slabs(_wgrad(xn, dproj, 640, 1024, "wgrad_in", col_shards=4))
    last = None
    if reduce:
        recv = _run_exchange(reduce.sibling([gbig["w_in"]]), "grad_sibling_exchange_w_in")
        last = reduce.chip(["w_in"], [gbig["w_in"]], recv)
    grad_x, d_mix, _ = _in_bwd(dproj, dh1, x, g_mix, big["w_in"], 512)
    parts = dict(zip(rest, parts_rest))

    gsmall = {
        "norm_mix_g": d_mix, "gm_v_norm_g": d_gv, "w_spatial": d_wsp, "b_spatial": d_bb[:, :, 0],
        "head_norm_g": jnp.concatenate([d_hga, jnp.sum(d_hgb, axis=0)], axis=1), "norm_cross_g": d_cross,
        "norm_mem_g": d_mem, "norm_ffn_g": d_ffn, "norm_final_g": d_fin,
    }
    return loss, grad_x, gsmall, gbig, parts, last


BIG = ("w_in", "w_out", "w_cq", "w_ckv", "w_co", "w_ff1", "w_ff2")
SMALL = ("norm_mix_g", "gm_v_norm_g", "w_spatial", "b_spatial", "head_norm_g", "norm_cross_g", "norm_mem_g",
         "norm_ffn_g", "norm_final_g")


def _local_copies_start(srcs, stages, sems):
    loads = [pltpu.make_async_copy(src, stage, sems.at[w]) for w, (src, stage) in enumerate(zip(srcs, stages))]
    for ld in loads:
        ld.start()
    return loads


def _local_copies_finish(loads, stages, dsts, sems):
    stores = []
    for w, (ld, stage, dst) in enumerate(zip(loads, stages, dsts)):
        ld.wait()
        st = pltpu.make_async_copy(stage, dst, sems.at[w])
        st.start()
        stores.append(st)
    for st in stores:
        st.wait()


def _chip_sum(slabs, recv, c_idx, name):
    _, r, cw = slabs.shape
    tr = min(r, 256)

    def body(c_ref, a_ref, b_ref, o_ref):
        del c_ref
        o_ref[...] = (a_ref[...] + b_ref[...]).astype(BF)

    return pl.pallas_call(
        body, name=name,
        grid_spec=pltpu.PrefetchScalarGridSpec(
            num_scalar_prefetch=1, grid=(N_CHIPS, r // tr),
            in_specs=[pl.BlockSpec((None, tr, cw), lambda p, i, c_ref: (2 * p + c_ref[0], i, 0)),
                      pl.BlockSpec((None, tr, cw), lambda p, i, c_ref: (p, i, 0))],
            out_specs=pl.BlockSpec((None, tr, cw), lambda p, i, c_ref: (p, i, 0))),
        out_shape=jax.ShapeDtypeStruct((N_CHIPS, r, cw), BF),
        compiler_params=_params(32, ("arbitrary", "arbitrary")),
    )(c_idx, *_in_hbm(slabs, recv))


def _sum4(sums, parts, q_idx, name):
    _, r, cw = parts.shape
    tr = min(r, 256)

    def body(q_ref, own_ref, a_ref, b_ref, c_ref, o_ref):
        del q_ref
        o_ref[...] = ((own_ref[...].astype(F32) + a_ref[...].astype(F32)) + b_ref[...].astype(F32)) + c_ref[
            ...].astype(F32)

    spec = lambda k: pl.BlockSpec((None, tr, cw), lambda i, q_ref: ((q_ref[0] + k) % N_CHIPS, i, 0))
    return pl.pallas_call(
        body, name=name,
        grid_spec=pltpu.PrefetchScalarGridSpec(
            num_scalar_prefetch=1, grid=(r // tr,), in_specs=[spec(0), spec(1), spec(2), spec(3)],
            out_specs=pl.BlockSpec((tr, cw), lambda i, q_ref: (i, 0))),
        out_shape=jax.ShapeDtypeStruct((r, cw), F32),
        compiler_params=_params(32, ("arbitrary",)),
    )(q_idx, *_in_hbm(sums, parts, parts, parts))


def _half_exchange(halves):
    n = len(halves)

    def body(*refs):
        ins, outs, stages = refs[:n], refs[n:2 * n], refs[2 * n:3 * n]
        send_sems, recv_sems, ld_sems, st_sems = refs[3 * n:]
        x, y, c = lax.axis_index("x"), lax.axis_index("y"), lax.axis_index("c")
        loads = _local_copies_start(ins, stages, ld_sems)
        copies = []
        for w in range(n):
            cp = pltpu.make_async_remote_copy(
                src_ref=ins[w], dst_ref=outs[w].at[c], send_sem=send_sems.at[w], recv_sem=recv_sems.at[w],
                device_id=(x, y, 1 - c), device_id_type=MESH)
            cp.start()
            copies.append(cp)
        _local_copies_finish(loads, stages, [outs[w].at[c] for w in range(n)], st_sems)
        for cp in copies:
            cp.wait()

    return pl.pallas_call(
        body, name="grad_half_exchange",
        in_specs=[ANY] * n, out_specs=[ANY] * n,
        out_shape=[jax.ShapeDtypeStruct((2,) + a.shape, a.dtype) for a in halves],
        scratch_shapes=[pltpu.VMEM(a.shape, a.dtype) for a in halves] + [
            pltpu.SemaphoreType.DMA((n,)), pltpu.SemaphoreType.DMA((n,)),
            pltpu.SemaphoreType.DMA((n,)), pltpu.SemaphoreType.DMA((n,))],
        compiler_params=_params(24),
    )(*halves)


def _small_all_reduce(packed, ride=None):
    rows = packed.shape[0]
    ride = ride or _NoExchange()
    ri, ro = len(ride.in_arrays), len(ride.out_shape)

    def body(*refs):
        in_ref, rins, out_ref, routs = refs[0], refs[1:1 + ri], refs[1 + ri], refs[2 + ri:2 + ri + ro]
        buf, send_sems, recv_sems = refs[2 + ri + ro:5 + ri + ro]
        rscr = refs[5 + ri + ro:]
        ride.start(rins, routs, rscr)
        x, y, c = lax.axis_index("x"), lax.axis_index("y"), lax.axis_index("c")
        me = 4 * x + 2 * y + c
        buf[me] = in_ref[...]
        copies = []
        for k in range(1, N_DEV):
            bx, by, bc = (k >> 2) & 1, (k >> 1) & 1, k & 1
            peer = (x ^ bx, y ^ by, c ^ bc)
            cp = pltpu.make_async_remote_copy(
                src_ref=in_ref, dst_ref=buf.at[me], send_sem=send_sems.at[k - 1], recv_sem=recv_sems.at[k - 1],
                device_id=peer, device_id_type=MESH)
            cp.start()
            copies.append(cp)
        for cp in copies:
            cp.wait()
        acc = buf[0]
        for dev in range(1, N_DEV):
            acc = acc + buf[dev]
        out_ref[...] = acc
        ride.finish(rins, routs, rscr)

    vmem = pl.BlockSpec(memory_space=pltpu.VMEM)
    res = pl.pallas_call(
        body, name="small_all_reduce",
        in_specs=[vmem] + [ANY] * ri, out_specs=[vmem] + [ANY] * ro,
        out_shape=[jax.ShapeDtypeStruct(packed.shape, F32)] + list(ride.out_shape),
        scratch_shapes=[pltpu.VMEM((N_DEV, rows, 128), F32), pltpu.SemaphoreType.DMA((N_DEV - 1,)),
                        pltpu.SemaphoreType.DMA((N_DEV - 1,))] + list(ride.scratch_shapes),
        compiler_params=_params(16),
    )(packed, *ride.in_arrays)
    return res[0], res[1:]


def _adamw(g, w, m, v, name):
    r, cw = g.shape
    tr = 256 if r % 256 == 0 else r

    def body(g_ref, w_ref, m_ref, v_ref, d_ref, nm_ref, nv_ref):
        gv = g_ref[...]
        nm = ADAM_B1 * m_ref[...] + (1.0 - ADAM_B1) * gv
        nv = ADAM_B2 * v_ref[...] + (1.0 - ADAM_B2) * (gv * gv)
        m_hat = nm / (1.0 - ADAM_B1 ** ADAM_STEP)
        v_hat = nv / (1.0 - ADAM_B2 ** ADAM_STEP)
        d_ref[...] = -ADAM_LR * (m_hat / (jnp.sqrt(v_hat) + ADAM_EPS) + ADAM_WD * w_ref[...])
        nm_ref[...] = nm
        nv_ref[...] = nv

    spec = pl.BlockSpec((tr, cw), lambda i: (i, 0))
    return pl.pallas_call(
        body, name=name, grid=(r // tr,),
        in_specs=[spec] * 4, out_specs=[spec] * 3,
        out_shape=[jax.ShapeDtypeStruct((r, cw), F32)] * 3,
        compiler_params=_params(32, ("arbitrary",)),
    )(*_in_hbm(g, w, m, v))


def _small_params(args):
    small = {k: args[k].reshape(1, -1) for k in SMALL}
    small["w_spatial"] = args["w_spatial"][0]
    small["b_spatial"] = args["b_spatial"][0]
    return small


def _pack(parts, rows):
    flat = jnp.concatenate([p.reshape(-1).astype(F32) for p in parts])
    return jnp.pad(flat, (0, rows * 128 - flat.shape[0])).reshape(rows, 128)


def _unpack(packed, shapes):
    flat = packed.reshape(-1)
    out, off = [], 0
    for shp in shapes:
        size = math.prod(shp)
        out.append(flat[off:off + size].reshape(shp))
        off += size
    return out


def kernel(x, mem, norm_mix_g, w_in, gm_v_norm_g, w_spatial, b_spatial, head_norm_g, w_out, norm_cross_g, norm_mem_g, w_cq, w_ckv, w_co, norm_ffn_g, w_ff1, w_ff2, norm_final_g, loss_target, m_norm_mix_g, m_w_in, m_gm_v_norm_g, m_w_spatial, m_b_spatial, m_head_norm_g, m_w_out, m_norm_cross_g, m_norm_mem_g, m_w_cq, m_w_ckv, m_w_co, m_norm_ffn_g, m_w_ff1, m_w_ff2, m_norm_final_g, v_norm_mix_g, v_w_in, v_gm_v_norm_g, v_w_spatial, v_b_spatial, v_head_norm_g, v_w_out, v_norm_cross_g, v_norm_mem_g, v_w_cq, v_w_ckv, v_w_co, v_norm_ffn_g, v_w_ff1, v_w_ff2, v_norm_final_g):
    args = dict(locals())
    d = D_MODEL
    nb, s, _ = x.shape
    c_idx = lax.axis_index("c").astype(jnp.int32).reshape(1)
    q_idx = (2 * lax.axis_index("x") + lax.axis_index("y")).astype(jnp.int32).reshape(1)
    rest = BIG[1:]

    shards = {k: args[k][0].astype(BF) for k in BIG}
    big = _full_weights({"w_in": _run_exchange(_GatherExchange([shards["w_in"]]), "all_gather_w_in")[0]})
    gather_rest = _GatherExchange([shards[k] for k in rest])

    reduce = _GradReduce(c_idx)
    loss, grad_x, gsmall, _, parts, last = _local_step(
        x.reshape(nb * s, d), mem.reshape(nb * N_MEM, d), loss_target.reshape(nb * s, d), _small_params(args), big,
        nb, s, gather_rest=gather_rest, reduce=reduce)

    shapes = [args[k].shape for k in SMALL]
    n_small = sum(math.prod(sh) for sh in shapes)
    rows = -(-(n_small + 1) // 1024) * 8
    reduced, (parts["w_in"],) = _small_all_reduce(_pack([gsmall[k] for k in SMALL] + [loss], rows), ride=last)
    halves = [_sum4(reduce.sums[k], parts[k], q_idx, "sum4_" + k) for k in BIG]
    both = _half_exchange(halves)

    out = {"grad_x": grad_x.reshape(nb, s, d)}
    for k, g2 in zip(BIG, both):
        shp = args[k].shape
        g = g2.reshape(shp[1], shp[2])
        dl, nm, nv = _adamw(g, args[k][0], args["m_" + k][0], args["v_" + k][0], "adamw_" + k)
        out["grad_" + k], out["delta_" + k], out["new_m_" + k], out["new_v_" + k] = (
            a.reshape(shp) for a in (g, dl, nm, nv))

    dl, nm, nv = _adamw(reduced, _pack([args[k] for k in SMALL], rows), _pack([args["m_" + k] for k in SMALL], rows),
                        _pack([args["v_" + k] for k in SMALL], rows), "adamw_small")
    for name, arr in (("grad_", reduced), ("delta_", dl), ("new_m_", nm), ("new_v_", nv)):
        for k, a in zip(SMALL, _unpack(arr, shapes)):
            out[name + k] = a
    out["loss"] = reduced.reshape(-1)[n_small]

    names = ["norm_mix_g", "w_in", "gm_v_norm_g", "w_spatial", "b_spatial", "head_norm_g", "w_out", "norm_cross_g",
             "norm_mem_g", "w_cq", "w_ckv", "w_co", "norm_ffn_g", "w_ff1", "w_ff2", "norm_final_g"]
    return (out["loss"], out["grad_x"], *[out["grad_" + k] for k in names], *[out["delta_" + k] for k in names],
            *[out["new_m_" + k] for k in names], *[out["new_v_" + k] for k in names])
```

```python
import functools
import math

import jax
import jax.numpy as jnp
from jax import lax
from jax.experimental import pallas as pl
from jax.experimental.pallas import tpu as pltpu

F32 = jnp.float32
BF = jnp.bfloat16

EPS = 1e-6
D_MODEL = 1024
CHUNK = 128
GM_GROUPS = 4
GM_WIDTH = 512
SB_WIDTH = 512
HEAD_LANES = 64
SB_SCALE = 0.125
SB_SKIP = -104.0
X_HEADS = 4
X_HEAD_DIM = 256
N_MEM = 256
D_FF = 4096
IN_COLS = 2560
N_CHIPS = 4
N_DEV = 8

ADAM_LR = 0.001
ADAM_B1 = 0.9
ADAM_B2 = 0.999
ADAM_EPS = 1e-08
ADAM_WD = 0.01
ADAM_STEP = 10

V7X_VMEM_BYTES = 64 * 1024 * 1024
MESH = pl.DeviceIdType.MESH
ANY = pl.BlockSpec(memory_space=pl.ANY)

GELU_C = math.sqrt(2.0 / math.pi)
GELU_A = 0.044715


def _params(vmem_mb, sem=None):
    assert vmem_mb * 1024 * 1024 <= V7X_VMEM_BYTES
    return pltpu.CompilerParams(vmem_limit_bytes=vmem_mb * 1024 * 1024, dimension_semantics=sem)


def _in_hbm(*arrays):
    return tuple(pltpu.with_memory_space_constraint(a, pltpu.HBM) for a in arrays)


def _dot(a, b):
    return jnp.dot(a, b, preferred_element_type=F32)


def _dot_bt(a, b):
    return lax.dot_general(a, b, (((1,), (1,)), ((), ())), preferred_element_type=F32)


def _dot_at(a, b):
    return lax.dot_general(a, b, (((0,), (0,)), ((), ())), preferred_element_type=F32)


def _gelu(x):
    t = jnp.tanh(GELU_C * (x + GELU_A * x * x * x))
    return 0.5 * x * (1.0 + t)


def _gelu_and_grad(x):
    x2 = x * x
    t = jnp.tanh(GELU_C * (x + GELU_A * x2 * x))
    h = 0.5 * (1.0 + t)
    return x * h, h + 0.5 * x * (1.0 - t * t) * (GELU_C * (1.0 + 3.0 * GELU_A * x2))


def _rs(x):
    return lax.rsqrt(jnp.mean(x * x, axis=-1, keepdims=True) + EPS)


def _rms_bwd(dxn, xhat, r, g):
    dxh = dxn * g
    dx = r * (dxh - xhat * jnp.mean(dxh * xhat, axis=-1, keepdims=True))
    return dx, dxn * xhat


def _norm_matmul(x, g, w, tm, name):
    t, d = x.shape
    n = w.shape[1]
    tm = min(tm, t)

    def body(x_ref, g_ref, w_ref, out_ref, xn_ref):
        xv = x_ref[...]
        xn = (xv * _rs(xv) * g_ref[...]).astype(BF)
        xn_ref[...] = xn
        out_ref[...] = _dot(xn, w_ref[...]).astype(out_ref.dtype)

    return pl.pallas_call(
        body, name=name, grid=(t // tm,),
        in_specs=[pl.BlockSpec((tm, d), lambda i: (i, 0)), pl.BlockSpec((1, d), lambda i: (0, 0)),
                  pl.BlockSpec((d, n), lambda i: (0, 0))],
        out_specs=[pl.BlockSpec((tm, n), lambda i: (i, 0)), pl.BlockSpec((tm, d), lambda i: (i, 0))],
        out_shape=[jax.ShapeDtypeStruct((t, n), BF), jax.ShapeDtypeStruct((t, d), BF)],
        compiler_params=_params(48, ("arbitrary",)),
    )(x, g, w)


def _wgrad(a, g, tn, tk, name, square_a=False, col_shards=1):
    t, m = a.shape
    n = g.shape[1]
    tk = min(tk, t)
    tm = min(m, 1024)
    ns = n // col_shards
    assert ns % tn == 0 and m % tm == 0
    per = ns // tn
    nk = t // tk

    def body(a_ref, g_ref, o_ref):
        k = pl.program_id(2)

        @pl.when(k == 0)
        def _():
            o_ref[...] = jnp.zeros_like(o_ref)

        av = a_ref[...]
        if square_a:
            af = av.astype(F32)
            av = af * af
        o_ref[...] += _dot_at(av.astype(BF), g_ref[...].astype(BF))

    return pl.pallas_call(
        body, name=name, grid=(m // tm, n // tn, nk),
        in_specs=[pl.BlockSpec((tk, tm), lambda i, j, k: (k, i)), pl.BlockSpec((tk, tn), lambda i, j, k: (k, j))],
        out_specs=pl.BlockSpec((None, tm, tn), lambda i, j, k: (j // per, i, j % per)),
        out_shape=jax.ShapeDtypeStruct((col_shards, m, ns), F32),
        compiler_params=_params(48, ("arbitrary", "arbitrary", "arbitrary")),
    )(a, g)


def _matmul_bt(a, w, tm, name):
    t, n = a.shape
    k = w.shape[0]
    tm = min(tm, t)

    def body(a_ref, w_ref, o_ref):
        o_ref[...] = _dot_bt(a_ref[...].astype(BF), w_ref[...]).astype(o_ref.dtype)

    return pl.pallas_call(
        body, name=name, grid=(t // tm,),
        in_specs=[pl.BlockSpec((tm, n), lambda i: (i, 0)), pl.BlockSpec((k, n), lambda i: (0, 0))],
        out_specs=pl.BlockSpec((tm, k), lambda i: (i, 0)),
        out_shape=jax.ShapeDtypeStruct((t, k), BF),
        compiler_params=_params(32, ("arbitrary",)),
    )(a, w)


def _gmlp_fwd(proj, gg, wt, bb, hg, tm):
    t = proj.shape[0]
    tm = min(tm, t)

    def body(u_ref, v_ref, gg_ref, wt_ref, bb_ref, hg_ref, out_ref):
        for cc in range(tm // CHUNK):
            rows = slice(cc * CHUNK, (cc + 1) * CHUNK)
            for g in range(GM_GROUPS):
                cols = slice(g * 128, (g + 1) * 128)
                u = _gelu(u_ref[rows, cols].astype(F32))
                gv = _gelu(v_ref[rows, cols].astype(F32))
                vn = gv * _rs(gv) * gg_ref[:, cols]
                mixed = _dot(wt_ref[g], vn.astype(BF)) + bb_ref[g]
                a = u * mixed
                out_ref[rows, cols] = (a * _rs(a) * hg_ref[:, cols]).astype(BF)

    return pl.pallas_call(
        body, name="gmlp_fwd", grid=(t // tm,),
        in_specs=[pl.BlockSpec((tm, 512), lambda i: (i, 0)), pl.BlockSpec((tm, 512), lambda i: (i, 1)),
                  pl.BlockSpec((1, 512), lambda i: (0, 0)), pl.BlockSpec((4, 128, 128), lambda i: (0, 0, 0)),
                  pl.BlockSpec((4, 128, 128), lambda i: (0, 0, 0)), pl.BlockSpec((1, 512), lambda i: (0, 0))],
        out_specs=pl.BlockSpec((tm, 512), lambda i: (i, 0)),
        out_shape=jax.ShapeDtypeStruct((t, 1024), BF),
        compiler_params=_params(32, ("arbitrary",)),
    )(proj, proj, gg, wt, bb, hg)


def _gmlp_bwd(proj, dmerged, gg, wt, wtt, bb, hg, tm, ride=None):
    t = proj.shape[0]
    tm = min(tm, t)
    nsteps = t // tm

    def body(u_ref, v_ref, dm_ref, gg_ref, wt_ref, wtt_ref, bb_ref, hg_ref,
             dp_ref, dw_ref, db_ref, dgg_ref, dhg_ref):
        i = pl.program_id(0)

        @pl.when(i == 0)
        def _():
            dw_ref[...] = jnp.zeros_like(dw_ref)
            db_ref[...] = jnp.zeros_like(db_ref)
            dgg_ref[...] = jnp.zeros_like(dgg_ref)
            dhg_ref[...] = jnp.zeros_like(dhg_ref)

        for cc in range(tm // CHUNK):
            rows = slice(cc * CHUNK, (cc + 1) * CHUNK)
            for g in range(GM_GROUPS):
                cols = slice(g * 128, (g + 1) * 128)
                up = u_ref[rows, cols].astype(F32)
                gp = v_ref[rows, cols].astype(F32)
                u, u_grad = _gelu_and_grad(up)
                gv, gv_grad = _gelu_and_grad(gp)
                rv = _rs(gv)
                gvh = gv * rv
                ggv = gg_ref[:, cols]
                vnb = (gvh * ggv).astype(BF)
                mixed = _dot(wt_ref[g], vnb) + bb_ref[g]
                a = u * mixed
                ra = _rs(a)
                ah = a * ra
                dm = dm_ref[rows, cols].astype(F32)
                dhg_ref[:, cols] += jnp.sum(dm * ah, axis=0, keepdims=True)
                dah = dm * hg_ref[:, cols]
                da = ra * (dah - ah * jnp.mean(dah * ah, axis=-1, keepdims=True))
                du = da * mixed
                dmix = da * u
                db_ref[g] += dmix
                dmb = dmix.astype(BF)
                dw_ref[g] += _dot_bt(dmb, vnb)
                dvn = _dot(wtt_ref[g], dmb)
                dgg_ref[:, cols] += jnp.sum(dvn * gvh, axis=0, keepdims=True)
                dgh = dvn * ggv
                dgv = rv * (dgh - gvh * jnp.mean(dgh * gvh, axis=-1, keepdims=True))
                dp_ref[rows, cols] = (du * u_grad).astype(BF)
                dp_ref[rows, 512 + g * 128:512 + (g + 1) * 128] = (dgv * gv_grad).astype(BF)

        @pl.when(i == nsteps - 1)
        def _():
            r = lax.broadcasted_iota(jnp.int32, (CHUNK, CHUNK), 0)
            c = lax.broadcasted_iota(jnp.int32, (CHUNK, CHUNK), 1)
            for g in range(GM_GROUPS):
                dw_ref[g] = jnp.where(c <= r, dw_ref[g], 0.0)
                db_ref[g] = jnp.broadcast_to(jnp.sum(db_ref[g], axis=-1, keepdims=True), (CHUNK, CHUNK))

    small = lambda shape: pl.BlockSpec(shape, lambda i: (0,) * len(shape))
    res, rode = _ride_call(
        body, "gmlp_bwd", (nsteps,),
        in_specs=[pl.BlockSpec((tm, 512), lambda i: (i, 0)), pl.BlockSpec((tm, 512), lambda i: (i, 1)),
                  pl.BlockSpec((tm, 512), lambda i: (i, 0)), small((1, 512)), small((4, 128, 128)),
                  small((4, 128, 128)), small((4, 128, 128)), small((1, 512))],
        out_specs=[pl.BlockSpec((tm, 1024), lambda i: (i, 0)), small((4, 128, 128)), small((4, 128, 128)),
                   small((1, 512)), small((1, 512))],
        out_shape=[jax.ShapeDtypeStruct((t, IN_COLS), BF), jax.ShapeDtypeStruct((4, 128, 128), F32),
                   jax.ShapeDtypeStruct((4, 128, 128), F32), jax.ShapeDtypeStruct((1, 512), F32),
                   jax.ShapeDtypeStruct((1, 512), F32)],
        scratch_shapes=[], operands=(proj, proj, dmerged, gg, wt, wtt, bb, hg), vmem_mb=32, ride=ride)
    return (*res, rode)


def _other_chips(x, y):
    return ((1 - x, y), (x, 1 - y), (1 - x, 1 - y))


class _GatherExchange:
    def __init__(self, shards):
        n = len(shards)
        self.n = n
        self.in_arrays = list(shards)
        self.out_shape = [jax.ShapeDtypeStruct((N_CHIPS,) + a.shape, a.dtype) for a in shards]
        self.half_rows = [a.shape[0] // 2 for a in shards]
        sems = lambda k: pltpu.SemaphoreType.DMA((k,))
        self.scratch_shapes = [pltpu.VMEM(a.shape, a.dtype) for a in shards] + [
            sems(3 * n), sems(3 * n), sems(3 * n), sems(3 * n), sems(n), sems(n)]

    def _copies(self, ins, outs, scr):
        n = self.n
        stages, (ici_send, ici_recv, d2d_send, d2d_recv, ld_sems, st_sems) = scr[:n], scr[n:]
        x, y, c = lax.axis_index("x"), lax.axis_index("y"), lax.axis_index("c")
        q = 2 * x + y
        loads = [pltpu.make_async_copy(ins[w], stages[w], ld_sems.at[w]) for w in range(n)]
        stores = [pltpu.make_async_copy(stages[w], outs[w].at[q], st_sems.at[w]) for w in range(n)]
        ici, d2d = [], []
        for w in range(n):
            half = pl.ds(c * self.half_rows[w], self.half_rows[w])
            for k, (px, py) in enumerate(_other_chips(x, y)):
                ici.append(pltpu.make_async_remote_copy(
                    src_ref=ins[w].at[half], dst_ref=outs[w].at[q, half], send_sem=ici_send.at[3 * w + k],
                    recv_sem=ici_recv.at[3 * w + k], device_id=(px, py, c), device_id_type=MESH))
                landed = outs[w].at[2 * px + py, half]
                d2d.append(pltpu.make_async_remote_copy(
                    src_ref=landed, dst_ref=landed, send_sem=d2d_send.at[3 * w + k],
                    recv_sem=d2d_recv.at[3 * w + k], device_id=(x, y, 1 - c), device_id_type=MESH))
        return loads, stores, ici, d2d

    def start(self, ins, outs, scr):
        loads, stores, ici, _ = self._copies(ins, outs, scr)
        for cp in loads + ici:
            cp.start()
        for ld, st in zip(loads, stores):
            ld.wait()
            st.start()

    def relay(self, ins, outs, scr):
        _, _, ici, d2d = self._copies(ins, outs, scr)
        for got, fwd in zip(ici, d2d):
            got.wait_recv()
            fwd.start()

    def finish(self, ins, outs, scr):
        _, stores, ici, d2d = self._copies(ins, outs, scr)
        for cp in ici:
            cp.wait_send()
        for cp in d2d + stores:
            cp.wait()


class _SiblingExchange:
    def __init__(self, slabs):
        n = len(slabs)
        self.n = n
        self.in_arrays = list(slabs)
        self.out_shape = [jax.ShapeDtypeStruct((N_CHIPS,) + a.shape[1:], a.dtype) for a in slabs]
        self.scratch_shapes = [pltpu.SemaphoreType.DMA((4 * n,)), pltpu.SemaphoreType.DMA((4 * n,))]

    def _copies(self, ins, outs, scr):
        send_sems, recv_sems = scr
        x, y, c = lax.axis_index("x"), lax.axis_index("y"), lax.axis_index("c")
        return [pltpu.make_async_remote_copy(
            src_ref=ins[w].at[2 * p + (1 - c)], dst_ref=outs[w].at[p], send_sem=send_sems.at[4 * w + p],
            recv_sem=recv_sems.at[4 * w + p], device_id=(x, y, 1 - c), device_id_type=MESH)
            for w in range(self.n) for p in range(N_CHIPS)]

    def start(self, ins, outs, scr):
        for cp in self._copies(ins, outs, scr):
            cp.start()

    def finish(self, ins, outs, scr):
        for cp in self._copies(ins, outs, scr):
            cp.wait()


class _ChipExchange:
    def __init__(self, sums):
        n = len(sums)
        self.n = n
        self.in_arrays = list(sums)
        self.out_shape = [jax.ShapeDtypeStruct(a.shape, a.dtype) for a in sums]
        self.scratch_shapes = [pltpu.SemaphoreType.DMA((3 * n,)), pltpu.SemaphoreType.DMA((3 * n,))]

    def _copies(self, ins, outs, scr):
        send_sems, recv_sems = scr
        x, y, c = lax.axis_index("x"), lax.axis_index("y"), lax.axis_index("c")
        q = 2 * x + y
        return [pltpu.make_async_remote_copy(
            src_ref=ins[w].at[2 * px + py], dst_ref=outs[w].at[q], send_sem=send_sems.at[3 * w + k],
            recv_sem=recv_sems.at[3 * w + k], device_id=(px, py, c), device_id_type=MESH)
            for w in range(self.n) for k, (px, py) in enumerate(_other_chips(x, y))]

    def start(self, ins, outs, scr):
        for cp in self._copies(ins, outs, scr):
            cp.start()

    def finish(self, ins, outs, scr):
        for cp in self._copies(ins, outs, scr):
            cp.wait()


class _NoExchange:
    in_arrays, out_shape, scratch_shapes = (), (), ()

    def start(self, ins, outs, scr):
        pass

    def finish(self, ins, outs, scr):
        pass


def _run_exchange(ex, name):
    n_in, n_out = len(ex.in_arrays), len(ex.out_shape)

    def body(*refs):
        ins, outs, scr = refs[:n_in], refs[n_in:n_in + n_out], refs[n_in + n_out:]
        ex.start(ins, outs, scr)
        if hasattr(ex, "relay"):
            ex.relay(ins, outs, scr)
        ex.finish(ins, outs, scr)

    return pl.pallas_call(
        body, name=name, in_specs=[ANY] * n_in, out_specs=[ANY] * n_out, out_shape=ex.out_shape,
        scratch_shapes=ex.scratch_shapes, compiler_params=_params(24),
    )(*ex.in_arrays)


def _ride_call(body, name, grid, in_specs, out_specs, out_shape, scratch_shapes, operands, vmem_mb, ride=None,
               aliases=None):
    ride = ride or _NoExchange()
    ni, no, ns = len(in_specs), len(out_specs), len(scratch_shapes)
    ri, ro = len(ride.in_arrays), len(ride.out_shape)
    total = math.prod(grid)

    def wrapped(*refs):
        ins, rins = refs[:ni], refs[ni:ni + ri]
        outs, routs = refs[ni + ri:ni + ri + no], refs[ni + ri + no:ni + ri + no + ro]
        scr, rscr = refs[ni + ri + no + ro:ni + ri + no + ro + ns], refs[ni + ri + no + ro + ns:]
        step = pl.program_id(0)
        for ax in range(1, len(grid)):
            step = step * grid[ax] + pl.program_id(ax)

        @pl.when(step == 0)
        def _():
            ride.start(rins, routs, rscr)

        if hasattr(ride, "relay"):
            @pl.when(step == (3 * total) // 4)
            def _():
                ride.relay(rins, routs, rscr)

        body(*ins, *outs, *scr)

        @pl.when(step == total - 1)
        def _():
            ride.finish(rins, routs, rscr)

    res = pl.pallas_call(
        wrapped, name=name, grid=grid, in_specs=list(in_specs) + [ANY] * ri, out_specs=list(out_specs) + [ANY] * ro,
        out_shape=list(out_shape) + list(ride.out_shape),
        scratch_shapes=list(scratch_shapes) + list(ride.scratch_shapes), input_output_aliases=aliases or {},
        compiler_params=_params(vmem_mb, ("arbitrary",) * len(grid)),
    )(*operands, *ride.in_arrays)
    return res[:no], res[no:]


def _neg_log_sig(z):
    n = jnp.maximum(z, 0.0) + jnp.log(1.0 + jnp.exp(-jnp.abs(z)))
    return n, z - n


def _running_sums(n, tri2):
    hi = n.astype(BF)
    lo = (n - hi.astype(F32)).astype(BF)
    return _dot(jnp.concatenate([hi, lo], axis=1), tri2)


def _head_sums(x, h0):
    s0 = jnp.sum(jnp.where(h0, x, 0.0), axis=-1, keepdims=True)
    s1 = jnp.sum(jnp.where(h0, 0.0, x), axis=-1, keepdims=True)
    return jnp.where(h0, s0, s1)


def _sb_setup(q_ref, tq):
    lane = lax.broadcasted_iota(jnp.int32, (tq, 128), 1)
    h0 = lane < HEAD_LANES
    qs = q_ref[...] * SB_SCALE
    zero = jnp.zeros_like(qs)
    qst = jnp.concatenate([jnp.where(h0, qs, zero), jnp.where(h0, zero, qs)], axis=0)
    r = lax.broadcasted_iota(jnp.int32, (2 * tq, tq), 0)
    c = lax.broadcasted_iota(jnp.int32, (2 * tq, tq), 1)
    causal = c < jnp.where(r >= tq, r - tq, r)
    return h0, qst, causal


def _tri(tq, op):
    return op(lax.broadcasted_iota(jnp.int32, (tq, tq), 0), lax.broadcasted_iota(jnp.int32, (tq, tq), 1)).astype(BF)


def _sb_fwd(proj, merged, hg, nb, s, tq, ride=None):
    t = nb * s
    tq = min(tq, s)
    nq = s // tq

    def body(q_ref, k_ref, v_ref, hg_ref, merged_ref, o_ref, tot_ref, mb_ref, nblk_ref, acc, cr):
        del merged_ref
        i = pl.program_id(2)
        h0, qst, causal = _sb_setup(q_ref, tq)
        tri_gt = _tri(tq, lambda r, c: r > c)
        tri_gt = jnp.concatenate([tri_gt, tri_gt], axis=0)

        def block(j, masked, c_in):
            start = pl.multiple_of(j * tq, tq)
            kj = k_ref[pl.ds(start, tq), :]
            vj = v_ref[pl.ds(start, tq), :]
            n, l = _neg_log_sig(_dot_bt(qst, kj))
            if masked:
                n = jnp.where(causal, n, 0.0)
            a = jnp.exp(l - (_running_sums(n, tri_gt) + c_in))
            if masked:
                a = jnp.where(causal, a, 0.0)
            return _dot(a.astype(BF), vj), c_in + jnp.sum(n, axis=-1, keepdims=True)

        @pl.when(i == 0)
        def _():
            acc[...], cr[...] = block(0, True, jnp.zeros((2 * tq, 1), F32))

        @pl.when(i > 0)
        def _():
            p_diag, c_diag = block(i, True, jnp.zeros((2 * tq, 1), F32))
            p_prev, c_prev = block(i - 1, False, c_diag)
            acc[...] = p_diag + p_prev
            cr[...] = c_prev

        def cond(carry):
            return jnp.logical_and(carry[0] < i, carry[1] < -SB_SKIP)

        def step(carry):
            p, c_new = block(i - 1 - carry[0], False, cr[...])
            acc[...] += p
            cr[...] = c_new
            return carry[0] + 1, jnp.min(c_new)

        walked, _ = lax.while_loop(cond, step, (jnp.minimum(i, 1), jnp.min(cr[...])))

        o = jnp.where(h0, acc[0:tq, :], acc[tq:2 * tq, :])
        o_ref[...] = o
        tot_ref[...] = jnp.where(h0, cr[0:tq, :], cr[tq:2 * tq, :])
        ro = lax.rsqrt(_head_sums(o * o, h0) * (1.0 / HEAD_LANES) + EPS)
        mb_ref[...] = (o * ro * hg_ref[...]).astype(BF)
        nblk_ref[...] = jnp.full((8, 128), walked.astype(F32))

    blk = lambda col0: pl.BlockSpec((tq, 128), lambda b, hp, i: (b * nq + i, col0 + hp))
    seq = lambda col0: pl.BlockSpec((s, 128), lambda b, hp, i: (b, col0 + hp))
    (o, tot, mb, nblk), rode = _ride_call(
        body, "sb_fwd", (nb, 4, nq),
        in_specs=[blk(8), seq(12), seq(16), pl.BlockSpec((1, 128), lambda b, hp, i: (0, 4 + hp)), ANY],
        out_specs=[blk(0), blk(0), blk(4), pl.BlockSpec((None, None, 8, 128), lambda b, hp, i: (b, hp, i, 0))],
        out_shape=[jax.ShapeDtypeStruct((t, 512), F32), jax.ShapeDtypeStruct((t, 512), F32),
                   jax.ShapeDtypeStruct((t, 1024), BF), jax.ShapeDtypeStruct((nb, 4, nq * 8, 128), F32)],
        scratch_shapes=[pltpu.VMEM((2 * tq, 128), F32), pltpu.VMEM((2 * tq, 1), F32)],
        operands=(proj, proj, proj, hg, merged), vmem_mb=40, ride=ride, aliases={4: 2})
    return o, tot, mb, nblk, rode


def _sb_bwd(proj, o_sb, tot, nblk, dmerged, dproj, hg, nb, s, tq, ride=None):
    t = nb * s
    tq = min(tq, s)
    nq = s // tq

    def body(q_ref, k_ref, v_ref, o_ref, tot_ref, nblk_ref, dm_ref, hg_ref, dproj_ref,
             dq_ref, dk_ref, dv_ref, dhg_ref, dk_acc, dv_acc, dq_acc, cm, cg):
        del dproj_ref
        i = pl.program_id(2)
        h0, qst, causal = _sb_setup(q_ref, tq)
        tri_le = _tri(tq, lambda r, c: r <= c)
        tri_le = jnp.concatenate([tri_le, tri_le], axis=0)
        tri_lt = _tri(tq, lambda r, c: r < c)

        @pl.when(i == 0)
        def _():
            dk_acc[...] = jnp.zeros_like(dk_acc)
            dv_acc[...] = jnp.zeros_like(dv_acc)
            dhg_ref[...] = jnp.zeros_like(dhg_ref)

        for ref in (dq_acc, cm, cg):
            ref[...] = jnp.zeros_like(ref)

        o = o_ref[...]
        ro = lax.rsqrt(_head_sums(o * o, h0) * (1.0 / HEAD_LANES) + EPS)
        oh = o * ro
        dm = dm_ref[...].astype(F32)
        dhg_ref[...] += jnp.sum(dm * oh, axis=0, keepdims=True)
        doh = dm * hg_ref[...]
        do = (ro * (doh - oh * (_head_sums(doh * oh, h0) * (1.0 / HEAD_LANES)))).astype(BF)
        zb = jnp.zeros_like(do)
        dost = jnp.concatenate([jnp.where(h0, do, zb), jnp.where(h0, zb, do)], axis=0)
        tots = jnp.concatenate([tot_ref[:, 0:1], tot_ref[:, HEAD_LANES:HEAD_LANES + 1]], axis=0)

        def block(j, masked, cm_in, cg_in):
            start = pl.multiple_of(j * tq, tq)
            kj = k_ref[pl.ds(start, tq), :]
            vj = v_ref[pl.ds(start, tq), :]
            n, l = _neg_log_sig(_dot_bt(qst, kj))
            if masked:
                n = jnp.where(causal, n, 0.0)
            a = jnp.exp(l - (tots - cm_in - _running_sums(n, tri_le)))
            if masked:
                a = jnp.where(causal, a, 0.0)
            gm = a * _dot_bt(dost, vj)
            pp = cg_in + _dot(gm.astype(BF), tri_lt)
            dz = gm - jnp.exp(l) * (gm + pp)
            if masked:
                dz = jnp.where(causal, dz, 0.0)
            dzb = dz.astype(BF)
            dk_acc[pl.ds(start, tq), :] += _dot_at(dzb, qst)
            dv_acc[pl.ds(start, tq), :] += _dot_at(a.astype(BF), dost)
            return (_dot(dzb, kj), cm_in + jnp.sum(n, axis=-1, keepdims=True),
                    cg_in + jnp.sum(gm, axis=-1, keepdims=True))

        def step(j, carry):
            dq, cm[...], cg[...] = block(j, False, cm[...], cg[...])
            dq_acc[...] += dq
            return carry

        walked = jnp.clip(jnp.max(nblk_ref[...]).astype(jnp.int32), jnp.minimum(i, 1), i)
        lax.fori_loop(i - walked, i - 1, step, 0)

        @pl.when(i == 0)
        def _():
            dq_acc[...] = block(0, True, cm[...], cg[...])[0]

        @pl.when(i > 0)
        def _():
            dq_prev, cm_prev, cg_prev = block(i - 1, False, cm[...], cg[...])
            dq_acc[...] += dq_prev + block(i, True, cm_prev, cg_prev)[0]

        dq_ref[...] = (jnp.where(h0, dq_acc[0:tq, :], dq_acc[tq:2 * tq, :]) * SB_SCALE).astype(BF)

        @pl.when(i == nq - 1)
        def _():
            dk_ref[...] = dk_acc[...].astype(BF)
            dv_ref[...] = dv_acc[...].astype(BF)

    blk = lambda col0: pl.BlockSpec((tq, 128), lambda b, hp, i: (b * nq + i, col0 + hp))
    seq = lambda col0: pl.BlockSpec((s, 128), lambda b, hp, i: (b, col0 + hp))
    (dq, dk, dv, dhg), rode = _ride_call(
        body, "sb_bwd", (nb, 4, nq),
        in_specs=[blk(8), seq(12), seq(16), blk(0), blk(0),
                  pl.BlockSpec((None, None, 8, 128), lambda b, hp, i: (b, hp, i, 0)), blk(4),
                  pl.BlockSpec((1, 128), lambda b, hp, i: (0, 4 + hp)), ANY],
        out_specs=[blk(8), seq(0), seq(0), pl.BlockSpec((None, 1, 128), lambda b, hp, i: (b, 0, hp))],
        out_shape=[jax.ShapeDtypeStruct((t, IN_COLS), BF), jax.ShapeDtypeStruct((t, 512), BF),
                   jax.ShapeDtypeStruct((t, 512), BF), jax.ShapeDtypeStruct((nb, 1, 512), F32)],
        scratch_shapes=[pltpu.VMEM((s, 128), F32), pltpu.VMEM((s, 128), F32), pltpu.VMEM((2 * tq, 128), F32),
                        pltpu.VMEM((2 * tq, 1), F32), pltpu.VMEM((2 * tq, 1), F32)],
        operands=(proj, proj, proj, o_sb, tot, nblk, dmerged, hg, dproj), vmem_mb=40, ride=ride, aliases={8: 0})
    return dq, dk, dv, dhg, rode


def _place(buf, piece, col_block, name):
    t, w = piece.shape
    tm = min(t, 1024)

    def body(piece_ref, buf_ref, out_ref):
        del buf_ref
        out_ref[...] = piece_ref[...]

    return pl.pallas_call(
        body, name=name, grid=(t // tm,),
        in_specs=[pl.BlockSpec((tm, w), lambda i: (i, 0)), ANY],
        out_specs=pl.BlockSpec((tm, w), lambda i: (i, col_block)),
        out_shape=jax.ShapeDtypeStruct(buf.shape, buf.dtype), input_output_aliases={1: 0},
        compiler_params=_params(16, ("arbitrary",)),
    )(piece, buf)


def _softmax_rows(sc):
    e = jnp.exp(sc - jnp.max(sc, axis=-1, keepdims=True))
    return e / jnp.sum(e, axis=-1, keepdims=True)


def _mix_cross_fwd(x, merged, w_out, gc, w_cq, kv, w_co, s, tm):
    t, d = x.shape
    tm = min(tm, s)
    per = s // tm
    inv = 1.0 / math.sqrt(X_HEAD_DIM)

    def body(x_ref, m_ref, wo_ref, gc_ref, wq_ref, kv_ref, wc_ref, h1_ref, h2_ref, hn_ref, oc_ref):
        h1 = x_ref[...] + _dot(m_ref[...], wo_ref[...])
        h1_ref[...] = h1
        hn = (h1 * _rs(h1) * gc_ref[...]).astype(BF)
        hn_ref[...] = hn
        qc = _dot(hn, wq_ref[...]).astype(BF)
        for h in range(X_HEADS):
            cols = slice(h * X_HEAD_DIM, (h + 1) * X_HEAD_DIM)
            kh = kv_ref[:, h * X_HEAD_DIM:(h + 1) * X_HEAD_DIM]
            vh = kv_ref[:, d + h * X_HEAD_DIM:d + (h + 1) * X_HEAD_DIM]
            p = _softmax_rows(_dot_bt(qc[:, cols], kh) * inv)
            oc_ref[:, cols] = _dot(p.astype(BF), vh).astype(BF)
        h2_ref[...] = h1 + _dot(oc_ref[...], wc_ref[...])

    row = lambda width: pl.BlockSpec((tm, width), lambda i: (i, 0))
    full = lambda a, b: pl.BlockSpec((a, b), lambda i: (0, 0))
    return pl.pallas_call(
        body, name="mix_cross_fwd", grid=(t // tm,),
        in_specs=[row(d), row(d), full(d, d), full(1, d), full(d, d),
                  pl.BlockSpec((N_MEM, 2 * d), lambda i: (i // per, 0)), full(d, d)],
        out_specs=[row(d), row(d), row(d), row(d)],
        out_shape=[jax.ShapeDtypeStruct((t, d), F32), jax.ShapeDtypeStruct((t, d), F32),
                   jax.ShapeDtypeStruct((t, d), BF), jax.ShapeDtypeStruct((t, d), BF)],
        compiler_params=_params(48, ("arbitrary",)),
    )(x, merged, w_out, gc, w_cq, kv, w_co)


def _cross_bwd(dh2, h1, gc, w_cq, kv, w_co, s, tm):
    t, d = dh2.shape
    tm = min(tm, s)
    per = s // tm
    nb = t // s
    inv = 1.0 / math.sqrt(X_HEAD_DIM)

    def body(dh2_ref, h1_ref, gc_ref, wq_ref, kv_ref, wc_ref, dh1_ref, dqc_ref, dkv_ref, dgc_ref):
        i = pl.program_id(0)

        @pl.when(i == 0)
        def _():
            dgc_ref[...] = jnp.zeros_like(dgc_ref)

        @pl.when(i % per == 0)
        def _():
            dkv_ref[...] = jnp.zeros_like(dkv_ref)

        dh2 = dh2_ref[...]
        h1 = h1_ref[...]
        r = _rs(h1)
        h1h = h1 * r
        gcv = gc_ref[...]
        hn = (h1h * gcv).astype(BF)
        qc = _dot(hn, wq_ref[...]).astype(BF)
        do = _dot_bt(dh2.astype(BF), wc_ref[...]).astype(BF)
        for h in range(X_HEADS):
            cols = slice(h * X_HEAD_DIM, (h + 1) * X_HEAD_DIM)
            vcols = slice(d + h * X_HEAD_DIM, d + (h + 1) * X_HEAD_DIM)
            kh = kv_ref[:, cols]
            vh = kv_ref[:, vcols]
            p = _softmax_rows(_dot_bt(qc[:, cols], kh) * inv)
            dp = _dot_bt(do[:, cols], vh)
            ds = (p * (dp - jnp.sum(dp * p, axis=-1, keepdims=True)) * inv).astype(BF)
            dqc_ref[:, cols] = _dot(ds, kh).astype(BF)
            dkv_ref[:, cols] += _dot_at(ds, qc[:, cols])
            dkv_ref[:, vcols] += _dot_at(p.astype(BF), do[:, cols])
        dhn = _dot_bt(dqc_ref[...], wq_ref[...])
        dx, dg = _rms_bwd(dhn, h1h, r, gcv)
        dh1_ref[...] = dh2 + dx
        dgc_ref[...] += jnp.sum(dg, axis=0, keepdims=True)

    row = lambda width: pl.BlockSpec((tm, width), lambda i: (i, 0))
    full = lambda a, b: pl.BlockSpec((a, b), lambda i: (0, 0))
    kvspec = pl.BlockSpec((N_MEM, 2 * d), lambda i: (i // per, 0))
    return pl.pallas_call(
        body, name="cross_bwd", grid=(t // tm,),
        in_specs=[row(d), row(d), full(1, d), full(d, d), kvspec, full(d, d)],
        out_specs=[row(d), row(d), kvspec, full(1, d)],
        out_shape=[jax.ShapeDtypeStruct((t, d), F32), jax.ShapeDtypeStruct((t, d), BF),
                   jax.ShapeDtypeStruct((nb * N_MEM, 2 * d), F32), jax.ShapeDtypeStruct((1, d), F32)],
        compiler_params=_params(48, ("arbitrary",)),
    )(dh2, h1, gc, w_cq, kv, w_co)


def _mem_bwd(mem, gm, dkv, w_ckv, tm):
    t, d = mem.shape
    tm = min(tm, t)

    def body(mem_ref, dkv_ref, w_ref, dg_ref):
        @pl.when(pl.program_id(0) == 0)
        def _():
            dg_ref[...] = jnp.zeros_like(dg_ref)

        mv = mem_ref[...]
        dmn = _dot_bt(dkv_ref[...].astype(BF), w_ref[...])
        dg_ref[...] += jnp.sum(dmn * (mv * _rs(mv)), axis=0, keepdims=True)

    del gm
    return pl.pallas_call(
        body, name="mem_bwd", grid=(t // tm,),
        in_specs=[pl.BlockSpec((tm, d), lambda i: (i, 0)), pl.BlockSpec((tm, 2 * d), lambda i: (i, 0)),
                  pl.BlockSpec((d, 2 * d), lambda i: (0, 0))],
        out_specs=pl.BlockSpec((1, d), lambda i: (0, 0)),
        out_shape=jax.ShapeDtypeStruct((1, d), F32),
        compiler_params=_params(32, ("arbitrary",)),
    )(mem, dkv, w_ckv)


def _ffn_loss_fwd(h2, gf, w1, w2, gl, target, tm):
    t, d = h2.shape
    tm = min(tm, t)

    def body(h2_ref, gf_ref, w1_ref, w2_ref, gl_ref, tg_ref, hn_ref, f_ref, dh3_ref, dgl_ref, loss_ref):
        @pl.when(pl.program_id(0) == 0)
        def _():
            dgl_ref[...] = jnp.zeros_like(dgl_ref)
            loss_ref[...] = jnp.zeros_like(loss_ref)

        h2 = h2_ref[...]
        hn = (h2 * _rs(h2) * gf_ref[...]).astype(BF)
        hn_ref[...] = hn
        h3 = h2
        for c in range(4):
            f = jnp.maximum(_dot(hn, w1_ref[c]), 0.0)
            f_ref[:, c * 1024:(c + 1) * 1024] = f.astype(BF)
            h3 = h3 + _dot((f * f).astype(BF), w2_ref[c])
        r3 = _rs(h3)
        yh = h3 * r3
        glv = gl_ref[...]
        e = yh * glv - tg_ref[...]
        loss_ref[...] += 0.5 * jnp.sum(jnp.sum(e * e, axis=-1, keepdims=True) * (1.0 / d), axis=0, keepdims=True)
        dy = e * (1.0 / d)
        dx, dg = _rms_bwd(dy, yh, r3, glv)
        dh3_ref[...] = dx
        dgl_ref[...] += jnp.sum(dg, axis=0, keepdims=True)

    row = lambda width: pl.BlockSpec((tm, width), lambda i: (i, 0))
    return pl.pallas_call(
        body, name="ffn_loss_fwd", grid=(t // tm,),
        in_specs=[row(d), pl.BlockSpec((1, d), lambda i: (0, 0)), pl.BlockSpec((4, d, 1024), lambda i: (0, 0, 0), pipeline_mode=pl.Buffered(1)),
                  pl.BlockSpec((4, 1024, d), lambda i: (0, 0, 0), pipeline_mode=pl.Buffered(1)),
                  pl.BlockSpec((1, d), lambda i: (0, 0)), row(d)],
        out_specs=[row(d), row(D_FF), row(d), pl.BlockSpec((1, d), lambda i: (0, 0)),
                   pl.BlockSpec((1, 1), lambda i: (0, 0))],
        out_shape=[jax.ShapeDtypeStruct((t, d), BF), jax.ShapeDtypeStruct((t, D_FF), BF),
                   jax.ShapeDtypeStruct((t, d), F32), jax.ShapeDtypeStruct((1, d), F32),
                   jax.ShapeDtypeStruct((1, 1), F32)],
        compiler_params=_params(56, ("arbitrary",)),
    )(h2, gf, w1, w2, gl, target)


def _ffn_bwd(dh3, f, h2, gf, w1, w2, tm):
    t, d = h2.shape
    tm = min(tm, t)

    def body(dh3_ref, f_ref, h2_ref, gf_ref, w1_ref, w2_ref, dh2_ref, dpre_ref, dgf_ref):
        @pl.when(pl.program_id(0) == 0)
        def _():
            dgf_ref[...] = jnp.zeros_like(dgf_ref)

        dh3 = dh3_ref[...]
        dh3b = dh3.astype(BF)
        dhn = jnp.zeros((tm, d), F32)
        for c in range(4):
            cols = slice(c * 1024, (c + 1) * 1024)
            dpre = (_dot_bt(dh3b, w2_ref[c]) * (2.0 * f_ref[:, cols].astype(F32))).astype(BF)
            dpre_ref[:, cols] = dpre
            dhn = dhn + _dot_bt(dpre, w1_ref[c])
        h2 = h2_ref[...]
        r = _rs(h2)
        dx, dg = _rms_bwd(dhn, h2 * r, r, gf_ref[...])
        dh2_ref[...] = dh3 + dx
        dgf_ref[...] += jnp.sum(dg, axis=0, keepdims=True)

    row = lambda width: pl.BlockSpec((tm, width), lambda i: (i, 0))
    return pl.pallas_call(
        body, name="ffn_bwd", grid=(t // tm,),
        in_specs=[row(d), row(D_FF), row(d), pl.BlockSpec((1, d), lambda i: (0, 0)),
                  pl.BlockSpec((4, d, 1024), lambda i: (0, 0, 0), pipeline_mode=pl.Buffered(1)),
                  pl.BlockSpec((4, 1024, d), lambda i: (0, 0, 0), pipeline_mode=pl.Buffered(1))],
        out_specs=[row(d), row(D_FF), pl.BlockSpec((1, d), lambda i: (0, 0))],
        out_shape=[jax.ShapeDtypeStruct((t, d), F32), jax.ShapeDtypeStruct((t, D_FF), BF),
                   jax.ShapeDtypeStruct((1, d), F32)],
        compiler_params=_params(56, ("arbitrary",)),
    )(dh3, f, h2, gf, w1, w2)


def _in_bwd(dproj, dh1, x, g, w_in, tm, ride=None):
    t, d = x.shape
    n = w_in.shape[1]
    tm = min(tm, t)

    def body(dp_ref, dh1_ref, x_ref, g_ref, w_ref, dx_ref, dg_ref):
        @pl.when(pl.program_id(0) == 0)
        def _():
            dg_ref[...] = jnp.zeros_like(dg_ref)

        dxn = _dot_bt(dp_ref[...], w_ref[...])
        xv = x_ref[...]
        r = _rs(xv)
        dx, dg = _rms_bwd(dxn, xv * r, r, g_ref[...])
        dx_ref[...] = dh1_ref[...] + dx
        dg_ref[...] += jnp.sum(dg, axis=0, keepdims=True)

    row = lambda width: pl.BlockSpec((tm, width), lambda i: (i, 0))
    (dx, dg), rode = _ride_call(
        body, "in_bwd", (t // tm,),
        in_specs=[row(n), row(d), row(d), pl.BlockSpec((1, d), lambda i: (0, 0)),
                  pl.BlockSpec((d, n), lambda i: (0, 0))],
        out_specs=[row(d), pl.BlockSpec((1, d), lambda i: (0, 0))],
        out_shape=[jax.ShapeDtypeStruct((t, d), F32), jax.ShapeDtypeStruct((1, d), F32)],
        scratch_shapes=[], operands=(dproj, dh1, x, g, w_in), vmem_mb=48, ride=ride)
    return dx, dg, rode


class _GradReduce:
    def __init__(self, c_idx):
        self.c_idx = c_idx
        self.sums = {}

    def sibling(self, slabs):
        return _SiblingExchange(slabs)

    def chip(self, names, slabs, recv):
        for k, a, r in zip(names, slabs, recv):
            self.sums[k] = _chip_sum(a, r, self.c_idx, "chip_sum_" + k)
        return _ChipExchange([self.sums[k] for k in names])


def _full_weights(gathered):
    d = D_MODEL
    out = {}
    for k, a in gathered.items():
        if k in ("w_in", "w_ckv", "w_ff1"):
            out[k] = a.transpose(1, 0, 2).reshape(d, -1)
        else:
            out[k] = a.reshape(-1, d)
    return out


def _slabs(a):
    return a.reshape(N_DEV, -1, a.shape[-1])


def _local_step(x, mem, target, small, big, nb, s, tq=256, gather_rest=None, reduce=None):
    d = D_MODEL
    g_mix, g_v, w_sp, b_sp, g_head, g_cross, g_mem, g_ffn, g_fin = (
        small[k] for k in ("norm_mix_g", "gm_v_norm_g", "w_spatial", "b_spatial", "head_norm_g", "norm_cross_g",
                           "norm_mem_g", "norm_ffn_g", "norm_final_g"))
    tri = jnp.tril(jnp.ones((CHUNK, CHUNK), dtype=bool))
    w_sp_m = jnp.where(tri[None], w_sp, 0.0)
    wt = w_sp_m.astype(BF)
    wtt = jnp.swapaxes(w_sp_m, 1, 2).astype(BF)
    bb = jnp.broadcast_to(b_sp[:, :, None], (GM_GROUPS, CHUNK, CHUNK))
    hg_a = g_head[:, :GM_WIDTH]

    proj, xn = _norm_matmul(x, g_mix, big["w_in"], 512, "in_proj")
    merged = _gmlp_fwd(proj, g_v, wt, bb, hg_a, 512)
    o_sb, tot, merged, nblk, gathered = _sb_fwd(proj, merged, g_head, nb, s, tq, ride=gather_rest)
    if gather_rest is not None:
        big = dict(big, **_full_weights(dict(zip(BIG[1:], gathered))))
    w1c = big["w_ff1"].reshape(d, 4, 1024).transpose(1, 0, 2)
    w2c = big["w_ff2"].reshape(4, 1024, d)
    kv, memn = _norm_matmul(mem, g_mem, big["w_ckv"], 512, "mem_proj")
    h1, h2, hn, oc = _mix_cross_fwd(x, merged, big["w_out"], g_cross, big["w_cq"], kv, big["w_co"], s, 512)
    hn2, f, dh3, d_fin, loss = _ffn_loss_fwd(h2, g_ffn, w1c, w2c, g_fin, target, 512)

    gbig = {}
    dh2, dpre, d_ffn = _ffn_bwd(dh3, f, h2, g_ffn, w1c, w2c, 512)
    gbig["w_ff2"] = _slabs(_wgrad(f, dh3, 1024, 1024, "wgrad_ff2", square_a=True))
    gbig["w_ff1"] = _slabs(_wgrad(hn2, dpre, 1024, 1024, "wgrad_ff1", col_shards=4))
    dh1, dqc, dkv, d_cross = _cross_bwd(dh2, h1, g_cross, big["w_cq"], kv, big["w_co"], s, 512)
    gbig["w_co"] = _slabs(_wgrad(oc, dh2, 1024, 1024, "wgrad_co"))
    gbig["w_cq"] = _slabs(_wgrad(hn, dqc, 1024, 1024, "wgrad_cq"))
    gbig["w_ckv"] = _slabs(_wgrad(memn, dkv, 512, 1024, "wgrad_ckv", col_shards=4))
    d_mem = _mem_bwd(mem, g_mem, dkv, big["w_ckv"], 512)
    dmerged = _matmul_bt(dh1, big["w_out"], 512, "out_bwd")
    gbig["w_out"] = _slabs(_wgrad(merged, dh1, 1024, 1024, "wgrad_out"))
    rest = BIG[1:]
    ride = reduce.sibling([gbig[k] for k in rest]) if reduce else None
    dproj, d_wsp, d_bb, d_gv, d_hga, recv = _gmlp_bwd(proj, dmerged, g_v, wt, wtt, bb, hg_a, 512, ride=ride)
    ride = reduce.chip(rest, [gbig[k] for k in rest], recv) if reduce else None
    dproj, dk, dv, d_hgb, parts_rest = _sb_bwd(proj, o_sb, tot, nblk, dmerged, dproj, g_head, nb, s, tq, ride=ride)
    dproj = _place(_place(dproj, dk, 3, "place_dk"), dv, 4, "place_dv")
    gbig["w_in"] = _slabs(_wgrad(xn, dproj, 640, 1024, "wgrad_in", col_shards=4))
    last = None
    if reduce:
        recv = _run_exchange(reduce.sibling([gbig["w_in"]]), "grad_sibling_exchange_w_in")
        last = reduce.chip(["w_in"], [gbig["w_in"]], recv)
    grad_x, d_mix, _ = _in_bwd(dproj, dh1, x, g_mix, big["w_in"], 512)
    parts = dict(zip(rest, parts_rest))

    gsmall = {
        "norm_mix_g": d_mix, "gm_v_norm_g": d_gv, "w_spatial": d_wsp, "b_spatial": d_bb[:, :, 0],
        "head_norm_g": jnp.concatenate([d_hga, jnp.sum(d_hgb, axis=0)], axis=1), "norm_cross_g": d_cross,
        "norm_mem_g": d_mem, "norm_ffn_g": d_ffn, "norm_final_g": d_fin,
    }
    return loss, grad_x, gsmall, gbig, parts, last


BIG = ("w_in", "w_out", "w_cq", "w_ckv", "w_co", "w_ff1", "w_ff2")
SMALL = ("norm_mix_g", "gm_v_norm_g", "w_spatial", "b_spatial", "head_norm_g", "norm_cross_g", "norm_mem_g",
         "norm_ffn_g", "norm_final_g")


def _local_copies_start(srcs, stages, sems):
    loads = [pltpu.make_async_copy(src, stage, sems.at[w]) for w, (src, stage) in enumerate(zip(srcs, stages))]
    for ld in loads:
        ld.start()
    return loads


def _local_copies_finish(loads, stages, dsts, sems):
    stores = []
    for w, (ld, stage, dst) in enumerate(zip(loads, stages, dsts)):
        ld.wait()
        st = pltpu.make_async_copy(stage, dst, sems.at[w])
        st.start()
        stores.append(st)
    for st in stores:
        st.wait()


def _chip_sum(slabs, recv, c_idx, name):
    _, r, cw = slabs.shape
    tr = min(r, 256)

    def body(c_ref, a_ref, b_ref, o_ref):
        del c_ref
        o_ref[...] = (a_ref[...] + b_ref[...]).astype(BF)

    return pl.pallas_call(
        body, name=name,
        grid_spec=pltpu.PrefetchScalarGridSpec(
            num_scalar_prefetch=1, grid=(N_CHIPS, r // tr),
            in_specs=[pl.BlockSpec((None, tr, cw), lambda p, i, c_ref: (2 * p + c_ref[0], i, 0)),
                      pl.BlockSpec((None, tr, cw), lambda p, i, c_ref: (p, i, 0))],
            out_specs=pl.BlockSpec((None, tr, cw), lambda p, i, c_ref: (p, i, 0))),
        out_shape=jax.ShapeDtypeStruct((N_CHIPS, r, cw), BF),
        compiler_params=_params(32, ("arbitrary", "arbitrary")),
    )(c_idx, *_in_hbm(slabs, recv))


def _sum4(sums, parts, q_idx, name):
    _, r, cw = parts.shape
    tr = min(r, 256)

    def body(q_ref, own_ref, a_ref, b_ref, c_ref, o_ref):
        del q_ref
        o_ref[...] = ((own_ref[...].astype(F32) + a_ref[...].astype(F32)) + b_ref[...].astype(F32)) + c_ref[
            ...].astype(F32)

    spec = lambda k: pl.BlockSpec((None, tr, cw), lambda i, q_ref: ((q_ref[0] + k) % N_CHIPS, i, 0))
    return pl.pallas_call(
        body, name=name,
        grid_spec=pltpu.PrefetchScalarGridSpec(
            num_scalar_prefetch=1, grid=(r // tr,), in_specs=[spec(0), spec(1), spec(2), spec(3)],
            out_specs=pl.BlockSpec((tr, cw), lambda i, q_ref: (i, 0))),
        out_shape=jax.ShapeDtypeStruct((r, cw), F32),
        compiler_params=_params(32, ("arbitrary",)),
    )(q_idx, *_in_hbm(sums, parts, parts, parts))


def _half_exchange(halves):
    n = len(halves)

    def body(*refs):
        ins, outs, stages = refs[:n], refs[n:2 * n], refs[2 * n:3 * n]
        send_sems, recv_sems, ld_sems, st_sems = refs[3 * n:]
        x, y, c = lax.axis_index("x"), lax.axis_index("y"), lax.axis_index("c")
        loads = _local_copies_start(ins, stages, ld_sems)
        copies = []
        for w in range(n):
            cp = pltpu.make_async_remote_copy(
                src_ref=ins[w], dst_ref=outs[w].at[c], send_sem=send_sems.at[w], recv_sem=recv_sems.at[w],
                device_id=(x, y, 1 - c), device_id_type=MESH)
            cp.start()
            copies.append(cp)
        _local_copies_finish(loads, stages, [outs[w].at[c] for w in range(n)], st_sems)
        for cp in copies:
            cp.wait()

    return pl.pallas_call(
        body, name="grad_half_exchange",
        in_specs=[ANY] * n, out_specs=[ANY] * n,
        out_shape=[jax.ShapeDtypeStruct((2,) + a.shape, a.dtype) for a in halves],
        scratch_shapes=[pltpu.VMEM(a.shape, a.dtype) for a in halves] + [
            pltpu.SemaphoreType.DMA((n,)), pltpu.SemaphoreType.DMA((n,)),
            pltpu.SemaphoreType.DMA((n,)), pltpu.SemaphoreType.DMA((n,))],
        compiler_params=_params(24),
    )(*halves)


def _small_all_reduce(packed, ride=None):
    rows = packed.shape[0]
    ride = ride or _NoExchange()
    ri, ro = len(ride.in_arrays), len(ride.out_shape)

    def body(*refs):
        in_ref, rins, out_ref, routs = refs[0], refs[1:1 + ri], refs[1 + ri], refs[2 + ri:2 + ri + ro]
        pair, chip_sum, chips, d2d_send, d2d_recv, ici_send, ici_recv = refs[2 + ri + ro:9 + ri + ro]
        rscr = refs[9 + ri + ro:]
        ride.start(rins, routs, rscr)
        x, y, c = lax.axis_index("x"), lax.axis_index("y"), lax.axis_index("c")
        q = 2 * x + y
        pair[c] = in_ref[...]
        swap = pltpu.make_async_remote_copy(
            src_ref=in_ref, dst_ref=pair.at[c], send_sem=d2d_send, recv_sem=d2d_recv,
            device_id=(x, y, 1 - c), device_id_type=MESH)
        swap.start()
        swap.wait()
        both = pair[0] + pair[1]
        chip_sum[...] = both
        chips[q] = both
        copies = [pltpu.make_async_remote_copy(
            src_ref=chip_sum, dst_ref=chips.at[q], send_sem=ici_send.at[k], recv_sem=ici_recv.at[k],
            device_id=(px, py, c), device_id_type=MESH) for k, (px, py) in enumerate(_other_chips(x, y))]
        for cp in copies:
            cp.start()
        for cp in copies:
            cp.wait()
        out_ref[...] = ((chips[0] + chips[1]) + chips[2]) + chips[3]
        ride.finish(rins, routs, rscr)

    vmem = pl.BlockSpec(memory_space=pltpu.VMEM)
    res = pl.pallas_call(
        body, name="small_all_reduce",
        in_specs=[vmem] + [ANY] * ri, out_specs=[vmem] + [ANY] * ro,
        out_shape=[jax.ShapeDtypeStruct(packed.shape, F32)] + list(ride.out_shape),
        scratch_shapes=[pltpu.VMEM((2, rows, 128), F32), pltpu.VMEM((rows, 128), F32),
                        pltpu.VMEM((N_CHIPS, rows, 128), F32), pltpu.SemaphoreType.DMA, pltpu.SemaphoreType.DMA,
                        pltpu.SemaphoreType.DMA((3,)), pltpu.SemaphoreType.DMA((3,))] + list(ride.scratch_shapes),
        compiler_params=_params(16),
    )(packed, *ride.in_arrays)
    return res[0], res[1:]


def _adamw(g, w, m, v, name):
    r, cw = g.shape
    tr = 256 if r % 256 == 0 else r

    def body(g_ref, w_ref, m_ref, v_ref, d_ref, nm_ref, nv_ref):
        gv = g_ref[...]
        nm = ADAM_B1 * m_ref[...] + (1.0 - ADAM_B1) * gv
        nv = ADAM_B2 * v_ref[...] + (1.0 - ADAM_B2) * (gv * gv)
        m_hat = nm / (1.0 - ADAM_B1 ** ADAM_STEP)
        v_hat = nv / (1.0 - ADAM_B2 ** ADAM_STEP)
        d_ref[...] = -ADAM_LR * (m_hat / (jnp.sqrt(v_hat) + ADAM_EPS) + ADAM_WD * w_ref[...])
        nm_ref[...] = nm
        nv_ref[...] = nv

    spec = pl.BlockSpec((tr, cw), lambda i: (i, 0))
    return pl.pallas_call(
        body, name=name, grid=(r // tr,),
        in_specs=[spec] * 4, out_specs=[spec] * 3,
        out_shape=[jax.ShapeDtypeStruct((r, cw), F32)] * 3,
        compiler_params=_params(32, ("arbitrary",)),
    )(*_in_hbm(g, w, m, v))


def _small_params(args):
    small = {k: args[k].reshape(1, -1) for k in SMALL}
    small["w_spatial"] = args["w_spatial"][0]
    small["b_spatial"] = args["b_spatial"][0]
    return small


def _pack(parts, rows):
    flat = jnp.concatenate([p.reshape(-1).astype(F32) for p in parts])
    return jnp.pad(flat, (0, rows * 128 - flat.shape[0])).reshape(rows, 128)


def _unpack(packed, shapes):
    flat = packed.reshape(-1)
    out, off = [], 0
    for shp in shapes:
        size = math.prod(shp)
        out.append(flat[off:off + size].reshape(shp))
        off += size
    return out


def kernel(x, mem, norm_mix_g, w_in, gm_v_norm_g, w_spatial, b_spatial, head_norm_g, w_out, norm_cross_g, norm_mem_g, w_cq, w_ckv, w_co, norm_ffn_g, w_ff1, w_ff2, norm_final_g, loss_target, m_norm_mix_g, m_w_in, m_gm_v_norm_g, m_w_spatial, m_b_spatial, m_head_norm_g, m_w_out, m_norm_cross_g, m_norm_mem_g, m_w_cq, m_w_ckv, m_w_co, m_norm_ffn_g, m_w_ff1, m_w_ff2, m_norm_final_g, v_norm_mix_g, v_w_in, v_gm_v_norm_g, v_w_spatial, v_b_spatial, v_head_norm_g, v_w_out, v_norm_cross_g, v_norm_mem_g, v_w_cq, v_w_ckv, v_w_co, v_norm_ffn_g, v_w_ff1, v_w_ff2, v_norm_final_g):
    args = dict(locals())
    d = D_MODEL
    nb, s, _ = x.shape
    c_idx = lax.axis_index("c").astype(jnp.int32).reshape(1)
    q_idx = (2 * lax.axis_index("x") + lax.axis_index("y")).astype(jnp.int32).reshape(1)
    rest = BIG[1:]

    shards = {k: args[k][0].astype(BF) for k in BIG}
    big = _full_weights({"w_in": _run_exchange(_GatherExchange([shards["w_in"]]), "all_gather_w_in")[0]})
    gather_rest = _GatherExchange([shards[k] for k in rest])

    reduce = _GradReduce(c_idx)
    loss, grad_x, gsmall, _, parts, last = _local_step(
        x.reshape(nb * s, d), mem.reshape(nb * N_MEM, d), loss_target.reshape(nb * s, d), _small_params(args), big,
        nb, s, gather_rest=gather_rest, reduce=reduce)

    shapes = [args[k].shape for k in SMALL]
    n_small = sum(math.prod(sh) for sh in shapes)
    rows = -(-(n_small + 1) // 1024) * 8
    reduced, (parts["w_in"],) = _small_all_reduce(_pack([gsmall[k] for k in SMALL] + [loss], rows), ride=last)
    halves = [_sum4(reduce.sums[k], parts[k], q_idx, "sum4_" + k) for k in BIG]
    both = _half_exchange(halves)

    out = {"grad_x": grad_x.reshape(nb, s, d)}
    for k, g2 in zip(BIG, both):
        shp = args[k].shape
        g = g2.reshape(shp[1], shp[2])
        dl, nm, nv = _adamw(g, args[k][0], args["m_" + k][0], args["v_" + k][0], "adamw_" + k)
        out["grad_" + k], out["delta_" + k], out["new_m_" + k], out["new_v_" + k] = (
            a.reshape(shp) for a in (g, dl, nm, nv))

    dl, nm, nv = _adamw(reduced, _pack([args[k] for k in SMALL], rows), _pack([args["m_" + k] for k in SMALL], rows),
                        _pack([args["v_" + k] for k in SMALL], rows), "adamw_small")
    for name, arr in (("grad_", reduced), ("delta_", dl), ("new_m_", nm), ("new_v_", nv)):
        for k, a in zip(SMALL, _unpack(arr, shapes)):
            out[name + k] = a
    out["loss"] = reduced.reshape(-1)[n_small]

    names = ["norm_mix_g", "w_in", "gm_v_norm_g", "w_spatial", "b_spatial", "head_norm_g", "w_out", "norm_cross_g",
             "norm_mem_g", "w_cq", "w_ckv", "w_co", "norm_ffn_g", "w_ff1", "w_ff2", "norm_final_g"]
    return (out["loss"], out["grad_x"], *[out["grad_" + k] for k in names], *[out["delta_" + k] for k in names],
            *[out["new_m_" + k] for k in names], *[out["new_v_" + k] for k in names])
```

```python
import functools
import math

import jax
import jax.numpy as jnp
from jax import lax
from jax.experimental import pallas as pl
from jax.experimental.pallas import tpu as pltpu

F32 = jnp.float32
BF = jnp.bfloat16

EPS = 1e-6
D_MODEL = 1024
CHUNK = 128
GM_GROUPS = 4
GM_WIDTH = 512
SB_WIDTH = 512
HEAD_LANES = 64
SB_SCALE = 0.125
SB_SKIP = -104.0
X_HEADS = 4
X_HEAD_DIM = 256
N_MEM = 256
D_FF = 4096
IN_COLS = 2560
N_CHIPS = 4
N_DEV = 8

ADAM_LR = 0.001
ADAM_B1 = 0.9
ADAM_B2 = 0.999
ADAM_EPS = 1e-08
ADAM_WD = 0.01
ADAM_STEP = 10

V7X_VMEM_BYTES = 64 * 1024 * 1024
MESH = pl.DeviceIdType.MESH
ANY = pl.BlockSpec(memory_space=pl.ANY)

GELU_C = math.sqrt(2.0 / math.pi)
GELU_A = 0.044715


def _params(vmem_mb, sem=None):
    assert vmem_mb * 1024 * 1024 <= V7X_VMEM_BYTES
    return pltpu.CompilerParams(vmem_limit_bytes=vmem_mb * 1024 * 1024, dimension_semantics=sem)


PIN_MIN_ELEMENTS = 1 << 18


def _in_hbm(*arrays):
    return tuple(pltpu.with_memory_space_constraint(a, pltpu.HBM) if a.size >= PIN_MIN_ELEMENTS else a
                 for a in arrays)


def _dot(a, b):
    return jnp.dot(a, b, preferred_element_type=F32)


def _dot_bt(a, b):
    return lax.dot_general(a, b, (((1,), (1,)), ((), ())), preferred_element_type=F32)


def _dot_at(a, b):
    return lax.dot_general(a, b, (((0,), (0,)), ((), ())), preferred_element_type=F32)


def _gelu(x):
    t = jnp.tanh(GELU_C * (x + GELU_A * x * x * x))
    return 0.5 * x * (1.0 + t)


def _gelu_and_grad(x):
    x2 = x * x
    t = jnp.tanh(GELU_C * (x + GELU_A * x2 * x))
    h = 0.5 * (1.0 + t)
    return x * h, h + 0.5 * x * (1.0 - t * t) * (GELU_C * (1.0 + 3.0 * GELU_A * x2))


def _rs(x):
    return lax.rsqrt(jnp.mean(x * x, axis=-1, keepdims=True) + EPS)


def _rms_bwd(dxn, xhat, r, g):
    dxh = dxn * g
    dx = r * (dxh - xhat * jnp.mean(dxh * xhat, axis=-1, keepdims=True))
    return dx, dxn * xhat


def _norm_matmul(x, g, w, tm, name):
    t, d = x.shape
    n = w.shape[1]
    tm = min(tm, t)

    def body(x_ref, g_ref, w_ref, out_ref, xn_ref):
        xv = x_ref[...]
        xn = (xv * _rs(xv) * g_ref[...]).astype(BF)
        xn_ref[...] = xn
        out_ref[...] = _dot(xn, w_ref[...]).astype(out_ref.dtype)

    return pl.pallas_call(
        body, name=name, grid=(t // tm,),
        in_specs=[pl.BlockSpec((tm, d), lambda i: (i, 0)), pl.BlockSpec((1, d), lambda i: (0, 0)),
                  pl.BlockSpec((d, n), lambda i: (0, 0))],
        out_specs=[pl.BlockSpec((tm, n), lambda i: (i, 0)), pl.BlockSpec((tm, d), lambda i: (i, 0))],
        out_shape=[jax.ShapeDtypeStruct((t, n), BF), jax.ShapeDtypeStruct((t, d), BF)],
        compiler_params=_params(48, ("arbitrary",)),
    )(*_in_hbm(x, g, w))


def _wgrad(a, g, tn, tk, name, square_a=False, col_shards=1):
    t, m = a.shape
    n = g.shape[1]
    tk = min(tk, t)
    tm = min(m, 1024)
    ns = n // col_shards
    assert ns % tn == 0 and m % tm == 0
    per = ns // tn
    nk = t // tk

    def body(a_ref, g_ref, o_ref):
        k = pl.program_id(2)

        @pl.when(k == 0)
        def _():
            o_ref[...] = jnp.zeros_like(o_ref)

        av = a_ref[...]
        if square_a:
            af = av.astype(F32)
            av = af * af
        o_ref[...] += _dot_at(av.astype(BF), g_ref[...].astype(BF))

    return pl.pallas_call(
        body, name=name, grid=(m // tm, n // tn, nk),
        in_specs=[pl.BlockSpec((tk, tm), lambda i, j, k: (k, i)), pl.BlockSpec((tk, tn), lambda i, j, k: (k, j))],
        out_specs=pl.BlockSpec((None, tm, tn), lambda i, j, k: (j // per, i, j % per)),
        out_shape=jax.ShapeDtypeStruct((col_shards, m, ns), F32),
        compiler_params=_params(48, ("arbitrary", "arbitrary", "arbitrary")),
    )(*_in_hbm(a, g))


def _matmul_bt(a, w, tm, name):
    t, n = a.shape
    k = w.shape[0]
    tm = min(tm, t)

    def body(a_ref, w_ref, o_ref):
        o_ref[...] = _dot_bt(a_ref[...].astype(BF), w_ref[...]).astype(o_ref.dtype)

    return pl.pallas_call(
        body, name=name, grid=(t // tm,),
        in_specs=[pl.BlockSpec((tm, n), lambda i: (i, 0)), pl.BlockSpec((k, n), lambda i: (0, 0))],
        out_specs=pl.BlockSpec((tm, k), lambda i: (i, 0)),
        out_shape=jax.ShapeDtypeStruct((t, k), BF),
        compiler_params=_params(32, ("arbitrary",)),
    )(*_in_hbm(a, w))


def _gmlp_fwd(proj, gg, wt, bb, hg, tm):
    t = proj.shape[0]
    tm = min(tm, t)

    def body(u_ref, v_ref, gg_ref, wt_ref, bb_ref, hg_ref, out_ref):
        for cc in range(tm // CHUNK):
            rows = slice(cc * CHUNK, (cc + 1) * CHUNK)
            for g in range(GM_GROUPS):
                cols = slice(g * 128, (g + 1) * 128)
                u = _gelu(u_ref[rows, cols].astype(F32))
                gv = _gelu(v_ref[rows, cols].astype(F32))
                vn = gv * _rs(gv) * gg_ref[:, cols]
                mixed = _dot(wt_ref[g], vn.astype(BF)) + bb_ref[g]
                a = u * mixed
                out_ref[rows, cols] = (a * _rs(a) * hg_ref[:, cols]).astype(BF)

    return pl.pallas_call(
        body, name="gmlp_fwd", grid=(t // tm,),
        in_specs=[pl.BlockSpec((tm, 512), lambda i: (i, 0)), pl.BlockSpec((tm, 512), lambda i: (i, 1)),
                  pl.BlockSpec((1, 512), lambda i: (0, 0)), pl.BlockSpec((4, 128, 128), lambda i: (0, 0, 0)),
                  pl.BlockSpec((4, 128, 128), lambda i: (0, 0, 0)), pl.BlockSpec((1, 512), lambda i: (0, 0))],
        out_specs=pl.BlockSpec((tm, 512), lambda i: (i, 0)),
        out_shape=jax.ShapeDtypeStruct((t, 1024), BF),
        compiler_params=_params(32, ("arbitrary",)),
    )(*_in_hbm(proj, proj, gg, wt, bb, hg))


def _gmlp_bwd(proj, dmerged, gg, wt, wtt, bb, hg, tm, ride=None):
    t = proj.shape[0]
    tm = min(tm, t)
    nsteps = t // tm

    def body(u_ref, v_ref, dm_ref, gg_ref, wt_ref, wtt_ref, bb_ref, hg_ref,
             dp_ref, dw_ref, db_ref, dgg_ref, dhg_ref):
        i = pl.program_id(0)

        @pl.when(i == 0)
        def _():
            dw_ref[...] = jnp.zeros_like(dw_ref)
            db_ref[...] = jnp.zeros_like(db_ref)
            dgg_ref[...] = jnp.zeros_like(dgg_ref)
            dhg_ref[...] = jnp.zeros_like(dhg_ref)

        for cc in range(tm // CHUNK):
            rows = slice(cc * CHUNK, (cc + 1) * CHUNK)
            for g in range(GM_GROUPS):
                cols = slice(g * 128, (g + 1) * 128)
                up = u_ref[rows, cols].astype(F32)
                gp = v_ref[rows, cols].astype(F32)
                u, u_grad = _gelu_and_grad(up)
                gv, gv_grad = _gelu_and_grad(gp)
                rv = _rs(gv)
                gvh = gv * rv
                ggv = gg_ref[:, cols]
                vnb = (gvh * ggv).astype(BF)
                mixed = _dot(wt_ref[g], vnb) + bb_ref[g]
                a = u * mixed
                ra = _rs(a)
                ah = a * ra
                dm = dm_ref[rows, cols].astype(F32)
                dhg_ref[:, cols] += jnp.sum(dm * ah, axis=0, keepdims=True)
                dah = dm * hg_ref[:, cols]
                da = ra * (dah - ah * jnp.mean(dah * ah, axis=-1, keepdims=True))
                du = da * mixed
                dmix = da * u
                db_ref[g] += dmix
                dmb = dmix.astype(BF)
                dw_ref[g] += _dot_bt(dmb, vnb)
                dvn = _dot(wtt_ref[g], dmb)
                dgg_ref[:, cols] += jnp.sum(dvn * gvh, axis=0, keepdims=True)
                dgh = dvn * ggv
                dgv = rv * (dgh - gvh * jnp.mean(dgh * gvh, axis=-1, keepdims=True))
                dp_ref[rows, cols] = (du * u_grad).astype(BF)
                dp_ref[rows, 512 + g * 128:512 + (g + 1) * 128] = (dgv * gv_grad).astype(BF)

        @pl.when(i == nsteps - 1)
        def _():
            r = lax.broadcasted_iota(jnp.int32, (CHUNK, CHUNK), 0)
            c = lax.broadcasted_iota(jnp.int32, (CHUNK, CHUNK), 1)
            for g in range(GM_GROUPS):
                dw_ref[g] = jnp.where(c <= r, dw_ref[g], 0.0)
                db_ref[g] = jnp.broadcast_to(jnp.sum(db_ref[g], axis=-1, keepdims=True), (CHUNK, CHUNK))

    small = lambda shape: pl.BlockSpec(shape, lambda i: (0,) * len(shape))
    res, rode = _ride_call(
        body, "gmlp_bwd", (nsteps,),
        in_specs=[pl.BlockSpec((tm, 512), lambda i: (i, 0)), pl.BlockSpec((tm, 512), lambda i: (i, 1)),
                  pl.BlockSpec((tm, 512), lambda i: (i, 0)), small((1, 512)), small((4, 128, 128)),
                  small((4, 128, 128)), small((4, 128, 128)), small((1, 512))],
        out_specs=[pl.BlockSpec((tm, 1024), lambda i: (i, 0)), small((4, 128, 128)), small((4, 128, 128)),
                   small((1, 512)), small((1, 512))],
        out_shape=[jax.ShapeDtypeStruct((t, IN_COLS), BF), jax.ShapeDtypeStruct((4, 128, 128), F32),
                   jax.ShapeDtypeStruct((4, 128, 128), F32), jax.ShapeDtypeStruct((1, 512), F32),
                   jax.ShapeDtypeStruct((1, 512), F32)],
        scratch_shapes=[], operands=(proj, proj, dmerged, gg, wt, wtt, bb, hg), vmem_mb=32, ride=ride)
    return (*res, rode)


def _other_chips(x, y):
    return ((1 - x, y), (x, 1 - y), (1 - x, 1 - y))


class _GatherExchange:
    def __init__(self, shards):
        n = len(shards)
        self.n = n
        self.in_arrays = list(shards)
        self.out_shape = [jax.ShapeDtypeStruct((N_CHIPS,) + a.shape, a.dtype) for a in shards]
        self.half_rows = [a.shape[0] // 2 for a in shards]
        sems = lambda k: pltpu.SemaphoreType.DMA((k,))
        self.scratch_shapes = [pltpu.VMEM(a.shape, a.dtype) for a in shards] + [
            sems(3 * n), sems(3 * n), sems(3 * n), sems(3 * n), sems(n), sems(n)]

    def _copies(self, ins, outs, scr):
        n = self.n
        stages, (ici_send, ici_recv, d2d_send, d2d_recv, ld_sems, st_sems) = scr[:n], scr[n:]
        x, y, c = lax.axis_index("x"), lax.axis_index("y"), lax.axis_index("c")
        q = 2 * x + y
        loads = [pltpu.make_async_copy(ins[w], stages[w], ld_sems.at[w]) for w in range(n)]
        stores = [pltpu.make_async_copy(stages[w], outs[w].at[q], st_sems.at[w]) for w in range(n)]
        ici, d2d = [], []
        for w in range(n):
            half = pl.ds(c * self.half_rows[w], self.half_rows[w])
            for k, (px, py) in enumerate(_other_chips(x, y)):
                ici.append(pltpu.make_async_remote_copy(
                    src_ref=ins[w].at[half], dst_ref=outs[w].at[q, half], send_sem=ici_send.at[3 * w + k],
                    recv_sem=ici_recv.at[3 * w + k], device_id=(px, py, c), device_id_type=MESH))
                landed = outs[w].at[2 * px + py, half]
                d2d.append(pltpu.make_async_remote_copy(
                    src_ref=landed, dst_ref=landed, send_sem=d2d_send.at[3 * w + k],
                    recv_sem=d2d_recv.at[3 * w + k], device_id=(x, y, 1 - c), device_id_type=MESH))
        return loads, stores, ici, d2d

    def start(self, ins, outs, scr):
        loads, stores, ici, _ = self._copies(ins, outs, scr)
        for cp in loads + ici:
            cp.start()
        for ld, st in zip(loads, stores):
            ld.wait()
            st.start()

    def relay(self, ins, outs, scr):
        _, _, ici, d2d = self._copies(ins, outs, scr)
        for got, fwd in zip(ici, d2d):
            got.wait_recv()
            fwd.start()

    def finish(self, ins, outs, scr):
        _, stores, ici, d2d = self._copies(ins, outs, scr)
        for cp in ici:
            cp.wait_send()
        for cp in d2d + stores:
            cp.wait()


class _SiblingExchange:
    def __init__(self, slabs):
        n = len(slabs)
        self.n = n
        self.in_arrays = list(slabs)
        self.out_shape = [jax.ShapeDtypeStruct((N_CHIPS,) + a.shape[1:], a.dtype) for a in slabs]
        self.scratch_shapes = [pltpu.SemaphoreType.DMA((4 * n,)), pltpu.SemaphoreType.DMA((4 * n,))]

    def _copies(self, ins, outs, scr):
        send_sems, recv_sems = scr
        x, y, c = lax.axis_index("x"), lax.axis_index("y"), lax.axis_index("c")
        return [pltpu.make_async_remote_copy(
            src_ref=ins[w].at[2 * p + (1 - c)], dst_ref=outs[w].at[p], send_sem=send_sems.at[4 * w + p],
            recv_sem=recv_sems.at[4 * w + p], device_id=(x, y, 1 - c), device_id_type=MESH)
            for w in range(self.n) for p in range(N_CHIPS)]

    def start(self, ins, outs, scr):
        for cp in self._copies(ins, outs, scr):
            cp.start()

    def finish(self, ins, outs, scr):
        for cp in self._copies(ins, outs, scr):
            cp.wait()


class _ChipExchange:
    def __init__(self, sums):
        n = len(sums)
        self.n = n
        self.in_arrays = list(sums)
        self.out_shape = [jax.ShapeDtypeStruct(a.shape, a.dtype) for a in sums]
        self.scratch_shapes = [pltpu.SemaphoreType.DMA((3 * n,)), pltpu.SemaphoreType.DMA((3 * n,))]

    def _copies(self, ins, outs, scr):
        send_sems, recv_sems = scr
        x, y, c = lax.axis_index("x"), lax.axis_index("y"), lax.axis_index("c")
        q = 2 * x + y
        return [pltpu.make_async_remote_copy(
            src_ref=ins[w].at[2 * px + py], dst_ref=outs[w].at[q], send_sem=send_sems.at[3 * w + k],
            recv_sem=recv_sems.at[3 * w + k], device_id=(px, py, c), device_id_type=MESH)
            for w in range(self.n) for k, (px, py) in enumerate(_other_chips(x, y))]

    def start(self, ins, outs, scr):
        for cp in self._copies(ins, outs, scr):
            cp.start()

    def finish(self, ins, outs, scr):
        for cp in self._copies(ins, outs, scr):
            cp.wait()


class _NoExchange:
    in_arrays, out_shape, scratch_shapes = (), (), ()

    def start(self, ins, outs, scr):
        pass

    def finish(self, ins, outs, scr):
        pass


def _run_exchange(ex, name):
    n_in, n_out = len(ex.in_arrays), len(ex.out_shape)

    def body(*refs):
        ins, outs, scr = refs[:n_in], refs[n_in:n_in + n_out], refs[n_in + n_out:]
        ex.start(ins, outs, scr)
        if hasattr(ex, "relay"):
            ex.relay(ins, outs, scr)
        ex.finish(ins, outs, scr)

    return pl.pallas_call(
        body, name=name, in_specs=[ANY] * n_in, out_specs=[ANY] * n_out, out_shape=ex.out_shape,
        scratch_shapes=ex.scratch_shapes, compiler_params=_params(24),
    )(*ex.in_arrays)


def _ride_call(body, name, grid, in_specs, out_specs, out_shape, scratch_shapes, operands, vmem_mb, ride=None,
               aliases=None):
    ride = ride or _NoExchange()
    ni, no, ns = len(in_specs), len(out_specs), len(scratch_shapes)
    ri, ro = len(ride.in_arrays), len(ride.out_shape)
    total = math.prod(grid)

    def wrapped(*refs):
        ins, rins = refs[:ni], refs[ni:ni + ri]
        outs, routs = refs[ni + ri:ni + ri + no], refs[ni + ri + no:ni + ri + no + ro]
        scr, rscr = refs[ni + ri + no + ro:ni + ri + no + ro + ns], refs[ni + ri + no + ro + ns:]
        step = pl.program_id(0)
        for ax in range(1, len(grid)):
            step = step * grid[ax] + pl.program_id(ax)

        @pl.when(step == 0)
        def _():
            ride.start(rins, routs, rscr)

        if hasattr(ride, "relay"):
            @pl.when(step == (3 * total) // 4)
            def _():
                ride.relay(rins, routs, rscr)

        body(*ins, *outs, *scr)

        @pl.when(step == total - 1)
        def _():
            ride.finish(rins, routs, rscr)

    res = pl.pallas_call(
        wrapped, name=name, grid=grid, in_specs=list(in_specs) + [ANY] * ri, out_specs=list(out_specs) + [ANY] * ro,
        out_shape=list(out_shape) + list(ride.out_shape),
        scratch_shapes=list(scratch_shapes) + list(ride.scratch_shapes), input_output_aliases=aliases or {},
        compiler_params=_params(vmem_mb, ("arbitrary",) * len(grid)),
    )(*_in_hbm(*operands), *ride.in_arrays)
    return res[:no], res[no:]


def _neg_log_sig(z):
    n = jnp.maximum(z, 0.0) + jnp.log(1.0 + jnp.exp(-jnp.abs(z)))
    return n, z - n


def _running_sums(n, tri2):
    hi = n.astype(BF)
    lo = (n - hi.astype(F32)).astype(BF)
    return _dot(jnp.concatenate([hi, lo], axis=1), tri2)


def _head_sums(x, h0):
    s0 = jnp.sum(jnp.where(h0, x, 0.0), axis=-1, keepdims=True)
    s1 = jnp.sum(jnp.where(h0, 0.0, x), axis=-1, keepdims=True)
    return jnp.where(h0, s0, s1)


def _sb_setup(q_ref, tq):
    lane = lax.broadcasted_iota(jnp.int32, (tq, 128), 1)
    h0 = lane < HEAD_LANES
    qs = q_ref[...] * SB_SCALE
    zero = jnp.zeros_like(qs)
    qst = jnp.concatenate([jnp.where(h0, qs, zero), jnp.where(h0, zero, qs)], axis=0)
    r = lax.broadcasted_iota(jnp.int32, (2 * tq, tq), 0)
    c = lax.broadcasted_iota(jnp.int32, (2 * tq, tq), 1)
    causal = c < jnp.where(r >= tq, r - tq, r)
    return h0, qst, causal


def _tri(tq, op):
    return op(lax.broadcasted_iota(jnp.int32, (tq, tq), 0), lax.broadcasted_iota(jnp.int32, (tq, tq), 1)).astype(BF)


def _sb_fwd(proj, merged, hg, nb, s, tq, ride=None):
    t = nb * s
    tq = min(tq, s)
    nq = s // tq

    def body(q_ref, k_ref, v_ref, hg_ref, merged_ref, o_ref, tot_ref, mb_ref, nblk_ref, acc, cr):
        del merged_ref
        i = pl.program_id(2)
        h0, qst, causal = _sb_setup(q_ref, tq)
        tri_gt = _tri(tq, lambda r, c: r > c)
        tri_gt = jnp.concatenate([tri_gt, tri_gt], axis=0)

        def block(j, masked, c_in):
            start = pl.multiple_of(j * tq, tq)
            kj = k_ref[pl.ds(start, tq), :]
            vj = v_ref[pl.ds(start, tq), :]
            n, l = _neg_log_sig(_dot_bt(qst, kj))
            if masked:
                n = jnp.where(causal, n, 0.0)
            a = jnp.exp(l - (_running_sums(n, tri_gt) + c_in))
            if masked:
                a = jnp.where(causal, a, 0.0)
            return _dot(a.astype(BF), vj), c_in + jnp.sum(n, axis=-1, keepdims=True)

        @pl.when(i == 0)
        def _():
            acc[...], cr[...] = block(0, True, jnp.zeros((2 * tq, 1), F32))

        @pl.when(i > 0)
        def _():
            p_diag, c_diag = block(i, True, jnp.zeros((2 * tq, 1), F32))
            p_prev, c_prev = block(i - 1, False, c_diag)
            acc[...] = p_diag + p_prev
            cr[...] = c_prev

        def cond(carry):
            return jnp.logical_and(carry[0] < i, carry[1] < -SB_SKIP)

        def step(carry):
            p, c_new = block(i - 1 - carry[0], False, cr[...])
            acc[...] += p
            cr[...] = c_new
            return carry[0] + 1, jnp.min(c_new)

        walked, _ = lax.while_loop(cond, step, (jnp.minimum(i, 1), jnp.min(cr[...])))

        o = jnp.where(h0, acc[0:tq, :], acc[tq:2 * tq, :])
        o_ref[...] = o
        tot_ref[...] = jnp.where(h0, cr[0:tq, :], cr[tq:2 * tq, :])
        ro = lax.rsqrt(_head_sums(o * o, h0) * (1.0 / HEAD_LANES) + EPS)
        mb_ref[...] = (o * ro * hg_ref[...]).astype(BF)
        nblk_ref[...] = jnp.full((8, 128), walked.astype(F32))

    blk = lambda col0: pl.BlockSpec((tq, 128), lambda b, hp, i: (b * nq + i, col0 + hp))
    seq = lambda col0: pl.BlockSpec((s, 128), lambda b, hp, i: (b, col0 + hp))
    (o, tot, mb, nblk), rode = _ride_call(
        body, "sb_fwd", (nb, 4, nq),
        in_specs=[blk(8), seq(12), seq(16), pl.BlockSpec((1, 128), lambda b, hp, i: (0, 4 + hp)), ANY],
        out_specs=[blk(0), blk(0), blk(4), pl.BlockSpec((None, None, 8, 128), lambda b, hp, i: (b, hp, i, 0))],
        out_shape=[jax.ShapeDtypeStruct((t, 512), F32), jax.ShapeDtypeStruct((t, 512), F32),
                   jax.ShapeDtypeStruct((t, 1024), BF), jax.ShapeDtypeStruct((nb, 4, nq * 8, 128), F32)],
        scratch_shapes=[pltpu.VMEM((2 * tq, 128), F32), pltpu.VMEM((2 * tq, 1), F32)],
        operands=(proj, proj, proj, hg, merged), vmem_mb=40, ride=ride, aliases={4: 2})
    return o, tot, mb, nblk, rode


def _sb_bwd(proj, o_sb, tot, nblk, dmerged, dproj, hg, nb, s, tq, ride=None):
    t = nb * s
    tq = min(tq, s)
    nq = s // tq

    def body(q_ref, k_ref, v_ref, o_ref, tot_ref, nblk_ref, dm_ref, hg_ref, dproj_ref,
             dq_ref, dk_ref, dv_ref, dhg_ref, dk_acc, dv_acc, dq_acc, cm, cg):
        del dproj_ref
        i = pl.program_id(2)
        h0, qst, causal = _sb_setup(q_ref, tq)
        tri_le = _tri(tq, lambda r, c: r <= c)
        tri_le = jnp.concatenate([tri_le, tri_le], axis=0)
        tri_lt = _tri(tq, lambda r, c: r < c)

        @pl.when(i == 0)
        def _():
            dk_acc[...] = jnp.zeros_like(dk_acc)
            dv_acc[...] = jnp.zeros_like(dv_acc)
            dhg_ref[...] = jnp.zeros_like(dhg_ref)

        for ref in (dq_acc, cm, cg):
            ref[...] = jnp.zeros_like(ref)

        o = o_ref[...]
        ro = lax.rsqrt(_head_sums(o * o, h0) * (1.0 / HEAD_LANES) + EPS)
        oh = o * ro
        dm = dm_ref[...].astype(F32)
        dhg_ref[...] += jnp.sum(dm * oh, axis=0, keepdims=True)
        doh = dm * hg_ref[...]
        do = (ro * (doh - oh * (_head_sums(doh * oh, h0) * (1.0 / HEAD_LANES)))).astype(BF)
        zb = jnp.zeros_like(do)
        dost = jnp.concatenate([jnp.where(h0, do, zb), jnp.where(h0, zb, do)], axis=0)
        tots = jnp.concatenate([tot_ref[:, 0:1], tot_ref[:, HEAD_LANES:HEAD_LANES + 1]], axis=0)

        def block(j, masked, cm_in, cg_in):
            start = pl.multiple_of(j * tq, tq)
            kj = k_ref[pl.ds(start, tq), :]
            vj = v_ref[pl.ds(start, tq), :]
            n, l = _neg_log_sig(_dot_bt(qst, kj))
            if masked:
                n = jnp.where(causal, n, 0.0)
            a = jnp.exp(l - (tots - cm_in - _running_sums(n, tri_le)))
            if masked:
                a = jnp.where(causal, a, 0.0)
            gm = a * _dot_bt(dost, vj)
            pp = cg_in + _dot(gm.astype(BF), tri_lt)
            dz = gm - jnp.exp(l) * (gm + pp)
            if masked:
                dz = jnp.where(causal, dz, 0.0)
            dzb = dz.astype(BF)
            dk_acc[pl.ds(start, tq), :] += _dot_at(dzb, qst)
            dv_acc[pl.ds(start, tq), :] += _dot_at(a.astype(BF), dost)
            return (_dot(dzb, kj), cm_in + jnp.sum(n, axis=-1, keepdims=True),
                    cg_in + jnp.sum(gm, axis=-1, keepdims=True))

        def step(j, carry):
            dq, cm[...], cg[...] = block(j, False, cm[...], cg[...])
            dq_acc[...] += dq
            return carry

        walked = jnp.clip(jnp.max(nblk_ref[...]).astype(jnp.int32), jnp.minimum(i, 1), i)
        lax.fori_loop(i - walked, i - 1, step, 0)

        @pl.when(i == 0)
        def _():
            dq_acc[...] = block(0, True, cm[...], cg[...])[0]

        @pl.when(i > 0)
        def _():
            dq_prev, cm_prev, cg_prev = block(i - 1, False, cm[...], cg[...])
            dq_acc[...] += dq_prev + block(i, True, cm_prev, cg_prev)[0]

        dq_ref[...] = (jnp.where(h0, dq_acc[0:tq, :], dq_acc[tq:2 * tq, :]) * SB_SCALE).astype(BF)

        @pl.when(i == nq - 1)
        def _():
            dk_ref[...] = dk_acc[...].astype(BF)
            dv_ref[...] = dv_acc[...].astype(BF)

    blk = lambda col0: pl.BlockSpec((tq, 128), lambda b, hp, i: (b * nq + i, col0 + hp))
    seq = lambda col0: pl.BlockSpec((s, 128), lambda b, hp, i: (b, col0 + hp))
    (dq, dk, dv, dhg), rode = _ride_call(
        body, "sb_bwd", (nb, 4, nq),
        in_specs=[blk(8), seq(12), seq(16), blk(0), blk(0),
                  pl.BlockSpec((None, None, 8, 128), lambda b, hp, i: (b, hp, i, 0)), blk(4),
                  pl.BlockSpec((1, 128), lambda b, hp, i: (0, 4 + hp)), ANY],
        out_specs=[blk(8), seq(0), seq(0), pl.BlockSpec((None, 1, 128), lambda b, hp, i: (b, 0, hp))],
        out_shape=[jax.ShapeDtypeStruct((t, IN_COLS), BF), jax.ShapeDtypeStruct((t, 512), BF),
                   jax.ShapeDtypeStruct((t, 512), BF), jax.ShapeDtypeStruct((nb, 1, 512), F32)],
        scratch_shapes=[pltpu.VMEM((s, 128), F32), pltpu.VMEM((s, 128), F32), pltpu.VMEM((2 * tq, 128), F32),
                        pltpu.VMEM((2 * tq, 1), F32), pltpu.VMEM((2 * tq, 1), F32)],
        operands=(proj, proj, proj, o_sb, tot, nblk, dmerged, hg, dproj), vmem_mb=40, ride=ride, aliases={8: 0})
    return dq, dk, dv, dhg, rode


def _pass_through(a, name):
    def body(a_ref, out_ref):
        del a_ref, out_ref

    return pl.pallas_call(
        body, name=name, in_specs=[ANY], out_specs=ANY, out_shape=jax.ShapeDtypeStruct(a.shape, a.dtype),
        input_output_aliases={0: 0},
    )(a)


def _place(buf, piece, col_block, name):
    t, w = piece.shape
    tm = min(t, 1024)

    def body(piece_ref, buf_ref, out_ref):
        del buf_ref
        out_ref[...] = piece_ref[...]

    return pl.pallas_call(
        body, name=name, grid=(t // tm,),
        in_specs=[pl.BlockSpec((tm, w), lambda i: (i, 0)), ANY],
        out_specs=pl.BlockSpec((tm, w), lambda i: (i, col_block)),
        out_shape=jax.ShapeDtypeStruct(buf.shape, buf.dtype), input_output_aliases={1: 0},
        compiler_params=_params(16, ("arbitrary",)),
    )(*_in_hbm(piece, buf))


def _softmax_rows(sc):
    e = jnp.exp(sc - jnp.max(sc, axis=-1, keepdims=True))
    return e / jnp.sum(e, axis=-1, keepdims=True)


def _mix_cross_fwd(x, merged, w_out, gc, w_cq, kv, w_co, s, tm):
    t, d = x.shape
    tm = min(tm, s)
    per = s // tm
    inv = 1.0 / math.sqrt(X_HEAD_DIM)

    def body(x_ref, m_ref, wo_ref, gc_ref, wq_ref, kv_ref, wc_ref, h1_ref, h2_ref, hn_ref, oc_ref):
        h1 = x_ref[...] + _dot(m_ref[...], wo_ref[...])
        h1_ref[...] = h1
        hn = (h1 * _rs(h1) * gc_ref[...]).astype(BF)
        hn_ref[...] = hn
        qc = _dot(hn, wq_ref[...]).astype(BF)
        for h in range(X_HEADS):
            cols = slice(h * X_HEAD_DIM, (h + 1) * X_HEAD_DIM)
            kh = kv_ref[:, h * X_HEAD_DIM:(h + 1) * X_HEAD_DIM]
            vh = kv_ref[:, d + h * X_HEAD_DIM:d + (h + 1) * X_HEAD_DIM]
            p = _softmax_rows(_dot_bt(qc[:, cols], kh) * inv)
            oc_ref[:, cols] = _dot(p.astype(BF), vh).astype(BF)
        h2_ref[...] = h1 + _dot(oc_ref[...], wc_ref[...])

    row = lambda width: pl.BlockSpec((tm, width), lambda i: (i, 0))
    full = lambda a, b: pl.BlockSpec((a, b), lambda i: (0, 0))
    return pl.pallas_call(
        body, name="mix_cross_fwd", grid=(t // tm,),
        in_specs=[row(d), row(d), full(d, d), full(1, d), full(d, d),
                  pl.BlockSpec((N_MEM, 2 * d), lambda i: (i // per, 0)), full(d, d)],
        out_specs=[row(d), row(d), row(d), row(d)],
        out_shape=[jax.ShapeDtypeStruct((t, d), F32), jax.ShapeDtypeStruct((t, d), F32),
                   jax.ShapeDtypeStruct((t, d), BF), jax.ShapeDtypeStruct((t, d), BF)],
        compiler_params=_params(48, ("arbitrary",)),
    )(*_in_hbm(x, merged, w_out, gc, w_cq, kv, w_co))


def _cross_bwd(dh2, h1, gc, w_cq, kv, w_co, s, tm):
    t, d = dh2.shape
    tm = min(tm, s)
    per = s // tm
    nb = t // s
    inv = 1.0 / math.sqrt(X_HEAD_DIM)

    def body(dh2_ref, h1_ref, gc_ref, wq_ref, kv_ref, wc_ref, dh1_ref, dqc_ref, dkv_ref, dgc_ref):
        i = pl.program_id(0)

        @pl.when(i == 0)
        def _():
            dgc_ref[...] = jnp.zeros_like(dgc_ref)

        @pl.when(i % per == 0)
        def _():
            dkv_ref[...] = jnp.zeros_like(dkv_ref)

        dh2 = dh2_ref[...]
        h1 = h1_ref[...]
        r = _rs(h1)
        h1h = h1 * r
        gcv = gc_ref[...]
        hn = (h1h * gcv).astype(BF)
        qc = _dot(hn, wq_ref[...]).astype(BF)
        do = _dot_bt(dh2.astype(BF), wc_ref[...]).astype(BF)
        for h in range(X_HEADS):
            cols = slice(h * X_HEAD_DIM, (h + 1) * X_HEAD_DIM)
            vcols = slice(d + h * X_HEAD_DIM, d + (h + 1) * X_HEAD_DIM)
            kh = kv_ref[:, cols]
            vh = kv_ref[:, vcols]
            p = _softmax_rows(_dot_bt(qc[:, cols], kh) * inv)
            dp = _dot_bt(do[:, cols], vh)
            ds = (p * (dp - jnp.sum(dp * p, axis=-1, keepdims=True)) * inv).astype(BF)
            dqc_ref[:, cols] = _dot(ds, kh).astype(BF)
            dkv_ref[:, cols] += _dot_at(ds, qc[:, cols])
            dkv_ref[:, vcols] += _dot_at(p.astype(BF), do[:, cols])
        dhn = _dot_bt(dqc_ref[...], wq_ref[...])
        dx, dg = _rms_bwd(dhn, h1h, r, gcv)
        dh1_ref[...] = dh2 + dx
        dgc_ref[...] += jnp.sum(dg, axis=0, keepdims=True)

    row = lambda width: pl.BlockSpec((tm, width), lambda i: (i, 0))
    full = lambda a, b: pl.BlockSpec((a, b), lambda i: (0, 0))
    kvspec = pl.BlockSpec((N_MEM, 2 * d), lambda i: (i // per, 0))
    return pl.pallas_call(
        body, name="cross_bwd", grid=(t // tm,),
        in_specs=[row(d), row(d), full(1, d), full(d, d), kvspec, full(d, d)],
        out_specs=[row(d), row(d), kvspec, full(1, d)],
        out_shape=[jax.ShapeDtypeStruct((t, d), F32), jax.ShapeDtypeStruct((t, d), BF),
                   jax.ShapeDtypeStruct((nb * N_MEM, 2 * d), F32), jax.ShapeDtypeStruct((1, d), F32)],
        compiler_params=_params(48, ("arbitrary",)),
    )(*_in_hbm(dh2, h1, gc, w_cq, kv, w_co))


def _mem_bwd(mem, gm, dkv, w_ckv, tm):
    t, d = mem.shape
    tm = min(tm, t)

    def body(mem_ref, dkv_ref, w_ref, dg_ref):
        @pl.when(pl.program_id(0) == 0)
        def _():
            dg_ref[...] = jnp.zeros_like(dg_ref)

        mv = mem_ref[...]
        dmn = _dot_bt(dkv_ref[...].astype(BF), w_ref[...])
        dg_ref[...] += jnp.sum(dmn * (mv * _rs(mv)), axis=0, keepdims=True)

    del gm
    return pl.pallas_call(
        body, name="mem_bwd", grid=(t // tm,),
        in_specs=[pl.BlockSpec((tm, d), lambda i: (i, 0)), pl.BlockSpec((tm, 2 * d), lambda i: (i, 0)),
                  pl.BlockSpec((d, 2 * d), lambda i: (0, 0))],
        out_specs=pl.BlockSpec((1, d), lambda i: (0, 0)),
        out_shape=jax.ShapeDtypeStruct((1, d), F32),
        compiler_params=_params(32, ("arbitrary",)),
    )(*_in_hbm(mem, dkv, w_ckv))


def _ffn_loss_fwd(h2, gf, w1, w2, gl, target, tm):
    t, d = h2.shape
    tm = min(tm, t)

    def body(h2_ref, gf_ref, w1_ref, w2_ref, gl_ref, tg_ref, hn_ref, f_ref, dh3_ref, dgl_ref, loss_ref):
        @pl.when(pl.program_id(0) == 0)
        def _():
            dgl_ref[...] = jnp.zeros_like(dgl_ref)
            loss_ref[...] = jnp.zeros_like(loss_ref)

        h2 = h2_ref[...]
        hn = (h2 * _rs(h2) * gf_ref[...]).astype(BF)
        hn_ref[...] = hn
        h3 = h2
        for c in range(4):
            f = jnp.maximum(_dot(hn, w1_ref[c]), 0.0)
            f_ref[:, c * 1024:(c + 1) * 1024] = f.astype(BF)
            h3 = h3 + _dot((f * f).astype(BF), w2_ref[c])
        r3 = _rs(h3)
        yh = h3 * r3
        glv = gl_ref[...]
        e = yh * glv - tg_ref[...]
        loss_ref[...] += 0.5 * jnp.sum(jnp.sum(e * e, axis=-1, keepdims=True) * (1.0 / d), axis=0, keepdims=True)
        dy = e * (1.0 / d)
        dx, dg = _rms_bwd(dy, yh, r3, glv)
        dh3_ref[...] = dx
        dgl_ref[...] += jnp.sum(dg, axis=0, keepdims=True)

    row = lambda width: pl.BlockSpec((tm, width), lambda i: (i, 0))
    return pl.pallas_call(
        body, name="ffn_loss_fwd", grid=(t // tm,),
        in_specs=[row(d), pl.BlockSpec((1, d), lambda i: (0, 0)), pl.BlockSpec((4, d, 1024), lambda i: (0, 0, 0), pipeline_mode=pl.Buffered(1)),
                  pl.BlockSpec((4, 1024, d), lambda i: (0, 0, 0), pipeline_mode=pl.Buffered(1)),
                  pl.BlockSpec((1, d), lambda i: (0, 0)), row(d)],
        out_specs=[row(d), row(D_FF), row(d), pl.BlockSpec((1, d), lambda i: (0, 0)),
                   pl.BlockSpec((1, 1), lambda i: (0, 0))],
        out_shape=[jax.ShapeDtypeStruct((t, d), BF), jax.ShapeDtypeStruct((t, D_FF), BF),
                   jax.ShapeDtypeStruct((t, d), F32), jax.ShapeDtypeStruct((1, d), F32),
                   jax.ShapeDtypeStruct((1, 1), F32)],
        compiler_params=_params(56, ("arbitrary",)),
    )(*_in_hbm(h2, gf, w1, w2, gl, target))


def _ffn_bwd(dh3, f, h2, gf, w1, w2, tm):
    t, d = h2.shape
    tm = min(tm, t)

    def body(dh3_ref, f_ref, h2_ref, gf_ref, w1_ref, w2_ref, dh2_ref, dpre_ref, dgf_ref):
        @pl.when(pl.program_id(0) == 0)
        def _():
            dgf_ref[...] = jnp.zeros_like(dgf_ref)

        dh3 = dh3_ref[...]
        dh3b = dh3.astype(BF)
        dhn = jnp.zeros((tm, d), F32)
        for c in range(4):
            cols = slice(c * 1024, (c + 1) * 1024)
            dpre = (_dot_bt(dh3b, w2_ref[c]) * (2.0 * f_ref[:, cols].astype(F32))).astype(BF)
            dpre_ref[:, cols] = dpre
            dhn = dhn + _dot_bt(dpre, w1_ref[c])
        h2 = h2_ref[...]
        r = _rs(h2)
        dx, dg = _rms_bwd(dhn, h2 * r, r, gf_ref[...])
        dh2_ref[...] = dh3 + dx
        dgf_ref[...] += jnp.sum(dg, axis=0, keepdims=True)

    row = lambda width: pl.BlockSpec((tm, width), lambda i: (i, 0))
    return pl.pallas_call(
        body, name="ffn_bwd", grid=(t // tm,),
        in_specs=[row(d), row(D_FF), row(d), pl.BlockSpec((1, d), lambda i: (0, 0)),
                  pl.BlockSpec((4, d, 1024), lambda i: (0, 0, 0), pipeline_mode=pl.Buffered(1)),
                  pl.BlockSpec((4, 1024, d), lambda i: (0, 0, 0), pipeline_mode=pl.Buffered(1))],
        out_specs=[row(d), row(D_FF), pl.BlockSpec((1, d), lambda i: (0, 0))],
        out_shape=[jax.ShapeDtypeStruct((t, d), F32), jax.ShapeDtypeStruct((t, D_FF), BF),
                   jax.ShapeDtypeStruct((1, d), F32)],
        compiler_params=_params(56, ("arbitrary",)),
    )(*_in_hbm(dh3, f, h2, gf, w1, w2))


def _in_bwd(dproj, dh1, x, g, w_in, tm, ride=None):
    t, d = x.shape
    n = w_in.shape[1]
    tm = min(tm, t)

    def body(dp_ref, dh1_ref, x_ref, g_ref, w_ref, dx_ref, dg_ref):
        @pl.when(pl.program_id(0) == 0)
        def _():
            dg_ref[...] = jnp.zeros_like(dg_ref)

        dxn = _dot_bt(dp_ref[...], w_ref[...])
        xv = x_ref[...]
        r = _rs(xv)
        dx, dg = _rms_bwd(dxn, xv * r, r, g_ref[...])
        dx_ref[...] = dh1_ref[...] + dx
        dg_ref[...] += jnp.sum(dg, axis=0, keepdims=True)

    row = lambda width: pl.BlockSpec((tm, width), lambda i: (i, 0))
    (dx, dg), rode = _ride_call(
        body, "in_bwd", (t // tm,),
        in_specs=[row(n), row(d), row(d), pl.BlockSpec((1, d), lambda i: (0, 0)),
                  pl.BlockSpec((d, n), lambda i: (0, 0))],
        out_specs=[row(d), pl.BlockSpec((1, d), lambda i: (0, 0))],
        out_shape=[jax.ShapeDtypeStruct((t, d), F32), jax.ShapeDtypeStruct((1, d), F32)],
        scratch_shapes=[], operands=(dproj, dh1, x, g, w_in), vmem_mb=48, ride=ride)
    return dx, dg, rode


class _GradReduce:
    def __init__(self, c_idx):
        self.c_idx = c_idx
        self.sums = {}

    def sibling(self, slabs):
        return _SiblingExchange(slabs)

    def chip(self, names, slabs, recv):
        for k, a, r in zip(names, slabs, recv):
            self.sums[k] = _chip_sum(a, r, self.c_idx, "chip_sum_" + k)
        return _ChipExchange([self.sums[k] for k in names])


def _full_weights(gathered):
    d = D_MODEL
    out = {}
    for k, a in gathered.items():
        if k in ("w_in", "w_ckv", "w_ff1"):
            out[k] = a.transpose(1, 0, 2).reshape(d, -1)
        else:
            out[k] = a.reshape(-1, d)
    return out


def _slabs(a):
    return a.reshape(N_DEV, -1, a.shape[-1])


def _local_step(x, mem, target, small, big, nb, s, tq=256, gather_rest=None, reduce=None):
    d = D_MODEL
    g_mix, g_v, w_sp, b_sp, g_head, g_cross, g_mem, g_ffn, g_fin = (
        small[k] for k in ("norm_mix_g", "gm_v_norm_g", "w_spatial", "b_spatial", "head_norm_g", "norm_cross_g",
                           "norm_mem_g", "norm_ffn_g", "norm_final_g"))
    tri = jnp.tril(jnp.ones((CHUNK, CHUNK), dtype=bool))
    w_sp_m = jnp.where(tri[None], w_sp, 0.0)
    wt = w_sp_m.astype(BF)
    wtt = jnp.swapaxes(w_sp_m, 1, 2).astype(BF)
    bb = jnp.broadcast_to(b_sp[:, :, None], (GM_GROUPS, CHUNK, CHUNK))
    hg_a = g_head[:, :GM_WIDTH]

    proj, xn = _norm_matmul(x, g_mix, big["w_in"], 512, "in_proj")
    merged = _gmlp_fwd(proj, g_v, wt, bb, hg_a, 512)
    o_sb, tot, merged, nblk, gathered = _sb_fwd(proj, merged, g_head, nb, s, tq, ride=gather_rest)
    if gather_rest is not None:
        big = dict(big, **_full_weights(dict(zip(BIG[1:], gathered))))
    w1c = big["w_ff1"].reshape(d, 4, 1024).transpose(1, 0, 2)
    w2c = big["w_ff2"].reshape(4, 1024, d)
    kv, memn = _norm_matmul(mem, g_mem, big["w_ckv"], 512, "mem_proj")
    h1, h2, hn, oc = _mix_cross_fwd(x, merged, big["w_out"], g_cross, big["w_cq"], kv, big["w_co"], s, 512)
    hn2, f, dh3, d_fin, loss = _ffn_loss_fwd(h2, g_ffn, w1c, w2c, g_fin, target, 512)

    gbig = {}
    dh2, dpre, d_ffn = _ffn_bwd(dh3, f, h2, g_ffn, w1c, w2c, 512)
    gbig["w_ff2"] = _slabs(_wgrad(f, dh3, 1024, 1024, "wgrad_ff2", square_a=True))
    gbig["w_ff1"] = _slabs(_wgrad(hn2, dpre, 1024, 1024, "wgrad_ff1", col_shards=4))
    dh1, dqc, dkv, d_cross = _cross_bwd(dh2, h1, g_cross, big["w_cq"], kv, big["w_co"], s, 512)
    gbig["w_co"] = _slabs(_wgrad(oc, dh2, 1024, 1024, "wgrad_co"))
    gbig["w_cq"] = _slabs(_wgrad(hn, dqc, 1024, 1024, "wgrad_cq"))
    gbig["w_ckv"] = _slabs(_wgrad(memn, dkv, 512, 1024, "wgrad_ckv", col_shards=4))
    d_mem = _mem_bwd(mem, g_mem, dkv, big["w_ckv"], 512)
    dmerged = _matmul_bt(dh1, big["w_out"], 512, "out_bwd")
    gbig["w_out"] = _slabs(_wgrad(merged, dh1, 1024, 1024, "wgrad_out"))
    rest = BIG[1:]
    ride = reduce.sibling([gbig[k] for k in rest]) if reduce else None
    dproj, d_wsp, d_bb, d_gv, d_hga, recv = _gmlp_bwd(proj, dmerged, g_v, wt, wtt, bb, hg_a, 512, ride=ride)
    ride = reduce.chip(rest, [gbig[k] for k in rest], recv) if reduce else None
    dproj, dk, dv, d_hgb, parts_rest = _sb_bwd(proj, o_sb, tot, nblk, dmerged, dproj, g_head, nb, s, tq, ride=ride)
    dproj = _place(_place(dproj, dk, 3, "place_dk"), dv, 4, "place_dv")
    gbig["w_in"] = _slabs(_wgrad(xn, dproj, 640, 1024, "wgrad_in", col_shards=4))
    ride = None
    if reduce:
        recv = _run_exchange(reduce.sibling([gbig["w_in"]]), "grad_sibling_exchange_w_in")
        ride = reduce.chip(["w_in"], [gbig["w_in"]], recv)
    grad_x, d_mix, parts_in = _in_bwd(dproj, dh1, x, g_mix, big["w_in"], 512, ride=ride)
    if reduce:
        grad_x = _pass_through(grad_x, "grad_x_pass_through")
    parts = dict(zip(BIG, list(parts_in) + list(parts_rest)))

    gsmall = {
        "norm_mix_g": d_mix, "gm_v_norm_g": d_gv, "w_spatial": d_wsp, "b_spatial": d_bb[:, :, 0],
        "head_norm_g": jnp.concatenate([d_hga, jnp.sum(d_hgb, axis=0)], axis=1), "norm_cross_g": d_cross,
        "norm_mem_g": d_mem, "norm_ffn_g": d_ffn, "norm_final_g": d_fin,
    }
    return loss, grad_x, gsmall, gbig, parts


BIG = ("w_in", "w_out", "w_cq", "w_ckv", "w_co", "w_ff1", "w_ff2")
SMALL = ("norm_mix_g", "gm_v_norm_g", "w_spatial", "b_spatial", "head_norm_g", "norm_cross_g", "norm_mem_g",
         "norm_ffn_g", "norm_final_g")


def _local_copies_start(srcs, stages, sems):
    loads = [pltpu.make_async_copy(src, stage, sems.at[w]) for w, (src, stage) in enumerate(zip(srcs, stages))]
    for ld in loads:
        ld.start()
    return loads


def _local_copies_finish(loads, stages, dsts, sems):
    stores = []
    for w, (ld, stage, dst) in enumerate(zip(loads, stages, dsts)):
        ld.wait()
        st = pltpu.make_async_copy(stage, dst, sems.at[w])
        st.start()
        stores.append(st)
    for st in stores:
        st.wait()


def _chip_sum(slabs, recv, c_idx, name):
    _, r, cw = slabs.shape
    tr = min(r, 256)

    def body(c_ref, a_ref, b_ref, o_ref):
        del c_ref
        o_ref[...] = (a_ref[...] + b_ref[...]).astype(BF)

    return pl.pallas_call(
        body, name=name,
        grid_spec=pltpu.PrefetchScalarGridSpec(
            num_scalar_prefetch=1, grid=(N_CHIPS, r // tr),
            in_specs=[pl.BlockSpec((None, tr, cw), lambda p, i, c_ref: (2 * p + c_ref[0], i, 0)),
                      pl.BlockSpec((None, tr, cw), lambda p, i, c_ref: (p, i, 0))],
            out_specs=pl.BlockSpec((None, tr, cw), lambda p, i, c_ref: (p, i, 0))),
        out_shape=jax.ShapeDtypeStruct((N_CHIPS, r, cw), BF),
        compiler_params=_params(32, ("arbitrary", "arbitrary")),
    )(c_idx, *_in_hbm(slabs, recv))


def _sum4(sums, parts, q_idx, name):
    _, r, cw = parts.shape
    tr = min(r, 256)

    def body(q_ref, own_ref, a_ref, b_ref, c_ref, o_ref):
        del q_ref
        o_ref[...] = ((own_ref[...].astype(F32) + a_ref[...].astype(F32)) + b_ref[...].astype(F32)) + c_ref[
            ...].astype(F32)

    spec = lambda k: pl.BlockSpec((None, tr, cw), lambda i, q_ref: ((q_ref[0] + k) % N_CHIPS, i, 0))
    return pl.pallas_call(
        body, name=name,
        grid_spec=pltpu.PrefetchScalarGridSpec(
            num_scalar_prefetch=1, grid=(r // tr,), in_specs=[spec(0), spec(1), spec(2), spec(3)],
            out_specs=pl.BlockSpec((tr, cw), lambda i, q_ref: (i, 0))),
        out_shape=jax.ShapeDtypeStruct((r, cw), F32),
        compiler_params=_params(32, ("arbitrary",)),
    )(q_idx, *_in_hbm(sums, parts, parts, parts))


def _half_exchange(halves):
    n = len(halves)

    def body(*refs):
        ins, outs, stages = refs[:n], refs[n:2 * n], refs[2 * n:3 * n]
        send_sems, recv_sems, ld_sems, st_sems = refs[3 * n:]
        x, y, c = lax.axis_index("x"), lax.axis_index("y"), lax.axis_index("c")
        loads = _local_copies_start(ins, stages, ld_sems)
        copies = []
        for w in range(n):
            cp = pltpu.make_async_remote_copy(
                src_ref=ins[w], dst_ref=outs[w].at[c], send_sem=send_sems.at[w], recv_sem=recv_sems.at[w],
                device_id=(x, y, 1 - c), device_id_type=MESH)
            cp.start()
            copies.append(cp)
        _local_copies_finish(loads, stages, [outs[w].at[c] for w in range(n)], st_sems)
        for cp in copies:
            cp.wait()

    return pl.pallas_call(
        body, name="grad_half_exchange",
        in_specs=[ANY] * n, out_specs=[ANY] * n,
        out_shape=[jax.ShapeDtypeStruct((2,) + a.shape, a.dtype) for a in halves],
        scratch_shapes=[pltpu.VMEM(a.shape, a.dtype) for a in halves] + [
            pltpu.SemaphoreType.DMA((n,)), pltpu.SemaphoreType.DMA((n,)),
            pltpu.SemaphoreType.DMA((n,)), pltpu.SemaphoreType.DMA((n,))],
        compiler_params=_params(24),
    )(*halves)


def _small_all_reduce(packed, ride=None):
    rows = packed.shape[0]
    ride = ride or _NoExchange()
    ri, ro = len(ride.in_arrays), len(ride.out_shape)

    def body(*refs):
        in_ref, rins, out_ref, routs = refs[0], refs[1:1 + ri], refs[1 + ri], refs[2 + ri:2 + ri + ro]
        pair, chip_sum, chips, d2d_send, d2d_recv, ici_send, ici_recv = refs[2 + ri + ro:9 + ri + ro]
        rscr = refs[9 + ri + ro:]
        ride.start(rins, routs, rscr)
        x, y, c = lax.axis_index("x"), lax.axis_index("y"), lax.axis_index("c")
        q = 2 * x + y
        pair[c] = in_ref[...]
        swap = pltpu.make_async_remote_copy(
            src_ref=in_ref, dst_ref=pair.at[c], send_sem=d2d_send, recv_sem=d2d_recv,
            device_id=(x, y, 1 - c), device_id_type=MESH)
        swap.start()
        swap.wait()
        both = pair[0] + pair[1]
        chip_sum[...] = both
        chips[q] = both
        copies = [pltpu.make_async_remote_copy(
            src_ref=chip_sum, dst_ref=chips.at[q], send_sem=ici_send.at[k], recv_sem=ici_recv.at[k],
            device_id=(px, py, c), device_id_type=MESH) for k, (px, py) in enumerate(_other_chips(x, y))]
        for cp in copies:
            cp.start()
        for cp in copies:
            cp.wait()
        out_ref[...] = ((chips[0] + chips[1]) + chips[2]) + chips[3]
        ride.finish(rins, routs, rscr)

    vmem = pl.BlockSpec(memory_space=pltpu.VMEM)
    res = pl.pallas_call(
        body, name="small_all_reduce",
        in_specs=[vmem] + [ANY] * ri, out_specs=[vmem] + [ANY] * ro,
        out_shape=[jax.ShapeDtypeStruct(packed.shape, F32)] + list(ride.out_shape),
        scratch_shapes=[pltpu.VMEM((2, rows, 128), F32), pltpu.VMEM((rows, 128), F32),
                        pltpu.VMEM((N_CHIPS, rows, 128), F32), pltpu.SemaphoreType.DMA, pltpu.SemaphoreType.DMA,
                        pltpu.SemaphoreType.DMA((3,)), pltpu.SemaphoreType.DMA((3,))] + list(ride.scratch_shapes),
        compiler_params=_params(16),
    )(packed, *ride.in_arrays)
    return res[0], res[1:]


def _adamw(g, w, m, v, name):
    r, cw = g.shape
    tr = 256 if r % 256 == 0 else r

    def body(g_ref, w_ref, m_ref, v_ref, d_ref, nm_ref, nv_ref):
        gv = g_ref[...]
        nm = ADAM_B1 * m_ref[...] + (1.0 - ADAM_B1) * gv
        nv = ADAM_B2 * v_ref[...] + (1.0 - ADAM_B2) * (gv * gv)
        m_hat = nm / (1.0 - ADAM_B1 ** ADAM_STEP)
        v_hat = nv / (1.0 - ADAM_B2 ** ADAM_STEP)
        d_ref[...] = -ADAM_LR * (m_hat / (jnp.sqrt(v_hat) + ADAM_EPS) + ADAM_WD * w_ref[...])
        nm_ref[...] = nm
        nv_ref[...] = nv

    spec = pl.BlockSpec((tr, cw), lambda i: (i, 0))
    return pl.pallas_call(
        body, name=name, grid=(r // tr,),
        in_specs=[spec] * 4, out_specs=[spec] * 3,
        out_shape=[jax.ShapeDtypeStruct((r, cw), F32)] * 3,
        compiler_params=_params(32, ("arbitrary",)),
    )(*_in_hbm(g, w, m, v))


def _small_params(args):
    small = {k: args[k].reshape(1, -1) for k in SMALL}
    small["w_spatial"] = args["w_spatial"][0]
    small["b_spatial"] = args["b_spatial"][0]
    return small


def _pack(parts, rows):
    flat = jnp.concatenate([p.reshape(-1).astype(F32) for p in parts])
    return jnp.pad(flat, (0, rows * 128 - flat.shape[0])).reshape(rows, 128)


def _unpack(packed, shapes):
    flat = packed.reshape(-1)
    out, off = [], 0
    for shp in shapes:
        size = math.prod(shp)
        out.append(flat[off:off + size].reshape(shp))
        off += size
    return out


def kernel(x, mem, norm_mix_g, w_in, gm_v_norm_g, w_spatial, b_spatial, head_norm_g, w_out, norm_cross_g, norm_mem_g, w_cq, w_ckv, w_co, norm_ffn_g, w_ff1, w_ff2, norm_final_g, loss_target, m_norm_mix_g, m_w_in, m_gm_v_norm_g, m_w_spatial, m_b_spatial, m_head_norm_g, m_w_out, m_norm_cross_g, m_norm_mem_g, m_w_cq, m_w_ckv, m_w_co, m_norm_ffn_g, m_w_ff1, m_w_ff2, m_norm_final_g, v_norm_mix_g, v_w_in, v_gm_v_norm_g, v_w_spatial, v_b_spatial, v_head_norm_g, v_w_out, v_norm_cross_g, v_norm_mem_g, v_w_cq, v_w_ckv, v_w_co, v_norm_ffn_g, v_w_ff1, v_w_ff2, v_norm_final_g):
    args = dict(locals())
    d = D_MODEL
    nb, s, _ = x.shape
    c_idx = lax.axis_index("c").astype(jnp.int32).reshape(1)
    q_idx = (2 * lax.axis_index("x") + lax.axis_index("y")).astype(jnp.int32).reshape(1)
    rest = BIG[1:]

    shards = {k: args[k][0].astype(BF) for k in BIG}
    big = _full_weights({"w_in": _run_exchange(_GatherExchange([shards["w_in"]]), "all_gather_w_in")[0]})
    gather_rest = _GatherExchange([shards[k] for k in rest])

    reduce = _GradReduce(c_idx)
    loss, grad_x, gsmall, _, parts = _local_step(
        x.reshape(nb * s, d), mem.reshape(nb * N_MEM, d), loss_target.reshape(nb * s, d), _small_params(args), big,
        nb, s, gather_rest=gather_rest, reduce=reduce)

    shapes = [args[k].shape for k in SMALL]
    n_small = sum(math.prod(sh) for sh in shapes)
    rows = -(-(n_small + 1) // 1024) * 8
    reduced, _ = _small_all_reduce(_pack([gsmall[k] for k in SMALL] + [loss], rows))
    halves = [_sum4(reduce.sums[k], parts[k], q_idx, "sum4_" + k) for k in BIG]
    both = _half_exchange(halves)

    out = {"grad_x": grad_x.reshape(nb, s, d)}
    for k, g2 in zip(BIG, both):
        shp = args[k].shape
        g = g2.reshape(shp[1], shp[2])
        dl, nm, nv = _adamw(g, args[k][0], args["m_" + k][0], args["v_" + k][0], "adamw_" + k)
        out["grad_" + k], out["delta_" + k], out["new_m_" + k], out["new_v_" + k] = (
            a.reshape(shp) for a in (g, dl, nm, nv))

    dl, nm, nv = _adamw(reduced, _pack([args[k] for k in SMALL], rows), _pack([args["m_" + k] for k in SMALL], rows),
                        _pack([args["v_" + k] for k in SMALL], rows), "adamw_small")
    for name, arr in (("grad_", reduced), ("delta_", dl), ("new_m_", nm), ("new_v_", nv)):
        for k, a in zip(SMALL, _unpack(arr, shapes)):
            out[name + k] = a
    out["loss"] = reduced.reshape(-1)[n_small]

    names = ["norm_mix_g", "w_in", "gm_v_norm_g", "w_spatial", "b_spatial", "head_norm_g", "w_out", "norm_cross_g",
             "norm_mem_g", "w_cq", "w_ckv", "w_co", "norm_ffn_g", "w_ff1", "w_ff2", "norm_final_g"]
    return (out["loss"], out["grad_x"], *[out["grad_" + k] for k in names], *[out["delta_" + k] for k in names],
            *[out["new_m_" + k] for k in names], *[out["new_v_" + k] for k in names])
```

```python
import functools
import math

import jax
import jax.numpy as jnp
from jax import lax
from jax.experimental import pallas as pl
from jax.experimental.pallas import tpu as pltpu

F32 = jnp.float32
BF = jnp.bfloat16

EPS = 1e-6
D_MODEL = 1024
CHUNK = 128
GM_GROUPS = 4
GM_WIDTH = 512
SB_WIDTH = 512
HEAD_LANES = 64
SB_SCALE = 0.125
SB_SKIP = -104.0
X_HEADS = 4
X_HEAD_DIM = 256
N_MEM = 256
D_FF = 4096
IN_COLS = 2560
N_CHIPS = 4
N_DEV = 8

ADAM_LR = 0.001
ADAM_B1 = 0.9
ADAM_B2 = 0.999
ADAM_EPS = 1e-08
ADAM_WD = 0.01
ADAM_STEP = 10

V7X_VMEM_BYTES = 64 * 1024 * 1024
MESH = pl.DeviceIdType.MESH
ANY = pl.BlockSpec(memory_space=pl.ANY)

GELU_C = math.sqrt(2.0 / math.pi)
GELU_A = 0.044715


def _params(vmem_mb, sem=None):
    assert vmem_mb * 1024 * 1024 <= V7X_VMEM_BYTES
    return pltpu.CompilerParams(vmem_limit_bytes=vmem_mb * 1024 * 1024, dimension_semantics=sem)


PIN_MIN_ELEMENTS = 1 << 18


def _in_hbm(*arrays):
    return tuple(pltpu.with_memory_space_constraint(a, pltpu.HBM) if a.size >= PIN_MIN_ELEMENTS else a
                 for a in arrays)


def _dot(a, b):
    return jnp.dot(a, b, preferred_element_type=F32)


def _dot_bt(a, b):
    return lax.dot_general(a, b, (((1,), (1,)), ((), ())), preferred_element_type=F32)


def _dot_at(a, b):
    return lax.dot_general(a, b, (((0,), (0,)), ((), ())), preferred_element_type=F32)


def _gelu(x):
    t = jnp.tanh(GELU_C * (x + GELU_A * x * x * x))
    return 0.5 * x * (1.0 + t)


def _gelu_and_grad(x):
    x2 = x * x
    t = jnp.tanh(GELU_C * (x + GELU_A * x2 * x))
    h = 0.5 * (1.0 + t)
    return x * h, h + 0.5 * x * (1.0 - t * t) * (GELU_C * (1.0 + 3.0 * GELU_A * x2))


def _rs(x):
    return lax.rsqrt(jnp.mean(x * x, axis=-1, keepdims=True) + EPS)


def _rms_bwd(dxn, xhat, r, g):
    dxh = dxn * g
    dx = r * (dxh - xhat * jnp.mean(dxh * xhat, axis=-1, keepdims=True))
    return dx, dxn * xhat


def _norm_matmul(x, g, w, tm, name):
    t, d = x.shape
    n = w.shape[1]
    tm = min(tm, t)

    def body(x_ref, g_ref, w_ref, out_ref, xn_ref):
        xv = x_ref[...]
        xn = (xv * _rs(xv) * g_ref[...]).astype(BF)
        xn_ref[...] = xn
        out_ref[...] = _dot(xn, w_ref[...]).astype(out_ref.dtype)

    return pl.pallas_call(
        body, name=name, grid=(t // tm,),
        in_specs=[pl.BlockSpec((tm, d), lambda i: (i, 0)), pl.BlockSpec((1, d), lambda i: (0, 0)),
                  pl.BlockSpec((d, n), lambda i: (0, 0))],
        out_specs=[pl.BlockSpec((tm, n), lambda i: (i, 0)), pl.BlockSpec((tm, d), lambda i: (i, 0))],
        out_shape=[jax.ShapeDtypeStruct((t, n), BF), jax.ShapeDtypeStruct((t, d), BF)],
        compiler_params=_params(48, ("arbitrary",)),
    )(*_in_hbm(x, g, w))


def _wgrad(a, g, tn, tk, name, square_a=False, col_shards=1):
    t, m = a.shape
    n = g.shape[1]
    tk = min(tk, t)
    tm = min(m, 1024)
    ns = n // col_shards
    assert ns % tn == 0 and m % tm == 0
    per = ns // tn
    nk = t // tk

    def body(a_ref, g_ref, o_ref):
        k = pl.program_id(2)

        @pl.when(k == 0)
        def _():
            o_ref[...] = jnp.zeros_like(o_ref)

        av = a_ref[...]
        if square_a:
            af = av.astype(F32)
            av = af * af
        o_ref[...] += _dot_at(av.astype(BF), g_ref[...].astype(BF))

    return pl.pallas_call(
        body, name=name, grid=(m // tm, n // tn, nk),
        in_specs=[pl.BlockSpec((tk, tm), lambda i, j, k: (k, i)), pl.BlockSpec((tk, tn), lambda i, j, k: (k, j))],
        out_specs=pl.BlockSpec((None, tm, tn), lambda i, j, k: (j // per, i, j % per)),
        out_shape=jax.ShapeDtypeStruct((col_shards, m, ns), F32),
        compiler_params=_params(48, ("arbitrary", "arbitrary", "arbitrary")),
    )(*_in_hbm(a, g))


def _wgrad_wide(a, g, tm, tk, name, col_shards):
    t, m = a.shape
    n = g.shape[1]
    tk = min(tk, t)
    tm = min(tm, m)
    ns = n // col_shards

    def body(a_ref, g_ref, o_ref):
        @pl.when(pl.program_id(1) == 0)
        def _():
            o_ref[...] = jnp.zeros_like(o_ref)

        a_t = a_ref[...].astype(BF).T
        for p in range(col_shards):
            o_ref[p] += _dot(a_t, g_ref[:, p * ns:(p + 1) * ns].astype(BF))

    return pl.pallas_call(
        body, name=name, grid=(m // tm, t // tk),
        in_specs=[pl.BlockSpec((tk, tm), lambda i, k: (k, i)), pl.BlockSpec((tk, n), lambda i, k: (k, 0))],
        out_specs=pl.BlockSpec((col_shards, tm, ns), lambda i, k: (0, i, 0)),
        out_shape=jax.ShapeDtypeStruct((col_shards, m, ns), F32),
        compiler_params=_params(48, ("arbitrary", "arbitrary")),
    )(*_in_hbm(a, g))


def _matmul_bt(a, w, tm, name):
    t, n = a.shape
    k = w.shape[0]
    tm = min(tm, t)

    def body(a_ref, w_ref, o_ref):
        o_ref[...] = _dot_bt(a_ref[...].astype(BF), w_ref[...]).astype(o_ref.dtype)

    return pl.pallas_call(
        body, name=name, grid=(t // tm,),
        in_specs=[pl.BlockSpec((tm, n), lambda i: (i, 0)), pl.BlockSpec((k, n), lambda i: (0, 0))],
        out_specs=pl.BlockSpec((tm, k), lambda i: (i, 0)),
        out_shape=jax.ShapeDtypeStruct((t, k), BF),
        compiler_params=_params(32, ("arbitrary",)),
    )(*_in_hbm(a, w))


def _gmlp_fwd(proj, gg, wt, bb, hg, tm):
    t = proj.shape[0]
    tm = min(tm, t)

    def body(u_ref, v_ref, gg_ref, wt_ref, bb_ref, hg_ref, out_ref):
        for cc in range(tm // CHUNK):
            rows = slice(cc * CHUNK, (cc + 1) * CHUNK)
            for g in range(GM_GROUPS):
                cols = slice(g * 128, (g + 1) * 128)
                u = _gelu(u_ref[rows, cols].astype(F32))
                gv = _gelu(v_ref[rows, cols].astype(F32))
                vn = gv * _rs(gv) * gg_ref[:, cols]
                mixed = _dot(wt_ref[g], vn.astype(BF)) + bb_ref[g]
                a = u * mixed
                out_ref[rows, cols] = (a * _rs(a) * hg_ref[:, cols]).astype(BF)

    return pl.pallas_call(
        body, name="gmlp_fwd", grid=(t // tm,),
        in_specs=[pl.BlockSpec((tm, 512), lambda i: (i, 0)), pl.BlockSpec((tm, 512), lambda i: (i, 1)),
                  pl.BlockSpec((1, 512), lambda i: (0, 0)), pl.BlockSpec((4, 128, 128), lambda i: (0, 0, 0)),
                  pl.BlockSpec((4, 128, 128), lambda i: (0, 0, 0)), pl.BlockSpec((1, 512), lambda i: (0, 0))],
        out_specs=pl.BlockSpec((tm, 512), lambda i: (i, 0)),
        out_shape=jax.ShapeDtypeStruct((t, 1024), BF),
        compiler_params=_params(32, ("arbitrary",)),
    )(*_in_hbm(proj, proj, gg, wt, bb, hg))


def _gmlp_bwd(proj, dmerged, gg, wt, wtt, bb, hg, tm, ride=None):
    t = proj.shape[0]
    tm = min(tm, t)
    nsteps = t // tm

    def body(u_ref, v_ref, dm_ref, gg_ref, wt_ref, wtt_ref, bb_ref, hg_ref,
             dp_ref, dw_ref, db_ref, dgg_ref, dhg_ref):
        i = pl.program_id(0)

        @pl.when(i == 0)
        def _():
            dw_ref[...] = jnp.zeros_like(dw_ref)
            db_ref[...] = jnp.zeros_like(db_ref)
            dgg_ref[...] = jnp.zeros_like(dgg_ref)
            dhg_ref[...] = jnp.zeros_like(dhg_ref)

        for cc in range(tm // CHUNK):
            rows = slice(cc * CHUNK, (cc + 1) * CHUNK)
            for g in range(GM_GROUPS):
                cols = slice(g * 128, (g + 1) * 128)
                up = u_ref[rows, cols].astype(F32)
                gp = v_ref[rows, cols].astype(F32)
                u, u_grad = _gelu_and_grad(up)
                gv, gv_grad = _gelu_and_grad(gp)
                rv = _rs(gv)
                gvh = gv * rv
                ggv = gg_ref[:, cols]
                vnb = (gvh * ggv).astype(BF)
                mixed = _dot(wt_ref[g], vnb) + bb_ref[g]
                a = u * mixed
                ra = _rs(a)
                ah = a * ra
                dm = dm_ref[rows, cols].astype(F32)
                dhg_ref[:, cols] += jnp.sum(dm * ah, axis=0, keepdims=True)
                dah = dm * hg_ref[:, cols]
                da = ra * (dah - ah * jnp.mean(dah * ah, axis=-1, keepdims=True))
                du = da * mixed
                dmix = da * u
                db_ref[g] += dmix
                dmb = dmix.astype(BF)
                dw_ref[g] += _dot_bt(dmb, vnb)
                dvn = _dot(wtt_ref[g], dmb)
                dgg_ref[:, cols] += jnp.sum(dvn * gvh, axis=0, keepdims=True)
                dgh = dvn * ggv
                dgv = rv * (dgh - gvh * jnp.mean(dgh * gvh, axis=-1, keepdims=True))
                dp_ref[rows, cols] = (du * u_grad).astype(BF)
                dp_ref[rows, 512 + g * 128:512 + (g + 1) * 128] = (dgv * gv_grad).astype(BF)

        @pl.when(i == nsteps - 1)
        def _():
            r = lax.broadcasted_iota(jnp.int32, (CHUNK, CHUNK), 0)
            c = lax.broadcasted_iota(jnp.int32, (CHUNK, CHUNK), 1)
            for g in range(GM_GROUPS):
                dw_ref[g] = jnp.where(c <= r, dw_ref[g], 0.0)
                db_ref[g] = jnp.broadcast_to(jnp.sum(db_ref[g], axis=-1, keepdims=True), (CHUNK, CHUNK))

    small = lambda shape: pl.BlockSpec(shape, lambda i: (0,) * len(shape))
    res, rode = _ride_call(
        body, "gmlp_bwd", (nsteps,),
        in_specs=[pl.BlockSpec((tm, 512), lambda i: (i, 0)), pl.BlockSpec((tm, 512), lambda i: (i, 1)),
                  pl.BlockSpec((tm, 512), lambda i: (i, 0)), small((1, 512)), small((4, 128, 128)),
                  small((4, 128, 128)), small((4, 128, 128)), small((1, 512))],
        out_specs=[pl.BlockSpec((tm, 1024), lambda i: (i, 0)), small((4, 128, 128)), small((4, 128, 128)),
                   small((1, 512)), small((1, 512))],
        out_shape=[jax.ShapeDtypeStruct((t, IN_COLS), BF), jax.ShapeDtypeStruct((4, 128, 128), F32),
                   jax.ShapeDtypeStruct((4, 128, 128), F32), jax.ShapeDtypeStruct((1, 512), F32),
                   jax.ShapeDtypeStruct((1, 512), F32)],
        scratch_shapes=[], operands=(proj, proj, dmerged, gg, wt, wtt, bb, hg), vmem_mb=32, ride=ride)
    return (*res, rode)


def _other_chips(x, y):
    return ((1 - x, y), (x, 1 - y), (1 - x, 1 - y))


class _GatherExchange:
    def __init__(self, shards):
        n = len(shards)
        self.n = n
        self.in_arrays = list(shards)
        self.out_shape = [jax.ShapeDtypeStruct((N_CHIPS,) + a.shape, a.dtype) for a in shards]
        self.half_rows = [a.shape[0] // 2 for a in shards]
        sems = lambda k: pltpu.SemaphoreType.DMA((k,))
        self.scratch_shapes = [pltpu.VMEM(a.shape, a.dtype) for a in shards] + [
            sems(3 * n), sems(3 * n), sems(3 * n), sems(3 * n), sems(n), sems(n)]

    def _copies(self, ins, outs, scr):
        n = self.n
        stages, (ici_send, ici_recv, d2d_send, d2d_recv, ld_sems, st_sems) = scr[:n], scr[n:]
        x, y, c = lax.axis_index("x"), lax.axis_index("y"), lax.axis_index("c")
        q = 2 * x + y
        loads = [pltpu.make_async_copy(ins[w], stages[w], ld_sems.at[w]) for w in range(n)]
        stores = [pltpu.make_async_copy(stages[w], outs[w].at[q], st_sems.at[w]) for w in range(n)]
        ici, d2d = [], []
        for w in range(n):
            half = pl.ds(c * self.half_rows[w], self.half_rows[w])
            for k, (px, py) in enumerate(_other_chips(x, y)):
                ici.append(pltpu.make_async_remote_copy(
                    src_ref=ins[w].at[half], dst_ref=outs[w].at[q, half], send_sem=ici_send.at[3 * w + k],
                    recv_sem=ici_recv.at[3 * w + k], device_id=(px, py, c), device_id_type=MESH))
                landed = outs[w].at[2 * px + py, half]
                d2d.append(pltpu.make_async_remote_copy(
                    src_ref=landed, dst_ref=landed, send_sem=d2d_send.at[3 * w + k],
                    recv_sem=d2d_recv.at[3 * w + k], device_id=(x, y, 1 - c), device_id_type=MESH))
        return loads, stores, ici, d2d

    def start(self, ins, outs, scr):
        loads, stores, ici, _ = self._copies(ins, outs, scr)
        for cp in loads + ici:
            cp.start()
        for ld, st in zip(loads, stores):
            ld.wait()
            st.start()

    def relay(self, ins, outs, scr):
        _, _, ici, d2d = self._copies(ins, outs, scr)
        for got, fwd in zip(ici, d2d):
            got.wait_recv()
            fwd.start()

    def finish(self, ins, outs, scr):
        _, stores, ici, d2d = self._copies(ins, outs, scr)
        for cp in ici:
            cp.wait_send()
        for cp in d2d + stores:
            cp.wait()


class _SiblingExchange:
    def __init__(self, slabs):
        n = len(slabs)
        self.n = n
        self.in_arrays = list(slabs)
        self.out_shape = [jax.ShapeDtypeStruct((N_CHIPS,) + a.shape[1:], a.dtype) for a in slabs]
        self.scratch_shapes = [pltpu.SemaphoreType.DMA((4 * n,)), pltpu.SemaphoreType.DMA((4 * n,))]

    def _copies(self, ins, outs, scr):
        send_sems, recv_sems = scr
        x, y, c = lax.axis_index("x"), lax.axis_index("y"), lax.axis_index("c")
        return [pltpu.make_async_remote_copy(
            src_ref=ins[w].at[2 * p + (1 - c)], dst_ref=outs[w].at[p], send_sem=send_sems.at[4 * w + p],
            recv_sem=recv_sems.at[4 * w + p], device_id=(x, y, 1 - c), device_id_type=MESH)
            for w in range(self.n) for p in range(N_CHIPS)]

    def start(self, ins, outs, scr):
        for cp in self._copies(ins, outs, scr):
            cp.start()

    def finish(self, ins, outs, scr):
        for cp in self._copies(ins, outs, scr):
            cp.wait()


class _ChipExchange:
    def __init__(self, sums):
        n = len(sums)
        self.n = n
        self.in_arrays = list(sums)
        self.out_shape = [jax.ShapeDtypeStruct(a.shape, a.dtype) for a in sums]
        self.scratch_shapes = [pltpu.SemaphoreType.DMA((3 * n,)), pltpu.SemaphoreType.DMA((3 * n,))]

    def _copies(self, ins, outs, scr):
        send_sems, recv_sems = scr
        x, y, c = lax.axis_index("x"), lax.axis_index("y"), lax.axis_index("c")
        q = 2 * x + y
        return [pltpu.make_async_remote_copy(
            src_ref=ins[w].at[2 * px + py], dst_ref=outs[w].at[q], send_sem=send_sems.at[3 * w + k],
            recv_sem=recv_sems.at[3 * w + k], device_id=(px, py, c), device_id_type=MESH)
            for w in range(self.n) for k, (px, py) in enumerate(_other_chips(x, y))]

    def start(self, ins, outs, scr):
        for cp in self._copies(ins, outs, scr):
            cp.start()

    def finish(self, ins, outs, scr):
        for cp in self._copies(ins, outs, scr):
            cp.wait()


class _NoExchange:
    in_arrays, out_shape, scratch_shapes = (), (), ()

    def start(self, ins, outs, scr):
        pass

    def finish(self, ins, outs, scr):
        pass


def _run_exchange(ex, name):
    n_in, n_out = len(ex.in_arrays), len(ex.out_shape)

    def body(*refs):
        ins, outs, scr = refs[:n_in], refs[n_in:n_in + n_out], refs[n_in + n_out:]
        ex.start(ins, outs, scr)
        if hasattr(ex, "relay"):
            ex.relay(ins, outs, scr)
        ex.finish(ins, outs, scr)

    return pl.pallas_call(
        body, name=name, in_specs=[ANY] * n_in, out_specs=[ANY] * n_out, out_shape=ex.out_shape,
        scratch_shapes=ex.scratch_shapes, compiler_params=_params(24),
    )(*ex.in_arrays)


def _ride_call(body, name, grid, in_specs, out_specs, out_shape, scratch_shapes, operands, vmem_mb, ride=None,
               aliases=None):
    ride = ride or _NoExchange()
    ni, no, ns = len(in_specs), len(out_specs), len(scratch_shapes)
    ri, ro = len(ride.in_arrays), len(ride.out_shape)
    total = math.prod(grid)

    def wrapped(*refs):
        ins, rins = refs[:ni], refs[ni:ni + ri]
        outs, routs = refs[ni + ri:ni + ri + no], refs[ni + ri + no:ni + ri + no + ro]
        scr, rscr = refs[ni + ri + no + ro:ni + ri + no + ro + ns], refs[ni + ri + no + ro + ns:]
        step = pl.program_id(0)
        for ax in range(1, len(grid)):
            step = step * grid[ax] + pl.program_id(ax)

        @pl.when(step == 0)
        def _():
            ride.start(rins, routs, rscr)

        if hasattr(ride, "relay"):
            @pl.when(step == (3 * total) // 4)
            def _():
                ride.relay(rins, routs, rscr)

        body(*ins, *outs, *scr)

        @pl.when(step == total - 1)
        def _():
            ride.finish(rins, routs, rscr)

    res = pl.pallas_call(
        wrapped, name=name, grid=grid, in_specs=list(in_specs) + [ANY] * ri, out_specs=list(out_specs) + [ANY] * ro,
        out_shape=list(out_shape) + list(ride.out_shape),
        scratch_shapes=list(scratch_shapes) + list(ride.scratch_shapes), input_output_aliases=aliases or {},
        compiler_params=_params(vmem_mb, ("arbitrary",) * len(grid)),
    )(*_in_hbm(*operands), *ride.in_arrays)
    return res[:no], res[no:]


def _neg_log_sig(z):
    n = jnp.maximum(z, 0.0) + jnp.log(1.0 + jnp.exp(-jnp.abs(z)))
    return n, z - n


def _running_sums(n, tri2):
    hi = n.astype(BF)
    lo = (n - hi.astype(F32)).astype(BF)
    return _dot(jnp.concatenate([hi, lo], axis=1), tri2)


def _head_sums(x, h0):
    s0 = jnp.sum(jnp.where(h0, x, 0.0), axis=-1, keepdims=True)
    s1 = jnp.sum(jnp.where(h0, 0.0, x), axis=-1, keepdims=True)
    return jnp.where(h0, s0, s1)


def _sb_setup(q_ref, tq):
    lane = lax.broadcasted_iota(jnp.int32, (tq, 128), 1)
    h0 = lane < HEAD_LANES
    qs = q_ref[...] * SB_SCALE
    zero = jnp.zeros_like(qs)
    qst = jnp.concatenate([jnp.where(h0, qs, zero), jnp.where(h0, zero, qs)], axis=0)
    r = lax.broadcasted_iota(jnp.int32, (2 * tq, tq), 0)
    c = lax.broadcasted_iota(jnp.int32, (2 * tq, tq), 1)
    causal = c < jnp.where(r >= tq, r - tq, r)
    return h0, qst, causal


def _tri(tq, op):
    return op(lax.broadcasted_iota(jnp.int32, (tq, tq), 0), lax.broadcasted_iota(jnp.int32, (tq, tq), 1)).astype(BF)


def _sb_fwd(proj, merged, hg, nb, s, tq, ride=None):
    t = nb * s
    tq = min(tq, s)
    nq = s // tq

    def body(q_ref, k_ref, v_ref, hg_ref, merged_ref, o_ref, tot_ref, mb_ref, nblk_ref, acc, cr):
        del merged_ref
        i = pl.program_id(2)
        h0, qst, causal = _sb_setup(q_ref, tq)
        tri_gt = _tri(tq, lambda r, c: r > c)
        tri_gt = jnp.concatenate([tri_gt, tri_gt], axis=0)

        def block(j, masked, c_in):
            start = pl.multiple_of(j * tq, tq)
            kj = k_ref[pl.ds(start, tq), :]
            vj = v_ref[pl.ds(start, tq), :]
            n, l = _neg_log_sig(_dot_bt(qst, kj))
            if masked:
                n = jnp.where(causal, n, 0.0)
            a = jnp.exp(l - (_running_sums(n, tri_gt) + c_in))
            if masked:
                a = jnp.where(causal, a, 0.0)
            return _dot(a.astype(BF), vj), c_in + jnp.sum(n, axis=-1, keepdims=True)

        @pl.when(i == 0)
        def _():
            acc[...], cr[...] = block(0, True, jnp.zeros((2 * tq, 1), F32))

        @pl.when(i > 0)
        def _():
            p_diag, c_diag = block(i, True, jnp.zeros((2 * tq, 1), F32))
            p_prev, c_prev = block(i - 1, False, c_diag)
            acc[...] = p_diag + p_prev
            cr[...] = c_prev

        def cond(carry):
            return jnp.logical_and(carry[0] < i, carry[1] < -SB_SKIP)

        def step(carry):
            p, c_new = block(i - 1 - carry[0], False, cr[...])
            acc[...] += p
            cr[...] = c_new
            return carry[0] + 1, jnp.min(c_new)

        walked, _ = lax.while_loop(cond, step, (jnp.minimum(i, 1), jnp.min(cr[...])))

        o = jnp.where(h0, acc[0:tq, :], acc[tq:2 * tq, :])
        o_ref[...] = o
        tot_ref[...] = jnp.where(h0, cr[0:tq, :], cr[tq:2 * tq, :])
        ro = lax.rsqrt(_head_sums(o * o, h0) * (1.0 / HEAD_LANES) + EPS)
        mb_ref[...] = (o * ro * hg_ref[...]).astype(BF)
        nblk_ref[...] = jnp.full((8, 128), walked.astype(F32))

    blk = lambda col0: pl.BlockSpec((tq, 128), lambda b, hp, i: (b * nq + i, col0 + hp))
    seq = lambda col0: pl.BlockSpec((s, 128), lambda b, hp, i: (b, col0 + hp))
    (o, tot, mb, nblk), rode = _ride_call(
        body, "sb_fwd", (nb, 4, nq),
        in_specs=[blk(8), seq(12), seq(16), pl.BlockSpec((1, 128), lambda b, hp, i: (0, 4 + hp)), ANY],
        out_specs=[blk(0), blk(0), blk(4), pl.BlockSpec((None, None, 8, 128), lambda b, hp, i: (b, hp, i, 0))],
        out_shape=[jax.ShapeDtypeStruct((t, 512), F32), jax.ShapeDtypeStruct((t, 512), F32),
                   jax.ShapeDtypeStruct((t, 1024), BF), jax.ShapeDtypeStruct((nb, 4, nq * 8, 128), F32)],
        scratch_shapes=[pltpu.VMEM((2 * tq, 128), F32), pltpu.VMEM((2 * tq, 1), F32)],
        operands=(proj, proj, proj, hg, merged), vmem_mb=40, ride=ride, aliases={4: 2})
    return o, tot, mb, nblk, rode


def _sb_bwd(proj, o_sb, tot, nblk, dmerged, dproj, hg, nb, s, tq, ride=None):
    t = nb * s
    tq = min(tq, s)
    nq = s // tq

    def body(q_ref, k_ref, v_ref, o_ref, tot_ref, nblk_ref, dm_ref, hg_ref, dproj_ref,
             dq_ref, dk_ref, dv_ref, dhg_ref, dk_acc, dv_acc, dq_acc, cm, cg):
        del dproj_ref
        i = pl.program_id(2)
        h0, qst, causal = _sb_setup(q_ref, tq)
        tri_le = _tri(tq, lambda r, c: r <= c)
        tri_le = jnp.concatenate([tri_le, tri_le], axis=0)
        tri_lt = _tri(tq, lambda r, c: r < c)

        @pl.when(i == 0)
        def _():
            dk_acc[...] = jnp.zeros_like(dk_acc)
            dv_acc[...] = jnp.zeros_like(dv_acc)
            dhg_ref[...] = jnp.zeros_like(dhg_ref)

        for ref in (dq_acc, cm, cg):
            ref[...] = jnp.zeros_like(ref)

        o = o_ref[...]
        ro = lax.rsqrt(_head_sums(o * o, h0) * (1.0 / HEAD_LANES) + EPS)
        oh = o * ro
        dm = dm_ref[...].astype(F32)
        dhg_ref[...] += jnp.sum(dm * oh, axis=0, keepdims=True)
        doh = dm * hg_ref[...]
        do = (ro * (doh - oh * (_head_sums(doh * oh, h0) * (1.0 / HEAD_LANES)))).astype(BF)
        zb = jnp.zeros_like(do)
        dost = jnp.concatenate([jnp.where(h0, do, zb), jnp.where(h0, zb, do)], axis=0)
        tots = jnp.concatenate([tot_ref[:, 0:1], tot_ref[:, HEAD_LANES:HEAD_LANES + 1]], axis=0)
        qst_t = qst.T
        dost_t = dost.T

        def block(j, masked, cm_in, cg_in):
            start = pl.multiple_of(j * tq, tq)
            kj = k_ref[pl.ds(start, tq), :]
            vj = v_ref[pl.ds(start, tq), :]
            n, l = _neg_log_sig(_dot_bt(qst, kj))
            if masked:
                n = jnp.where(causal, n, 0.0)
            a = jnp.exp(l - (tots - cm_in - _running_sums(n, tri_le)))
            if masked:
                a = jnp.where(causal, a, 0.0)
            gm = a * _dot_bt(dost, vj)
            pp = cg_in + _dot(gm.astype(BF), tri_lt)
            dz = gm - jnp.exp(l) * (gm + pp)
            if masked:
                dz = jnp.where(causal, dz, 0.0)
            dzb = dz.astype(BF)
            dk_acc[:, pl.ds(start, tq)] += _dot(qst_t, dzb)
            dv_acc[:, pl.ds(start, tq)] += _dot(dost_t, a.astype(BF))
            return (_dot(dzb, kj), cm_in + jnp.sum(n, axis=-1, keepdims=True),
                    cg_in + jnp.sum(gm, axis=-1, keepdims=True))

        def step(j, carry):
            dq, cm[...], cg[...] = block(j, False, cm[...], cg[...])
            dq_acc[...] += dq
            return carry

        walked = jnp.clip(jnp.max(nblk_ref[...]).astype(jnp.int32), jnp.minimum(i, 1), i)
        lax.fori_loop(i - walked, i - 1, step, 0)

        @pl.when(i == 0)
        def _():
            dq_acc[...] = block(0, True, cm[...], cg[...])[0]

        @pl.when(i > 0)
        def _():
            dq_prev, cm_prev, cg_prev = block(i - 1, False, cm[...], cg[...])
            dq_acc[...] += dq_prev + block(i, True, cm_prev, cg_prev)[0]

        dq_ref[...] = (jnp.where(h0, dq_acc[0:tq, :], dq_acc[tq:2 * tq, :]) * SB_SCALE).astype(BF)

        @pl.when(i == nq - 1)
        def _():
            dk_ref[...] = dk_acc[...].T.astype(BF)
            dv_ref[...] = dv_acc[...].T.astype(BF)

    blk = lambda col0: pl.BlockSpec((tq, 128), lambda b, hp, i: (b * nq + i, col0 + hp))
    seq = lambda col0: pl.BlockSpec((s, 128), lambda b, hp, i: (b, col0 + hp))
    (dq, dk, dv, dhg), rode = _ride_call(
        body, "sb_bwd", (nb, 4, nq),
        in_specs=[blk(8), seq(12), seq(16), blk(0), blk(0),
                  pl.BlockSpec((None, None, 8, 128), lambda b, hp, i: (b, hp, i, 0)), blk(4),
                  pl.BlockSpec((1, 128), lambda b, hp, i: (0, 4 + hp)), ANY],
        out_specs=[blk(8), seq(0), seq(0), pl.BlockSpec((None, 1, 128), lambda b, hp, i: (b, 0, hp))],
        out_shape=[jax.ShapeDtypeStruct((t, IN_COLS), BF), jax.ShapeDtypeStruct((t, 512), BF),
                   jax.ShapeDtypeStruct((t, 512), BF), jax.ShapeDtypeStruct((nb, 1, 512), F32)],
        scratch_shapes=[pltpu.VMEM((128, s), F32), pltpu.VMEM((128, s), F32), pltpu.VMEM((2 * tq, 128), F32),
                        pltpu.VMEM((2 * tq, 1), F32), pltpu.VMEM((2 * tq, 1), F32)],
        operands=(proj, proj, proj, o_sb, tot, nblk, dmerged, hg, dproj), vmem_mb=40, ride=ride, aliases={8: 0})
    return dq, dk, dv, dhg, rode


def _place(buf, piece, col_block, name):
    t, w = piece.shape
    tm = min(t, 1024)

    def body(piece_ref, buf_ref, out_ref):
        del buf_ref
        out_ref[...] = piece_ref[...]

    return pl.pallas_call(
        body, name=name, grid=(t // tm,),
        in_specs=[pl.BlockSpec((tm, w), lambda i: (i, 0)), ANY],
        out_specs=pl.BlockSpec((tm, w), lambda i: (i, col_block)),
        out_shape=jax.ShapeDtypeStruct(buf.shape, buf.dtype), input_output_aliases={1: 0},
        compiler_params=_params(16, ("arbitrary",)),
    )(piece, buf)


def _softmax_rows(sc):
    e = jnp.exp(sc - jnp.max(sc, axis=-1, keepdims=True))
    return e / jnp.sum(e, axis=-1, keepdims=True)


def _mix_cross_fwd(x, merged, w_out, gc, w_cq, kv, w_co, s, tm):
    t, d = x.shape
    tm = min(tm, s)
    per = s // tm
    inv = 1.0 / math.sqrt(X_HEAD_DIM)

    def body(x_ref, m_ref, wo_ref, gc_ref, wq_ref, kv_ref, wc_ref, h1_ref, h2_ref, hn_ref, oc_ref):
        h1 = x_ref[...] + _dot(m_ref[...], wo_ref[...])
        h1_ref[...] = h1
        hn = (h1 * _rs(h1) * gc_ref[...]).astype(BF)
        hn_ref[...] = hn
        qc = _dot(hn, wq_ref[...]).astype(BF)
        for h in range(X_HEADS):
            cols = slice(h * X_HEAD_DIM, (h + 1) * X_HEAD_DIM)
            kh = kv_ref[:, h * X_HEAD_DIM:(h + 1) * X_HEAD_DIM]
            vh = kv_ref[:, d + h * X_HEAD_DIM:d + (h + 1) * X_HEAD_DIM]
            p = _softmax_rows(_dot_bt(qc[:, cols], kh) * inv)
            oc_ref[:, cols] = _dot(p.astype(BF), vh).astype(BF)
        h2_ref[...] = h1 + _dot(oc_ref[...], wc_ref[...])

    row = lambda width: pl.BlockSpec((tm, width), lambda i: (i, 0))
    full = lambda a, b: pl.BlockSpec((a, b), lambda i: (0, 0))
    return pl.pallas_call(
        body, name="mix_cross_fwd", grid=(t // tm,),
        in_specs=[row(d), row(d), full(d, d), full(1, d), full(d, d),
                  pl.BlockSpec((N_MEM, 2 * d), lambda i: (i // per, 0)), full(d, d)],
        out_specs=[row(d), row(d), row(d), row(d)],
        out_shape=[jax.ShapeDtypeStruct((t, d), F32), jax.ShapeDtypeStruct((t, d), F32),
                   jax.ShapeDtypeStruct((t, d), BF), jax.ShapeDtypeStruct((t, d), BF)],
        compiler_params=_params(48, ("arbitrary",)),
    )(*_in_hbm(x, merged, w_out, gc, w_cq, kv, w_co))


def _cross_bwd(dh2, h1, gc, w_cq, kv, w_co, s, tm):
    t, d = dh2.shape
    tm = min(tm, s)
    per = s // tm
    nb = t // s
    inv = 1.0 / math.sqrt(X_HEAD_DIM)

    def body(dh2_ref, h1_ref, gc_ref, wq_ref, kv_ref, wc_ref, dh1_ref, dqc_ref, dkv_ref, dgc_ref):
        i = pl.program_id(0)

        @pl.when(i == 0)
        def _():
            dgc_ref[...] = jnp.zeros_like(dgc_ref)

        @pl.when(i % per == 0)
        def _():
            dkv_ref[...] = jnp.zeros_like(dkv_ref)

        dh2 = dh2_ref[...]
        h1 = h1_ref[...]
        r = _rs(h1)
        h1h = h1 * r
        gcv = gc_ref[...]
        hn = (h1h * gcv).astype(BF)
        qc = _dot(hn, wq_ref[...]).astype(BF)
        do = _dot_bt(dh2.astype(BF), wc_ref[...]).astype(BF)
        for h in range(X_HEADS):
            cols = slice(h * X_HEAD_DIM, (h + 1) * X_HEAD_DIM)
            vcols = slice(d + h * X_HEAD_DIM, d + (h + 1) * X_HEAD_DIM)
            kh = kv_ref[:, cols]
            vh = kv_ref[:, vcols]
            p = _softmax_rows(_dot_bt(qc[:, cols], kh) * inv)
            dp = _dot_bt(do[:, cols], vh)
            ds = (p * (dp - jnp.sum(dp * p, axis=-1, keepdims=True)) * inv).astype(BF)
            dqc_ref[:, cols] = _dot(ds, kh).astype(BF)
            dkv_ref[:, cols] += _dot_at(ds, qc[:, cols])
            dkv_ref[:, vcols] += _dot_at(p.astype(BF), do[:, cols])
        dhn = _dot_bt(dqc_ref[...], wq_ref[...])
        dx, dg = _rms_bwd(dhn, h1h, r, gcv)
        dh1_ref[...] = dh2 + dx
        dgc_ref[...] += jnp.sum(dg, axis=0, keepdims=True)

    row = lambda width: pl.BlockSpec((tm, width), lambda i: (i, 0))
    full = lambda a, b: pl.BlockSpec((a, b), lambda i: (0, 0))
    kvspec = pl.BlockSpec((N_MEM, 2 * d), lambda i: (i // per, 0))
    return pl.pallas_call(
        body, name="cross_bwd", grid=(t // tm,),
        in_specs=[row(d), row(d), full(1, d), full(d, d), kvspec, full(d, d)],
        out_specs=[row(d), row(d), kvspec, full(1, d)],
        out_shape=[jax.ShapeDtypeStruct((t, d), F32), jax.ShapeDtypeStruct((t, d), BF),
                   jax.ShapeDtypeStruct((nb * N_MEM, 2 * d), F32), jax.ShapeDtypeStruct((1, d), F32)],
        compiler_params=_params(48, ("arbitrary",)),
    )(*_in_hbm(dh2, h1, gc, w_cq, kv, w_co))


def _mem_bwd(mem, gm, dkv, w_ckv, tm):
    t, d = mem.shape
    tm = min(tm, t)

    def body(mem_ref, dkv_ref, w_ref, dg_ref):
        @pl.when(pl.program_id(0) == 0)
        def _():
            dg_ref[...] = jnp.zeros_like(dg_ref)

        mv = mem_ref[...]
        dmn = _dot_bt(dkv_ref[...].astype(BF), w_ref[...])
        dg_ref[...] += jnp.sum(dmn * (mv * _rs(mv)), axis=0, keepdims=True)

    del gm
    return pl.pallas_call(
        body, name="mem_bwd", grid=(t // tm,),
        in_specs=[pl.BlockSpec((tm, d), lambda i: (i, 0)), pl.BlockSpec((tm, 2 * d), lambda i: (i, 0)),
                  pl.BlockSpec((d, 2 * d), lambda i: (0, 0))],
        out_specs=pl.BlockSpec((1, d), lambda i: (0, 0)),
        out_shape=jax.ShapeDtypeStruct((1, d), F32),
        compiler_params=_params(32, ("arbitrary",)),
    )(mem, dkv, w_ckv)


def _ffn_loss_fwd(h2, gf, w1, w2, gl, target, tm):
    t, d = h2.shape
    tm = min(tm, t)

    def body(h2_ref, gf_ref, w1_ref, w2_ref, gl_ref, tg_ref, hn_ref, f_ref, dh3_ref, dgl_ref, loss_ref):
        @pl.when(pl.program_id(0) == 0)
        def _():
            dgl_ref[...] = jnp.zeros_like(dgl_ref)
            loss_ref[...] = jnp.zeros_like(loss_ref)

        h2 = h2_ref[...]
        hn = (h2 * _rs(h2) * gf_ref[...]).astype(BF)
        hn_ref[...] = hn
        h3 = h2
        for c in range(4):
            f = jnp.maximum(_dot(hn, w1_ref[c]), 0.0)
            f_ref[:, c * 1024:(c + 1) * 1024] = f.astype(BF)
            h3 = h3 + _dot((f * f).astype(BF), w2_ref[c])
        r3 = _rs(h3)
        yh = h3 * r3
        glv = gl_ref[...]
        e = yh * glv - tg_ref[...]
        loss_ref[...] += 0.5 * jnp.sum(jnp.sum(e * e, axis=-1, keepdims=True) * (1.0 / d), axis=0, keepdims=True)
        dy = e * (1.0 / d)
        dx, dg = _rms_bwd(dy, yh, r3, glv)
        dh3_ref[...] = dx
        dgl_ref[...] += jnp.sum(dg, axis=0, keepdims=True)

    row = lambda width: pl.BlockSpec((tm, width), lambda i: (i, 0))
    return pl.pallas_call(
        body, name="ffn_loss_fwd", grid=(t // tm,),
        in_specs=[row(d), pl.BlockSpec((1, d), lambda i: (0, 0)), pl.BlockSpec((4, d, 1024), lambda i: (0, 0, 0), pipeline_mode=pl.Buffered(1)),
                  pl.BlockSpec((4, 1024, d), lambda i: (0, 0, 0), pipeline_mode=pl.Buffered(1)),
                  pl.BlockSpec((1, d), lambda i: (0, 0)), row(d)],
        out_specs=[row(d), row(D_FF), row(d), pl.BlockSpec((1, d), lambda i: (0, 0)),
                   pl.BlockSpec((1, 1), lambda i: (0, 0))],
        out_shape=[jax.ShapeDtypeStruct((t, d), BF), jax.ShapeDtypeStruct((t, D_FF), BF),
                   jax.ShapeDtypeStruct((t, d), F32), jax.ShapeDtypeStruct((1, d), F32),
                   jax.ShapeDtypeStruct((1, 1), F32)],
        compiler_params=_params(56, ("arbitrary",)),
    )(*_in_hbm(h2, gf, w1, w2, gl, target))


def _ffn_bwd(dh3, f, h2, gf, w1, w2, tm):
    t, d = h2.shape
    tm = min(tm, t)

    def body(dh3_ref, f_ref, h2_ref, gf_ref, w1_ref, w2_ref, dh2_ref, dpre_ref, dgf_ref):
        @pl.when(pl.program_id(0) == 0)
        def _():
            dgf_ref[...] = jnp.zeros_like(dgf_ref)

        dh3 = dh3_ref[...]
        dh3b = dh3.astype(BF)
        dhn = jnp.zeros((tm, d), F32)
        for c in range(4):
            cols = slice(c * 1024, (c + 1) * 1024)
            dpre = (_dot_bt(dh3b, w2_ref[c]) * (2.0 * f_ref[:, cols].astype(F32))).astype(BF)
            dpre_ref[:, cols] = dpre
            dhn = dhn + _dot_bt(dpre, w1_ref[c])
        h2 = h2_ref[...]
        r = _rs(h2)
        dx, dg = _rms_bwd(dhn, h2 * r, r, gf_ref[...])
        dh2_ref[...] = dh3 + dx
        dgf_ref[...] += jnp.sum(dg, axis=0, keepdims=True)

    row = lambda width: pl.BlockSpec((tm, width), lambda i: (i, 0))
    return pl.pallas_call(
        body, name="ffn_bwd", grid=(t // tm,),
        in_specs=[row(d), row(D_FF), row(d), pl.BlockSpec((1, d), lambda i: (0, 0)),
                  pl.BlockSpec((4, d, 1024), lambda i: (0, 0, 0), pipeline_mode=pl.Buffered(1)),
                  pl.BlockSpec((4, 1024, d), lambda i: (0, 0, 0), pipeline_mode=pl.Buffered(1))],
        out_specs=[row(d), row(D_FF), pl.BlockSpec((1, d), lambda i: (0, 0))],
        out_shape=[jax.ShapeDtypeStruct((t, d), F32), jax.ShapeDtypeStruct((t, D_FF), BF),
                   jax.ShapeDtypeStruct((1, d), F32)],
        compiler_params=_params(56, ("arbitrary",)),
    )(*_in_hbm(dh3, f, h2, gf, w1, w2))


def _in_bwd(dproj, dh1, x, g, w_in, tm, ride=None):
    t, d = x.shape
    n = w_in.shape[1]
    tm = min(tm, t)

    def body(dp_ref, dh1_ref, x_ref, g_ref, w_ref, dx_ref, dg_ref):
        @pl.when(pl.program_id(0) == 0)
        def _():
            dg_ref[...] = jnp.zeros_like(dg_ref)

        dxn = _dot_bt(dp_ref[...], w_ref[...])
        xv = x_ref[...]
        r = _rs(xv)
        dx, dg = _rms_bwd(dxn, xv * r, r, g_ref[...])
        dx_ref[...] = dh1_ref[...] + dx
        dg_ref[...] += jnp.sum(dg, axis=0, keepdims=True)

    row = lambda width: pl.BlockSpec((tm, width), lambda i: (i, 0))
    (dx, dg), rode = _ride_call(
        body, "in_bwd", (t // tm,),
        in_specs=[row(n), row(d), row(d), pl.BlockSpec((1, d), lambda i: (0, 0)),
                  pl.BlockSpec((d, n), lambda i: (0, 0))],
        out_specs=[row(d), pl.BlockSpec((1, d), lambda i: (0, 0))],
        out_shape=[jax.ShapeDtypeStruct((t, d), F32), jax.ShapeDtypeStruct((1, d), F32)],
        scratch_shapes=[], operands=(dproj, dh1, x, g, w_in), vmem_mb=48, ride=ride)
    return dx, dg, rode


class _GradReduce:
    def __init__(self, c_idx):
        self.c_idx = c_idx
        self.sums = {}

    def sibling(self, slabs):
        return _SiblingExchange(slabs)

    def chip(self, names, slabs, recv):
        for k, a, r in zip(names, slabs, recv):
            self.sums[k] = _chip_sum(a, r, self.c_idx, "chip_sum_" + k)
        return _ChipExchange([self.sums[k] for k in names])


def _full_weights(gathered):
    d = D_MODEL
    out = {}
    for k, a in gathered.items():
        if k in ("w_in", "w_ckv", "w_ff1"):
            out[k] = a.transpose(1, 0, 2).reshape(d, -1)
        else:
            out[k] = a.reshape(-1, d)
    return out


def _slabs(a):
    return a.reshape(N_DEV, -1, a.shape[-1])


def _local_step(x, mem, target, small, big, nb, s, tq=256, gather_rest=None, reduce=None):
    d = D_MODEL
    g_mix, g_v, w_sp, b_sp, g_head, g_cross, g_mem, g_ffn, g_fin = (
        small[k] for k in ("norm_mix_g", "gm_v_norm_g", "w_spatial", "b_spatial", "head_norm_g", "norm_cross_g",
                           "norm_mem_g", "norm_ffn_g", "norm_final_g"))
    tri = jnp.tril(jnp.ones((CHUNK, CHUNK), dtype=bool))
    w_sp_m = jnp.where(tri[None], w_sp, 0.0)
    wt = w_sp_m.astype(BF)
    wtt = jnp.swapaxes(w_sp_m, 1, 2).astype(BF)
    bb = jnp.broadcast_to(b_sp[:, :, None], (GM_GROUPS, CHUNK, CHUNK))
    hg_a = g_head[:, :GM_WIDTH]

    proj, xn = _norm_matmul(x, g_mix, big["w_in"], 512, "in_proj")
    merged = _gmlp_fwd(proj, g_v, wt, bb, hg_a, 512)
    o_sb, tot, merged, nblk, gathered = _sb_fwd(proj, merged, g_head, nb, s, tq, ride=gather_rest)
    if gather_rest is not None:
        big = dict(big, **_full_weights(dict(zip(BIG[1:], gathered))))
    w1c = big["w_ff1"].reshape(d, 4, 1024).transpose(1, 0, 2)
    w2c = big["w_ff2"].reshape(4, 1024, d)
    kv, memn = _norm_matmul(mem, g_mem, big["w_ckv"], 512, "mem_proj")
    h1, h2, hn, oc = _mix_cross_fwd(x, merged, big["w_out"], g_cross, big["w_cq"], kv, big["w_co"], s, 512)
    hn2, f, dh3, d_fin, loss = _ffn_loss_fwd(h2, g_ffn, w1c, w2c, g_fin, target, 512)

    gbig = {}
    dh2, dpre, d_ffn = _ffn_bwd(dh3, f, h2, g_ffn, w1c, w2c, 512)
    gbig["w_ff2"] = _slabs(_wgrad(f, dh3, 1024, 1024, "wgrad_ff2", square_a=True))
    gbig["w_ff1"] = _slabs(_wgrad_wide(hn2, dpre, 512, 1024, "wgrad_ff1", col_shards=4))
    dh1, dqc, dkv, d_cross = _cross_bwd(dh2, h1, g_cross, big["w_cq"], kv, big["w_co"], s, 512)
    gbig["w_co"] = _slabs(_wgrad(oc, dh2, 1024, 1024, "wgrad_co"))
    gbig["w_cq"] = _slabs(_wgrad(hn, dqc, 1024, 1024, "wgrad_cq"))
    gbig["w_ckv"] = _slabs(_wgrad(memn, dkv, 512, 1024, "wgrad_ckv", col_shards=4))
    d_mem = _mem_bwd(mem, g_mem, dkv, big["w_ckv"], 512)
    dmerged = _matmul_bt(dh1, big["w_out"], 512, "out_bwd")
    gbig["w_out"] = _slabs(_wgrad(merged, dh1, 1024, 1024, "wgrad_out"))
    rest = BIG[1:]
    ride = reduce.sibling([gbig[k] for k in rest]) if reduce else None
    dproj, d_wsp, d_bb, d_gv, d_hga, recv = _gmlp_bwd(proj, dmerged, g_v, wt, wtt, bb, hg_a, 512, ride=ride)
    ride = reduce.chip(rest, [gbig[k] for k in rest], recv) if reduce else None
    dproj, dk, dv, d_hgb, parts_rest = _sb_bwd(proj, o_sb, tot, nblk, dmerged, dproj, g_head, nb, s, tq, ride=ride)
    dproj = _place(_place(dproj, dk, 3, "place_dk"), dv, 4, "place_dv")
    gbig["w_in"] = _slabs(_wgrad_wide(xn, dproj, 512, 1024, "wgrad_in", col_shards=4))
    last = None
    if reduce:
        recv = _run_exchange(reduce.sibling([gbig["w_in"]]), "grad_sibling_exchange_w_in")
        last = reduce.chip(["w_in"], [gbig["w_in"]], recv)
    grad_x, d_mix, _ = _in_bwd(dproj, dh1, x, g_mix, big["w_in"], 512)
    parts = dict(zip(rest, parts_rest))

    gsmall = {
        "norm_mix_g": d_mix, "gm_v_norm_g": d_gv, "w_spatial": d_wsp, "b_spatial": d_bb[:, :, 0],
        "head_norm_g": jnp.concatenate([d_hga, jnp.sum(d_hgb, axis=0)], axis=1), "norm_cross_g": d_cross,
        "norm_mem_g": d_mem, "norm_ffn_g": d_ffn, "norm_final_g": d_fin,
    }
    return loss, grad_x, gsmall, gbig, parts, last


BIG = ("w_in", "w_out", "w_cq", "w_ckv", "w_co", "w_ff1", "w_ff2")
SMALL = ("norm_mix_g", "gm_v_norm_g", "w_spatial", "b_spatial", "head_norm_g", "norm_cross_g", "norm_mem_g",
         "norm_ffn_g", "norm_final_g")


def _local_copies_start(srcs, stages, sems):
    loads = [pltpu.make_async_copy(src, stage, sems.at[w]) for w, (src, stage) in enumerate(zip(srcs, stages))]
    for ld in loads:
        ld.start()
    return loads


def _local_copies_finish(loads, stages, dsts, sems):
    stores = []
    for w, (ld, stage, dst) in enumerate(zip(loads, stages, dsts)):
        ld.wait()
        st = pltpu.make_async_copy(stage, dst, sems.at[w])
        st.start()
        stores.append(st)
    for st in stores:
        st.wait()


def _chip_sum(slabs, recv, c_idx, name):
    _, r, cw = slabs.shape
    tr = min(r, 256)

    def body(c_ref, a_ref, b_ref, o_ref):
        del c_ref
        o_ref[...] = (a_ref[...] + b_ref[...]).astype(BF)

    return pl.pallas_call(
        body, name=name,
        grid_spec=pltpu.PrefetchScalarGridSpec(
            num_scalar_prefetch=1, grid=(N_CHIPS, r // tr),
            in_specs=[pl.BlockSpec((None, tr, cw), lambda p, i, c_ref: (2 * p + c_ref[0], i, 0)),
                      pl.BlockSpec((None, tr, cw), lambda p, i, c_ref: (p, i, 0))],
            out_specs=pl.BlockSpec((None, tr, cw), lambda p, i, c_ref: (p, i, 0))),
        out_shape=jax.ShapeDtypeStruct((N_CHIPS, r, cw), BF),
        compiler_params=_params(32, ("arbitrary", "arbitrary")),
    )(c_idx, *_in_hbm(slabs, recv))


def _sum4(sums, parts, q_idx, name):
    _, r, cw = parts.shape
    tr = min(r, 256)

    def body(q_ref, own_ref, a_ref, b_ref, c_ref, o_ref):
        del q_ref
        o_ref[...] = ((own_ref[...].astype(F32) + a_ref[...].astype(F32)) + b_ref[...].astype(F32)) + c_ref[
            ...].astype(F32)

    spec = lambda k: pl.BlockSpec((None, tr, cw), lambda i, q_ref: ((q_ref[0] + k) % N_CHIPS, i, 0))
    return pl.pallas_call(
        body, name=name,
        grid_spec=pltpu.PrefetchScalarGridSpec(
            num_scalar_prefetch=1, grid=(r // tr,), in_specs=[spec(0), spec(1), spec(2), spec(3)],
            out_specs=pl.BlockSpec((tr, cw), lambda i, q_ref: (i, 0))),
        out_shape=jax.ShapeDtypeStruct((r, cw), F32),
        compiler_params=_params(32, ("arbitrary",)),
    )(q_idx, *_in_hbm(sums, parts, parts, parts))


def _half_exchange(halves):
    n = len(halves)

    def body(*refs):
        ins, outs, stages = refs[:n], refs[n:2 * n], refs[2 * n:3 * n]
        send_sems, recv_sems, ld_sems, st_sems = refs[3 * n:]
        x, y, c = lax.axis_index("x"), lax.axis_index("y"), lax.axis_index("c")
        loads = _local_copies_start(ins, stages, ld_sems)
        copies = []
        for w in range(n):
            cp = pltpu.make_async_remote_copy(
                src_ref=ins[w], dst_ref=outs[w].at[c], send_sem=send_sems.at[w], recv_sem=recv_sems.at[w],
                device_id=(x, y, 1 - c), device_id_type=MESH)
            cp.start()
            copies.append(cp)
        _local_copies_finish(loads, stages, [outs[w].at[c] for w in range(n)], st_sems)
        for cp in copies:
            cp.wait()

    return pl.pallas_call(
        body, name="grad_half_exchange",
        in_specs=[ANY] * n, out_specs=[ANY] * n,
        out_shape=[jax.ShapeDtypeStruct((2,) + a.shape, a.dtype) for a in halves],
        scratch_shapes=[pltpu.VMEM(a.shape, a.dtype) for a in halves] + [
            pltpu.SemaphoreType.DMA((n,)), pltpu.SemaphoreType.DMA((n,)),
            pltpu.SemaphoreType.DMA((n,)), pltpu.SemaphoreType.DMA((n,))],
        compiler_params=_params(24),
    )(*halves)


def _small_all_reduce(packed, ride=None):
    rows = packed.shape[0]
    ride = ride or _NoExchange()
    ri, ro = len(ride.in_arrays), len(ride.out_shape)

    def body(*refs):
        in_ref, rins, out_ref, routs = refs[0], refs[1:1 + ri], refs[1 + ri], refs[2 + ri:2 + ri + ro]
        pair, chip_sum, chips, d2d_send, d2d_recv, ici_send, ici_recv = refs[2 + ri + ro:9 + ri + ro]
        rscr = refs[9 + ri + ro:]
        ride.start(rins, routs, rscr)
        x, y, c = lax.axis_index("x"), lax.axis_index("y"), lax.axis_index("c")
        q = 2 * x + y
        pair[c] = in_ref[...]
        swap = pltpu.make_async_remote_copy(
            src_ref=in_ref, dst_ref=pair.at[c], send_sem=d2d_send, recv_sem=d2d_recv,
            device_id=(x, y, 1 - c), device_id_type=MESH)
        swap.start()
        swap.wait()
        both = pair[0] + pair[1]
        chip_sum[...] = both
        chips[q] = both
        copies = [pltpu.make_async_remote_copy(
            src_ref=chip_sum, dst_ref=chips.at[q], send_sem=ici_send.at[k], recv_sem=ici_recv.at[k],
            device_id=(px, py, c), device_id_type=MESH) for k, (px, py) in enumerate(_other_chips(x, y))]
        for cp in copies:
            cp.start()
        for cp in copies:
            cp.wait()
        out_ref[...] = ((chips[0] + chips[1]) + chips[2]) + chips[3]
        ride.finish(rins, routs, rscr)

    vmem = pl.BlockSpec(memory_space=pltpu.VMEM)
    res = pl.pallas_call(
        body, name="small_all_reduce",
        in_specs=[vmem] + [ANY] * ri, out_specs=[vmem] + [ANY] * ro,
        out_shape=[jax.ShapeDtypeStruct(packed.shape, F32)] + list(ride.out_shape),
        scratch_shapes=[pltpu.VMEM((2, rows, 128), F32), pltpu.VMEM((rows, 128), F32),
                        pltpu.VMEM((N_CHIPS, rows, 128), F32), pltpu.SemaphoreType.DMA, pltpu.SemaphoreType.DMA,
                        pltpu.SemaphoreType.DMA((3,)), pltpu.SemaphoreType.DMA((3,))] + list(ride.scratch_shapes),
        compiler_params=_params(16),
    )(packed, *ride.in_arrays)
    return res[0], res[1:]


def _adamw(g, w, m, v, name):
    r, cw = g.shape
    tr = 256 if r % 256 == 0 else r

    def body(g_ref, w_ref, m_ref, v_ref, d_ref, nm_ref, nv_ref):
        gv = g_ref[...]
        nm = ADAM_B1 * m_ref[...] + (1.0 - ADAM_B1) * gv
        nv = ADAM_B2 * v_ref[...] + (1.0 - ADAM_B2) * (gv * gv)
        m_hat = nm / (1.0 - ADAM_B1 ** ADAM_STEP)
        v_hat = nv / (1.0 - ADAM_B2 ** ADAM_STEP)
        d_ref[...] = -ADAM_LR * (m_hat / (jnp.sqrt(v_hat) + ADAM_EPS) + ADAM_WD * w_ref[...])
        nm_ref[...] = nm
        nv_ref[...] = nv

    spec = pl.BlockSpec((tr, cw), lambda i: (i, 0))
    return pl.pallas_call(
        body, name=name, grid=(r // tr,),
        in_specs=[spec] * 4, out_specs=[spec] * 3,
        out_shape=[jax.ShapeDtypeStruct((r, cw), F32)] * 3,
        compiler_params=_params(32, ("arbitrary",)),
    )(*_in_hbm(g, w, m, v))


def _small_params(args):
    small = {k: args[k].reshape(1, -1) for k in SMALL}
    small["w_spatial"] = args["w_spatial"][0]
    small["b_spatial"] = args["b_spatial"][0]
    return small


def _pack(parts, rows):
    flat = jnp.concatenate([p.reshape(-1).astype(F32) for p in parts])
    return jnp.pad(flat, (0, rows * 128 - flat.shape[0])).reshape(rows, 128)


def _unpack(packed, shapes):
    flat = packed.reshape(-1)
    out, off = [], 0
    for shp in shapes:
        size = math.prod(shp)
        out.append(flat[off:off + size].reshape(shp))
        off += size
    return out


def kernel(x, mem, norm_mix_g, w_in, gm_v_norm_g, w_spatial, b_spatial, head_norm_g, w_out, norm_cross_g, norm_mem_g, w_cq, w_ckv, w_co, norm_ffn_g, w_ff1, w_ff2, norm_final_g, loss_target, m_norm_mix_g, m_w_in, m_gm_v_norm_g, m_w_spatial, m_b_spatial, m_head_norm_g, m_w_out, m_norm_cross_g, m_norm_mem_g, m_w_cq, m_w_ckv, m_w_co, m_norm_ffn_g, m_w_ff1, m_w_ff2, m_norm_final_g, v_norm_mix_g, v_w_in, v_gm_v_norm_g, v_w_spatial, v_b_spatial, v_head_norm_g, v_w_out, v_norm_cross_g, v_norm_mem_g, v_w_cq, v_w_ckv, v_w_co, v_norm_ffn_g, v_w_ff1, v_w_ff2, v_norm_final_g):
    args = dict(locals())
    d = D_MODEL
    nb, s, _ = x.shape
    c_idx = lax.axis_index("c").astype(jnp.int32).reshape(1)
    q_idx = (2 * lax.axis_index("x") + lax.axis_index("y")).astype(jnp.int32).reshape(1)
    rest = BIG[1:]

    shards = {k: args[k][0].astype(BF) for k in BIG}
    big = _full_weights({"w_in": _run_exchange(_GatherExchange([shards["w_in"]]), "all_gather_w_in")[0]})
    gather_rest = _GatherExchange([shards[k] for k in rest])

    reduce = _GradReduce(c_idx)
    loss, grad_x, gsmall, _, parts, last = _local_step(
        x.reshape(nb * s, d), mem.reshape(nb * N_MEM, d), loss_target.reshape(nb * s, d), _small_params(args), big,
        nb, s, gather_rest=gather_rest, reduce=reduce)

    shapes = [args[k].shape for k in SMALL]
    n_small = sum(math.prod(sh) for sh in shapes)
    rows = -(-(n_small + 1) // 1024) * 8
    reduced, (parts["w_in"],) = _small_all_reduce(_pack([gsmall[k] for k in SMALL] + [loss], rows), ride=last)
    halves = [_sum4(reduce.sums[k], parts[k], q_idx, "sum4_" + k) for k in BIG]
    both = _half_exchange(halves)

    out = {"grad_x": grad_x.reshape(nb, s, d)}
    for k, g2 in zip(BIG, both):
        shp = args[k].shape
        g = g2.reshape(shp[1], shp[2])
        dl, nm, nv = _adamw(g, args[k][0], args["m_" + k][0], args["v_" + k][0], "adamw_" + k)
        out["grad_" + k], out["delta_" + k], out["new_m_" + k], out["new_v_" + k] = (
            a.reshape(shp) for a in (g, dl, nm, nv))

    dl, nm, nv = _adamw(reduced, _pack([args[k] for k in SMALL], rows), _pack([args["m_" + k] for k in SMALL], rows),
                        _pack([args["v_" + k] for k in SMALL], rows), "adamw_small")
    for name, arr in (("grad_", reduced), ("delta_", dl), ("new_m_", nm), ("new_v_", nv)):
        for k, a in zip(SMALL, _unpack(arr, shapes)):
            out[name + k] = a
    out["loss"] = reduced.reshape(-1)[n_small]

    names = ["norm_mix_g", "w_in", "gm_v_norm_g", "w_spatial", "b_spatial", "head_norm_g", "w_out", "norm_cross_g",
             "norm_mem_g", "w_cq", "w_ckv", "w_co", "norm_ffn_g", "w_ff1", "w_ff2", "norm_final_g"]
    return (out["loss"], out["grad_x"], *[out["grad_" + k] for k in names], *[out["delta_" + k] for k in names],
            *[out["new_m_" + k] for k in names], *[out["new_v_" + k] for k in names])
```

```python
import functools
import math

import jax
import jax.numpy as jnp
from jax import lax
from jax.experimental import pallas as pl
from jax.experimental.pallas import tpu as pltpu

F32 = jnp.float32
BF = jnp.bfloat16

EPS = 1e-6
D_MODEL = 1024
CHUNK = 128
GM_GROUPS = 4
GM_WIDTH = 512
SB_WIDTH = 512
HEAD_LANES = 64
SB_SCALE = 0.125
SB_SKIP = -104.0
X_HEADS = 4
X_HEAD_DIM = 256
N_MEM = 256
D_FF = 4096
IN_COLS = 2560
N_CHIPS = 4
N_DEV = 8

ADAM_LR = 0.001
ADAM_B1 = 0.9
ADAM_B2 = 0.999
ADAM_EPS = 1e-08
ADAM_WD = 0.01
ADAM_STEP = 10

V7X_VMEM_BYTES = 64 * 1024 * 1024
MESH = pl.DeviceIdType.MESH
ANY = pl.BlockSpec(memory_space=pl.ANY)

GELU_C = math.sqrt(2.0 / math.pi)
GELU_A = 0.044715


def _params(vmem_mb, sem=None):
    assert vmem_mb * 1024 * 1024 <= V7X_VMEM_BYTES
    return pltpu.CompilerParams(vmem_limit_bytes=vmem_mb * 1024 * 1024, dimension_semantics=sem)


PIN_MIN_ELEMENTS = 1 << 18


def _in_hbm(*arrays):
    return tuple(pltpu.with_memory_space_constraint(a, pltpu.HBM) if a.size >= PIN_MIN_ELEMENTS else a
                 for a in arrays)


def _dot(a, b):
    return jnp.dot(a, b, preferred_element_type=F32)


def _dot_bt(a, b):
    return lax.dot_general(a, b, (((1,), (1,)), ((), ())), preferred_element_type=F32)


def _dot_at(a, b):
    return lax.dot_general(a, b, (((0,), (0,)), ((), ())), preferred_element_type=F32)


def _gelu(x):
    t = jnp.tanh(GELU_C * (x + GELU_A * x * x * x))
    return 0.5 * x * (1.0 + t)


def _gelu_and_grad(x):
    x2 = x * x
    t = jnp.tanh(GELU_C * (x + GELU_A * x2 * x))
    h = 0.5 * (1.0 + t)
    return x * h, h + 0.5 * x * (1.0 - t * t) * (GELU_C * (1.0 + 3.0 * GELU_A * x2))


def _rs(x):
    return lax.rsqrt(jnp.mean(x * x, axis=-1, keepdims=True) + EPS)


def _rms_bwd(dxn, xhat, r, g):
    dxh = dxn * g
    dx = r * (dxh - xhat * jnp.mean(dxh * xhat, axis=-1, keepdims=True))
    return dx, dxn * xhat


def _norm_matmul(x, g, w, tm, name):
    t, d = x.shape
    n = w.shape[1]
    tm = min(tm, t)

    def body(x_ref, g_ref, w_ref, out_ref, xn_ref):
        xv = x_ref[...]
        xn = (xv * _rs(xv) * g_ref[...]).astype(BF)
        xn_ref[...] = xn
        out_ref[...] = _dot(xn, w_ref[...]).astype(out_ref.dtype)

    return pl.pallas_call(
        body, name=name, grid=(t // tm,),
        in_specs=[pl.BlockSpec((tm, d), lambda i: (i, 0)), pl.BlockSpec((1, d), lambda i: (0, 0)),
                  pl.BlockSpec((d, n), lambda i: (0, 0))],
        out_specs=[pl.BlockSpec((tm, n), lambda i: (i, 0)), pl.BlockSpec((tm, d), lambda i: (i, 0))],
        out_shape=[jax.ShapeDtypeStruct((t, n), BF), jax.ShapeDtypeStruct((t, d), BF)],
        compiler_params=_params(48, ("arbitrary",)),
    )(*_in_hbm(x, g, w))


def _wgrad(a, g, tn, tk, name, square_a=False, col_shards=1):
    t, m = a.shape
    n = g.shape[1]
    tk = min(tk, t)
    tm = min(m, 1024)
    ns = n // col_shards
    assert ns % tn == 0 and m % tm == 0
    per = ns // tn
    nk = t // tk

    def body(a_ref, g_ref, o_ref):
        k = pl.program_id(2)

        @pl.when(k == 0)
        def _():
            o_ref[...] = jnp.zeros_like(o_ref)

        av = a_ref[...]
        if square_a:
            af = av.astype(F32)
            av = af * af
        o_ref[...] += _dot_at(av.astype(BF), g_ref[...].astype(BF))

    return pl.pallas_call(
        body, name=name, grid=(m // tm, n // tn, nk),
        in_specs=[pl.BlockSpec((tk, tm), lambda i, j, k: (k, i)), pl.BlockSpec((tk, tn), lambda i, j, k: (k, j))],
        out_specs=pl.BlockSpec((None, tm, tn), lambda i, j, k: (j // per, i, j % per)),
        out_shape=jax.ShapeDtypeStruct((col_shards, m, ns), F32),
        compiler_params=_params(48, ("arbitrary", "arbitrary", "arbitrary")),
    )(*_in_hbm(a, g))


def _wgrad_wide(a, g, tm, tk, name, col_shards):
    t, m = a.shape
    n = g.shape[1]
    tk = min(tk, t)
    tm = min(tm, m)
    ns = n // col_shards

    def body(a_ref, g_ref, o_ref):
        @pl.when(pl.program_id(1) == 0)
        def _():
            o_ref[...] = jnp.zeros_like(o_ref)

        a_t = a_ref[...].astype(BF).T
        for p in range(col_shards):
            o_ref[p] += _dot(a_t, g_ref[:, p * ns:(p + 1) * ns].astype(BF))

    return pl.pallas_call(
        body, name=name, grid=(m // tm, t // tk),
        in_specs=[pl.BlockSpec((tk, tm), lambda i, k: (k, i)), pl.BlockSpec((tk, n), lambda i, k: (k, 0))],
        out_specs=pl.BlockSpec((col_shards, tm, ns), lambda i, k: (0, i, 0)),
        out_shape=jax.ShapeDtypeStruct((col_shards, m, ns), F32),
        compiler_params=_params(48, ("arbitrary", "arbitrary")),
    )(*_in_hbm(a, g))


def _matmul_bt(a, w, tm, name):
    t, n = a.shape
    k = w.shape[0]
    tm = min(tm, t)

    def body(a_ref, w_ref, o_ref):
        o_ref[...] = _dot_bt(a_ref[...].astype(BF), w_ref[...]).astype(o_ref.dtype)

    return pl.pallas_call(
        body, name=name, grid=(t // tm,),
        in_specs=[pl.BlockSpec((tm, n), lambda i: (i, 0)), pl.BlockSpec((k, n), lambda i: (0, 0))],
        out_specs=pl.BlockSpec((tm, k), lambda i: (i, 0)),
        out_shape=jax.ShapeDtypeStruct((t, k), BF),
        compiler_params=_params(32, ("arbitrary",)),
    )(*_in_hbm(a, w))


def _gmlp_fwd(proj, gg, wt, bb, hg, tm):
    t = proj.shape[0]
    tm = min(tm, t)

    def body(u_ref, v_ref, gg_ref, wt_ref, bb_ref, hg_ref, out_ref):
        for cc in range(tm // CHUNK):
            rows = slice(cc * CHUNK, (cc + 1) * CHUNK)
            for g in range(GM_GROUPS):
                cols = slice(g * 128, (g + 1) * 128)
                u = _gelu(u_ref[rows, cols].astype(F32))
                gv = _gelu(v_ref[rows, cols].astype(F32))
                vn = gv * _rs(gv) * gg_ref[:, cols]
                mixed = _dot(wt_ref[g], vn.astype(BF)) + bb_ref[g]
                a = u * mixed
                out_ref[rows, cols] = (a * _rs(a) * hg_ref[:, cols]).astype(BF)

    return pl.pallas_call(
        body, name="gmlp_fwd", grid=(t // tm,),
        in_specs=[pl.BlockSpec((tm, 512), lambda i: (i, 0)), pl.BlockSpec((tm, 512), lambda i: (i, 1)),
                  pl.BlockSpec((1, 512), lambda i: (0, 0)), pl.BlockSpec((4, 128, 128), lambda i: (0, 0, 0)),
                  pl.BlockSpec((4, 128, 128), lambda i: (0, 0, 0)), pl.BlockSpec((1, 512), lambda i: (0, 0))],
        out_specs=pl.BlockSpec((tm, 512), lambda i: (i, 0)),
        out_shape=jax.ShapeDtypeStruct((t, 1024), BF),
        compiler_params=_params(32, ("arbitrary",)),
    )(*_in_hbm(proj, proj, gg, wt, bb, hg))


def _gmlp_bwd(proj, dmerged, gg, wt, wtt, bb, hg, tm, ride=None):
    t = proj.shape[0]
    tm = min(tm, t)
    nsteps = t // tm

    def body(u_ref, v_ref, dm_ref, gg_ref, wt_ref, wtt_ref, bb_ref, hg_ref,
             dp_ref, dw_ref, db_ref, dgg_ref, dhg_ref):
        i = pl.program_id(0)

        @pl.when(i == 0)
        def _():
            dw_ref[...] = jnp.zeros_like(dw_ref)
            db_ref[...] = jnp.zeros_like(db_ref)
            dgg_ref[...] = jnp.zeros_like(dgg_ref)
            dhg_ref[...] = jnp.zeros_like(dhg_ref)

        for cc in range(tm // CHUNK):
            rows = slice(cc * CHUNK, (cc + 1) * CHUNK)
            for g in range(GM_GROUPS):
                cols = slice(g * 128, (g + 1) * 128)
                up = u_ref[rows, cols].astype(F32)
                gp = v_ref[rows, cols].astype(F32)
                u, u_grad = _gelu_and_grad(up)
                gv, gv_grad = _gelu_and_grad(gp)
                rv = _rs(gv)
                gvh = gv * rv
                ggv = gg_ref[:, cols]
                vnb = (gvh * ggv).astype(BF)
                mixed = _dot(wt_ref[g], vnb) + bb_ref[g]
                a = u * mixed
                ra = _rs(a)
                ah = a * ra
                dm = dm_ref[rows, cols].astype(F32)
                dhg_ref[:, cols] += jnp.sum(dm * ah, axis=0, keepdims=True)
                dah = dm * hg_ref[:, cols]
                da = ra * (dah - ah * jnp.mean(dah * ah, axis=-1, keepdims=True))
                du = da * mixed
                dmix = da * u
                db_ref[g] += dmix
                dmb = dmix.astype(BF)
                dw_ref[g] += _dot_bt(dmb, vnb)
                dvn = _dot(wtt_ref[g], dmb)
                dgg_ref[:, cols] += jnp.sum(dvn * gvh, axis=0, keepdims=True)
                dgh = dvn * ggv
                dgv = rv * (dgh - gvh * jnp.mean(dgh * gvh, axis=-1, keepdims=True))
                dp_ref[rows, cols] = (du * u_grad).astype(BF)
                dp_ref[rows, 512 + g * 128:512 + (g + 1) * 128] = (dgv * gv_grad).astype(BF)

        @pl.when(i == nsteps - 1)
        def _():
            r = lax.broadcasted_iota(jnp.int32, (CHUNK, CHUNK), 0)
            c = lax.broadcasted_iota(jnp.int32, (CHUNK, CHUNK), 1)
            for g in range(GM_GROUPS):
                dw_ref[g] = jnp.where(c <= r, dw_ref[g], 0.0)
                db_ref[g] = jnp.broadcast_to(jnp.sum(db_ref[g], axis=-1, keepdims=True), (CHUNK, CHUNK))

    small = lambda shape: pl.BlockSpec(shape, lambda i: (0,) * len(shape))
    res, rode = _ride_call(
        body, "gmlp_bwd", (nsteps,),
        in_specs=[pl.BlockSpec((tm, 512), lambda i: (i, 0)), pl.BlockSpec((tm, 512), lambda i: (i, 1)),
                  pl.BlockSpec((tm, 512), lambda i: (i, 0)), small((1, 512)), small((4, 128, 128)),
                  small((4, 128, 128)), small((4, 128, 128)), small((1, 512))],
        out_specs=[pl.BlockSpec((tm, 1024), lambda i: (i, 0)), small((4, 128, 128)), small((4, 128, 128)),
                   small((1, 512)), small((1, 512))],
        out_shape=[jax.ShapeDtypeStruct((t, IN_COLS), BF), jax.ShapeDtypeStruct((4, 128, 128), F32),
                   jax.ShapeDtypeStruct((4, 128, 128), F32), jax.ShapeDtypeStruct((1, 512), F32),
                   jax.ShapeDtypeStruct((1, 512), F32)],
        scratch_shapes=[], operands=(proj, proj, dmerged, gg, wt, wtt, bb, hg), vmem_mb=32, ride=ride)
    return (*res, rode)


def _other_chips(x, y):
    return ((1 - x, y), (x, 1 - y), (1 - x, 1 - y))


class _GatherExchange:
    def __init__(self, shards):
        n = len(shards)
        self.n = n
        self.in_arrays = list(shards)
        self.out_shape = [jax.ShapeDtypeStruct((N_CHIPS,) + a.shape, a.dtype) for a in shards]
        self.half_rows = [a.shape[0] // 2 for a in shards]
        sems = lambda k: pltpu.SemaphoreType.DMA((k,))
        self.scratch_shapes = [pltpu.VMEM(a.shape, a.dtype) for a in shards] + [
            sems(3 * n), sems(3 * n), sems(3 * n), sems(3 * n), sems(n), sems(n)]

    def _copies(self, ins, outs, scr):
        n = self.n
        stages, (ici_send, ici_recv, d2d_send, d2d_recv, ld_sems, st_sems) = scr[:n], scr[n:]
        x, y, c = lax.axis_index("x"), lax.axis_index("y"), lax.axis_index("c")
        q = 2 * x + y
        loads = [pltpu.make_async_copy(ins[w], stages[w], ld_sems.at[w]) for w in range(n)]
        stores = [pltpu.make_async_copy(stages[w], outs[w].at[q], st_sems.at[w]) for w in range(n)]
        ici, d2d = [], []
        for w in range(n):
            half = pl.ds(c * self.half_rows[w], self.half_rows[w])
            for k, (px, py) in enumerate(_other_chips(x, y)):
                ici.append(pltpu.make_async_remote_copy(
                    src_ref=ins[w].at[half], dst_ref=outs[w].at[q, half], send_sem=ici_send.at[3 * w + k],
                    recv_sem=ici_recv.at[3 * w + k], device_id=(px, py, c), device_id_type=MESH))
                landed = outs[w].at[2 * px + py, half]
                d2d.append(pltpu.make_async_remote_copy(
                    src_ref=landed, dst_ref=landed, send_sem=d2d_send.at[3 * w + k],
                    recv_sem=d2d_recv.at[3 * w + k], device_id=(x, y, 1 - c), device_id_type=MESH))
        return loads, stores, ici, d2d

    def start(self, ins, outs, scr):
        loads, stores, ici, _ = self._copies(ins, outs, scr)
        for cp in loads + ici:
            cp.start()
        for ld, st in zip(loads, stores):
            ld.wait()
            st.start()

    def relay(self, ins, outs, scr):
        _, _, ici, d2d = self._copies(ins, outs, scr)
        for got, fwd in zip(ici, d2d):
            got.wait_recv()
            fwd.start()

    def finish(self, ins, outs, scr):
        _, stores, ici, d2d = self._copies(ins, outs, scr)
        for cp in ici:
            cp.wait_send()
        for cp in d2d + stores:
            cp.wait()


class _SiblingExchange:
    def __init__(self, slabs):
        n = len(slabs)
        self.n = n
        self.in_arrays = list(slabs)
        self.out_shape = [jax.ShapeDtypeStruct((N_CHIPS,) + a.shape[1:], a.dtype) for a in slabs]
        self.scratch_shapes = [pltpu.SemaphoreType.DMA((4 * n,)), pltpu.SemaphoreType.DMA((4 * n,))]

    def _copies(self, ins, outs, scr):
        send_sems, recv_sems = scr
        x, y, c = lax.axis_index("x"), lax.axis_index("y"), lax.axis_index("c")
        return [pltpu.make_async_remote_copy(
            src_ref=ins[w].at[2 * p + (1 - c)], dst_ref=outs[w].at[p], send_sem=send_sems.at[4 * w + p],
            recv_sem=recv_sems.at[4 * w + p], device_id=(x, y, 1 - c), device_id_type=MESH)
            for w in range(self.n) for p in range(N_CHIPS)]

    def start(self, ins, outs, scr):
        for cp in self._copies(ins, outs, scr):
            cp.start()

    def finish(self, ins, outs, scr):
        for cp in self._copies(ins, outs, scr):
            cp.wait()


class _ChipExchange:
    def __init__(self, sums):
        n = len(sums)
        self.n = n
        self.in_arrays = list(sums)
        self.out_shape = [jax.ShapeDtypeStruct(a.shape, a.dtype) for a in sums]
        self.scratch_shapes = [pltpu.SemaphoreType.DMA((3 * n,)), pltpu.SemaphoreType.DMA((3 * n,))]

    def _copies(self, ins, outs, scr):
        send_sems, recv_sems = scr
        x, y, c = lax.axis_index("x"), lax.axis_index("y"), lax.axis_index("c")
        q = 2 * x + y
        return [pltpu.make_async_remote_copy(
            src_ref=ins[w].at[2 * px + py], dst_ref=outs[w].at[q], send_sem=send_sems.at[3 * w + k],
            recv_sem=recv_sems.at[3 * w + k], device_id=(px, py, c), device_id_type=MESH)
            for w in range(self.n) for k, (px, py) in enumerate(_other_chips(x, y))]

    def start(self, ins, outs, scr):
        for cp in self._copies(ins, outs, scr):
            cp.start()

    def finish(self, ins, outs, scr):
        for cp in self._copies(ins, outs, scr):
            cp.wait()


class _NoExchange:
    in_arrays, out_shape, scratch_shapes = (), (), ()

    def start(self, ins, outs, scr):
        pass

    def finish(self, ins, outs, scr):
        pass


def _run_exchange(ex, name):
    n_in, n_out = len(ex.in_arrays), len(ex.out_shape)

    def body(*refs):
        ins, outs, scr = refs[:n_in], refs[n_in:n_in + n_out], refs[n_in + n_out:]
        ex.start(ins, outs, scr)
        if hasattr(ex, "relay"):
            ex.relay(ins, outs, scr)
        ex.finish(ins, outs, scr)

    return pl.pallas_call(
        body, name=name, in_specs=[ANY] * n_in, out_specs=[ANY] * n_out, out_shape=ex.out_shape,
        scratch_shapes=ex.scratch_shapes, compiler_params=_params(24),
    )(*ex.in_arrays)


def _ride_call(body, name, grid, in_specs, out_specs, out_shape, scratch_shapes, operands, vmem_mb, ride=None,
               aliases=None):
    ride = ride or _NoExchange()
    ni, no, ns = len(in_specs), len(out_specs), len(scratch_shapes)
    ri, ro = len(ride.in_arrays), len(ride.out_shape)
    total = math.prod(grid)

    def wrapped(*refs):
        ins, rins = refs[:ni], refs[ni:ni + ri]
        outs, routs = refs[ni + ri:ni + ri + no], refs[ni + ri + no:ni + ri + no + ro]
        scr, rscr = refs[ni + ri + no + ro:ni + ri + no + ro + ns], refs[ni + ri + no + ro + ns:]
        step = pl.program_id(0)
        for ax in range(1, len(grid)):
            step = step * grid[ax] + pl.program_id(ax)

        @pl.when(step == 0)
        def _():
            ride.start(rins, routs, rscr)

        if hasattr(ride, "relay"):
            @pl.when(step == (3 * total) // 4)
            def _():
                ride.relay(rins, routs, rscr)

        body(*ins, *outs, *scr)

        @pl.when(step == total - 1)
        def _():
            ride.finish(rins, routs, rscr)

    res = pl.pallas_call(
        wrapped, name=name, grid=grid, in_specs=list(in_specs) + [ANY] * ri, out_specs=list(out_specs) + [ANY] * ro,
        out_shape=list(out_shape) + list(ride.out_shape),
        scratch_shapes=list(scratch_shapes) + list(ride.scratch_shapes), input_output_aliases=aliases or {},
        compiler_params=_params(vmem_mb, ("arbitrary",) * len(grid)),
    )(*_in_hbm(*operands), *ride.in_arrays)
    return res[:no], res[no:]


def _in_proj_gather(x, g, shard, order, tm):
    t, d = x.shape
    ns = shard.shape[1]
    tm = min(tm, t)
    ni = t // tm
    ex = _GatherExchange([shard])

    def body(order_ref, x_ref, g_ref, shard_ref, proj_ref, xn_ref, full_ref, w_sem, *scr):
        del order_ref
        j, i = pl.program_id(0), pl.program_id(1)
        loads, stores, ici, d2d = ex._copies((shard_ref,), (full_ref,), scr)
        wbuf = scr[0]

        @pl.when(jnp.logical_and(j == 0, i == 0))
        def _():
            loads[0].start()
            for cp in ici:
                cp.start()
            loads[0].wait()
            stores[0].start()

        for k in range(3):
            @pl.when(jnp.logical_and(j == k + 1, i == 0))
            def _(k=k):
                if k == 0:
                    stores[0].wait()
                ici[k].wait_recv()
                d2d[k].start()
                d2d[k].wait_recv()
                px, py = _other_chips(lax.axis_index("x"), lax.axis_index("y"))[k]
                cp = pltpu.make_async_copy(full_ref.at[2 * px + py], wbuf, w_sem)
                cp.start()
                cp.wait()

        xv = x_ref[...]
        xn = (xv * _rs(xv) * g_ref[...]).astype(BF)

        @pl.when(j == 0)
        def _():
            xn_ref[...] = xn

        proj_ref[...] = _dot(xn, wbuf[...]).astype(BF)

        @pl.when(jnp.logical_and(j == 3, i == ni - 1))
        def _():
            for cp in ici + d2d:
                cp.wait_send()

    return pl.pallas_call(
        body, name="in_proj",
        grid_spec=pltpu.PrefetchScalarGridSpec(
            num_scalar_prefetch=1, grid=(N_CHIPS, ni),
            in_specs=[pl.BlockSpec((tm, d), lambda j, i, o: (i, 0)), pl.BlockSpec((1, d), lambda j, i, o: (0, 0)),
                      ANY],
            out_specs=[pl.BlockSpec((tm, ns), lambda j, i, o: (i, o[j])),
                       pl.BlockSpec((tm, d), lambda j, i, o: (jnp.where(j == 0, i, ni - 1), 0)), ANY],
            scratch_shapes=[pltpu.SemaphoreType.DMA] + list(ex.scratch_shapes)),
        out_shape=[jax.ShapeDtypeStruct((t, N_CHIPS * ns), BF), jax.ShapeDtypeStruct((t, d), BF)]
        + list(ex.out_shape),
        compiler_params=_params(32, ("arbitrary", "arbitrary")),
    )(order, x, g, shard)


def _neg_log_sig(z):
    n = jnp.maximum(z, 0.0) + jnp.log(1.0 + jnp.exp(-jnp.abs(z)))
    return n, z - n


def _running_sums(n, tri2):
    hi = n.astype(BF)
    lo = (n - hi.astype(F32)).astype(BF)
    return _dot(jnp.concatenate([hi, lo], axis=1), tri2)


def _head_sums(x, h0):
    s0 = jnp.sum(jnp.where(h0, x, 0.0), axis=-1, keepdims=True)
    s1 = jnp.sum(jnp.where(h0, 0.0, x), axis=-1, keepdims=True)
    return jnp.where(h0, s0, s1)


def _sb_setup(q_ref, tq):
    lane = lax.broadcasted_iota(jnp.int32, (tq, 128), 1)
    h0 = lane < HEAD_LANES
    qs = q_ref[...] * SB_SCALE
    zero = jnp.zeros_like(qs)
    qst = jnp.concatenate([jnp.where(h0, qs, zero), jnp.where(h0, zero, qs)], axis=0)
    r = lax.broadcasted_iota(jnp.int32, (2 * tq, tq), 0)
    c = lax.broadcasted_iota(jnp.int32, (2 * tq, tq), 1)
    causal = c < jnp.where(r >= tq, r - tq, r)
    return h0, qst, causal


def _tri(tq, op):
    return op(lax.broadcasted_iota(jnp.int32, (tq, tq), 0), lax.broadcasted_iota(jnp.int32, (tq, tq), 1)).astype(BF)


def _sb_fwd(proj, merged, hg, nb, s, tq, ride=None):
    t = nb * s
    tq = min(tq, s)
    nq = s // tq

    def body(q_ref, k_ref, v_ref, hg_ref, merged_ref, o_ref, tot_ref, mb_ref, nblk_ref, acc, cr):
        del merged_ref
        i = pl.program_id(2)
        h0, qst, causal = _sb_setup(q_ref, tq)
        tri_gt = _tri(tq, lambda r, c: r > c)
        tri_gt = jnp.concatenate([tri_gt, tri_gt], axis=0)

        def block(j, masked, c_in):
            start = pl.multiple_of(j * tq, tq)
            kj = k_ref[pl.ds(start, tq), :]
            vj = v_ref[pl.ds(start, tq), :]
            n, l = _neg_log_sig(_dot_bt(qst, kj))
            if masked:
                n = jnp.where(causal, n, 0.0)
            a = jnp.exp(l - (_running_sums(n, tri_gt) + c_in))
            if masked:
                a = jnp.where(causal, a, 0.0)
            return _dot(a.astype(BF), vj), c_in + jnp.sum(n, axis=-1, keepdims=True)

        @pl.when(i == 0)
        def _():
            acc[...], cr[...] = block(0, True, jnp.zeros((2 * tq, 1), F32))

        @pl.when(i > 0)
        def _():
            p_diag, c_diag = block(i, True, jnp.zeros((2 * tq, 1), F32))
            p_prev, c_prev = block(i - 1, False, c_diag)
            acc[...] = p_diag + p_prev
            cr[...] = c_prev

        def cond(carry):
            return jnp.logical_and(carry[0] < i, carry[1] < -SB_SKIP)

        def step(carry):
            p, c_new = block(i - 1 - carry[0], False, cr[...])
            acc[...] += p
            cr[...] = c_new
            return carry[0] + 1, jnp.min(c_new)

        walked, _ = lax.while_loop(cond, step, (jnp.minimum(i, 1), jnp.min(cr[...])))

        o = jnp.where(h0, acc[0:tq, :], acc[tq:2 * tq, :])
        o_ref[...] = o
        tot_ref[...] = jnp.where(h0, cr[0:tq, :], cr[tq:2 * tq, :])
        ro = lax.rsqrt(_head_sums(o * o, h0) * (1.0 / HEAD_LANES) + EPS)
        mb_ref[...] = (o * ro * hg_ref[...]).astype(BF)
        nblk_ref[...] = jnp.full((8, 128), walked.astype(F32))

    blk = lambda col0: pl.BlockSpec((tq, 128), lambda b, hp, i: (b * nq + i, col0 + hp))
    seq = lambda col0: pl.BlockSpec((s, 128), lambda b, hp, i: (b, col0 + hp))
    (o, tot, mb, nblk), rode = _ride_call(
        body, "sb_fwd", (nb, 4, nq),
        in_specs=[blk(8), seq(12), seq(16), pl.BlockSpec((1, 128), lambda b, hp, i: (0, 4 + hp)), ANY],
        out_specs=[blk(0), blk(0), blk(4), pl.BlockSpec((None, None, 8, 128), lambda b, hp, i: (b, hp, i, 0))],
        out_shape=[jax.ShapeDtypeStruct((t, 512), F32), jax.ShapeDtypeStruct((t, 512), F32),
                   jax.ShapeDtypeStruct((t, 1024), BF), jax.ShapeDtypeStruct((nb, 4, nq * 8, 128), F32)],
        scratch_shapes=[pltpu.VMEM((2 * tq, 128), F32), pltpu.VMEM((2 * tq, 1), F32)],
        operands=(proj, proj, proj, hg, merged), vmem_mb=40, ride=ride, aliases={4: 2})
    return o, tot, mb, nblk, rode


def _sb_bwd(proj, o_sb, tot, nblk, dmerged, dproj, hg, nb, s, tq, ride=None):
    t = nb * s
    tq = min(tq, s)
    nq = s // tq

    def body(q_ref, k_ref, v_ref, o_ref, tot_ref, nblk_ref, dm_ref, hg_ref, dproj_ref,
             dq_ref, dk_ref, dv_ref, dhg_ref, dk_acc, dv_acc, dq_acc, cm, cg):
        del dproj_ref
        i = pl.program_id(2)
        h0, qst, causal = _sb_setup(q_ref, tq)
        tri_le = _tri(tq, lambda r, c: r <= c)
        tri_le = jnp.concatenate([tri_le, tri_le], axis=0)
        tri_lt = _tri(tq, lambda r, c: r < c)

        @pl.when(i == 0)
        def _():
            dk_acc[...] = jnp.zeros_like(dk_acc)
            dv_acc[...] = jnp.zeros_like(dv_acc)
            dhg_ref[...] = jnp.zeros_like(dhg_ref)

        for ref in (dq_acc, cm, cg):
            ref[...] = jnp.zeros_like(ref)

        o = o_ref[...]
        ro = lax.rsqrt(_head_sums(o * o, h0) * (1.0 / HEAD_LANES) + EPS)
        oh = o * ro
        dm = dm_ref[...].astype(F32)
        dhg_ref[...] += jnp.sum(dm * oh, axis=0, keepdims=True)
        doh = dm * hg_ref[...]
        do = (ro * (doh - oh * (_head_sums(doh * oh, h0) * (1.0 / HEAD_LANES)))).astype(BF)
        zb = jnp.zeros_like(do)
        dost = jnp.concatenate([jnp.where(h0, do, zb), jnp.where(h0, zb, do)], axis=0)
        tots = jnp.concatenate([tot_ref[:, 0:1], tot_ref[:, HEAD_LANES:HEAD_LANES + 1]], axis=0)
        qst_t = qst.T
        dost_t = dost.T

        def block(j, masked, cm_in, cg_in):
            start = pl.multiple_of(j * tq, tq)
            kj = k_ref[pl.ds(start, tq), :]
            vj = v_ref[pl.ds(start, tq), :]
            n, l = _neg_log_sig(_dot_bt(qst, kj))
            if masked:
                n = jnp.where(causal, n, 0.0)
            a = jnp.exp(l - (tots - cm_in - _running_sums(n, tri_le)))
            if masked:
                a = jnp.where(causal, a, 0.0)
            gm = a * _dot_bt(dost, vj)
            pp = cg_in + _dot(gm.astype(BF), tri_lt)
            dz = gm - jnp.exp(l) * (gm + pp)
            if masked:
                dz = jnp.where(causal, dz, 0.0)
            dzb = dz.astype(BF)
            dk_acc[:, pl.ds(start, tq)] += _dot(qst_t, dzb)
            dv_acc[:, pl.ds(start, tq)] += _dot(dost_t, a.astype(BF))
            return (_dot(dzb, kj), cm_in + jnp.sum(n, axis=-1, keepdims=True),
                    cg_in + jnp.sum(gm, axis=-1, keepdims=True))

        def step(j, carry):
            dq, cm[...], cg[...] = block(j, False, cm[...], cg[...])
            dq_acc[...] += dq
            return carry

        walked = jnp.clip(jnp.max(nblk_ref[...]).astype(jnp.int32), jnp.minimum(i, 1), i)
        lax.fori_loop(i - walked, i - 1, step, 0)

        @pl.when(i == 0)
        def _():
            dq_acc[...] = block(0, True, cm[...], cg[...])[0]

        @pl.when(i > 0)
        def _():
            dq_prev, cm_prev, cg_prev = block(i - 1, False, cm[...], cg[...])
            dq_acc[...] += dq_prev + block(i, True, cm_prev, cg_prev)[0]

        dq_ref[...] = (jnp.where(h0, dq_acc[0:tq, :], dq_acc[tq:2 * tq, :]) * SB_SCALE).astype(BF)

        @pl.when(i == nq - 1)
        def _():
            dk_ref[...] = dk_acc[...].T.astype(BF)
            dv_ref[...] = dv_acc[...].T.astype(BF)

    blk = lambda col0: pl.BlockSpec((tq, 128), lambda b, hp, i: (b * nq + i, col0 + hp))
    seq = lambda col0: pl.BlockSpec((s, 128), lambda b, hp, i: (b, col0 + hp))
    (dq, dk, dv, dhg), rode = _ride_call(
        body, "sb_bwd", (nb, 4, nq),
        in_specs=[blk(8), seq(12), seq(16), blk(0), blk(0),
                  pl.BlockSpec((None, None, 8, 128), lambda b, hp, i: (b, hp, i, 0)), blk(4),
                  pl.BlockSpec((1, 128), lambda b, hp, i: (0, 4 + hp)), ANY],
        out_specs=[blk(8), seq(0), seq(0), pl.BlockSpec((None, 1, 128), lambda b, hp, i: (b, 0, hp))],
        out_shape=[jax.ShapeDtypeStruct((t, IN_COLS), BF), jax.ShapeDtypeStruct((t, 512), BF),
                   jax.ShapeDtypeStruct((t, 512), BF), jax.ShapeDtypeStruct((nb, 1, 512), F32)],
        scratch_shapes=[pltpu.VMEM((128, s), F32), pltpu.VMEM((128, s), F32), pltpu.VMEM((2 * tq, 128), F32),
                        pltpu.VMEM((2 * tq, 1), F32), pltpu.VMEM((2 * tq, 1), F32)],
        operands=(proj, proj, proj, o_sb, tot, nblk, dmerged, hg, dproj), vmem_mb=40, ride=ride, aliases={8: 0})
    return dq, dk, dv, dhg, rode


def _place(buf, piece, col_block, name):
    t, w = piece.shape
    tm = min(t, 1024)

    def body(piece_ref, buf_ref, out_ref):
        del buf_ref
        out_ref[...] = piece_ref[...]

    return pl.pallas_call(
        body, name=name, grid=(t // tm,),
        in_specs=[pl.BlockSpec((tm, w), lambda i: (i, 0)), ANY],
        out_specs=pl.BlockSpec((tm, w), lambda i: (i, col_block)),
        out_shape=jax.ShapeDtypeStruct(buf.shape, buf.dtype), input_output_aliases={1: 0},
        compiler_params=_params(16, ("arbitrary",)),
    )(piece, buf)


def _softmax_rows(sc):
    e = jnp.exp(sc - jnp.max(sc, axis=-1, keepdims=True))
    return e / jnp.sum(e, axis=-1, keepdims=True)


def _mix_cross_fwd(x, merged, w_out, gc, w_cq, kv, w_co, s, tm):
    t, d = x.shape
    tm = min(tm, s)
    per = s // tm
    inv = 1.0 / math.sqrt(X_HEAD_DIM)

    def body(x_ref, m_ref, wo_ref, gc_ref, wq_ref, kv_ref, wc_ref, h1_ref, h2_ref, hn_ref, qc_ref, oc_ref):
        h1 = x_ref[...] + _dot(m_ref[...], wo_ref[...])
        h1_ref[...] = h1
        hn = (h1 * _rs(h1) * gc_ref[...]).astype(BF)
        hn_ref[...] = hn
        qc = _dot(hn, wq_ref[...]).astype(BF)
        qc_ref[...] = qc
        for h in range(X_HEADS):
            cols = slice(h * X_HEAD_DIM, (h + 1) * X_HEAD_DIM)
            kh = kv_ref[:, h * X_HEAD_DIM:(h + 1) * X_HEAD_DIM]
            vh = kv_ref[:, d + h * X_HEAD_DIM:d + (h + 1) * X_HEAD_DIM]
            p = _softmax_rows(_dot_bt(qc[:, cols], kh) * inv)
            oc_ref[:, cols] = _dot(p.astype(BF), vh).astype(BF)
        h2_ref[...] = h1 + _dot(oc_ref[...], wc_ref[...])

    row = lambda width: pl.BlockSpec((tm, width), lambda i: (i, 0))
    full = lambda a, b: pl.BlockSpec((a, b), lambda i: (0, 0))
    return pl.pallas_call(
        body, name="mix_cross_fwd", grid=(t // tm,),
        in_specs=[row(d), row(d), full(d, d), full(1, d), full(d, d),
                  pl.BlockSpec((N_MEM, 2 * d), lambda i: (i // per, 0)), full(d, d)],
        out_specs=[row(d), row(d), row(d), row(d), row(d)],
        out_shape=[jax.ShapeDtypeStruct((t, d), F32), jax.ShapeDtypeStruct((t, d), F32),
                   jax.ShapeDtypeStruct((t, d), BF), jax.ShapeDtypeStruct((t, d), BF),
                   jax.ShapeDtypeStruct((t, d), BF)],
        compiler_params=_params(48, ("arbitrary",)),
    )(*_in_hbm(x, merged, w_out, gc, w_cq, kv, w_co))


def _cross_bwd(dh2, h1, qc, gc, w_cq, kv, w_co, s, tm):
    t, d = dh2.shape
    tm = min(tm, s)
    per = s // tm
    nb = t // s
    inv = 1.0 / math.sqrt(X_HEAD_DIM)

    def body(dh2_ref, h1_ref, qc_ref, gc_ref, wq_ref, kv_ref, wc_ref, dh1_ref, dqc_ref, dkv_ref, dgc_ref):
        i = pl.program_id(0)

        @pl.when(i == 0)
        def _():
            dgc_ref[...] = jnp.zeros_like(dgc_ref)

        @pl.when(i % per == 0)
        def _():
            dkv_ref[...] = jnp.zeros_like(dkv_ref)

        dh2 = dh2_ref[...]
        h1 = h1_ref[...]
        r = _rs(h1)
        h1h = h1 * r
        gcv = gc_ref[...]
        qc = qc_ref[...]
        do = _dot_bt(dh2.astype(BF), wc_ref[...]).astype(BF)
        for h in range(X_HEADS):
            cols = slice(h * X_HEAD_DIM, (h + 1) * X_HEAD_DIM)
            vcols = slice(d + h * X_HEAD_DIM, d + (h + 1) * X_HEAD_DIM)
            kh = kv_ref[:, cols]
            vh = kv_ref[:, vcols]
            p = _softmax_rows(_dot_bt(qc[:, cols], kh) * inv)
            dp = _dot_bt(do[:, cols], vh)
            ds = (p * (dp - jnp.sum(dp * p, axis=-1, keepdims=True)) * inv).astype(BF)
            dqc_ref[:, cols] = _dot(ds, kh).astype(BF)
            dkv_ref[:, cols] += _dot_at(ds, qc[:, cols])
            dkv_ref[:, vcols] += _dot_at(p.astype(BF), do[:, cols])
        dhn = _dot_bt(dqc_ref[...], wq_ref[...])
        dx, dg = _rms_bwd(dhn, h1h, r, gcv)
        dh1_ref[...] = dh2 + dx
        dgc_ref[...] += jnp.sum(dg, axis=0, keepdims=True)

    row = lambda width: pl.BlockSpec((tm, width), lambda i: (i, 0))
    full = lambda a, b: pl.BlockSpec((a, b), lambda i: (0, 0))
    kvspec = pl.BlockSpec((N_MEM, 2 * d), lambda i: (i // per, 0))
    return pl.pallas_call(
        body, name="cross_bwd", grid=(t // tm,),
        in_specs=[row(d), row(d), row(d), full(1, d), full(d, d), kvspec, full(d, d)],
        out_specs=[row(d), row(d), kvspec, full(1, d)],
        out_shape=[jax.ShapeDtypeStruct((t, d), F32), jax.ShapeDtypeStruct((t, d), BF),
                   jax.ShapeDtypeStruct((nb * N_MEM, 2 * d), F32), jax.ShapeDtypeStruct((1, d), F32)],
        compiler_params=_params(48, ("arbitrary",)),
    )(*_in_hbm(dh2, h1, qc, gc, w_cq, kv, w_co))


def _mem_bwd(mem, gm, dkv, w_ckv, tm):
    t, d = mem.shape
    tm = min(tm, t)

    def body(mem_ref, dkv_ref, w_ref, dg_ref):
        @pl.when(pl.program_id(0) == 0)
        def _():
            dg_ref[...] = jnp.zeros_like(dg_ref)

        mv = mem_ref[...]
        dmn = _dot_bt(dkv_ref[...].astype(BF), w_ref[...])
        dg_ref[...] += jnp.sum(dmn * (mv * _rs(mv)), axis=0, keepdims=True)

    del gm
    return pl.pallas_call(
        body, name="mem_bwd", grid=(t // tm,),
        in_specs=[pl.BlockSpec((tm, d), lambda i: (i, 0)), pl.BlockSpec((tm, 2 * d), lambda i: (i, 0)),
                  pl.BlockSpec((d, 2 * d), lambda i: (0, 0))],
        out_specs=pl.BlockSpec((1, d), lambda i: (0, 0)),
        out_shape=jax.ShapeDtypeStruct((1, d), F32),
        compiler_params=_params(32, ("arbitrary",)),
    )(mem, dkv, w_ckv)


def _ffn_loss_fwd(h2, gf, w1, w2, gl, target, tm):
    t, d = h2.shape
    tm = min(tm, t)

    def body(h2_ref, gf_ref, w1_ref, w2_ref, gl_ref, tg_ref, hn_ref, f_ref, dh3_ref, dgl_ref, loss_ref):
        @pl.when(pl.program_id(0) == 0)
        def _():
            dgl_ref[...] = jnp.zeros_like(dgl_ref)
            loss_ref[...] = jnp.zeros_like(loss_ref)

        h2 = h2_ref[...]
        hn = (h2 * _rs(h2) * gf_ref[...]).astype(BF)
        hn_ref[...] = hn
        h3 = h2
        for c in range(4):
            f = jnp.maximum(_dot(hn, w1_ref[c]), 0.0)
            f_ref[:, c * 1024:(c + 1) * 1024] = f.astype(BF)
            h3 = h3 + _dot((f * f).astype(BF), w2_ref[c])
        r3 = _rs(h3)
        yh = h3 * r3
        glv = gl_ref[...]
        e = yh * glv - tg_ref[...]
        loss_ref[...] += 0.5 * jnp.sum(jnp.sum(e * e, axis=-1, keepdims=True) * (1.0 / d), axis=0, keepdims=True)
        dy = e * (1.0 / d)
        dx, dg = _rms_bwd(dy, yh, r3, glv)
        dh3_ref[...] = dx
        dgl_ref[...] += jnp.sum(dg, axis=0, keepdims=True)

    row = lambda width: pl.BlockSpec((tm, width), lambda i: (i, 0))
    return pl.pallas_call(
        body, name="ffn_loss_fwd", grid=(t // tm,),
        in_specs=[row(d), pl.BlockSpec((1, d), lambda i: (0, 0)), pl.BlockSpec((4, d, 1024), lambda i: (0, 0, 0), pipeline_mode=pl.Buffered(1)),
                  pl.BlockSpec((4, 1024, d), lambda i: (0, 0, 0), pipeline_mode=pl.Buffered(1)),
                  pl.BlockSpec((1, d), lambda i: (0, 0)), row(d)],
        out_specs=[row(d), row(D_FF), row(d), pl.BlockSpec((1, d), lambda i: (0, 0)),
                   pl.BlockSpec((1, 1), lambda i: (0, 0))],
        out_shape=[jax.ShapeDtypeStruct((t, d), BF), jax.ShapeDtypeStruct((t, D_FF), BF),
                   jax.ShapeDtypeStruct((t, d), F32), jax.ShapeDtypeStruct((1, d), F32),
                   jax.ShapeDtypeStruct((1, 1), F32)],
        compiler_params=_params(56, ("arbitrary",)),
    )(*_in_hbm(h2, gf, w1, w2, gl, target))


def _ffn_bwd(dh3, f, h2, gf, w1, w2, tm):
    t, d = h2.shape
    tm = min(tm, t)

    def body(dh3_ref, f_ref, h2_ref, gf_ref, w1_ref, w2_ref, dh2_ref, dpre_ref, dgf_ref):
        @pl.when(pl.program_id(0) == 0)
        def _():
            dgf_ref[...] = jnp.zeros_like(dgf_ref)

        dh3 = dh3_ref[...]
        dh3b = dh3.astype(BF)
        dhn = jnp.zeros((tm, d), F32)
        for c in range(4):
            cols = slice(c * 1024, (c + 1) * 1024)
            dpre = (_dot_bt(dh3b, w2_ref[c]) * (2.0 * f_ref[:, cols].astype(F32))).astype(BF)
            dpre_ref[:, cols] = dpre
            dhn = dhn + _dot_bt(dpre, w1_ref[c])
        h2 = h2_ref[...]
        r = _rs(h2)
        dx, dg = _rms_bwd(dhn, h2 * r, r, gf_ref[...])
        dh2_ref[...] = dh3 + dx
        dgf_ref[...] += jnp.sum(dg, axis=0, keepdims=True)

    row = lambda width: pl.BlockSpec((tm, width), lambda i: (i, 0))
    return pl.pallas_call(
        body, name="ffn_bwd", grid=(t // tm,),
        in_specs=[row(d), row(D_FF), row(d), pl.BlockSpec((1, d), lambda i: (0, 0)),
                  pl.BlockSpec((4, d, 1024), lambda i: (0, 0, 0), pipeline_mode=pl.Buffered(1)),
                  pl.BlockSpec((4, 1024, d), lambda i: (0, 0, 0), pipeline_mode=pl.Buffered(1))],
        out_specs=[row(d), row(D_FF), pl.BlockSpec((1, d), lambda i: (0, 0))],
        out_shape=[jax.ShapeDtypeStruct((t, d), F32), jax.ShapeDtypeStruct((t, D_FF), BF),
                   jax.ShapeDtypeStruct((1, d), F32)],
        compiler_params=_params(56, ("arbitrary",)),
    )(*_in_hbm(dh3, f, h2, gf, w1, w2))


def _in_bwd(dproj, dh1, x, g, w_in, tm, ride=None):
    t, d = x.shape
    n = w_in.shape[1]
    tm = min(tm, t)

    def body(dp_ref, dh1_ref, x_ref, g_ref, w_ref, dx_ref, dg_ref):
        @pl.when(pl.program_id(0) == 0)
        def _():
            dg_ref[...] = jnp.zeros_like(dg_ref)

        dxn = _dot_bt(dp_ref[...], w_ref[...])
        xv = x_ref[...]
        r = _rs(xv)
        dx, dg = _rms_bwd(dxn, xv * r, r, g_ref[...])
        dx_ref[...] = dh1_ref[...] + dx
        dg_ref[...] += jnp.sum(dg, axis=0, keepdims=True)

    row = lambda width: pl.BlockSpec((tm, width), lambda i: (i, 0))
    (dx, dg), rode = _ride_call(
        body, "in_bwd", (t // tm,),
        in_specs=[row(n), row(d), row(d), pl.BlockSpec((1, d), lambda i: (0, 0)),
                  pl.BlockSpec((d, n), lambda i: (0, 0))],
        out_specs=[row(d), pl.BlockSpec((1, d), lambda i: (0, 0))],
        out_shape=[jax.ShapeDtypeStruct((t, d), F32), jax.ShapeDtypeStruct((1, d), F32)],
        scratch_shapes=[], operands=(dproj, dh1, x, g, w_in), vmem_mb=48, ride=ride)
    return dx, dg, rode


class _GradReduce:
    def __init__(self, c_idx):
        self.c_idx = c_idx
        self.sums = {}

    def sibling(self, slabs):
        return _SiblingExchange(slabs)

    def chip(self, names, slabs, recv):
        for k, a, r in zip(names, slabs, recv):
            self.sums[k] = _chip_sum(a, r, self.c_idx, "chip_sum_" + k)
        return _ChipExchange([self.sums[k] for k in names])


def _full_weights(gathered):
    d = D_MODEL
    out = {}
    for k, a in gathered.items():
        if k in ("w_in", "w_ckv", "w_ff1"):
            out[k] = a.transpose(1, 0, 2).reshape(d, -1)
        else:
            out[k] = a.reshape(-1, d)
    return out


def _slabs(a):
    return a.reshape(N_DEV, -1, a.shape[-1])


def _local_step(x, mem, target, small, big, nb, s, tq=256, gather_in=None, gather_rest=None, reduce=None):
    d = D_MODEL
    g_mix, g_v, w_sp, b_sp, g_head, g_cross, g_mem, g_ffn, g_fin = (
        small[k] for k in ("norm_mix_g", "gm_v_norm_g", "w_spatial", "b_spatial", "head_norm_g", "norm_cross_g",
                           "norm_mem_g", "norm_ffn_g", "norm_final_g"))
    tri = jnp.tril(jnp.ones((CHUNK, CHUNK), dtype=bool))
    w_sp_m = jnp.where(tri[None], w_sp, 0.0)
    wt = w_sp_m.astype(BF)
    wtt = jnp.swapaxes(w_sp_m, 1, 2).astype(BF)
    bb = jnp.broadcast_to(b_sp[:, :, None], (GM_GROUPS, CHUNK, CHUNK))
    hg_a = g_head[:, :GM_WIDTH]

    if gather_in is not None:
        proj, xn, w_in = _in_proj_gather(x, g_mix, gather_in[0], gather_in[1], 512)
        big = dict(big, **_full_weights({"w_in": w_in}))
    else:
        proj, xn = _norm_matmul(x, g_mix, big["w_in"], 512, "in_proj")
    merged = _gmlp_fwd(proj, g_v, wt, bb, hg_a, 512)
    o_sb, tot, merged, nblk, gathered = _sb_fwd(proj, merged, g_head, nb, s, tq, ride=gather_rest)
    if gather_rest is not None:
        big = dict(big, **_full_weights(dict(zip(BIG[1:], gathered))))
    w1c = big["w_ff1"].reshape(d, 4, 1024).transpose(1, 0, 2)
    w2c = big["w_ff2"].reshape(4, 1024, d)
    kv, memn = _norm_matmul(mem, g_mem, big["w_ckv"], 512, "mem_proj")
    h1, h2, hn, qc, oc = _mix_cross_fwd(x, merged, big["w_out"], g_cross, big["w_cq"], kv, big["w_co"], s, 512)
    hn2, f, dh3, d_fin, loss = _ffn_loss_fwd(h2, g_ffn, w1c, w2c, g_fin, target, 512)

    gbig = {}
    dh2, dpre, d_ffn = _ffn_bwd(dh3, f, h2, g_ffn, w1c, w2c, 512)
    gbig["w_ff2"] = _slabs(_wgrad(f, dh3, 1024, 1024, "wgrad_ff2", square_a=True))
    gbig["w_ff1"] = _slabs(_wgrad_wide(hn2, dpre, 512, 1024, "wgrad_ff1", col_shards=4))
    dh1, dqc, dkv, d_cross = _cross_bwd(dh2, h1, qc, g_cross, big["w_cq"], kv, big["w_co"], s, 512)
    gbig["w_co"] = _slabs(_wgrad(oc, dh2, 1024, 1024, "wgrad_co"))
    gbig["w_cq"] = _slabs(_wgrad(hn, dqc, 1024, 1024, "wgrad_cq"))
    gbig["w_ckv"] = _slabs(_wgrad(memn, dkv, 512, 1024, "wgrad_ckv", col_shards=4))
    d_mem = _mem_bwd(mem, g_mem, dkv, big["w_ckv"], 512)
    dmerged = _matmul_bt(dh1, big["w_out"], 512, "out_bwd")
    gbig["w_out"] = _slabs(_wgrad(merged, dh1, 1024, 1024, "wgrad_out"))
    rest = BIG[1:]
    ride = reduce.sibling([gbig[k] for k in rest]) if reduce else None
    dproj, d_wsp, d_bb, d_gv, d_hga, recv = _gmlp_bwd(proj, dmerged, g_v, wt, wtt, bb, hg_a, 512, ride=ride)
    ride = reduce.chip(rest, [gbig[k] for k in rest], recv) if reduce else None
    dproj, dk, dv, d_hgb, parts_rest = _sb_bwd(proj, o_sb, tot, nblk, dmerged, dproj, g_head, nb, s, tq, ride=ride)
    dproj = _place(_place(dproj, dk, 3, "place_dk"), dv, 4, "place_dv")
    gbig["w_in"] = _slabs(_wgrad_wide(xn, dproj, 512, 1024, "wgrad_in", col_shards=4))
    last = None
    if reduce:
        recv = _run_exchange(reduce.sibling([gbig["w_in"]]), "grad_sibling_exchange_w_in")
        last = reduce.chip(["w_in"], [gbig["w_in"]], recv)
    grad_x, d_mix, _ = _in_bwd(dproj, dh1, x, g_mix, big["w_in"], 512)
    parts = dict(zip(rest, parts_rest))

    gsmall = {
        "norm_mix_g": d_mix, "gm_v_norm_g": d_gv, "w_spatial": d_wsp, "b_spatial": d_bb[:, :, 0],
        "head_norm_g": jnp.concatenate([d_hga, jnp.sum(d_hgb, axis=0)], axis=1), "norm_cross_g": d_cross,
        "norm_mem_g": d_mem, "norm_ffn_g": d_ffn, "norm_final_g": d_fin,
    }
    return loss, grad_x, gsmall, gbig, parts, last


BIG = ("w_in", "w_out", "w_cq", "w_ckv", "w_co", "w_ff1", "w_ff2")
SMALL = ("norm_mix_g", "gm_v_norm_g", "w_spatial", "b_spatial", "head_norm_g", "norm_cross_g", "norm_mem_g",
         "norm_ffn_g", "norm_final_g")


def _local_copies_start(srcs, stages, sems):
    loads = [pltpu.make_async_copy(src, stage, sems.at[w]) for w, (src, stage) in enumerate(zip(srcs, stages))]
    for ld in loads:
        ld.start()
    return loads


def _local_copies_finish(loads, stages, dsts, sems):
    stores = []
    for w, (ld, stage, dst) in enumerate(zip(loads, stages, dsts)):
        ld.wait()
        st = pltpu.make_async_copy(stage, dst, sems.at[w])
        st.start()
        stores.append(st)
    for st in stores:
        st.wait()


def _chip_sum(slabs, recv, c_idx, name):
    _, r, cw = slabs.shape
    tr = min(r, 256)

    def body(c_ref, a_ref, b_ref, o_ref):
        del c_ref
        o_ref[...] = (a_ref[...] + b_ref[...]).astype(BF)

    return pl.pallas_call(
        body, name=name,
        grid_spec=pltpu.PrefetchScalarGridSpec(
            num_scalar_prefetch=1, grid=(N_CHIPS, r // tr),
            in_specs=[pl.BlockSpec((None, tr, cw), lambda p, i, c_ref: (2 * p + c_ref[0], i, 0)),
                      pl.BlockSpec((None, tr, cw), lambda p, i, c_ref: (p, i, 0))],
            out_specs=pl.BlockSpec((None, tr, cw), lambda p, i, c_ref: (p, i, 0))),
        out_shape=jax.ShapeDtypeStruct((N_CHIPS, r, cw), BF),
        compiler_params=_params(32, ("arbitrary", "arbitrary")),
    )(c_idx, *_in_hbm(slabs, recv))


def _sum4(sums, parts, q_idx, name):
    _, r, cw = parts.shape
    tr = min(r, 256)

    def body(q_ref, own_ref, a_ref, b_ref, c_ref, o_ref):
        del q_ref
        o_ref[...] = ((own_ref[...].astype(F32) + a_ref[...].astype(F32)) + b_ref[...].astype(F32)) + c_ref[
            ...].astype(F32)

    spec = lambda k: pl.BlockSpec((None, tr, cw), lambda i, q_ref: ((q_ref[0] + k) % N_CHIPS, i, 0))
    return pl.pallas_call(
        body, name=name,
        grid_spec=pltpu.PrefetchScalarGridSpec(
            num_scalar_prefetch=1, grid=(r // tr,), in_specs=[spec(0), spec(1), spec(2), spec(3)],
            out_specs=pl.BlockSpec((tr, cw), lambda i, q_ref: (i, 0))),
        out_shape=jax.ShapeDtypeStruct((r, cw), F32),
        compiler_params=_params(32, ("arbitrary",)),
    )(q_idx, *_in_hbm(sums, parts, parts, parts))


def _half_exchange(halves):
    n = len(halves)

    def body(*refs):
        ins, outs, stages = refs[:n], refs[n:2 * n], refs[2 * n:3 * n]
        send_sems, recv_sems, ld_sems, st_sems = refs[3 * n:]
        x, y, c = lax.axis_index("x"), lax.axis_index("y"), lax.axis_index("c")
        loads = _local_copies_start(ins, stages, ld_sems)
        copies = []
        for w in range(n):
            cp = pltpu.make_async_remote_copy(
                src_ref=ins[w], dst_ref=outs[w].at[c], send_sem=send_sems.at[w], recv_sem=recv_sems.at[w],
                device_id=(x, y, 1 - c), device_id_type=MESH)
            cp.start()
            copies.append(cp)
        _local_copies_finish(loads, stages, [outs[w].at[c] for w in range(n)], st_sems)
        for cp in copies:
            cp.wait()

    return pl.pallas_call(
        body, name="grad_half_exchange",
        in_specs=[ANY] * n, out_specs=[ANY] * n,
        out_shape=[jax.ShapeDtypeStruct((2,) + a.shape, a.dtype) for a in halves],
        scratch_shapes=[pltpu.VMEM(a.shape, a.dtype) for a in halves] + [
            pltpu.SemaphoreType.DMA((n,)), pltpu.SemaphoreType.DMA((n,)),
            pltpu.SemaphoreType.DMA((n,)), pltpu.SemaphoreType.DMA((n,))],
        compiler_params=_params(24),
    )(*halves)


def _small_all_reduce(packed, ride=None):
    rows = packed.shape[0]
    ride = ride or _NoExchange()
    ri, ro = len(ride.in_arrays), len(ride.out_shape)

    def body(*refs):
        in_ref, rins, out_ref, routs = refs[0], refs[1:1 + ri], refs[1 + ri], refs[2 + ri:2 + ri + ro]
        pair, chip_sum, chips, d2d_send, d2d_recv, ici_send, ici_recv = refs[2 + ri + ro:9 + ri + ro]
        rscr = refs[9 + ri + ro:]
        ride.start(rins, routs, rscr)
        x, y, c = lax.axis_index("x"), lax.axis_index("y"), lax.axis_index("c")
        q = 2 * x + y
        pair[c] = in_ref[...]
        swap = pltpu.make_async_remote_copy(
            src_ref=in_ref, dst_ref=pair.at[c], send_sem=d2d_send, recv_sem=d2d_recv,
            device_id=(x, y, 1 - c), device_id_type=MESH)
        swap.start()
        swap.wait()
        both = pair[0] + pair[1]
        chip_sum[...] = both
        chips[q] = both
        copies = [pltpu.make_async_remote_copy(
            src_ref=chip_sum, dst_ref=chips.at[q], send_sem=ici_send.at[k], recv_sem=ici_recv.at[k],
            device_id=(px, py, c), device_id_type=MESH) for k, (px, py) in enumerate(_other_chips(x, y))]
        for cp in copies:
            cp.start()
        for cp in copies:
            cp.wait()
        out_ref[...] = ((chips[0] + chips[1]) + chips[2]) + chips[3]
        ride.finish(rins, routs, rscr)

    vmem = pl.BlockSpec(memory_space=pltpu.VMEM)
    res = pl.pallas_call(
        body, name="small_all_reduce",
        in_specs=[vmem] + [ANY] * ri, out_specs=[vmem] + [ANY] * ro,
        out_shape=[jax.ShapeDtypeStruct(packed.shape, F32)] + list(ride.out_shape),
        scratch_shapes=[pltpu.VMEM((2, rows, 128), F32), pltpu.VMEM((rows, 128), F32),
                        pltpu.VMEM((N_CHIPS, rows, 128), F32), pltpu.SemaphoreType.DMA, pltpu.SemaphoreType.DMA,
                        pltpu.SemaphoreType.DMA((3,)), pltpu.SemaphoreType.DMA((3,))] + list(ride.scratch_shapes),
        compiler_params=_params(16),
    )(packed, *ride.in_arrays)
    return res[0], res[1:]


def _adamw(g, w, m, v, name):
    r, cw = g.shape
    tr = 256 if r % 256 == 0 else r

    def body(g_ref, w_ref, m_ref, v_ref, d_ref, nm_ref, nv_ref):
        gv = g_ref[...]
        nm = ADAM_B1 * m_ref[...] + (1.0 - ADAM_B1) * gv
        nv = ADAM_B2 * v_ref[...] + (1.0 - ADAM_B2) * (gv * gv)
        m_hat = nm / (1.0 - ADAM_B1 ** ADAM_STEP)
        v_hat = nv / (1.0 - ADAM_B2 ** ADAM_STEP)
        d_ref[...] = -ADAM_LR * (m_hat / (jnp.sqrt(v_hat) + ADAM_EPS) + ADAM_WD * w_ref[...])
        nm_ref[...] = nm
        nv_ref[...] = nv

    spec = pl.BlockSpec((tr, cw), lambda i: (i, 0))
    return pl.pallas_call(
        body, name=name, grid=(r // tr,),
        in_specs=[spec] * 4, out_specs=[spec] * 3,
        out_shape=[jax.ShapeDtypeStruct((r, cw), F32)] * 3,
        compiler_params=_params(32, ("arbitrary",)),
    )(*_in_hbm(g, w, m, v))


def _small_params(args):
    small = {k: args[k].reshape(1, -1) for k in SMALL}
    small["w_spatial"] = args["w_spatial"][0]
    small["b_spatial"] = args["b_spatial"][0]
    return small


def _pack(parts, rows):
    flat = jnp.concatenate([p.reshape(-1).astype(F32) for p in parts])
    return jnp.pad(flat, (0, rows * 128 - flat.shape[0])).reshape(rows, 128)


def _unpack(packed, shapes):
    flat = packed.reshape(-1)
    out, off = [], 0
    for shp in shapes:
        size = math.prod(shp)
        out.append(flat[off:off + size].reshape(shp))
        off += size
    return out


def kernel(x, mem, norm_mix_g, w_in, gm_v_norm_g, w_spatial, b_spatial, head_norm_g, w_out, norm_cross_g, norm_mem_g, w_cq, w_ckv, w_co, norm_ffn_g, w_ff1, w_ff2, norm_final_g, loss_target, m_norm_mix_g, m_w_in, m_gm_v_norm_g, m_w_spatial, m_b_spatial, m_head_norm_g, m_w_out, m_norm_cross_g, m_norm_mem_g, m_w_cq, m_w_ckv, m_w_co, m_norm_ffn_g, m_w_ff1, m_w_ff2, m_norm_final_g, v_norm_mix_g, v_w_in, v_gm_v_norm_g, v_w_spatial, v_b_spatial, v_head_norm_g, v_w_out, v_norm_cross_g, v_norm_mem_g, v_w_cq, v_w_ckv, v_w_co, v_norm_ffn_g, v_w_ff1, v_w_ff2, v_norm_final_g):
    args = dict(locals())
    d = D_MODEL
    nb, s, _ = x.shape
    c_idx = lax.axis_index("c").astype(jnp.int32).reshape(1)
    q_idx = (2 * lax.axis_index("x") + lax.axis_index("y")).astype(jnp.int32).reshape(1)
    rest = BIG[1:]

    shards = {k: args[k][0].astype(BF) for k in BIG}
    order = jnp.bitwise_xor(q_idx, jnp.array([0, 2, 1, 3], jnp.int32))
    gather_rest = _GatherExchange([shards[k] for k in rest])

    reduce = _GradReduce(c_idx)
    loss, grad_x, gsmall, _, parts, last = _local_step(
        x.reshape(nb * s, d), mem.reshape(nb * N_MEM, d), loss_target.reshape(nb * s, d), _small_params(args), {},
        nb, s, gather_in=(shards["w_in"], order), gather_rest=gather_rest, reduce=reduce)

    shapes = [args[k].shape for k in SMALL]
    n_small = sum(math.prod(sh) for sh in shapes)
    rows = -(-(n_small + 1) // 1024) * 8
    reduced, (parts["w_in"],) = _small_all_reduce(_pack([gsmall[k] for k in SMALL] + [loss], rows), ride=last)
    halves = [_sum4(reduce.sums[k], parts[k], q_idx, "sum4_" + k) for k in BIG]
    both = _half_exchange(halves)

    out = {"grad_x": grad_x.reshape(nb, s, d)}
    for k, g2 in zip(BIG, both):
        shp = args[k].shape
        g = g2.reshape(shp[1], shp[2])
        dl, nm, nv = _adamw(g, args[k][0], args["m_" + k][0], args["v_" + k][0], "adamw_" + k)
        out["grad_" + k], out["delta_" + k], out["new_m_" + k], out["new_v_" + k] = (
            a.reshape(shp) for a in (g, dl, nm, nv))

    dl, nm, nv = _adamw(reduced, _pack([args[k] for k in SMALL], rows), _pack([args["m_" + k] for k in SMALL], rows),
                        _pack([args["v_" + k] for k in SMALL], rows), "adamw_small")
    for name, arr in (("grad_", reduced), ("delta_", dl), ("new_m_", nm), ("new_v_", nv)):
        for k, a in zip(SMALL, _unpack(arr, shapes)):
            out[name + k] = a
    out["loss"] = reduced.reshape(-1)[n_small]

    names = ["norm_mix_g", "w_in", "gm_v_norm_g", "w_spatial", "b_spatial", "head_norm_g", "w_out", "norm_cross_g",
             "norm_mem_g", "w_cq", "w_ckv", "w_co", "norm_ffn_g", "w_ff1", "w_ff2", "norm_final_g"]
    return (out["loss"], out["grad_x"], *[out["grad_" + k] for k in names], *[out["delta_" + k] for k in names],
            *[out["new_m_" + k] for k in names], *[out["new_v_" + k] for k in names])
```

```python
import functools
import math

import jax
import jax.numpy as jnp
from jax import lax
from jax.experimental import pallas as pl
from jax.experimental.pallas import tpu as pltpu

F32 = jnp.float32
BF = jnp.bfloat16

EPS = 1e-6
D_MODEL = 1024
CHUNK = 128
GM_GROUPS = 4
GM_WIDTH = 512
SB_WIDTH = 512
HEAD_LANES = 64
SB_SCALE = 0.125
SB_SKIP = -104.0
X_HEADS = 4
X_HEAD_DIM = 256
N_MEM = 256
D_FF = 4096
IN_COLS = 2560
N_CHIPS = 4
N_DEV = 8

ADAM_LR = 0.001
ADAM_B1 = 0.9
ADAM_B2 = 0.999
ADAM_EPS = 1e-08
ADAM_WD = 0.01
ADAM_STEP = 10

V7X_VMEM_BYTES = 64 * 1024 * 1024
MESH = pl.DeviceIdType.MESH
ANY = pl.BlockSpec(memory_space=pl.ANY)

GELU_C = math.sqrt(2.0 / math.pi)
GELU_A = 0.044715


def _params(vmem_mb, sem=None):
    assert vmem_mb * 1024 * 1024 <= V7X_VMEM_BYTES
    return pltpu.CompilerParams(vmem_limit_bytes=vmem_mb * 1024 * 1024, dimension_semantics=sem)


PIN_MIN_ELEMENTS = 1 << 18


def _in_hbm(*arrays):
    return tuple(pltpu.with_memory_space_constraint(a, pltpu.HBM) if a.size >= PIN_MIN_ELEMENTS else a
                 for a in arrays)


def _dot(a, b):
    return jnp.dot(a, b, preferred_element_type=F32)


def _dot_bt(a, b):
    return lax.dot_general(a, b, (((1,), (1,)), ((), ())), preferred_element_type=F32)


def _dot_at(a, b):
    return lax.dot_general(a, b, (((0,), (0,)), ((), ())), preferred_element_type=F32)


def _gelu(x):
    t = jnp.tanh(GELU_C * (x + GELU_A * x * x * x))
    return 0.5 * x * (1.0 + t)


def _gelu_and_grad(x):
    x2 = x * x
    t = jnp.tanh(GELU_C * (x + GELU_A * x2 * x))
    h = 0.5 * (1.0 + t)
    return x * h, h + 0.5 * x * (1.0 - t * t) * (GELU_C * (1.0 + 3.0 * GELU_A * x2))


def _rs(x):
    return lax.rsqrt(jnp.mean(x * x, axis=-1, keepdims=True) + EPS)


def _rms_bwd(dxn, xhat, r, g):
    dxh = dxn * g
    dx = r * (dxh - xhat * jnp.mean(dxh * xhat, axis=-1, keepdims=True))
    return dx, dxn * xhat


def _norm_matmul(x, g, w, tm, name):
    t, d = x.shape
    n = w.shape[1]
    tm = min(tm, t)

    def body(x_ref, g_ref, w_ref, out_ref, xn_ref):
        xv = x_ref[...]
        xn = (xv * _rs(xv) * g_ref[...]).astype(BF)
        xn_ref[...] = xn
        out_ref[...] = _dot(xn, w_ref[...]).astype(out_ref.dtype)

    return pl.pallas_call(
        body, name=name, grid=(t // tm,),
        in_specs=[pl.BlockSpec((tm, d), lambda i: (i, 0)), pl.BlockSpec((1, d), lambda i: (0, 0)),
                  pl.BlockSpec((d, n), lambda i: (0, 0))],
        out_specs=[pl.BlockSpec((tm, n), lambda i: (i, 0)), pl.BlockSpec((tm, d), lambda i: (i, 0))],
        out_shape=[jax.ShapeDtypeStruct((t, n), BF), jax.ShapeDtypeStruct((t, d), BF)],
        compiler_params=_params(48, ("arbitrary",)),
    )(*_in_hbm(x, g, w))


def _wgrad(a, g, tn, tk, name, square_a=False, col_shards=1):
    t, m = a.shape
    n = g.shape[1]
    tk = min(tk, t)
    tm = min(m, 1024)
    ns = n // col_shards
    assert ns % tn == 0 and m % tm == 0
    per = ns // tn
    nk = t // tk

    def body(a_ref, g_ref, o_ref):
        k = pl.program_id(2)

        @pl.when(k == 0)
        def _():
            o_ref[...] = jnp.zeros_like(o_ref)

        av = a_ref[...]
        if square_a:
            af = av.astype(F32)
            av = af * af
        o_ref[...] += _dot_at(av.astype(BF), g_ref[...].astype(BF))

    return pl.pallas_call(
        body, name=name, grid=(m // tm, n // tn, nk),
        in_specs=[pl.BlockSpec((tk, tm), lambda i, j, k: (k, i)), pl.BlockSpec((tk, tn), lambda i, j, k: (k, j))],
        out_specs=pl.BlockSpec((None, tm, tn), lambda i, j, k: (j // per, i, j % per)),
        out_shape=jax.ShapeDtypeStruct((col_shards, m, ns), F32),
        compiler_params=_params(48, ("arbitrary", "arbitrary", "arbitrary")),
    )(*_in_hbm(a, g))


def _wgrad_wide(a, g, tm, tk, name, col_shards):
    t, m = a.shape
    n = g.shape[1]
    tk = min(tk, t)
    tm = min(tm, m)
    ns = n // col_shards

    def body(a_ref, g_ref, o_ref):
        @pl.when(pl.program_id(1) == 0)
        def _():
            o_ref[...] = jnp.zeros_like(o_ref)

        a_t = a_ref[...].astype(BF).T
        for p in range(col_shards):
            o_ref[p] += _dot(a_t, g_ref[:, p * ns:(p + 1) * ns].astype(BF))

    return pl.pallas_call(
        body, name=name, grid=(m // tm, t // tk),
        in_specs=[pl.BlockSpec((tk, tm), lambda i, k: (k, i)), pl.BlockSpec((tk, n), lambda i, k: (k, 0))],
        out_specs=pl.BlockSpec((col_shards, tm, ns), lambda i, k: (0, i, 0)),
        out_shape=jax.ShapeDtypeStruct((col_shards, m, ns), F32),
        compiler_params=_params(48, ("arbitrary", "arbitrary")),
    )(*_in_hbm(a, g))


def _matmul_bt(a, w, tm, name):
    t, n = a.shape
    k = w.shape[0]
    tm = min(tm, t)

    def body(a_ref, w_ref, o_ref):
        o_ref[...] = _dot_bt(a_ref[...].astype(BF), w_ref[...]).astype(o_ref.dtype)

    return pl.pallas_call(
        body, name=name, grid=(t // tm,),
        in_specs=[pl.BlockSpec((tm, n), lambda i: (i, 0)), pl.BlockSpec((k, n), lambda i: (0, 0))],
        out_specs=pl.BlockSpec((tm, k), lambda i: (i, 0)),
        out_shape=jax.ShapeDtypeStruct((t, k), BF),
        compiler_params=_params(32, ("arbitrary",)),
    )(*_in_hbm(a, w))


def _gmlp_fwd(proj, gg, wt, bb, hg, tm):
    t = proj.shape[0]
    tm = min(tm, t)

    def body(u_ref, v_ref, gg_ref, wt_ref, bb_ref, hg_ref, out_ref):
        for cc in range(tm // CHUNK):
            rows = slice(cc * CHUNK, (cc + 1) * CHUNK)
            for g in range(GM_GROUPS):
                cols = slice(g * 128, (g + 1) * 128)
                u = _gelu(u_ref[rows, cols].astype(F32))
                gv = _gelu(v_ref[rows, cols].astype(F32))
                vn = gv * _rs(gv) * gg_ref[:, cols]
                mixed = _dot(wt_ref[g], vn.astype(BF)) + bb_ref[g]
                a = u * mixed
                out_ref[rows, cols] = (a * _rs(a) * hg_ref[:, cols]).astype(BF)

    return pl.pallas_call(
        body, name="gmlp_fwd", grid=(t // tm,),
        in_specs=[pl.BlockSpec((tm, 512), lambda i: (i, 0)), pl.BlockSpec((tm, 512), lambda i: (i, 1)),
                  pl.BlockSpec((1, 512), lambda i: (0, 0)), pl.BlockSpec((4, 128, 128), lambda i: (0, 0, 0)),
                  pl.BlockSpec((4, 128, 128), lambda i: (0, 0, 0)), pl.BlockSpec((1, 512), lambda i: (0, 0))],
        out_specs=pl.BlockSpec((tm, 512), lambda i: (i, 0)),
        out_shape=jax.ShapeDtypeStruct((t, 1024), BF),
        compiler_params=_params(32, ("arbitrary",)),
    )(*_in_hbm(proj, proj, gg, wt, bb, hg))


def _gmlp_bwd(proj, dmerged, gg, wt, wtt, bb, hg, tm, ride=None):
    t = proj.shape[0]
    tm = min(tm, t)
    nsteps = t // tm

    def body(u_ref, v_ref, dm_ref, gg_ref, wt_ref, wtt_ref, bb_ref, hg_ref,
             dp_ref, dw_ref, db_ref, dgg_ref, dhg_ref):
        i = pl.program_id(0)

        @pl.when(i == 0)
        def _():
            dw_ref[...] = jnp.zeros_like(dw_ref)
            db_ref[...] = jnp.zeros_like(db_ref)
            dgg_ref[...] = jnp.zeros_like(dgg_ref)
            dhg_ref[...] = jnp.zeros_like(dhg_ref)

        for cc in range(tm // CHUNK):
            rows = slice(cc * CHUNK, (cc + 1) * CHUNK)
            for g in range(GM_GROUPS):
                cols = slice(g * 128, (g + 1) * 128)
                up = u_ref[rows, cols].astype(F32)
                gp = v_ref[rows, cols].astype(F32)
                u, u_grad = _gelu_and_grad(up)
                gv, gv_grad = _gelu_and_grad(gp)
                rv = _rs(gv)
                gvh = gv * rv
                ggv = gg_ref[:, cols]
                vnb = (gvh * ggv).astype(BF)
                mixed = _dot(wt_ref[g], vnb) + bb_ref[g]
                a = u * mixed
                ra = _rs(a)
                ah = a * ra
                dm = dm_ref[rows, cols].astype(F32)
                dhg_ref[:, cols] += jnp.sum(dm * ah, axis=0, keepdims=True)
                dah = dm * hg_ref[:, cols]
                da = ra * (dah - ah * jnp.mean(dah * ah, axis=-1, keepdims=True))
                du = da * mixed
                dmix = da * u
                db_ref[g] += dmix
                dmb = dmix.astype(BF)
                dw_ref[g] += _dot_bt(dmb, vnb)
                dvn = _dot(wtt_ref[g], dmb)
                dgg_ref[:, cols] += jnp.sum(dvn * gvh, axis=0, keepdims=True)
                dgh = dvn * ggv
                dgv = rv * (dgh - gvh * jnp.mean(dgh * gvh, axis=-1, keepdims=True))
                dp_ref[rows, cols] = (du * u_grad).astype(BF)
                dp_ref[rows, 512 + g * 128:512 + (g + 1) * 128] = (dgv * gv_grad).astype(BF)

        @pl.when(i == nsteps - 1)
        def _():
            r = lax.broadcasted_iota(jnp.int32, (CHUNK, CHUNK), 0)
            c = lax.broadcasted_iota(jnp.int32, (CHUNK, CHUNK), 1)
            for g in range(GM_GROUPS):
                dw_ref[g] = jnp.where(c <= r, dw_ref[g], 0.0)
                db_ref[g] = jnp.broadcast_to(jnp.sum(db_ref[g], axis=-1, keepdims=True), (CHUNK, CHUNK))

    small = lambda shape: pl.BlockSpec(shape, lambda i: (0,) * len(shape))
    res, rode = _ride_call(
        body, "gmlp_bwd", (nsteps,),
        in_specs=[pl.BlockSpec((tm, 512), lambda i: (i, 0)), pl.BlockSpec((tm, 512), lambda i: (i, 1)),
                  pl.BlockSpec((tm, 512), lambda i: (i, 0)), small((1, 512)), small((4, 128, 128)),
                  small((4, 128, 128)), small((4, 128, 128)), small((1, 512))],
        out_specs=[pl.BlockSpec((tm, 1024), lambda i: (i, 0)), small((4, 128, 128)), small((4, 128, 128)),
                   small((1, 512)), small((1, 512))],
        out_shape=[jax.ShapeDtypeStruct((t, IN_COLS), BF), jax.ShapeDtypeStruct((4, 128, 128), F32),
                   jax.ShapeDtypeStruct((4, 128, 128), F32), jax.ShapeDtypeStruct((1, 512), F32),
                   jax.ShapeDtypeStruct((1, 512), F32)],
        scratch_shapes=[], operands=(proj, proj, dmerged, gg, wt, wtt, bb, hg), vmem_mb=32, ride=ride)
    return (*res, rode)


def _other_chips(x, y):
    return ((1 - x, y), (x, 1 - y), (1 - x, 1 - y))


class _GatherExchange:
    def __init__(self, shards):
        n = len(shards)
        self.n = n
        self.in_arrays = list(shards)
        self.out_shape = [jax.ShapeDtypeStruct((N_CHIPS,) + a.shape, a.dtype) for a in shards]
        self.half_rows = [a.shape[0] // 2 for a in shards]
        sems = lambda k: pltpu.SemaphoreType.DMA((k,))
        self.scratch_shapes = [pltpu.VMEM(a.shape, a.dtype) for a in shards] + [
            sems(3 * n), sems(3 * n), sems(3 * n), sems(3 * n), sems(n), sems(n)]

    def _copies(self, ins, outs, scr):
        n = self.n
        stages, (ici_send, ici_recv, d2d_send, d2d_recv, ld_sems, st_sems) = scr[:n], scr[n:]
        x, y, c = lax.axis_index("x"), lax.axis_index("y"), lax.axis_index("c")
        q = 2 * x + y
        loads = [pltpu.make_async_copy(ins[w], stages[w], ld_sems.at[w]) for w in range(n)]
        stores = [pltpu.make_async_copy(stages[w], outs[w].at[q], st_sems.at[w]) for w in range(n)]
        ici, d2d = [], []
        for w in range(n):
            half = pl.ds(c * self.half_rows[w], self.half_rows[w])
            for k, (px, py) in enumerate(_other_chips(x, y)):
                ici.append(pltpu.make_async_remote_copy(
                    src_ref=ins[w].at[half], dst_ref=outs[w].at[q, half], send_sem=ici_send.at[3 * w + k],
                    recv_sem=ici_recv.at[3 * w + k], device_id=(px, py, c), device_id_type=MESH))
                landed = outs[w].at[2 * px + py, half]
                d2d.append(pltpu.make_async_remote_copy(
                    src_ref=landed, dst_ref=landed, send_sem=d2d_send.at[3 * w + k],
                    recv_sem=d2d_recv.at[3 * w + k], device_id=(x, y, 1 - c), device_id_type=MESH))
        return loads, stores, ici, d2d

    def start(self, ins, outs, scr):
        loads, stores, ici, _ = self._copies(ins, outs, scr)
        for cp in loads + ici:
            cp.start()
        for ld, st in zip(loads, stores):
            ld.wait()
            st.start()

    def relay(self, ins, outs, scr):
        _, _, ici, d2d = self._copies(ins, outs, scr)
        for got, fwd in zip(ici, d2d):
            got.wait_recv()
            fwd.start()

    def finish(self, ins, outs, scr):
        _, stores, ici, d2d = self._copies(ins, outs, scr)
        for cp in ici:
            cp.wait_send()
        for cp in d2d + stores:
            cp.wait()


class _SiblingExchange:
    def __init__(self, slabs):
        n = len(slabs)
        self.n = n
        self.in_arrays = list(slabs)
        self.out_shape = [jax.ShapeDtypeStruct((N_CHIPS,) + a.shape[1:], a.dtype) for a in slabs]
        self.scratch_shapes = [pltpu.SemaphoreType.DMA((4 * n,)), pltpu.SemaphoreType.DMA((4 * n,))]

    def _copies(self, ins, outs, scr):
        send_sems, recv_sems = scr
        x, y, c = lax.axis_index("x"), lax.axis_index("y"), lax.axis_index("c")
        return [pltpu.make_async_remote_copy(
            src_ref=ins[w].at[2 * p + (1 - c)], dst_ref=outs[w].at[p], send_sem=send_sems.at[4 * w + p],
            recv_sem=recv_sems.at[4 * w + p], device_id=(x, y, 1 - c), device_id_type=MESH)
            for w in range(self.n) for p in range(N_CHIPS)]

    def start(self, ins, outs, scr):
        for cp in self._copies(ins, outs, scr):
            cp.start()

    def finish(self, ins, outs, scr):
        for cp in self._copies(ins, outs, scr):
            cp.wait()


class _ChipExchange:
    def __init__(self, sums):
        n = len(sums)
        self.n = n
        self.in_arrays = list(sums)
        self.out_shape = [jax.ShapeDtypeStruct(a.shape, a.dtype) for a in sums]
        self.scratch_shapes = [pltpu.SemaphoreType.DMA((3 * n,)), pltpu.SemaphoreType.DMA((3 * n,))]

    def _copies(self, ins, outs, scr):
        send_sems, recv_sems = scr
        x, y, c = lax.axis_index("x"), lax.axis_index("y"), lax.axis_index("c")
        q = 2 * x + y
        return [pltpu.make_async_remote_copy(
            src_ref=ins[w].at[2 * px + py], dst_ref=outs[w].at[q], send_sem=send_sems.at[3 * w + k],
            recv_sem=recv_sems.at[3 * w + k], device_id=(px, py, c), device_id_type=MESH)
            for w in range(self.n) for k, (px, py) in enumerate(_other_chips(x, y))]

    def start(self, ins, outs, scr):
        for cp in self._copies(ins, outs, scr):
            cp.start()

    def finish(self, ins, outs, scr):
        for cp in self._copies(ins, outs, scr):
            cp.wait()


class _NoExchange:
    in_arrays, out_shape, scratch_shapes = (), (), ()

    def start(self, ins, outs, scr):
        pass

    def finish(self, ins, outs, scr):
        pass


def _run_exchange(ex, name):
    n_in, n_out = len(ex.in_arrays), len(ex.out_shape)

    def body(*refs):
        ins, outs, scr = refs[:n_in], refs[n_in:n_in + n_out], refs[n_in + n_out:]
        ex.start(ins, outs, scr)
        if hasattr(ex, "relay"):
            ex.relay(ins, outs, scr)
        ex.finish(ins, outs, scr)

    return pl.pallas_call(
        body, name=name, in_specs=[ANY] * n_in, out_specs=[ANY] * n_out, out_shape=ex.out_shape,
        scratch_shapes=ex.scratch_shapes, compiler_params=_params(24),
    )(*ex.in_arrays)


def _ride_call(body, name, grid, in_specs, out_specs, out_shape, scratch_shapes, operands, vmem_mb, ride=None,
               aliases=None):
    ride = ride or _NoExchange()
    ni, no, ns = len(in_specs), len(out_specs), len(scratch_shapes)
    ri, ro = len(ride.in_arrays), len(ride.out_shape)
    total = math.prod(grid)

    def wrapped(*refs):
        ins, rins = refs[:ni], refs[ni:ni + ri]
        outs, routs = refs[ni + ri:ni + ri + no], refs[ni + ri + no:ni + ri + no + ro]
        scr, rscr = refs[ni + ri + no + ro:ni + ri + no + ro + ns], refs[ni + ri + no + ro + ns:]
        step = pl.program_id(0)
        for ax in range(1, len(grid)):
            step = step * grid[ax] + pl.program_id(ax)

        @pl.when(step == 0)
        def _():
            ride.start(rins, routs, rscr)

        if hasattr(ride, "relay"):
            @pl.when(step == (3 * total) // 4)
            def _():
                ride.relay(rins, routs, rscr)

        body(*ins, *outs, *scr)

        @pl.when(step == total - 1)
        def _():
            ride.finish(rins, routs, rscr)

    res = pl.pallas_call(
        wrapped, name=name, grid=grid, in_specs=list(in_specs) + [ANY] * ri, out_specs=list(out_specs) + [ANY] * ro,
        out_shape=list(out_shape) + list(ride.out_shape),
        scratch_shapes=list(scratch_shapes) + list(ride.scratch_shapes), input_output_aliases=aliases or {},
        compiler_params=_params(vmem_mb, ("arbitrary",) * len(grid)),
    )(*_in_hbm(*operands), *ride.in_arrays)
    return res[:no], res[no:]


def _neg_log_sig(z):
    n = jnp.maximum(z, 0.0) + jnp.log(1.0 + jnp.exp(-jnp.abs(z)))
    return n, z - n


def _running_sums(n, tri2):
    hi = n.astype(BF)
    lo = (n - hi.astype(F32)).astype(BF)
    return _dot(jnp.concatenate([hi, lo], axis=1), tri2)


def _head_sums(x, h0):
    s0 = jnp.sum(jnp.where(h0, x, 0.0), axis=-1, keepdims=True)
    s1 = jnp.sum(jnp.where(h0, 0.0, x), axis=-1, keepdims=True)
    return jnp.where(h0, s0, s1)


def _sb_setup(q_ref, tq):
    lane = lax.broadcasted_iota(jnp.int32, (tq, 128), 1)
    h0 = lane < HEAD_LANES
    qs = q_ref[...] * SB_SCALE
    zero = jnp.zeros_like(qs)
    qst = jnp.concatenate([jnp.where(h0, qs, zero), jnp.where(h0, zero, qs)], axis=0)
    r = lax.broadcasted_iota(jnp.int32, (2 * tq, tq), 0)
    c = lax.broadcasted_iota(jnp.int32, (2 * tq, tq), 1)
    causal = c < jnp.where(r >= tq, r - tq, r)
    return h0, qst, causal


def _tri(tq, op):
    return op(lax.broadcasted_iota(jnp.int32, (tq, tq), 0), lax.broadcasted_iota(jnp.int32, (tq, tq), 1)).astype(BF)


def _sb_fwd(proj, merged, hg, nb, s, tq, ride=None):
    t = nb * s
    tq = min(tq, s)
    nq = s // tq

    def body(q_ref, k_ref, v_ref, hg_ref, merged_ref, o_ref, tot_ref, mb_ref, nblk_ref, acc, cr, c_min):
        del merged_ref
        i = pl.program_id(2)
        h0, qst, causal = _sb_setup(q_ref, tq)
        tri_gt = _tri(tq, lambda r, c: r > c)
        tri_gt = jnp.concatenate([tri_gt, tri_gt], axis=0)

        def block(j, masked, c_in):
            start = pl.multiple_of(j * tq, tq)
            kj = k_ref[pl.ds(start, tq), :]
            vj = v_ref[pl.ds(start, tq), :]
            n, l = _neg_log_sig(_dot_bt(qst, kj))
            if masked:
                n = jnp.where(causal, n, 0.0)
            a = jnp.exp(l - (_running_sums(n, tri_gt) + c_in))
            if masked:
                a = jnp.where(causal, a, 0.0)
            return _dot(a.astype(BF), vj), c_in + jnp.sum(n, axis=-1, keepdims=True)

        @pl.when(i == 0)
        def _():
            acc[...], c_diag = block(0, True, jnp.zeros((2 * tq, 1), F32))
            c_min[0] = jnp.min(c_diag)
            cr[...] = c_diag

        @pl.when(i > 0)
        def _():
            p_diag, c_diag = block(i, True, jnp.zeros((2 * tq, 1), F32))
            p_prev, c_prev = block(i - 1, False, c_diag)
            c_min[0] = jnp.min(c_prev)
            acc[...] = p_diag + p_prev
            cr[...] = c_prev

        def cond(carry):
            return jnp.logical_and(carry[0] < i, carry[1] < -SB_SKIP)

        def step(carry):
            p, c_new = block(i - 1 - carry[0], False, cr[...])
            acc[...] += p
            cr[...] = c_new
            return carry[0] + 1, jnp.min(c_new)

        walked, _ = lax.while_loop(cond, step, (jnp.minimum(i, 1), c_min[0]))

        o = jnp.where(h0, acc[0:tq, :], acc[tq:2 * tq, :])
        o_ref[...] = o
        tot_ref[...] = jnp.where(h0, cr[0:tq, :], cr[tq:2 * tq, :])
        ro = lax.rsqrt(_head_sums(o * o, h0) * (1.0 / HEAD_LANES) + EPS)
        mb_ref[...] = (o * ro * hg_ref[...]).astype(BF)
        nblk_ref[...] = jnp.full((8, 128), walked.astype(F32))

    blk = lambda col0: pl.BlockSpec((tq, 128), lambda b, hp, i: (b * nq + i, col0 + hp))
    seq = lambda col0: pl.BlockSpec((s, 128), lambda b, hp, i: (b, col0 + hp))
    (o, tot, mb, nblk), rode = _ride_call(
        body, "sb_fwd", (nb, 4, nq),
        in_specs=[blk(8), seq(12), seq(16), pl.BlockSpec((1, 128), lambda b, hp, i: (0, 4 + hp)), ANY],
        out_specs=[blk(0), blk(0), blk(4), pl.BlockSpec((None, None, 8, 128), lambda b, hp, i: (b, hp, i, 0))],
        out_shape=[jax.ShapeDtypeStruct((t, 512), F32), jax.ShapeDtypeStruct((t, 512), F32),
                   jax.ShapeDtypeStruct((t, 1024), BF), jax.ShapeDtypeStruct((nb, 4, nq * 8, 128), F32)],
        scratch_shapes=[pltpu.VMEM((2 * tq, 128), F32), pltpu.VMEM((2 * tq, 1), F32), pltpu.SMEM((1,), F32)],
        operands=(proj, proj, proj, hg, merged), vmem_mb=40, ride=ride, aliases={4: 2})
    return o, tot, mb, nblk, rode


def _sb_bwd(proj, o_sb, tot, nblk, dmerged, dproj, hg, nb, s, tq, ride=None):
    t = nb * s
    tq = min(tq, s)
    nq = s // tq

    def body(q_ref, k_ref, v_ref, o_ref, tot_ref, nblk_ref, dm_ref, hg_ref, dproj_ref,
             dq_ref, dk_ref, dv_ref, dhg_ref, dk_acc, dv_acc, dq_acc, cm, cg):
        del dproj_ref
        i = pl.program_id(2)
        h0, qst, causal = _sb_setup(q_ref, tq)
        tri_le = _tri(tq, lambda r, c: r <= c)
        tri_le = jnp.concatenate([tri_le, tri_le], axis=0)
        tri_lt = _tri(tq, lambda r, c: r < c)

        @pl.when(i == 0)
        def _():
            dk_acc[...] = jnp.zeros_like(dk_acc)
            dv_acc[...] = jnp.zeros_like(dv_acc)
            dhg_ref[...] = jnp.zeros_like(dhg_ref)

        for ref in (dq_acc, cm, cg):
            ref[...] = jnp.zeros_like(ref)

        o = o_ref[...]
        ro = lax.rsqrt(_head_sums(o * o, h0) * (1.0 / HEAD_LANES) + EPS)
        oh = o * ro
        dm = dm_ref[...].astype(F32)
        dhg_ref[...] += jnp.sum(dm * oh, axis=0, keepdims=True)
        doh = dm * hg_ref[...]
        do = (ro * (doh - oh * (_head_sums(doh * oh, h0) * (1.0 / HEAD_LANES)))).astype(BF)
        zb = jnp.zeros_like(do)
        dost = jnp.concatenate([jnp.where(h0, do, zb), jnp.where(h0, zb, do)], axis=0)
        tots = jnp.concatenate([tot_ref[:, 0:1], tot_ref[:, HEAD_LANES:HEAD_LANES + 1]], axis=0)
        qst_t = qst.T
        dost_t = dost.T

        def block(j, masked, cm_in, cg_in):
            start = pl.multiple_of(j * tq, tq)
            kj = k_ref[pl.ds(start, tq), :]
            vj = v_ref[pl.ds(start, tq), :]
            n, l = _neg_log_sig(_dot_bt(qst, kj))
            if masked:
                n = jnp.where(causal, n, 0.0)
            a = jnp.exp(l - (tots - cm_in - _running_sums(n, tri_le)))
            if masked:
                a = jnp.where(causal, a, 0.0)
            gm = a * _dot_bt(dost, vj)
            pp = cg_in + _dot(gm.astype(BF), tri_lt)
            dz = gm - jnp.exp(l) * (gm + pp)
            if masked:
                dz = jnp.where(causal, dz, 0.0)
            dzb = dz.astype(BF)
            dk_acc[:, pl.ds(start, tq)] += _dot(qst_t, dzb)
            dv_acc[:, pl.ds(start, tq)] += _dot(dost_t, a.astype(BF))
            return (_dot(dzb, kj), cm_in + jnp.sum(n, axis=-1, keepdims=True),
                    cg_in + jnp.sum(gm, axis=-1, keepdims=True))

        def step(j, carry):
            dq, cm[...], cg[...] = block(j, False, cm[...], cg[...])
            dq_acc[...] += dq
            return carry

        walked = jnp.clip(nblk_ref[pl.program_id(0), pl.program_id(1), i].astype(jnp.int32), jnp.minimum(i, 1), i)
        lax.fori_loop(i - walked, i - 1, step, 0)

        @pl.when(i == 0)
        def _():
            dq_acc[...] = block(0, True, cm[...], cg[...])[0]

        @pl.when(i > 0)
        def _():
            dq_prev, cm_prev, cg_prev = block(i - 1, False, cm[...], cg[...])
            dq_acc[...] += dq_prev + block(i, True, cm_prev, cg_prev)[0]

        dq_ref[...] = (jnp.where(h0, dq_acc[0:tq, :], dq_acc[tq:2 * tq, :]) * SB_SCALE).astype(BF)

        @pl.when(i == nq - 1)
        def _():
            dk_ref[...] = dk_acc[...].T.astype(BF)
            dv_ref[...] = dv_acc[...].T.astype(BF)

    blk = lambda col0: pl.BlockSpec((tq, 128), lambda b, hp, i: (b * nq + i, col0 + hp))
    seq = lambda col0: pl.BlockSpec((s, 128), lambda b, hp, i: (b, col0 + hp))
    (dq, dk, dv, dhg), rode = _ride_call(
        body, "sb_bwd", (nb, 4, nq),
        in_specs=[blk(8), seq(12), seq(16), blk(0), blk(0), pl.BlockSpec(memory_space=pltpu.SMEM), blk(4),
                  pl.BlockSpec((1, 128), lambda b, hp, i: (0, 4 + hp)), ANY],
        out_specs=[blk(8), seq(0), seq(0), pl.BlockSpec((None, 1, 128), lambda b, hp, i: (b, 0, hp))],
        out_shape=[jax.ShapeDtypeStruct((t, IN_COLS), BF), jax.ShapeDtypeStruct((t, 512), BF),
                   jax.ShapeDtypeStruct((t, 512), BF), jax.ShapeDtypeStruct((nb, 1, 512), F32)],
        scratch_shapes=[pltpu.VMEM((128, s), F32), pltpu.VMEM((128, s), F32), pltpu.VMEM((2 * tq, 128), F32),
                        pltpu.VMEM((2 * tq, 1), F32), pltpu.VMEM((2 * tq, 1), F32)],
        operands=(proj, proj, proj, o_sb, tot, nblk.reshape(nb, 4, nq, 8, 128)[:, :, :, 0, 0], dmerged, hg, dproj),
        vmem_mb=40, ride=ride, aliases={8: 0})
    return dq, dk, dv, dhg, rode


def _place(buf, piece, col_block, name):
    t, w = piece.shape
    tm = min(t, 1024)

    def body(piece_ref, buf_ref, out_ref):
        del buf_ref
        out_ref[...] = piece_ref[...]

    return pl.pallas_call(
        body, name=name, grid=(t // tm,),
        in_specs=[pl.BlockSpec((tm, w), lambda i: (i, 0)), ANY],
        out_specs=pl.BlockSpec((tm, w), lambda i: (i, col_block)),
        out_shape=jax.ShapeDtypeStruct(buf.shape, buf.dtype), input_output_aliases={1: 0},
        compiler_params=_params(16, ("arbitrary",)),
    )(piece, buf)


def _softmax_rows(sc):
    e = jnp.exp(sc - jnp.max(sc, axis=-1, keepdims=True))
    return e / jnp.sum(e, axis=-1, keepdims=True)


def _mix_cross_fwd(x, merged, w_out, gc, w_cq, kv, w_co, s, tm):
    t, d = x.shape
    tm = min(tm, s)
    per = s // tm
    inv = 1.0 / math.sqrt(X_HEAD_DIM)

    def body(x_ref, m_ref, wo_ref, gc_ref, wq_ref, kv_ref, wc_ref, h1_ref, h2_ref, hn_ref, qc_ref, oc_ref):
        h1 = x_ref[...] + _dot(m_ref[...], wo_ref[...])
        h1_ref[...] = h1
        hn = (h1 * _rs(h1) * gc_ref[...]).astype(BF)
        hn_ref[...] = hn
        qc = _dot(hn, wq_ref[...]).astype(BF)
        qc_ref[...] = qc
        for h in range(X_HEADS):
            cols = slice(h * X_HEAD_DIM, (h + 1) * X_HEAD_DIM)
            kh = kv_ref[:, h * X_HEAD_DIM:(h + 1) * X_HEAD_DIM]
            vh = kv_ref[:, d + h * X_HEAD_DIM:d + (h + 1) * X_HEAD_DIM]
            p = _softmax_rows(_dot_bt(qc[:, cols], kh) * inv)
            oc_ref[:, cols] = _dot(p.astype(BF), vh).astype(BF)
        h2_ref[...] = h1 + _dot(oc_ref[...], wc_ref[...])

    row = lambda width: pl.BlockSpec((tm, width), lambda i: (i, 0))
    full = lambda a, b: pl.BlockSpec((a, b), lambda i: (0, 0))
    return pl.pallas_call(
        body, name="mix_cross_fwd", grid=(t // tm,),
        in_specs=[row(d), row(d), full(d, d), full(1, d), full(d, d),
                  pl.BlockSpec((N_MEM, 2 * d), lambda i: (i // per, 0)), full(d, d)],
        out_specs=[row(d), row(d), row(d), row(d), row(d)],
        out_shape=[jax.ShapeDtypeStruct((t, d), F32), jax.ShapeDtypeStruct((t, d), F32),
                   jax.ShapeDtypeStruct((t, d), BF), jax.ShapeDtypeStruct((t, d), BF),
                   jax.ShapeDtypeStruct((t, d), BF)],
        compiler_params=_params(48, ("arbitrary",)),
    )(*_in_hbm(x, merged, w_out, gc, w_cq, kv, w_co))


def _cross_bwd(dh2, h1, qc, gc, w_cq, kv, w_co, s, tm):
    t, d = dh2.shape
    tm = min(tm, s)
    per = s // tm
    nb = t // s
    inv = 1.0 / math.sqrt(X_HEAD_DIM)

    def body(dh2_ref, h1_ref, qc_ref, gc_ref, wq_ref, kv_ref, wc_ref, dh1_ref, dqc_ref, dkv_ref, dgc_ref):
        i = pl.program_id(0)

        @pl.when(i == 0)
        def _():
            dgc_ref[...] = jnp.zeros_like(dgc_ref)

        @pl.when(i % per == 0)
        def _():
            dkv_ref[...] = jnp.zeros_like(dkv_ref)

        dh2 = dh2_ref[...]
        h1 = h1_ref[...]
        r = _rs(h1)
        h1h = h1 * r
        gcv = gc_ref[...]
        qc = qc_ref[...]
        do = _dot_bt(dh2.astype(BF), wc_ref[...]).astype(BF)
        for h in range(X_HEADS):
            cols = slice(h * X_HEAD_DIM, (h + 1) * X_HEAD_DIM)
            vcols = slice(d + h * X_HEAD_DIM, d + (h + 1) * X_HEAD_DIM)
            kh = kv_ref[:, cols]
            vh = kv_ref[:, vcols]
            p = _softmax_rows(_dot_bt(qc[:, cols], kh) * inv)
            dp = _dot_bt(do[:, cols], vh)
            ds = (p * (dp - jnp.sum(dp * p, axis=-1, keepdims=True)) * inv).astype(BF)
            dqc_ref[:, cols] = _dot(ds, kh).astype(BF)
            dkv_ref[:, cols] += _dot_at(ds, qc[:, cols])
            dkv_ref[:, vcols] += _dot_at(p.astype(BF), do[:, cols])
        dhn = _dot_bt(dqc_ref[...], wq_ref[...])
        dx, dg = _rms_bwd(dhn, h1h, r, gcv)
        dh1_ref[...] = dh2 + dx
        dgc_ref[...] += jnp.sum(dg, axis=0, keepdims=True)

    row = lambda width: pl.BlockSpec((tm, width), lambda i: (i, 0))
    full = lambda a, b: pl.BlockSpec((a, b), lambda i: (0, 0))
    kvspec = pl.BlockSpec((N_MEM, 2 * d), lambda i: (i // per, 0))
    return pl.pallas_call(
        body, name="cross_bwd", grid=(t // tm,),
        in_specs=[row(d), row(d), row(d), full(1, d), full(d, d), kvspec, full(d, d)],
        out_specs=[row(d), row(d), kvspec, full(1, d)],
        out_shape=[jax.ShapeDtypeStruct((t, d), F32), jax.ShapeDtypeStruct((t, d), BF),
                   jax.ShapeDtypeStruct((nb * N_MEM, 2 * d), F32), jax.ShapeDtypeStruct((1, d), F32)],
        compiler_params=_params(48, ("arbitrary",)),
    )(*_in_hbm(dh2, h1, qc, gc, w_cq, kv, w_co))


def _mem_bwd(mem, gm, dkv, w_ckv, tm):
    t, d = mem.shape
    tm = min(tm, t)

    def body(mem_ref, dkv_ref, w_ref, dg_ref):
        @pl.when(pl.program_id(0) == 0)
        def _():
            dg_ref[...] = jnp.zeros_like(dg_ref)

        mv = mem_ref[...]
        dmn = _dot_bt(dkv_ref[...].astype(BF), w_ref[...])
        dg_ref[...] += jnp.sum(dmn * (mv * _rs(mv)), axis=0, keepdims=True)

    del gm
    return pl.pallas_call(
        body, name="mem_bwd", grid=(t // tm,),
        in_specs=[pl.BlockSpec((tm, d), lambda i: (i, 0)), pl.BlockSpec((tm, 2 * d), lambda i: (i, 0)),
                  pl.BlockSpec((d, 2 * d), lambda i: (0, 0))],
        out_specs=pl.BlockSpec((1, d), lambda i: (0, 0)),
        out_shape=jax.ShapeDtypeStruct((1, d), F32),
        compiler_params=_params(32, ("arbitrary",)),
    )(mem, dkv, w_ckv)


def _ffn_loss_fwd(h2, gf, w1, w2, gl, target, tm):
    t, d = h2.shape
    tm = min(tm, t)

    def body(h2_ref, gf_ref, w1_ref, w2_ref, gl_ref, tg_ref, hn_ref, f_ref, dh3_ref, dgl_ref, loss_ref):
        @pl.when(pl.program_id(0) == 0)
        def _():
            dgl_ref[...] = jnp.zeros_like(dgl_ref)
            loss_ref[...] = jnp.zeros_like(loss_ref)

        h2 = h2_ref[...]
        hn = (h2 * _rs(h2) * gf_ref[...]).astype(BF)
        hn_ref[...] = hn
        h3 = h2
        for c in range(4):
            f = jnp.maximum(_dot(hn, w1_ref[c]), 0.0)
            f_ref[:, c * 1024:(c + 1) * 1024] = f.astype(BF)
            h3 = h3 + _dot((f * f).astype(BF), w2_ref[c])
        r3 = _rs(h3)
        yh = h3 * r3
        glv = gl_ref[...]
        e = yh * glv - tg_ref[...]
        loss_ref[...] += 0.5 * jnp.sum(jnp.sum(e * e, axis=-1, keepdims=True) * (1.0 / d), axis=0, keepdims=True)
        dy = e * (1.0 / d)
        dx, dg = _rms_bwd(dy, yh, r3, glv)
        dh3_ref[...] = dx
        dgl_ref[...] += jnp.sum(dg, axis=0, keepdims=True)

    row = lambda width: pl.BlockSpec((tm, width), lambda i: (i, 0))
    return pl.pallas_call(
        body, name="ffn_loss_fwd", grid=(t // tm,),
        in_specs=[row(d), pl.BlockSpec((1, d), lambda i: (0, 0)), pl.BlockSpec((4, d, 1024), lambda i: (0, 0, 0), pipeline_mode=pl.Buffered(1)),
                  pl.BlockSpec((4, 1024, d), lambda i: (0, 0, 0), pipeline_mode=pl.Buffered(1)),
                  pl.BlockSpec((1, d), lambda i: (0, 0)), row(d)],
        out_specs=[row(d), row(D_FF), row(d), pl.BlockSpec((1, d), lambda i: (0, 0)),
                   pl.BlockSpec((1, 1), lambda i: (0, 0))],
        out_shape=[jax.ShapeDtypeStruct((t, d), BF), jax.ShapeDtypeStruct((t, D_FF), BF),
                   jax.ShapeDtypeStruct((t, d), F32), jax.ShapeDtypeStruct((1, d), F32),
                   jax.ShapeDtypeStruct((1, 1), F32)],
        compiler_params=_params(56, ("arbitrary",)),
    )(*_in_hbm(h2, gf, w1, w2, gl, target))


def _ffn_bwd(dh3, f, h2, gf, w1, w2, tm):
    t, d = h2.shape
    tm = min(tm, t)

    def body(dh3_ref, f_ref, h2_ref, gf_ref, w1_ref, w2_ref, dh2_ref, dpre_ref, dgf_ref):
        @pl.when(pl.program_id(0) == 0)
        def _():
            dgf_ref[...] = jnp.zeros_like(dgf_ref)

        dh3 = dh3_ref[...]
        dh3b = dh3.astype(BF)
        dhn = jnp.zeros((tm, d), F32)
        for c in range(4):
            cols = slice(c * 1024, (c + 1) * 1024)
            dpre = (_dot_bt(dh3b, w2_ref[c]) * (2.0 * f_ref[:, cols].astype(F32))).astype(BF)
            dpre_ref[:, cols] = dpre
            dhn = dhn + _dot_bt(dpre, w1_ref[c])
        h2 = h2_ref[...]
        r = _rs(h2)
        dx, dg = _rms_bwd(dhn, h2 * r, r, gf_ref[...])
        dh2_ref[...] = dh3 + dx
        dgf_ref[...] += jnp.sum(dg, axis=0, keepdims=True)

    row = lambda width: pl.BlockSpec((tm, width), lambda i: (i, 0))
    return pl.pallas_call(
        body, name="ffn_bwd", grid=(t // tm,),
        in_specs=[row(d), row(D_FF), row(d), pl.BlockSpec((1, d), lambda i: (0, 0)),
                  pl.BlockSpec((4, d, 1024), lambda i: (0, 0, 0), pipeline_mode=pl.Buffered(1)),
                  pl.BlockSpec((4, 1024, d), lambda i: (0, 0, 0), pipeline_mode=pl.Buffered(1))],
        out_specs=[row(d), row(D_FF), pl.BlockSpec((1, d), lambda i: (0, 0))],
        out_shape=[jax.ShapeDtypeStruct((t, d), F32), jax.ShapeDtypeStruct((t, D_FF), BF),
                   jax.ShapeDtypeStruct((1, d), F32)],
        compiler_params=_params(56, ("arbitrary",)),
    )(*_in_hbm(dh3, f, h2, gf, w1, w2))


def _in_bwd(dproj, dh1, x, g, w_in, tm, ride=None):
    t, d = x.shape
    n = w_in.shape[1]
    tm = min(tm, t)

    def body(dp_ref, dh1_ref, x_ref, g_ref, w_ref, dx_ref, dg_ref):
        @pl.when(pl.program_id(0) == 0)
        def _():
            dg_ref[...] = jnp.zeros_like(dg_ref)

        dxn = _dot_bt(dp_ref[...], w_ref[...])
        xv = x_ref[...]
        r = _rs(xv)
        dx, dg = _rms_bwd(dxn, xv * r, r, g_ref[...])
        dx_ref[...] = dh1_ref[...] + dx
        dg_ref[...] += jnp.sum(dg, axis=0, keepdims=True)

    row = lambda width: pl.BlockSpec((tm, width), lambda i: (i, 0))
    (dx, dg), rode = _ride_call(
        body, "in_bwd", (t // tm,),
        in_specs=[row(n), row(d), row(d), pl.BlockSpec((1, d), lambda i: (0, 0)),
                  pl.BlockSpec((d, n), lambda i: (0, 0))],
        out_specs=[row(d), pl.BlockSpec((1, d), lambda i: (0, 0))],
        out_shape=[jax.ShapeDtypeStruct((t, d), F32), jax.ShapeDtypeStruct((1, d), F32)],
        scratch_shapes=[], operands=(dproj, dh1, x, g, w_in), vmem_mb=48, ride=ride)
    return dx, dg, rode


class _GradReduce:
    def __init__(self, c_idx):
        self.c_idx = c_idx
        self.sums = {}

    def sibling(self, slabs):
        return _SiblingExchange(slabs)

    def chip(self, names, slabs, recv):
        for k, a, r in zip(names, slabs, recv):
            self.sums[k] = _chip_sum(a, r, self.c_idx, "chip_sum_" + k)
        return _ChipExchange([self.sums[k] for k in names])


def _full_weights(gathered):
    d = D_MODEL
    out = {}
    for k, a in gathered.items():
        if k in ("w_in", "w_ckv", "w_ff1"):
            out[k] = a.transpose(1, 0, 2).reshape(d, -1)
        else:
            out[k] = a.reshape(-1, d)
    return out


def _slabs(a):
    return a.reshape(N_DEV, -1, a.shape[-1])


def _local_step(x, mem, target, small, big, nb, s, tq=256, gather_rest=None, reduce=None):
    d = D_MODEL
    g_mix, g_v, w_sp, b_sp, g_head, g_cross, g_mem, g_ffn, g_fin = (
        small[k] for k in ("norm_mix_g", "gm_v_norm_g", "w_spatial", "b_spatial", "head_norm_g", "norm_cross_g",
                           "norm_mem_g", "norm_ffn_g", "norm_final_g"))
    tri = jnp.tril(jnp.ones((CHUNK, CHUNK), dtype=bool))
    w_sp_m = jnp.where(tri[None], w_sp, 0.0)
    wt = w_sp_m.astype(BF)
    wtt = jnp.swapaxes(w_sp_m, 1, 2).astype(BF)
    bb = jnp.broadcast_to(b_sp[:, :, None], (GM_GROUPS, CHUNK, CHUNK))
    hg_a = g_head[:, :GM_WIDTH]

    proj, xn = _norm_matmul(x, g_mix, big["w_in"], 512, "in_proj")
    merged = _gmlp_fwd(proj, g_v, wt, bb, hg_a, 512)
    o_sb, tot, merged, nblk, gathered = _sb_fwd(proj, merged, g_head, nb, s, tq, ride=gather_rest)
    if gather_rest is not None:
        big = dict(big, **_full_weights(dict(zip(BIG[1:], gathered))))
    w1c = big["w_ff1"].reshape(d, 4, 1024).transpose(1, 0, 2)
    w2c = big["w_ff2"].reshape(4, 1024, d)
    kv, memn = _norm_matmul(mem, g_mem, big["w_ckv"], 512, "mem_proj")
    h1, h2, hn, qc, oc = _mix_cross_fwd(x, merged, big["w_out"], g_cross, big["w_cq"], kv, big["w_co"], s, 512)
    hn2, f, dh3, d_fin, loss = _ffn_loss_fwd(h2, g_ffn, w1c, w2c, g_fin, target, 512)

    gbig = {}
    dh2, dpre, d_ffn = _ffn_bwd(dh3, f, h2, g_ffn, w1c, w2c, 512)
    gbig["w_ff2"] = _slabs(_wgrad(f, dh3, 1024, 1024, "wgrad_ff2", square_a=True))
    gbig["w_ff1"] = _slabs(_wgrad_wide(hn2, dpre, 512, 1024, "wgrad_ff1", col_shards=4))
    dh1, dqc, dkv, d_cross = _cross_bwd(dh2, h1, qc, g_cross, big["w_cq"], kv, big["w_co"], s, 512)
    gbig["w_co"] = _slabs(_wgrad(oc, dh2, 1024, 1024, "wgrad_co"))
    gbig["w_cq"] = _slabs(_wgrad(hn, dqc, 1024, 1024, "wgrad_cq"))
    gbig["w_ckv"] = _slabs(_wgrad(memn, dkv, 512, 1024, "wgrad_ckv", col_shards=4))
    d_mem = _mem_bwd(mem, g_mem, dkv, big["w_ckv"], 512)
    dmerged = _matmul_bt(dh1, big["w_out"], 512, "out_bwd")
    gbig["w_out"] = _slabs(_wgrad(merged, dh1, 1024, 1024, "wgrad_out"))
    rest = BIG[1:]
    ride = reduce.sibling([gbig[k] for k in rest]) if reduce else None
    dproj, d_wsp, d_bb, d_gv, d_hga, recv = _gmlp_bwd(proj, dmerged, g_v, wt, wtt, bb, hg_a, 512, ride=ride)
    ride = reduce.chip(rest, [gbig[k] for k in rest], recv) if reduce else None
    dproj, dk, dv, d_hgb, parts_rest = _sb_bwd(proj, o_sb, tot, nblk, dmerged, dproj, g_head, nb, s, tq, ride=ride)
    dproj = _place(_place(dproj, dk, 3, "place_dk"), dv, 4, "place_dv")
    gbig["w_in"] = _slabs(_wgrad_wide(xn, dproj, 512, 1024, "wgrad_in", col_shards=4))
    last = None
    if reduce:
        recv = _run_exchange(reduce.sibling([gbig["w_in"]]), "grad_sibling_exchange_w_in")
        last = reduce.chip(["w_in"], [gbig["w_in"]], recv)
    grad_x, d_mix, _ = _in_bwd(dproj, dh1, x, g_mix, big["w_in"], 512)
    parts = dict(zip(rest, parts_rest))

    gsmall = {
        "norm_mix_g": d_mix, "gm_v_norm_g": d_gv, "w_spatial": d_wsp, "b_spatial": d_bb[:, :, 0],
        "head_norm_g": jnp.concatenate([d_hga, jnp.sum(d_hgb, axis=0)], axis=1), "norm_cross_g": d_cross,
        "norm_mem_g": d_mem, "norm_ffn_g": d_ffn, "norm_final_g": d_fin,
    }
    return loss, grad_x, gsmall, gbig, parts, last


BIG = ("w_in", "w_out", "w_cq", "w_ckv", "w_co", "w_ff1", "w_ff2")
SMALL = ("norm_mix_g", "gm_v_norm_g", "w_spatial", "b_spatial", "head_norm_g", "norm_cross_g", "norm_mem_g",
         "norm_ffn_g", "norm_final_g")


def _local_copies_start(srcs, stages, sems):
    loads = [pltpu.make_async_copy(src, stage, sems.at[w]) for w, (src, stage) in enumerate(zip(srcs, stages))]
    for ld in loads:
        ld.start()
    return loads


def _local_copies_finish(loads, stages, dsts, sems):
    stores = []
    for w, (ld, stage, dst) in enumerate(zip(loads, stages, dsts)):
        ld.wait()
        st = pltpu.make_async_copy(stage, dst, sems.at[w])
        st.start()
        stores.append(st)
    for st in stores:
        st.wait()


def _chip_sum(slabs, recv, c_idx, name):
    _, r, cw = slabs.shape
    tr = min(r, 256)

    def body(c_ref, a_ref, b_ref, o_ref):
        del c_ref
        o_ref[...] = (a_ref[...] + b_ref[...]).astype(BF)

    return pl.pallas_call(
        body, name=name,
        grid_spec=pltpu.PrefetchScalarGridSpec(
            num_scalar_prefetch=1, grid=(N_CHIPS, r // tr),
            in_specs=[pl.BlockSpec((None, tr, cw), lambda p, i, c_ref: (2 * p + c_ref[0], i, 0)),
                      pl.BlockSpec((None, tr, cw), lambda p, i, c_ref: (p, i, 0))],
            out_specs=pl.BlockSpec((None, tr, cw), lambda p, i, c_ref: (p, i, 0))),
        out_shape=jax.ShapeDtypeStruct((N_CHIPS, r, cw), BF),
        compiler_params=_params(32, ("arbitrary", "arbitrary")),
    )(c_idx, *_in_hbm(slabs, recv))


def _sum4(sums, parts, q_idx, name):
    _, r, cw = parts.shape
    tr = min(r, 256)

    def body(q_ref, own_ref, a_ref, b_ref, c_ref, o_ref):
        del q_ref
        o_ref[...] = ((own_ref[...].astype(F32) + a_ref[...].astype(F32)) + b_ref[...].astype(F32)) + c_ref[
            ...].astype(F32)

    spec = lambda k: pl.BlockSpec((None, tr, cw), lambda i, q_ref: ((q_ref[0] + k) % N_CHIPS, i, 0))
    return pl.pallas_call(
        body, name=name,
        grid_spec=pltpu.PrefetchScalarGridSpec(
            num_scalar_prefetch=1, grid=(r // tr,), in_specs=[spec(0), spec(1), spec(2), spec(3)],
            out_specs=pl.BlockSpec((tr, cw), lambda i, q_ref: (i, 0))),
        out_shape=jax.ShapeDtypeStruct((r, cw), F32),
        compiler_params=_params(32, ("arbitrary",)),
    )(q_idx, *_in_hbm(sums, parts, parts, parts))


def _half_exchange(halves):
    n = len(halves)

    def body(*refs):
        ins, outs, stages = refs[:n], refs[n:2 * n], refs[2 * n:3 * n]
        send_sems, recv_sems, ld_sems, st_sems = refs[3 * n:]
        x, y, c = lax.axis_index("x"), lax.axis_index("y"), lax.axis_index("c")
        loads = _local_copies_start(ins, stages, ld_sems)
        copies = []
        for w in range(n):
            cp = pltpu.make_async_remote_copy(
                src_ref=ins[w], dst_ref=outs[w].at[c], send_sem=send_sems.at[w], recv_sem=recv_sems.at[w],
                device_id=(x, y, 1 - c), device_id_type=MESH)
            cp.start()
            copies.append(cp)
        _local_copies_finish(loads, stages, [outs[w].at[c] for w in range(n)], st_sems)
        for cp in copies:
            cp.wait()

    return pl.pallas_call(
        body, name="grad_half_exchange",
        in_specs=[ANY] * n, out_specs=[ANY] * n,
        out_shape=[jax.ShapeDtypeStruct((2,) + a.shape, a.dtype) for a in halves],
        scratch_shapes=[pltpu.VMEM(a.shape, a.dtype) for a in halves] + [
            pltpu.SemaphoreType.DMA((n,)), pltpu.SemaphoreType.DMA((n,)),
            pltpu.SemaphoreType.DMA((n,)), pltpu.SemaphoreType.DMA((n,))],
        compiler_params=_params(24),
    )(*halves)


def _small_all_reduce(packed, ride=None):
    rows = packed.shape[0]
    ride = ride or _NoExchange()
    ri, ro = len(ride.in_arrays), len(ride.out_shape)

    def body(*refs):
        in_ref, rins, out_ref, routs = refs[0], refs[1:1 + ri], refs[1 + ri], refs[2 + ri:2 + ri + ro]
        pair, chip_sum, chips, d2d_send, d2d_recv, ici_send, ici_recv = refs[2 + ri + ro:9 + ri + ro]
        rscr = refs[9 + ri + ro:]
        ride.start(rins, routs, rscr)
        x, y, c = lax.axis_index("x"), lax.axis_index("y"), lax.axis_index("c")
        q = 2 * x + y
        pair[c] = in_ref[...]
        swap = pltpu.make_async_remote_copy(
            src_ref=in_ref, dst_ref=pair.at[c], send_sem=d2d_send, recv_sem=d2d_recv,
            device_id=(x, y, 1 - c), device_id_type=MESH)
        swap.start()
        swap.wait()
        both = pair[0] + pair[1]
        chip_sum[...] = both
        chips[q] = both
        copies = [pltpu.make_async_remote_copy(
            src_ref=chip_sum, dst_ref=chips.at[q], send_sem=ici_send.at[k], recv_sem=ici_recv.at[k],
            device_id=(px, py, c), device_id_type=MESH) for k, (px, py) in enumerate(_other_chips(x, y))]
        for cp in copies:
            cp.start()
        for cp in copies:
            cp.wait()
        out_ref[...] = ((chips[0] + chips[1]) + chips[2]) + chips[3]
        ride.finish(rins, routs, rscr)

    vmem = pl.BlockSpec(memory_space=pltpu.VMEM)
    res = pl.pallas_call(
        body, name="small_all_reduce",
        in_specs=[vmem] + [ANY] * ri, out_specs=[vmem] + [ANY] * ro,
        out_shape=[jax.ShapeDtypeStruct(packed.shape, F32)] + list(ride.out_shape),
        scratch_shapes=[pltpu.VMEM((2, rows, 128), F32), pltpu.VMEM((rows, 128), F32),
                        pltpu.VMEM((N_CHIPS, rows, 128), F32), pltpu.SemaphoreType.DMA, pltpu.SemaphoreType.DMA,
                        pltpu.SemaphoreType.DMA((3,)), pltpu.SemaphoreType.DMA((3,))] + list(ride.scratch_shapes),
        compiler_params=_params(16),
    )(packed, *ride.in_arrays)
    return res[0], res[1:]


def _adamw(g, w, m, v, name):
    r, cw = g.shape
    tr = 256 if r % 256 == 0 else r

    def body(g_ref, w_ref, m_ref, v_ref, d_ref, nm_ref, nv_ref):
        gv = g_ref[...]
        nm = ADAM_B1 * m_ref[...] + (1.0 - ADAM_B1) * gv
        nv = ADAM_B2 * v_ref[...] + (1.0 - ADAM_B2) * (gv * gv)
        m_hat = nm / (1.0 - ADAM_B1 ** ADAM_STEP)
        v_hat = nv / (1.0 - ADAM_B2 ** ADAM_STEP)
        d_ref[...] = -ADAM_LR * (m_hat / (jnp.sqrt(v_hat) + ADAM_EPS) + ADAM_WD * w_ref[...])
        nm_ref[...] = nm
        nv_ref[...] = nv

    spec = pl.BlockSpec((tr, cw), lambda i: (i, 0))
    return pl.pallas_call(
        body, name=name, grid=(r // tr,),
        in_specs=[spec] * 4, out_specs=[spec] * 3,
        out_shape=[jax.ShapeDtypeStruct((r, cw), F32)] * 3,
        compiler_params=_params(32, ("arbitrary",)),
    )(*_in_hbm(g, w, m, v))


def _small_params(args):
    small = {k: args[k].reshape(1, -1) for k in SMALL}
    small["w_spatial"] = args["w_spatial"][0]
    small["b_spatial"] = args["b_spatial"][0]
    return small


def _pack(parts, rows):
    flat = jnp.concatenate([p.reshape(-1).astype(F32) for p in parts])
    return jnp.pad(flat, (0, rows * 128 - flat.shape[0])).reshape(rows, 128)


def _unpack(packed, shapes):
    flat = packed.reshape(-1)
    out, off = [], 0
    for shp in shapes:
        size = math.prod(shp)
        out.append(flat[off:off + size].reshape(shp))
        off += size
    return out


def kernel(x, mem, norm_mix_g, w_in, gm_v_norm_g, w_spatial, b_spatial, head_norm_g, w_out, norm_cross_g, norm_mem_g, w_cq, w_ckv, w_co, norm_ffn_g, w_ff1, w_ff2, norm_final_g, loss_target, m_norm_mix_g, m_w_in, m_gm_v_norm_g, m_w_spatial, m_b_spatial, m_head_norm_g, m_w_out, m_norm_cross_g, m_norm_mem_g, m_w_cq, m_w_ckv, m_w_co, m_norm_ffn_g, m_w_ff1, m_w_ff2, m_norm_final_g, v_norm_mix_g, v_w_in, v_gm_v_norm_g, v_w_spatial, v_b_spatial, v_head_norm_g, v_w_out, v_norm_cross_g, v_norm_mem_g, v_w_cq, v_w_ckv, v_w_co, v_norm_ffn_g, v_w_ff1, v_w_ff2, v_norm_final_g):
    args = dict(locals())
    d = D_MODEL
    nb, s, _ = x.shape
    c_idx = lax.axis_index("c").astype(jnp.int32).reshape(1)
    q_idx = (2 * lax.axis_index("x") + lax.axis_index("y")).astype(jnp.int32).reshape(1)
    rest = BIG[1:]

    shards = {k: args[k][0].astype(BF) for k in BIG}
    big = _full_weights({"w_in": _run_exchange(_GatherExchange([shards["w_in"]]), "all_gather_w_in")[0]})
    gather_rest = _GatherExchange([shards[k] for k in rest])

    reduce = _GradReduce(c_idx)
    loss, grad_x, gsmall, _, parts, last = _local_step(
        x.reshape(nb * s, d), mem.reshape(nb * N_MEM, d), loss_target.reshape(nb * s, d), _small_params(args), big,
        nb, s, gather_rest=gather_rest, reduce=reduce)

    shapes = [args[k].shape for k in SMALL]
    n_small = sum(math.prod(sh) for sh in shapes)
    rows = -(-(n_small + 1) // 1024) * 8
    reduced, (parts["w_in"],) = _small_all_reduce(_pack([gsmall[k] for k in SMALL] + [loss], rows), ride=last)
    halves = [_sum4(reduce.sums[k], parts[k], q_idx, "sum4_" + k) for k in BIG]
    both = _half_exchange(halves)

    out = {"grad_x": grad_x.reshape(nb, s, d)}
    for k, g2 in zip(BIG, both):
        shp = args[k].shape
        g = g2.reshape(shp[1], shp[2])
        dl, nm, nv = _adamw(g, args[k][0], args["m_" + k][0], args["v_" + k][0], "adamw_" + k)
        out["grad_" + k], out["delta_" + k], out["new_m_" + k], out["new_v_" + k] = (
            a.reshape(shp) for a in (g, dl, nm, nv))

    dl, nm, nv = _adamw(reduced, _pack([args[k] for k in SMALL], rows), _pack([args["m_" + k] for k in SMALL], rows),
                        _pack([args["v_" + k] for k in SMALL], rows), "adamw_small")
    for name, arr in (("grad_", reduced), ("delta_", dl), ("new_m_", nm), ("new_v_", nv)):
        for k, a in zip(SMALL, _unpack(arr, shapes)):
            out[name + k] = a
    out["loss"] = reduced.reshape(-1)[n_small]

    names = ["norm_mix_g", "w_in", "gm_v_norm_g", "w_spatial", "b_spatial", "head_norm_g", "w_out", "norm_cross_g",
             "norm_mem_g", "w_cq", "w_ckv", "w_co", "norm_ffn_g", "w_ff1", "w_ff2", "norm_final_g"]
    return (out["loss"], out["grad_x"], *[out["grad_" + k] for k in names], *[out["delta_" + k] for k in names],
            *[out["new_m_" + k] for k in names], *[out["new_v_" + k] for k in names])
```

```python
import functools
import math

import jax
import jax.numpy as jnp
from jax import lax
from jax.experimental import pallas as pl
from jax.experimental.pallas import tpu as pltpu

F32 = jnp.float32
BF = jnp.bfloat16

EPS = 1e-6
D_MODEL = 1024
CHUNK = 128
GM_GROUPS = 4
GM_WIDTH = 512
SB_WIDTH = 512
HEAD_LANES = 64
SB_SCALE = 0.125
SB_SKIP = -104.0
X_HEADS = 4
X_HEAD_DIM = 256
N_MEM = 256
D_FF = 4096
IN_COLS = 2560
N_CHIPS = 4
N_DEV = 8

ADAM_LR = 0.001
ADAM_B1 = 0.9
ADAM_B2 = 0.999
ADAM_EPS = 1e-08
ADAM_WD = 0.01
ADAM_STEP = 10

V7X_VMEM_BYTES = 64 * 1024 * 1024
MESH = pl.DeviceIdType.MESH
ANY = pl.BlockSpec(memory_space=pl.ANY)

GELU_C = math.sqrt(2.0 / math.pi)
GELU_A = 0.044715


def _params(vmem_mb, sem=None):
    assert vmem_mb * 1024 * 1024 <= V7X_VMEM_BYTES
    return pltpu.CompilerParams(vmem_limit_bytes=vmem_mb * 1024 * 1024, dimension_semantics=sem)


PIN_MIN_ELEMENTS = 1 << 18


def _in_hbm(*arrays):
    return tuple(pltpu.with_memory_space_constraint(a, pltpu.HBM) if a.size >= PIN_MIN_ELEMENTS else a
                 for a in arrays)


def _dot(a, b):
    return jnp.dot(a, b, preferred_element_type=F32)


def _dot_bt(a, b):
    return lax.dot_general(a, b, (((1,), (1,)), ((), ())), preferred_element_type=F32)


def _dot_at(a, b):
    return lax.dot_general(a, b, (((0,), (0,)), ((), ())), preferred_element_type=F32)


def _gelu(x):
    t = jnp.tanh(GELU_C * (x + GELU_A * x * x * x))
    return 0.5 * x * (1.0 + t)


def _gelu_and_grad(x):
    x2 = x * x
    t = jnp.tanh(GELU_C * (x + GELU_A * x2 * x))
    h = 0.5 * (1.0 + t)
    return x * h, h + 0.5 * x * (1.0 - t * t) * (GELU_C * (1.0 + 3.0 * GELU_A * x2))


def _rs(x):
    return lax.rsqrt(jnp.mean(x * x, axis=-1, keepdims=True) + EPS)


def _rms_bwd(dxn, xhat, r, g):
    dxh = dxn * g
    dx = r * (dxh - xhat * jnp.mean(dxh * xhat, axis=-1, keepdims=True))
    return dx, dxn * xhat


def _norm_matmul(x, g, w, tm, name):
    t, d = x.shape
    n = w.shape[1]
    tm = min(tm, t)

    def body(x_ref, g_ref, w_ref, out_ref, xn_ref):
        xv = x_ref[...]
        xn = (xv * _rs(xv) * g_ref[...]).astype(BF)
        xn_ref[...] = xn
        out_ref[...] = _dot(xn, w_ref[...]).astype(out_ref.dtype)

    return pl.pallas_call(
        body, name=name, grid=(t // tm,),
        in_specs=[pl.BlockSpec((tm, d), lambda i: (i, 0)), pl.BlockSpec((1, d), lambda i: (0, 0)),
                  pl.BlockSpec((d, n), lambda i: (0, 0))],
        out_specs=[pl.BlockSpec((tm, n), lambda i: (i, 0)), pl.BlockSpec((tm, d), lambda i: (i, 0))],
        out_shape=[jax.ShapeDtypeStruct((t, n), BF), jax.ShapeDtypeStruct((t, d), BF)],
        compiler_params=_params(48, ("arbitrary",)),
    )(*_in_hbm(x, g, w))


def _wgrad(a, g, tn, tk, name, square_a=False, col_shards=1):
    t, m = a.shape
    n = g.shape[1]
    tk = min(tk, t)
    tm = min(m, 1024)
    ns = n // col_shards
    assert ns % tn == 0 and m % tm == 0
    per = ns // tn
    nk = t // tk

    def body(a_ref, g_ref, o_ref):
        k = pl.program_id(2)

        @pl.when(k == 0)
        def _():
            o_ref[...] = jnp.zeros_like(o_ref)

        av = a_ref[...]
        if square_a:
            af = av.astype(F32)
            av = af * af
        o_ref[...] += _dot_at(av.astype(BF), g_ref[...].astype(BF))

    return pl.pallas_call(
        body, name=name, grid=(m // tm, n // tn, nk),
        in_specs=[pl.BlockSpec((tk, tm), lambda i, j, k: (k, i)), pl.BlockSpec((tk, tn), lambda i, j, k: (k, j))],
        out_specs=pl.BlockSpec((None, tm, tn), lambda i, j, k: (j // per, i, j % per)),
        out_shape=jax.ShapeDtypeStruct((col_shards, m, ns), F32),
        compiler_params=_params(48, ("arbitrary", "arbitrary", "arbitrary")),
    )(*_in_hbm(a, g))


def _wgrad_wide(a, g, tm, tk, name, col_shards):
    t, m = a.shape
    n = g.shape[1]
    tk = min(tk, t)
    tm = min(tm, m)
    ns = n // col_shards

    def body(a_ref, g_ref, o_ref):
        @pl.when(pl.program_id(1) == 0)
        def _():
            o_ref[...] = jnp.zeros_like(o_ref)

        a_t = a_ref[...].astype(BF).T
        for p in range(col_shards):
            o_ref[p] += _dot(a_t, g_ref[:, p * ns:(p + 1) * ns].astype(BF))

    return pl.pallas_call(
        body, name=name, grid=(m // tm, t // tk),
        in_specs=[pl.BlockSpec((tk, tm), lambda i, k: (k, i)), pl.BlockSpec((tk, n), lambda i, k: (k, 0))],
        out_specs=pl.BlockSpec((col_shards, tm, ns), lambda i, k: (0, i, 0)),
        out_shape=jax.ShapeDtypeStruct((col_shards, m, ns), F32),
        compiler_params=_params(48, ("arbitrary", "arbitrary")),
    )(*_in_hbm(a, g))


def _matmul_bt(a, w, tm, name):
    t, n = a.shape
    k = w.shape[0]
    tm = min(tm, t)

    def body(a_ref, w_ref, o_ref):
        o_ref[...] = _dot_bt(a_ref[...].astype(BF), w_ref[...]).astype(o_ref.dtype)

    return pl.pallas_call(
        body, name=name, grid=(t // tm,),
        in_specs=[pl.BlockSpec((tm, n), lambda i: (i, 0)), pl.BlockSpec((k, n), lambda i: (0, 0))],
        out_specs=pl.BlockSpec((tm, k), lambda i: (i, 0)),
        out_shape=jax.ShapeDtypeStruct((t, k), BF),
        compiler_params=_params(32, ("arbitrary",)),
    )(*_in_hbm(a, w))


def _gmlp_fwd(proj, gg, wt, bb, hg, tm):
    t = proj.shape[0]
    tm = min(tm, t)

    def body(u_ref, v_ref, gg_ref, wt_ref, bb_ref, hg_ref, out_ref):
        for cc in range(tm // CHUNK):
            rows = slice(cc * CHUNK, (cc + 1) * CHUNK)
            for g in range(GM_GROUPS):
                cols = slice(g * 128, (g + 1) * 128)
                u = _gelu(u_ref[rows, cols].astype(F32))
                gv = _gelu(v_ref[rows, cols].astype(F32))
                vn = gv * _rs(gv) * gg_ref[:, cols]
                mixed = _dot(wt_ref[g], vn.astype(BF)) + bb_ref[g]
                a = u * mixed
                out_ref[rows, cols] = (a * _rs(a) * hg_ref[:, cols]).astype(BF)

    return pl.pallas_call(
        body, name="gmlp_fwd", grid=(t // tm,),
        in_specs=[pl.BlockSpec((tm, 512), lambda i: (i, 0)), pl.BlockSpec((tm, 512), lambda i: (i, 1)),
                  pl.BlockSpec((1, 512), lambda i: (0, 0)), pl.BlockSpec((4, 128, 128), lambda i: (0, 0, 0)),
                  pl.BlockSpec((4, 128, 128), lambda i: (0, 0, 0)), pl.BlockSpec((1, 512), lambda i: (0, 0))],
        out_specs=pl.BlockSpec((tm, 512), lambda i: (i, 0)),
        out_shape=jax.ShapeDtypeStruct((t, 1024), BF),
        compiler_params=_params(32, ("arbitrary",)),
    )(*_in_hbm(proj, proj, gg, wt, bb, hg))


def _gmlp_bwd(proj, dmerged, gg, wt, wtt, bb, hg, tm, ride=None):
    t = proj.shape[0]
    tm = min(tm, t)
    nsteps = t // tm

    def body(u_ref, v_ref, dm_ref, gg_ref, wt_ref, wtt_ref, bb_ref, hg_ref,
             dp_ref, dw_ref, db_ref, dgg_ref, dhg_ref):
        i = pl.program_id(0)

        @pl.when(i == 0)
        def _():
            dw_ref[...] = jnp.zeros_like(dw_ref)
            db_ref[...] = jnp.zeros_like(db_ref)
            dgg_ref[...] = jnp.zeros_like(dgg_ref)
            dhg_ref[...] = jnp.zeros_like(dhg_ref)

        for cc in range(tm // CHUNK):
            rows = slice(cc * CHUNK, (cc + 1) * CHUNK)
            for g in range(GM_GROUPS):
                cols = slice(g * 128, (g + 1) * 128)
                up = u_ref[rows, cols].astype(F32)
                gp = v_ref[rows, cols].astype(F32)
                u, u_grad = _gelu_and_grad(up)
                gv, gv_grad = _gelu_and_grad(gp)
                rv = _rs(gv)
                gvh = gv * rv
                ggv = gg_ref[:, cols]
                vnb = (gvh * ggv).astype(BF)
                mixed = _dot(wt_ref[g], vnb) + bb_ref[g]
                a = u * mixed
                ra = _rs(a)
                ah = a * ra
                dm = dm_ref[rows, cols].astype(F32)
                dhg_ref[:, cols] += jnp.sum(dm * ah, axis=0, keepdims=True)
                dah = dm * hg_ref[:, cols]
                da = ra * (dah - ah * jnp.mean(dah * ah, axis=-1, keepdims=True))
                du = da * mixed
                dmix = da * u
                db_ref[g] += dmix
                dmb = dmix.astype(BF)
                dw_ref[g] += _dot_bt(dmb, vnb)
                dvn = _dot(wtt_ref[g], dmb)
                dgg_ref[:, cols] += jnp.sum(dvn * gvh, axis=0, keepdims=True)
                dgh = dvn * ggv
                dgv = rv * (dgh - gvh * jnp.mean(dgh * gvh, axis=-1, keepdims=True))
                dp_ref[rows, cols] = (du * u_grad).astype(BF)
                dp_ref[rows, 512 + g * 128:512 + (g + 1) * 128] = (dgv * gv_grad).astype(BF)

        @pl.when(i == nsteps - 1)
        def _():
            r = lax.broadcasted_iota(jnp.int32, (CHUNK, CHUNK), 0)
            c = lax.broadcasted_iota(jnp.int32, (CHUNK, CHUNK), 1)
            for g in range(GM_GROUPS):
                dw_ref[g] = jnp.where(c <= r, dw_ref[g], 0.0)
                db_ref[g] = jnp.broadcast_to(jnp.sum(db_ref[g], axis=-1, keepdims=True), (CHUNK, CHUNK))

    small = lambda shape: pl.BlockSpec(shape, lambda i: (0,) * len(shape))
    res, rode = _ride_call(
        body, "gmlp_bwd", (nsteps,),
        in_specs=[pl.BlockSpec((tm, 512), lambda i: (i, 0)), pl.BlockSpec((tm, 512), lambda i: (i, 1)),
                  pl.BlockSpec((tm, 512), lambda i: (i, 0)), small((1, 512)), small((4, 128, 128)),
                  small((4, 128, 128)), small((4, 128, 128)), small((1, 512))],
        out_specs=[pl.BlockSpec((tm, 1024), lambda i: (i, 0)), small((4, 128, 128)), small((4, 128, 128)),
                   small((1, 512)), small((1, 512))],
        out_shape=[jax.ShapeDtypeStruct((t, IN_COLS), BF), jax.ShapeDtypeStruct((4, 128, 128), F32),
                   jax.ShapeDtypeStruct((4, 128, 128), F32), jax.ShapeDtypeStruct((1, 512), F32),
                   jax.ShapeDtypeStruct((1, 512), F32)],
        scratch_shapes=[], operands=(proj, proj, dmerged, gg, wt, wtt, bb, hg), vmem_mb=32, ride=ride)
    return (*res, rode)


def _other_chips(x, y):
    return ((1 - x, y), (x, 1 - y), (1 - x, 1 - y))


class _GatherExchange:
    def __init__(self, shards):
        n = len(shards)
        self.n = n
        self.in_arrays = list(shards)
        self.out_shape = [jax.ShapeDtypeStruct((N_CHIPS,) + a.shape, a.dtype) for a in shards]
        self.half_rows = [a.shape[0] // 2 for a in shards]
        sems = lambda k: pltpu.SemaphoreType.DMA((k,))
        self.scratch_shapes = [pltpu.VMEM(a.shape, a.dtype) for a in shards] + [
            sems(3 * n), sems(3 * n), sems(3 * n), sems(3 * n), sems(n), sems(n)]

    def _copies(self, ins, outs, scr):
        n = self.n
        stages, (ici_send, ici_recv, d2d_send, d2d_recv, ld_sems, st_sems) = scr[:n], scr[n:]
        x, y, c = lax.axis_index("x"), lax.axis_index("y"), lax.axis_index("c")
        q = 2 * x + y
        loads = [pltpu.make_async_copy(ins[w], stages[w], ld_sems.at[w]) for w in range(n)]
        stores = [pltpu.make_async_copy(stages[w], outs[w].at[q], st_sems.at[w]) for w in range(n)]
        ici, d2d = [], []
        for w in range(n):
            half = pl.ds(c * self.half_rows[w], self.half_rows[w])
            for k, (px, py) in enumerate(_other_chips(x, y)):
                ici.append(pltpu.make_async_remote_copy(
                    src_ref=ins[w].at[half], dst_ref=outs[w].at[q, half], send_sem=ici_send.at[3 * w + k],
                    recv_sem=ici_recv.at[3 * w + k], device_id=(px, py, c), device_id_type=MESH))
                landed = outs[w].at[2 * px + py, half]
                d2d.append(pltpu.make_async_remote_copy(
                    src_ref=landed, dst_ref=landed, send_sem=d2d_send.at[3 * w + k],
                    recv_sem=d2d_recv.at[3 * w + k], device_id=(x, y, 1 - c), device_id_type=MESH))
        return loads, stores, ici, d2d

    def start(self, ins, outs, scr):
        loads, stores, ici, _ = self._copies(ins, outs, scr)
        for cp in loads + ici:
            cp.start()
        for ld, st in zip(loads, stores):
            ld.wait()
            st.start()

    def relay(self, ins, outs, scr):
        _, _, ici, d2d = self._copies(ins, outs, scr)
        for got, fwd in zip(ici, d2d):
            got.wait_recv()
            fwd.start()

    def finish(self, ins, outs, scr):
        _, stores, ici, d2d = self._copies(ins, outs, scr)
        for cp in ici:
            cp.wait_send()
        for cp in d2d + stores:
            cp.wait()


class _SiblingExchange:
    def __init__(self, slabs):
        n = len(slabs)
        self.n = n
        self.in_arrays = list(slabs)
        self.out_shape = [jax.ShapeDtypeStruct((N_CHIPS,) + a.shape[1:], a.dtype) for a in slabs]
        self.scratch_shapes = [pltpu.SemaphoreType.DMA((4 * n,)), pltpu.SemaphoreType.DMA((4 * n,))]

    def _copies(self, ins, outs, scr):
        send_sems, recv_sems = scr
        x, y, c = lax.axis_index("x"), lax.axis_index("y"), lax.axis_index("c")
        return [pltpu.make_async_remote_copy(
            src_ref=ins[w].at[2 * p + (1 - c)], dst_ref=outs[w].at[p], send_sem=send_sems.at[4 * w + p],
            recv_sem=recv_sems.at[4 * w + p], device_id=(x, y, 1 - c), device_id_type=MESH)
            for w in range(self.n) for p in range(N_CHIPS)]

    def start(self, ins, outs, scr):
        for cp in self._copies(ins, outs, scr):
            cp.start()

    def finish(self, ins, outs, scr):
        for cp in self._copies(ins, outs, scr):
            cp.wait()


class _ChipExchange:
    def __init__(self, sums):
        n = len(sums)
        self.n = n
        self.in_arrays = list(sums)
        self.out_shape = [jax.ShapeDtypeStruct(a.shape, a.dtype) for a in sums]
        self.scratch_shapes = [pltpu.SemaphoreType.DMA((3 * n,)), pltpu.SemaphoreType.DMA((3 * n,))]

    def _copies(self, ins, outs, scr):
        send_sems, recv_sems = scr
        x, y, c = lax.axis_index("x"), lax.axis_index("y"), lax.axis_index("c")
        q = 2 * x + y
        return [pltpu.make_async_remote_copy(
            src_ref=ins[w].at[2 * px + py], dst_ref=outs[w].at[q], send_sem=send_sems.at[3 * w + k],
            recv_sem=recv_sems.at[3 * w + k], device_id=(px, py, c), device_id_type=MESH)
            for w in range(self.n) for k, (px, py) in enumerate(_other_chips(x, y))]

    def start(self, ins, outs, scr):
        for cp in self._copies(ins, outs, scr):
            cp.start()

    def finish(self, ins, outs, scr):
        for cp in self._copies(ins, outs, scr):
            cp.wait()


class _NoExchange:
    in_arrays, out_shape, scratch_shapes = (), (), ()

    def start(self, ins, outs, scr):
        pass

    def finish(self, ins, outs, scr):
        pass


def _run_exchange(ex, name):
    n_in, n_out = len(ex.in_arrays), len(ex.out_shape)

    def body(*refs):
        ins, outs, scr = refs[:n_in], refs[n_in:n_in + n_out], refs[n_in + n_out:]
        ex.start(ins, outs, scr)
        if hasattr(ex, "relay"):
            ex.relay(ins, outs, scr)
        ex.finish(ins, outs, scr)

    return pl.pallas_call(
        body, name=name, in_specs=[ANY] * n_in, out_specs=[ANY] * n_out, out_shape=ex.out_shape,
        scratch_shapes=ex.scratch_shapes, compiler_params=_params(24),
    )(*ex.in_arrays)


def _ride_call(body, name, grid, in_specs, out_specs, out_shape, scratch_shapes, operands, vmem_mb, ride=None,
               aliases=None):
    ride = ride or _NoExchange()
    ni, no, ns = len(in_specs), len(out_specs), len(scratch_shapes)
    ri, ro = len(ride.in_arrays), len(ride.out_shape)
    total = math.prod(grid)

    def wrapped(*refs):
        ins, rins = refs[:ni], refs[ni:ni + ri]
        outs, routs = refs[ni + ri:ni + ri + no], refs[ni + ri + no:ni + ri + no + ro]
        scr, rscr = refs[ni + ri + no + ro:ni + ri + no + ro + ns], refs[ni + ri + no + ro + ns:]
        step = pl.program_id(0)
        for ax in range(1, len(grid)):
            step = step * grid[ax] + pl.program_id(ax)

        @pl.when(step == 0)
        def _():
            ride.start(rins, routs, rscr)

        if hasattr(ride, "relay"):
            @pl.when(step == (3 * total) // 4)
            def _():
                ride.relay(rins, routs, rscr)

        body(*ins, *outs, *scr)

        @pl.when(step == total - 1)
        def _():
            ride.finish(rins, routs, rscr)

    res = pl.pallas_call(
        wrapped, name=name, grid=grid, in_specs=list(in_specs) + [ANY] * ri, out_specs=list(out_specs) + [ANY] * ro,
        out_shape=list(out_shape) + list(ride.out_shape),
        scratch_shapes=list(scratch_shapes) + list(ride.scratch_shapes), input_output_aliases=aliases or {},
        compiler_params=_params(vmem_mb, ("arbitrary",) * len(grid)),
    )(*_in_hbm(*operands), *ride.in_arrays)
    return res[:no], res[no:]


def _neg_log_sig(z):
    n = jnp.maximum(z, 0.0) + jnp.log(1.0 + jnp.exp(-jnp.abs(z)))
    return n, z - n


def _running_sums(n, tri2):
    hi = n.astype(BF)
    lo = (n - hi.astype(F32)).astype(BF)
    return _dot(jnp.concatenate([hi, lo], axis=1), tri2)


def _head_sums(x, h0):
    s0 = jnp.sum(jnp.where(h0, x, 0.0), axis=-1, keepdims=True)
    s1 = jnp.sum(jnp.where(h0, 0.0, x), axis=-1, keepdims=True)
    return jnp.where(h0, s0, s1)


SB_BLOCKS_PER_STEP = 2


def _sb_masks(tq):
    h0 = lax.broadcasted_iota(jnp.int32, (tq, 128), 1) < HEAD_LANES
    r = lax.broadcasted_iota(jnp.int32, (2 * tq, tq), 0)
    c = lax.broadcasted_iota(jnp.int32, (2 * tq, tq), 1)
    return h0, c < jnp.where(r >= tq, r - tq, r)


def _sb_stack(x, h0):
    zero = jnp.zeros_like(x)
    return jnp.concatenate([jnp.where(h0, x, zero), jnp.where(h0, zero, x)], axis=0)


def _tri(tq, op):
    return op(lax.broadcasted_iota(jnp.int32, (tq, tq), 0), lax.broadcasted_iota(jnp.int32, (tq, tq), 1)).astype(BF)


def _sb_fwd(proj, merged, hg, nb, s, tq, ride=None):
    t = nb * s
    tq = min(tq, s)
    nq = s // tq
    per = min(SB_BLOCKS_PER_STEP, nq)
    ns = nq // per

    def body(q_ref, k_ref, v_ref, hg_ref, merged_ref, o_ref, tot_ref, mb_ref, nblk_ref, acc, cr, c_min):
        del merged_ref
        h0, causal = _sb_masks(tq)
        tri_gt = _tri(tq, lambda r, c: r > c)
        tri_gt = jnp.concatenate([tri_gt, tri_gt], axis=0)

        def query_block(i, rows):
            qst = _sb_stack(q_ref[rows, :] * SB_SCALE, h0)

            def block(j, masked, c_in):
                start = pl.multiple_of(j * tq, tq)
                kj = k_ref[pl.ds(start, tq), :]
                vj = v_ref[pl.ds(start, tq), :]
                n, l = _neg_log_sig(_dot_bt(qst, kj))
                if masked:
                    n = jnp.where(causal, n, 0.0)
                a = jnp.exp(l - (_running_sums(n, tri_gt) + c_in))
                if masked:
                    a = jnp.where(causal, a, 0.0)
                return _dot(a.astype(BF), vj), c_in + jnp.sum(n, axis=-1, keepdims=True)

            @pl.when(i == 0)
            def _():
                acc[...], c_diag = block(0, True, jnp.zeros((2 * tq, 1), F32))
                c_min[0] = jnp.min(c_diag)
                cr[...] = c_diag

            @pl.when(i > 0)
            def _():
                p_diag, c_diag = block(i, True, jnp.zeros((2 * tq, 1), F32))
                p_prev, c_prev = block(i - 1, False, c_diag)
                c_min[0] = jnp.min(c_prev)
                acc[...] = p_diag + p_prev
                cr[...] = c_prev

            def cond(carry):
                return jnp.logical_and(carry[0] < i, carry[1] < -SB_SKIP)

            def step(carry):
                p, c_new = block(i - 1 - carry[0], False, cr[...])
                acc[...] += p
                cr[...] = c_new
                return carry[0] + 1, jnp.min(c_new)

            walked, _ = lax.while_loop(cond, step, (jnp.minimum(i, 1), c_min[0]))
            return walked

        for u in range(per):
            rows = slice(u * tq, (u + 1) * tq)
            walked = query_block(pl.program_id(2) * per + u, rows)
            o = jnp.where(h0, acc[0:tq, :], acc[tq:2 * tq, :])
            o_ref[rows, :] = o
            tot_ref[rows, :] = jnp.where(h0, cr[0:tq, :], cr[tq:2 * tq, :])
            ro = lax.rsqrt(_head_sums(o * o, h0) * (1.0 / HEAD_LANES) + EPS)
            mb_ref[rows, :] = (o * ro * hg_ref[...]).astype(BF)
            nblk_ref[u * 8:(u + 1) * 8, :] = jnp.full((8, 128), walked.astype(F32))

    blk = lambda col0: pl.BlockSpec((per * tq, 128), lambda b, hp, i: (b * ns + i, col0 + hp))
    seq = lambda col0: pl.BlockSpec((s, 128), lambda b, hp, i: (b, col0 + hp))
    (o, tot, mb, nblk), rode = _ride_call(
        body, "sb_fwd", (nb, 4, ns),
        in_specs=[blk(8), seq(12), seq(16), pl.BlockSpec((1, 128), lambda b, hp, i: (0, 4 + hp)), ANY],
        out_specs=[blk(0), blk(0), blk(4),
                   pl.BlockSpec((None, None, per * 8, 128), lambda b, hp, i: (b, hp, i, 0))],
        out_shape=[jax.ShapeDtypeStruct((t, 512), F32), jax.ShapeDtypeStruct((t, 512), F32),
                   jax.ShapeDtypeStruct((t, 1024), BF), jax.ShapeDtypeStruct((nb, 4, nq * 8, 128), F32)],
        scratch_shapes=[pltpu.VMEM((2 * tq, 128), F32), pltpu.VMEM((2 * tq, 1), F32), pltpu.SMEM((1,), F32)],
        operands=(proj, proj, proj, hg, merged), vmem_mb=40, ride=ride, aliases={4: 2})
    return o, tot, mb, nblk, rode


def _sb_bwd(proj, o_sb, tot, nblk, dmerged, dproj, hg, nb, s, tq, ride=None):
    t = nb * s
    tq = min(tq, s)
    nq = s // tq
    per = min(SB_BLOCKS_PER_STEP, nq)
    ns = nq // per

    def body(q_ref, k_ref, v_ref, o_ref, tot_ref, nblk_ref, dm_ref, hg_ref, dproj_ref,
             dq_ref, dk_ref, dv_ref, dhg_ref, dk_acc, dv_acc, dq_acc, cm, cg):
        del dproj_ref
        h0, causal = _sb_masks(tq)
        tri_le = _tri(tq, lambda r, c: r <= c)
        tri_le = jnp.concatenate([tri_le, tri_le], axis=0)
        tri_lt = _tri(tq, lambda r, c: r < c)

        @pl.when(pl.program_id(2) == 0)
        def _():
            dk_acc[...] = jnp.zeros_like(dk_acc)
            dv_acc[...] = jnp.zeros_like(dv_acc)
            dhg_ref[...] = jnp.zeros_like(dhg_ref)

        def query_block(i, rows):
            qst = _sb_stack(q_ref[rows, :] * SB_SCALE, h0)
            for ref in (dq_acc, cm, cg):
                ref[...] = jnp.zeros_like(ref)

            o = o_ref[rows, :]
            ro = lax.rsqrt(_head_sums(o * o, h0) * (1.0 / HEAD_LANES) + EPS)
            oh = o * ro
            dm = dm_ref[rows, :].astype(F32)
            dhg_ref[...] += jnp.sum(dm * oh, axis=0, keepdims=True)
            doh = dm * hg_ref[...]
            dost = _sb_stack((ro * (doh - oh * (_head_sums(doh * oh, h0) * (1.0 / HEAD_LANES)))).astype(BF), h0)
            tots = jnp.concatenate([tot_ref[rows, 0:1], tot_ref[rows, HEAD_LANES:HEAD_LANES + 1]], axis=0)
            qst_t = qst.T
            dost_t = dost.T

            def block(j, masked, cm_in, cg_in):
                start = pl.multiple_of(j * tq, tq)
                kj = k_ref[pl.ds(start, tq), :]
                vj = v_ref[pl.ds(start, tq), :]
                n, l = _neg_log_sig(_dot_bt(qst, kj))
                if masked:
                    n = jnp.where(causal, n, 0.0)
                a = jnp.exp(l - (tots - cm_in - _running_sums(n, tri_le)))
                if masked:
                    a = jnp.where(causal, a, 0.0)
                gm = a * _dot_bt(dost, vj)
                pp = cg_in + _dot(gm.astype(BF), tri_lt)
                dz = gm - jnp.exp(l) * (gm + pp)
                if masked:
                    dz = jnp.where(causal, dz, 0.0)
                dzb = dz.astype(BF)
                dk_acc[:, pl.ds(start, tq)] += _dot(qst_t, dzb)
                dv_acc[:, pl.ds(start, tq)] += _dot(dost_t, a.astype(BF))
                return (_dot(dzb, kj), cm_in + jnp.sum(n, axis=-1, keepdims=True),
                        cg_in + jnp.sum(gm, axis=-1, keepdims=True))

            def step(j, carry):
                dq, cm[...], cg[...] = block(j, False, cm[...], cg[...])
                dq_acc[...] += dq
                return carry

            walked = jnp.clip(nblk_ref[pl.program_id(0), pl.program_id(1), i].astype(jnp.int32),
                              jnp.minimum(i, 1), i)
            lax.fori_loop(i - walked, i - 1, step, 0)

            @pl.when(i == 0)
            def _():
                dq_acc[...] = block(0, True, cm[...], cg[...])[0]

            @pl.when(i > 0)
            def _():
                dq_prev, cm_prev, cg_prev = block(i - 1, False, cm[...], cg[...])
                dq_acc[...] += dq_prev + block(i, True, cm_prev, cg_prev)[0]

            dq_ref[rows, :] = (jnp.where(h0, dq_acc[0:tq, :], dq_acc[tq:2 * tq, :]) * SB_SCALE).astype(BF)

        for u in range(per):
            query_block(pl.program_id(2) * per + u, slice(u * tq, (u + 1) * tq))

        @pl.when(pl.program_id(2) == ns - 1)
        def _():
            dk_ref[...] = dk_acc[...].T.astype(BF)
            dv_ref[...] = dv_acc[...].T.astype(BF)

    blk = lambda col0: pl.BlockSpec((per * tq, 128), lambda b, hp, i: (b * ns + i, col0 + hp))
    seq = lambda col0: pl.BlockSpec((s, 128), lambda b, hp, i: (b, col0 + hp))
    (dq, dk, dv, dhg), rode = _ride_call(
        body, "sb_bwd", (nb, 4, ns),
        in_specs=[blk(8), seq(12), seq(16), blk(0), blk(0), pl.BlockSpec(memory_space=pltpu.SMEM), blk(4),
                  pl.BlockSpec((1, 128), lambda b, hp, i: (0, 4 + hp)), ANY],
        out_specs=[blk(8), seq(0), seq(0), pl.BlockSpec((None, 1, 128), lambda b, hp, i: (b, 0, hp))],
        out_shape=[jax.ShapeDtypeStruct((t, IN_COLS), BF), jax.ShapeDtypeStruct((t, 512), BF),
                   jax.ShapeDtypeStruct((t, 512), BF), jax.ShapeDtypeStruct((nb, 1, 512), F32)],
        scratch_shapes=[pltpu.VMEM((128, s), F32), pltpu.VMEM((128, s), F32), pltpu.VMEM((2 * tq, 128), F32),
                        pltpu.VMEM((2 * tq, 1), F32), pltpu.VMEM((2 * tq, 1), F32)],
        operands=(proj, proj, proj, o_sb, tot, nblk.reshape(nb, 4, nq, 8, 128)[:, :, :, 0, 0], dmerged, hg, dproj),
        vmem_mb=40, ride=ride, aliases={8: 0})
    return dq, dk, dv, dhg, rode


def _place(buf, piece, col_block, name):
    t, w = piece.shape
    tm = min(t, 1024)

    def body(piece_ref, buf_ref, out_ref):
        del buf_ref
        out_ref[...] = piece_ref[...]

    return pl.pallas_call(
        body, name=name, grid=(t // tm,),
        in_specs=[pl.BlockSpec((tm, w), lambda i: (i, 0)), ANY],
        out_specs=pl.BlockSpec((tm, w), lambda i: (i, col_block)),
        out_shape=jax.ShapeDtypeStruct(buf.shape, buf.dtype), input_output_aliases={1: 0},
        compiler_params=_params(16, ("arbitrary",)),
    )(piece, buf)


def _softmax_rows(sc):
    e = jnp.exp(sc - jnp.max(sc, axis=-1, keepdims=True))
    return e / jnp.sum(e, axis=-1, keepdims=True)


def _mix_cross_fwd(x, merged, w_out, gc, w_cq, kv, w_co, s, tm):
    t, d = x.shape
    tm = min(tm, s)
    per = s // tm
    inv = 1.0 / math.sqrt(X_HEAD_DIM)

    def body(x_ref, m_ref, wo_ref, gc_ref, wq_ref, kv_ref, wc_ref, h1_ref, h2_ref, hn_ref, qc_ref, oc_ref):
        h1 = x_ref[...] + _dot(m_ref[...], wo_ref[...])
        h1_ref[...] = h1
        hn = (h1 * _rs(h1) * gc_ref[...]).astype(BF)
        hn_ref[...] = hn
        qc = _dot(hn, wq_ref[...]).astype(BF)
        qc_ref[...] = qc
        for h in range(X_HEADS):
            cols = slice(h * X_HEAD_DIM, (h + 1) * X_HEAD_DIM)
            kh = kv_ref[:, h * X_HEAD_DIM:(h + 1) * X_HEAD_DIM]
            vh = kv_ref[:, d + h * X_HEAD_DIM:d + (h + 1) * X_HEAD_DIM]
            p = _softmax_rows(_dot_bt(qc[:, cols], kh) * inv)
            oc_ref[:, cols] = _dot(p.astype(BF), vh).astype(BF)
        h2_ref[...] = h1 + _dot(oc_ref[...], wc_ref[...])

    row = lambda width: pl.BlockSpec((tm, width), lambda i: (i, 0))
    full = lambda a, b: pl.BlockSpec((a, b), lambda i: (0, 0))
    return pl.pallas_call(
        body, name="mix_cross_fwd", grid=(t // tm,),
        in_specs=[row(d), row(d), full(d, d), full(1, d), full(d, d),
                  pl.BlockSpec((N_MEM, 2 * d), lambda i: (i // per, 0)), full(d, d)],
        out_specs=[row(d), row(d), row(d), row(d), row(d)],
        out_shape=[jax.ShapeDtypeStruct((t, d), F32), jax.ShapeDtypeStruct((t, d), F32),
                   jax.ShapeDtypeStruct((t, d), BF), jax.ShapeDtypeStruct((t, d), BF),
                   jax.ShapeDtypeStruct((t, d), BF)],
        compiler_params=_params(48, ("arbitrary",)),
    )(*_in_hbm(x, merged, w_out, gc, w_cq, kv, w_co))


def _cross_bwd(dh2, h1, qc, gc, w_cq, kv, w_co, s, tm):
    t, d = dh2.shape
    tm = min(tm, s)
    per = s // tm
    nb = t // s
    inv = 1.0 / math.sqrt(X_HEAD_DIM)

    def body(dh2_ref, h1_ref, qc_ref, gc_ref, wq_ref, kv_ref, wc_ref, dh1_ref, dqc_ref, dkv_ref, dgc_ref):
        i = pl.program_id(0)

        @pl.when(i == 0)
        def _():
            dgc_ref[...] = jnp.zeros_like(dgc_ref)

        @pl.when(i % per == 0)
        def _():
            dkv_ref[...] = jnp.zeros_like(dkv_ref)

        dh2 = dh2_ref[...]
        h1 = h1_ref[...]
        r = _rs(h1)
        h1h = h1 * r
        gcv = gc_ref[...]
        qc = qc_ref[...]
        do = _dot_bt(dh2.astype(BF), wc_ref[...]).astype(BF)
        for h in range(X_HEADS):
            cols = slice(h * X_HEAD_DIM, (h + 1) * X_HEAD_DIM)
            vcols = slice(d + h * X_HEAD_DIM, d + (h + 1) * X_HEAD_DIM)
            kh = kv_ref[:, cols]
            vh = kv_ref[:, vcols]
            p = _softmax_rows(_dot_bt(qc[:, cols], kh) * inv)
            dp = _dot_bt(do[:, cols], vh)
            ds = (p * (dp - jnp.sum(dp * p, axis=-1, keepdims=True)) * inv).astype(BF)
            dqc_ref[:, cols] = _dot(ds, kh).astype(BF)
            dkv_ref[:, cols] += _dot_at(ds, qc[:, cols])
            dkv_ref[:, vcols] += _dot_at(p.astype(BF), do[:, cols])
        dhn = _dot_bt(dqc_ref[...], wq_ref[...])
        dx, dg = _rms_bwd(dhn, h1h, r, gcv)
        dh1_ref[...] = dh2 + dx
        dgc_ref[...] += jnp.sum(dg, axis=0, keepdims=True)

    row = lambda width: pl.BlockSpec((tm, width), lambda i: (i, 0))
    full = lambda a, b: pl.BlockSpec((a, b), lambda i: (0, 0))
    kvspec = pl.BlockSpec((N_MEM, 2 * d), lambda i: (i // per, 0))
    return pl.pallas_call(
        body, name="cross_bwd", grid=(t // tm,),
        in_specs=[row(d), row(d), row(d), full(1, d), full(d, d), kvspec, full(d, d)],
        out_specs=[row(d), row(d), kvspec, full(1, d)],
        out_shape=[jax.ShapeDtypeStruct((t, d), F32), jax.ShapeDtypeStruct((t, d), BF),
                   jax.ShapeDtypeStruct((nb * N_MEM, 2 * d), F32), jax.ShapeDtypeStruct((1, d), F32)],
        compiler_params=_params(48, ("arbitrary",)),
    )(*_in_hbm(dh2, h1, qc, gc, w_cq, kv, w_co))


def _mem_bwd(mem, gm, dkv, w_ckv, tm):
    t, d = mem.shape
    tm = min(tm, t)

    def body(mem_ref, dkv_ref, w_ref, dg_ref):
        @pl.when(pl.program_id(0) == 0)
        def _():
            dg_ref[...] = jnp.zeros_like(dg_ref)

        mv = mem_ref[...]
        dmn = _dot_bt(dkv_ref[...].astype(BF), w_ref[...])
        dg_ref[...] += jnp.sum(dmn * (mv * _rs(mv)), axis=0, keepdims=True)

    del gm
    return pl.pallas_call(
        body, name="mem_bwd", grid=(t // tm,),
        in_specs=[pl.BlockSpec((tm, d), lambda i: (i, 0)), pl.BlockSpec((tm, 2 * d), lambda i: (i, 0)),
                  pl.BlockSpec((d, 2 * d), lambda i: (0, 0))],
        out_specs=pl.BlockSpec((1, d), lambda i: (0, 0)),
        out_shape=jax.ShapeDtypeStruct((1, d), F32),
        compiler_params=_params(32, ("arbitrary",)),
    )(mem, dkv, w_ckv)


def _ffn_loss_fwd(h2, gf, w1, w2, gl, target, tm):
    t, d = h2.shape
    tm = min(tm, t)

    def body(h2_ref, gf_ref, w1_ref, w2_ref, gl_ref, tg_ref, hn_ref, f_ref, dh3_ref, dgl_ref, loss_ref):
        @pl.when(pl.program_id(0) == 0)
        def _():
            dgl_ref[...] = jnp.zeros_like(dgl_ref)
            loss_ref[...] = jnp.zeros_like(loss_ref)

        h2 = h2_ref[...]
        hn = (h2 * _rs(h2) * gf_ref[...]).astype(BF)
        hn_ref[...] = hn
        h3 = h2
        for c in range(4):
            f = jnp.maximum(_dot(hn, w1_ref[c]), 0.0)
            f_ref[:, c * 1024:(c + 1) * 1024] = f.astype(BF)
            h3 = h3 + _dot((f * f).astype(BF), w2_ref[c])
        r3 = _rs(h3)
        yh = h3 * r3
        glv = gl_ref[...]
        e = yh * glv - tg_ref[...]
        loss_ref[...] += 0.5 * jnp.sum(jnp.sum(e * e, axis=-1, keepdims=True) * (1.0 / d), axis=0, keepdims=True)
        dy = e * (1.0 / d)
        dx, dg = _rms_bwd(dy, yh, r3, glv)
        dh3_ref[...] = dx
        dgl_ref[...] += jnp.sum(dg, axis=0, keepdims=True)

    row = lambda width: pl.BlockSpec((tm, width), lambda i: (i, 0))
    return pl.pallas_call(
        body, name="ffn_loss_fwd", grid=(t // tm,),
        in_specs=[row(d), pl.BlockSpec((1, d), lambda i: (0, 0)), pl.BlockSpec((4, d, 1024), lambda i: (0, 0, 0), pipeline_mode=pl.Buffered(1)),
                  pl.BlockSpec((4, 1024, d), lambda i: (0, 0, 0), pipeline_mode=pl.Buffered(1)),
                  pl.BlockSpec((1, d), lambda i: (0, 0)), row(d)],
        out_specs=[row(d), row(D_FF), row(d), pl.BlockSpec((1, d), lambda i: (0, 0)),
                   pl.BlockSpec((1, 1), lambda i: (0, 0))],
        out_shape=[jax.ShapeDtypeStruct((t, d), BF), jax.ShapeDtypeStruct((t, D_FF), BF),
                   jax.ShapeDtypeStruct((t, d), F32), jax.ShapeDtypeStruct((1, d), F32),
                   jax.ShapeDtypeStruct((1, 1), F32)],
        compiler_params=_params(56, ("arbitrary",)),
    )(*_in_hbm(h2, gf, w1, w2, gl, target))


def _ffn_bwd(dh3, f, h2, gf, w1, w2, tm):
    t, d = h2.shape
    tm = min(tm, t)

    def body(dh3_ref, f_ref, h2_ref, gf_ref, w1_ref, w2_ref, dh2_ref, dpre_ref, dgf_ref):
        @pl.when(pl.program_id(0) == 0)
        def _():
            dgf_ref[...] = jnp.zeros_like(dgf_ref)

        dh3 = dh3_ref[...]
        dh3b = dh3.astype(BF)
        dhn = jnp.zeros((tm, d), F32)
        for c in range(4):
            cols = slice(c * 1024, (c + 1) * 1024)
            dpre = (_dot_bt(dh3b, w2_ref[c]) * (2.0 * f_ref[:, cols].astype(F32))).astype(BF)
            dpre_ref[:, cols] = dpre
            dhn = dhn + _dot_bt(dpre, w1_ref[c])
        h2 = h2_ref[...]
        r = _rs(h2)
        dx, dg = _rms_bwd(dhn, h2 * r, r, gf_ref[...])
        dh2_ref[...] = dh3 + dx
        dgf_ref[...] += jnp.sum(dg, axis=0, keepdims=True)

    row = lambda width: pl.BlockSpec((tm, width), lambda i: (i, 0))
    return pl.pallas_call(
        body, name="ffn_bwd", grid=(t // tm,),
        in_specs=[row(d), row(D_FF), row(d), pl.BlockSpec((1, d), lambda i: (0, 0)),
                  pl.BlockSpec((4, d, 1024), lambda i: (0, 0, 0), pipeline_mode=pl.Buffered(1)),
                  pl.BlockSpec((4, 1024, d), lambda i: (0, 0, 0), pipeline_mode=pl.Buffered(1))],
        out_specs=[row(d), row(D_FF), pl.BlockSpec((1, d), lambda i: (0, 0))],
        out_shape=[jax.ShapeDtypeStruct((t, d), F32), jax.ShapeDtypeStruct((t, D_FF), BF),
                   jax.ShapeDtypeStruct((1, d), F32)],
        compiler_params=_params(56, ("arbitrary",)),
    )(*_in_hbm(dh3, f, h2, gf, w1, w2))


def _in_bwd(dproj, dh1, x, g, w_in, tm, ride=None):
    t, d = x.shape
    n = w_in.shape[1]
    tm = min(tm, t)

    def body(dp_ref, dh1_ref, x_ref, g_ref, w_ref, dx_ref, dg_ref):
        @pl.when(pl.program_id(0) == 0)
        def _():
            dg_ref[...] = jnp.zeros_like(dg_ref)

        dxn = _dot_bt(dp_ref[...], w_ref[...])
        xv = x_ref[...]
        r = _rs(xv)
        dx, dg = _rms_bwd(dxn, xv * r, r, g_ref[...])
        dx_ref[...] = dh1_ref[...] + dx
        dg_ref[...] += jnp.sum(dg, axis=0, keepdims=True)

    row = lambda width: pl.BlockSpec((tm, width), lambda i: (i, 0))
    (dx, dg), rode = _ride_call(
        body, "in_bwd", (t // tm,),
        in_specs=[row(n), row(d), row(d), pl.BlockSpec((1, d), lambda i: (0, 0)),
                  pl.BlockSpec((d, n), lambda i: (0, 0))],
        out_specs=[row(d), pl.BlockSpec((1, d), lambda i: (0, 0))],
        out_shape=[jax.ShapeDtypeStruct((t, d), F32), jax.ShapeDtypeStruct((1, d), F32)],
        scratch_shapes=[], operands=(dproj, dh1, x, g, w_in), vmem_mb=48, ride=ride)
    return dx, dg, rode


class _GradReduce:
    def __init__(self, c_idx):
        self.c_idx = c_idx
        self.sums = {}

    def sibling(self, slabs):
        return _SiblingExchange(slabs)

    def chip(self, names, slabs, recv):
        for k, a, r in zip(names, slabs, recv):
            self.sums[k] = _chip_sum(a, r, self.c_idx, "chip_sum_" + k)
        return _ChipExchange([self.sums[k] for k in names])


def _full_weights(gathered):
    d = D_MODEL
    out = {}
    for k, a in gathered.items():
        if k in ("w_in", "w_ckv", "w_ff1"):
            out[k] = a.transpose(1, 0, 2).reshape(d, -1)
        else:
            out[k] = a.reshape(-1, d)
    return out


def _slabs(a):
    return a.reshape(N_DEV, -1, a.shape[-1])


def _local_step(x, mem, target, small, big, nb, s, tq=256, gather_rest=None, reduce=None):
    d = D_MODEL
    g_mix, g_v, w_sp, b_sp, g_head, g_cross, g_mem, g_ffn, g_fin = (
        small[k] for k in ("norm_mix_g", "gm_v_norm_g", "w_spatial", "b_spatial", "head_norm_g", "norm_cross_g",
                           "norm_mem_g", "norm_ffn_g", "norm_final_g"))
    tri = jnp.tril(jnp.ones((CHUNK, CHUNK), dtype=bool))
    w_sp_m = jnp.where(tri[None], w_sp, 0.0)
    wt = w_sp_m.astype(BF)
    wtt = jnp.swapaxes(w_sp_m, 1, 2).astype(BF)
    bb = jnp.broadcast_to(b_sp[:, :, None], (GM_GROUPS, CHUNK, CHUNK))
    hg_a = g_head[:, :GM_WIDTH]

    proj, xn = _norm_matmul(x, g_mix, big["w_in"], 512, "in_proj")
    merged = _gmlp_fwd(proj, g_v, wt, bb, hg_a, 512)
    o_sb, tot, merged, nblk, gathered = _sb_fwd(proj, merged, g_head, nb, s, tq, ride=gather_rest)
    if gather_rest is not None:
        big = dict(big, **_full_weights(dict(zip(BIG[1:], gathered))))
    w1c = big["w_ff1"].reshape(d, 4, 1024).transpose(1, 0, 2)
    w2c = big["w_ff2"].reshape(4, 1024, d)
    kv, memn = _norm_matmul(mem, g_mem, big["w_ckv"], 512, "mem_proj")
    h1, h2, hn, qc, oc = _mix_cross_fwd(x, merged, big["w_out"], g_cross, big["w_cq"], kv, big["w_co"], s, 512)
    hn2, f, dh3, d_fin, loss = _ffn_loss_fwd(h2, g_ffn, w1c, w2c, g_fin, target, 512)

    gbig = {}
    dh2, dpre, d_ffn = _ffn_bwd(dh3, f, h2, g_ffn, w1c, w2c, 512)
    gbig["w_ff2"] = _slabs(_wgrad(f, dh3, 1024, 1024, "wgrad_ff2", square_a=True))
    gbig["w_ff1"] = _slabs(_wgrad_wide(hn2, dpre, 512, 1024, "wgrad_ff1", col_shards=4))
    dh1, dqc, dkv, d_cross = _cross_bwd(dh2, h1, qc, g_cross, big["w_cq"], kv, big["w_co"], s, 512)
    gbig["w_co"] = _slabs(_wgrad(oc, dh2, 1024, 1024, "wgrad_co"))
    gbig["w_cq"] = _slabs(_wgrad(hn, dqc, 1024, 1024, "wgrad_cq"))
    gbig["w_ckv"] = _slabs(_wgrad(memn, dkv, 512, 1024, "wgrad_ckv", col_shards=4))
    d_mem = _mem_bwd(mem, g_mem, dkv, big["w_ckv"], 512)
    dmerged = _matmul_bt(dh1, big["w_out"], 512, "out_bwd")
    gbig["w_out"] = _slabs(_wgrad(merged, dh1, 1024, 1024, "wgrad_out"))
    rest = BIG[1:]
    ride = reduce.sibling([gbig[k] for k in rest]) if reduce else None
    dproj, d_wsp, d_bb, d_gv, d_hga, recv = _gmlp_bwd(proj, dmerged, g_v, wt, wtt, bb, hg_a, 512, ride=ride)
    ride = reduce.chip(rest, [gbig[k] for k in rest], recv) if reduce else None
    dproj, dk, dv, d_hgb, parts_rest = _sb_bwd(proj, o_sb, tot, nblk, dmerged, dproj, g_head, nb, s, tq, ride=ride)
    dproj = _place(_place(dproj, dk, 3, "place_dk"), dv, 4, "place_dv")
    gbig["w_in"] = _slabs(_wgrad_wide(xn, dproj, 512, 1024, "wgrad_in", col_shards=4))
    last = None
    if reduce:
        recv = _run_exchange(reduce.sibling([gbig["w_in"]]), "grad_sibling_exchange_w_in")
        last = reduce.chip(["w_in"], [gbig["w_in"]], recv)
    grad_x, d_mix, _ = _in_bwd(dproj, dh1, x, g_mix, big["w_in"], 512)
    parts = dict(zip(rest, parts_rest))

    gsmall = {
        "norm_mix_g": d_mix, "gm_v_norm_g": d_gv, "w_spatial": d_wsp, "b_spatial": d_bb[:, :, 0],
        "head_norm_g": jnp.concatenate([d_hga, jnp.sum(d_hgb, axis=0)], axis=1), "norm_cross_g": d_cross,
        "norm_mem_g": d_mem, "norm_ffn_g": d_ffn, "norm_final_g": d_fin,
    }
    return loss, grad_x, gsmall, gbig, parts, last


BIG = ("w_in", "w_out", "w_cq", "w_ckv", "w_co", "w_ff1", "w_ff2")
SMALL = ("norm_mix_g", "gm_v_norm_g", "w_spatial", "b_spatial", "head_norm_g", "norm_cross_g", "norm_mem_g",
         "norm_ffn_g", "norm_final_g")


def _local_copies_start(srcs, stages, sems):
    loads = [pltpu.make_async_copy(src, stage, sems.at[w]) for w, (src, stage) in enumerate(zip(srcs, stages))]
    for ld in loads:
        ld.start()
    return loads


def _local_copies_finish(loads, stages, dsts, sems):
    stores = []
    for w, (ld, stage, dst) in enumerate(zip(loads, stages, dsts)):
        ld.wait()
        st = pltpu.make_async_copy(stage, dst, sems.at[w])
        st.start()
        stores.append(st)
    for st in stores:
        st.wait()


def _chip_sum(slabs, recv, c_idx, name):
    _, r, cw = slabs.shape
    tr = min(r, 256)

    def body(c_ref, a_ref, b_ref, o_ref):
        del c_ref
        o_ref[...] = (a_ref[...] + b_ref[...]).astype(BF)

    return pl.pallas_call(
        body, name=name,
        grid_spec=pltpu.PrefetchScalarGridSpec(
            num_scalar_prefetch=1, grid=(N_CHIPS, r // tr),
            in_specs=[pl.BlockSpec((None, tr, cw), lambda p, i, c_ref: (2 * p + c_ref[0], i, 0)),
                      pl.BlockSpec((None, tr, cw), lambda p, i, c_ref: (p, i, 0))],
            out_specs=pl.BlockSpec((None, tr, cw), lambda p, i, c_ref: (p, i, 0))),
        out_shape=jax.ShapeDtypeStruct((N_CHIPS, r, cw), BF),
        compiler_params=_params(32, ("arbitrary", "arbitrary")),
    )(c_idx, *_in_hbm(slabs, recv))


def _sum4(sums, parts, q_idx, name):
    _, r, cw = parts.shape
    tr = min(r, 256)

    def body(q_ref, own_ref, a_ref, b_ref, c_ref, o_ref):
        del q_ref
        o_ref[...] = ((own_ref[...].astype(F32) + a_ref[...].astype(F32)) + b_ref[...].astype(F32)) + c_ref[
            ...].astype(F32)

    spec = lambda k: pl.BlockSpec((None, tr, cw), lambda i, q_ref: ((q_ref[0] + k) % N_CHIPS, i, 0))
    return pl.pallas_call(
        body, name=name,
        grid_spec=pltpu.PrefetchScalarGridSpec(
            num_scalar_prefetch=1, grid=(r // tr,), in_specs=[spec(0), spec(1), spec(2), spec(3)],
            out_specs=pl.BlockSpec((tr, cw), lambda i, q_ref: (i, 0))),
        out_shape=jax.ShapeDtypeStruct((r, cw), F32),
        compiler_params=_params(32, ("arbitrary",)),
    )(q_idx, *_in_hbm(sums, parts, parts, parts))


def _half_exchange(halves):
    n = len(halves)

    def body(*refs):
        ins, outs, stages = refs[:n], refs[n:2 * n], refs[2 * n:3 * n]
        send_sems, recv_sems, ld_sems, st_sems = refs[3 * n:]
        x, y, c = lax.axis_index("x"), lax.axis_index("y"), lax.axis_index("c")
        loads = _local_copies_start(ins, stages, ld_sems)
        copies = []
        for w in range(n):
            cp = pltpu.make_async_remote_copy(
                src_ref=ins[w], dst_ref=outs[w].at[c], send_sem=send_sems.at[w], recv_sem=recv_sems.at[w],
                device_id=(x, y, 1 - c), device_id_type=MESH)
            cp.start()
            copies.append(cp)
        _local_copies_finish(loads, stages, [outs[w].at[c] for w in range(n)], st_sems)
        for cp in copies:
            cp.wait()

    return pl.pallas_call(
        body, name="grad_half_exchange",
        in_specs=[ANY] * n, out_specs=[ANY] * n,
        out_shape=[jax.ShapeDtypeStruct((2,) + a.shape, a.dtype) for a in halves],
        scratch_shapes=[pltpu.VMEM(a.shape, a.dtype) for a in halves] + [
            pltpu.SemaphoreType.DMA((n,)), pltpu.SemaphoreType.DMA((n,)),
            pltpu.SemaphoreType.DMA((n,)), pltpu.SemaphoreType.DMA((n,))],
        compiler_params=_params(24),
    )(*halves)


def _small_all_reduce(packed, ride=None):
    rows = packed.shape[0]
    ride = ride or _NoExchange()
    ri, ro = len(ride.in_arrays), len(ride.out_shape)

    def body(*refs):
        in_ref, rins, out_ref, routs = refs[0], refs[1:1 + ri], refs[1 + ri], refs[2 + ri:2 + ri + ro]
        pair, chip_sum, chips, d2d_send, d2d_recv, ici_send, ici_recv = refs[2 + ri + ro:9 + ri + ro]
        rscr = refs[9 + ri + ro:]
        ride.start(rins, routs, rscr)
        x, y, c = lax.axis_index("x"), lax.axis_index("y"), lax.axis_index("c")
        q = 2 * x + y
        pair[c] = in_ref[...]
        swap = pltpu.make_async_remote_copy(
            src_ref=in_ref, dst_ref=pair.at[c], send_sem=d2d_send, recv_sem=d2d_recv,
            device_id=(x, y, 1 - c), device_id_type=MESH)
        swap.start()
        swap.wait()
        both = pair[0] + pair[1]
        chip_sum[...] = both
        chips[q] = both
        copies = [pltpu.make_async_remote_copy(
            src_ref=chip_sum, dst_ref=chips.at[q], send_sem=ici_send.at[k], recv_sem=ici_recv.at[k],
            device_id=(px, py, c), device_id_type=MESH) for k, (px, py) in enumerate(_other_chips(x, y))]
        for cp in copies:
            cp.start()
        for cp in copies:
            cp.wait()
        out_ref[...] = ((chips[0] + chips[1]) + chips[2]) + chips[3]
        ride.finish(rins, routs, rscr)

    vmem = pl.BlockSpec(memory_space=pltpu.VMEM)
    res = pl.pallas_call(
        body, name="small_all_reduce",
        in_specs=[vmem] + [ANY] * ri, out_specs=[vmem] + [ANY] * ro,
        out_shape=[jax.ShapeDtypeStruct(packed.shape, F32)] + list(ride.out_shape),
        scratch_shapes=[pltpu.VMEM((2, rows, 128), F32), pltpu.VMEM((rows, 128), F32),
                        pltpu.VMEM((N_CHIPS, rows, 128), F32), pltpu.SemaphoreType.DMA, pltpu.SemaphoreType.DMA,
                        pltpu.SemaphoreType.DMA((3,)), pltpu.SemaphoreType.DMA((3,))] + list(ride.scratch_shapes),
        compiler_params=_params(16),
    )(packed, *ride.in_arrays)
    return res[0], res[1:]


def _adamw(g, w, m, v, name):
    r, cw = g.shape
    tr = 256 if r % 256 == 0 else r

    def body(g_ref, w_ref, m_ref, v_ref, d_ref, nm_ref, nv_ref):
        gv = g_ref[...]
        nm = ADAM_B1 * m_ref[...] + (1.0 - ADAM_B1) * gv
        nv = ADAM_B2 * v_ref[...] + (1.0 - ADAM_B2) * (gv * gv)
        m_hat = nm / (1.0 - ADAM_B1 ** ADAM_STEP)
        v_hat = nv / (1.0 - ADAM_B2 ** ADAM_STEP)
        d_ref[...] = -ADAM_LR * (m_hat / (jnp.sqrt(v_hat) + ADAM_EPS) + ADAM_WD * w_ref[...])
        nm_ref[...] = nm
        nv_ref[...] = nv

    spec = pl.BlockSpec((tr, cw), lambda i: (i, 0))
    return pl.pallas_call(
        body, name=name, grid=(r // tr,),
        in_specs=[spec] * 4, out_specs=[spec] * 3,
        out_shape=[jax.ShapeDtypeStruct((r, cw), F32)] * 3,
        compiler_params=_params(32, ("arbitrary",)),
    )(*_in_hbm(g, w, m, v))


def _small_params(args):
    small = {k: args[k].reshape(1, -1) for k in SMALL}
    small["w_spatial"] = args["w_spatial"][0]
    small["b_spatial"] = args["b_spatial"][0]
    return small


def _pack(parts, rows):
    flat = jnp.concatenate([p.reshape(-1).astype(F32) for p in parts])
    return jnp.pad(flat, (0, rows * 128 - flat.shape[0])).reshape(rows, 128)


def _unpack(packed, shapes):
    flat = packed.reshape(-1)
    out, off = [], 0
    for shp in shapes:
        size = math.prod(shp)
        out.append(flat[off:off + size].reshape(shp))
        off += size
    return out


def kernel(x, mem, norm_mix_g, w_in, gm_v_norm_g, w_spatial, b_spatial, head_norm_g, w_out, norm_cross_g, norm_mem_g, w_cq, w_ckv, w_co, norm_ffn_g, w_ff1, w_ff2, norm_final_g, loss_target, m_norm_mix_g, m_w_in, m_gm_v_norm_g, m_w_spatial, m_b_spatial, m_head_norm_g, m_w_out, m_norm_cross_g, m_norm_mem_g, m_w_cq, m_w_ckv, m_w_co, m_norm_ffn_g, m_w_ff1, m_w_ff2, m_norm_final_g, v_norm_mix_g, v_w_in, v_gm_v_norm_g, v_w_spatial, v_b_spatial, v_head_norm_g, v_w_out, v_norm_cross_g, v_norm_mem_g, v_w_cq, v_w_ckv, v_w_co, v_norm_ffn_g, v_w_ff1, v_w_ff2, v_norm_final_g):
    args = dict(locals())
    d = D_MODEL
    nb, s, _ = x.shape
    c_idx = lax.axis_index("c").astype(jnp.int32).reshape(1)
    q_idx = (2 * lax.axis_index("x") + lax.axis_index("y")).astype(jnp.int32).reshape(1)
    rest = BIG[1:]

    shards = {k: args[k][0].astype(BF) for k in BIG}
    big = _full_weights({"w_in": _run_exchange(_GatherExchange([shards["w_in"]]), "all_gather_w_in")[0]})
    gather_rest = _GatherExchange([shards[k] for k in rest])

    reduce = _GradReduce(c_idx)
    loss, grad_x, gsmall, _, parts, last = _local_step(
        x.reshape(nb * s, d), mem.reshape(nb * N_MEM, d), loss_target.reshape(nb * s, d), _small_params(args), big,
        nb, s, gather_rest=gather_rest, reduce=reduce)

    shapes = [args[k].shape for k in SMALL]
    n_small = sum(math.prod(sh) for sh in shapes)
    rows = -(-(n_small + 1) // 1024) * 8
    reduced, (parts["w_in"],) = _small_all_reduce(_pack([gsmall[k] for k in SMALL] + [loss], rows), ride=last)
    halves = [_sum4(reduce.sums[k], parts[k], q_idx, "sum4_" + k) for k in BIG]
    both = _half_exchange(halves)

    out = {"grad_x": grad_x.reshape(nb, s, d)}
    for k, g2 in zip(BIG, both):
        shp = args[k].shape
        g = g2.reshape(shp[1], shp[2])
        dl, nm, nv = _adamw(g, args[k][0], args["m_" + k][0], args["v_" + k][0], "adamw_" + k)
        out["grad_" + k], out["delta_" + k], out["new_m_" + k], out["new_v_" + k] = (
            a.reshape(shp) for a in (g, dl, nm, nv))

    dl, nm, nv = _adamw(reduced, _pack([args[k] for k in SMALL], rows), _pack([args["m_" + k] for k in SMALL], rows),
                        _pack([args["v_" + k] for k in SMALL], rows), "adamw_small")
    for name, arr in (("grad_", reduced), ("delta_", dl), ("new_m_", nm), ("new_v_", nv)):
        for k, a in zip(SMALL, _unpack(arr, shapes)):
            out[name + k] = a
    out["loss"] = reduced.reshape(-1)[n_small]

    names = ["norm_mix_g", "w_in", "gm_v_norm_g", "w_spatial", "b_spatial", "head_norm_g", "w_out", "norm_cross_g",
             "norm_mem_g", "w_cq", "w_ckv", "w_co", "norm_ffn_g", "w_ff1", "w_ff2", "norm_final_g"]
    return (out["loss"], out["grad_x"], *[out["grad_" + k] for k in names], *[out["delta_" + k] for k in names],
            *[out["new_m_" + k] for k in names], *[out["new_v_" + k] for k in names])
```

```python
import functools
import math

import jax
import jax.numpy as jnp
from jax import lax
from jax.experimental import pallas as pl
from jax.experimental.pallas import tpu as pltpu

F32 = jnp.float32
BF = jnp.bfloat16

EPS = 1e-6
D_MODEL = 1024
CHUNK = 128
GM_GROUPS = 4
GM_WIDTH = 512
SB_WIDTH = 512
HEAD_LANES = 64
SB_SCALE = 0.125
SB_SKIP = -104.0
X_HEADS = 4
X_HEAD_DIM = 256
N_MEM = 256
D_FF = 4096
IN_COLS = 2560
N_CHIPS = 4
N_DEV = 8

ADAM_LR = 0.001
ADAM_B1 = 0.9
ADAM_B2 = 0.999
ADAM_EPS = 1e-08
ADAM_WD = 0.01
ADAM_STEP = 10

V7X_VMEM_BYTES = 64 * 1024 * 1024
MESH = pl.DeviceIdType.MESH
ANY = pl.BlockSpec(memory_space=pl.ANY)

GELU_C = math.sqrt(2.0 / math.pi)
GELU_A = 0.044715


def _params(vmem_mb, sem=None):
    assert vmem_mb * 1024 * 1024 <= V7X_VMEM_BYTES
    return pltpu.CompilerParams(vmem_limit_bytes=vmem_mb * 1024 * 1024, dimension_semantics=sem)


PIN_MIN_ELEMENTS = 1 << 18


def _in_hbm(*arrays):
    return tuple(pltpu.with_memory_space_constraint(a, pltpu.HBM) if a.size >= PIN_MIN_ELEMENTS else a
                 for a in arrays)


def _dot(a, b):
    return jnp.dot(a, b, preferred_element_type=F32)


def _dot_bt(a, b):
    return lax.dot_general(a, b, (((1,), (1,)), ((), ())), preferred_element_type=F32)


def _dot_at(a, b):
    return lax.dot_general(a, b, (((0,), (0,)), ((), ())), preferred_element_type=F32)


def _gelu(x):
    t = jnp.tanh(GELU_C * (x + GELU_A * x * x * x))
    return 0.5 * x * (1.0 + t)


def _gelu_and_grad(x):
    x2 = x * x
    t = jnp.tanh(GELU_C * (x + GELU_A * x2 * x))
    h = 0.5 * (1.0 + t)
    return x * h, h + 0.5 * x * (1.0 - t * t) * (GELU_C * (1.0 + 3.0 * GELU_A * x2))


def _rs(x):
    return lax.rsqrt(jnp.mean(x * x, axis=-1, keepdims=True) + EPS)


def _rms_bwd(dxn, xhat, r, g):
    dxh = dxn * g
    dx = r * (dxh - xhat * jnp.mean(dxh * xhat, axis=-1, keepdims=True))
    return dx, dxn * xhat


def _norm_matmul(x, g, w, tm, name):
    t, d = x.shape
    n = w.shape[1]
    tm = min(tm, t)

    def body(x_ref, g_ref, w_ref, out_ref, xn_ref):
        xv = x_ref[...]
        xn = (xv * _rs(xv) * g_ref[...]).astype(BF)
        xn_ref[...] = xn
        out_ref[...] = _dot(xn, w_ref[...]).astype(out_ref.dtype)

    return pl.pallas_call(
        body, name=name, grid=(t // tm,),
        in_specs=[pl.BlockSpec((tm, d), lambda i: (i, 0)), pl.BlockSpec((1, d), lambda i: (0, 0)),
                  pl.BlockSpec((d, n), lambda i: (0, 0))],
        out_specs=[pl.BlockSpec((tm, n), lambda i: (i, 0)), pl.BlockSpec((tm, d), lambda i: (i, 0))],
        out_shape=[jax.ShapeDtypeStruct((t, n), BF), jax.ShapeDtypeStruct((t, d), BF)],
        compiler_params=_params(48, ("arbitrary",)),
    )(*_in_hbm(x, g, w))


def _wgrad(a, g, tn, tk, name, square_a=False, col_shards=1):
    t, m = a.shape
    n = g.shape[1]
    tk = min(tk, t)
    tm = min(m, 1024)
    ns = n // col_shards
    assert ns % tn == 0 and m % tm == 0
    per = ns // tn
    nk = t // tk

    def body(a_ref, g_ref, o_ref):
        k = pl.program_id(2)

        @pl.when(k == 0)
        def _():
            o_ref[...] = jnp.zeros_like(o_ref)

        av = a_ref[...]
        if square_a:
            af = av.astype(F32)
            av = af * af
        o_ref[...] += _dot_at(av.astype(BF), g_ref[...].astype(BF))

    return pl.pallas_call(
        body, name=name, grid=(m // tm, n // tn, nk),
        in_specs=[pl.BlockSpec((tk, tm), lambda i, j, k: (k, i)), pl.BlockSpec((tk, tn), lambda i, j, k: (k, j))],
        out_specs=pl.BlockSpec((None, tm, tn), lambda i, j, k: (j // per, i, j % per)),
        out_shape=jax.ShapeDtypeStruct((col_shards, m, ns), F32),
        compiler_params=_params(48, ("arbitrary", "arbitrary", "arbitrary")),
    )(*_in_hbm(a, g))


def _wgrad_wide(a, g, tm, tk, name, col_shards):
    t, m = a.shape
    n = g.shape[1]
    tk = min(tk, t)
    tm = min(tm, m)
    ns = n // col_shards

    def body(a_ref, g_ref, o_ref):
        @pl.when(pl.program_id(1) == 0)
        def _():
            o_ref[...] = jnp.zeros_like(o_ref)

        a_t = a_ref[...].astype(BF).T
        for p in range(col_shards):
            o_ref[p] += _dot(a_t, g_ref[:, p * ns:(p + 1) * ns].astype(BF))

    return pl.pallas_call(
        body, name=name, grid=(m // tm, t // tk),
        in_specs=[pl.BlockSpec((tk, tm), lambda i, k: (k, i)), pl.BlockSpec((tk, n), lambda i, k: (k, 0))],
        out_specs=pl.BlockSpec((col_shards, tm, ns), lambda i, k: (0, i, 0)),
        out_shape=jax.ShapeDtypeStruct((col_shards, m, ns), F32),
        compiler_params=_params(48, ("arbitrary", "arbitrary")),
    )(*_in_hbm(a, g))


def _matmul_bt(a, w, tm, name):
    t, n = a.shape
    k = w.shape[0]
    tm = min(tm, t)

    def body(a_ref, w_ref, o_ref):
        o_ref[...] = _dot_bt(a_ref[...].astype(BF), w_ref[...]).astype(o_ref.dtype)

    return pl.pallas_call(
        body, name=name, grid=(t // tm,),
        in_specs=[pl.BlockSpec((tm, n), lambda i: (i, 0)), pl.BlockSpec((k, n), lambda i: (0, 0))],
        out_specs=pl.BlockSpec((tm, k), lambda i: (i, 0)),
        out_shape=jax.ShapeDtypeStruct((t, k), BF),
        compiler_params=_params(32, ("arbitrary",)),
    )(*_in_hbm(a, w))


def _gmlp_fwd(proj, gg, wt, bb, hg, tm):
    t = proj.shape[0]
    tm = min(tm, t)

    def body(u_ref, v_ref, gg_ref, wt_ref, bb_ref, hg_ref, out_ref):
        for cc in range(tm // CHUNK):
            rows = slice(cc * CHUNK, (cc + 1) * CHUNK)
            for g in range(GM_GROUPS):
                cols = slice(g * 128, (g + 1) * 128)
                u = _gelu(u_ref[rows, cols].astype(F32))
                gv = _gelu(v_ref[rows, cols].astype(F32))
                vn = gv * _rs(gv) * gg_ref[:, cols]
                mixed = _dot(wt_ref[g], vn.astype(BF)) + bb_ref[g]
                a = u * mixed
                out_ref[rows, cols] = (a * _rs(a) * hg_ref[:, cols]).astype(BF)

    return pl.pallas_call(
        body, name="gmlp_fwd", grid=(t // tm,),
        in_specs=[pl.BlockSpec((tm, 512), lambda i: (i, 0)), pl.BlockSpec((tm, 512), lambda i: (i, 1)),
                  pl.BlockSpec((1, 512), lambda i: (0, 0)), pl.BlockSpec((4, 128, 128), lambda i: (0, 0, 0)),
                  pl.BlockSpec((4, 128, 128), lambda i: (0, 0, 0)), pl.BlockSpec((1, 512), lambda i: (0, 0))],
        out_specs=pl.BlockSpec((tm, 512), lambda i: (i, 0)),
        out_shape=jax.ShapeDtypeStruct((t, 1024), BF),
        compiler_params=_params(32, ("arbitrary",)),
    )(*_in_hbm(proj, proj, gg, wt, bb, hg))


def _gmlp_bwd(proj, dmerged, gg, wt, wtt, bb, hg, tm, ride=None):
    t = proj.shape[0]
    tm = min(tm, t)
    nsteps = t // tm

    def body(u_ref, v_ref, dm_ref, gg_ref, wt_ref, wtt_ref, bb_ref, hg_ref,
             dp_ref, dw_ref, db_ref, dgg_ref, dhg_ref):
        i = pl.program_id(0)

        @pl.when(i == 0)
        def _():
            dw_ref[...] = jnp.zeros_like(dw_ref)
            db_ref[...] = jnp.zeros_like(db_ref)
            dgg_ref[...] = jnp.zeros_like(dgg_ref)
            dhg_ref[...] = jnp.zeros_like(dhg_ref)

        for cc in range(tm // CHUNK):
            rows = slice(cc * CHUNK, (cc + 1) * CHUNK)
            for g in range(GM_GROUPS):
                cols = slice(g * 128, (g + 1) * 128)
                up = u_ref[rows, cols].astype(F32)
                gp = v_ref[rows, cols].astype(F32)
                u, u_grad = _gelu_and_grad(up)
                gv, gv_grad = _gelu_and_grad(gp)
                rv = _rs(gv)
                gvh = gv * rv
                ggv = gg_ref[:, cols]
                vnb = (gvh * ggv).astype(BF)
                mixed = _dot(wt_ref[g], vnb) + bb_ref[g]
                a = u * mixed
                ra = _rs(a)
                ah = a * ra
                dm = dm_ref[rows, cols].astype(F32)
                dhg_ref[:, cols] += jnp.sum(dm * ah, axis=0, keepdims=True)
                dah = dm * hg_ref[:, cols]
                da = ra * (dah - ah * jnp.mean(dah * ah, axis=-1, keepdims=True))
                du = da * mixed
                dmix = da * u
                db_ref[g] += dmix
                dmb = dmix.astype(BF)
                dw_ref[g] += _dot_bt(dmb, vnb)
                dvn = _dot(wtt_ref[g], dmb)
                dgg_ref[:, cols] += jnp.sum(dvn * gvh, axis=0, keepdims=True)
                dgh = dvn * ggv
                dgv = rv * (dgh - gvh * jnp.mean(dgh * gvh, axis=-1, keepdims=True))
                dp_ref[rows, cols] = (du * u_grad).astype(BF)
                dp_ref[rows, 512 + g * 128:512 + (g + 1) * 128] = (dgv * gv_grad).astype(BF)

        @pl.when(i == nsteps - 1)
        def _():
            r = lax.broadcasted_iota(jnp.int32, (CHUNK, CHUNK), 0)
            c = lax.broadcasted_iota(jnp.int32, (CHUNK, CHUNK), 1)
            for g in range(GM_GROUPS):
                dw_ref[g] = jnp.where(c <= r, dw_ref[g], 0.0)
                db_ref[g] = jnp.broadcast_to(jnp.sum(db_ref[g], axis=-1, keepdims=True), (CHUNK, CHUNK))

    small = lambda shape: pl.BlockSpec(shape, lambda i: (0,) * len(shape))
    res, rode = _ride_call(
        body, "gmlp_bwd", (nsteps,),
        in_specs=[pl.BlockSpec((tm, 512), lambda i: (i, 0)), pl.BlockSpec((tm, 512), lambda i: (i, 1)),
                  pl.BlockSpec((tm, 512), lambda i: (i, 0)), small((1, 512)), small((4, 128, 128)),
                  small((4, 128, 128)), small((4, 128, 128)), small((1, 512))],
        out_specs=[pl.BlockSpec((tm, 1024), lambda i: (i, 0)), small((4, 128, 128)), small((4, 128, 128)),
                   small((1, 512)), small((1, 512))],
        out_shape=[jax.ShapeDtypeStruct((t, IN_COLS), BF), jax.ShapeDtypeStruct((4, 128, 128), F32),
                   jax.ShapeDtypeStruct((4, 128, 128), F32), jax.ShapeDtypeStruct((1, 512), F32),
                   jax.ShapeDtypeStruct((1, 512), F32)],
        scratch_shapes=[], operands=(proj, proj, dmerged, gg, wt, wtt, bb, hg), vmem_mb=32, ride=ride)
    return (*res, rode)


def _other_chips(x, y):
    return ((1 - x, y), (x, 1 - y), (1 - x, 1 - y))


class _GatherExchange:
    def __init__(self, shards):
        n = len(shards)
        self.n = n
        self.in_arrays = list(shards)
        self.out_shape = [jax.ShapeDtypeStruct((N_CHIPS,) + a.shape, a.dtype) for a in shards]
        self.half_rows = [a.shape[0] // 2 for a in shards]
        sems = lambda k: pltpu.SemaphoreType.DMA((k,))
        self.scratch_shapes = [pltpu.VMEM(a.shape, a.dtype) for a in shards] + [
            sems(3 * n), sems(3 * n), sems(3 * n), sems(3 * n), sems(n), sems(n)]

    def _copies(self, ins, outs, scr):
        n = self.n
        stages, (ici_send, ici_recv, d2d_send, d2d_recv, ld_sems, st_sems) = scr[:n], scr[n:]
        x, y, c = lax.axis_index("x"), lax.axis_index("y"), lax.axis_index("c")
        q = 2 * x + y
        loads = [pltpu.make_async_copy(ins[w], stages[w], ld_sems.at[w]) for w in range(n)]
        stores = [pltpu.make_async_copy(stages[w], outs[w].at[q], st_sems.at[w]) for w in range(n)]
        ici, d2d = [], []
        for w in range(n):
            half = pl.ds(c * self.half_rows[w], self.half_rows[w])
            for k, (px, py) in enumerate(_other_chips(x, y)):
                ici.append(pltpu.make_async_remote_copy(
                    src_ref=ins[w].at[half], dst_ref=outs[w].at[q, half], send_sem=ici_send.at[3 * w + k],
                    recv_sem=ici_recv.at[3 * w + k], device_id=(px, py, c), device_id_type=MESH))
                landed = outs[w].at[2 * px + py, half]
                d2d.append(pltpu.make_async_remote_copy(
                    src_ref=landed, dst_ref=landed, send_sem=d2d_send.at[3 * w + k],
                    recv_sem=d2d_recv.at[3 * w + k], device_id=(x, y, 1 - c), device_id_type=MESH))
        return loads, stores, ici, d2d

    def start(self, ins, outs, scr):
        loads, stores, ici, _ = self._copies(ins, outs, scr)
        for cp in loads + ici:
            cp.start()
        for ld, st in zip(loads, stores):
            ld.wait()
            st.start()

    def relay(self, ins, outs, scr):
        _, _, ici, d2d = self._copies(ins, outs, scr)
        for got, fwd in zip(ici, d2d):
            got.wait_recv()
            fwd.start()

    def finish(self, ins, outs, scr):
        _, stores, ici, d2d = self._copies(ins, outs, scr)
        for cp in ici:
            cp.wait_send()
        for cp in d2d + stores:
            cp.wait()


class _SiblingExchange:
    def __init__(self, slabs):
        n = len(slabs)
        self.n = n
        self.in_arrays = list(slabs)
        self.out_shape = [jax.ShapeDtypeStruct((N_CHIPS,) + a.shape[1:], a.dtype) for a in slabs]
        self.scratch_shapes = [pltpu.SemaphoreType.DMA((4 * n,)), pltpu.SemaphoreType.DMA((4 * n,))]

    def _copies(self, ins, outs, scr):
        send_sems, recv_sems = scr
        x, y, c = lax.axis_index("x"), lax.axis_index("y"), lax.axis_index("c")
        return [pltpu.make_async_remote_copy(
            src_ref=ins[w].at[2 * p + (1 - c)], dst_ref=outs[w].at[p], send_sem=send_sems.at[4 * w + p],
            recv_sem=recv_sems.at[4 * w + p], device_id=(x, y, 1 - c), device_id_type=MESH)
            for w in range(self.n) for p in range(N_CHIPS)]

    def start(self, ins, outs, scr):
        for cp in self._copies(ins, outs, scr):
            cp.start()

    def finish(self, ins, outs, scr):
        for cp in self._copies(ins, outs, scr):
            cp.wait()


class _ChipExchange:
    def __init__(self, sums):
        n = len(sums)
        self.n = n
        self.in_arrays = list(sums)
        self.out_shape = [jax.ShapeDtypeStruct(a.shape, a.dtype) for a in sums]
        self.scratch_shapes = [pltpu.SemaphoreType.DMA((3 * n,)), pltpu.SemaphoreType.DMA((3 * n,))]

    def _copies(self, ins, outs, scr):
        send_sems, recv_sems = scr
        x, y, c = lax.axis_index("x"), lax.axis_index("y"), lax.axis_index("c")
        q = 2 * x + y
        return [pltpu.make_async_remote_copy(
            src_ref=ins[w].at[2 * px + py], dst_ref=outs[w].at[q], send_sem=send_sems.at[3 * w + k],
            recv_sem=recv_sems.at[3 * w + k], device_id=(px, py, c), device_id_type=MESH)
            for w in range(self.n) for k, (px, py) in enumerate(_other_chips(x, y))]

    def start(self, ins, outs, scr):
        for cp in self._copies(ins, outs, scr):
            cp.start()

    def finish(self, ins, outs, scr):
        for cp in self._copies(ins, outs, scr):
            cp.wait()


class _NoExchange:
    in_arrays, out_shape, scratch_shapes = (), (), ()

    def start(self, ins, outs, scr):
        pass

    def finish(self, ins, outs, scr):
        pass


def _run_exchange(ex, name):
    n_in, n_out = len(ex.in_arrays), len(ex.out_shape)

    def body(*refs):
        ins, outs, scr = refs[:n_in], refs[n_in:n_in + n_out], refs[n_in + n_out:]
        ex.start(ins, outs, scr)
        if hasattr(ex, "relay"):
            ex.relay(ins, outs, scr)
        ex.finish(ins, outs, scr)

    return pl.pallas_call(
        body, name=name, in_specs=[ANY] * n_in, out_specs=[ANY] * n_out, out_shape=ex.out_shape,
        scratch_shapes=ex.scratch_shapes, compiler_params=_params(24),
    )(*ex.in_arrays)


def _ride_call(body, name, grid, in_specs, out_specs, out_shape, scratch_shapes, operands, vmem_mb, ride=None,
               aliases=None):
    ride = ride or _NoExchange()
    ni, no, ns = len(in_specs), len(out_specs), len(scratch_shapes)
    ri, ro = len(ride.in_arrays), len(ride.out_shape)
    total = math.prod(grid)

    def wrapped(*refs):
        ins, rins = refs[:ni], refs[ni:ni + ri]
        outs, routs = refs[ni + ri:ni + ri + no], refs[ni + ri + no:ni + ri + no + ro]
        scr, rscr = refs[ni + ri + no + ro:ni + ri + no + ro + ns], refs[ni + ri + no + ro + ns:]
        step = pl.program_id(0)
        for ax in range(1, len(grid)):
            step = step * grid[ax] + pl.program_id(ax)

        @pl.when(step == 0)
        def _():
            ride.start(rins, routs, rscr)

        if hasattr(ride, "relay"):
            @pl.when(step == (3 * total) // 4)
            def _():
                ride.relay(rins, routs, rscr)

        body(*ins, *outs, *scr)

        @pl.when(step == total - 1)
        def _():
            ride.finish(rins, routs, rscr)

    res = pl.pallas_call(
        wrapped, name=name, grid=grid, in_specs=list(in_specs) + [ANY] * ri, out_specs=list(out_specs) + [ANY] * ro,
        out_shape=list(out_shape) + list(ride.out_shape),
        scratch_shapes=list(scratch_shapes) + list(ride.scratch_shapes), input_output_aliases=aliases or {},
        compiler_params=_params(vmem_mb, ("arbitrary",) * len(grid)),
    )(*_in_hbm(*operands), *ride.in_arrays)
    return res[:no], res[no:]


def _neg_log_sig(z):
    n = jnp.maximum(z, 0.0) + jnp.log(1.0 + jnp.exp(-jnp.abs(z)))
    return n, z - n


def _running_sums(n, tri2):
    hi = n.astype(BF)
    lo = (n - hi.astype(F32)).astype(BF)
    return _dot(jnp.concatenate([hi, lo], axis=1), tri2)


def _head_sums(x, h0):
    s0 = jnp.sum(jnp.where(h0, x, 0.0), axis=-1, keepdims=True)
    s1 = jnp.sum(jnp.where(h0, 0.0, x), axis=-1, keepdims=True)
    return jnp.where(h0, s0, s1)


SB_BLOCKS_PER_STEP = 4


def _sb_masks(tq):
    h0 = lax.broadcasted_iota(jnp.int32, (tq, 128), 1) < HEAD_LANES
    r = lax.broadcasted_iota(jnp.int32, (2 * tq, tq), 0)
    c = lax.broadcasted_iota(jnp.int32, (2 * tq, tq), 1)
    return h0, c < jnp.where(r >= tq, r - tq, r)


def _sb_stack(x, h0):
    zero = jnp.zeros_like(x)
    return jnp.concatenate([jnp.where(h0, x, zero), jnp.where(h0, zero, x)], axis=0)


def _tri(tq, op):
    return op(lax.broadcasted_iota(jnp.int32, (tq, tq), 0), lax.broadcasted_iota(jnp.int32, (tq, tq), 1)).astype(BF)


def _sb_fwd(proj, merged, hg, nb, s, tq, ride=None):
    t = nb * s
    tq = min(tq, s)
    nq = s // tq
    per = min(SB_BLOCKS_PER_STEP, nq)
    ns = nq // per

    def body(q_ref, k_ref, v_ref, hg_ref, merged_ref, o_ref, tot_ref, mb_ref, nblk_ref, acc, cr, c_min):
        del merged_ref
        h0, causal = _sb_masks(tq)
        tri_gt = _tri(tq, lambda r, c: r > c)
        tri_gt = jnp.concatenate([tri_gt, tri_gt], axis=0)

        def query_block(i, rows):
            qst = _sb_stack(q_ref[rows, :] * SB_SCALE, h0)

            def block(j, masked, c_in):
                start = pl.multiple_of(j * tq, tq)
                kj = k_ref[pl.ds(start, tq), :]
                vj = v_ref[pl.ds(start, tq), :]
                n, l = _neg_log_sig(_dot_bt(qst, kj))
                if masked:
                    n = jnp.where(causal, n, 0.0)
                a = jnp.exp(l - (_running_sums(n, tri_gt) + c_in))
                if masked:
                    a = jnp.where(causal, a, 0.0)
                return _dot(a.astype(BF), vj), c_in + jnp.sum(n, axis=-1, keepdims=True)

            @pl.when(i == 0)
            def _():
                acc[...], c_diag = block(0, True, jnp.zeros((2 * tq, 1), F32))
                c_min[0] = jnp.min(c_diag)
                cr[...] = c_diag

            @pl.when(i > 0)
            def _():
                p_diag, c_diag = block(i, True, jnp.zeros((2 * tq, 1), F32))
                p_prev, c_prev = block(i - 1, False, c_diag)
                c_min[0] = jnp.min(c_prev)
                acc[...] = p_diag + p_prev
                cr[...] = c_prev

            def cond(carry):
                return jnp.logical_and(carry[0] < i, carry[1] < -SB_SKIP)

            def step(carry):
                p, c_new = block(i - 1 - carry[0], False, cr[...])
                acc[...] += p
                cr[...] = c_new
                return carry[0] + 1, jnp.min(c_new)

            walked, _ = lax.while_loop(cond, step, (jnp.minimum(i, 1), c_min[0]))
            return walked

        for u in range(per):
            rows = slice(u * tq, (u + 1) * tq)
            walked = query_block(pl.program_id(2) * per + u, rows)
            o = jnp.where(h0, acc[0:tq, :], acc[tq:2 * tq, :])
            o_ref[rows, :] = o
            tot_ref[rows, :] = jnp.where(h0, cr[0:tq, :], cr[tq:2 * tq, :])
            ro = lax.rsqrt(_head_sums(o * o, h0) * (1.0 / HEAD_LANES) + EPS)
            mb_ref[rows, :] = (o * ro * hg_ref[...]).astype(BF)
            nblk_ref[u * 8:(u + 1) * 8, :] = jnp.full((8, 128), walked.astype(F32))

    blk = lambda col0: pl.BlockSpec((per * tq, 128), lambda b, hp, i: (b * ns + i, col0 + hp))
    seq = lambda col0: pl.BlockSpec((s, 128), lambda b, hp, i: (b, col0 + hp))
    (o, tot, mb, nblk), rode = _ride_call(
        body, "sb_fwd", (nb, 4, ns),
        in_specs=[blk(8), seq(12), seq(16), pl.BlockSpec((1, 128), lambda b, hp, i: (0, 4 + hp)), ANY],
        out_specs=[blk(0), blk(0), blk(4),
                   pl.BlockSpec((None, None, per * 8, 128), lambda b, hp, i: (b, hp, i, 0))],
        out_shape=[jax.ShapeDtypeStruct((t, 512), F32), jax.ShapeDtypeStruct((t, 512), F32),
                   jax.ShapeDtypeStruct((t, 1024), BF), jax.ShapeDtypeStruct((nb, 4, nq * 8, 128), F32)],
        scratch_shapes=[pltpu.VMEM((2 * tq, 128), F32), pltpu.VMEM((2 * tq, 1), F32), pltpu.SMEM((1,), F32)],
        operands=(proj, proj, proj, hg, merged), vmem_mb=40, ride=ride, aliases={4: 2})
    return o, tot, mb, nblk, rode


def _sb_bwd(proj, o_sb, tot, nblk, dmerged, dproj, hg, nb, s, tq, ride=None):
    t = nb * s
    tq = min(tq, s)
    nq = s // tq
    per = min(SB_BLOCKS_PER_STEP, nq)
    ns = nq // per

    def body(q_ref, k_ref, v_ref, o_ref, tot_ref, nblk_ref, dm_ref, hg_ref, dproj_ref,
             dq_ref, dk_ref, dv_ref, dhg_ref, dk_acc, dv_acc, dq_acc, cm, cg):
        del dproj_ref
        h0, causal = _sb_masks(tq)
        tri_le = _tri(tq, lambda r, c: r <= c)
        tri_le = jnp.concatenate([tri_le, tri_le], axis=0)
        tri_lt = _tri(tq, lambda r, c: r < c)

        @pl.when(pl.program_id(2) == 0)
        def _():
            dk_acc[...] = jnp.zeros_like(dk_acc)
            dv_acc[...] = jnp.zeros_like(dv_acc)
            dhg_ref[...] = jnp.zeros_like(dhg_ref)

        def query_block(i, rows):
            qst = _sb_stack(q_ref[rows, :] * SB_SCALE, h0)
            for ref in (dq_acc, cm, cg):
                ref[...] = jnp.zeros_like(ref)

            o = o_ref[rows, :]
            ro = lax.rsqrt(_head_sums(o * o, h0) * (1.0 / HEAD_LANES) + EPS)
            oh = o * ro
            dm = dm_ref[rows, :].astype(F32)
            dhg_ref[...] += jnp.sum(dm * oh, axis=0, keepdims=True)
            doh = dm * hg_ref[...]
            dost = _sb_stack((ro * (doh - oh * (_head_sums(doh * oh, h0) * (1.0 / HEAD_LANES)))).astype(BF), h0)
            tots = jnp.concatenate([tot_ref[rows, 0:1], tot_ref[rows, HEAD_LANES:HEAD_LANES + 1]], axis=0)
            qst_t = qst.T
            dost_t = dost.T

            def block(j, masked, cm_in, cg_in):
                start = pl.multiple_of(j * tq, tq)
                kj = k_ref[pl.ds(start, tq), :]
                vj = v_ref[pl.ds(start, tq), :]
                n, l = _neg_log_sig(_dot_bt(qst, kj))
                if masked:
                    n = jnp.where(causal, n, 0.0)
                a = jnp.exp(l - (tots - cm_in - _running_sums(n, tri_le)))
                if masked:
                    a = jnp.where(causal, a, 0.0)
                gm = a * _dot_bt(dost, vj)
                pp = cg_in + _dot(gm.astype(BF), tri_lt)
                dz = gm - jnp.exp(l) * (gm + pp)
                if masked:
                    dz = jnp.where(causal, dz, 0.0)
                dzb = dz.astype(BF)
                dk_acc[:, pl.ds(start, tq)] += _dot(qst_t, dzb)
                dv_acc[:, pl.ds(start, tq)] += _dot(dost_t, a.astype(BF))
                return (_dot(dzb, kj), cm_in + jnp.sum(n, axis=-1, keepdims=True),
                        cg_in + jnp.sum(gm, axis=-1, keepdims=True))

            def step(j, carry):
                dq, cm[...], cg[...] = block(j, False, cm[...], cg[...])
                dq_acc[...] += dq
                return carry

            walked = jnp.clip(nblk_ref[pl.program_id(0), pl.program_id(1), i].astype(jnp.int32),
                              jnp.minimum(i, 1), i)
            lax.fori_loop(i - walked, i - 1, step, 0)

            @pl.when(i == 0)
            def _():
                dq_acc[...] = block(0, True, cm[...], cg[...])[0]

            @pl.when(i > 0)
            def _():
                dq_prev, cm_prev, cg_prev = block(i - 1, False, cm[...], cg[...])
                dq_acc[...] += dq_prev + block(i, True, cm_prev, cg_prev)[0]

            dq_ref[rows, :] = (jnp.where(h0, dq_acc[0:tq, :], dq_acc[tq:2 * tq, :]) * SB_SCALE).astype(BF)

        for u in range(per):
            query_block(pl.program_id(2) * per + u, slice(u * tq, (u + 1) * tq))

        @pl.when(pl.program_id(2) == ns - 1)
        def _():
            dk_ref[...] = dk_acc[...].T.astype(BF)
            dv_ref[...] = dv_acc[...].T.astype(BF)

    blk = lambda col0: pl.BlockSpec((per * tq, 128), lambda b, hp, i: (b * ns + i, col0 + hp))
    seq = lambda col0: pl.BlockSpec((s, 128), lambda b, hp, i: (b, col0 + hp))
    (dq, dk, dv, dhg), rode = _ride_call(
        body, "sb_bwd", (nb, 4, ns),
        in_specs=[blk(8), seq(12), seq(16), blk(0), blk(0), pl.BlockSpec(memory_space=pltpu.SMEM), blk(4),
                  pl.BlockSpec((1, 128), lambda b, hp, i: (0, 4 + hp)), ANY],
        out_specs=[blk(8), seq(0), seq(0), pl.BlockSpec((None, 1, 128), lambda b, hp, i: (b, 0, hp))],
        out_shape=[jax.ShapeDtypeStruct((t, IN_COLS), BF), jax.ShapeDtypeStruct((t, 512), BF),
                   jax.ShapeDtypeStruct((t, 512), BF), jax.ShapeDtypeStruct((nb, 1, 512), F32)],
        scratch_shapes=[pltpu.VMEM((128, s), F32), pltpu.VMEM((128, s), F32), pltpu.VMEM((2 * tq, 128), F32),
                        pltpu.VMEM((2 * tq, 1), F32), pltpu.VMEM((2 * tq, 1), F32)],
        operands=(proj, proj, proj, o_sb, tot, nblk.reshape(nb, 4, nq, 8, 128)[:, :, :, 0, 0], dmerged, hg, dproj),
        vmem_mb=40, ride=ride, aliases={8: 0})
    return dq, dk, dv, dhg, rode


def _place(buf, piece, col_block, name):
    t, w = piece.shape
    tm = min(t, 1024)

    def body(piece_ref, buf_ref, out_ref):
        del buf_ref
        out_ref[...] = piece_ref[...]

    return pl.pallas_call(
        body, name=name, grid=(t // tm,),
        in_specs=[pl.BlockSpec((tm, w), lambda i: (i, 0)), ANY],
        out_specs=pl.BlockSpec((tm, w), lambda i: (i, col_block)),
        out_shape=jax.ShapeDtypeStruct(buf.shape, buf.dtype), input_output_aliases={1: 0},
        compiler_params=_params(16, ("arbitrary",)),
    )(piece, buf)


def _softmax_rows(sc):
    e = jnp.exp(sc - jnp.max(sc, axis=-1, keepdims=True))
    return e / jnp.sum(e, axis=-1, keepdims=True)


def _mix_cross_fwd(x, merged, w_out, gc, w_cq, kv, w_co, s, tm):
    t, d = x.shape
    tm = min(tm, s)
    per = s // tm
    inv = 1.0 / math.sqrt(X_HEAD_DIM)

    def body(x_ref, m_ref, wo_ref, gc_ref, wq_ref, kv_ref, wc_ref, h1_ref, h2_ref, hn_ref, qc_ref, oc_ref):
        h1 = x_ref[...] + _dot(m_ref[...], wo_ref[...])
        h1_ref[...] = h1
        hn = (h1 * _rs(h1) * gc_ref[...]).astype(BF)
        hn_ref[...] = hn
        qc = _dot(hn, wq_ref[...]).astype(BF)
        qc_ref[...] = qc
        for h in range(X_HEADS):
            cols = slice(h * X_HEAD_DIM, (h + 1) * X_HEAD_DIM)
            kh = kv_ref[:, h * X_HEAD_DIM:(h + 1) * X_HEAD_DIM]
            vh = kv_ref[:, d + h * X_HEAD_DIM:d + (h + 1) * X_HEAD_DIM]
            p = _softmax_rows(_dot_bt(qc[:, cols], kh) * inv)
            oc_ref[:, cols] = _dot(p.astype(BF), vh).astype(BF)
        h2_ref[...] = h1 + _dot(oc_ref[...], wc_ref[...])

    row = lambda width: pl.BlockSpec((tm, width), lambda i: (i, 0))
    full = lambda a, b: pl.BlockSpec((a, b), lambda i: (0, 0))
    return pl.pallas_call(
        body, name="mix_cross_fwd", grid=(t // tm,),
        in_specs=[row(d), row(d), full(d, d), full(1, d), full(d, d),
                  pl.BlockSpec((N_MEM, 2 * d), lambda i: (i // per, 0)), full(d, d)],
        out_specs=[row(d), row(d), row(d), row(d), row(d)],
        out_shape=[jax.ShapeDtypeStruct((t, d), F32), jax.ShapeDtypeStruct((t, d), F32),
                   jax.ShapeDtypeStruct((t, d), BF), jax.ShapeDtypeStruct((t, d), BF),
                   jax.ShapeDtypeStruct((t, d), BF)],
        compiler_params=_params(48, ("arbitrary",)),
    )(*_in_hbm(x, merged, w_out, gc, w_cq, kv, w_co))


def _cross_bwd(dh2, h1, qc, gc, w_cq, kv, w_co, s, tm):
    t, d = dh2.shape
    tm = min(tm, s)
    per = s // tm
    nb = t // s
    inv = 1.0 / math.sqrt(X_HEAD_DIM)

    def body(dh2_ref, h1_ref, qc_ref, gc_ref, wq_ref, kv_ref, wc_ref, dh1_ref, dqc_ref, dkv_ref, dgc_ref):
        i = pl.program_id(0)

        @pl.when(i == 0)
        def _():
            dgc_ref[...] = jnp.zeros_like(dgc_ref)

        @pl.when(i % per == 0)
        def _():
            dkv_ref[...] = jnp.zeros_like(dkv_ref)

        dh2 = dh2_ref[...]
        h1 = h1_ref[...]
        r = _rs(h1)
        h1h = h1 * r
        gcv = gc_ref[...]
        qc = qc_ref[...]
        do = _dot_bt(dh2.astype(BF), wc_ref[...]).astype(BF)
        for h in range(X_HEADS):
            cols = slice(h * X_HEAD_DIM, (h + 1) * X_HEAD_DIM)
            vcols = slice(d + h * X_HEAD_DIM, d + (h + 1) * X_HEAD_DIM)
            kh = kv_ref[:, cols]
            vh = kv_ref[:, vcols]
            p = _softmax_rows(_dot_bt(qc[:, cols], kh) * inv)
            dp = _dot_bt(do[:, cols], vh)
            ds = (p * (dp - jnp.sum(dp * p, axis=-1, keepdims=True)) * inv).astype(BF)
            dqc_ref[:, cols] = _dot(ds, kh).astype(BF)
            dkv_ref[:, cols] += _dot_at(ds, qc[:, cols])
            dkv_ref[:, vcols] += _dot_at(p.astype(BF), do[:, cols])
        dhn = _dot_bt(dqc_ref[...], wq_ref[...])
        dx, dg = _rms_bwd(dhn, h1h, r, gcv)
        dh1_ref[...] = dh2 + dx
        dgc_ref[...] += jnp.sum(dg, axis=0, keepdims=True)

    row = lambda width: pl.BlockSpec((tm, width), lambda i: (i, 0))
    full = lambda a, b: pl.BlockSpec((a, b), lambda i: (0, 0))
    kvspec = pl.BlockSpec((N_MEM, 2 * d), lambda i: (i // per, 0))
    return pl.pallas_call(
        body, name="cross_bwd", grid=(t // tm,),
        in_specs=[row(d), row(d), row(d), full(1, d), full(d, d), kvspec, full(d, d)],
        out_specs=[row(d), row(d), kvspec, full(1, d)],
        out_shape=[jax.ShapeDtypeStruct((t, d), F32), jax.ShapeDtypeStruct((t, d), BF),
                   jax.ShapeDtypeStruct((nb * N_MEM, 2 * d), F32), jax.ShapeDtypeStruct((1, d), F32)],
        compiler_params=_params(48, ("arbitrary",)),
    )(*_in_hbm(dh2, h1, qc, gc, w_cq, kv, w_co))


def _mem_bwd(mem, gm, dkv, w_ckv, tm):
    t, d = mem.shape
    tm = min(tm, t)

    def body(mem_ref, dkv_ref, w_ref, dg_ref):
        @pl.when(pl.program_id(0) == 0)
        def _():
            dg_ref[...] = jnp.zeros_like(dg_ref)

        mv = mem_ref[...]
        dmn = _dot_bt(dkv_ref[...].astype(BF), w_ref[...])
        dg_ref[...] += jnp.sum(dmn * (mv * _rs(mv)), axis=0, keepdims=True)

    del gm
    return pl.pallas_call(
        body, name="mem_bwd", grid=(t // tm,),
        in_specs=[pl.BlockSpec((tm, d), lambda i: (i, 0)), pl.BlockSpec((tm, 2 * d), lambda i: (i, 0)),
                  pl.BlockSpec((d, 2 * d), lambda i: (0, 0))],
        out_specs=pl.BlockSpec((1, d), lambda i: (0, 0)),
        out_shape=jax.ShapeDtypeStruct((1, d), F32),
        compiler_params=_params(32, ("arbitrary",)),
    )(mem, dkv, w_ckv)


def _ffn_loss_fwd(h2, gf, w1, w2, gl, target, tm):
    t, d = h2.shape
    tm = min(tm, t)

    def body(h2_ref, gf_ref, w1_ref, w2_ref, gl_ref, tg_ref, hn_ref, f_ref, dh3_ref, dgl_ref, loss_ref):
        @pl.when(pl.program_id(0) == 0)
        def _():
            dgl_ref[...] = jnp.zeros_like(dgl_ref)
            loss_ref[...] = jnp.zeros_like(loss_ref)

        h2 = h2_ref[...]
        hn = (h2 * _rs(h2) * gf_ref[...]).astype(BF)
        hn_ref[...] = hn
        h3 = h2
        for c in range(4):
            f = jnp.maximum(_dot(hn, w1_ref[c]), 0.0)
            f_ref[:, c * 1024:(c + 1) * 1024] = f.astype(BF)
            h3 = h3 + _dot((f * f).astype(BF), w2_ref[c])
        r3 = _rs(h3)
        yh = h3 * r3
        glv = gl_ref[...]
        e = yh * glv - tg_ref[...]
        loss_ref[...] += 0.5 * jnp.sum(jnp.sum(e * e, axis=-1, keepdims=True) * (1.0 / d), axis=0, keepdims=True)
        dy = e * (1.0 / d)
        dx, dg = _rms_bwd(dy, yh, r3, glv)
        dh3_ref[...] = dx
        dgl_ref[...] += jnp.sum(dg, axis=0, keepdims=True)

    row = lambda width: pl.BlockSpec((tm, width), lambda i: (i, 0))
    return pl.pallas_call(
        body, name="ffn_loss_fwd", grid=(t // tm,),
        in_specs=[row(d), pl.BlockSpec((1, d), lambda i: (0, 0)), pl.BlockSpec((4, d, 1024), lambda i: (0, 0, 0), pipeline_mode=pl.Buffered(1)),
                  pl.BlockSpec((4, 1024, d), lambda i: (0, 0, 0), pipeline_mode=pl.Buffered(1)),
                  pl.BlockSpec((1, d), lambda i: (0, 0)), row(d)],
        out_specs=[row(d), row(D_FF), row(d), pl.BlockSpec((1, d), lambda i: (0, 0)),
                   pl.BlockSpec((1, 1), lambda i: (0, 0))],
        out_shape=[jax.ShapeDtypeStruct((t, d), BF), jax.ShapeDtypeStruct((t, D_FF), BF),
                   jax.ShapeDtypeStruct((t, d), F32), jax.ShapeDtypeStruct((1, d), F32),
                   jax.ShapeDtypeStruct((1, 1), F32)],
        compiler_params=_params(56, ("arbitrary",)),
    )(*_in_hbm(h2, gf, w1, w2, gl, target))


def _ffn_bwd(dh3, f, h2, gf, w1, w2, tm):
    t, d = h2.shape
    tm = min(tm, t)

    def body(dh3_ref, f_ref, h2_ref, gf_ref, w1_ref, w2_ref, dh2_ref, dpre_ref, dgf_ref):
        @pl.when(pl.program_id(0) == 0)
        def _():
            dgf_ref[...] = jnp.zeros_like(dgf_ref)

        dh3 = dh3_ref[...]
        dh3b = dh3.astype(BF)
        dhn = jnp.zeros((tm, d), F32)
        for c in range(4):
            cols = slice(c * 1024, (c + 1) * 1024)
            dpre = (_dot_bt(dh3b, w2_ref[c]) * (2.0 * f_ref[:, cols].astype(F32))).astype(BF)
            dpre_ref[:, cols] = dpre
            dhn = dhn + _dot_bt(dpre, w1_ref[c])
        h2 = h2_ref[...]
        r = _rs(h2)
        dx, dg = _rms_bwd(dhn, h2 * r, r, gf_ref[...])
        dh2_ref[...] = dh3 + dx
        dgf_ref[...] += jnp.sum(dg, axis=0, keepdims=True)

    row = lambda width: pl.BlockSpec((tm, width), lambda i: (i, 0))
    return pl.pallas_call(
        body, name="ffn_bwd", grid=(t // tm,),
        in_specs=[row(d), row(D_FF), row(d), pl.BlockSpec((1, d), lambda i: (0, 0)),
                  pl.BlockSpec((4, d, 1024), lambda i: (0, 0, 0), pipeline_mode=pl.Buffered(1)),
                  pl.BlockSpec((4, 1024, d), lambda i: (0, 0, 0), pipeline_mode=pl.Buffered(1))],
        out_specs=[row(d), row(D_FF), pl.BlockSpec((1, d), lambda i: (0, 0))],
        out_shape=[jax.ShapeDtypeStruct((t, d), F32), jax.ShapeDtypeStruct((t, D_FF), BF),
                   jax.ShapeDtypeStruct((1, d), F32)],
        compiler_params=_params(56, ("arbitrary",)),
    )(*_in_hbm(dh3, f, h2, gf, w1, w2))


def _in_bwd(dproj, dh1, x, g, w_in, tm, ride=None):
    t, d = x.shape
    n = w_in.shape[1]
    tm = min(tm, t)

    def body(dp_ref, dh1_ref, x_ref, g_ref, w_ref, dx_ref, dg_ref):
        @pl.when(pl.program_id(0) == 0)
        def _():
            dg_ref[...] = jnp.zeros_like(dg_ref)

        dxn = _dot_bt(dp_ref[...], w_ref[...])
        xv = x_ref[...]
        r = _rs(xv)
        dx, dg = _rms_bwd(dxn, xv * r, r, g_ref[...])
        dx_ref[...] = dh1_ref[...] + dx
        dg_ref[...] += jnp.sum(dg, axis=0, keepdims=True)

    row = lambda width: pl.BlockSpec((tm, width), lambda i: (i, 0))
    (dx, dg), rode = _ride_call(
        body, "in_bwd", (t // tm,),
        in_specs=[row(n), row(d), row(d), pl.BlockSpec((1, d), lambda i: (0, 0)),
                  pl.BlockSpec((d, n), lambda i: (0, 0))],
        out_specs=[row(d), pl.BlockSpec((1, d), lambda i: (0, 0))],
        out_shape=[jax.ShapeDtypeStruct((t, d), F32), jax.ShapeDtypeStruct((1, d), F32)],
        scratch_shapes=[], operands=(dproj, dh1, x, g, w_in), vmem_mb=48, ride=ride)
    return dx, dg, rode


class _GradReduce:
    def __init__(self, c_idx):
        self.c_idx = c_idx
        self.sums = {}

    def sibling(self, slabs):
        return _SiblingExchange(slabs)

    def chip(self, names, slabs, recv):
        for k, a, r in zip(names, slabs, recv):
            self.sums[k] = _chip_sum(a, r, self.c_idx, "chip_sum_" + k)
        return _ChipExchange([self.sums[k] for k in names])


def _full_weights(gathered):
    d = D_MODEL
    out = {}
    for k, a in gathered.items():
        if k in ("w_in", "w_ckv", "w_ff1"):
            out[k] = a.transpose(1, 0, 2).reshape(d, -1)
        else:
            out[k] = a.reshape(-1, d)
    return out


def _slabs(a):
    return a.reshape(N_DEV, -1, a.shape[-1])


def _local_step(x, mem, target, small, big, nb, s, tq=256, gather_rest=None, reduce=None):
    d = D_MODEL
    g_mix, g_v, w_sp, b_sp, g_head, g_cross, g_mem, g_ffn, g_fin = (
        small[k] for k in ("norm_mix_g", "gm_v_norm_g", "w_spatial", "b_spatial", "head_norm_g", "norm_cross_g",
                           "norm_mem_g", "norm_ffn_g", "norm_final_g"))
    tri = jnp.tril(jnp.ones((CHUNK, CHUNK), dtype=bool))
    w_sp_m = jnp.where(tri[None], w_sp, 0.0)
    wt = w_sp_m.astype(BF)
    wtt = jnp.swapaxes(w_sp_m, 1, 2).astype(BF)
    bb = jnp.broadcast_to(b_sp[:, :, None], (GM_GROUPS, CHUNK, CHUNK))
    hg_a = g_head[:, :GM_WIDTH]

    proj, xn = _norm_matmul(x, g_mix, big["w_in"], 512, "in_proj")
    merged = _gmlp_fwd(proj, g_v, wt, bb, hg_a, 512)
    o_sb, tot, merged, nblk, gathered = _sb_fwd(proj, merged, g_head, nb, s, tq, ride=gather_rest)
    if gather_rest is not None:
        big = dict(big, **_full_weights(dict(zip(BIG[1:], gathered))))
    w1c = big["w_ff1"].reshape(d, 4, 1024).transpose(1, 0, 2)
    w2c = big["w_ff2"].reshape(4, 1024, d)
    kv, memn = _norm_matmul(mem, g_mem, big["w_ckv"], 512, "mem_proj")
    h1, h2, hn, qc, oc = _mix_cross_fwd(x, merged, big["w_out"], g_cross, big["w_cq"], kv, big["w_co"], s, 512)
    hn2, f, dh3, d_fin, loss = _ffn_loss_fwd(h2, g_ffn, w1c, w2c, g_fin, target, 512)

    gbig = {}
    dh2, dpre, d_ffn = _ffn_bwd(dh3, f, h2, g_ffn, w1c, w2c, 512)
    gbig["w_ff2"] = _slabs(_wgrad(f, dh3, 1024, 1024, "wgrad_ff2", square_a=True))
    gbig["w_ff1"] = _slabs(_wgrad_wide(hn2, dpre, 512, 1024, "wgrad_ff1", col_shards=4))
    dh1, dqc, dkv, d_cross = _cross_bwd(dh2, h1, qc, g_cross, big["w_cq"], kv, big["w_co"], s, 512)
    gbig["w_co"] = _slabs(_wgrad(oc, dh2, 1024, 1024, "wgrad_co"))
    gbig["w_cq"] = _slabs(_wgrad(hn, dqc, 1024, 1024, "wgrad_cq"))
    gbig["w_ckv"] = _slabs(_wgrad(memn, dkv, 512, 1024, "wgrad_ckv", col_shards=4))
    d_mem = _mem_bwd(mem, g_mem, dkv, big["w_ckv"], 512)
    dmerged = _matmul_bt(dh1, big["w_out"], 512, "out_bwd")
    gbig["w_out"] = _slabs(_wgrad(merged, dh1, 1024, 1024, "wgrad_out"))
    rest = BIG[1:]
    ride = reduce.sibling([gbig[k] for k in rest]) if reduce else None
    dproj, d_wsp, d_bb, d_gv, d_hga, recv = _gmlp_bwd(proj, dmerged, g_v, wt, wtt, bb, hg_a, 512, ride=ride)
    ride = reduce.chip(rest, [gbig[k] for k in rest], recv) if reduce else None
    dproj, dk, dv, d_hgb, parts_rest = _sb_bwd(proj, o_sb, tot, nblk, dmerged, dproj, g_head, nb, s, tq, ride=ride)
    dproj = _place(_place(dproj, dk, 3, "place_dk"), dv, 4, "place_dv")
    gbig["w_in"] = _slabs(_wgrad_wide(xn, dproj, 512, 1024, "wgrad_in", col_shards=4))
    last = None
    if reduce:
        recv = _run_exchange(reduce.sibling([gbig["w_in"]]), "grad_sibling_exchange_w_in")
        last = reduce.chip(["w_in"], [gbig["w_in"]], recv)
    grad_x, d_mix, _ = _in_bwd(dproj, dh1, x, g_mix, big["w_in"], 512)
    parts = dict(zip(rest, parts_rest))

    gsmall = {
        "norm_mix_g": d_mix, "gm_v_norm_g": d_gv, "w_spatial": d_wsp, "b_spatial": d_bb[:, :, 0],
        "head_norm_g": jnp.concatenate([d_hga, jnp.sum(d_hgb, axis=0)], axis=1), "norm_cross_g": d_cross,
        "norm_mem_g": d_mem, "norm_ffn_g": d_ffn, "norm_final_g": d_fin,
    }
    return loss, grad_x, gsmall, gbig, parts, last


BIG = ("w_in", "w_out", "w_cq", "w_ckv", "w_co", "w_ff1", "w_ff2")
SMALL = ("norm_mix_g", "gm_v_norm_g", "w_spatial", "b_spatial", "head_norm_g", "norm_cross_g", "norm_mem_g",
         "norm_ffn_g", "norm_final_g")


def _local_copies_start(srcs, stages, sems):
    loads = [pltpu.make_async_copy(src, stage, sems.at[w]) for w, (src, stage) in enumerate(zip(srcs, stages))]
    for ld in loads:
        ld.start()
    return loads


def _local_copies_finish(loads, stages, dsts, sems):
    stores = []
    for w, (ld, stage, dst) in enumerate(zip(loads, stages, dsts)):
        ld.wait()
        st = pltpu.make_async_copy(stage, dst, sems.at[w])
        st.start()
        stores.append(st)
    for st in stores:
        st.wait()


def _chip_sum(slabs, recv, c_idx, name):
    _, r, cw = slabs.shape
    tr = min(r, 256)

    def body(c_ref, a_ref, b_ref, o_ref):
        del c_ref
        o_ref[...] = (a_ref[...] + b_ref[...]).astype(BF)

    return pl.pallas_call(
        body, name=name,
        grid_spec=pltpu.PrefetchScalarGridSpec(
            num_scalar_prefetch=1, grid=(N_CHIPS, r // tr),
            in_specs=[pl.BlockSpec((None, tr, cw), lambda p, i, c_ref: (2 * p + c_ref[0], i, 0)),
                      pl.BlockSpec((None, tr, cw), lambda p, i, c_ref: (p, i, 0))],
            out_specs=pl.BlockSpec((None, tr, cw), lambda p, i, c_ref: (p, i, 0))),
        out_shape=jax.ShapeDtypeStruct((N_CHIPS, r, cw), BF),
        compiler_params=_params(32, ("arbitrary", "arbitrary")),
    )(c_idx, *_in_hbm(slabs, recv))


def _sum4(sums, parts, q_idx, name):
    _, r, cw = parts.shape
    tr = min(r, 256)

    def body(q_ref, own_ref, a_ref, b_ref, c_ref, o_ref):
        del q_ref
        o_ref[...] = ((own_ref[...].astype(F32) + a_ref[...].astype(F32)) + b_ref[...].astype(F32)) + c_ref[
            ...].astype(F32)

    spec = lambda k: pl.BlockSpec((None, tr, cw), lambda i, q_ref: ((q_ref[0] + k) % N_CHIPS, i, 0))
    return pl.pallas_call(
        body, name=name,
        grid_spec=pltpu.PrefetchScalarGridSpec(
            num_scalar_prefetch=1, grid=(r // tr,), in_specs=[spec(0), spec(1), spec(2), spec(3)],
            out_specs=pl.BlockSpec((tr, cw), lambda i, q_ref: (i, 0))),
        out_shape=jax.ShapeDtypeStruct((r, cw), F32),
        compiler_params=_params(32, ("arbitrary",)),
    )(q_idx, *_in_hbm(sums, parts, parts, parts))


def _half_exchange(halves):
    n = len(halves)

    def body(*refs):
        ins, outs, stages = refs[:n], refs[n:2 * n], refs[2 * n:3 * n]
        send_sems, recv_sems, ld_sems, st_sems = refs[3 * n:]
        x, y, c = lax.axis_index("x"), lax.axis_index("y"), lax.axis_index("c")
        loads = _local_copies_start(ins, stages, ld_sems)
        copies = []
        for w in range(n):
            cp = pltpu.make_async_remote_copy(
                src_ref=ins[w], dst_ref=outs[w].at[c], send_sem=send_sems.at[w], recv_sem=recv_sems.at[w],
                device_id=(x, y, 1 - c), device_id_type=MESH)
            cp.start()
            copies.append(cp)
        _local_copies_finish(loads, stages, [outs[w].at[c] for w in range(n)], st_sems)
        for cp in copies:
            cp.wait()

    return pl.pallas_call(
        body, name="grad_half_exchange",
        in_specs=[ANY] * n, out_specs=[ANY] * n,
        out_shape=[jax.ShapeDtypeStruct((2,) + a.shape, a.dtype) for a in halves],
        scratch_shapes=[pltpu.VMEM(a.shape, a.dtype) for a in halves] + [
            pltpu.SemaphoreType.DMA((n,)), pltpu.SemaphoreType.DMA((n,)),
            pltpu.SemaphoreType.DMA((n,)), pltpu.SemaphoreType.DMA((n,))],
        compiler_params=_params(24),
    )(*halves)


def _small_all_reduce(packed, ride=None):
    rows = packed.shape[0]
    ride = ride or _NoExchange()
    ri, ro = len(ride.in_arrays), len(ride.out_shape)

    def body(*refs):
        in_ref, rins, out_ref, routs = refs[0], refs[1:1 + ri], refs[1 + ri], refs[2 + ri:2 + ri + ro]
        pair, chip_sum, chips, d2d_send, d2d_recv, ici_send, ici_recv = refs[2 + ri + ro:9 + ri + ro]
        rscr = refs[9 + ri + ro:]
        ride.start(rins, routs, rscr)
        x, y, c = lax.axis_index("x"), lax.axis_index("y"), lax.axis_index("c")
        q = 2 * x + y
        pair[c] = in_ref[...]
        swap = pltpu.make_async_remote_copy(
            src_ref=in_ref, dst_ref=pair.at[c], send_sem=d2d_send, recv_sem=d2d_recv,
            device_id=(x, y, 1 - c), device_id_type=MESH)
        swap.start()
        swap.wait()
        both = pair[0] + pair[1]
        chip_sum[...] = both
        chips[q] = both
        copies = [pltpu.make_async_remote_copy(
            src_ref=chip_sum, dst_ref=chips.at[q], send_sem=ici_send.at[k], recv_sem=ici_recv.at[k],
            device_id=(px, py, c), device_id_type=MESH) for k, (px, py) in enumerate(_other_chips(x, y))]
        for cp in copies:
            cp.start()
        for cp in copies:
            cp.wait()
        out_ref[...] = ((chips[0] + chips[1]) + chips[2]) + chips[3]
        ride.finish(rins, routs, rscr)

    vmem = pl.BlockSpec(memory_space=pltpu.VMEM)
    res = pl.pallas_call(
        body, name="small_all_reduce",
        in_specs=[vmem] + [ANY] * ri, out_specs=[vmem] + [ANY] * ro,
        out_shape=[jax.ShapeDtypeStruct(packed.shape, F32)] + list(ride.out_shape),
        scratch_shapes=[pltpu.VMEM((2, rows, 128), F32), pltpu.VMEM((rows, 128), F32),
                        pltpu.VMEM((N_CHIPS, rows, 128), F32), pltpu.SemaphoreType.DMA, pltpu.SemaphoreType.DMA,
                        pltpu.SemaphoreType.DMA((3,)), pltpu.SemaphoreType.DMA((3,))] + list(ride.scratch_shapes),
        compiler_params=_params(16),
    )(packed, *ride.in_arrays)
    return res[0], res[1:]


def _adamw(g, w, m, v, name):
    r, cw = g.shape
    tr = 256 if r % 256 == 0 else r

    def body(g_ref, w_ref, m_ref, v_ref, d_ref, nm_ref, nv_ref):
        gv = g_ref[...]
        nm = ADAM_B1 * m_ref[...] + (1.0 - ADAM_B1) * gv
        nv = ADAM_B2 * v_ref[...] + (1.0 - ADAM_B2) * (gv * gv)
        m_hat = nm / (1.0 - ADAM_B1 ** ADAM_STEP)
        v_hat = nv / (1.0 - ADAM_B2 ** ADAM_STEP)
        d_ref[...] = -ADAM_LR * (m_hat / (jnp.sqrt(v_hat) + ADAM_EPS) + ADAM_WD * w_ref[...])
        nm_ref[...] = nm
        nv_ref[...] = nv

    spec = pl.BlockSpec((tr, cw), lambda i: (i, 0))
    return pl.pallas_call(
        body, name=name, grid=(r // tr,),
        in_specs=[spec] * 4, out_specs=[spec] * 3,
        out_shape=[jax.ShapeDtypeStruct((r, cw), F32)] * 3,
        compiler_params=_params(32, ("arbitrary",)),
    )(*_in_hbm(g, w, m, v))


def _small_params(args):
    small = {k: args[k].reshape(1, -1) for k in SMALL}
    small["w_spatial"] = args["w_spatial"][0]
    small["b_spatial"] = args["b_spatial"][0]
    return small


def _pack(parts, rows):
    flat = jnp.concatenate([p.reshape(-1).astype(F32) for p in parts])
    return jnp.pad(flat, (0, rows * 128 - flat.shape[0])).reshape(rows, 128)


def _unpack(packed, shapes):
    flat = packed.reshape(-1)
    out, off = [], 0
    for shp in shapes:
        size = math.prod(shp)
        out.append(flat[off:off + size].reshape(shp))
        off += size
    return out


def kernel(x, mem, norm_mix_g, w_in, gm_v_norm_g, w_spatial, b_spatial, head_norm_g, w_out, norm_cross_g, norm_mem_g, w_cq, w_ckv, w_co, norm_ffn_g, w_ff1, w_ff2, norm_final_g, loss_target, m_norm_mix_g, m_w_in, m_gm_v_norm_g, m_w_spatial, m_b_spatial, m_head_norm_g, m_w_out, m_norm_cross_g, m_norm_mem_g, m_w_cq, m_w_ckv, m_w_co, m_norm_ffn_g, m_w_ff1, m_w_ff2, m_norm_final_g, v_norm_mix_g, v_w_in, v_gm_v_norm_g, v_w_spatial, v_b_spatial, v_head_norm_g, v_w_out, v_norm_cross_g, v_norm_mem_g, v_w_cq, v_w_ckv, v_w_co, v_norm_ffn_g, v_w_ff1, v_w_ff2, v_norm_final_g):
    args = dict(locals())
    d = D_MODEL
    nb, s, _ = x.shape
    c_idx = lax.axis_index("c").astype(jnp.int32).reshape(1)
    q_idx = (2 * lax.axis_index("x") + lax.axis_index("y")).astype(jnp.int32).reshape(1)
    rest = BIG[1:]

    shards = {k: args[k][0].astype(BF) for k in BIG}
    big = _full_weights({"w_in": _run_exchange(_GatherExchange([shards["w_in"]]), "all_gather_w_in")[0]})
    gather_rest = _GatherExchange([shards[k] for k in rest])

    reduce = _GradReduce(c_idx)
    loss, grad_x, gsmall, _, parts, last = _local_step(
        x.reshape(nb * s, d), mem.reshape(nb * N_MEM, d), loss_target.reshape(nb * s, d), _small_params(args), big,
        nb, s, gather_rest=gather_rest, reduce=reduce)

    shapes = [args[k].shape for k in SMALL]
    n_small = sum(math.prod(sh) for sh in shapes)
    rows = -(-(n_small + 1) // 1024) * 8
    reduced, (parts["w_in"],) = _small_all_reduce(_pack([gsmall[k] for k in SMALL] + [loss], rows), ride=last)
    halves = [_sum4(reduce.sums[k], parts[k], q_idx, "sum4_" + k) for k in BIG]
    both = _half_exchange(halves)

    out = {"grad_x": grad_x.reshape(nb, s, d)}
    for k, g2 in zip(BIG, both):
        shp = args[k].shape
        g = g2.reshape(shp[1], shp[2])
        dl, nm, nv = _adamw(g, args[k][0], args["m_" + k][0], args["v_" + k][0], "adamw_" + k)
        out["grad_" + k], out["delta_" + k], out["new_m_" + k], out["new_v_" + k] = (
            a.reshape(shp) for a in (g, dl, nm, nv))

    dl, nm, nv = _adamw(reduced, _pack([args[k] for k in SMALL], rows), _pack([args["m_" + k] for k in SMALL], rows),
                        _pack([args["v_" + k] for k in SMALL], rows), "adamw_small")
    for name, arr in (("grad_", reduced), ("delta_", dl), ("new_m_", nm), ("new_v_", nv)):
        for k, a in zip(SMALL, _unpack(arr, shapes)):
            out[name + k] = a
    out["loss"] = reduced.reshape(-1)[n_small]

    names = ["norm_mix_g", "w_in", "gm_v_norm_g", "w_spatial", "b_spatial", "head_norm_g", "w_out", "norm_cross_g",
             "norm_mem_g", "w_cq", "w_ckv", "w_co", "norm_ffn_g", "w_ff1", "w_ff2", "norm_final_g"]
    return (out["loss"], out["grad_x"], *[out["grad_" + k] for k in names], *[out["delta_" + k] for k in names],
            *[out["new_m_" + k] for k in names], *[out["new_v_" + k] for k in names])
```

```python
import functools
import math

import jax
import jax.numpy as jnp
from jax import lax
from jax.experimental import pallas as pl
from jax.experimental.pallas import tpu as pltpu

F32 = jnp.float32
BF = jnp.bfloat16

EPS = 1e-6
D_MODEL = 1024
CHUNK = 128
GM_GROUPS = 4
GM_WIDTH = 512
SB_WIDTH = 512
HEAD_LANES = 64
SB_SCALE = 0.125
SB_SKIP = -104.0
X_HEADS = 4
X_HEAD_DIM = 256
N_MEM = 256
D_FF = 4096
IN_COLS = 2560
N_CHIPS = 4
N_DEV = 8

ADAM_LR = 0.001
ADAM_B1 = 0.9
ADAM_B2 = 0.999
ADAM_EPS = 1e-08
ADAM_WD = 0.01
ADAM_STEP = 10

V7X_VMEM_BYTES = 64 * 1024 * 1024
MESH = pl.DeviceIdType.MESH
ANY = pl.BlockSpec(memory_space=pl.ANY)

GELU_C = math.sqrt(2.0 / math.pi)
GELU_A = 0.044715


def _params(vmem_mb, sem=None):
    assert vmem_mb * 1024 * 1024 <= V7X_VMEM_BYTES
    return pltpu.CompilerParams(vmem_limit_bytes=vmem_mb * 1024 * 1024, dimension_semantics=sem)


PIN_MIN_ELEMENTS = 1 << 18


def _in_hbm(*arrays):
    return tuple(pltpu.with_memory_space_constraint(a, pltpu.HBM) if a.size >= PIN_MIN_ELEMENTS else a
                 for a in arrays)


def _dot(a, b):
    return jnp.dot(a, b, preferred_element_type=F32)


def _dot_bt(a, b):
    return lax.dot_general(a, b, (((1,), (1,)), ((), ())), preferred_element_type=F32)


def _dot_at(a, b):
    return lax.dot_general(a, b, (((0,), (0,)), ((), ())), preferred_element_type=F32)


def _gelu(x):
    t = jnp.tanh(GELU_C * (x + GELU_A * x * x * x))
    return 0.5 * x * (1.0 + t)


def _gelu_and_grad(x):
    x2 = x * x
    t = jnp.tanh(GELU_C * (x + GELU_A * x2 * x))
    h = 0.5 * (1.0 + t)
    return x * h, h + 0.5 * x * (1.0 - t * t) * (GELU_C * (1.0 + 3.0 * GELU_A * x2))


def _rs(x):
    return lax.rsqrt(jnp.mean(x * x, axis=-1, keepdims=True) + EPS)


def _rms_bwd(dxn, xhat, r, g):
    dxh = dxn * g
    dx = r * (dxh - xhat * jnp.mean(dxh * xhat, axis=-1, keepdims=True))
    return dx, dxn * xhat


def _norm_matmul(x, g, w, tm, name):
    t, d = x.shape
    n = w.shape[1]
    tm = min(tm, t)

    def body(x_ref, g_ref, w_ref, out_ref, xn_ref):
        xv = x_ref[...]
        xn = (xv * _rs(xv) * g_ref[...]).astype(BF)
        xn_ref[...] = xn
        out_ref[...] = _dot(xn, w_ref[...]).astype(out_ref.dtype)

    return pl.pallas_call(
        body, name=name, grid=(t // tm,),
        in_specs=[pl.BlockSpec((tm, d), lambda i: (i, 0)), pl.BlockSpec((1, d), lambda i: (0, 0)),
                  pl.BlockSpec((d, n), lambda i: (0, 0))],
        out_specs=[pl.BlockSpec((tm, n), lambda i: (i, 0)), pl.BlockSpec((tm, d), lambda i: (i, 0))],
        out_shape=[jax.ShapeDtypeStruct((t, n), BF), jax.ShapeDtypeStruct((t, d), BF)],
        compiler_params=_params(48, ("arbitrary",)),
    )(*_in_hbm(x, g, w))


def _wgrad(a, g, tn, tk, name, square_a=False, col_shards=1):
    t, m = a.shape
    n = g.shape[1]
    tk = min(tk, t)
    tm = min(m, 1024)
    ns = n // col_shards
    assert ns % tn == 0 and m % tm == 0
    per = ns // tn
    nk = t // tk

    def body(a_ref, g_ref, o_ref):
        k = pl.program_id(2)

        @pl.when(k == 0)
        def _():
            o_ref[...] = jnp.zeros_like(o_ref)

        av = a_ref[...]
        if square_a:
            af = av.astype(F32)
            av = af * af
        o_ref[...] += _dot_at(av.astype(BF), g_ref[...].astype(BF))

    return pl.pallas_call(
        body, name=name, grid=(m // tm, n // tn, nk),
        in_specs=[pl.BlockSpec((tk, tm), lambda i, j, k: (k, i)), pl.BlockSpec((tk, tn), lambda i, j, k: (k, j))],
        out_specs=pl.BlockSpec((None, tm, tn), lambda i, j, k: (j // per, i, j % per)),
        out_shape=jax.ShapeDtypeStruct((col_shards, m, ns), F32),
        compiler_params=_params(48, ("arbitrary", "arbitrary", "arbitrary")),
    )(*_in_hbm(a, g))


def _wgrad_wide(a, g, tm, tk, name, col_shards):
    t, m = a.shape
    n = g.shape[1]
    tk = min(tk, t)
    tm = min(tm, m)
    ns = n // col_shards

    def body(a_ref, g_ref, o_ref):
        @pl.when(pl.program_id(1) == 0)
        def _():
            o_ref[...] = jnp.zeros_like(o_ref)

        a_t = a_ref[...].astype(BF).T
        for p in range(col_shards):
            o_ref[p] += _dot(a_t, g_ref[:, p * ns:(p + 1) * ns].astype(BF))

    return pl.pallas_call(
        body, name=name, grid=(m // tm, t // tk),
        in_specs=[pl.BlockSpec((tk, tm), lambda i, k: (k, i)), pl.BlockSpec((tk, n), lambda i, k: (k, 0))],
        out_specs=pl.BlockSpec((col_shards, tm, ns), lambda i, k: (0, i, 0)),
        out_shape=jax.ShapeDtypeStruct((col_shards, m, ns), F32),
        compiler_params=_params(48, ("arbitrary", "arbitrary")),
    )(*_in_hbm(a, g))


def _matmul_bt(a, w, tm, name):
    t, n = a.shape
    k = w.shape[0]
    tm = min(tm, t)

    def body(a_ref, w_ref, o_ref):
        o_ref[...] = _dot_bt(a_ref[...].astype(BF), w_ref[...]).astype(o_ref.dtype)

    return pl.pallas_call(
        body, name=name, grid=(t // tm,),
        in_specs=[pl.BlockSpec((tm, n), lambda i: (i, 0)), pl.BlockSpec((k, n), lambda i: (0, 0))],
        out_specs=pl.BlockSpec((tm, k), lambda i: (i, 0)),
        out_shape=jax.ShapeDtypeStruct((t, k), BF),
        compiler_params=_params(32, ("arbitrary",)),
    )(*_in_hbm(a, w))


def _gmlp_fwd(proj, gg, wt, bb, hg, tm):
    t = proj.shape[0]
    tm = min(tm, t)

    def body(u_ref, v_ref, gg_ref, wt_ref, bb_ref, hg_ref, out_ref):
        for cc in range(tm // CHUNK):
            rows = slice(cc * CHUNK, (cc + 1) * CHUNK)
            for g in range(GM_GROUPS):
                cols = slice(g * 128, (g + 1) * 128)
                u = _gelu(u_ref[rows, cols].astype(F32))
                gv = _gelu(v_ref[rows, cols].astype(F32))
                vn = gv * _rs(gv) * gg_ref[:, cols]
                mixed = _dot(wt_ref[g], vn.astype(BF)) + bb_ref[g]
                a = u * mixed
                out_ref[rows, cols] = (a * _rs(a) * hg_ref[:, cols]).astype(BF)

    return pl.pallas_call(
        body, name="gmlp_fwd", grid=(t // tm,),
        in_specs=[pl.BlockSpec((tm, 512), lambda i: (i, 0)), pl.BlockSpec((tm, 512), lambda i: (i, 1)),
                  pl.BlockSpec((1, 512), lambda i: (0, 0)), pl.BlockSpec((4, 128, 128), lambda i: (0, 0, 0)),
                  pl.BlockSpec((4, 128, 128), lambda i: (0, 0, 0)), pl.BlockSpec((1, 512), lambda i: (0, 0))],
        out_specs=pl.BlockSpec((tm, 512), lambda i: (i, 0)),
        out_shape=jax.ShapeDtypeStruct((t, 1024), BF),
        compiler_params=_params(32, ("arbitrary",)),
    )(*_in_hbm(proj, proj, gg, wt, bb, hg))


def _gmlp_bwd(proj, dmerged, gg, wt, wtt, bb, hg, tm, ride=None):
    t = proj.shape[0]
    tm = min(tm, t)
    nsteps = t // tm

    def body(u_ref, v_ref, dm_ref, gg_ref, wt_ref, wtt_ref, bb_ref, hg_ref,
             dp_ref, dw_ref, db_ref, dgg_ref, dhg_ref):
        i = pl.program_id(0)

        @pl.when(i == 0)
        def _():
            dw_ref[...] = jnp.zeros_like(dw_ref)
            db_ref[...] = jnp.zeros_like(db_ref)
            dgg_ref[...] = jnp.zeros_like(dgg_ref)
            dhg_ref[...] = jnp.zeros_like(dhg_ref)

        for cc in range(tm // CHUNK):
            rows = slice(cc * CHUNK, (cc + 1) * CHUNK)
            for g in range(GM_GROUPS):
                cols = slice(g * 128, (g + 1) * 128)
                up = u_ref[rows, cols].astype(F32)
                gp = v_ref[rows, cols].astype(F32)
                u, u_grad = _gelu_and_grad(up)
                gv, gv_grad = _gelu_and_grad(gp)
                rv = _rs(gv)
                gvh = gv * rv
                ggv = gg_ref[:, cols]
                vnb = (gvh * ggv).astype(BF)
                mixed = _dot(wt_ref[g], vnb) + bb_ref[g]
                a = u * mixed
                ra = _rs(a)
                ah = a * ra
                dm = dm_ref[rows, cols].astype(F32)
                dhg_ref[:, cols] += jnp.sum(dm * ah, axis=0, keepdims=True)
                dah = dm * hg_ref[:, cols]
                da = ra * (dah - ah * jnp.mean(dah * ah, axis=-1, keepdims=True))
                du = da * mixed
                dmix = da * u
                db_ref[g] += dmix
                dmb = dmix.astype(BF)
                dw_ref[g] += _dot_bt(dmb, vnb)
                dvn = _dot(wtt_ref[g], dmb)
                dgg_ref[:, cols] += jnp.sum(dvn * gvh, axis=0, keepdims=True)
                dgh = dvn * ggv
                dgv = rv * (dgh - gvh * jnp.mean(dgh * gvh, axis=-1, keepdims=True))
                dp_ref[rows, cols] = (du * u_grad).astype(BF)
                dp_ref[rows, 512 + g * 128:512 + (g + 1) * 128] = (dgv * gv_grad).astype(BF)

        @pl.when(i == nsteps - 1)
        def _():
            r = lax.broadcasted_iota(jnp.int32, (CHUNK, CHUNK), 0)
            c = lax.broadcasted_iota(jnp.int32, (CHUNK, CHUNK), 1)
            for g in range(GM_GROUPS):
                dw_ref[g] = jnp.where(c <= r, dw_ref[g], 0.0)
                db_ref[g] = jnp.broadcast_to(jnp.sum(db_ref[g], axis=-1, keepdims=True), (CHUNK, CHUNK))

    small = lambda shape: pl.BlockSpec(shape, lambda i: (0,) * len(shape))
    res, rode = _ride_call(
        body, "gmlp_bwd", (nsteps,),
        in_specs=[pl.BlockSpec((tm, 512), lambda i: (i, 0)), pl.BlockSpec((tm, 512), lambda i: (i, 1)),
                  pl.BlockSpec((tm, 512), lambda i: (i, 0)), small((1, 512)), small((4, 128, 128)),
                  small((4, 128, 128)), small((4, 128, 128)), small((1, 512))],
        out_specs=[pl.BlockSpec((tm, 1024), lambda i: (i, 0)), small((4, 128, 128)), small((4, 128, 128)),
                   small((1, 512)), small((1, 512))],
        out_shape=[jax.ShapeDtypeStruct((t, IN_COLS), BF), jax.ShapeDtypeStruct((4, 128, 128), F32),
                   jax.ShapeDtypeStruct((4, 128, 128), F32), jax.ShapeDtypeStruct((1, 512), F32),
                   jax.ShapeDtypeStruct((1, 512), F32)],
        scratch_shapes=[], operands=(proj, proj, dmerged, gg, wt, wtt, bb, hg), vmem_mb=32, ride=ride)
    return (*res, rode)


def _other_chips(x, y):
    return ((1 - x, y), (x, 1 - y), (1 - x, 1 - y))


class _GatherExchange:
    def __init__(self, shards):
        n = len(shards)
        self.n = n
        self.in_arrays = list(shards)
        self.out_shape = [jax.ShapeDtypeStruct((N_CHIPS,) + a.shape, a.dtype) for a in shards]
        self.half_rows = [a.shape[0] // 2 for a in shards]
        sems = lambda k: pltpu.SemaphoreType.DMA((k,))
        self.scratch_shapes = [pltpu.VMEM(a.shape, a.dtype) for a in shards] + [
            sems(3 * n), sems(3 * n), sems(3 * n), sems(3 * n), sems(n), sems(n)]

    def _copies(self, ins, outs, scr):
        n = self.n
        stages, (ici_send, ici_recv, d2d_send, d2d_recv, ld_sems, st_sems) = scr[:n], scr[n:]
        x, y, c = lax.axis_index("x"), lax.axis_index("y"), lax.axis_index("c")
        q = 2 * x + y
        loads = [pltpu.make_async_copy(ins[w], stages[w], ld_sems.at[w]) for w in range(n)]
        stores = [pltpu.make_async_copy(stages[w], outs[w].at[q], st_sems.at[w]) for w in range(n)]
        ici, d2d = [], []
        for w in range(n):
            half = pl.ds(c * self.half_rows[w], self.half_rows[w])
            for k, (px, py) in enumerate(_other_chips(x, y)):
                ici.append(pltpu.make_async_remote_copy(
                    src_ref=ins[w].at[half], dst_ref=outs[w].at[q, half], send_sem=ici_send.at[3 * w + k],
                    recv_sem=ici_recv.at[3 * w + k], device_id=(px, py, c), device_id_type=MESH))
                landed = outs[w].at[2 * px + py, half]
                d2d.append(pltpu.make_async_remote_copy(
                    src_ref=landed, dst_ref=landed, send_sem=d2d_send.at[3 * w + k],
                    recv_sem=d2d_recv.at[3 * w + k], device_id=(x, y, 1 - c), device_id_type=MESH))
        return loads, stores, ici, d2d

    def start(self, ins, outs, scr):
        loads, stores, ici, _ = self._copies(ins, outs, scr)
        for cp in loads + ici:
            cp.start()
        for ld, st in zip(loads, stores):
            ld.wait()
            st.start()

    def relay(self, ins, outs, scr):
        _, _, ici, d2d = self._copies(ins, outs, scr)
        for got, fwd in zip(ici, d2d):
            got.wait_recv()
            fwd.start()

    def finish(self, ins, outs, scr):
        _, stores, ici, d2d = self._copies(ins, outs, scr)
        for cp in ici:
            cp.wait_send()
        for cp in d2d + stores:
            cp.wait()


class _SiblingExchange:
    def __init__(self, slabs):
        n = len(slabs)
        self.n = n
        self.in_arrays = list(slabs)
        self.out_shape = [jax.ShapeDtypeStruct((N_CHIPS,) + a.shape[1:], a.dtype) for a in slabs]
        self.scratch_shapes = [pltpu.SemaphoreType.DMA((4 * n,)), pltpu.SemaphoreType.DMA((4 * n,))]

    def _copies(self, ins, outs, scr):
        send_sems, recv_sems = scr
        x, y, c = lax.axis_index("x"), lax.axis_index("y"), lax.axis_index("c")
        return [pltpu.make_async_remote_copy(
            src_ref=ins[w].at[2 * p + (1 - c)], dst_ref=outs[w].at[p], send_sem=send_sems.at[4 * w + p],
            recv_sem=recv_sems.at[4 * w + p], device_id=(x, y, 1 - c), device_id_type=MESH)
            for w in range(self.n) for p in range(N_CHIPS)]

    def start(self, ins, outs, scr):
        for cp in self._copies(ins, outs, scr):
            cp.start()

    def finish(self, ins, outs, scr):
        for cp in self._copies(ins, outs, scr):
            cp.wait()


class _ChipExchange:
    def __init__(self, sums):
        n = len(sums)
        self.n = n
        self.in_arrays = list(sums)
        self.out_shape = [jax.ShapeDtypeStruct(a.shape, a.dtype) for a in sums]
        self.scratch_shapes = [pltpu.SemaphoreType.DMA((3 * n,)), pltpu.SemaphoreType.DMA((3 * n,))]

    def _copies(self, ins, outs, scr):
        send_sems, recv_sems = scr
        x, y, c = lax.axis_index("x"), lax.axis_index("y"), lax.axis_index("c")
        q = 2 * x + y
        return [pltpu.make_async_remote_copy(
            src_ref=ins[w].at[2 * px + py], dst_ref=outs[w].at[q], send_sem=send_sems.at[3 * w + k],
            recv_sem=recv_sems.at[3 * w + k], device_id=(px, py, c), device_id_type=MESH)
            for w in range(self.n) for k, (px, py) in enumerate(_other_chips(x, y))]

    def start(self, ins, outs, scr):
        for cp in self._copies(ins, outs, scr):
            cp.start()

    def finish(self, ins, outs, scr):
        for cp in self._copies(ins, outs, scr):
            cp.wait()


class _NoExchange:
    in_arrays, out_shape, scratch_shapes = (), (), ()

    def start(self, ins, outs, scr):
        pass

    def finish(self, ins, outs, scr):
        pass


def _run_exchange(ex, name):
    n_in, n_out = len(ex.in_arrays), len(ex.out_shape)

    def body(*refs):
        ins, outs, scr = refs[:n_in], refs[n_in:n_in + n_out], refs[n_in + n_out:]
        ex.start(ins, outs, scr)
        if hasattr(ex, "relay"):
            ex.relay(ins, outs, scr)
        ex.finish(ins, outs, scr)

    return pl.pallas_call(
        body, name=name, in_specs=[ANY] * n_in, out_specs=[ANY] * n_out, out_shape=ex.out_shape,
        scratch_shapes=ex.scratch_shapes, compiler_params=_params(24),
    )(*ex.in_arrays)


def _ride_call(body, name, grid, in_specs, out_specs, out_shape, scratch_shapes, operands, vmem_mb, ride=None,
               aliases=None):
    ride = ride or _NoExchange()
    ni, no, ns = len(in_specs), len(out_specs), len(scratch_shapes)
    ri, ro = len(ride.in_arrays), len(ride.out_shape)
    total = math.prod(grid)

    def wrapped(*refs):
        ins, rins = refs[:ni], refs[ni:ni + ri]
        outs, routs = refs[ni + ri:ni + ri + no], refs[ni + ri + no:ni + ri + no + ro]
        scr, rscr = refs[ni + ri + no + ro:ni + ri + no + ro + ns], refs[ni + ri + no + ro + ns:]
        step = pl.program_id(0)
        for ax in range(1, len(grid)):
            step = step * grid[ax] + pl.program_id(ax)

        @pl.when(step == 0)
        def _():
            ride.start(rins, routs, rscr)

        if hasattr(ride, "relay"):
            @pl.when(step == (3 * total) // 4)
            def _():
                ride.relay(rins, routs, rscr)

        body(*ins, *outs, *scr)

        @pl.when(step == total - 1)
        def _():
            ride.finish(rins, routs, rscr)

    res = pl.pallas_call(
        wrapped, name=name, grid=grid, in_specs=list(in_specs) + [ANY] * ri, out_specs=list(out_specs) + [ANY] * ro,
        out_shape=list(out_shape) + list(ride.out_shape),
        scratch_shapes=list(scratch_shapes) + list(ride.scratch_shapes), input_output_aliases=aliases or {},
        compiler_params=_params(vmem_mb, ("arbitrary",) * len(grid)),
    )(*_in_hbm(*operands), *ride.in_arrays)
    return res[:no], res[no:]


def _neg_log_sig(z):
    n = jnp.maximum(z, 0.0) + jnp.log(1.0 + jnp.exp(-jnp.abs(z)))
    return n, z - n


def _running_sums(n, tri2):
    hi = n.astype(BF)
    lo = (n - hi.astype(F32)).astype(BF)
    return _dot(jnp.concatenate([hi, lo], axis=1), tri2)


def _head_sums(x, h0):
    s0 = jnp.sum(jnp.where(h0, x, 0.0), axis=-1, keepdims=True)
    s1 = jnp.sum(jnp.where(h0, 0.0, x), axis=-1, keepdims=True)
    return jnp.where(h0, s0, s1)


SB_BLOCKS_PER_STEP = 4
SB_PAIRS_PER_STEP = 2


def _sb_masks(tq):
    h0 = lax.broadcasted_iota(jnp.int32, (tq, 128), 1) < HEAD_LANES
    r = lax.broadcasted_iota(jnp.int32, (2 * tq, tq), 0)
    c = lax.broadcasted_iota(jnp.int32, (2 * tq, tq), 1)
    return h0, c < jnp.where(r >= tq, r - tq, r)


def _sb_stack(x, h0):
    zero = jnp.zeros_like(x)
    return jnp.concatenate([jnp.where(h0, x, zero), jnp.where(h0, zero, x)], axis=0)


def _tri(tq, op):
    return op(lax.broadcasted_iota(jnp.int32, (tq, tq), 0), lax.broadcasted_iota(jnp.int32, (tq, tq), 1)).astype(BF)


def _sb_fwd(proj, merged, hg, nb, s, tq, ride=None):
    t = nb * s
    tq = min(tq, s)
    nq = s // tq
    per = min(SB_BLOCKS_PER_STEP, nq)
    ns = nq // per
    gp, ng, w = SB_PAIRS_PER_STEP, 4 // SB_PAIRS_PER_STEP, 128 * SB_PAIRS_PER_STEP

    def body(q_ref, k_ref, v_ref, hg_ref, merged_ref, o_ref, tot_ref, mb_ref, nblk_ref, acc, cr, c_min):
        del merged_ref
        h0, causal = _sb_masks(tq)
        tri_gt = _tri(tq, lambda r, c: r > c)
        tri_gt = jnp.concatenate([tri_gt, tri_gt], axis=0)
        lanes = [slice(g * 128, (g + 1) * 128) for g in range(gp)]
        zeros = jnp.zeros((2 * tq, 1), F32)

        def query_block(i, rows):
            qsts = [_sb_stack(q_ref[rows, lanes[g]] * SB_SCALE, h0) for g in range(gp)]

            def block(g, j, masked, c_in):
                start = pl.multiple_of(j * tq, tq)
                kj = k_ref[pl.ds(start, tq), lanes[g]]
                vj = v_ref[pl.ds(start, tq), lanes[g]]
                n, l = _neg_log_sig(_dot_bt(qsts[g], kj))
                if masked:
                    n = jnp.where(causal, n, 0.0)
                a = jnp.exp(l - (_running_sums(n, tri_gt) + c_in))
                if masked:
                    a = jnp.where(causal, a, 0.0)
                return _dot(a.astype(BF), vj), c_in + jnp.sum(n, axis=-1, keepdims=True)

            def keep(parts):
                for g, (p, c) in enumerate(parts):
                    acc[g] = p
                    cr[g] = c
                c_min[0] = jnp.min(functools.reduce(jnp.minimum, [c for _, c in parts]))

            @pl.when(i == 0)
            def _():
                keep([block(g, 0, True, zeros) for g in range(gp)])

            @pl.when(i > 0)
            def _():
                diag = [block(g, i, True, zeros) for g in range(gp)]
                prev = [block(g, i - 1, False, diag[g][1]) for g in range(gp)]
                keep([(diag[g][0] + prev[g][0], prev[g][1]) for g in range(gp)])

            def cond(carry):
                return jnp.logical_and(carry[0] < i, carry[1] < -SB_SKIP)

            def step(carry):
                more = [block(g, i - 1 - carry[0], False, cr[g]) for g in range(gp)]
                for g, (p, c) in enumerate(more):
                    acc[g] += p
                    cr[g] = c
                return carry[0] + 1, jnp.min(functools.reduce(jnp.minimum, [c for _, c in more]))

            walked, _ = lax.while_loop(cond, step, (jnp.minimum(i, 1), c_min[0]))
            return walked

        for u in range(per):
            rows = slice(u * tq, (u + 1) * tq)
            walked = query_block(pl.program_id(2) * per + u, rows)
            for g in range(gp):
                o = jnp.where(h0, acc[g, 0:tq, :], acc[g, tq:2 * tq, :])
                o_ref[rows, lanes[g]] = o
                tot_ref[rows, lanes[g]] = jnp.where(h0, cr[g, 0:tq, :], cr[g, tq:2 * tq, :])
                ro = lax.rsqrt(_head_sums(o * o, h0) * (1.0 / HEAD_LANES) + EPS)
                mb_ref[rows, lanes[g]] = (o * ro * hg_ref[:, lanes[g]]).astype(BF)
            nblk_ref[u * 8:(u + 1) * 8, :] = jnp.full((8, 128), walked.astype(F32))

    blk = lambda col0: pl.BlockSpec((per * tq, w), lambda b, hg_, i: (b * ns + i, col0 + hg_))
    seq = lambda col0: pl.BlockSpec((s, w), lambda b, hg_, i: (b, col0 + hg_))
    first = 1024 // w
    (o, tot, mb, nblk), rode = _ride_call(
        body, "sb_fwd", (nb, ng, ns),
        in_specs=[blk(first), seq(first + ng), seq(first + 2 * ng),
                  pl.BlockSpec((1, w), lambda b, hg_, i: (0, ng + hg_)), ANY],
        out_specs=[blk(0), blk(0), blk(ng),
                   pl.BlockSpec((None, None, per * 8, 128), lambda b, hg_, i: (b, hg_, i, 0))],
        out_shape=[jax.ShapeDtypeStruct((t, 512), F32), jax.ShapeDtypeStruct((t, 512), F32),
                   jax.ShapeDtypeStruct((t, 1024), BF), jax.ShapeDtypeStruct((nb, ng, nq * 8, 128), F32)],
        scratch_shapes=[pltpu.VMEM((gp, 2 * tq, 128), F32), pltpu.VMEM((gp, 2 * tq, 1), F32),
                        pltpu.SMEM((1,), F32)],
        operands=(proj, proj, proj, hg, merged), vmem_mb=40, ride=ride, aliases={4: 2})
    return o, tot, mb, nblk, rode


def _sb_bwd(proj, o_sb, tot, nblk, dmerged, dproj, hg, nb, s, tq, ride=None):
    t = nb * s
    tq = min(tq, s)
    nq = s // tq
    per = min(SB_BLOCKS_PER_STEP, nq)
    ns = nq // per
    gp, ng, w = SB_PAIRS_PER_STEP, 4 // SB_PAIRS_PER_STEP, 128 * SB_PAIRS_PER_STEP

    def body(q_ref, k_ref, v_ref, o_ref, tot_ref, nblk_ref, dm_ref, hg_ref, dproj_ref,
             dq_ref, dk_ref, dv_ref, dhg_ref, dk_acc, dv_acc, dq_acc, cm, cg):
        del dproj_ref
        h0, causal = _sb_masks(tq)
        tri_le = _tri(tq, lambda r, c: r <= c)
        tri_le = jnp.concatenate([tri_le, tri_le], axis=0)
        tri_lt = _tri(tq, lambda r, c: r < c)
        lanes = [slice(g * 128, (g + 1) * 128) for g in range(gp)]

        @pl.when(pl.program_id(2) == 0)
        def _():
            dk_acc[...] = jnp.zeros_like(dk_acc)
            dv_acc[...] = jnp.zeros_like(dv_acc)
            dhg_ref[...] = jnp.zeros_like(dhg_ref)

        def query_block(i, rows):
            for ref in (dq_acc, cm, cg):
                ref[...] = jnp.zeros_like(ref)
            qsts, dosts, tots = [], [], []
            for g in range(gp):
                qsts.append(_sb_stack(q_ref[rows, lanes[g]] * SB_SCALE, h0))
                o = o_ref[rows, lanes[g]]
                ro = lax.rsqrt(_head_sums(o * o, h0) * (1.0 / HEAD_LANES) + EPS)
                oh = o * ro
                dm = dm_ref[rows, lanes[g]].astype(F32)
                dhg_ref[:, lanes[g]] += jnp.sum(dm * oh, axis=0, keepdims=True)
                doh = dm * hg_ref[:, lanes[g]]
                do = ro * (doh - oh * (_head_sums(doh * oh, h0) * (1.0 / HEAD_LANES)))
                dosts.append(_sb_stack(do.astype(BF), h0))
                first = g * 128
                tots.append(jnp.concatenate(
                    [tot_ref[rows, first:first + 1], tot_ref[rows, first + HEAD_LANES:first + HEAD_LANES + 1]], axis=0))
            qsts_t = [q.T for q in qsts]
            dosts_t = [d.T for d in dosts]

            def block(g, j, masked, cm_in, cg_in):
                start = pl.multiple_of(j * tq, tq)
                kj = k_ref[pl.ds(start, tq), lanes[g]]
                vj = v_ref[pl.ds(start, tq), lanes[g]]
                n, l = _neg_log_sig(_dot_bt(qsts[g], kj))
                if masked:
                    n = jnp.where(causal, n, 0.0)
                a = jnp.exp(l - (tots[g] - cm_in - _running_sums(n, tri_le)))
                if masked:
                    a = jnp.where(causal, a, 0.0)
                gm = a * _dot_bt(dosts[g], vj)
                pp = cg_in + _dot(gm.astype(BF), tri_lt)
                dz = gm - jnp.exp(l) * (gm + pp)
                if masked:
                    dz = jnp.where(causal, dz, 0.0)
                dzb = dz.astype(BF)
                dk_acc[g, :, pl.ds(start, tq)] += _dot(qsts_t[g], dzb)
                dv_acc[g, :, pl.ds(start, tq)] += _dot(dosts_t[g], a.astype(BF))
                return (_dot(dzb, kj), cm_in + jnp.sum(n, axis=-1, keepdims=True),
                        cg_in + jnp.sum(gm, axis=-1, keepdims=True))

            def step(j, carry):
                for g in range(gp):
                    dq, cm[g], cg[g] = block(g, j, False, cm[g], cg[g])
                    dq_acc[g] += dq
                return carry

            walked = jnp.clip(nblk_ref[pl.program_id(0), pl.program_id(1), i].astype(jnp.int32),
                              jnp.minimum(i, 1), i)
            lax.fori_loop(i - walked, i - 1, step, 0)

            @pl.when(i == 0)
            def _():
                for g in range(gp):
                    dq_acc[g] = block(g, 0, True, cm[g], cg[g])[0]

            @pl.when(i > 0)
            def _():
                prev = [block(g, i - 1, False, cm[g], cg[g]) for g in range(gp)]
                diag = [block(g, i, True, prev[g][1], prev[g][2]) for g in range(gp)]
                for g in range(gp):
                    dq_acc[g] += prev[g][0] + diag[g][0]

            for g in range(gp):
                dq = jnp.where(h0, dq_acc[g, 0:tq, :], dq_acc[g, tq:2 * tq, :])
                dq_ref[rows, lanes[g]] = (dq * SB_SCALE).astype(BF)

        for u in range(per):
            query_block(pl.program_id(2) * per + u, slice(u * tq, (u + 1) * tq))

        @pl.when(pl.program_id(2) == ns - 1)
        def _():
            for g in range(gp):
                dk_ref[:, lanes[g]] = dk_acc[g].T.astype(BF)
                dv_ref[:, lanes[g]] = dv_acc[g].T.astype(BF)

    blk = lambda col0: pl.BlockSpec((per * tq, w), lambda b, hg_, i: (b * ns + i, col0 + hg_))
    seq = lambda col0: pl.BlockSpec((s, w), lambda b, hg_, i: (b, col0 + hg_))
    first = 1024 // w
    (dq, dk, dv, dhg), rode = _ride_call(
        body, "sb_bwd", (nb, ng, ns),
        in_specs=[blk(first), seq(first + ng), seq(first + 2 * ng), blk(0), blk(0),
                  pl.BlockSpec(memory_space=pltpu.SMEM), blk(ng),
                  pl.BlockSpec((1, w), lambda b, hg_, i: (0, ng + hg_)), ANY],
        out_specs=[blk(first), seq(0), seq(0), pl.BlockSpec((None, 1, w), lambda b, hg_, i: (b, 0, hg_))],
        out_shape=[jax.ShapeDtypeStruct((t, IN_COLS), BF), jax.ShapeDtypeStruct((t, 512), BF),
                   jax.ShapeDtypeStruct((t, 512), BF), jax.ShapeDtypeStruct((nb, 1, 512), F32)],
        scratch_shapes=[pltpu.VMEM((gp, 128, s), F32), pltpu.VMEM((gp, 128, s), F32),
                        pltpu.VMEM((gp, 2 * tq, 128), F32), pltpu.VMEM((gp, 2 * tq, 1), F32),
                        pltpu.VMEM((gp, 2 * tq, 1), F32)],
        operands=(proj, proj, proj, o_sb, tot, nblk.reshape(nb, ng, nq, 8, 128)[:, :, :, 0, 0], dmerged, hg, dproj),
        vmem_mb=48, ride=ride, aliases={8: 0})
    return dq, dk, dv, dhg, rode


def _place(buf, piece, col_block, name):
    t, w = piece.shape
    tm = min(t, 1024)

    def body(piece_ref, buf_ref, out_ref):
        del buf_ref
        out_ref[...] = piece_ref[...]

    return pl.pallas_call(
        body, name=name, grid=(t // tm,),
        in_specs=[pl.BlockSpec((tm, w), lambda i: (i, 0)), ANY],
        out_specs=pl.BlockSpec((tm, w), lambda i: (i, col_block)),
        out_shape=jax.ShapeDtypeStruct(buf.shape, buf.dtype), input_output_aliases={1: 0},
        compiler_params=_params(16, ("arbitrary",)),
    )(piece, buf)


def _softmax_rows(sc):
    e = jnp.exp(sc - jnp.max(sc, axis=-1, keepdims=True))
    return e / jnp.sum(e, axis=-1, keepdims=True)


def _mix_cross_fwd(x, merged, w_out, gc, w_cq, kv, w_co, s, tm):
    t, d = x.shape
    tm = min(tm, s)
    per = s // tm
    inv = 1.0 / math.sqrt(X_HEAD_DIM)

    def body(x_ref, m_ref, wo_ref, gc_ref, wq_ref, kv_ref, wc_ref, h1_ref, h2_ref, hn_ref, qc_ref, oc_ref):
        h1 = x_ref[...] + _dot(m_ref[...], wo_ref[...])
        h1_ref[...] = h1
        hn = (h1 * _rs(h1) * gc_ref[...]).astype(BF)
        hn_ref[...] = hn
        qc = _dot(hn, wq_ref[...]).astype(BF)
        qc_ref[...] = qc
        for h in range(X_HEADS):
            cols = slice(h * X_HEAD_DIM, (h + 1) * X_HEAD_DIM)
            kh = kv_ref[:, h * X_HEAD_DIM:(h + 1) * X_HEAD_DIM]
            vh = kv_ref[:, d + h * X_HEAD_DIM:d + (h + 1) * X_HEAD_DIM]
            p = _softmax_rows(_dot_bt(qc[:, cols], kh) * inv)
            oc_ref[:, cols] = _dot(p.astype(BF), vh).astype(BF)
        h2_ref[...] = h1 + _dot(oc_ref[...], wc_ref[...])

    row = lambda width: pl.BlockSpec((tm, width), lambda i: (i, 0))
    full = lambda a, b: pl.BlockSpec((a, b), lambda i: (0, 0))
    return pl.pallas_call(
        body, name="mix_cross_fwd", grid=(t // tm,),
        in_specs=[row(d), row(d), full(d, d), full(1, d), full(d, d),
                  pl.BlockSpec((N_MEM, 2 * d), lambda i: (i // per, 0)), full(d, d)],
        out_specs=[row(d), row(d), row(d), row(d), row(d)],
        out_shape=[jax.ShapeDtypeStruct((t, d), F32), jax.ShapeDtypeStruct((t, d), F32),
                   jax.ShapeDtypeStruct((t, d), BF), jax.ShapeDtypeStruct((t, d), BF),
                   jax.ShapeDtypeStruct((t, d), BF)],
        compiler_params=_params(48, ("arbitrary",)),
    )(*_in_hbm(x, merged, w_out, gc, w_cq, kv, w_co))


def _cross_bwd(dh2, h1, qc, gc, w_cq, kv, w_co, s, tm):
    t, d = dh2.shape
    tm = min(tm, s)
    per = s // tm
    nb = t // s
    inv = 1.0 / math.sqrt(X_HEAD_DIM)

    def body(dh2_ref, h1_ref, qc_ref, gc_ref, wq_ref, kv_ref, wc_ref, dh1_ref, dqc_ref, dkv_ref, dgc_ref):
        i = pl.program_id(0)

        @pl.when(i == 0)
        def _():
            dgc_ref[...] = jnp.zeros_like(dgc_ref)

        @pl.when(i % per == 0)
        def _():
            dkv_ref[...] = jnp.zeros_like(dkv_ref)

        dh2 = dh2_ref[...]
        h1 = h1_ref[...]
        r = _rs(h1)
        h1h = h1 * r
        gcv = gc_ref[...]
        qc = qc_ref[...]
        do = _dot_bt(dh2.astype(BF), wc_ref[...]).astype(BF)
        for h in range(X_HEADS):
            cols = slice(h * X_HEAD_DIM, (h + 1) * X_HEAD_DIM)
            vcols = slice(d + h * X_HEAD_DIM, d + (h + 1) * X_HEAD_DIM)
            kh = kv_ref[:, cols]
            vh = kv_ref[:, vcols]
            p = _softmax_rows(_dot_bt(qc[:, cols], kh) * inv)
            dp = _dot_bt(do[:, cols], vh)
            ds = (p * (dp - jnp.sum(dp * p, axis=-1, keepdims=True)) * inv).astype(BF)
            dqc_ref[:, cols] = _dot(ds, kh).astype(BF)
            dkv_ref[:, cols] += _dot_at(ds, qc[:, cols])
            dkv_ref[:, vcols] += _dot_at(p.astype(BF), do[:, cols])
        dhn = _dot_bt(dqc_ref[...], wq_ref[...])
        dx, dg = _rms_bwd(dhn, h1h, r, gcv)
        dh1_ref[...] = dh2 + dx
        dgc_ref[...] += jnp.sum(dg, axis=0, keepdims=True)

    row = lambda width: pl.BlockSpec((tm, width), lambda i: (i, 0))
    full = lambda a, b: pl.BlockSpec((a, b), lambda i: (0, 0))
    kvspec = pl.BlockSpec((N_MEM, 2 * d), lambda i: (i // per, 0))
    return pl.pallas_call(
        body, name="cross_bwd", grid=(t // tm,),
        in_specs=[row(d), row(d), row(d), full(1, d), full(d, d), kvspec, full(d, d)],
        out_specs=[row(d), row(d), kvspec, full(1, d)],
        out_shape=[jax.ShapeDtypeStruct((t, d), F32), jax.ShapeDtypeStruct((t, d), BF),
                   jax.ShapeDtypeStruct((nb * N_MEM, 2 * d), F32), jax.ShapeDtypeStruct((1, d), F32)],
        compiler_params=_params(48, ("arbitrary",)),
    )(*_in_hbm(dh2, h1, qc, gc, w_cq, kv, w_co))


def _mem_bwd(mem, gm, dkv, w_ckv, tm):
    t, d = mem.shape
    tm = min(tm, t)

    def body(mem_ref, dkv_ref, w_ref, dg_ref):
        @pl.when(pl.program_id(0) == 0)
        def _():
            dg_ref[...] = jnp.zeros_like(dg_ref)

        mv = mem_ref[...]
        dmn = _dot_bt(dkv_ref[...].astype(BF), w_ref[...])
        dg_ref[...] += jnp.sum(dmn * (mv * _rs(mv)), axis=0, keepdims=True)

    del gm
    return pl.pallas_call(
        body, name="mem_bwd", grid=(t // tm,),
        in_specs=[pl.BlockSpec((tm, d), lambda i: (i, 0)), pl.BlockSpec((tm, 2 * d), lambda i: (i, 0)),
                  pl.BlockSpec((d, 2 * d), lambda i: (0, 0))],
        out_specs=pl.BlockSpec((1, d), lambda i: (0, 0)),
        out_shape=jax.ShapeDtypeStruct((1, d), F32),
        compiler_params=_params(32, ("arbitrary",)),
    )(mem, dkv, w_ckv)


def _ffn_loss_fwd(h2, gf, w1, w2, gl, target, tm):
    t, d = h2.shape
    tm = min(tm, t)

    def body(h2_ref, gf_ref, w1_ref, w2_ref, gl_ref, tg_ref, hn_ref, f_ref, dh3_ref, dgl_ref, loss_ref):
        @pl.when(pl.program_id(0) == 0)
        def _():
            dgl_ref[...] = jnp.zeros_like(dgl_ref)
            loss_ref[...] = jnp.zeros_like(loss_ref)

        h2 = h2_ref[...]
        hn = (h2 * _rs(h2) * gf_ref[...]).astype(BF)
        hn_ref[...] = hn
        h3 = h2
        for c in range(4):
            f = jnp.maximum(_dot(hn, w1_ref[c]), 0.0)
            f_ref[:, c * 1024:(c + 1) * 1024] = f.astype(BF)
            h3 = h3 + _dot((f * f).astype(BF), w2_ref[c])
        r3 = _rs(h3)
        yh = h3 * r3
        glv = gl_ref[...]
        e = yh * glv - tg_ref[...]
        loss_ref[...] += 0.5 * jnp.sum(jnp.sum(e * e, axis=-1, keepdims=True) * (1.0 / d), axis=0, keepdims=True)
        dy = e * (1.0 / d)
        dx, dg = _rms_bwd(dy, yh, r3, glv)
        dh3_ref[...] = dx
        dgl_ref[...] += jnp.sum(dg, axis=0, keepdims=True)

    row = lambda width: pl.BlockSpec((tm, width), lambda i: (i, 0))
    return pl.pallas_call(
        body, name="ffn_loss_fwd", grid=(t // tm,),
        in_specs=[row(d), pl.BlockSpec((1, d), lambda i: (0, 0)), pl.BlockSpec((4, d, 1024), lambda i: (0, 0, 0), pipeline_mode=pl.Buffered(1)),
                  pl.BlockSpec((4, 1024, d), lambda i: (0, 0, 0), pipeline_mode=pl.Buffered(1)),
                  pl.BlockSpec((1, d), lambda i: (0, 0)), row(d)],
        out_specs=[row(d), row(D_FF), row(d), pl.BlockSpec((1, d), lambda i: (0, 0)),
                   pl.BlockSpec((1, 1), lambda i: (0, 0))],
        out_shape=[jax.ShapeDtypeStruct((t, d), BF), jax.ShapeDtypeStruct((t, D_FF), BF),
                   jax.ShapeDtypeStruct((t, d), F32), jax.ShapeDtypeStruct((1, d), F32),
                   jax.ShapeDtypeStruct((1, 1), F32)],
        compiler_params=_params(56, ("arbitrary",)),
    )(*_in_hbm(h2, gf, w1, w2, gl, target))


def _ffn_bwd(dh3, f, h2, gf, w1, w2, tm):
    t, d = h2.shape
    tm = min(tm, t)

    def body(dh3_ref, f_ref, h2_ref, gf_ref, w1_ref, w2_ref, dh2_ref, dpre_ref, dgf_ref):
        @pl.when(pl.program_id(0) == 0)
        def _():
            dgf_ref[...] = jnp.zeros_like(dgf_ref)

        dh3 = dh3_ref[...]
        dh3b = dh3.astype(BF)
        dhn = jnp.zeros((tm, d), F32)
        for c in range(4):
            cols = slice(c * 1024, (c + 1) * 1024)
            dpre = (_dot_bt(dh3b, w2_ref[c]) * (2.0 * f_ref[:, cols].astype(F32))).astype(BF)
            dpre_ref[:, cols] = dpre
            dhn = dhn + _dot_bt(dpre, w1_ref[c])
        h2 = h2_ref[...]
        r = _rs(h2)
        dx, dg = _rms_bwd(dhn, h2 * r, r, gf_ref[...])
        dh2_ref[...] = dh3 + dx
        dgf_ref[...] += jnp.sum(dg, axis=0, keepdims=True)

    row = lambda width: pl.BlockSpec((tm, width), lambda i: (i, 0))
    return pl.pallas_call(
        body, name="ffn_bwd", grid=(t // tm,),
        in_specs=[row(d), row(D_FF), row(d), pl.BlockSpec((1, d), lambda i: (0, 0)),
                  pl.BlockSpec((4, d, 1024), lambda i: (0, 0, 0), pipeline_mode=pl.Buffered(1)),
                  pl.BlockSpec((4, 1024, d), lambda i: (0, 0, 0), pipeline_mode=pl.Buffered(1))],
        out_specs=[row(d), row(D_FF), pl.BlockSpec((1, d), lambda i: (0, 0))],
        out_shape=[jax.ShapeDtypeStruct((t, d), F32), jax.ShapeDtypeStruct((t, D_FF), BF),
                   jax.ShapeDtypeStruct((1, d), F32)],
        compiler_params=_params(56, ("arbitrary",)),
    )(*_in_hbm(dh3, f, h2, gf, w1, w2))


def _in_bwd(dproj, dh1, x, g, w_in, tm, ride=None):
    t, d = x.shape
    n = w_in.shape[1]
    tm = min(tm, t)

    def body(dp_ref, dh1_ref, x_ref, g_ref, w_ref, dx_ref, dg_ref):
        @pl.when(pl.program_id(0) == 0)
        def _():
            dg_ref[...] = jnp.zeros_like(dg_ref)

        dxn = _dot_bt(dp_ref[...], w_ref[...])
        xv = x_ref[...]
        r = _rs(xv)
        dx, dg = _rms_bwd(dxn, xv * r, r, g_ref[...])
        dx_ref[...] = dh1_ref[...] + dx
        dg_ref[...] += jnp.sum(dg, axis=0, keepdims=True)

    row = lambda width: pl.BlockSpec((tm, width), lambda i: (i, 0))
    (dx, dg), rode = _ride_call(
        body, "in_bwd", (t // tm,),
        in_specs=[row(n), row(d), row(d), pl.BlockSpec((1, d), lambda i: (0, 0)),
                  pl.BlockSpec((d, n), lambda i: (0, 0))],
        out_specs=[row(d), pl.BlockSpec((1, d), lambda i: (0, 0))],
        out_shape=[jax.ShapeDtypeStruct((t, d), F32), jax.ShapeDtypeStruct((1, d), F32)],
        scratch_shapes=[], operands=(dproj, dh1, x, g, w_in), vmem_mb=48, ride=ride)
    return dx, dg, rode


class _GradReduce:
    def __init__(self, c_idx):
        self.c_idx = c_idx
        self.sums = {}

    def sibling(self, slabs):
        return _SiblingExchange(slabs)

    def chip(self, names, slabs, recv):
        for k, a, r in zip(names, slabs, recv):
            self.sums[k] = _chip_sum(a, r, self.c_idx, "chip_sum_" + k)
        return _ChipExchange([self.sums[k] for k in names])


def _full_weights(gathered):
    d = D_MODEL
    out = {}
    for k, a in gathered.items():
        if k in ("w_in", "w_ckv", "w_ff1"):
            out[k] = a.transpose(1, 0, 2).reshape(d, -1)
        else:
            out[k] = a.reshape(-1, d)
    return out


def _slabs(a):
    return a.reshape(N_DEV, -1, a.shape[-1])


def _local_step(x, mem, target, small, big, nb, s, tq=256, gather_rest=None, reduce=None):
    d = D_MODEL
    g_mix, g_v, w_sp, b_sp, g_head, g_cross, g_mem, g_ffn, g_fin = (
        small[k] for k in ("norm_mix_g", "gm_v_norm_g", "w_spatial", "b_spatial", "head_norm_g", "norm_cross_g",
                           "norm_mem_g", "norm_ffn_g", "norm_final_g"))
    tri = jnp.tril(jnp.ones((CHUNK, CHUNK), dtype=bool))
    w_sp_m = jnp.where(tri[None], w_sp, 0.0)
    wt = w_sp_m.astype(BF)
    wtt = jnp.swapaxes(w_sp_m, 1, 2).astype(BF)
    bb = jnp.broadcast_to(b_sp[:, :, None], (GM_GROUPS, CHUNK, CHUNK))
    hg_a = g_head[:, :GM_WIDTH]

    proj, xn = _norm_matmul(x, g_mix, big["w_in"], 512, "in_proj")
    merged = _gmlp_fwd(proj, g_v, wt, bb, hg_a, 512)
    o_sb, tot, merged, nblk, gathered = _sb_fwd(proj, merged, g_head, nb, s, tq, ride=gather_rest)
    if gather_rest is not None:
        big = dict(big, **_full_weights(dict(zip(BIG[1:], gathered))))
    w1c = big["w_ff1"].reshape(d, 4, 1024).transpose(1, 0, 2)
    w2c = big["w_ff2"].reshape(4, 1024, d)
    kv, memn = _norm_matmul(mem, g_mem, big["w_ckv"], 512, "mem_proj")
    h1, h2, hn, qc, oc = _mix_cross_fwd(x, merged, big["w_out"], g_cross, big["w_cq"], kv, big["w_co"], s, 512)
    hn2, f, dh3, d_fin, loss = _ffn_loss_fwd(h2, g_ffn, w1c, w2c, g_fin, target, 512)

    gbig = {}
    dh2, dpre, d_ffn = _ffn_bwd(dh3, f, h2, g_ffn, w1c, w2c, 512)
    gbig["w_ff2"] = _slabs(_wgrad(f, dh3, 1024, 1024, "wgrad_ff2", square_a=True))
    gbig["w_ff1"] = _slabs(_wgrad_wide(hn2, dpre, 512, 1024, "wgrad_ff1", col_shards=4))
    dh1, dqc, dkv, d_cross = _cross_bwd(dh2, h1, qc, g_cross, big["w_cq"], kv, big["w_co"], s, 512)
    gbig["w_co"] = _slabs(_wgrad(oc, dh2, 1024, 1024, "wgrad_co"))
    gbig["w_cq"] = _slabs(_wgrad(hn, dqc, 1024, 1024, "wgrad_cq"))
    gbig["w_ckv"] = _slabs(_wgrad(memn, dkv, 512, 1024, "wgrad_ckv", col_shards=4))
    d_mem = _mem_bwd(mem, g_mem, dkv, big["w_ckv"], 512)
    dmerged = _matmul_bt(dh1, big["w_out"], 512, "out_bwd")
    gbig["w_out"] = _slabs(_wgrad(merged, dh1, 1024, 1024, "wgrad_out"))
    rest = BIG[1:]
    ride = reduce.sibling([gbig[k] for k in rest]) if reduce else None
    dproj, d_wsp, d_bb, d_gv, d_hga, recv = _gmlp_bwd(proj, dmerged, g_v, wt, wtt, bb, hg_a, 512, ride=ride)
    ride = reduce.chip(rest, [gbig[k] for k in rest], recv) if reduce else None
    dproj, dk, dv, d_hgb, parts_rest = _sb_bwd(proj, o_sb, tot, nblk, dmerged, dproj, g_head, nb, s, tq, ride=ride)
    dproj = _place(_place(dproj, dk, 3, "place_dk"), dv, 4, "place_dv")
    gbig["w_in"] = _slabs(_wgrad_wide(xn, dproj, 512, 1024, "wgrad_in", col_shards=4))
    last = None
    if reduce:
        recv = _run_exchange(reduce.sibling([gbig["w_in"]]), "grad_sibling_exchange_w_in")
        last = reduce.chip(["w_in"], [gbig["w_in"]], recv)
    grad_x, d_mix, _ = _in_bwd(dproj, dh1, x, g_mix, big["w_in"], 512)
    parts = dict(zip(rest, parts_rest))

    gsmall = {
        "norm_mix_g": d_mix, "gm_v_norm_g": d_gv, "w_spatial": d_wsp, "b_spatial": d_bb[:, :, 0],
        "head_norm_g": jnp.concatenate([d_hga, jnp.sum(d_hgb, axis=0)], axis=1), "norm_cross_g": d_cross,
        "norm_mem_g": d_mem, "norm_ffn_g": d_ffn, "norm_final_g": d_fin,
    }
    return loss, grad_x, gsmall, gbig, parts, last


BIG = ("w_in", "w_out", "w_cq", "w_ckv", "w_co", "w_ff1", "w_ff2")
SMALL = ("norm_mix_g", "gm_v_norm_g", "w_spatial", "b_spatial", "head_norm_g", "norm_cross_g", "norm_mem_g",
         "norm_ffn_g", "norm_final_g")


def _local_copies_start(srcs, stages, sems):
    loads = [pltpu.make_async_copy(src, stage, sems.at[w]) for w, (src, stage) in enumerate(zip(srcs, stages))]
    for ld in loads:
        ld.start()
    return loads


def _local_copies_finish(loads, stages, dsts, sems):
    stores = []
    for w, (ld, stage, dst) in enumerate(zip(loads, stages, dsts)):
        ld.wait()
        st = pltpu.make_async_copy(stage, dst, sems.at[w])
        st.start()
        stores.append(st)
    for st in stores:
        st.wait()


def _chip_sum(slabs, recv, c_idx, name):
    _, r, cw = slabs.shape
    tr = min(r, 256)

    def body(c_ref, a_ref, b_ref, o_ref):
        del c_ref
        o_ref[...] = (a_ref[...] + b_ref[...]).astype(BF)

    return pl.pallas_call(
        body, name=name,
        grid_spec=pltpu.PrefetchScalarGridSpec(
            num_scalar_prefetch=1, grid=(N_CHIPS, r // tr),
            in_specs=[pl.BlockSpec((None, tr, cw), lambda p, i, c_ref: (2 * p + c_ref[0], i, 0)),
                      pl.BlockSpec((None, tr, cw), lambda p, i, c_ref: (p, i, 0))],
            out_specs=pl.BlockSpec((None, tr, cw), lambda p, i, c_ref: (p, i, 0))),
        out_shape=jax.ShapeDtypeStruct((N_CHIPS, r, cw), BF),
        compiler_params=_params(32, ("arbitrary", "arbitrary")),
    )(c_idx, *_in_hbm(slabs, recv))


def _sum4(sums, parts, q_idx, name):
    _, r, cw = parts.shape
    tr = min(r, 256)

    def body(q_ref, own_ref, a_ref, b_ref, c_ref, o_ref):
        del q_ref
        o_ref[...] = ((own_ref[...].astype(F32) + a_ref[...].astype(F32)) + b_ref[...].astype(F32)) + c_ref[
            ...].astype(F32)

    spec = lambda k: pl.BlockSpec((None, tr, cw), lambda i, q_ref: ((q_ref[0] + k) % N_CHIPS, i, 0))
    return pl.pallas_call(
        body, name=name,
        grid_spec=pltpu.PrefetchScalarGridSpec(
            num_scalar_prefetch=1, grid=(r // tr,), in_specs=[spec(0), spec(1), spec(2), spec(3)],
            out_specs=pl.BlockSpec((tr, cw), lambda i, q_ref: (i, 0))),
        out_shape=jax.ShapeDtypeStruct((r, cw), F32),
        compiler_params=_params(32, ("arbitrary",)),
    )(q_idx, *_in_hbm(sums, parts, parts, parts))


def _half_exchange(halves):
    n = len(halves)

    def body(*refs):
        ins, outs, stages = refs[:n], refs[n:2 * n], refs[2 * n:3 * n]
        send_sems, recv_sems, ld_sems, st_sems = refs[3 * n:]
        x, y, c = lax.axis_index("x"), lax.axis_index("y"), lax.axis_index("c")
        loads = _local_copies_start(ins, stages, ld_sems)
        copies = []
        for w in range(n):
            cp = pltpu.make_async_remote_copy(
                src_ref=ins[w], dst_ref=outs[w].at[c], send_sem=send_sems.at[w], recv_sem=recv_sems.at[w],
                device_id=(x, y, 1 - c), device_id_type=MESH)
            cp.start()
            copies.append(cp)
        _local_copies_finish(loads, stages, [outs[w].at[c] for w in range(n)], st_sems)
        for cp in copies:
            cp.wait()

    return pl.pallas_call(
        body, name="grad_half_exchange",
        in_specs=[ANY] * n, out_specs=[ANY] * n,
        out_shape=[jax.ShapeDtypeStruct((2,) + a.shape, a.dtype) for a in halves],
        scratch_shapes=[pltpu.VMEM(a.shape, a.dtype) for a in halves] + [
            pltpu.SemaphoreType.DMA((n,)), pltpu.SemaphoreType.DMA((n,)),
            pltpu.SemaphoreType.DMA((n,)), pltpu.SemaphoreType.DMA((n,))],
        compiler_params=_params(24),
    )(*halves)


def _small_all_reduce(packed, ride=None):
    rows = packed.shape[0]
    ride = ride or _NoExchange()
    ri, ro = len(ride.in_arrays), len(ride.out_shape)

    def body(*refs):
        in_ref, rins, out_ref, routs = refs[0], refs[1:1 + ri], refs[1 + ri], refs[2 + ri:2 + ri + ro]
        pair, chip_sum, chips, d2d_send, d2d_recv, ici_send, ici_recv = refs[2 + ri + ro:9 + ri + ro]
        rscr = refs[9 + ri + ro:]
        ride.start(rins, routs, rscr)
        x, y, c = lax.axis_index("x"), lax.axis_index("y"), lax.axis_index("c")
        q = 2 * x + y
        pair[c] = in_ref[...]
        swap = pltpu.make_async_remote_copy(
            src_ref=in_ref, dst_ref=pair.at[c], send_sem=d2d_send, recv_sem=d2d_recv,
            device_id=(x, y, 1 - c), device_id_type=MESH)
        swap.start()
        swap.wait()
        both = pair[0] + pair[1]
        chip_sum[...] = both
        chips[q] = both
        copies = [pltpu.make_async_remote_copy(
            src_ref=chip_sum, dst_ref=chips.at[q], send_sem=ici_send.at[k], recv_sem=ici_recv.at[k],
            device_id=(px, py, c), device_id_type=MESH) for k, (px, py) in enumerate(_other_chips(x, y))]
        for cp in copies:
            cp.start()
        for cp in copies:
            cp.wait()
        out_ref[...] = ((chips[0] + chips[1]) + chips[2]) + chips[3]
        ride.finish(rins, routs, rscr)

    vmem = pl.BlockSpec(memory_space=pltpu.VMEM)
    res = pl.pallas_call(
        body, name="small_all_reduce",
        in_specs=[vmem] + [ANY] * ri, out_specs=[vmem] + [ANY] * ro,
        out_shape=[jax.ShapeDtypeStruct(packed.shape, F32)] + list(ride.out_shape),
        scratch_shapes=[pltpu.VMEM((2, rows, 128), F32), pltpu.VMEM((rows, 128), F32),
                        pltpu.VMEM((N_CHIPS, rows, 128), F32), pltpu.SemaphoreType.DMA, pltpu.SemaphoreType.DMA,
                        pltpu.SemaphoreType.DMA((3,)), pltpu.SemaphoreType.DMA((3,))] + list(ride.scratch_shapes),
        compiler_params=_params(16),
    )(packed, *ride.in_arrays)
    return res[0], res[1:]


def _adamw(g, w, m, v, name):
    r, cw = g.shape
    tr = 256 if r % 256 == 0 else r

    def body(g_ref, w_ref, m_ref, v_ref, d_ref, nm_ref, nv_ref):
        gv = g_ref[...]
        nm = ADAM_B1 * m_ref[...] + (1.0 - ADAM_B1) * gv
        nv = ADAM_B2 * v_ref[...] + (1.0 - ADAM_B2) * (gv * gv)
        m_hat = nm / (1.0 - ADAM_B1 ** ADAM_STEP)
        v_hat = nv / (1.0 - ADAM_B2 ** ADAM_STEP)
        d_ref[...] = -ADAM_LR * (m_hat / (jnp.sqrt(v_hat) + ADAM_EPS) + ADAM_WD * w_ref[...])
        nm_ref[...] = nm
        nv_ref[...] = nv

    spec = pl.BlockSpec((tr, cw), lambda i: (i, 0))
    return pl.pallas_call(
        body, name=name, grid=(r // tr,),
        in_specs=[spec] * 4, out_specs=[spec] * 3,
        out_shape=[jax.ShapeDtypeStruct((r, cw), F32)] * 3,
        compiler_params=_params(32, ("arbitrary",)),
    )(*_in_hbm(g, w, m, v))


def _small_params(args):
    small = {k: args[k].reshape(1, -1) for k in SMALL}
    small["w_spatial"] = args["w_spatial"][0]
    small["b_spatial"] = args["b_spatial"][0]
    return small


def _pack(parts, rows):
    flat = jnp.concatenate([p.reshape(-1).astype(F32) for p in parts])
    return jnp.pad(flat, (0, rows * 128 - flat.shape[0])).reshape(rows, 128)


def _unpack(packed, shapes):
    flat = packed.reshape(-1)
    out, off = [], 0
    for shp in shapes:
        size = math.prod(shp)
        out.append(flat[off:off + size].reshape(shp))
        off += size
    return out


def kernel(x, mem, norm_mix_g, w_in, gm_v_norm_g, w_spatial, b_spatial, head_norm_g, w_out, norm_cross_g, norm_mem_g, w_cq, w_ckv, w_co, norm_ffn_g, w_ff1, w_ff2, norm_final_g, loss_target, m_norm_mix_g, m_w_in, m_gm_v_norm_g, m_w_spatial, m_b_spatial, m_head_norm_g, m_w_out, m_norm_cross_g, m_norm_mem_g, m_w_cq, m_w_ckv, m_w_co, m_norm_ffn_g, m_w_ff1, m_w_ff2, m_norm_final_g, v_norm_mix_g, v_w_in, v_gm_v_norm_g, v_w_spatial, v_b_spatial, v_head_norm_g, v_w_out, v_norm_cross_g, v_norm_mem_g, v_w_cq, v_w_ckv, v_w_co, v_norm_ffn_g, v_w_ff1, v_w_ff2, v_norm_final_g):
    args = dict(locals())
    d = D_MODEL
    nb, s, _ = x.shape
    c_idx = lax.axis_index("c").astype(jnp.int32).reshape(1)
    q_idx = (2 * lax.axis_index("x") + lax.axis_index("y")).astype(jnp.int32).reshape(1)
    rest = BIG[1:]

    shards = {k: args[k][0].astype(BF) for k in BIG}
    big = _full_weights({"w_in": _run_exchange(_GatherExchange([shards["w_in"]]), "all_gather_w_in")[0]})
    gather_rest = _GatherExchange([shards[k] for k in rest])

    reduce = _GradReduce(c_idx)
    loss, grad_x, gsmall, _, parts, last = _local_step(
        x.reshape(nb * s, d), mem.reshape(nb * N_MEM, d), loss_target.reshape(nb * s, d), _small_params(args), big,
        nb, s, gather_rest=gather_rest, reduce=reduce)

    shapes = [args[k].shape for k in SMALL]
    n_small = sum(math.prod(sh) for sh in shapes)
    rows = -(-(n_small + 1) // 1024) * 8
    reduced, (parts["w_in"],) = _small_all_reduce(_pack([gsmall[k] for k in SMALL] + [loss], rows), ride=last)
    halves = [_sum4(reduce.sums[k], parts[k], q_idx, "sum4_" + k) for k in BIG]
    both = _half_exchange(halves)

    out = {"grad_x": grad_x.reshape(nb, s, d)}
    for k, g2 in zip(BIG, both):
        shp = args[k].shape
        g = g2.reshape(shp[1], shp[2])
        dl, nm, nv = _adamw(g, args[k][0], args["m_" + k][0], args["v_" + k][0], "adamw_" + k)
        out["grad_" + k], out["delta_" + k], out["new_m_" + k], out["new_v_" + k] = (
            a.reshape(shp) for a in (g, dl, nm, nv))

    dl, nm, nv = _adamw(reduced, _pack([args[k] for k in SMALL], rows), _pack([args["m_" + k] for k in SMALL], rows),
                        _pack([args["v_" + k] for k in SMALL], rows), "adamw_small")
    for name, arr in (("grad_", reduced), ("delta_", dl), ("new_m_", nm), ("new_v_", nv)):
        for k, a in zip(SMALL, _unpack(arr, shapes)):
            out[name + k] = a
    out["loss"] = reduced.reshape(-1)[n_small]

    names = ["norm_mix_g", "w_in", "gm_v_norm_g", "w_spatial", "b_spatial", "head_norm_g", "w_out", "norm_cross_g",
             "norm_mem_g", "w_cq", "w_ckv", "w_co", "norm_ffn_g", "w_ff1", "w_ff2", "norm_final_g"]
    return (out["loss"], out["grad_x"], *[out["grad_" + k] for k in names], *[out["delta_" + k] for k in names],
            *[out["new_m_" + k] for k in names], *[out["new_v_" + k] for k in names])
```

```python
import functools
import math

import jax
import jax.numpy as jnp
from jax import lax
from jax.experimental import pallas as pl
from jax.experimental.pallas import tpu as pltpu

F32 = jnp.float32
BF = jnp.bfloat16

EPS = 1e-6
D_MODEL = 1024
CHUNK = 128
GM_GROUPS = 4
GM_WIDTH = 512
SB_WIDTH = 512
HEAD_LANES = 64
SB_SCALE = 0.125
SB_SKIP = -104.0
X_HEADS = 4
X_HEAD_DIM = 256
N_MEM = 256
D_FF = 4096
IN_COLS = 2560
N_CHIPS = 4
N_DEV = 8

ADAM_LR = 0.001
ADAM_B1 = 0.9
ADAM_B2 = 0.999
ADAM_EPS = 1e-08
ADAM_WD = 0.01
ADAM_STEP = 10

V7X_VMEM_BYTES = 64 * 1024 * 1024
MESH = pl.DeviceIdType.MESH
ANY = pl.BlockSpec(memory_space=pl.ANY)

GELU_C = math.sqrt(2.0 / math.pi)
GELU_A = 0.044715


def _params(vmem_mb, sem=None):
    assert vmem_mb * 1024 * 1024 <= V7X_VMEM_BYTES
    return pltpu.CompilerParams(vmem_limit_bytes=vmem_mb * 1024 * 1024, dimension_semantics=sem)


PIN_MIN_ELEMENTS = 1 << 18


def _in_hbm(*arrays):
    return tuple(pltpu.with_memory_space_constraint(a, pltpu.HBM) if a.size >= PIN_MIN_ELEMENTS else a
                 for a in arrays)


def _dot(a, b):
    return jnp.dot(a, b, preferred_element_type=F32)


def _dot_bt(a, b):
    return lax.dot_general(a, b, (((1,), (1,)), ((), ())), preferred_element_type=F32)


def _dot_at(a, b):
    return lax.dot_general(a, b, (((0,), (0,)), ((), ())), preferred_element_type=F32)


def _gelu(x):
    t = jnp.tanh(GELU_C * (x + GELU_A * x * x * x))
    return 0.5 * x * (1.0 + t)


def _gelu_and_grad(x):
    x2 = x * x
    t = jnp.tanh(GELU_C * (x + GELU_A * x2 * x))
    h = 0.5 * (1.0 + t)
    return x * h, h + 0.5 * x * (1.0 - t * t) * (GELU_C * (1.0 + 3.0 * GELU_A * x2))


def _rs(x):
    return lax.rsqrt(jnp.mean(x * x, axis=-1, keepdims=True) + EPS)


def _rms_bwd(dxn, xhat, r, g):
    dxh = dxn * g
    dx = r * (dxh - xhat * jnp.mean(dxh * xhat, axis=-1, keepdims=True))
    return dx, dxn * xhat


def _norm_matmul(x, g, w, tm, name):
    t, d = x.shape
    n = w.shape[1]
    tm = min(tm, t)

    def body(x_ref, g_ref, w_ref, out_ref, xn_ref):
        xv = x_ref[...]
        xn = (xv * _rs(xv) * g_ref[...]).astype(BF)
        xn_ref[...] = xn
        out_ref[...] = _dot(xn, w_ref[...]).astype(out_ref.dtype)

    return pl.pallas_call(
        body, name=name, grid=(t // tm,),
        in_specs=[pl.BlockSpec((tm, d), lambda i: (i, 0)), pl.BlockSpec((1, d), lambda i: (0, 0)),
                  pl.BlockSpec((d, n), lambda i: (0, 0))],
        out_specs=[pl.BlockSpec((tm, n), lambda i: (i, 0)), pl.BlockSpec((tm, d), lambda i: (i, 0))],
        out_shape=[jax.ShapeDtypeStruct((t, n), BF), jax.ShapeDtypeStruct((t, d), BF)],
        compiler_params=_params(48, ("arbitrary",)),
    )(*_in_hbm(x, g, w))


def _wgrad(a, g, tn, tk, name, square_a=False, col_shards=1, ride=None):
    t, m = a.shape
    n = g.shape[1]
    tk = min(tk, t)
    tm = min(m, 1024)
    ns = n // col_shards
    assert ns % tn == 0 and m % tm == 0
    per = ns // tn
    nk = t // tk

    def body(a_ref, g_ref, o_ref):
        k = pl.program_id(2)

        @pl.when(k == 0)
        def _():
            o_ref[...] = jnp.zeros_like(o_ref)

        av = a_ref[...]
        if square_a:
            af = av.astype(F32)
            av = af * af
        o_ref[...] += _dot_at(av.astype(BF), g_ref[...].astype(BF))

    (out,), rode = _ride_call(
        body, name, (m // tm, n // tn, nk),
        in_specs=[pl.BlockSpec((tk, tm), lambda i, j, k: (k, i)), pl.BlockSpec((tk, tn), lambda i, j, k: (k, j))],
        out_specs=[pl.BlockSpec((None, tm, tn), lambda i, j, k: (j // per, i, j % per))],
        out_shape=[jax.ShapeDtypeStruct((col_shards, m, ns), F32)],
        scratch_shapes=[], operands=(a, g), vmem_mb=48, ride=ride)
    return out if ride is None else (out, rode)


def _wgrad_wide(a, g, tm, tk, name, col_shards):
    t, m = a.shape
    n = g.shape[1]
    tk = min(tk, t)
    tm = min(tm, m)
    ns = n // col_shards

    def body(a_ref, g_ref, o_ref):
        @pl.when(pl.program_id(1) == 0)
        def _():
            o_ref[...] = jnp.zeros_like(o_ref)

        a_t = a_ref[...].astype(BF).T
        for p in range(col_shards):
            o_ref[p] += _dot(a_t, g_ref[:, p * ns:(p + 1) * ns].astype(BF))

    return pl.pallas_call(
        body, name=name, grid=(m // tm, t // tk),
        in_specs=[pl.BlockSpec((tk, tm), lambda i, k: (k, i)), pl.BlockSpec((tk, n), lambda i, k: (k, 0))],
        out_specs=pl.BlockSpec((col_shards, tm, ns), lambda i, k: (0, i, 0)),
        out_shape=jax.ShapeDtypeStruct((col_shards, m, ns), F32),
        compiler_params=_params(48, ("arbitrary", "arbitrary")),
    )(*_in_hbm(a, g))


def _matmul_bt(a, w, tm, name):
    t, n = a.shape
    k = w.shape[0]
    tm = min(tm, t)

    def body(a_ref, w_ref, o_ref):
        o_ref[...] = _dot_bt(a_ref[...].astype(BF), w_ref[...]).astype(o_ref.dtype)

    return pl.pallas_call(
        body, name=name, grid=(t // tm,),
        in_specs=[pl.BlockSpec((tm, n), lambda i: (i, 0)), pl.BlockSpec((k, n), lambda i: (0, 0))],
        out_specs=pl.BlockSpec((tm, k), lambda i: (i, 0)),
        out_shape=jax.ShapeDtypeStruct((t, k), BF),
        compiler_params=_params(32, ("arbitrary",)),
    )(*_in_hbm(a, w))


def _gmlp_fwd(proj, gg, wt, bb, hg, tm):
    t = proj.shape[0]
    tm = min(tm, t)

    def body(u_ref, v_ref, gg_ref, wt_ref, bb_ref, hg_ref, out_ref):
        for cc in range(tm // CHUNK):
            rows = slice(cc * CHUNK, (cc + 1) * CHUNK)
            for g in range(GM_GROUPS):
                cols = slice(g * 128, (g + 1) * 128)
                u = _gelu(u_ref[rows, cols].astype(F32))
                gv = _gelu(v_ref[rows, cols].astype(F32))
                vn = gv * _rs(gv) * gg_ref[:, cols]
                mixed = _dot(wt_ref[g], vn.astype(BF)) + bb_ref[g]
                a = u * mixed
                out_ref[rows, cols] = (a * _rs(a) * hg_ref[:, cols]).astype(BF)

    return pl.pallas_call(
        body, name="gmlp_fwd", grid=(t // tm,),
        in_specs=[pl.BlockSpec((tm, 512), lambda i: (i, 0)), pl.BlockSpec((tm, 512), lambda i: (i, 1)),
                  pl.BlockSpec((1, 512), lambda i: (0, 0)), pl.BlockSpec((4, 128, 128), lambda i: (0, 0, 0)),
                  pl.BlockSpec((4, 128, 128), lambda i: (0, 0, 0)), pl.BlockSpec((1, 512), lambda i: (0, 0))],
        out_specs=pl.BlockSpec((tm, 512), lambda i: (i, 0)),
        out_shape=jax.ShapeDtypeStruct((t, 1024), BF),
        compiler_params=_params(32, ("arbitrary",)),
    )(*_in_hbm(proj, proj, gg, wt, bb, hg))


def _gmlp_bwd(proj, dmerged, gg, wt, wtt, bb, hg, tm, ride=None):
    t = proj.shape[0]
    tm = min(tm, t)
    nsteps = t // tm

    def body(u_ref, v_ref, dm_ref, gg_ref, wt_ref, wtt_ref, bb_ref, hg_ref,
             dp_ref, dw_ref, db_ref, dgg_ref, dhg_ref):
        i = pl.program_id(0)

        @pl.when(i == 0)
        def _():
            dw_ref[...] = jnp.zeros_like(dw_ref)
            db_ref[...] = jnp.zeros_like(db_ref)
            dgg_ref[...] = jnp.zeros_like(dgg_ref)
            dhg_ref[...] = jnp.zeros_like(dhg_ref)

        for cc in range(tm // CHUNK):
            rows = slice(cc * CHUNK, (cc + 1) * CHUNK)
            for g in range(GM_GROUPS):
                cols = slice(g * 128, (g + 1) * 128)
                up = u_ref[rows, cols].astype(F32)
                gp = v_ref[rows, cols].astype(F32)
                u, u_grad = _gelu_and_grad(up)
                gv, gv_grad = _gelu_and_grad(gp)
                rv = _rs(gv)
                gvh = gv * rv
                ggv = gg_ref[:, cols]
                vnb = (gvh * ggv).astype(BF)
                mixed = _dot(wt_ref[g], vnb) + bb_ref[g]
                a = u * mixed
                ra = _rs(a)
                ah = a * ra
                dm = dm_ref[rows, cols].astype(F32)
                dhg_ref[:, cols] += jnp.sum(dm * ah, axis=0, keepdims=True)
                dah = dm * hg_ref[:, cols]
                da = ra * (dah - ah * jnp.mean(dah * ah, axis=-1, keepdims=True))
                du = da * mixed
                dmix = da * u
                db_ref[g] += dmix
                dmb = dmix.astype(BF)
                dw_ref[g] += _dot_bt(dmb, vnb)
                dvn = _dot(wtt_ref[g], dmb)
                dgg_ref[:, cols] += jnp.sum(dvn * gvh, axis=0, keepdims=True)
                dgh = dvn * ggv
                dgv = rv * (dgh - gvh * jnp.mean(dgh * gvh, axis=-1, keepdims=True))
                dp_ref[rows, cols] = (du * u_grad).astype(BF)
                dp_ref[rows, 512 + g * 128:512 + (g + 1) * 128] = (dgv * gv_grad).astype(BF)

        @pl.when(i == nsteps - 1)
        def _():
            r = lax.broadcasted_iota(jnp.int32, (CHUNK, CHUNK), 0)
            c = lax.broadcasted_iota(jnp.int32, (CHUNK, CHUNK), 1)
            for g in range(GM_GROUPS):
                dw_ref[g] = jnp.where(c <= r, dw_ref[g], 0.0)
                db_ref[g] = jnp.broadcast_to(jnp.sum(db_ref[g], axis=-1, keepdims=True), (CHUNK, CHUNK))

    small = lambda shape: pl.BlockSpec(shape, lambda i: (0,) * len(shape))
    res, rode = _ride_call(
        body, "gmlp_bwd", (nsteps,),
        in_specs=[pl.BlockSpec((tm, 512), lambda i: (i, 0)), pl.BlockSpec((tm, 512), lambda i: (i, 1)),
                  pl.BlockSpec((tm, 512), lambda i: (i, 0)), small((1, 512)), small((4, 128, 128)),
                  small((4, 128, 128)), small((4, 128, 128)), small((1, 512))],
        out_specs=[pl.BlockSpec((tm, 1024), lambda i: (i, 0)), small((4, 128, 128)), small((4, 128, 128)),
                   small((1, 512)), small((1, 512))],
        out_shape=[jax.ShapeDtypeStruct((t, IN_COLS), BF), jax.ShapeDtypeStruct((4, 128, 128), F32),
                   jax.ShapeDtypeStruct((4, 128, 128), F32), jax.ShapeDtypeStruct((1, 512), F32),
                   jax.ShapeDtypeStruct((1, 512), F32)],
        scratch_shapes=[], operands=(proj, proj, dmerged, gg, wt, wtt, bb, hg), vmem_mb=32, ride=ride)
    return (*res, rode)


def _other_chips(x, y):
    return ((1 - x, y), (x, 1 - y), (1 - x, 1 - y))


class _GatherExchange:
    def __init__(self, shards):
        n = len(shards)
        self.n = n
        self.in_arrays = list(shards)
        self.out_shape = [jax.ShapeDtypeStruct((N_CHIPS,) + a.shape, a.dtype) for a in shards]
        self.half_rows = [a.shape[0] // 2 for a in shards]
        sems = lambda k: pltpu.SemaphoreType.DMA((k,))
        self.scratch_shapes = [pltpu.VMEM(a.shape, a.dtype) for a in shards] + [
            sems(3 * n), sems(3 * n), sems(3 * n), sems(3 * n), sems(n), sems(n)]

    def _copies(self, ins, outs, scr):
        n = self.n
        stages, (ici_send, ici_recv, d2d_send, d2d_recv, ld_sems, st_sems) = scr[:n], scr[n:]
        x, y, c = lax.axis_index("x"), lax.axis_index("y"), lax.axis_index("c")
        q = 2 * x + y
        loads = [pltpu.make_async_copy(ins[w], stages[w], ld_sems.at[w]) for w in range(n)]
        stores = [pltpu.make_async_copy(stages[w], outs[w].at[q], st_sems.at[w]) for w in range(n)]
        ici, d2d = [], []
        for w in range(n):
            half = pl.ds(c * self.half_rows[w], self.half_rows[w])
            for k, (px, py) in enumerate(_other_chips(x, y)):
                ici.append(pltpu.make_async_remote_copy(
                    src_ref=ins[w].at[half], dst_ref=outs[w].at[q, half], send_sem=ici_send.at[3 * w + k],
                    recv_sem=ici_recv.at[3 * w + k], device_id=(px, py, c), device_id_type=MESH))
                landed = outs[w].at[2 * px + py, half]
                d2d.append(pltpu.make_async_remote_copy(
                    src_ref=landed, dst_ref=landed, send_sem=d2d_send.at[3 * w + k],
                    recv_sem=d2d_recv.at[3 * w + k], device_id=(x, y, 1 - c), device_id_type=MESH))
        return loads, stores, ici, d2d

    def start(self, ins, outs, scr):
        loads, stores, ici, _ = self._copies(ins, outs, scr)
        for cp in loads + ici:
            cp.start()
        for ld, st in zip(loads, stores):
            ld.wait()
            st.start()

    def relay(self, ins, outs, scr):
        _, _, ici, d2d = self._copies(ins, outs, scr)
        for got, fwd in zip(ici, d2d):
            got.wait_recv()
            fwd.start()

    def finish(self, ins, outs, scr):
        _, stores, ici, d2d = self._copies(ins, outs, scr)
        for cp in ici:
            cp.wait_send()
        for cp in d2d + stores:
            cp.wait()


class _SiblingExchange:
    def __init__(self, slabs):
        n = len(slabs)
        self.n = n
        self.in_arrays = list(slabs)
        self.out_shape = [jax.ShapeDtypeStruct((N_CHIPS,) + a.shape[1:], a.dtype) for a in slabs]
        self.scratch_shapes = [pltpu.SemaphoreType.DMA((4 * n,)), pltpu.SemaphoreType.DMA((4 * n,))]

    def _copies(self, ins, outs, scr):
        send_sems, recv_sems = scr
        x, y, c = lax.axis_index("x"), lax.axis_index("y"), lax.axis_index("c")
        return [pltpu.make_async_remote_copy(
            src_ref=ins[w].at[2 * p + (1 - c)], dst_ref=outs[w].at[p], send_sem=send_sems.at[4 * w + p],
            recv_sem=recv_sems.at[4 * w + p], device_id=(x, y, 1 - c), device_id_type=MESH)
            for w in range(self.n) for p in range(N_CHIPS)]

    def start(self, ins, outs, scr):
        for cp in self._copies(ins, outs, scr):
            cp.start()

    def finish(self, ins, outs, scr):
        for cp in self._copies(ins, outs, scr):
            cp.wait()


class _ChipExchange:
    def __init__(self, sums):
        n = len(sums)
        self.n = n
        self.in_arrays = list(sums)
        self.out_shape = [jax.ShapeDtypeStruct(a.shape, a.dtype) for a in sums]
        self.scratch_shapes = [pltpu.SemaphoreType.DMA((3 * n,)), pltpu.SemaphoreType.DMA((3 * n,))]

    def _copies(self, ins, outs, scr):
        send_sems, recv_sems = scr
        x, y, c = lax.axis_index("x"), lax.axis_index("y"), lax.axis_index("c")
        q = 2 * x + y
        return [pltpu.make_async_remote_copy(
            src_ref=ins[w].at[2 * px + py], dst_ref=outs[w].at[q], send_sem=send_sems.at[3 * w + k],
            recv_sem=recv_sems.at[3 * w + k], device_id=(px, py, c), device_id_type=MESH)
            for w in range(self.n) for k, (px, py) in enumerate(_other_chips(x, y))]

    def start(self, ins, outs, scr):
        for cp in self._copies(ins, outs, scr):
            cp.start()

    def finish(self, ins, outs, scr):
        for cp in self._copies(ins, outs, scr):
            cp.wait()


class _NoExchange:
    in_arrays, out_shape, scratch_shapes = (), (), ()

    def start(self, ins, outs, scr):
        pass

    def finish(self, ins, outs, scr):
        pass


def _run_exchange(ex, name):
    n_in, n_out = len(ex.in_arrays), len(ex.out_shape)

    def body(*refs):
        ins, outs, scr = refs[:n_in], refs[n_in:n_in + n_out], refs[n_in + n_out:]
        ex.start(ins, outs, scr)
        if hasattr(ex, "relay"):
            ex.relay(ins, outs, scr)
        ex.finish(ins, outs, scr)

    return pl.pallas_call(
        body, name=name, in_specs=[ANY] * n_in, out_specs=[ANY] * n_out, out_shape=ex.out_shape,
        scratch_shapes=ex.scratch_shapes, compiler_params=_params(24),
    )(*ex.in_arrays)


def _ride_call(body, name, grid, in_specs, out_specs, out_shape, scratch_shapes, operands, vmem_mb, ride=None,
               aliases=None):
    ride = ride or _NoExchange()
    ni, no, ns = len(in_specs), len(out_specs), len(scratch_shapes)
    ri, ro = len(ride.in_arrays), len(ride.out_shape)
    total = math.prod(grid)

    def wrapped(*refs):
        ins, rins = refs[:ni], refs[ni:ni + ri]
        outs, routs = refs[ni + ri:ni + ri + no], refs[ni + ri + no:ni + ri + no + ro]
        scr, rscr = refs[ni + ri + no + ro:ni + ri + no + ro + ns], refs[ni + ri + no + ro + ns:]
        step = pl.program_id(0)
        for ax in range(1, len(grid)):
            step = step * grid[ax] + pl.program_id(ax)

        @pl.when(step == 0)
        def _():
            ride.start(rins, routs, rscr)

        if hasattr(ride, "relay"):
            @pl.when(step == (3 * total) // 4)
            def _():
                ride.relay(rins, routs, rscr)

        body(*ins, *outs, *scr)

        @pl.when(step == total - 1)
        def _():
            ride.finish(rins, routs, rscr)

    res = pl.pallas_call(
        wrapped, name=name, grid=grid, in_specs=list(in_specs) + [ANY] * ri, out_specs=list(out_specs) + [ANY] * ro,
        out_shape=list(out_shape) + list(ride.out_shape),
        scratch_shapes=list(scratch_shapes) + list(ride.scratch_shapes), input_output_aliases=aliases or {},
        compiler_params=_params(vmem_mb, ("arbitrary",) * len(grid)),
    )(*_in_hbm(*operands), *ride.in_arrays)
    return res[:no], res[no:]


def _neg_log_sig(z):
    n = jnp.maximum(z, 0.0) + jnp.log(1.0 + jnp.exp(-jnp.abs(z)))
    return n, z - n


def _running_sums(n, tri2):
    hi = n.astype(BF)
    lo = (n - hi.astype(F32)).astype(BF)
    return _dot(jnp.concatenate([hi, lo], axis=1), tri2)


def _head_sums(x, h0):
    s0 = jnp.sum(jnp.where(h0, x, 0.0), axis=-1, keepdims=True)
    s1 = jnp.sum(jnp.where(h0, 0.0, x), axis=-1, keepdims=True)
    return jnp.where(h0, s0, s1)


SB_BLOCKS_PER_STEP = 4
SB_PAIRS_PER_STEP = 2


def _sb_masks(tq):
    h0 = lax.broadcasted_iota(jnp.int32, (tq, 128), 1) < HEAD_LANES
    r = lax.broadcasted_iota(jnp.int32, (2 * tq, tq), 0)
    c = lax.broadcasted_iota(jnp.int32, (2 * tq, tq), 1)
    return h0, c < jnp.where(r >= tq, r - tq, r)


def _sb_stack(x, h0):
    zero = jnp.zeros_like(x)
    return jnp.concatenate([jnp.where(h0, x, zero), jnp.where(h0, zero, x)], axis=0)


def _tri(tq, op):
    return op(lax.broadcasted_iota(jnp.int32, (tq, tq), 0), lax.broadcasted_iota(jnp.int32, (tq, tq), 1)).astype(BF)


def _sb_fwd(proj, merged, hg, nb, s, tq, ride=None):
    t = nb * s
    tq = min(tq, s)
    nq = s // tq
    per = min(SB_BLOCKS_PER_STEP, nq)
    ns = nq // per
    gp, ng, w = SB_PAIRS_PER_STEP, 4 // SB_PAIRS_PER_STEP, 128 * SB_PAIRS_PER_STEP

    def body(q_ref, k_ref, v_ref, hg_ref, merged_ref, o_ref, tot_ref, mb_ref, nblk_ref, acc, cr, c_min):
        del merged_ref
        h0, causal = _sb_masks(tq)
        tri_gt = _tri(tq, lambda r, c: r > c)
        tri_gt = jnp.concatenate([tri_gt, tri_gt], axis=0)
        lanes = [slice(g * 128, (g + 1) * 128) for g in range(gp)]
        zeros = jnp.zeros((2 * tq, 1), F32)

        def query_block(i, rows):
            qsts = [_sb_stack(q_ref[rows, lanes[g]] * SB_SCALE, h0) for g in range(gp)]

            def block(g, j, masked, c_in):
                start = pl.multiple_of(j * tq, tq)
                kj = k_ref[pl.ds(start, tq), lanes[g]]
                vj = v_ref[pl.ds(start, tq), lanes[g]]
                n, l = _neg_log_sig(_dot_bt(qsts[g], kj))
                if masked:
                    n = jnp.where(causal, n, 0.0)
                a = jnp.exp(l - (_running_sums(n, tri_gt) + c_in))
                if masked:
                    a = jnp.where(causal, a, 0.0)
                return _dot(a.astype(BF), vj), c_in + jnp.sum(n, axis=-1, keepdims=True)

            def keep(parts):
                for g, (p, c) in enumerate(parts):
                    acc[g] = p
                    cr[g] = c
                c_min[0] = jnp.min(functools.reduce(jnp.minimum, [c for _, c in parts]))

            @pl.when(i == 0)
            def _():
                keep([block(g, 0, True, zeros) for g in range(gp)])

            @pl.when(i > 0)
            def _():
                diag = [block(g, i, True, zeros) for g in range(gp)]
                prev = [block(g, i - 1, False, diag[g][1]) for g in range(gp)]
                keep([(diag[g][0] + prev[g][0], prev[g][1]) for g in range(gp)])

            def cond(carry):
                return jnp.logical_and(carry[0] < i, carry[1] < -SB_SKIP)

            def step(carry):
                more = [block(g, i - 1 - carry[0], False, cr[g]) for g in range(gp)]
                for g, (p, c) in enumerate(more):
                    acc[g] += p
                    cr[g] = c
                return carry[0] + 1, jnp.min(functools.reduce(jnp.minimum, [c for _, c in more]))

            walked, _ = lax.while_loop(cond, step, (jnp.minimum(i, 1), c_min[0]))
            return walked

        for u in range(per):
            rows = slice(u * tq, (u + 1) * tq)
            walked = query_block(pl.program_id(2) * per + u, rows)
            for g in range(gp):
                o = jnp.where(h0, acc[g, 0:tq, :], acc[g, tq:2 * tq, :])
                o_ref[rows, lanes[g]] = o
                tot_ref[rows, lanes[g]] = jnp.where(h0, cr[g, 0:tq, :], cr[g, tq:2 * tq, :])
                ro = lax.rsqrt(_head_sums(o * o, h0) * (1.0 / HEAD_LANES) + EPS)
                mb_ref[rows, lanes[g]] = (o * ro * hg_ref[:, lanes[g]]).astype(BF)
            nblk_ref[u * 8:(u + 1) * 8, :] = jnp.full((8, 128), walked.astype(F32))

    blk = lambda col0: pl.BlockSpec((per * tq, w), lambda b, hg_, i: (b * ns + i, col0 + hg_))
    seq = lambda col0: pl.BlockSpec((s, w), lambda b, hg_, i: (b, col0 + hg_))
    first = 1024 // w
    (o, tot, mb, nblk), rode = _ride_call(
        body, "sb_fwd", (nb, ng, ns),
        in_specs=[blk(first), seq(first + ng), seq(first + 2 * ng),
                  pl.BlockSpec((1, w), lambda b, hg_, i: (0, ng + hg_)), ANY],
        out_specs=[blk(0), blk(0), blk(ng),
                   pl.BlockSpec((None, None, per * 8, 128), lambda b, hg_, i: (b, hg_, i, 0))],
        out_shape=[jax.ShapeDtypeStruct((t, 512), F32), jax.ShapeDtypeStruct((t, 512), F32),
                   jax.ShapeDtypeStruct((t, 1024), BF), jax.ShapeDtypeStruct((nb, ng, nq * 8, 128), F32)],
        scratch_shapes=[pltpu.VMEM((gp, 2 * tq, 128), F32), pltpu.VMEM((gp, 2 * tq, 1), F32),
                        pltpu.SMEM((1,), F32)],
        operands=(proj, proj, proj, hg, merged), vmem_mb=40, ride=ride, aliases={4: 2})
    return o, tot, mb, nblk, rode


def _sb_bwd(proj, o_sb, tot, nblk, dmerged, dproj, hg, nb, s, tq, ride=None):
    t = nb * s
    tq = min(tq, s)
    nq = s // tq
    per = min(SB_BLOCKS_PER_STEP, nq)
    ns = nq // per
    gp, ng, w = SB_PAIRS_PER_STEP, 4 // SB_PAIRS_PER_STEP, 128 * SB_PAIRS_PER_STEP

    def body(q_ref, k_ref, v_ref, o_ref, tot_ref, nblk_ref, dm_ref, hg_ref, dproj_ref,
             dq_ref, dk_ref, dv_ref, dhg_ref, dk_acc, dv_acc, dq_acc, cm, cg):
        del dproj_ref
        h0, causal = _sb_masks(tq)
        tri_le = _tri(tq, lambda r, c: r <= c)
        tri_le = jnp.concatenate([tri_le, tri_le], axis=0)
        tri_lt = _tri(tq, lambda r, c: r < c)
        lanes = [slice(g * 128, (g + 1) * 128) for g in range(gp)]

        @pl.when(pl.program_id(2) == 0)
        def _():
            dk_acc[...] = jnp.zeros_like(dk_acc)
            dv_acc[...] = jnp.zeros_like(dv_acc)
            dhg_ref[...] = jnp.zeros_like(dhg_ref)

        def query_block(i, rows):
            for ref in (dq_acc, cm, cg):
                ref[...] = jnp.zeros_like(ref)
            qsts, dosts, tots = [], [], []
            for g in range(gp):
                qsts.append(_sb_stack(q_ref[rows, lanes[g]] * SB_SCALE, h0))
                o = o_ref[rows, lanes[g]]
                ro = lax.rsqrt(_head_sums(o * o, h0) * (1.0 / HEAD_LANES) + EPS)
                oh = o * ro
                dm = dm_ref[rows, lanes[g]].astype(F32)
                dhg_ref[:, lanes[g]] += jnp.sum(dm * oh, axis=0, keepdims=True)
                doh = dm * hg_ref[:, lanes[g]]
                do = ro * (doh - oh * (_head_sums(doh * oh, h0) * (1.0 / HEAD_LANES)))
                dosts.append(_sb_stack(do.astype(BF), h0))
                first = g * 128
                tots.append(jnp.concatenate(
                    [tot_ref[rows, first:first + 1], tot_ref[rows, first + HEAD_LANES:first + HEAD_LANES + 1]], axis=0))
            qsts_t = [q.T for q in qsts]
            dosts_t = [d.T for d in dosts]

            def block(g, j, masked, cm_in, cg_in):
                start = pl.multiple_of(j * tq, tq)
                kj = k_ref[pl.ds(start, tq), lanes[g]]
                vj = v_ref[pl.ds(start, tq), lanes[g]]
                n, l = _neg_log_sig(_dot_bt(qsts[g], kj))
                if masked:
                    n = jnp.where(causal, n, 0.0)
                a = jnp.exp(l - (tots[g] - cm_in - _running_sums(n, tri_le)))
                if masked:
                    a = jnp.where(causal, a, 0.0)
                gm = a * _dot_bt(dosts[g], vj)
                pp = cg_in + _dot(gm.astype(BF), tri_lt)
                dz = gm - jnp.exp(l) * (gm + pp)
                if masked:
                    dz = jnp.where(causal, dz, 0.0)
                dzb = dz.astype(BF)
                dk_acc[g, :, pl.ds(start, tq)] += _dot(qsts_t[g], dzb)
                dv_acc[g, :, pl.ds(start, tq)] += _dot(dosts_t[g], a.astype(BF))
                return (_dot(dzb, kj), cm_in + jnp.sum(n, axis=-1, keepdims=True),
                        cg_in + jnp.sum(gm, axis=-1, keepdims=True))

            def step(j, carry):
                for g in range(gp):
                    dq, cm[g], cg[g] = block(g, j, False, cm[g], cg[g])
                    dq_acc[g] += dq
                return carry

            walked = jnp.clip(nblk_ref[pl.program_id(0), pl.program_id(1), i].astype(jnp.int32),
                              jnp.minimum(i, 1), i)
            lax.fori_loop(i - walked, i - 1, step, 0)

            @pl.when(i == 0)
            def _():
                for g in range(gp):
                    dq_acc[g] = block(g, 0, True, cm[g], cg[g])[0]

            @pl.when(i > 0)
            def _():
                prev = [block(g, i - 1, False, cm[g], cg[g]) for g in range(gp)]
                diag = [block(g, i, True, prev[g][1], prev[g][2]) for g in range(gp)]
                for g in range(gp):
                    dq_acc[g] += prev[g][0] + diag[g][0]

            for g in range(gp):
                dq = jnp.where(h0, dq_acc[g, 0:tq, :], dq_acc[g, tq:2 * tq, :])
                dq_ref[rows, lanes[g]] = (dq * SB_SCALE).astype(BF)

        for u in range(per):
            query_block(pl.program_id(2) * per + u, slice(u * tq, (u + 1) * tq))

        @pl.when(pl.program_id(2) == ns - 1)
        def _():
            for g in range(gp):
                dk_ref[:, lanes[g]] = dk_acc[g].T.astype(BF)
                dv_ref[:, lanes[g]] = dv_acc[g].T.astype(BF)

    blk = lambda col0: pl.BlockSpec((per * tq, w), lambda b, hg_, i: (b * ns + i, col0 + hg_))
    seq = lambda col0: pl.BlockSpec((s, w), lambda b, hg_, i: (b, col0 + hg_))
    first = 1024 // w
    (dq, dk, dv, dhg), rode = _ride_call(
        body, "sb_bwd", (nb, ng, ns),
        in_specs=[blk(first), seq(first + ng), seq(first + 2 * ng), blk(0), blk(0),
                  pl.BlockSpec(memory_space=pltpu.SMEM), blk(ng),
                  pl.BlockSpec((1, w), lambda b, hg_, i: (0, ng + hg_)), ANY],
        out_specs=[blk(first), seq(0), seq(0), pl.BlockSpec((None, 1, w), lambda b, hg_, i: (b, 0, hg_))],
        out_shape=[jax.ShapeDtypeStruct((t, IN_COLS), BF), jax.ShapeDtypeStruct((t, 512), BF),
                   jax.ShapeDtypeStruct((t, 512), BF), jax.ShapeDtypeStruct((nb, 1, 512), F32)],
        scratch_shapes=[pltpu.VMEM((gp, 128, s), F32), pltpu.VMEM((gp, 128, s), F32),
                        pltpu.VMEM((gp, 2 * tq, 128), F32), pltpu.VMEM((gp, 2 * tq, 1), F32),
                        pltpu.VMEM((gp, 2 * tq, 1), F32)],
        operands=(proj, proj, proj, o_sb, tot, nblk.reshape(nb, ng, nq, 8, 128)[:, :, :, 0, 0], dmerged, hg, dproj),
        vmem_mb=48, ride=ride, aliases={8: 0})
    return dq, dk, dv, dhg, rode


def _place(buf, piece, col_block, name):
    t, w = piece.shape
    tm = min(t, 1024)

    def body(piece_ref, buf_ref, out_ref):
        del buf_ref
        out_ref[...] = piece_ref[...]

    return pl.pallas_call(
        body, name=name, grid=(t // tm,),
        in_specs=[pl.BlockSpec((tm, w), lambda i: (i, 0)), ANY],
        out_specs=pl.BlockSpec((tm, w), lambda i: (i, col_block)),
        out_shape=jax.ShapeDtypeStruct(buf.shape, buf.dtype), input_output_aliases={1: 0},
        compiler_params=_params(16, ("arbitrary",)),
    )(piece, buf)


def _softmax_rows(sc):
    e = jnp.exp(sc - jnp.max(sc, axis=-1, keepdims=True))
    return e / jnp.sum(e, axis=-1, keepdims=True)


def _mix_cross_fwd(x, merged, w_out, gc, w_cq, kv, w_co, s, tm):
    t, d = x.shape
    tm = min(tm, s)
    per = s // tm
    inv = 1.0 / math.sqrt(X_HEAD_DIM)

    def body(x_ref, m_ref, wo_ref, gc_ref, wq_ref, kv_ref, wc_ref, h1_ref, h2_ref, hn_ref, qc_ref, oc_ref):
        h1 = x_ref[...] + _dot(m_ref[...], wo_ref[...])
        h1_ref[...] = h1
        hn = (h1 * _rs(h1) * gc_ref[...]).astype(BF)
        hn_ref[...] = hn
        qc = _dot(hn, wq_ref[...]).astype(BF)
        qc_ref[...] = qc
        for h in range(X_HEADS):
            cols = slice(h * X_HEAD_DIM, (h + 1) * X_HEAD_DIM)
            kh = kv_ref[:, h * X_HEAD_DIM:(h + 1) * X_HEAD_DIM]
            vh = kv_ref[:, d + h * X_HEAD_DIM:d + (h + 1) * X_HEAD_DIM]
            p = _softmax_rows(_dot_bt(qc[:, cols], kh) * inv)
            oc_ref[:, cols] = _dot(p.astype(BF), vh).astype(BF)
        h2_ref[...] = h1 + _dot(oc_ref[...], wc_ref[...])

    row = lambda width: pl.BlockSpec((tm, width), lambda i: (i, 0))
    full = lambda a, b: pl.BlockSpec((a, b), lambda i: (0, 0))
    return pl.pallas_call(
        body, name="mix_cross_fwd", grid=(t // tm,),
        in_specs=[row(d), row(d), full(d, d), full(1, d), full(d, d),
                  pl.BlockSpec((N_MEM, 2 * d), lambda i: (i // per, 0)), full(d, d)],
        out_specs=[row(d), row(d), row(d), row(d), row(d)],
        out_shape=[jax.ShapeDtypeStruct((t, d), F32), jax.ShapeDtypeStruct((t, d), F32),
                   jax.ShapeDtypeStruct((t, d), BF), jax.ShapeDtypeStruct((t, d), BF),
                   jax.ShapeDtypeStruct((t, d), BF)],
        compiler_params=_params(48, ("arbitrary",)),
    )(*_in_hbm(x, merged, w_out, gc, w_cq, kv, w_co))


def _cross_bwd(dh2, h1, qc, gc, w_cq, kv, w_co, s, tm):
    t, d = dh2.shape
    tm = min(tm, s)
    per = s // tm
    nb = t // s
    inv = 1.0 / math.sqrt(X_HEAD_DIM)

    def body(dh2_ref, h1_ref, qc_ref, gc_ref, wq_ref, kv_ref, wc_ref, dh1_ref, dqc_ref, dkv_ref, dgc_ref):
        i = pl.program_id(0)

        @pl.when(i == 0)
        def _():
            dgc_ref[...] = jnp.zeros_like(dgc_ref)

        @pl.when(i % per == 0)
        def _():
            dkv_ref[...] = jnp.zeros_like(dkv_ref)

        dh2 = dh2_ref[...]
        h1 = h1_ref[...]
        r = _rs(h1)
        h1h = h1 * r
        gcv = gc_ref[...]
        qc = qc_ref[...]
        do = _dot_bt(dh2.astype(BF), wc_ref[...]).astype(BF)
        for h in range(X_HEADS):
            cols = slice(h * X_HEAD_DIM, (h + 1) * X_HEAD_DIM)
            vcols = slice(d + h * X_HEAD_DIM, d + (h + 1) * X_HEAD_DIM)
            kh = kv_ref[:, cols]
            vh = kv_ref[:, vcols]
            p = _softmax_rows(_dot_bt(qc[:, cols], kh) * inv)
            dp = _dot_bt(do[:, cols], vh)
            ds = (p * (dp - jnp.sum(dp * p, axis=-1, keepdims=True)) * inv).astype(BF)
            dqc_ref[:, cols] = _dot(ds, kh).astype(BF)
            dkv_ref[:, cols] += _dot_at(ds, qc[:, cols])
            dkv_ref[:, vcols] += _dot_at(p.astype(BF), do[:, cols])
        dhn = _dot_bt(dqc_ref[...], wq_ref[...])
        dx, dg = _rms_bwd(dhn, h1h, r, gcv)
        dh1_ref[...] = dh2 + dx
        dgc_ref[...] += jnp.sum(dg, axis=0, keepdims=True)

    row = lambda width: pl.BlockSpec((tm, width), lambda i: (i, 0))
    full = lambda a, b: pl.BlockSpec((a, b), lambda i: (0, 0))
    kvspec = pl.BlockSpec((N_MEM, 2 * d), lambda i: (i // per, 0))
    return pl.pallas_call(
        body, name="cross_bwd", grid=(t // tm,),
        in_specs=[row(d), row(d), row(d), full(1, d), full(d, d), kvspec, full(d, d)],
        out_specs=[row(d), row(d), kvspec, full(1, d)],
        out_shape=[jax.ShapeDtypeStruct((t, d), F32), jax.ShapeDtypeStruct((t, d), BF),
                   jax.ShapeDtypeStruct((nb * N_MEM, 2 * d), F32), jax.ShapeDtypeStruct((1, d), F32)],
        compiler_params=_params(48, ("arbitrary",)),
    )(*_in_hbm(dh2, h1, qc, gc, w_cq, kv, w_co))


def _mem_bwd(mem, gm, dkv, w_ckv, tm):
    t, d = mem.shape
    tm = min(tm, t)

    def body(mem_ref, dkv_ref, w_ref, dg_ref):
        @pl.when(pl.program_id(0) == 0)
        def _():
            dg_ref[...] = jnp.zeros_like(dg_ref)

        mv = mem_ref[...]
        dmn = _dot_bt(dkv_ref[...].astype(BF), w_ref[...])
        dg_ref[...] += jnp.sum(dmn * (mv * _rs(mv)), axis=0, keepdims=True)

    del gm
    return pl.pallas_call(
        body, name="mem_bwd", grid=(t // tm,),
        in_specs=[pl.BlockSpec((tm, d), lambda i: (i, 0)), pl.BlockSpec((tm, 2 * d), lambda i: (i, 0)),
                  pl.BlockSpec((d, 2 * d), lambda i: (0, 0))],
        out_specs=pl.BlockSpec((1, d), lambda i: (0, 0)),
        out_shape=jax.ShapeDtypeStruct((1, d), F32),
        compiler_params=_params(32, ("arbitrary",)),
    )(mem, dkv, w_ckv)


def _ffn_loss_fwd(h2, gf, w1, w2, gl, target, tm):
    t, d = h2.shape
    tm = min(tm, t)

    def body(h2_ref, gf_ref, w1_ref, w2_ref, gl_ref, tg_ref, hn_ref, f_ref, dh3_ref, dgl_ref, loss_ref):
        @pl.when(pl.program_id(0) == 0)
        def _():
            dgl_ref[...] = jnp.zeros_like(dgl_ref)
            loss_ref[...] = jnp.zeros_like(loss_ref)

        h2 = h2_ref[...]
        hn = (h2 * _rs(h2) * gf_ref[...]).astype(BF)
        hn_ref[...] = hn
        h3 = h2
        for c in range(4):
            f = jnp.maximum(_dot(hn, w1_ref[c]), 0.0)
            f_ref[:, c * 1024:(c + 1) * 1024] = f.astype(BF)
            h3 = h3 + _dot((f * f).astype(BF), w2_ref[c])
        r3 = _rs(h3)
        yh = h3 * r3
        glv = gl_ref[...]
        e = yh * glv - tg_ref[...]
        loss_ref[...] += 0.5 * jnp.sum(jnp.sum(e * e, axis=-1, keepdims=True) * (1.0 / d), axis=0, keepdims=True)
        dy = e * (1.0 / d)
        dx, dg = _rms_bwd(dy, yh, r3, glv)
        dh3_ref[...] = dx
        dgl_ref[...] += jnp.sum(dg, axis=0, keepdims=True)

    row = lambda width: pl.BlockSpec((tm, width), lambda i: (i, 0))
    return pl.pallas_call(
        body, name="ffn_loss_fwd", grid=(t // tm,),
        in_specs=[row(d), pl.BlockSpec((1, d), lambda i: (0, 0)), pl.BlockSpec((4, d, 1024), lambda i: (0, 0, 0), pipeline_mode=pl.Buffered(1)),
                  pl.BlockSpec((4, 1024, d), lambda i: (0, 0, 0), pipeline_mode=pl.Buffered(1)),
                  pl.BlockSpec((1, d), lambda i: (0, 0)), row(d)],
        out_specs=[row(d), row(D_FF), row(d), pl.BlockSpec((1, d), lambda i: (0, 0)),
                   pl.BlockSpec((1, 1), lambda i: (0, 0))],
        out_shape=[jax.ShapeDtypeStruct((t, d), BF), jax.ShapeDtypeStruct((t, D_FF), BF),
                   jax.ShapeDtypeStruct((t, d), F32), jax.ShapeDtypeStruct((1, d), F32),
                   jax.ShapeDtypeStruct((1, 1), F32)],
        compiler_params=_params(56, ("arbitrary",)),
    )(*_in_hbm(h2, gf, w1, w2, gl, target))


def _ffn_bwd(dh3, f, h2, gf, w1, w2, tm):
    t, d = h2.shape
    tm = min(tm, t)

    def body(dh3_ref, f_ref, h2_ref, gf_ref, w1_ref, w2_ref, dh2_ref, dpre_ref, dgf_ref):
        @pl.when(pl.program_id(0) == 0)
        def _():
            dgf_ref[...] = jnp.zeros_like(dgf_ref)

        dh3 = dh3_ref[...]
        dh3b = dh3.astype(BF)
        dhn = jnp.zeros((tm, d), F32)
        for c in range(4):
            cols = slice(c * 1024, (c + 1) * 1024)
            dpre = (_dot_bt(dh3b, w2_ref[c]) * (2.0 * f_ref[:, cols].astype(F32))).astype(BF)
            dpre_ref[:, cols] = dpre
            dhn = dhn + _dot_bt(dpre, w1_ref[c])
        h2 = h2_ref[...]
        r = _rs(h2)
        dx, dg = _rms_bwd(dhn, h2 * r, r, gf_ref[...])
        dh2_ref[...] = dh3 + dx
        dgf_ref[...] += jnp.sum(dg, axis=0, keepdims=True)

    row = lambda width: pl.BlockSpec((tm, width), lambda i: (i, 0))
    return pl.pallas_call(
        body, name="ffn_bwd", grid=(t // tm,),
        in_specs=[row(d), row(D_FF), row(d), pl.BlockSpec((1, d), lambda i: (0, 0)),
                  pl.BlockSpec((4, d, 1024), lambda i: (0, 0, 0), pipeline_mode=pl.Buffered(1)),
                  pl.BlockSpec((4, 1024, d), lambda i: (0, 0, 0), pipeline_mode=pl.Buffered(1))],
        out_specs=[row(d), row(D_FF), pl.BlockSpec((1, d), lambda i: (0, 0))],
        out_shape=[jax.ShapeDtypeStruct((t, d), F32), jax.ShapeDtypeStruct((t, D_FF), BF),
                   jax.ShapeDtypeStruct((1, d), F32)],
        compiler_params=_params(56, ("arbitrary",)),
    )(*_in_hbm(dh3, f, h2, gf, w1, w2))


def _in_bwd(dproj, dh1, x, g, w_in, tm, ride=None):
    t, d = x.shape
    n = w_in.shape[1]
    tm = min(tm, t)

    def body(dp_ref, dh1_ref, x_ref, g_ref, w_ref, dx_ref, dg_ref):
        @pl.when(pl.program_id(0) == 0)
        def _():
            dg_ref[...] = jnp.zeros_like(dg_ref)

        dxn = _dot_bt(dp_ref[...], w_ref[...])
        xv = x_ref[...]
        r = _rs(xv)
        dx, dg = _rms_bwd(dxn, xv * r, r, g_ref[...])
        dx_ref[...] = dh1_ref[...] + dx
        dg_ref[...] += jnp.sum(dg, axis=0, keepdims=True)

    row = lambda width: pl.BlockSpec((tm, width), lambda i: (i, 0))
    (dx, dg), rode = _ride_call(
        body, "in_bwd", (t // tm,),
        in_specs=[row(n), row(d), row(d), pl.BlockSpec((1, d), lambda i: (0, 0)),
                  pl.BlockSpec((d, n), lambda i: (0, 0))],
        out_specs=[row(d), pl.BlockSpec((1, d), lambda i: (0, 0))],
        out_shape=[jax.ShapeDtypeStruct((t, d), F32), jax.ShapeDtypeStruct((1, d), F32)],
        scratch_shapes=[], operands=(dproj, dh1, x, g, w_in), vmem_mb=48, ride=ride)
    return dx, dg, rode


class _GradReduce:
    def __init__(self, c_idx):
        self.c_idx = c_idx
        self.sums = {}

    def sibling(self, slabs):
        return _SiblingExchange(slabs)

    def chip(self, names, slabs, recv):
        for k, a, r in zip(names, slabs, recv):
            self.sums[k] = _chip_sum(a, r, self.c_idx, "chip_sum_" + k)
        return _ChipExchange([self.sums[k] for k in names])


def _full_weights(gathered):
    d = D_MODEL
    out = {}
    for k, a in gathered.items():
        if k in ("w_in", "w_ckv", "w_ff1"):
            out[k] = a.transpose(1, 0, 2).reshape(d, -1)
        else:
            out[k] = a.reshape(-1, d)
    return out


def _slabs(a):
    return a.reshape(N_DEV, -1, a.shape[-1])


def _local_step(x, mem, target, small, big, nb, s, tq=256, gather_rest=None, reduce=None):
    d = D_MODEL
    g_mix, g_v, w_sp, b_sp, g_head, g_cross, g_mem, g_ffn, g_fin = (
        small[k] for k in ("norm_mix_g", "gm_v_norm_g", "w_spatial", "b_spatial", "head_norm_g", "norm_cross_g",
                           "norm_mem_g", "norm_ffn_g", "norm_final_g"))
    tri = jnp.tril(jnp.ones((CHUNK, CHUNK), dtype=bool))
    w_sp_m = jnp.where(tri[None], w_sp, 0.0)
    wt = w_sp_m.astype(BF)
    wtt = jnp.swapaxes(w_sp_m, 1, 2).astype(BF)
    bb = jnp.broadcast_to(b_sp[:, :, None], (GM_GROUPS, CHUNK, CHUNK))
    hg_a = g_head[:, :GM_WIDTH]

    proj, xn = _norm_matmul(x, g_mix, big["w_in"], 512, "in_proj")
    merged = _gmlp_fwd(proj, g_v, wt, bb, hg_a, 512)
    o_sb, tot, merged, nblk, gathered = _sb_fwd(proj, merged, g_head, nb, s, tq, ride=gather_rest)
    if gather_rest is not None:
        big = dict(big, **_full_weights(dict(zip(BIG[1:], gathered))))
    w1c = big["w_ff1"].reshape(d, 4, 1024).transpose(1, 0, 2)
    w2c = big["w_ff2"].reshape(4, 1024, d)
    kv, memn = _norm_matmul(mem, g_mem, big["w_ckv"], 512, "mem_proj")
    h1, h2, hn, qc, oc = _mix_cross_fwd(x, merged, big["w_out"], g_cross, big["w_cq"], kv, big["w_co"], s, 512)
    hn2, f, dh3, d_fin, loss = _ffn_loss_fwd(h2, g_ffn, w1c, w2c, g_fin, target, 512)

    gbig = {}
    dh2, dpre, d_ffn = _ffn_bwd(dh3, f, h2, g_ffn, w1c, w2c, 512)
    gbig["w_ff2"] = _slabs(_wgrad(f, dh3, 1024, 1024, "wgrad_ff2", square_a=True))
    gbig["w_ff1"] = _slabs(_wgrad_wide(hn2, dpre, 512, 1024, "wgrad_ff1", col_shards=4))
    dh1, dqc, dkv, d_cross = _cross_bwd(dh2, h1, qc, g_cross, big["w_cq"], kv, big["w_co"], s, 512)
    gbig["w_co"] = _slabs(_wgrad(oc, dh2, 1024, 1024, "wgrad_co"))
    gbig["w_cq"] = _slabs(_wgrad(hn, dqc, 1024, 1024, "wgrad_cq"))
    gbig["w_ckv"] = _slabs(_wgrad(memn, dkv, 512, 1024, "wgrad_ckv", col_shards=4))
    d_mem = _mem_bwd(mem, g_mem, dkv, big["w_ckv"], 512)
    dmerged = _matmul_bt(dh1, big["w_out"], 512, "out_bwd")
    early = [k for k in BIG if k not in ("w_in", "w_out")]
    ride = reduce.sibling([gbig[k] for k in early]) if reduce else None
    dproj, d_wsp, d_bb, d_gv, d_hga, recv = _gmlp_bwd(proj, dmerged, g_v, wt, wtt, bb, hg_a, 512, ride=ride)
    ride = reduce.chip(early, [gbig[k] for k in early], recv) if reduce else None
    dproj, dk, dv, d_hgb, parts_early = _sb_bwd(proj, o_sb, tot, nblk, dmerged, dproj, g_head, nb, s, tq, ride=ride)
    parts = dict(zip(early, parts_early))
    dproj = _place(_place(dproj, dk, 3, "place_dk"), dv, 4, "place_dv")
    grad_x, d_mix, _ = _in_bwd(dproj, dh1, x, g_mix, big["w_in"], 512)
    gbig["w_in"] = _slabs(_wgrad_wide(xn, dproj, 512, 1024, "wgrad_in", col_shards=4))
    last = None
    if reduce:
        recv = _run_exchange(reduce.sibling([gbig["w_in"]]), "grad_sibling_exchange_w_in")
        g_out, (parts["w_in"],) = _wgrad(merged, dh1, 1024, 1024, "wgrad_out",
                                         ride=reduce.chip(["w_in"], [gbig["w_in"]], recv))
        gbig["w_out"] = _slabs(g_out)
        recv = _run_exchange(reduce.sibling([gbig["w_out"]]), "grad_sibling_exchange_w_out")
        last = reduce.chip(["w_out"], [gbig["w_out"]], recv)
    else:
        gbig["w_out"] = _slabs(_wgrad(merged, dh1, 1024, 1024, "wgrad_out"))

    gsmall = {
        "norm_mix_g": d_mix, "gm_v_norm_g": d_gv, "w_spatial": d_wsp, "b_spatial": d_bb[:, :, 0],
        "head_norm_g": jnp.concatenate([d_hga, jnp.sum(d_hgb, axis=0)], axis=1), "norm_cross_g": d_cross,
        "norm_mem_g": d_mem, "norm_ffn_g": d_ffn, "norm_final_g": d_fin,
    }
    return loss, grad_x, gsmall, gbig, parts, last


BIG = ("w_in", "w_out", "w_cq", "w_ckv", "w_co", "w_ff1", "w_ff2")
SMALL = ("norm_mix_g", "gm_v_norm_g", "w_spatial", "b_spatial", "head_norm_g", "norm_cross_g", "norm_mem_g",
         "norm_ffn_g", "norm_final_g")


def _local_copies_start(srcs, stages, sems):
    loads = [pltpu.make_async_copy(src, stage, sems.at[w]) for w, (src, stage) in enumerate(zip(srcs, stages))]
    for ld in loads:
        ld.start()
    return loads


def _local_copies_finish(loads, stages, dsts, sems):
    stores = []
    for w, (ld, stage, dst) in enumerate(zip(loads, stages, dsts)):
        ld.wait()
        st = pltpu.make_async_copy(stage, dst, sems.at[w])
        st.start()
        stores.append(st)
    for st in stores:
        st.wait()


def _chip_sum(slabs, recv, c_idx, name):
    _, r, cw = slabs.shape
    tr = min(r, 256)

    def body(c_ref, a_ref, b_ref, o_ref):
        del c_ref
        o_ref[...] = (a_ref[...] + b_ref[...]).astype(BF)

    return pl.pallas_call(
        body, name=name,
        grid_spec=pltpu.PrefetchScalarGridSpec(
            num_scalar_prefetch=1, grid=(N_CHIPS, r // tr),
            in_specs=[pl.BlockSpec((None, tr, cw), lambda p, i, c_ref: (2 * p + c_ref[0], i, 0)),
                      pl.BlockSpec((None, tr, cw), lambda p, i, c_ref: (p, i, 0))],
            out_specs=pl.BlockSpec((None, tr, cw), lambda p, i, c_ref: (p, i, 0))),
        out_shape=jax.ShapeDtypeStruct((N_CHIPS, r, cw), BF),
        compiler_params=_params(32, ("arbitrary", "arbitrary")),
    )(c_idx, *_in_hbm(slabs, recv))


def _sum4(sums, parts, q_idx, name):
    _, r, cw = parts.shape
    tr = min(r, 256)

    def body(q_ref, own_ref, a_ref, b_ref, c_ref, o_ref):
        del q_ref
        o_ref[...] = ((own_ref[...].astype(F32) + a_ref[...].astype(F32)) + b_ref[...].astype(F32)) + c_ref[
            ...].astype(F32)

    spec = lambda k: pl.BlockSpec((None, tr, cw), lambda i, q_ref: ((q_ref[0] + k) % N_CHIPS, i, 0))
    return pl.pallas_call(
        body, name=name,
        grid_spec=pltpu.PrefetchScalarGridSpec(
            num_scalar_prefetch=1, grid=(r // tr,), in_specs=[spec(0), spec(1), spec(2), spec(3)],
            out_specs=pl.BlockSpec((tr, cw), lambda i, q_ref: (i, 0))),
        out_shape=jax.ShapeDtypeStruct((r, cw), F32),
        compiler_params=_params(32, ("arbitrary",)),
    )(q_idx, *_in_hbm(sums, parts, parts, parts))


def _half_exchange(halves):
    n = len(halves)

    def body(*refs):
        ins, outs, stages = refs[:n], refs[n:2 * n], refs[2 * n:3 * n]
        send_sems, recv_sems, ld_sems, st_sems = refs[3 * n:]
        x, y, c = lax.axis_index("x"), lax.axis_index("y"), lax.axis_index("c")
        loads = _local_copies_start(ins, stages, ld_sems)
        copies = []
        for w in range(n):
            cp = pltpu.make_async_remote_copy(
                src_ref=ins[w], dst_ref=outs[w].at[c], send_sem=send_sems.at[w], recv_sem=recv_sems.at[w],
                device_id=(x, y, 1 - c), device_id_type=MESH)
            cp.start()
            copies.append(cp)
        _local_copies_finish(loads, stages, [outs[w].at[c] for w in range(n)], st_sems)
        for cp in copies:
            cp.wait()

    return pl.pallas_call(
        body, name="grad_half_exchange",
        in_specs=[ANY] * n, out_specs=[ANY] * n,
        out_shape=[jax.ShapeDtypeStruct((2,) + a.shape, a.dtype) for a in halves],
        scratch_shapes=[pltpu.VMEM(a.shape, a.dtype) for a in halves] + [
            pltpu.SemaphoreType.DMA((n,)), pltpu.SemaphoreType.DMA((n,)),
            pltpu.SemaphoreType.DMA((n,)), pltpu.SemaphoreType.DMA((n,))],
        compiler_params=_params(24),
    )(*halves)


def _small_all_reduce(packed, ride=None):
    rows = packed.shape[0]
    ride = ride or _NoExchange()
    ri, ro = len(ride.in_arrays), len(ride.out_shape)

    def body(*refs):
        in_ref, rins, out_ref, routs = refs[0], refs[1:1 + ri], refs[1 + ri], refs[2 + ri:2 + ri + ro]
        pair, chip_sum, chips, d2d_send, d2d_recv, ici_send, ici_recv = refs[2 + ri + ro:9 + ri + ro]
        rscr = refs[9 + ri + ro:]
        ride.start(rins, routs, rscr)
        x, y, c = lax.axis_index("x"), lax.axis_index("y"), lax.axis_index("c")
        q = 2 * x + y
        pair[c] = in_ref[...]
        swap = pltpu.make_async_remote_copy(
            src_ref=in_ref, dst_ref=pair.at[c], send_sem=d2d_send, recv_sem=d2d_recv,
            device_id=(x, y, 1 - c), device_id_type=MESH)
        swap.start()
        swap.wait()
        both = pair[0] + pair[1]
        chip_sum[...] = both
        chips[q] = both
        copies = [pltpu.make_async_remote_copy(
            src_ref=chip_sum, dst_ref=chips.at[q], send_sem=ici_send.at[k], recv_sem=ici_recv.at[k],
            device_id=(px, py, c), device_id_type=MESH) for k, (px, py) in enumerate(_other_chips(x, y))]
        for cp in copies:
            cp.start()
        for cp in copies:
            cp.wait()
        out_ref[...] = ((chips[0] + chips[1]) + chips[2]) + chips[3]
        ride.finish(rins, routs, rscr)

    vmem = pl.BlockSpec(memory_space=pltpu.VMEM)
    res = pl.pallas_call(
        body, name="small_all_reduce",
        in_specs=[vmem] + [ANY] * ri, out_specs=[vmem] + [ANY] * ro,
        out_shape=[jax.ShapeDtypeStruct(packed.shape, F32)] + list(ride.out_shape),
        scratch_shapes=[pltpu.VMEM((2, rows, 128), F32), pltpu.VMEM((rows, 128), F32),
                        pltpu.VMEM((N_CHIPS, rows, 128), F32), pltpu.SemaphoreType.DMA, pltpu.SemaphoreType.DMA,
                        pltpu.SemaphoreType.DMA((3,)), pltpu.SemaphoreType.DMA((3,))] + list(ride.scratch_shapes),
        compiler_params=_params(16),
    )(packed, *ride.in_arrays)
    return res[0], res[1:]


def _adamw(g, w, m, v, name):
    r, cw = g.shape
    tr = 256 if r % 256 == 0 else r

    def body(g_ref, w_ref, m_ref, v_ref, d_ref, nm_ref, nv_ref):
        gv = g_ref[...]
        nm = ADAM_B1 * m_ref[...] + (1.0 - ADAM_B1) * gv
        nv = ADAM_B2 * v_ref[...] + (1.0 - ADAM_B2) * (gv * gv)
        m_hat = nm / (1.0 - ADAM_B1 ** ADAM_STEP)
        v_hat = nv / (1.0 - ADAM_B2 ** ADAM_STEP)
        d_ref[...] = -ADAM_LR * (m_hat / (jnp.sqrt(v_hat) + ADAM_EPS) + ADAM_WD * w_ref[...])
        nm_ref[...] = nm
        nv_ref[...] = nv

    spec = pl.BlockSpec((tr, cw), lambda i: (i, 0))
    return pl.pallas_call(
        body, name=name, grid=(r // tr,),
        in_specs=[spec] * 4, out_specs=[spec] * 3,
        out_shape=[jax.ShapeDtypeStruct((r, cw), F32)] * 3,
        compiler_params=_params(32, ("arbitrary",)),
    )(*_in_hbm(g, w, m, v))


def _small_params(args):
    small = {k: args[k].reshape(1, -1) for k in SMALL}
    small["w_spatial"] = args["w_spatial"][0]
    small["b_spatial"] = args["b_spatial"][0]
    return small


def _pack(parts, rows):
    flat = jnp.concatenate([p.reshape(-1).astype(F32) for p in parts])
    return jnp.pad(flat, (0, rows * 128 - flat.shape[0])).reshape(rows, 128)


def _unpack(packed, shapes):
    flat = packed.reshape(-1)
    out, off = [], 0
    for shp in shapes:
        size = math.prod(shp)
        out.append(flat[off:off + size].reshape(shp))
        off += size
    return out


def kernel(x, mem, norm_mix_g, w_in, gm_v_norm_g, w_spatial, b_spatial, head_norm_g, w_out, norm_cross_g, norm_mem_g, w_cq, w_ckv, w_co, norm_ffn_g, w_ff1, w_ff2, norm_final_g, loss_target, m_norm_mix_g, m_w_in, m_gm_v_norm_g, m_w_spatial, m_b_spatial, m_head_norm_g, m_w_out, m_norm_cross_g, m_norm_mem_g, m_w_cq, m_w_ckv, m_w_co, m_norm_ffn_g, m_w_ff1, m_w_ff2, m_norm_final_g, v_norm_mix_g, v_w_in, v_gm_v_norm_g, v_w_spatial, v_b_spatial, v_head_norm_g, v_w_out, v_norm_cross_g, v_norm_mem_g, v_w_cq, v_w_ckv, v_w_co, v_norm_ffn_g, v_w_ff1, v_w_ff2, v_norm_final_g):
    args = dict(locals())
    d = D_MODEL
    nb, s, _ = x.shape
    c_idx = lax.axis_index("c").astype(jnp.int32).reshape(1)
    q_idx = (2 * lax.axis_index("x") + lax.axis_index("y")).astype(jnp.int32).reshape(1)
    rest = BIG[1:]

    shards = {k: args[k][0].astype(BF) for k in BIG}
    big = _full_weights({"w_in": _run_exchange(_GatherExchange([shards["w_in"]]), "all_gather_w_in")[0]})
    gather_rest = _GatherExchange([shards[k] for k in rest])

    reduce = _GradReduce(c_idx)
    loss, grad_x, gsmall, _, parts, last = _local_step(
        x.reshape(nb * s, d), mem.reshape(nb * N_MEM, d), loss_target.reshape(nb * s, d), _small_params(args), big,
        nb, s, gather_rest=gather_rest, reduce=reduce)

    shapes = [args[k].shape for k in SMALL]
    n_small = sum(math.prod(sh) for sh in shapes)
    rows = -(-(n_small + 1) // 1024) * 8
    reduced, (parts["w_out"],) = _small_all_reduce(_pack([gsmall[k] for k in SMALL] + [loss], rows), ride=last)
    halves = [_sum4(reduce.sums[k], parts[k], q_idx, "sum4_" + k) for k in BIG]
    both = _half_exchange(halves)

    out = {"grad_x": grad_x.reshape(nb, s, d)}
    for k, g2 in zip(BIG, both):
        shp = args[k].shape
        g = g2.reshape(shp[1], shp[2])
        dl, nm, nv = _adamw(g, args[k][0], args["m_" + k][0], args["v_" + k][0], "adamw_" + k)
        out["grad_" + k], out["delta_" + k], out["new_m_" + k], out["new_v_" + k] = (
            a.reshape(shp) for a in (g, dl, nm, nv))

    dl, nm, nv = _adamw(reduced, _pack([args[k] for k in SMALL], rows), _pack([args["m_" + k] for k in SMALL], rows),
                        _pack([args["v_" + k] for k in SMALL], rows), "adamw_small")
    for name, arr in (("grad_", reduced), ("delta_", dl), ("new_m_", nm), ("new_v_", nv)):
        for k, a in zip(SMALL, _unpack(arr, shapes)):
            out[name + k] = a
    out["loss"] = reduced.reshape(-1)[n_small]

    names = ["norm_mix_g", "w_in", "gm_v_norm_g", "w_spatial", "b_spatial", "head_norm_g", "w_out", "norm_cross_g",
             "norm_mem_g", "w_cq", "w_ckv", "w_co", "norm_ffn_g", "w_ff1", "w_ff2", "norm_final_g"]
    return (out["loss"], out["grad_x"], *[out["grad_" + k] for k in names], *[out["delta_" + k] for k in names],
            *[out["new_m_" + k] for k in names], *[out["new_v_" + k] for k in names])
```

```python
import functools
import math

import jax
import jax.numpy as jnp
from jax import lax
from jax.experimental import pallas as pl
from jax.experimental.pallas import tpu as pltpu

F32 = jnp.float32
BF = jnp.bfloat16

EPS = 1e-6
D_MODEL = 1024
CHUNK = 128
GM_GROUPS = 4
GM_WIDTH = 512
SB_WIDTH = 512
HEAD_LANES = 64
SB_SCALE = 0.125
SB_SKIP = -104.0
X_HEADS = 4
X_HEAD_DIM = 256
N_MEM = 256
D_FF = 4096
IN_COLS = 2560
N_CHIPS = 4
N_DEV = 8

ADAM_LR = 0.001
ADAM_B1 = 0.9
ADAM_B2 = 0.999
ADAM_EPS = 1e-08
ADAM_WD = 0.01
ADAM_STEP = 10

V7X_VMEM_BYTES = 64 * 1024 * 1024
MESH = pl.DeviceIdType.MESH
ANY = pl.BlockSpec(memory_space=pl.ANY)

GELU_C = math.sqrt(2.0 / math.pi)
GELU_A = 0.044715


def _params(vmem_mb, sem=None):
    assert vmem_mb * 1024 * 1024 <= V7X_VMEM_BYTES
    return pltpu.CompilerParams(vmem_limit_bytes=vmem_mb * 1024 * 1024, dimension_semantics=sem)


PIN_MIN_ELEMENTS = 1 << 18


def _in_hbm(*arrays):
    return tuple(pltpu.with_memory_space_constraint(a, pltpu.HBM) if a.size >= PIN_MIN_ELEMENTS else a
                 for a in arrays)


def _dot(a, b):
    return jnp.dot(a, b, preferred_element_type=F32)


def _dot_bt(a, b):
    return lax.dot_general(a, b, (((1,), (1,)), ((), ())), preferred_element_type=F32)


def _dot_at(a, b):
    return lax.dot_general(a, b, (((0,), (0,)), ((), ())), preferred_element_type=F32)


def _gelu(x):
    t = jnp.tanh(GELU_C * (x + GELU_A * x * x * x))
    return 0.5 * x * (1.0 + t)


def _gelu_and_grad(x):
    x2 = x * x
    t = jnp.tanh(GELU_C * (x + GELU_A * x2 * x))
    h = 0.5 * (1.0 + t)
    return x * h, h + 0.5 * x * (1.0 - t * t) * (GELU_C * (1.0 + 3.0 * GELU_A * x2))


def _rs(x):
    return lax.rsqrt(jnp.mean(x * x, axis=-1, keepdims=True) + EPS)


def _rms_bwd(dxn, xhat, r, g):
    dxh = dxn * g
    dx = r * (dxh - xhat * jnp.mean(dxh * xhat, axis=-1, keepdims=True))
    return dx, dxn * xhat


def _norm_matmul(x, g, w, tm, name):
    t, d = x.shape
    n = w.shape[1]
    tm = min(tm, t)

    def body(x_ref, g_ref, w_ref, out_ref, xn_ref):
        xv = x_ref[...]
        xn = (xv * _rs(xv) * g_ref[...]).astype(BF)
        xn_ref[...] = xn
        out_ref[...] = _dot(xn, w_ref[...]).astype(out_ref.dtype)

    return pl.pallas_call(
        body, name=name, grid=(t // tm,),
        in_specs=[pl.BlockSpec((tm, d), lambda i: (i, 0)), pl.BlockSpec((1, d), lambda i: (0, 0)),
                  pl.BlockSpec((d, n), lambda i: (0, 0))],
        out_specs=[pl.BlockSpec((tm, n), lambda i: (i, 0)), pl.BlockSpec((tm, d), lambda i: (i, 0))],
        out_shape=[jax.ShapeDtypeStruct((t, n), BF), jax.ShapeDtypeStruct((t, d), BF)],
        compiler_params=_params(48, ("arbitrary",)),
    )(*_in_hbm(x, g, w))


def _wgrad(a, g, tn, tk, name, square_a=False, col_shards=1, tm=1024):
    t, m = a.shape
    n = g.shape[1]
    tk = min(tk, t)
    tm = min(m, tm)
    ns = n // col_shards
    assert ns % tn == 0 and m % tm == 0
    per = ns // tn
    nk = t // tk

    def body(a_ref, g_ref, o_ref):
        k = pl.program_id(2)

        @pl.when(k == 0)
        def _():
            o_ref[...] = jnp.zeros_like(o_ref)

        av = a_ref[...]
        if square_a:
            af = av.astype(F32)
            av = af * af
        o_ref[...] += _dot_at(av.astype(BF), g_ref[...].astype(BF))

    return pl.pallas_call(
        body, name=name, grid=(m // tm, n // tn, nk),
        in_specs=[pl.BlockSpec((tk, tm), lambda i, j, k: (k, i)), pl.BlockSpec((tk, tn), lambda i, j, k: (k, j))],
        out_specs=pl.BlockSpec((None, tm, tn), lambda i, j, k: (j // per, i, j % per)),
        out_shape=jax.ShapeDtypeStruct((col_shards, m, ns), F32),
        compiler_params=_params(48, ("arbitrary", "arbitrary", "arbitrary")),
    )(*_in_hbm(a, g))


def _wgrad_wide(a, g, tm, tk, name, col_shards):
    t, m = a.shape
    n = g.shape[1]
    tk = min(tk, t)
    tm = min(tm, m)
    ns = n // col_shards

    def body(a_ref, g_ref, o_ref):
        @pl.when(pl.program_id(1) == 0)
        def _():
            o_ref[...] = jnp.zeros_like(o_ref)

        a_t = a_ref[...].astype(BF).T
        for p in range(col_shards):
            o_ref[p] += _dot(a_t, g_ref[:, p * ns:(p + 1) * ns].astype(BF))

    return pl.pallas_call(
        body, name=name, grid=(m // tm, t // tk),
        in_specs=[pl.BlockSpec((tk, tm), lambda i, k: (k, i)), pl.BlockSpec((tk, n), lambda i, k: (k, 0))],
        out_specs=pl.BlockSpec((col_shards, tm, ns), lambda i, k: (0, i, 0)),
        out_shape=jax.ShapeDtypeStruct((col_shards, m, ns), F32),
        compiler_params=_params(48, ("arbitrary", "arbitrary")),
    )(*_in_hbm(a, g))


def _matmul_bt(a, w, tm, name):
    t, n = a.shape
    k = w.shape[0]
    tm = min(tm, t)

    def body(a_ref, w_ref, o_ref):
        o_ref[...] = _dot_bt(a_ref[...].astype(BF), w_ref[...]).astype(o_ref.dtype)

    return pl.pallas_call(
        body, name=name, grid=(t // tm,),
        in_specs=[pl.BlockSpec((tm, n), lambda i: (i, 0)), pl.BlockSpec((k, n), lambda i: (0, 0))],
        out_specs=pl.BlockSpec((tm, k), lambda i: (i, 0)),
        out_shape=jax.ShapeDtypeStruct((t, k), BF),
        compiler_params=_params(32, ("arbitrary",)),
    )(*_in_hbm(a, w))


def _gmlp_fwd(proj, gg, wt, bb, hg, tm):
    t = proj.shape[0]
    tm = min(tm, t)

    def body(u_ref, v_ref, gg_ref, wt_ref, bb_ref, hg_ref, out_ref):
        for cc in range(tm // CHUNK):
            rows = slice(cc * CHUNK, (cc + 1) * CHUNK)
            for g in range(GM_GROUPS):
                cols = slice(g * 128, (g + 1) * 128)
                u = _gelu(u_ref[rows, cols].astype(F32))
                gv = _gelu(v_ref[rows, cols].astype(F32))
                vn = gv * _rs(gv) * gg_ref[:, cols]
                mixed = _dot(wt_ref[g], vn.astype(BF)) + bb_ref[g]
                a = u * mixed
                out_ref[rows, cols] = (a * _rs(a) * hg_ref[:, cols]).astype(BF)

    return pl.pallas_call(
        body, name="gmlp_fwd", grid=(t // tm,),
        in_specs=[pl.BlockSpec((tm, 512), lambda i: (i, 0)), pl.BlockSpec((tm, 512), lambda i: (i, 1)),
                  pl.BlockSpec((1, 512), lambda i: (0, 0)), pl.BlockSpec((4, 128, 128), lambda i: (0, 0, 0)),
                  pl.BlockSpec((4, 128, 128), lambda i: (0, 0, 0)), pl.BlockSpec((1, 512), lambda i: (0, 0))],
        out_specs=pl.BlockSpec((tm, 512), lambda i: (i, 0)),
        out_shape=jax.ShapeDtypeStruct((t, 1024), BF),
        compiler_params=_params(32, ("arbitrary",)),
    )(*_in_hbm(proj, proj, gg, wt, bb, hg))


def _gmlp_bwd(proj, dmerged, gg, wt, wtt, bb, hg, tm, ride=None):
    t = proj.shape[0]
    tm = min(tm, t)
    nsteps = t // tm

    def body(u_ref, v_ref, dm_ref, gg_ref, wt_ref, wtt_ref, bb_ref, hg_ref,
             dp_ref, dw_ref, db_ref, dgg_ref, dhg_ref):
        i = pl.program_id(0)

        @pl.when(i == 0)
        def _():
            dw_ref[...] = jnp.zeros_like(dw_ref)
            db_ref[...] = jnp.zeros_like(db_ref)
            dgg_ref[...] = jnp.zeros_like(dgg_ref)
            dhg_ref[...] = jnp.zeros_like(dhg_ref)

        for cc in range(tm // CHUNK):
            rows = slice(cc * CHUNK, (cc + 1) * CHUNK)
            for g in range(GM_GROUPS):
                cols = slice(g * 128, (g + 1) * 128)
                up = u_ref[rows, cols].astype(F32)
                gp = v_ref[rows, cols].astype(F32)
                u, u_grad = _gelu_and_grad(up)
                gv, gv_grad = _gelu_and_grad(gp)
                rv = _rs(gv)
                gvh = gv * rv
                ggv = gg_ref[:, cols]
                vnb = (gvh * ggv).astype(BF)
                mixed = _dot(wt_ref[g], vnb) + bb_ref[g]
                a = u * mixed
                ra = _rs(a)
                ah = a * ra
                dm = dm_ref[rows, cols].astype(F32)
                dhg_ref[:, cols] += jnp.sum(dm * ah, axis=0, keepdims=True)
                dah = dm * hg_ref[:, cols]
                da = ra * (dah - ah * jnp.mean(dah * ah, axis=-1, keepdims=True))
                du = da * mixed
                dmix = da * u
                db_ref[g] += dmix
                dmb = dmix.astype(BF)
                dw_ref[g] += _dot_bt(dmb, vnb)
                dvn = _dot(wtt_ref[g], dmb)
                dgg_ref[:, cols] += jnp.sum(dvn * gvh, axis=0, keepdims=True)
                dgh = dvn * ggv
                dgv = rv * (dgh - gvh * jnp.mean(dgh * gvh, axis=-1, keepdims=True))
                dp_ref[rows, cols] = (du * u_grad).astype(BF)
                dp_ref[rows, 512 + g * 128:512 + (g + 1) * 128] = (dgv * gv_grad).astype(BF)

        @pl.when(i == nsteps - 1)
        def _():
            r = lax.broadcasted_iota(jnp.int32, (CHUNK, CHUNK), 0)
            c = lax.broadcasted_iota(jnp.int32, (CHUNK, CHUNK), 1)
            for g in range(GM_GROUPS):
                dw_ref[g] = jnp.where(c <= r, dw_ref[g], 0.0)
                db_ref[g] = jnp.broadcast_to(jnp.sum(db_ref[g], axis=-1, keepdims=True), (CHUNK, CHUNK))

    small = lambda shape: pl.BlockSpec(shape, lambda i: (0,) * len(shape))
    res, rode = _ride_call(
        body, "gmlp_bwd", (nsteps,),
        in_specs=[pl.BlockSpec((tm, 512), lambda i: (i, 0)), pl.BlockSpec((tm, 512), lambda i: (i, 1)),
                  pl.BlockSpec((tm, 512), lambda i: (i, 0)), small((1, 512)), small((4, 128, 128)),
                  small((4, 128, 128)), small((4, 128, 128)), small((1, 512))],
        out_specs=[pl.BlockSpec((tm, 1024), lambda i: (i, 0)), small((4, 128, 128)), small((4, 128, 128)),
                   small((1, 512)), small((1, 512))],
        out_shape=[jax.ShapeDtypeStruct((t, IN_COLS), BF), jax.ShapeDtypeStruct((4, 128, 128), F32),
                   jax.ShapeDtypeStruct((4, 128, 128), F32), jax.ShapeDtypeStruct((1, 512), F32),
                   jax.ShapeDtypeStruct((1, 512), F32)],
        scratch_shapes=[], operands=(proj, proj, dmerged, gg, wt, wtt, bb, hg), vmem_mb=32, ride=ride)
    return (*res, rode)


def _other_chips(x, y):
    return ((1 - x, y), (x, 1 - y), (1 - x, 1 - y))


class _GatherExchange:
    def __init__(self, shards):
        n = len(shards)
        self.n = n
        self.in_arrays = list(shards)
        self.out_shape = [jax.ShapeDtypeStruct((N_CHIPS,) + a.shape, a.dtype) for a in shards]
        self.half_rows = [a.shape[0] // 2 for a in shards]
        sems = lambda k: pltpu.SemaphoreType.DMA((k,))
        self.scratch_shapes = [pltpu.VMEM(a.shape, a.dtype) for a in shards] + [
            sems(3 * n), sems(3 * n), sems(3 * n), sems(3 * n), sems(n), sems(n)]

    def _copies(self, ins, outs, scr):
        n = self.n
        stages, (ici_send, ici_recv, d2d_send, d2d_recv, ld_sems, st_sems) = scr[:n], scr[n:]
        x, y, c = lax.axis_index("x"), lax.axis_index("y"), lax.axis_index("c")
        q = 2 * x + y
        loads = [pltpu.make_async_copy(ins[w], stages[w], ld_sems.at[w]) for w in range(n)]
        stores = [pltpu.make_async_copy(stages[w], outs[w].at[q], st_sems.at[w]) for w in range(n)]
        ici, d2d = [], []
        for w in range(n):
            half = pl.ds(c * self.half_rows[w], self.half_rows[w])
            for k, (px, py) in enumerate(_other_chips(x, y)):
                ici.append(pltpu.make_async_remote_copy(
                    src_ref=ins[w].at[half], dst_ref=outs[w].at[q, half], send_sem=ici_send.at[3 * w + k],
                    recv_sem=ici_recv.at[3 * w + k], device_id=(px, py, c), device_id_type=MESH))
                landed = outs[w].at[2 * px + py, half]
                d2d.append(pltpu.make_async_remote_copy(
                    src_ref=landed, dst_ref=landed, send_sem=d2d_send.at[3 * w + k],
                    recv_sem=d2d_recv.at[3 * w + k], device_id=(x, y, 1 - c), device_id_type=MESH))
        return loads, stores, ici, d2d

    def start(self, ins, outs, scr):
        loads, stores, ici, _ = self._copies(ins, outs, scr)
        for cp in loads + ici:
            cp.start()
        for ld, st in zip(loads, stores):
            ld.wait()
            st.start()

    def relay(self, ins, outs, scr):
        _, _, ici, d2d = self._copies(ins, outs, scr)
        for got, fwd in zip(ici, d2d):
            got.wait_recv()
            fwd.start()

    def finish(self, ins, outs, scr):
        _, stores, ici, d2d = self._copies(ins, outs, scr)
        for cp in ici:
            cp.wait_send()
        for cp in d2d + stores:
            cp.wait()


class _SiblingExchange:
    def __init__(self, slabs):
        n = len(slabs)
        self.n = n
        self.in_arrays = list(slabs)
        self.out_shape = [jax.ShapeDtypeStruct((N_CHIPS,) + a.shape[1:], a.dtype) for a in slabs]
        self.scratch_shapes = [pltpu.SemaphoreType.DMA((4 * n,)), pltpu.SemaphoreType.DMA((4 * n,))]

    def _copies(self, ins, outs, scr):
        send_sems, recv_sems = scr
        x, y, c = lax.axis_index("x"), lax.axis_index("y"), lax.axis_index("c")
        return [pltpu.make_async_remote_copy(
            src_ref=ins[w].at[2 * p + (1 - c)], dst_ref=outs[w].at[p], send_sem=send_sems.at[4 * w + p],
            recv_sem=recv_sems.at[4 * w + p], device_id=(x, y, 1 - c), device_id_type=MESH)
            for w in range(self.n) for p in range(N_CHIPS)]

    def start(self, ins, outs, scr):
        for cp in self._copies(ins, outs, scr):
            cp.start()

    def finish(self, ins, outs, scr):
        for cp in self._copies(ins, outs, scr):
            cp.wait()


class _ChipExchange:
    def __init__(self, sums):
        n = len(sums)
        self.n = n
        self.in_arrays = list(sums)
        self.out_shape = [jax.ShapeDtypeStruct(a.shape, a.dtype) for a in sums]
        self.scratch_shapes = [pltpu.SemaphoreType.DMA((3 * n,)), pltpu.SemaphoreType.DMA((3 * n,))]

    def _copies(self, ins, outs, scr):
        send_sems, recv_sems = scr
        x, y, c = lax.axis_index("x"), lax.axis_index("y"), lax.axis_index("c")
        q = 2 * x + y
        return [pltpu.make_async_remote_copy(
            src_ref=ins[w].at[2 * px + py], dst_ref=outs[w].at[q], send_sem=send_sems.at[3 * w + k],
            recv_sem=recv_sems.at[3 * w + k], device_id=(px, py, c), device_id_type=MESH)
            for w in range(self.n) for k, (px, py) in enumerate(_other_chips(x, y))]

    def start(self, ins, outs, scr):
        for cp in self._copies(ins, outs, scr):
            cp.start()

    def finish(self, ins, outs, scr):
        for cp in self._copies(ins, outs, scr):
            cp.wait()


class _NoExchange:
    in_arrays, out_shape, scratch_shapes = (), (), ()

    def start(self, ins, outs, scr):
        pass

    def finish(self, ins, outs, scr):
        pass


def _run_exchange(ex, name):
    n_in, n_out = len(ex.in_arrays), len(ex.out_shape)

    def body(*refs):
        ins, outs, scr = refs[:n_in], refs[n_in:n_in + n_out], refs[n_in + n_out:]
        ex.start(ins, outs, scr)
        if hasattr(ex, "relay"):
            ex.relay(ins, outs, scr)
        ex.finish(ins, outs, scr)

    return pl.pallas_call(
        body, name=name, in_specs=[ANY] * n_in, out_specs=[ANY] * n_out, out_shape=ex.out_shape,
        scratch_shapes=ex.scratch_shapes, compiler_params=_params(24),
    )(*ex.in_arrays)


def _ride_call(body, name, grid, in_specs, out_specs, out_shape, scratch_shapes, operands, vmem_mb, ride=None,
               aliases=None):
    ride = ride or _NoExchange()
    ni, no, ns = len(in_specs), len(out_specs), len(scratch_shapes)
    ri, ro = len(ride.in_arrays), len(ride.out_shape)
    total = math.prod(grid)

    def wrapped(*refs):
        ins, rins = refs[:ni], refs[ni:ni + ri]
        outs, routs = refs[ni + ri:ni + ri + no], refs[ni + ri + no:ni + ri + no + ro]
        scr, rscr = refs[ni + ri + no + ro:ni + ri + no + ro + ns], refs[ni + ri + no + ro + ns:]
        step = pl.program_id(0)
        for ax in range(1, len(grid)):
            step = step * grid[ax] + pl.program_id(ax)

        @pl.when(step == 0)
        def _():
            ride.start(rins, routs, rscr)

        if hasattr(ride, "relay"):
            @pl.when(step == (3 * total) // 4)
            def _():
                ride.relay(rins, routs, rscr)

        body(*ins, *outs, *scr)

        @pl.when(step == total - 1)
        def _():
            ride.finish(rins, routs, rscr)

    res = pl.pallas_call(
        wrapped, name=name, grid=grid, in_specs=list(in_specs) + [ANY] * ri, out_specs=list(out_specs) + [ANY] * ro,
        out_shape=list(out_shape) + list(ride.out_shape),
        scratch_shapes=list(scratch_shapes) + list(ride.scratch_shapes), input_output_aliases=aliases or {},
        compiler_params=_params(vmem_mb, ("arbitrary",) * len(grid)),
    )(*_in_hbm(*operands), *ride.in_arrays)
    return res[:no], res[no:]


def _neg_log_sig(z):
    n = jnp.maximum(z, 0.0) + jnp.log(1.0 + jnp.exp(-jnp.abs(z)))
    return n, z - n


def _running_sums(n, tri2):
    hi = n.astype(BF)
    lo = (n - hi.astype(F32)).astype(BF)
    return _dot(jnp.concatenate([hi, lo], axis=1), tri2)


def _head_sums(x, h0):
    s0 = jnp.sum(jnp.where(h0, x, 0.0), axis=-1, keepdims=True)
    s1 = jnp.sum(jnp.where(h0, 0.0, x), axis=-1, keepdims=True)
    return jnp.where(h0, s0, s1)


SB_BLOCKS_PER_STEP = 4
SB_PAIRS_PER_STEP = 2


def _sb_masks(tq):
    h0 = lax.broadcasted_iota(jnp.int32, (tq, 128), 1) < HEAD_LANES
    r = lax.broadcasted_iota(jnp.int32, (2 * tq, tq), 0)
    c = lax.broadcasted_iota(jnp.int32, (2 * tq, tq), 1)
    return h0, c < jnp.where(r >= tq, r - tq, r)


def _sb_stack(x, h0):
    zero = jnp.zeros_like(x)
    return jnp.concatenate([jnp.where(h0, x, zero), jnp.where(h0, zero, x)], axis=0)


def _tri(tq, op):
    return op(lax.broadcasted_iota(jnp.int32, (tq, tq), 0), lax.broadcasted_iota(jnp.int32, (tq, tq), 1)).astype(BF)


def _sb_fwd(proj, merged, hg, nb, s, tq, ride=None):
    t = nb * s
    tq = min(tq, s)
    nq = s // tq
    per = min(SB_BLOCKS_PER_STEP, nq)
    ns = nq // per
    gp, ng, w = SB_PAIRS_PER_STEP, 4 // SB_PAIRS_PER_STEP, 128 * SB_PAIRS_PER_STEP

    def body(q_ref, k_ref, v_ref, hg_ref, merged_ref, o_ref, tot_ref, mb_ref, nblk_ref, acc, cr, c_min):
        del merged_ref
        h0, causal = _sb_masks(tq)
        tri_gt = _tri(tq, lambda r, c: r > c)
        tri_gt = jnp.concatenate([tri_gt, tri_gt], axis=0)
        lanes = [slice(g * 128, (g + 1) * 128) for g in range(gp)]
        zeros = jnp.zeros((2 * tq, 1), F32)

        def query_block(i, rows):
            qsts = [_sb_stack(q_ref[rows, lanes[g]] * SB_SCALE, h0) for g in range(gp)]

            def block(g, j, masked, c_in):
                start = pl.multiple_of(j * tq, tq)
                kj = k_ref[pl.ds(start, tq), lanes[g]]
                vj = v_ref[pl.ds(start, tq), lanes[g]]
                n, l = _neg_log_sig(_dot_bt(qsts[g], kj))
                if masked:
                    n = jnp.where(causal, n, 0.0)
                a = jnp.exp(l - (_running_sums(n, tri_gt) + c_in))
                if masked:
                    a = jnp.where(causal, a, 0.0)
                return _dot(a.astype(BF), vj), c_in + jnp.sum(n, axis=-1, keepdims=True)

            def keep(parts):
                for g, (p, c) in enumerate(parts):
                    acc[g] = p
                    cr[g] = c
                c_min[0] = jnp.min(functools.reduce(jnp.minimum, [c for _, c in parts]))

            @pl.when(i == 0)
            def _():
                keep([block(g, 0, True, zeros) for g in range(gp)])

            @pl.when(i > 0)
            def _():
                diag = [block(g, i, True, zeros) for g in range(gp)]
                prev = [block(g, i - 1, False, diag[g][1]) for g in range(gp)]
                keep([(diag[g][0] + prev[g][0], prev[g][1]) for g in range(gp)])

            def cond(carry):
                return jnp.logical_and(carry[0] < i, carry[1] < -SB_SKIP)

            def step(carry):
                more = [block(g, i - 1 - carry[0], False, cr[g]) for g in range(gp)]
                for g, (p, c) in enumerate(more):
                    acc[g] += p
                    cr[g] = c
                return carry[0] + 1, jnp.min(functools.reduce(jnp.minimum, [c for _, c in more]))

            walked, _ = lax.while_loop(cond, step, (jnp.minimum(i, 1), c_min[0]))
            return walked

        for u in range(per):
            rows = slice(u * tq, (u + 1) * tq)
            walked = query_block(pl.program_id(2) * per + u, rows)
            for g in range(gp):
                o = jnp.where(h0, acc[g, 0:tq, :], acc[g, tq:2 * tq, :])
                o_ref[rows, lanes[g]] = o
                tot_ref[rows, lanes[g]] = jnp.where(h0, cr[g, 0:tq, :], cr[g, tq:2 * tq, :])
                ro = lax.rsqrt(_head_sums(o * o, h0) * (1.0 / HEAD_LANES) + EPS)
                mb_ref[rows, lanes[g]] = (o * ro * hg_ref[:, lanes[g]]).astype(BF)
            nblk_ref[u * 8:(u + 1) * 8, :] = jnp.full((8, 128), walked.astype(F32))

    blk = lambda col0: pl.BlockSpec((per * tq, w), lambda b, hg_, i: (b * ns + i, col0 + hg_))
    seq = lambda col0: pl.BlockSpec((s, w), lambda b, hg_, i: (b, col0 + hg_))
    first = 1024 // w
    (o, tot, mb, nblk), rode = _ride_call(
        body, "sb_fwd", (nb, ng, ns),
        in_specs=[blk(first), seq(first + ng), seq(first + 2 * ng),
                  pl.BlockSpec((1, w), lambda b, hg_, i: (0, ng + hg_)), ANY],
        out_specs=[blk(0), blk(0), blk(ng),
                   pl.BlockSpec((None, None, per * 8, 128), lambda b, hg_, i: (b, hg_, i, 0))],
        out_shape=[jax.ShapeDtypeStruct((t, 512), F32), jax.ShapeDtypeStruct((t, 512), F32),
                   jax.ShapeDtypeStruct((t, 1024), BF), jax.ShapeDtypeStruct((nb, ng, nq * 8, 128), F32)],
        scratch_shapes=[pltpu.VMEM((gp, 2 * tq, 128), F32), pltpu.VMEM((gp, 2 * tq, 1), F32),
                        pltpu.SMEM((1,), F32)],
        operands=(proj, proj, proj, hg, merged), vmem_mb=40, ride=ride, aliases={4: 2})
    return o, tot, mb, nblk, rode


def _sb_bwd(proj, o_sb, tot, nblk, dmerged, dproj, hg, nb, s, tq, ride=None):
    t = nb * s
    tq = min(tq, s)
    nq = s // tq
    per = min(SB_BLOCKS_PER_STEP, nq)
    ns = nq // per
    gp, ng, w = SB_PAIRS_PER_STEP, 4 // SB_PAIRS_PER_STEP, 128 * SB_PAIRS_PER_STEP

    def body(q_ref, k_ref, v_ref, o_ref, tot_ref, nblk_ref, dm_ref, hg_ref, dproj_ref,
             dq_ref, dk_ref, dv_ref, dhg_ref, dk_acc, dv_acc, dq_acc, cm, cg):
        del dproj_ref
        h0, causal = _sb_masks(tq)
        tri_le = _tri(tq, lambda r, c: r <= c)
        tri_le = jnp.concatenate([tri_le, tri_le], axis=0)
        tri_lt = _tri(tq, lambda r, c: r < c)
        lanes = [slice(g * 128, (g + 1) * 128) for g in range(gp)]

        @pl.when(pl.program_id(2) == 0)
        def _():
            dk_acc[...] = jnp.zeros_like(dk_acc)
            dv_acc[...] = jnp.zeros_like(dv_acc)
            dhg_ref[...] = jnp.zeros_like(dhg_ref)

        def query_block(i, rows):
            for ref in (dq_acc, cm, cg):
                ref[...] = jnp.zeros_like(ref)
            qsts, dosts, tots = [], [], []
            for g in range(gp):
                qsts.append(_sb_stack(q_ref[rows, lanes[g]] * SB_SCALE, h0))
                o = o_ref[rows, lanes[g]]
                ro = lax.rsqrt(_head_sums(o * o, h0) * (1.0 / HEAD_LANES) + EPS)
                oh = o * ro
                dm = dm_ref[rows, lanes[g]].astype(F32)
                dhg_ref[:, lanes[g]] += jnp.sum(dm * oh, axis=0, keepdims=True)
                doh = dm * hg_ref[:, lanes[g]]
                do = ro * (doh - oh * (_head_sums(doh * oh, h0) * (1.0 / HEAD_LANES)))
                dosts.append(_sb_stack(do.astype(BF), h0))
                first = g * 128
                tots.append(jnp.concatenate(
                    [tot_ref[rows, first:first + 1], tot_ref[rows, first + HEAD_LANES:first + HEAD_LANES + 1]], axis=0))
            qsts_t = [q.T for q in qsts]
            dosts_t = [d.T for d in dosts]

            def block(g, j, masked, cm_in, cg_in):
                start = pl.multiple_of(j * tq, tq)
                kj = k_ref[pl.ds(start, tq), lanes[g]]
                vj = v_ref[pl.ds(start, tq), lanes[g]]
                n, l = _neg_log_sig(_dot_bt(qsts[g], kj))
                if masked:
                    n = jnp.where(causal, n, 0.0)
                a = jnp.exp(l - (tots[g] - cm_in - _running_sums(n, tri_le)))
                if masked:
                    a = jnp.where(causal, a, 0.0)
                gm = a * _dot_bt(dosts[g], vj)
                pp = cg_in + _dot(gm.astype(BF), tri_lt)
                dz = gm - jnp.exp(l) * (gm + pp)
                if masked:
                    dz = jnp.where(causal, dz, 0.0)
                dzb = dz.astype(BF)
                dk_acc[g, :, pl.ds(start, tq)] += _dot(qsts_t[g], dzb)
                dv_acc[g, :, pl.ds(start, tq)] += _dot(dosts_t[g], a.astype(BF))
                return (_dot(dzb, kj), cm_in + jnp.sum(n, axis=-1, keepdims=True),
                        cg_in + jnp.sum(gm, axis=-1, keepdims=True))

            def step(j, carry):
                for g in range(gp):
                    dq, cm[g], cg[g] = block(g, j, False, cm[g], cg[g])
                    dq_acc[g] += dq
                return carry

            walked = jnp.clip(nblk_ref[pl.program_id(0), pl.program_id(1), i].astype(jnp.int32),
                              jnp.minimum(i, 1), i)
            lax.fori_loop(i - walked, i - 1, step, 0)

            @pl.when(i == 0)
            def _():
                for g in range(gp):
                    dq_acc[g] = block(g, 0, True, cm[g], cg[g])[0]

            @pl.when(i > 0)
            def _():
                prev = [block(g, i - 1, False, cm[g], cg[g]) for g in range(gp)]
                diag = [block(g, i, True, prev[g][1], prev[g][2]) for g in range(gp)]
                for g in range(gp):
                    dq_acc[g] += prev[g][0] + diag[g][0]

            for g in range(gp):
                dq = jnp.where(h0, dq_acc[g, 0:tq, :], dq_acc[g, tq:2 * tq, :])
                dq_ref[rows, lanes[g]] = (dq * SB_SCALE).astype(BF)

        for u in range(per):
            query_block(pl.program_id(2) * per + u, slice(u * tq, (u + 1) * tq))

        @pl.when(pl.program_id(2) == ns - 1)
        def _():
            for g in range(gp):
                dk_ref[:, lanes[g]] = dk_acc[g].T.astype(BF)
                dv_ref[:, lanes[g]] = dv_acc[g].T.astype(BF)

    blk = lambda col0: pl.BlockSpec((per * tq, w), lambda b, hg_, i: (b * ns + i, col0 + hg_))
    seq = lambda col0: pl.BlockSpec((s, w), lambda b, hg_, i: (b, col0 + hg_))
    first = 1024 // w
    (dq, dk, dv, dhg), rode = _ride_call(
        body, "sb_bwd", (nb, ng, ns),
        in_specs=[blk(first), seq(first + ng), seq(first + 2 * ng), blk(0), blk(0),
                  pl.BlockSpec(memory_space=pltpu.SMEM), blk(ng),
                  pl.BlockSpec((1, w), lambda b, hg_, i: (0, ng + hg_)), ANY],
        out_specs=[blk(first), seq(0), seq(0), pl.BlockSpec((None, 1, w), lambda b, hg_, i: (b, 0, hg_))],
        out_shape=[jax.ShapeDtypeStruct((t, IN_COLS), BF), jax.ShapeDtypeStruct((t, 512), BF),
                   jax.ShapeDtypeStruct((t, 512), BF), jax.ShapeDtypeStruct((nb, 1, 512), F32)],
        scratch_shapes=[pltpu.VMEM((gp, 128, s), F32), pltpu.VMEM((gp, 128, s), F32),
                        pltpu.VMEM((gp, 2 * tq, 128), F32), pltpu.VMEM((gp, 2 * tq, 1), F32),
                        pltpu.VMEM((gp, 2 * tq, 1), F32)],
        operands=(proj, proj, proj, o_sb, tot, nblk.reshape(nb, ng, nq, 8, 128)[:, :, :, 0, 0], dmerged, hg, dproj),
        vmem_mb=48, ride=ride, aliases={8: 0})
    return dq, dk, dv, dhg, rode


def _place(buf, piece, col_block, name):
    t, w = piece.shape
    tm = min(t, 1024)

    def body(piece_ref, buf_ref, out_ref):
        del buf_ref
        out_ref[...] = piece_ref[...]

    return pl.pallas_call(
        body, name=name, grid=(t // tm,),
        in_specs=[pl.BlockSpec((tm, w), lambda i: (i, 0)), ANY],
        out_specs=pl.BlockSpec((tm, w), lambda i: (i, col_block)),
        out_shape=jax.ShapeDtypeStruct(buf.shape, buf.dtype), input_output_aliases={1: 0},
        compiler_params=_params(16, ("arbitrary",)),
    )(piece, buf)


def _softmax_rows(sc):
    e = jnp.exp(sc - jnp.max(sc, axis=-1, keepdims=True))
    return e / jnp.sum(e, axis=-1, keepdims=True)


def _mix_cross_fwd(x, merged, w_out, gc, w_cq, kv, w_co, s, tm):
    t, d = x.shape
    tm = min(tm, s)
    per = s // tm
    inv = 1.0 / math.sqrt(X_HEAD_DIM)

    def body(x_ref, m_ref, wo_ref, gc_ref, wq_ref, kv_ref, wc_ref, h1_ref, h2_ref, hn_ref, qc_ref, oc_ref):
        h1 = x_ref[...] + _dot(m_ref[...], wo_ref[...])
        h1_ref[...] = h1
        hn = (h1 * _rs(h1) * gc_ref[...]).astype(BF)
        hn_ref[...] = hn
        qc = _dot(hn, wq_ref[...]).astype(BF)
        qc_ref[...] = qc
        for h in range(X_HEADS):
            cols = slice(h * X_HEAD_DIM, (h + 1) * X_HEAD_DIM)
            kh = kv_ref[:, h * X_HEAD_DIM:(h + 1) * X_HEAD_DIM]
            vh = kv_ref[:, d + h * X_HEAD_DIM:d + (h + 1) * X_HEAD_DIM]
            p = _softmax_rows(_dot_bt(qc[:, cols], kh) * inv)
            oc_ref[:, cols] = _dot(p.astype(BF), vh).astype(BF)
        h2_ref[...] = h1 + _dot(oc_ref[...], wc_ref[...])

    row = lambda width: pl.BlockSpec((tm, width), lambda i: (i, 0))
    full = lambda a, b: pl.BlockSpec((a, b), lambda i: (0, 0))
    return pl.pallas_call(
        body, name="mix_cross_fwd", grid=(t // tm,),
        in_specs=[row(d), row(d), full(d, d), full(1, d), full(d, d),
                  pl.BlockSpec((N_MEM, 2 * d), lambda i: (i // per, 0)), full(d, d)],
        out_specs=[row(d), row(d), row(d), row(d), row(d)],
        out_shape=[jax.ShapeDtypeStruct((t, d), F32), jax.ShapeDtypeStruct((t, d), F32),
                   jax.ShapeDtypeStruct((t, d), BF), jax.ShapeDtypeStruct((t, d), BF),
                   jax.ShapeDtypeStruct((t, d), BF)],
        compiler_params=_params(48, ("arbitrary",)),
    )(*_in_hbm(x, merged, w_out, gc, w_cq, kv, w_co))


def _cross_bwd(dh2, h1, qc, gc, w_cq, kv, w_co, s, tm):
    t, d = dh2.shape
    tm = min(tm, s)
    per = s // tm
    nb = t // s
    inv = 1.0 / math.sqrt(X_HEAD_DIM)

    def body(dh2_ref, h1_ref, qc_ref, gc_ref, wq_ref, kv_ref, wc_ref, dh1_ref, dqc_ref, dkv_ref, dgc_ref):
        i = pl.program_id(0)

        @pl.when(i == 0)
        def _():
            dgc_ref[...] = jnp.zeros_like(dgc_ref)

        @pl.when(i % per == 0)
        def _():
            dkv_ref[...] = jnp.zeros_like(dkv_ref)

        dh2 = dh2_ref[...]
        h1 = h1_ref[...]
        r = _rs(h1)
        h1h = h1 * r
        gcv = gc_ref[...]
        qc = qc_ref[...]
        do = _dot_bt(dh2.astype(BF), wc_ref[...]).astype(BF)
        for h in range(X_HEADS):
            cols = slice(h * X_HEAD_DIM, (h + 1) * X_HEAD_DIM)
            vcols = slice(d + h * X_HEAD_DIM, d + (h + 1) * X_HEAD_DIM)
            kh = kv_ref[:, cols]
            vh = kv_ref[:, vcols]
            p = _softmax_rows(_dot_bt(qc[:, cols], kh) * inv)
            dp = _dot_bt(do[:, cols], vh)
            ds = (p * (dp - jnp.sum(dp * p, axis=-1, keepdims=True)) * inv).astype(BF)
            dqc_ref[:, cols] = _dot(ds, kh).astype(BF)
            dkv_ref[:, cols] += _dot_at(ds, qc[:, cols])
            dkv_ref[:, vcols] += _dot_at(p.astype(BF), do[:, cols])
        dhn = _dot_bt(dqc_ref[...], wq_ref[...])
        dx, dg = _rms_bwd(dhn, h1h, r, gcv)
        dh1_ref[...] = dh2 + dx
        dgc_ref[...] += jnp.sum(dg, axis=0, keepdims=True)

    row = lambda width: pl.BlockSpec((tm, width), lambda i: (i, 0))
    full = lambda a, b: pl.BlockSpec((a, b), lambda i: (0, 0))
    kvspec = pl.BlockSpec((N_MEM, 2 * d), lambda i: (i // per, 0))
    return pl.pallas_call(
        body, name="cross_bwd", grid=(t // tm,),
        in_specs=[row(d), row(d), row(d), full(1, d), full(d, d), kvspec, full(d, d)],
        out_specs=[row(d), row(d), kvspec, full(1, d)],
        out_shape=[jax.ShapeDtypeStruct((t, d), F32), jax.ShapeDtypeStruct((t, d), BF),
                   jax.ShapeDtypeStruct((nb * N_MEM, 2 * d), F32), jax.ShapeDtypeStruct((1, d), F32)],
        compiler_params=_params(48, ("arbitrary",)),
    )(*_in_hbm(dh2, h1, qc, gc, w_cq, kv, w_co))


def _mem_bwd(mem, gm, dkv, w_ckv, tm):
    t, d = mem.shape
    tm = min(tm, t)

    def body(mem_ref, dkv_ref, w_ref, dg_ref):
        @pl.when(pl.program_id(0) == 0)
        def _():
            dg_ref[...] = jnp.zeros_like(dg_ref)

        mv = mem_ref[...]
        dmn = _dot_bt(dkv_ref[...].astype(BF), w_ref[...])
        dg_ref[...] += jnp.sum(dmn * (mv * _rs(mv)), axis=0, keepdims=True)

    del gm
    return pl.pallas_call(
        body, name="mem_bwd", grid=(t // tm,),
        in_specs=[pl.BlockSpec((tm, d), lambda i: (i, 0)), pl.BlockSpec((tm, 2 * d), lambda i: (i, 0)),
                  pl.BlockSpec((d, 2 * d), lambda i: (0, 0))],
        out_specs=pl.BlockSpec((1, d), lambda i: (0, 0)),
        out_shape=jax.ShapeDtypeStruct((1, d), F32),
        compiler_params=_params(32, ("arbitrary",)),
    )(mem, dkv, w_ckv)


def _ffn_loss_fwd(h2, gf, w1, w2, gl, target, tm):
    t, d = h2.shape
    tm = min(tm, t)

    def body(h2_ref, gf_ref, w1_ref, w2_ref, gl_ref, tg_ref, hn_ref, f_ref, dh3_ref, dgl_ref, loss_ref):
        @pl.when(pl.program_id(0) == 0)
        def _():
            dgl_ref[...] = jnp.zeros_like(dgl_ref)
            loss_ref[...] = jnp.zeros_like(loss_ref)

        h2 = h2_ref[...]
        hn = (h2 * _rs(h2) * gf_ref[...]).astype(BF)
        hn_ref[...] = hn
        h3 = h2
        for c in range(4):
            f = jnp.maximum(_dot(hn, w1_ref[c]), 0.0)
            f_ref[:, c * 1024:(c + 1) * 1024] = f.astype(BF)
            h3 = h3 + _dot((f * f).astype(BF), w2_ref[c])
        r3 = _rs(h3)
        yh = h3 * r3
        glv = gl_ref[...]
        e = yh * glv - tg_ref[...]
        loss_ref[...] += 0.5 * jnp.sum(jnp.sum(e * e, axis=-1, keepdims=True) * (1.0 / d), axis=0, keepdims=True)
        dy = e * (1.0 / d)
        dx, dg = _rms_bwd(dy, yh, r3, glv)
        dh3_ref[...] = dx
        dgl_ref[...] += jnp.sum(dg, axis=0, keepdims=True)

    row = lambda width: pl.BlockSpec((tm, width), lambda i: (i, 0))
    return pl.pallas_call(
        body, name="ffn_loss_fwd", grid=(t // tm,),
        in_specs=[row(d), pl.BlockSpec((1, d), lambda i: (0, 0)), pl.BlockSpec((4, d, 1024), lambda i: (0, 0, 0), pipeline_mode=pl.Buffered(1)),
                  pl.BlockSpec((4, 1024, d), lambda i: (0, 0, 0), pipeline_mode=pl.Buffered(1)),
                  pl.BlockSpec((1, d), lambda i: (0, 0)), row(d)],
        out_specs=[row(d), row(D_FF), row(d), pl.BlockSpec((1, d), lambda i: (0, 0)),
                   pl.BlockSpec((1, 1), lambda i: (0, 0))],
        out_shape=[jax.ShapeDtypeStruct((t, d), BF), jax.ShapeDtypeStruct((t, D_FF), BF),
                   jax.ShapeDtypeStruct((t, d), F32), jax.ShapeDtypeStruct((1, d), F32),
                   jax.ShapeDtypeStruct((1, 1), F32)],
        compiler_params=_params(56, ("arbitrary",)),
    )(*_in_hbm(h2, gf, w1, w2, gl, target))


def _ffn_bwd(dh3, f, h2, gf, w1, w2, tm):
    t, d = h2.shape
    tm = min(tm, t)

    def body(dh3_ref, f_ref, h2_ref, gf_ref, w1_ref, w2_ref, dh2_ref, dpre_ref, dgf_ref):
        @pl.when(pl.program_id(0) == 0)
        def _():
            dgf_ref[...] = jnp.zeros_like(dgf_ref)

        dh3 = dh3_ref[...]
        dh3b = dh3.astype(BF)
        dhn = jnp.zeros((tm, d), F32)
        for c in range(4):
            cols = slice(c * 1024, (c + 1) * 1024)
            dpre = (_dot_bt(dh3b, w2_ref[c]) * (2.0 * f_ref[:, cols].astype(F32))).astype(BF)
            dpre_ref[:, cols] = dpre
            dhn = dhn + _dot_bt(dpre, w1_ref[c])
        h2 = h2_ref[...]
        r = _rs(h2)
        dx, dg = _rms_bwd(dhn, h2 * r, r, gf_ref[...])
        dh2_ref[...] = dh3 + dx
        dgf_ref[...] += jnp.sum(dg, axis=0, keepdims=True)

    row = lambda width: pl.BlockSpec((tm, width), lambda i: (i, 0))
    return pl.pallas_call(
        body, name="ffn_bwd", grid=(t // tm,),
        in_specs=[row(d), row(D_FF), row(d), pl.BlockSpec((1, d), lambda i: (0, 0)),
                  pl.BlockSpec((4, d, 1024), lambda i: (0, 0, 0), pipeline_mode=pl.Buffered(1)),
                  pl.BlockSpec((4, 1024, d), lambda i: (0, 0, 0), pipeline_mode=pl.Buffered(1))],
        out_specs=[row(d), row(D_FF), pl.BlockSpec((1, d), lambda i: (0, 0))],
        out_shape=[jax.ShapeDtypeStruct((t, d), F32), jax.ShapeDtypeStruct((t, D_FF), BF),
                   jax.ShapeDtypeStruct((1, d), F32)],
        compiler_params=_params(56, ("arbitrary",)),
    )(*_in_hbm(dh3, f, h2, gf, w1, w2))


def _in_bwd(dproj, dh1, x, g, w_in, tm, ride=None):
    t, d = x.shape
    n = w_in.shape[1]
    tm = min(tm, t)

    def body(dp_ref, dh1_ref, x_ref, g_ref, w_ref, dx_ref, dg_ref):
        @pl.when(pl.program_id(0) == 0)
        def _():
            dg_ref[...] = jnp.zeros_like(dg_ref)

        dxn = _dot_bt(dp_ref[...], w_ref[...])
        xv = x_ref[...]
        r = _rs(xv)
        dx, dg = _rms_bwd(dxn, xv * r, r, g_ref[...])
        dx_ref[...] = dh1_ref[...] + dx
        dg_ref[...] += jnp.sum(dg, axis=0, keepdims=True)

    row = lambda width: pl.BlockSpec((tm, width), lambda i: (i, 0))
    (dx, dg), rode = _ride_call(
        body, "in_bwd", (t // tm,),
        in_specs=[row(n), row(d), row(d), pl.BlockSpec((1, d), lambda i: (0, 0)),
                  pl.BlockSpec((d, n), lambda i: (0, 0))],
        out_specs=[row(d), pl.BlockSpec((1, d), lambda i: (0, 0))],
        out_shape=[jax.ShapeDtypeStruct((t, d), F32), jax.ShapeDtypeStruct((1, d), F32)],
        scratch_shapes=[], operands=(dproj, dh1, x, g, w_in), vmem_mb=48, ride=ride)
    return dx, dg, rode


class _GradReduce:
    def __init__(self, c_idx):
        self.c_idx = c_idx
        self.sums = {}

    def sibling(self, slabs):
        return _SiblingExchange(slabs)

    def chip(self, names, slabs, recv):
        for k, a, r in zip(names, slabs, recv):
            self.sums[k] = _chip_sum(a, r, self.c_idx, "chip_sum_" + k)
        return _ChipExchange([self.sums[k] for k in names])


def _full_weights(gathered):
    d = D_MODEL
    out = {}
    for k, a in gathered.items():
        if k in ("w_in", "w_ckv", "w_ff1"):
            out[k] = a.transpose(1, 0, 2).reshape(d, -1)
        else:
            out[k] = a.reshape(-1, d)
    return out


def _slabs(a):
    return a.reshape(N_DEV, -1, a.shape[-1])


def _local_step(x, mem, target, small, big, nb, s, tq=256, gather_rest=None, reduce=None):
    d = D_MODEL
    g_mix, g_v, w_sp, b_sp, g_head, g_cross, g_mem, g_ffn, g_fin = (
        small[k] for k in ("norm_mix_g", "gm_v_norm_g", "w_spatial", "b_spatial", "head_norm_g", "norm_cross_g",
                           "norm_mem_g", "norm_ffn_g", "norm_final_g"))
    tri = jnp.tril(jnp.ones((CHUNK, CHUNK), dtype=bool))
    w_sp_m = jnp.where(tri[None], w_sp, 0.0)
    wt = w_sp_m.astype(BF)
    wtt = jnp.swapaxes(w_sp_m, 1, 2).astype(BF)
    bb = jnp.broadcast_to(b_sp[:, :, None], (GM_GROUPS, CHUNK, CHUNK))
    hg_a = g_head[:, :GM_WIDTH]

    proj, xn = _norm_matmul(x, g_mix, big["w_in"], 512, "in_proj")
    merged = _gmlp_fwd(proj, g_v, wt, bb, hg_a, 512)
    o_sb, tot, merged, nblk, gathered = _sb_fwd(proj, merged, g_head, nb, s, tq, ride=gather_rest)
    if gather_rest is not None:
        big = dict(big, **_full_weights(dict(zip(BIG[1:], gathered))))
    w1c = big["w_ff1"].reshape(d, 4, 1024).transpose(1, 0, 2)
    w2c = big["w_ff2"].reshape(4, 1024, d)
    kv, memn = _norm_matmul(mem, g_mem, big["w_ckv"], 512, "mem_proj")
    h1, h2, hn, qc, oc = _mix_cross_fwd(x, merged, big["w_out"], g_cross, big["w_cq"], kv, big["w_co"], s, 512)
    hn2, f, dh3, d_fin, loss = _ffn_loss_fwd(h2, g_ffn, w1c, w2c, g_fin, target, 512)

    gbig = {}
    dh2, dpre, d_ffn = _ffn_bwd(dh3, f, h2, g_ffn, w1c, w2c, 512)
    gbig["w_ff2"] = _slabs(_wgrad(f, dh3, 1024, 512, "wgrad_ff2", square_a=True, tm=2048))
    gbig["w_ff1"] = _slabs(_wgrad_wide(hn2, dpre, 1024, 512, "wgrad_ff1", col_shards=4))
    dh1, dqc, dkv, d_cross = _cross_bwd(dh2, h1, qc, g_cross, big["w_cq"], kv, big["w_co"], s, 512)
    gbig["w_co"] = _slabs(_wgrad(oc, dh2, 1024, 1024, "wgrad_co"))
    gbig["w_cq"] = _slabs(_wgrad(hn, dqc, 1024, 1024, "wgrad_cq"))
    gbig["w_ckv"] = _slabs(_wgrad(memn, dkv, 512, 1024, "wgrad_ckv", col_shards=4))
    d_mem = _mem_bwd(mem, g_mem, dkv, big["w_ckv"], 512)
    dmerged = _matmul_bt(dh1, big["w_out"], 512, "out_bwd")
    gbig["w_out"] = _slabs(_wgrad(merged, dh1, 1024, 1024, "wgrad_out"))
    rest = BIG[1:]
    ride = reduce.sibling([gbig[k] for k in rest]) if reduce else None
    dproj, d_wsp, d_bb, d_gv, d_hga, recv = _gmlp_bwd(proj, dmerged, g_v, wt, wtt, bb, hg_a, 512, ride=ride)
    ride = reduce.chip(rest, [gbig[k] for k in rest], recv) if reduce else None
    dproj, dk, dv, d_hgb, parts_rest = _sb_bwd(proj, o_sb, tot, nblk, dmerged, dproj, g_head, nb, s, tq, ride=ride)
    dproj = _place(_place(dproj, dk, 3, "place_dk"), dv, 4, "place_dv")
    gbig["w_in"] = _slabs(_wgrad_wide(xn, dproj, 512, 1024, "wgrad_in", col_shards=4))
    last = None
    if reduce:
        recv = _run_exchange(reduce.sibling([gbig["w_in"]]), "grad_sibling_exchange_w_in")
        last = reduce.chip(["w_in"], [gbig["w_in"]], recv)
    grad_x, d_mix, _ = _in_bwd(dproj, dh1, x, g_mix, big["w_in"], 512)
    parts = dict(zip(rest, parts_rest))

    gsmall = {
        "norm_mix_g": d_mix, "gm_v_norm_g": d_gv, "w_spatial": d_wsp, "b_spatial": d_bb[:, :, 0],
        "head_norm_g": jnp.concatenate([d_hga, jnp.sum(d_hgb, axis=0)], axis=1), "norm_cross_g": d_cross,
        "norm_mem_g": d_mem, "norm_ffn_g": d_ffn, "norm_final_g": d_fin,
    }
    return loss, grad_x, gsmall, gbig, parts, last


BIG = ("w_in", "w_out", "w_cq", "w_ckv", "w_co", "w_ff1", "w_ff2")
SMALL = ("norm_mix_g", "gm_v_norm_g", "w_spatial", "b_spatial", "head_norm_g", "norm_cross_g", "norm_mem_g",
         "norm_ffn_g", "norm_final_g")


def _local_copies_start(srcs, stages, sems):
    loads = [pltpu.make_async_copy(src, stage, sems.at[w]) for w, (src, stage) in enumerate(zip(srcs, stages))]
    for ld in loads:
        ld.start()
    return loads


def _local_copies_finish(loads, stages, dsts, sems):
    stores = []
    for w, (ld, stage, dst) in enumerate(zip(loads, stages, dsts)):
        ld.wait()
        st = pltpu.make_async_copy(stage, dst, sems.at[w])
        st.start()
        stores.append(st)
    for st in stores:
        st.wait()


def _chip_sum(slabs, recv, c_idx, name):
    _, r, cw = slabs.shape
    tr = min(r, 256)

    def body(c_ref, a_ref, b_ref, o_ref):
        del c_ref
        o_ref[...] = (a_ref[...] + b_ref[...]).astype(BF)

    return pl.pallas_call(
        body, name=name,
        grid_spec=pltpu.PrefetchScalarGridSpec(
            num_scalar_prefetch=1, grid=(N_CHIPS, r // tr),
            in_specs=[pl.BlockSpec((None, tr, cw), lambda p, i, c_ref: (2 * p + c_ref[0], i, 0)),
                      pl.BlockSpec((None, tr, cw), lambda p, i, c_ref: (p, i, 0))],
            out_specs=pl.BlockSpec((None, tr, cw), lambda p, i, c_ref: (p, i, 0))),
        out_shape=jax.ShapeDtypeStruct((N_CHIPS, r, cw), BF),
        compiler_params=_params(32, ("arbitrary", "arbitrary")),
    )(c_idx, *_in_hbm(slabs, recv))


def _sum4(sums, parts, q_idx, name):
    _, r, cw = parts.shape
    tr = min(r, 256)

    def body(q_ref, own_ref, a_ref, b_ref, c_ref, o_ref):
        del q_ref
        o_ref[...] = ((own_ref[...].astype(F32) + a_ref[...].astype(F32)) + b_ref[...].astype(F32)) + c_ref[
            ...].astype(F32)

    spec = lambda k: pl.BlockSpec((None, tr, cw), lambda i, q_ref: ((q_ref[0] + k) % N_CHIPS, i, 0))
    return pl.pallas_call(
        body, name=name,
        grid_spec=pltpu.PrefetchScalarGridSpec(
            num_scalar_prefetch=1, grid=(r // tr,), in_specs=[spec(0), spec(1), spec(2), spec(3)],
            out_specs=pl.BlockSpec((tr, cw), lambda i, q_ref: (i, 0))),
        out_shape=jax.ShapeDtypeStruct((r, cw), F32),
        compiler_params=_params(32, ("arbitrary",)),
    )(q_idx, *_in_hbm(sums, parts, parts, parts))


def _half_exchange(halves):
    n = len(halves)

    def body(*refs):
        ins, outs, stages = refs[:n], refs[n:2 * n], refs[2 * n:3 * n]
        send_sems, recv_sems, ld_sems, st_sems = refs[3 * n:]
        x, y, c = lax.axis_index("x"), lax.axis_index("y"), lax.axis_index("c")
        loads = _local_copies_start(ins, stages, ld_sems)
        copies = []
        for w in range(n):
            cp = pltpu.make_async_remote_copy(
                src_ref=ins[w], dst_ref=outs[w].at[c], send_sem=send_sems.at[w], recv_sem=recv_sems.at[w],
                device_id=(x, y, 1 - c), device_id_type=MESH)
            cp.start()
            copies.append(cp)
        _local_copies_finish(loads, stages, [outs[w].at[c] for w in range(n)], st_sems)
        for cp in copies:
            cp.wait()

    return pl.pallas_call(
        body, name="grad_half_exchange",
        in_specs=[ANY] * n, out_specs=[ANY] * n,
        out_shape=[jax.ShapeDtypeStruct((2,) + a.shape, a.dtype) for a in halves],
        scratch_shapes=[pltpu.VMEM(a.shape, a.dtype) for a in halves] + [
            pltpu.SemaphoreType.DMA((n,)), pltpu.SemaphoreType.DMA((n,)),
            pltpu.SemaphoreType.DMA((n,)), pltpu.SemaphoreType.DMA((n,))],
        compiler_params=_params(24),
    )(*halves)


def _small_all_reduce(packed, ride=None):
    rows = packed.shape[0]
    ride = ride or _NoExchange()
    ri, ro = len(ride.in_arrays), len(ride.out_shape)

    def body(*refs):
        in_ref, rins, out_ref, routs = refs[0], refs[1:1 + ri], refs[1 + ri], refs[2 + ri:2 + ri + ro]
        pair, chip_sum, chips, d2d_send, d2d_recv, ici_send, ici_recv = refs[2 + ri + ro:9 + ri + ro]
        rscr = refs[9 + ri + ro:]
        ride.start(rins, routs, rscr)
        x, y, c = lax.axis_index("x"), lax.axis_index("y"), lax.axis_index("c")
        q = 2 * x + y
        pair[c] = in_ref[...]
        swap = pltpu.make_async_remote_copy(
            src_ref=in_ref, dst_ref=pair.at[c], send_sem=d2d_send, recv_sem=d2d_recv,
            device_id=(x, y, 1 - c), device_id_type=MESH)
        swap.start()
        swap.wait()
        both = pair[0] + pair[1]
        chip_sum[...] = both
        chips[q] = both
        copies = [pltpu.make_async_remote_copy(
            src_ref=chip_sum, dst_ref=chips.at[q], send_sem=ici_send.at[k], recv_sem=ici_recv.at[k],
            device_id=(px, py, c), device_id_type=MESH) for k, (px, py) in enumerate(_other_chips(x, y))]
        for cp in copies:
            cp.start()
        for cp in copies:
            cp.wait()
        out_ref[...] = ((chips[0] + chips[1]) + chips[2]) + chips[3]
        ride.finish(rins, routs, rscr)

    vmem = pl.BlockSpec(memory_space=pltpu.VMEM)
    res = pl.pallas_call(
        body, name="small_all_reduce",
        in_specs=[vmem] + [ANY] * ri, out_specs=[vmem] + [ANY] * ro,
        out_shape=[jax.ShapeDtypeStruct(packed.shape, F32)] + list(ride.out_shape),
        scratch_shapes=[pltpu.VMEM((2, rows, 128), F32), pltpu.VMEM((rows, 128), F32),
                        pltpu.VMEM((N_CHIPS, rows, 128), F32), pltpu.SemaphoreType.DMA, pltpu.SemaphoreType.DMA,
                        pltpu.SemaphoreType.DMA((3,)), pltpu.SemaphoreType.DMA((3,))] + list(ride.scratch_shapes),
        compiler_params=_params(16),
    )(packed, *ride.in_arrays)
    return res[0], res[1:]


def _adamw(g, w, m, v, name):
    r, cw = g.shape
    tr = 256 if r % 256 == 0 else r

    def body(g_ref, w_ref, m_ref, v_ref, d_ref, nm_ref, nv_ref):
        gv = g_ref[...]
        nm = ADAM_B1 * m_ref[...] + (1.0 - ADAM_B1) * gv
        nv = ADAM_B2 * v_ref[...] + (1.0 - ADAM_B2) * (gv * gv)
        m_hat = nm / (1.0 - ADAM_B1 ** ADAM_STEP)
        v_hat = nv / (1.0 - ADAM_B2 ** ADAM_STEP)
        d_ref[...] = -ADAM_LR * (m_hat / (jnp.sqrt(v_hat) + ADAM_EPS) + ADAM_WD * w_ref[...])
        nm_ref[...] = nm
        nv_ref[...] = nv

    spec = pl.BlockSpec((tr, cw), lambda i: (i, 0))
    return pl.pallas_call(
        body, name=name, grid=(r // tr,),
        in_specs=[spec] * 4, out_specs=[spec] * 3,
        out_shape=[jax.ShapeDtypeStruct((r, cw), F32)] * 3,
        compiler_params=_params(32, ("arbitrary",)),
    )(*_in_hbm(g, w, m, v))


def _small_params(args):
    small = {k: args[k].reshape(1, -1) for k in SMALL}
    small["w_spatial"] = args["w_spatial"][0]
    small["b_spatial"] = args["b_spatial"][0]
    return small


def _pack(parts, rows):
    flat = jnp.concatenate([p.reshape(-1).astype(F32) for p in parts])
    return jnp.pad(flat, (0, rows * 128 - flat.shape[0])).reshape(rows, 128)


def _unpack(packed, shapes):
    flat = packed.reshape(-1)
    out, off = [], 0
    for shp in shapes:
        size = math.prod(shp)
        out.append(flat[off:off + size].reshape(shp))
        off += size
    return out


def kernel(x, mem, norm_mix_g, w_in, gm_v_norm_g, w_spatial, b_spatial, head_norm_g, w_out, norm_cross_g, norm_mem_g, w_cq, w_ckv, w_co, norm_ffn_g, w_ff1, w_ff2, norm_final_g, loss_target, m_norm_mix_g, m_w_in, m_gm_v_norm_g, m_w_spatial, m_b_spatial, m_head_norm_g, m_w_out, m_norm_cross_g, m_norm_mem_g, m_w_cq, m_w_ckv, m_w_co, m_norm_ffn_g, m_w_ff1, m_w_ff2, m_norm_final_g, v_norm_mix_g, v_w_in, v_gm_v_norm_g, v_w_spatial, v_b_spatial, v_head_norm_g, v_w_out, v_norm_cross_g, v_norm_mem_g, v_w_cq, v_w_ckv, v_w_co, v_norm_ffn_g, v_w_ff1, v_w_ff2, v_norm_final_g):
    args = dict(locals())
    d = D_MODEL
    nb, s, _ = x.shape
    c_idx = lax.axis_index("c").astype(jnp.int32).reshape(1)
    q_idx = (2 * lax.axis_index("x") + lax.axis_index("y")).astype(jnp.int32).reshape(1)
    rest = BIG[1:]

    shards = {k: args[k][0].astype(BF) for k in BIG}
    big = _full_weights({"w_in": _run_exchange(_GatherExchange([shards["w_in"]]), "all_gather_w_in")[0]})
    gather_rest = _GatherExchange([shards[k] for k in rest])

    reduce = _GradReduce(c_idx)
    loss, grad_x, gsmall, _, parts, last = _local_step(
        x.reshape(nb * s, d), mem.reshape(nb * N_MEM, d), loss_target.reshape(nb * s, d), _small_params(args), big,
        nb, s, gather_rest=gather_rest, reduce=reduce)

    shapes = [args[k].shape for k in SMALL]
    n_small = sum(math.prod(sh) for sh in shapes)
    rows = -(-(n_small + 1) // 1024) * 8
    reduced, (parts["w_in"],) = _small_all_reduce(_pack([gsmall[k] for k in SMALL] + [loss], rows), ride=last)
    halves = [_sum4(reduce.sums[k], parts[k], q_idx, "sum4_" + k) for k in BIG]
    both = _half_exchange(halves)

    out = {"grad_x": grad_x.reshape(nb, s, d)}
    for k, g2 in zip(BIG, both):
        shp = args[k].shape
        g = g2.reshape(shp[1], shp[2])
        dl, nm, nv = _adamw(g, args[k][0], args["m_" + k][0], args["v_" + k][0], "adamw_" + k)
        out["grad_" + k], out["delta_" + k], out["new_m_" + k], out["new_v_" + k] = (
            a.reshape(shp) for a in (g, dl, nm, nv))

    dl, nm, nv = _adamw(reduced, _pack([args[k] for k in SMALL], rows), _pack([args["m_" + k] for k in SMALL], rows),
                        _pack([args["v_" + k] for k in SMALL], rows), "adamw_small")
    for name, arr in (("grad_", reduced), ("delta_", dl), ("new_m_", nm), ("new_v_", nv)):
        for k, a in zip(SMALL, _unpack(arr, shapes)):
            out[name + k] = a
    out["loss"] = reduced.reshape(-1)[n_small]

    names = ["norm_mix_g", "w_in", "gm_v_norm_g", "w_spatial", "b_spatial", "head_norm_g", "w_out", "norm_cross_g",
             "norm_mem_g", "w_cq", "w_ckv", "w_co", "norm_ffn_g", "w_ff1", "w_ff2", "norm_final_g"]
    return (out["loss"], out["grad_x"], *[out["grad_" + k] for k in names], *[out["delta_" + k] for k in names],
            *[out["new_m_" + k] for k in names], *[out["new_v_" + k] for k in names])
```

```python
import functools
import math

import jax
import jax.numpy as jnp
from jax import lax
from jax.experimental import pallas as pl
from jax.experimental.pallas import tpu as pltpu

F32 = jnp.float32
BF = jnp.bfloat16

EPS = 1e-6
D_MODEL = 1024
CHUNK = 128
GM_GROUPS = 4
GM_WIDTH = 512
SB_WIDTH = 512
HEAD_LANES = 64
SB_SCALE = 0.125
SB_SKIP = -104.0
X_HEADS = 4
X_HEAD_DIM = 256
N_MEM = 256
D_FF = 4096
IN_COLS = 2560
N_CHIPS = 4
N_DEV = 8

ADAM_LR = 0.001
ADAM_B1 = 0.9
ADAM_B2 = 0.999
ADAM_EPS = 1e-08
ADAM_WD = 0.01
ADAM_STEP = 10

V7X_VMEM_BYTES = 64 * 1024 * 1024
MESH = pl.DeviceIdType.MESH
ANY = pl.BlockSpec(memory_space=pl.ANY)

GELU_C = math.sqrt(2.0 / math.pi)
GELU_A = 0.044715


def _params(vmem_mb, sem=None):
    assert vmem_mb * 1024 * 1024 <= V7X_VMEM_BYTES
    return pltpu.CompilerParams(vmem_limit_bytes=vmem_mb * 1024 * 1024, dimension_semantics=sem)


PIN_MIN_ELEMENTS = 1 << 18


def _in_hbm(*arrays):
    return tuple(pltpu.with_memory_space_constraint(a, pltpu.HBM) if a.size >= PIN_MIN_ELEMENTS else a
                 for a in arrays)


def _dot(a, b):
    return jnp.dot(a, b, preferred_element_type=F32)


def _dot_bt(a, b):
    return lax.dot_general(a, b, (((1,), (1,)), ((), ())), preferred_element_type=F32)


def _dot_at(a, b):
    return lax.dot_general(a, b, (((0,), (0,)), ((), ())), preferred_element_type=F32)


def _gelu(x):
    t = jnp.tanh(GELU_C * (x + GELU_A * x * x * x))
    return 0.5 * x * (1.0 + t)


def _gelu_and_grad(x):
    x2 = x * x
    t = jnp.tanh(GELU_C * (x + GELU_A * x2 * x))
    h = 0.5 * (1.0 + t)
    return x * h, h + 0.5 * x * (1.0 - t * t) * (GELU_C * (1.0 + 3.0 * GELU_A * x2))


def _rs(x):
    return lax.rsqrt(jnp.mean(x * x, axis=-1, keepdims=True) + EPS)


def _rms_bwd(dxn, xhat, r, g):
    dxh = dxn * g
    dx = r * (dxh - xhat * jnp.mean(dxh * xhat, axis=-1, keepdims=True))
    return dx, dxn * xhat


def _norm_matmul(x, g, w, tm, name):
    t, d = x.shape
    n = w.shape[1]
    tm = min(tm, t)

    def body(x_ref, g_ref, w_ref, out_ref, xn_ref):
        xv = x_ref[...]
        xn = (xv * _rs(xv) * g_ref[...]).astype(BF)
        xn_ref[...] = xn
        out_ref[...] = _dot(xn, w_ref[...]).astype(out_ref.dtype)

    return pl.pallas_call(
        body, name=name, grid=(t // tm,),
        in_specs=[pl.BlockSpec((tm, d), lambda i: (i, 0)), pl.BlockSpec((1, d), lambda i: (0, 0)),
                  pl.BlockSpec((d, n), lambda i: (0, 0))],
        out_specs=[pl.BlockSpec((tm, n), lambda i: (i, 0)), pl.BlockSpec((tm, d), lambda i: (i, 0))],
        out_shape=[jax.ShapeDtypeStruct((t, n), BF), jax.ShapeDtypeStruct((t, d), BF)],
        compiler_params=_params(48, ("arbitrary",)),
    )(*_in_hbm(x, g, w))


def _wgrad(a, g, tn, tk, name, square_a=False, col_shards=1, tm=1024):
    t, m = a.shape
    n = g.shape[1]
    tk = min(tk, t)
    tm = min(m, tm)
    ns = n // col_shards
    assert ns % tn == 0 and m % tm == 0
    per = ns // tn
    nk = t // tk

    def body(a_ref, g_ref, o_ref):
        k = pl.program_id(2)

        @pl.when(k == 0)
        def _():
            o_ref[...] = jnp.zeros_like(o_ref)

        av = a_ref[...]
        if square_a:
            af = av.astype(F32)
            av = af * af
        o_ref[...] += _dot_at(av.astype(BF), g_ref[...].astype(BF))

    return pl.pallas_call(
        body, name=name, grid=(m // tm, n // tn, nk),
        in_specs=[pl.BlockSpec((tk, tm), lambda i, j, k: (k, i)), pl.BlockSpec((tk, tn), lambda i, j, k: (k, j))],
        out_specs=pl.BlockSpec((None, tm, tn), lambda i, j, k: (j // per, i, j % per)),
        out_shape=jax.ShapeDtypeStruct((col_shards, m, ns), F32),
        compiler_params=_params(48, ("arbitrary", "arbitrary", "arbitrary")),
    )(*_in_hbm(a, g))


def _wgrad_wide(a, g, tm, tk, name, col_shards):
    t, m = a.shape
    n = g.shape[1]
    tk = min(tk, t)
    tm = min(tm, m)
    ns = n // col_shards

    def body(a_ref, g_ref, o_ref):
        @pl.when(pl.program_id(1) == 0)
        def _():
            o_ref[...] = jnp.zeros_like(o_ref)

        a_t = a_ref[...].astype(BF).T
        for p in range(col_shards):
            o_ref[p] += _dot(a_t, g_ref[:, p * ns:(p + 1) * ns].astype(BF))

    return pl.pallas_call(
        body, name=name, grid=(m // tm, t // tk),
        in_specs=[pl.BlockSpec((tk, tm), lambda i, k: (k, i)), pl.BlockSpec((tk, n), lambda i, k: (k, 0))],
        out_specs=pl.BlockSpec((col_shards, tm, ns), lambda i, k: (0, i, 0)),
        out_shape=jax.ShapeDtypeStruct((col_shards, m, ns), F32),
        compiler_params=_params(48, ("arbitrary", "arbitrary")),
    )(*_in_hbm(a, g))


def _matmul_bt(a, w, tm, name):
    t, n = a.shape
    k = w.shape[0]
    tm = min(tm, t)

    def body(a_ref, w_ref, o_ref):
        o_ref[...] = _dot_bt(a_ref[...].astype(BF), w_ref[...]).astype(o_ref.dtype)

    return pl.pallas_call(
        body, name=name, grid=(t // tm,),
        in_specs=[pl.BlockSpec((tm, n), lambda i: (i, 0)), pl.BlockSpec((k, n), lambda i: (0, 0))],
        out_specs=pl.BlockSpec((tm, k), lambda i: (i, 0)),
        out_shape=jax.ShapeDtypeStruct((t, k), BF),
        compiler_params=_params(32, ("arbitrary",)),
    )(*_in_hbm(a, w))


def _gmlp_fwd(proj, gg, wt, bb, hg, tm):
    t = proj.shape[0]
    tm = min(tm, t)

    def body(u_ref, v_ref, gg_ref, wt_ref, bb_ref, hg_ref, out_ref):
        for cc in range(tm // CHUNK):
            rows = slice(cc * CHUNK, (cc + 1) * CHUNK)
            for g in range(GM_GROUPS):
                cols = slice(g * 128, (g + 1) * 128)
                u = _gelu(u_ref[rows, cols].astype(F32))
                gv = _gelu(v_ref[rows, cols].astype(F32))
                vn = gv * _rs(gv) * gg_ref[:, cols]
                mixed = _dot(wt_ref[g], vn.astype(BF)) + bb_ref[g]
                a = u * mixed
                out_ref[rows, cols] = (a * _rs(a) * hg_ref[:, cols]).astype(BF)

    return pl.pallas_call(
        body, name="gmlp_fwd", grid=(t // tm,),
        in_specs=[pl.BlockSpec((tm, 512), lambda i: (i, 0)), pl.BlockSpec((tm, 512), lambda i: (i, 1)),
                  pl.BlockSpec((1, 512), lambda i: (0, 0)), pl.BlockSpec((4, 128, 128), lambda i: (0, 0, 0)),
                  pl.BlockSpec((4, 128, 128), lambda i: (0, 0, 0)), pl.BlockSpec((1, 512), lambda i: (0, 0))],
        out_specs=pl.BlockSpec((tm, 512), lambda i: (i, 0)),
        out_shape=jax.ShapeDtypeStruct((t, 1024), BF),
        compiler_params=_params(32, ("arbitrary",)),
    )(*_in_hbm(proj, proj, gg, wt, bb, hg))


def _gmlp_bwd(proj, dmerged, gg, wt, wtt, bb, hg, tm, ride=None):
    t = proj.shape[0]
    tm = min(tm, t)
    nsteps = t // tm

    def body(u_ref, v_ref, dm_ref, gg_ref, wt_ref, wtt_ref, bb_ref, hg_ref,
             dp_ref, dw_ref, db_ref, dgg_ref, dhg_ref):
        i = pl.program_id(0)

        @pl.when(i == 0)
        def _():
            dw_ref[...] = jnp.zeros_like(dw_ref)
            db_ref[...] = jnp.zeros_like(db_ref)
            dgg_ref[...] = jnp.zeros_like(dgg_ref)
            dhg_ref[...] = jnp.zeros_like(dhg_ref)

        for cc in range(tm // CHUNK):
            rows = slice(cc * CHUNK, (cc + 1) * CHUNK)
            for g in range(GM_GROUPS):
                cols = slice(g * 128, (g + 1) * 128)
                up = u_ref[rows, cols].astype(F32)
                gp = v_ref[rows, cols].astype(F32)
                u, u_grad = _gelu_and_grad(up)
                gv, gv_grad = _gelu_and_grad(gp)
                rv = _rs(gv)
                gvh = gv * rv
                ggv = gg_ref[:, cols]
                vnb = (gvh * ggv).astype(BF)
                mixed = _dot(wt_ref[g], vnb) + bb_ref[g]
                a = u * mixed
                ra = _rs(a)
                ah = a * ra
                dm = dm_ref[rows, cols].astype(F32)
                dhg_ref[:, cols] += jnp.sum(dm * ah, axis=0, keepdims=True)
                dah = dm * hg_ref[:, cols]
                da = ra * (dah - ah * jnp.mean(dah * ah, axis=-1, keepdims=True))
                du = da * mixed
                dmix = da * u
                db_ref[g] += dmix
                dmb = dmix.astype(BF)
                dw_ref[g] += _dot_bt(dmb, vnb)
                dvn = _dot(wtt_ref[g], dmb)
                dgg_ref[:, cols] += jnp.sum(dvn * gvh, axis=0, keepdims=True)
                dgh = dvn * ggv
                dgv = rv * (dgh - gvh * jnp.mean(dgh * gvh, axis=-1, keepdims=True))
                dp_ref[rows, cols] = (du * u_grad).astype(BF)
                dp_ref[rows, 512 + g * 128:512 + (g + 1) * 128] = (dgv * gv_grad).astype(BF)

        @pl.when(i == nsteps - 1)
        def _():
            r = lax.broadcasted_iota(jnp.int32, (CHUNK, CHUNK), 0)
            c = lax.broadcasted_iota(jnp.int32, (CHUNK, CHUNK), 1)
            for g in range(GM_GROUPS):
                dw_ref[g] = jnp.where(c <= r, dw_ref[g], 0.0)
                db_ref[g] = jnp.broadcast_to(jnp.sum(db_ref[g], axis=-1, keepdims=True), (CHUNK, CHUNK))

    small = lambda shape: pl.BlockSpec(shape, lambda i: (0,) * len(shape))
    res, rode = _ride_call(
        body, "gmlp_bwd", (nsteps,),
        in_specs=[pl.BlockSpec((tm, 512), lambda i: (i, 0)), pl.BlockSpec((tm, 512), lambda i: (i, 1)),
                  pl.BlockSpec((tm, 512), lambda i: (i, 0)), small((1, 512)), small((4, 128, 128)),
                  small((4, 128, 128)), small((4, 128, 128)), small((1, 512))],
        out_specs=[pl.BlockSpec((tm, 1024), lambda i: (i, 0)), small((4, 128, 128)), small((4, 128, 128)),
                   small((1, 512)), small((1, 512))],
        out_shape=[jax.ShapeDtypeStruct((t, IN_COLS), BF), jax.ShapeDtypeStruct((4, 128, 128), F32),
                   jax.ShapeDtypeStruct((4, 128, 128), F32), jax.ShapeDtypeStruct((1, 512), F32),
                   jax.ShapeDtypeStruct((1, 512), F32)],
        scratch_shapes=[], operands=(proj, proj, dmerged, gg, wt, wtt, bb, hg), vmem_mb=32, ride=ride)
    return (*res, rode)


def _other_chips(x, y):
    return ((1 - x, y), (x, 1 - y), (1 - x, 1 - y))


class _GatherExchange:
    def __init__(self, shards):
        n = len(shards)
        self.n = n
        self.in_arrays = list(shards)
        self.out_shape = [jax.ShapeDtypeStruct((N_CHIPS,) + a.shape, a.dtype) for a in shards]
        self.half_rows = [a.shape[0] // 2 for a in shards]
        sems = lambda k: pltpu.SemaphoreType.DMA((k,))
        self.scratch_shapes = [pltpu.VMEM(a.shape, a.dtype) for a in shards] + [
            sems(3 * n), sems(3 * n), sems(3 * n), sems(3 * n), sems(n), sems(n)]

    def _copies(self, ins, outs, scr):
        n = self.n
        stages, (ici_send, ici_recv, d2d_send, d2d_recv, ld_sems, st_sems) = scr[:n], scr[n:]
        x, y, c = lax.axis_index("x"), lax.axis_index("y"), lax.axis_index("c")
        q = 2 * x + y
        loads = [pltpu.make_async_copy(ins[w], stages[w], ld_sems.at[w]) for w in range(n)]
        stores = [pltpu.make_async_copy(stages[w], outs[w].at[q], st_sems.at[w]) for w in range(n)]
        ici, d2d = [], []
        for w in range(n):
            half = pl.ds(c * self.half_rows[w], self.half_rows[w])
            for k, (px, py) in enumerate(_other_chips(x, y)):
                ici.append(pltpu.make_async_remote_copy(
                    src_ref=ins[w].at[half], dst_ref=outs[w].at[q, half], send_sem=ici_send.at[3 * w + k],
                    recv_sem=ici_recv.at[3 * w + k], device_id=(px, py, c), device_id_type=MESH))
                landed = outs[w].at[2 * px + py, half]
                d2d.append(pltpu.make_async_remote_copy(
                    src_ref=landed, dst_ref=landed, send_sem=d2d_send.at[3 * w + k],
                    recv_sem=d2d_recv.at[3 * w + k], device_id=(x, y, 1 - c), device_id_type=MESH))
        return loads, stores, ici, d2d

    def start(self, ins, outs, scr):
        loads, stores, ici, _ = self._copies(ins, outs, scr)
        for cp in loads + ici:
            cp.start()
        for ld, st in zip(loads, stores):
            ld.wait()
            st.start()

    def relay(self, ins, outs, scr):
        _, _, ici, d2d = self._copies(ins, outs, scr)
        for got, fwd in zip(ici, d2d):
            got.wait_recv()
            fwd.start()

    def finish(self, ins, outs, scr):
        _, stores, ici, d2d = self._copies(ins, outs, scr)
        for cp in ici:
            cp.wait_send()
        for cp in d2d + stores:
            cp.wait()


class _SiblingExchange:
    def __init__(self, slabs):
        n = len(slabs)
        self.n = n
        self.in_arrays = list(slabs)
        self.out_shape = [jax.ShapeDtypeStruct((N_CHIPS,) + a.shape[1:], a.dtype) for a in slabs]
        self.scratch_shapes = [pltpu.SemaphoreType.DMA((4 * n,)), pltpu.SemaphoreType.DMA((4 * n,))]

    def _copies(self, ins, outs, scr):
        send_sems, recv_sems = scr
        x, y, c = lax.axis_index("x"), lax.axis_index("y"), lax.axis_index("c")
        return [pltpu.make_async_remote_copy(
            src_ref=ins[w].at[2 * p + (1 - c)], dst_ref=outs[w].at[p], send_sem=send_sems.at[4 * w + p],
            recv_sem=recv_sems.at[4 * w + p], device_id=(x, y, 1 - c), device_id_type=MESH)
            for w in range(self.n) for p in range(N_CHIPS)]

    def start(self, ins, outs, scr):
        for cp in self._copies(ins, outs, scr):
            cp.start()

    def finish(self, ins, outs, scr):
        for cp in self._copies(ins, outs, scr):
            cp.wait()


class _ChipExchange:
    def __init__(self, sums):
        n = len(sums)
        self.n = n
        self.in_arrays = list(sums)
        self.out_shape = [jax.ShapeDtypeStruct(a.shape, a.dtype) for a in sums]
        self.scratch_shapes = [pltpu.SemaphoreType.DMA((3 * n,)), pltpu.SemaphoreType.DMA((3 * n,))]

    def _copies(self, ins, outs, scr):
        send_sems, recv_sems = scr
        x, y, c = lax.axis_index("x"), lax.axis_index("y"), lax.axis_index("c")
        q = 2 * x + y
        return [pltpu.make_async_remote_copy(
            src_ref=ins[w].at[2 * px + py], dst_ref=outs[w].at[q], send_sem=send_sems.at[3 * w + k],
            recv_sem=recv_sems.at[3 * w + k], device_id=(px, py, c), device_id_type=MESH)
            for w in range(self.n) for k, (px, py) in enumerate(_other_chips(x, y))]

    def start(self, ins, outs, scr):
        for cp in self._copies(ins, outs, scr):
            cp.start()

    def finish(self, ins, outs, scr):
        for cp in self._copies(ins, outs, scr):
            cp.wait()


class _NoExchange:
    in_arrays, out_shape, scratch_shapes = (), (), ()

    def start(self, ins, outs, scr):
        pass

    def finish(self, ins, outs, scr):
        pass


def _run_exchange(ex, name):
    n_in, n_out = len(ex.in_arrays), len(ex.out_shape)

    def body(*refs):
        ins, outs, scr = refs[:n_in], refs[n_in:n_in + n_out], refs[n_in + n_out:]
        ex.start(ins, outs, scr)
        if hasattr(ex, "relay"):
            ex.relay(ins, outs, scr)
        ex.finish(ins, outs, scr)

    return pl.pallas_call(
        body, name=name, in_specs=[ANY] * n_in, out_specs=[ANY] * n_out, out_shape=ex.out_shape,
        scratch_shapes=ex.scratch_shapes, compiler_params=_params(24),
    )(*ex.in_arrays)


def _ride_call(body, name, grid, in_specs, out_specs, out_shape, scratch_shapes, operands, vmem_mb, ride=None,
               aliases=None):
    ride = ride or _NoExchange()
    ni, no, ns = len(in_specs), len(out_specs), len(scratch_shapes)
    ri, ro = len(ride.in_arrays), len(ride.out_shape)
    total = math.prod(grid)

    def wrapped(*refs):
        ins, rins = refs[:ni], refs[ni:ni + ri]
        outs, routs = refs[ni + ri:ni + ri + no], refs[ni + ri + no:ni + ri + no + ro]
        scr, rscr = refs[ni + ri + no + ro:ni + ri + no + ro + ns], refs[ni + ri + no + ro + ns:]
        step = pl.program_id(0)
        for ax in range(1, len(grid)):
            step = step * grid[ax] + pl.program_id(ax)

        @pl.when(step == 0)
        def _():
            ride.start(rins, routs, rscr)

        if hasattr(ride, "relay"):
            @pl.when(step == (3 * total) // 4)
            def _():
                ride.relay(rins, routs, rscr)

        body(*ins, *outs, *scr)

        @pl.when(step == total - 1)
        def _():
            ride.finish(rins, routs, rscr)

    res = pl.pallas_call(
        wrapped, name=name, grid=grid, in_specs=list(in_specs) + [ANY] * ri, out_specs=list(out_specs) + [ANY] * ro,
        out_shape=list(out_shape) + list(ride.out_shape),
        scratch_shapes=list(scratch_shapes) + list(ride.scratch_shapes), input_output_aliases=aliases or {},
        compiler_params=_params(vmem_mb, ("arbitrary",) * len(grid)),
    )(*_in_hbm(*operands), *ride.in_arrays)
    return res[:no], res[no:]


def _neg_log_sig(z):
    n = jnp.maximum(z, 0.0) + jnp.log(1.0 + jnp.exp(-jnp.abs(z)))
    return n, z - n


def _running_sums(n, tri2):
    hi = n.astype(BF)
    lo = (n - hi.astype(F32)).astype(BF)
    return _dot(jnp.concatenate([hi, lo], axis=1), tri2)


def _head_sums(x, h0):
    s0 = jnp.sum(jnp.where(h0, x, 0.0), axis=-1, keepdims=True)
    s1 = jnp.sum(jnp.where(h0, 0.0, x), axis=-1, keepdims=True)
    return jnp.where(h0, s0, s1)


SB_BLOCKS_PER_STEP = 8
SB_PAIRS_PER_STEP = 2


def _sb_masks(tq):
    h0 = lax.broadcasted_iota(jnp.int32, (tq, 128), 1) < HEAD_LANES
    r = lax.broadcasted_iota(jnp.int32, (2 * tq, tq), 0)
    c = lax.broadcasted_iota(jnp.int32, (2 * tq, tq), 1)
    return h0, c < jnp.where(r >= tq, r - tq, r)


def _sb_stack(x, h0):
    zero = jnp.zeros_like(x)
    return jnp.concatenate([jnp.where(h0, x, zero), jnp.where(h0, zero, x)], axis=0)


def _tri(tq, op):
    return op(lax.broadcasted_iota(jnp.int32, (tq, tq), 0), lax.broadcasted_iota(jnp.int32, (tq, tq), 1)).astype(BF)


def _sb_fwd(proj, merged, hg, nb, s, tq, ride=None):
    t = nb * s
    tq = min(tq, s)
    nq = s // tq
    per = min(SB_BLOCKS_PER_STEP, nq)
    ns = nq // per
    gp, ng, w = SB_PAIRS_PER_STEP, 4 // SB_PAIRS_PER_STEP, 128 * SB_PAIRS_PER_STEP

    def body(q_ref, k_ref, v_ref, hg_ref, merged_ref, o_ref, tot_ref, mb_ref, nblk_ref, acc, cr, c_min):
        del merged_ref
        h0, causal = _sb_masks(tq)
        tri_gt = _tri(tq, lambda r, c: r > c)
        tri_gt = jnp.concatenate([tri_gt, tri_gt], axis=0)
        lanes = [slice(g * 128, (g + 1) * 128) for g in range(gp)]
        zeros = jnp.zeros((2 * tq, 1), F32)

        def query_block(i, rows):
            qsts = [_sb_stack(q_ref[rows, lanes[g]] * SB_SCALE, h0) for g in range(gp)]

            def block(g, j, masked, c_in):
                start = pl.multiple_of(j * tq, tq)
                kj = k_ref[pl.ds(start, tq), lanes[g]]
                vj = v_ref[pl.ds(start, tq), lanes[g]]
                n, l = _neg_log_sig(_dot_bt(qsts[g], kj))
                if masked:
                    n = jnp.where(causal, n, 0.0)
                a = jnp.exp(l - (_running_sums(n, tri_gt) + c_in))
                if masked:
                    a = jnp.where(causal, a, 0.0)
                return _dot(a.astype(BF), vj), c_in + jnp.sum(n, axis=-1, keepdims=True)

            def keep(parts):
                for g, (p, c) in enumerate(parts):
                    acc[g] = p
                    cr[g] = c
                c_min[0] = jnp.min(functools.reduce(jnp.minimum, [c for _, c in parts]))

            @pl.when(i == 0)
            def _():
                keep([block(g, 0, True, zeros) for g in range(gp)])

            @pl.when(i > 0)
            def _():
                diag = [block(g, i, True, zeros) for g in range(gp)]
                prev = [block(g, i - 1, False, diag[g][1]) for g in range(gp)]
                keep([(diag[g][0] + prev[g][0], prev[g][1]) for g in range(gp)])

            def cond(carry):
                return jnp.logical_and(carry[0] < i, carry[1] < -SB_SKIP)

            def step(carry):
                more = [block(g, i - 1 - carry[0], False, cr[g]) for g in range(gp)]
                for g, (p, c) in enumerate(more):
                    acc[g] += p
                    cr[g] = c
                return carry[0] + 1, jnp.min(functools.reduce(jnp.minimum, [c for _, c in more]))

            walked, _ = lax.while_loop(cond, step, (jnp.minimum(i, 1), c_min[0]))
            return walked

        for u in range(per):
            rows = slice(u * tq, (u + 1) * tq)
            walked = query_block(pl.program_id(2) * per + u, rows)
            for g in range(gp):
                o = jnp.where(h0, acc[g, 0:tq, :], acc[g, tq:2 * tq, :])
                o_ref[rows, lanes[g]] = o
                tot_ref[rows, lanes[g]] = jnp.where(h0, cr[g, 0:tq, :], cr[g, tq:2 * tq, :])
                ro = lax.rsqrt(_head_sums(o * o, h0) * (1.0 / HEAD_LANES) + EPS)
                mb_ref[rows, lanes[g]] = (o * ro * hg_ref[:, lanes[g]]).astype(BF)
            nblk_ref[u * 8:(u + 1) * 8, :] = jnp.full((8, 128), walked.astype(F32))

    blk = lambda col0: pl.BlockSpec((per * tq, w), lambda b, hg_, i: (b * ns + i, col0 + hg_))
    seq = lambda col0: pl.BlockSpec((s, w), lambda b, hg_, i: (b, col0 + hg_))
    first = 1024 // w
    (o, tot, mb, nblk), rode = _ride_call(
        body, "sb_fwd", (nb, ng, ns),
        in_specs=[blk(first), seq(first + ng), seq(first + 2 * ng),
                  pl.BlockSpec((1, w), lambda b, hg_, i: (0, ng + hg_)), ANY],
        out_specs=[blk(0), blk(0), blk(ng),
                   pl.BlockSpec((None, None, per * 8, 128), lambda b, hg_, i: (b, hg_, i, 0))],
        out_shape=[jax.ShapeDtypeStruct((t, 512), F32), jax.ShapeDtypeStruct((t, 512), F32),
                   jax.ShapeDtypeStruct((t, 1024), BF), jax.ShapeDtypeStruct((nb, ng, nq * 8, 128), F32)],
        scratch_shapes=[pltpu.VMEM((gp, 2 * tq, 128), F32), pltpu.VMEM((gp, 2 * tq, 1), F32),
                        pltpu.SMEM((1,), F32)],
        operands=(proj, proj, proj, hg, merged), vmem_mb=40, ride=ride, aliases={4: 2})
    return o, tot, mb, nblk, rode


def _sb_bwd(proj, o_sb, tot, nblk, dmerged, dproj, hg, nb, s, tq, ride=None):
    t = nb * s
    tq = min(tq, s)
    nq = s // tq
    per = min(SB_BLOCKS_PER_STEP, nq)
    ns = nq // per
    gp, ng, w = SB_PAIRS_PER_STEP, 4 // SB_PAIRS_PER_STEP, 128 * SB_PAIRS_PER_STEP

    def body(q_ref, k_ref, v_ref, o_ref, tot_ref, nblk_ref, dm_ref, hg_ref, dproj_ref,
             dq_ref, dk_ref, dv_ref, dhg_ref, dk_acc, dv_acc, dq_acc, cm, cg):
        del dproj_ref
        h0, causal = _sb_masks(tq)
        tri_le = _tri(tq, lambda r, c: r <= c)
        tri_le = jnp.concatenate([tri_le, tri_le], axis=0)
        tri_lt = _tri(tq, lambda r, c: r < c)
        lanes = [slice(g * 128, (g + 1) * 128) for g in range(gp)]

        @pl.when(pl.program_id(2) == 0)
        def _():
            dk_acc[...] = jnp.zeros_like(dk_acc)
            dv_acc[...] = jnp.zeros_like(dv_acc)
            dhg_ref[...] = jnp.zeros_like(dhg_ref)

        def query_block(i, rows):
            for ref in (dq_acc, cm, cg):
                ref[...] = jnp.zeros_like(ref)
            qsts, dosts, tots = [], [], []
            for g in range(gp):
                qsts.append(_sb_stack(q_ref[rows, lanes[g]] * SB_SCALE, h0))
                o = o_ref[rows, lanes[g]]
                ro = lax.rsqrt(_head_sums(o * o, h0) * (1.0 / HEAD_LANES) + EPS)
                oh = o * ro
                dm = dm_ref[rows, lanes[g]].astype(F32)
                dhg_ref[:, lanes[g]] += jnp.sum(dm * oh, axis=0, keepdims=True)
                doh = dm * hg_ref[:, lanes[g]]
                do = ro * (doh - oh * (_head_sums(doh * oh, h0) * (1.0 / HEAD_LANES)))
                dosts.append(_sb_stack(do.astype(BF), h0))
                first = g * 128
                tots.append(jnp.concatenate(
                    [tot_ref[rows, first:first + 1], tot_ref[rows, first + HEAD_LANES:first + HEAD_LANES + 1]], axis=0))
            qsts_t = [q.T for q in qsts]
            dosts_t = [d.T for d in dosts]

            def block(g, j, masked, cm_in, cg_in):
                start = pl.multiple_of(j * tq, tq)
                kj = k_ref[pl.ds(start, tq), lanes[g]]
                vj = v_ref[pl.ds(start, tq), lanes[g]]
                n, l = _neg_log_sig(_dot_bt(qsts[g], kj))
                if masked:
                    n = jnp.where(causal, n, 0.0)
                a = jnp.exp(l - (tots[g] - cm_in - _running_sums(n, tri_le)))
                if masked:
                    a = jnp.where(causal, a, 0.0)
                gm = a * _dot_bt(dosts[g], vj)
                pp = cg_in + _dot(gm.astype(BF), tri_lt)
                dz = gm - jnp.exp(l) * (gm + pp)
                if masked:
                    dz = jnp.where(causal, dz, 0.0)
                dzb = dz.astype(BF)
                dk_acc[g, :, pl.ds(start, tq)] += _dot(qsts_t[g], dzb)
                dv_acc[g, :, pl.ds(start, tq)] += _dot(dosts_t[g], a.astype(BF))
                return (_dot(dzb, kj), cm_in + jnp.sum(n, axis=-1, keepdims=True),
                        cg_in + jnp.sum(gm, axis=-1, keepdims=True))

            def step(j, carry):
                for g in range(gp):
                    dq, cm[g], cg[g] = block(g, j, False, cm[g], cg[g])
                    dq_acc[g] += dq
                return carry

            walked = jnp.clip(nblk_ref[pl.program_id(0), pl.program_id(1), i].astype(jnp.int32),
                              jnp.minimum(i, 1), i)
            lax.fori_loop(i - walked, i - 1, step, 0)

            @pl.when(i == 0)
            def _():
                for g in range(gp):
                    dq_acc[g] = block(g, 0, True, cm[g], cg[g])[0]

            @pl.when(i > 0)
            def _():
                prev = [block(g, i - 1, False, cm[g], cg[g]) for g in range(gp)]
                diag = [block(g, i, True, prev[g][1], prev[g][2]) for g in range(gp)]
                for g in range(gp):
                    dq_acc[g] += prev[g][0] + diag[g][0]

            for g in range(gp):
                dq = jnp.where(h0, dq_acc[g, 0:tq, :], dq_acc[g, tq:2 * tq, :])
                dq_ref[rows, lanes[g]] = (dq * SB_SCALE).astype(BF)

        for u in range(per):
            query_block(pl.program_id(2) * per + u, slice(u * tq, (u + 1) * tq))

        @pl.when(pl.program_id(2) == ns - 1)
        def _():
            for g in range(gp):
                dk_ref[:, lanes[g]] = dk_acc[g].T.astype(BF)
                dv_ref[:, lanes[g]] = dv_acc[g].T.astype(BF)

    blk = lambda col0: pl.BlockSpec((per * tq, w), lambda b, hg_, i: (b * ns + i, col0 + hg_))
    seq = lambda col0: pl.BlockSpec((s, w), lambda b, hg_, i: (b, col0 + hg_))
    first = 1024 // w
    (dq, dk, dv, dhg), rode = _ride_call(
        body, "sb_bwd", (nb, ng, ns),
        in_specs=[blk(first), seq(first + ng), seq(first + 2 * ng), blk(0), blk(0),
                  pl.BlockSpec(memory_space=pltpu.SMEM), blk(ng),
                  pl.BlockSpec((1, w), lambda b, hg_, i: (0, ng + hg_)), ANY],
        out_specs=[blk(first), seq(0), seq(0), pl.BlockSpec((None, 1, w), lambda b, hg_, i: (b, 0, hg_))],
        out_shape=[jax.ShapeDtypeStruct((t, IN_COLS), BF), jax.ShapeDtypeStruct((t, 512), BF),
                   jax.ShapeDtypeStruct((t, 512), BF), jax.ShapeDtypeStruct((nb, 1, 512), F32)],
        scratch_shapes=[pltpu.VMEM((gp, 128, s), F32), pltpu.VMEM((gp, 128, s), F32),
                        pltpu.VMEM((gp, 2 * tq, 128), F32), pltpu.VMEM((gp, 2 * tq, 1), F32),
                        pltpu.VMEM((gp, 2 * tq, 1), F32)],
        operands=(proj, proj, proj, o_sb, tot, nblk.reshape(nb, ng, nq, 8, 128)[:, :, :, 0, 0], dmerged, hg, dproj),
        vmem_mb=48, ride=ride, aliases={8: 0})
    return dq, dk, dv, dhg, rode


def _place(buf, piece, col_block, name):
    t, w = piece.shape
    tm = min(t, 1024)

    def body(piece_ref, buf_ref, out_ref):
        del buf_ref
        out_ref[...] = piece_ref[...]

    return pl.pallas_call(
        body, name=name, grid=(t // tm,),
        in_specs=[pl.BlockSpec((tm, w), lambda i: (i, 0)), ANY],
        out_specs=pl.BlockSpec((tm, w), lambda i: (i, col_block)),
        out_shape=jax.ShapeDtypeStruct(buf.shape, buf.dtype), input_output_aliases={1: 0},
        compiler_params=_params(16, ("arbitrary",)),
    )(piece, buf)


def _softmax_rows(sc):
    e = jnp.exp(sc - jnp.max(sc, axis=-1, keepdims=True))
    return e / jnp.sum(e, axis=-1, keepdims=True)


def _mix_cross_fwd(x, merged, w_out, gc, w_cq, kv, w_co, s, tm):
    t, d = x.shape
    tm = min(tm, s)
    per = s // tm
    inv = 1.0 / math.sqrt(X_HEAD_DIM)

    def body(x_ref, m_ref, wo_ref, gc_ref, wq_ref, kv_ref, wc_ref, h1_ref, h2_ref, hn_ref, qc_ref, oc_ref):
        h1 = x_ref[...] + _dot(m_ref[...], wo_ref[...])
        h1_ref[...] = h1
        hn = (h1 * _rs(h1) * gc_ref[...]).astype(BF)
        hn_ref[...] = hn
        qc = _dot(hn, wq_ref[...]).astype(BF)
        qc_ref[...] = qc
        for h in range(X_HEADS):
            cols = slice(h * X_HEAD_DIM, (h + 1) * X_HEAD_DIM)
            kh = kv_ref[:, h * X_HEAD_DIM:(h + 1) * X_HEAD_DIM]
            vh = kv_ref[:, d + h * X_HEAD_DIM:d + (h + 1) * X_HEAD_DIM]
            p = _softmax_rows(_dot_bt(qc[:, cols], kh) * inv)
            oc_ref[:, cols] = _dot(p.astype(BF), vh).astype(BF)
        h2_ref[...] = h1 + _dot(oc_ref[...], wc_ref[...])

    row = lambda width: pl.BlockSpec((tm, width), lambda i: (i, 0))
    full = lambda a, b: pl.BlockSpec((a, b), lambda i: (0, 0))
    return pl.pallas_call(
        body, name="mix_cross_fwd", grid=(t // tm,),
        in_specs=[row(d), row(d), full(d, d), full(1, d), full(d, d),
                  pl.BlockSpec((N_MEM, 2 * d), lambda i: (i // per, 0)), full(d, d)],
        out_specs=[row(d), row(d), row(d), row(d), row(d)],
        out_shape=[jax.ShapeDtypeStruct((t, d), F32), jax.ShapeDtypeStruct((t, d), F32),
                   jax.ShapeDtypeStruct((t, d), BF), jax.ShapeDtypeStruct((t, d), BF),
                   jax.ShapeDtypeStruct((t, d), BF)],
        compiler_params=_params(48, ("arbitrary",)),
    )(*_in_hbm(x, merged, w_out, gc, w_cq, kv, w_co))


def _cross_bwd(dh2, h1, qc, gc, w_cq, kv, w_co, s, tm):
    t, d = dh2.shape
    tm = min(tm, s)
    per = s // tm
    nb = t // s
    inv = 1.0 / math.sqrt(X_HEAD_DIM)

    def body(dh2_ref, h1_ref, qc_ref, gc_ref, wq_ref, kv_ref, wc_ref, dh1_ref, dqc_ref, dkv_ref, dgc_ref):
        i = pl.program_id(0)

        @pl.when(i == 0)
        def _():
            dgc_ref[...] = jnp.zeros_like(dgc_ref)

        @pl.when(i % per == 0)
        def _():
            dkv_ref[...] = jnp.zeros_like(dkv_ref)

        dh2 = dh2_ref[...]
        h1 = h1_ref[...]
        r = _rs(h1)
        h1h = h1 * r
        gcv = gc_ref[...]
        qc = qc_ref[...]
        do = _dot_bt(dh2.astype(BF), wc_ref[...]).astype(BF)
        for h in range(X_HEADS):
            cols = slice(h * X_HEAD_DIM, (h + 1) * X_HEAD_DIM)
            vcols = slice(d + h * X_HEAD_DIM, d + (h + 1) * X_HEAD_DIM)
            kh = kv_ref[:, cols]
            vh = kv_ref[:, vcols]
            p = _softmax_rows(_dot_bt(qc[:, cols], kh) * inv)
            dp = _dot_bt(do[:, cols], vh)
            ds = (p * (dp - jnp.sum(dp * p, axis=-1, keepdims=True)) * inv).astype(BF)
            dqc_ref[:, cols] = _dot(ds, kh).astype(BF)
            dkv_ref[:, cols] += _dot_at(ds, qc[:, cols])
            dkv_ref[:, vcols] += _dot_at(p.astype(BF), do[:, cols])
        dhn = _dot_bt(dqc_ref[...], wq_ref[...])
        dx, dg = _rms_bwd(dhn, h1h, r, gcv)
        dh1_ref[...] = dh2 + dx
        dgc_ref[...] += jnp.sum(dg, axis=0, keepdims=True)

    row = lambda width: pl.BlockSpec((tm, width), lambda i: (i, 0))
    full = lambda a, b: pl.BlockSpec((a, b), lambda i: (0, 0))
    kvspec = pl.BlockSpec((N_MEM, 2 * d), lambda i: (i // per, 0))
    return pl.pallas_call(
        body, name="cross_bwd", grid=(t // tm,),
        in_specs=[row(d), row(d), row(d), full(1, d), full(d, d), kvspec, full(d, d)],
        out_specs=[row(d), row(d), kvspec, full(1, d)],
        out_shape=[jax.ShapeDtypeStruct((t, d), F32), jax.ShapeDtypeStruct((t, d), BF),
                   jax.ShapeDtypeStruct((nb * N_MEM, 2 * d), F32), jax.ShapeDtypeStruct((1, d), F32)],
        compiler_params=_params(48, ("arbitrary",)),
    )(*_in_hbm(dh2, h1, qc, gc, w_cq, kv, w_co))


def _mem_bwd(mem, gm, dkv, w_ckv, tm):
    t, d = mem.shape
    tm = min(tm, t)

    def body(mem_ref, dkv_ref, w_ref, dg_ref):
        @pl.when(pl.program_id(0) == 0)
        def _():
            dg_ref[...] = jnp.zeros_like(dg_ref)

        mv = mem_ref[...]
        dmn = _dot_bt(dkv_ref[...].astype(BF), w_ref[...])
        dg_ref[...] += jnp.sum(dmn * (mv * _rs(mv)), axis=0, keepdims=True)

    del gm
    return pl.pallas_call(
        body, name="mem_bwd", grid=(t // tm,),
        in_specs=[pl.BlockSpec((tm, d), lambda i: (i, 0)), pl.BlockSpec((tm, 2 * d), lambda i: (i, 0)),
                  pl.BlockSpec((d, 2 * d), lambda i: (0, 0))],
        out_specs=pl.BlockSpec((1, d), lambda i: (0, 0)),
        out_shape=jax.ShapeDtypeStruct((1, d), F32),
        compiler_params=_params(32, ("arbitrary",)),
    )(mem, dkv, w_ckv)


def _ffn_loss_fwd(h2, gf, w1, w2, gl, target, tm):
    t, d = h2.shape
    tm = min(tm, t)

    def body(h2_ref, gf_ref, w1_ref, w2_ref, gl_ref, tg_ref, hn_ref, f_ref, dh3_ref, dgl_ref, loss_ref):
        @pl.when(pl.program_id(0) == 0)
        def _():
            dgl_ref[...] = jnp.zeros_like(dgl_ref)
            loss_ref[...] = jnp.zeros_like(loss_ref)

        h2 = h2_ref[...]
        hn = (h2 * _rs(h2) * gf_ref[...]).astype(BF)
        hn_ref[...] = hn
        h3 = h2
        for c in range(4):
            f = jnp.maximum(_dot(hn, w1_ref[c]), 0.0)
            f_ref[:, c * 1024:(c + 1) * 1024] = f.astype(BF)
            h3 = h3 + _dot((f * f).astype(BF), w2_ref[c])
        r3 = _rs(h3)
        yh = h3 * r3
        glv = gl_ref[...]
        e = yh * glv - tg_ref[...]
        loss_ref[...] += 0.5 * jnp.sum(jnp.sum(e * e, axis=-1, keepdims=True) * (1.0 / d), axis=0, keepdims=True)
        dy = e * (1.0 / d)
        dx, dg = _rms_bwd(dy, yh, r3, glv)
        dh3_ref[...] = dx
        dgl_ref[...] += jnp.sum(dg, axis=0, keepdims=True)

    row = lambda width: pl.BlockSpec((tm, width), lambda i: (i, 0))
    return pl.pallas_call(
        body, name="ffn_loss_fwd", grid=(t // tm,),
        in_specs=[row(d), pl.BlockSpec((1, d), lambda i: (0, 0)), pl.BlockSpec((4, d, 1024), lambda i: (0, 0, 0), pipeline_mode=pl.Buffered(1)),
                  pl.BlockSpec((4, 1024, d), lambda i: (0, 0, 0), pipeline_mode=pl.Buffered(1)),
                  pl.BlockSpec((1, d), lambda i: (0, 0)), row(d)],
        out_specs=[row(d), row(D_FF), row(d), pl.BlockSpec((1, d), lambda i: (0, 0)),
                   pl.BlockSpec((1, 1), lambda i: (0, 0))],
        out_shape=[jax.ShapeDtypeStruct((t, d), BF), jax.ShapeDtypeStruct((t, D_FF), BF),
                   jax.ShapeDtypeStruct((t, d), F32), jax.ShapeDtypeStruct((1, d), F32),
                   jax.ShapeDtypeStruct((1, 1), F32)],
        compiler_params=_params(56, ("arbitrary",)),
    )(*_in_hbm(h2, gf, w1, w2, gl, target))


def _ffn_bwd(dh3, f, h2, gf, w1, w2, tm):
    t, d = h2.shape
    tm = min(tm, t)

    def body(dh3_ref, f_ref, h2_ref, gf_ref, w1_ref, w2_ref, dh2_ref, dpre_ref, dgf_ref):
        @pl.when(pl.program_id(0) == 0)
        def _():
            dgf_ref[...] = jnp.zeros_like(dgf_ref)

        dh3 = dh3_ref[...]
        dh3b = dh3.astype(BF)
        dhn = jnp.zeros((tm, d), F32)
        for c in range(4):
            cols = slice(c * 1024, (c + 1) * 1024)
            dpre = (_dot_bt(dh3b, w2_ref[c]) * (2.0 * f_ref[:, cols].astype(F32))).astype(BF)
            dpre_ref[:, cols] = dpre
            dhn = dhn + _dot_bt(dpre, w1_ref[c])
        h2 = h2_ref[...]
        r = _rs(h2)
        dx, dg = _rms_bwd(dhn, h2 * r, r, gf_ref[...])
        dh2_ref[...] = dh3 + dx
        dgf_ref[...] += jnp.sum(dg, axis=0, keepdims=True)

    row = lambda width: pl.BlockSpec((tm, width), lambda i: (i, 0))
    return pl.pallas_call(
        body, name="ffn_bwd", grid=(t // tm,),
        in_specs=[row(d), row(D_FF), row(d), pl.BlockSpec((1, d), lambda i: (0, 0)),
                  pl.BlockSpec((4, d, 1024), lambda i: (0, 0, 0), pipeline_mode=pl.Buffered(1)),
                  pl.BlockSpec((4, 1024, d), lambda i: (0, 0, 0), pipeline_mode=pl.Buffered(1))],
        out_specs=[row(d), row(D_FF), pl.BlockSpec((1, d), lambda i: (0, 0))],
        out_shape=[jax.ShapeDtypeStruct((t, d), F32), jax.ShapeDtypeStruct((t, D_FF), BF),
                   jax.ShapeDtypeStruct((1, d), F32)],
        compiler_params=_params(56, ("arbitrary",)),
    )(*_in_hbm(dh3, f, h2, gf, w1, w2))


def _in_bwd(dproj, dh1, x, g, w_in, tm, ride=None):
    t, d = x.shape
    n = w_in.shape[1]
    tm = min(tm, t)

    def body(dp_ref, dh1_ref, x_ref, g_ref, w_ref, dx_ref, dg_ref):
        @pl.when(pl.program_id(0) == 0)
        def _():
            dg_ref[...] = jnp.zeros_like(dg_ref)

        dxn = _dot_bt(dp_ref[...], w_ref[...])
        xv = x_ref[...]
        r = _rs(xv)
        dx, dg = _rms_bwd(dxn, xv * r, r, g_ref[...])
        dx_ref[...] = dh1_ref[...] + dx
        dg_ref[...] += jnp.sum(dg, axis=0, keepdims=True)

    row = lambda width: pl.BlockSpec((tm, width), lambda i: (i, 0))
    (dx, dg), rode = _ride_call(
        body, "in_bwd", (t // tm,),
        in_specs=[row(n), row(d), row(d), pl.BlockSpec((1, d), lambda i: (0, 0)),
                  pl.BlockSpec((d, n), lambda i: (0, 0))],
        out_specs=[row(d), pl.BlockSpec((1, d), lambda i: (0, 0))],
        out_shape=[jax.ShapeDtypeStruct((t, d), F32), jax.ShapeDtypeStruct((1, d), F32)],
        scratch_shapes=[], operands=(dproj, dh1, x, g, w_in), vmem_mb=48, ride=ride)
    return dx, dg, rode


class _GradReduce:
    def __init__(self, c_idx):
        self.c_idx = c_idx
        self.sums = {}

    def sibling(self, slabs):
        return _SiblingExchange(slabs)

    def chip(self, names, slabs, recv):
        for k, a, r in zip(names, slabs, recv):
            self.sums[k] = _chip_sum(a, r, self.c_idx, "chip_sum_" + k)
        return _ChipExchange([self.sums[k] for k in names])


def _full_weights(gathered):
    d = D_MODEL
    out = {}
    for k, a in gathered.items():
        if k in ("w_in", "w_ckv", "w_ff1"):
            out[k] = a.transpose(1, 0, 2).reshape(d, -1)
        else:
            out[k] = a.reshape(-1, d)
    return out


def _slabs(a):
    return a.reshape(N_DEV, -1, a.shape[-1])


def _local_step(x, mem, target, small, big, nb, s, tq=256, gather_rest=None, reduce=None):
    d = D_MODEL
    g_mix, g_v, w_sp, b_sp, g_head, g_cross, g_mem, g_ffn, g_fin = (
        small[k] for k in ("norm_mix_g", "gm_v_norm_g", "w_spatial", "b_spatial", "head_norm_g", "norm_cross_g",
                           "norm_mem_g", "norm_ffn_g", "norm_final_g"))
    tri = jnp.tril(jnp.ones((CHUNK, CHUNK), dtype=bool))
    w_sp_m = jnp.where(tri[None], w_sp, 0.0)
    wt = w_sp_m.astype(BF)
    wtt = jnp.swapaxes(w_sp_m, 1, 2).astype(BF)
    bb = jnp.broadcast_to(b_sp[:, :, None], (GM_GROUPS, CHUNK, CHUNK))
    hg_a = g_head[:, :GM_WIDTH]

    proj, xn = _norm_matmul(x, g_mix, big["w_in"], 512, "in_proj")
    merged = _gmlp_fwd(proj, g_v, wt, bb, hg_a, 512)
    o_sb, tot, merged, nblk, gathered = _sb_fwd(proj, merged, g_head, nb, s, tq, ride=gather_rest)
    if gather_rest is not None:
        big = dict(big, **_full_weights(dict(zip(BIG[1:], gathered))))
    w1c = big["w_ff1"].reshape(d, 4, 1024).transpose(1, 0, 2)
    w2c = big["w_ff2"].reshape(4, 1024, d)
    kv, memn = _norm_matmul(mem, g_mem, big["w_ckv"], 512, "mem_proj")
    h1, h2, hn, qc, oc = _mix_cross_fwd(x, merged, big["w_out"], g_cross, big["w_cq"], kv, big["w_co"], s, 512)
    hn2, f, dh3, d_fin, loss = _ffn_loss_fwd(h2, g_ffn, w1c, w2c, g_fin, target, 512)

    gbig = {}
    dh2, dpre, d_ffn = _ffn_bwd(dh3, f, h2, g_ffn, w1c, w2c, 512)
    gbig["w_ff2"] = _slabs(_wgrad(f, dh3, 1024, 512, "wgrad_ff2", square_a=True, tm=2048))
    gbig["w_ff1"] = _slabs(_wgrad_wide(hn2, dpre, 1024, 512, "wgrad_ff1", col_shards=4))
    dh1, dqc, dkv, d_cross = _cross_bwd(dh2, h1, qc, g_cross, big["w_cq"], kv, big["w_co"], s, 512)
    gbig["w_co"] = _slabs(_wgrad(oc, dh2, 1024, 1024, "wgrad_co"))
    gbig["w_cq"] = _slabs(_wgrad(hn, dqc, 1024, 1024, "wgrad_cq"))
    gbig["w_ckv"] = _slabs(_wgrad(memn, dkv, 512, 1024, "wgrad_ckv", col_shards=4))
    d_mem = _mem_bwd(mem, g_mem, dkv, big["w_ckv"], 512)
    dmerged = _matmul_bt(dh1, big["w_out"], 512, "out_bwd")
    gbig["w_out"] = _slabs(_wgrad(merged, dh1, 1024, 1024, "wgrad_out"))
    rest = BIG[1:]
    ride = reduce.sibling([gbig[k] for k in rest]) if reduce else None
    dproj, d_wsp, d_bb, d_gv, d_hga, recv = _gmlp_bwd(proj, dmerged, g_v, wt, wtt, bb, hg_a, 512, ride=ride)
    ride = reduce.chip(rest, [gbig[k] for k in rest], recv) if reduce else None
    dproj, dk, dv, d_hgb, parts_rest = _sb_bwd(proj, o_sb, tot, nblk, dmerged, dproj, g_head, nb, s, tq, ride=ride)
    dproj = _place(_place(dproj, dk, 3, "place_dk"), dv, 4, "place_dv")
    gbig["w_in"] = _slabs(_wgrad_wide(xn, dproj, 512, 1024, "wgrad_in", col_shards=4))
    last = None
    if reduce:
        recv = _run_exchange(reduce.sibling([gbig["w_in"]]), "grad_sibling_exchange_w_in")
        last = reduce.chip(["w_in"], [gbig["w_in"]], recv)
    grad_x, d_mix, _ = _in_bwd(dproj, dh1, x, g_mix, big["w_in"], 512)
    parts = dict(zip(rest, parts_rest))

    gsmall = {
        "norm_mix_g": d_mix, "gm_v_norm_g": d_gv, "w_spatial": d_wsp, "b_spatial": d_bb[:, :, 0],
        "head_norm_g": jnp.concatenate([d_hga, jnp.sum(d_hgb, axis=0)], axis=1), "norm_cross_g": d_cross,
        "norm_mem_g": d_mem, "norm_ffn_g": d_ffn, "norm_final_g": d_fin,
    }
    return loss, grad_x, gsmall, gbig, parts, last


BIG = ("w_in", "w_out", "w_cq", "w_ckv", "w_co", "w_ff1", "w_ff2")
SMALL = ("norm_mix_g", "gm_v_norm_g", "w_spatial", "b_spatial", "head_norm_g", "norm_cross_g", "norm_mem_g",
         "norm_ffn_g", "norm_final_g")


def _local_copies_start(srcs, stages, sems):
    loads = [pltpu.make_async_copy(src, stage, sems.at[w]) for w, (src, stage) in enumerate(zip(srcs, stages))]
    for ld in loads:
        ld.start()
    return loads


def _local_copies_finish(loads, stages, dsts, sems):
    stores = []
    for w, (ld, stage, dst) in enumerate(zip(loads, stages, dsts)):
        ld.wait()
        st = pltpu.make_async_copy(stage, dst, sems.at[w])
        st.start()
        stores.append(st)
    for st in stores:
        st.wait()


def _chip_sum(slabs, recv, c_idx, name):
    _, r, cw = slabs.shape
    tr = min(r, 256)

    def body(c_ref, a_ref, b_ref, o_ref):
        del c_ref
        o_ref[...] = (a_ref[...] + b_ref[...]).astype(BF)

    return pl.pallas_call(
        body, name=name,
        grid_spec=pltpu.PrefetchScalarGridSpec(
            num_scalar_prefetch=1, grid=(N_CHIPS, r // tr),
            in_specs=[pl.BlockSpec((None, tr, cw), lambda p, i, c_ref: (2 * p + c_ref[0], i, 0)),
                      pl.BlockSpec((None, tr, cw), lambda p, i, c_ref: (p, i, 0))],
            out_specs=pl.BlockSpec((None, tr, cw), lambda p, i, c_ref: (p, i, 0))),
        out_shape=jax.ShapeDtypeStruct((N_CHIPS, r, cw), BF),
        compiler_params=_params(32, ("arbitrary", "arbitrary")),
    )(c_idx, *_in_hbm(slabs, recv))


def _sum4(sums, parts, q_idx, name):
    _, r, cw = parts.shape
    tr = min(r, 256)

    def body(q_ref, own_ref, a_ref, b_ref, c_ref, o_ref):
        del q_ref
        o_ref[...] = ((own_ref[...].astype(F32) + a_ref[...].astype(F32)) + b_ref[...].astype(F32)) + c_ref[
            ...].astype(F32)

    spec = lambda k: pl.BlockSpec((None, tr, cw), lambda i, q_ref: ((q_ref[0] + k) % N_CHIPS, i, 0))
    return pl.pallas_call(
        body, name=name,
        grid_spec=pltpu.PrefetchScalarGridSpec(
            num_scalar_prefetch=1, grid=(r // tr,), in_specs=[spec(0), spec(1), spec(2), spec(3)],
            out_specs=pl.BlockSpec((tr, cw), lambda i, q_ref: (i, 0))),
        out_shape=jax.ShapeDtypeStruct((r, cw), F32),
        compiler_params=_params(32, ("arbitrary",)),
    )(q_idx, *_in_hbm(sums, parts, parts, parts))


def _half_exchange(halves):
    n = len(halves)

    def body(*refs):
        ins, outs, stages = refs[:n], refs[n:2 * n], refs[2 * n:3 * n]
        send_sems, recv_sems, ld_sems, st_sems = refs[3 * n:]
        x, y, c = lax.axis_index("x"), lax.axis_index("y"), lax.axis_index("c")
        loads = _local_copies_start(ins, stages, ld_sems)
        copies = []
        for w in range(n):
            cp = pltpu.make_async_remote_copy(
                src_ref=ins[w], dst_ref=outs[w].at[c], send_sem=send_sems.at[w], recv_sem=recv_sems.at[w],
                device_id=(x, y, 1 - c), device_id_type=MESH)
            cp.start()
            copies.append(cp)
        _local_copies_finish(loads, stages, [outs[w].at[c] for w in range(n)], st_sems)
        for cp in copies:
            cp.wait()

    return pl.pallas_call(
        body, name="grad_half_exchange",
        in_specs=[ANY] * n, out_specs=[ANY] * n,
        out_shape=[jax.ShapeDtypeStruct((2,) + a.shape, a.dtype) for a in halves],
        scratch_shapes=[pltpu.VMEM(a.shape, a.dtype) for a in halves] + [
            pltpu.SemaphoreType.DMA((n,)), pltpu.SemaphoreType.DMA((n,)),
            pltpu.SemaphoreType.DMA((n,)), pltpu.SemaphoreType.DMA((n,))],
        compiler_params=_params(24),
    )(*halves)


def _small_all_reduce(packed, ride=None):
    rows = packed.shape[0]
    ride = ride or _NoExchange()
    ri, ro = len(ride.in_arrays), len(ride.out_shape)

    def body(*refs):
        in_ref, rins, out_ref, routs = refs[0], refs[1:1 + ri], refs[1 + ri], refs[2 + ri:2 + ri + ro]
        pair, chip_sum, chips, d2d_send, d2d_recv, ici_send, ici_recv = refs[2 + ri + ro:9 + ri + ro]
        rscr = refs[9 + ri + ro:]
        ride.start(rins, routs, rscr)
        x, y, c = lax.axis_index("x"), lax.axis_index("y"), lax.axis_index("c")
        q = 2 * x + y
        pair[c] = in_ref[...]
        swap = pltpu.make_async_remote_copy(
            src_ref=in_ref, dst_ref=pair.at[c], send_sem=d2d_send, recv_sem=d2d_recv,
            device_id=(x, y, 1 - c), device_id_type=MESH)
        swap.start()
        swap.wait()
        both = pair[0] + pair[1]
        chip_sum[...] = both
        chips[q] = both
        copies = [pltpu.make_async_remote_copy(
            src_ref=chip_sum, dst_ref=chips.at[q], send_sem=ici_send.at[k], recv_sem=ici_recv.at[k],
            device_id=(px, py, c), device_id_type=MESH) for k, (px, py) in enumerate(_other_chips(x, y))]
        for cp in copies:
            cp.start()
        for cp in copies:
            cp.wait()
        out_ref[...] = ((chips[0] + chips[1]) + chips[2]) + chips[3]
        ride.finish(rins, routs, rscr)

    vmem = pl.BlockSpec(memory_space=pltpu.VMEM)
    res = pl.pallas_call(
        body, name="small_all_reduce",
        in_specs=[vmem] + [ANY] * ri, out_specs=[vmem] + [ANY] * ro,
        out_shape=[jax.ShapeDtypeStruct(packed.shape, F32)] + list(ride.out_shape),
        scratch_shapes=[pltpu.VMEM((2, rows, 128), F32), pltpu.VMEM((rows, 128), F32),
                        pltpu.VMEM((N_CHIPS, rows, 128), F32), pltpu.SemaphoreType.DMA, pltpu.SemaphoreType.DMA,
                        pltpu.SemaphoreType.DMA((3,)), pltpu.SemaphoreType.DMA((3,))] + list(ride.scratch_shapes),
        compiler_params=_params(16),
    )(packed, *ride.in_arrays)
    return res[0], res[1:]


def _adamw(g, w, m, v, name):
    r, cw = g.shape
    tr = 256 if r % 256 == 0 else r

    def body(g_ref, w_ref, m_ref, v_ref, g_out_ref, d_ref, nm_ref, nv_ref):
        gv = g_ref[...]
        g_out_ref[...] = gv
        nm = ADAM_B1 * m_ref[...] + (1.0 - ADAM_B1) * gv
        nv = ADAM_B2 * v_ref[...] + (1.0 - ADAM_B2) * (gv * gv)
        m_hat = nm / (1.0 - ADAM_B1 ** ADAM_STEP)
        v_hat = nv / (1.0 - ADAM_B2 ** ADAM_STEP)
        d_ref[...] = -ADAM_LR * (m_hat / (jnp.sqrt(v_hat) + ADAM_EPS) + ADAM_WD * w_ref[...])
        nm_ref[...] = nm
        nv_ref[...] = nv

    spec = pl.BlockSpec((tr, cw), lambda i: (i, 0))
    return pl.pallas_call(
        body, name=name, grid=(r // tr,),
        in_specs=[spec] * 4, out_specs=[spec] * 4,
        out_shape=[jax.ShapeDtypeStruct((r, cw), F32)] * 4,
        compiler_params=_params(32, ("arbitrary",)),
    )(*_in_hbm(g, w, m, v))


def _small_params(args):
    small = {k: args[k].reshape(1, -1) for k in SMALL}
    small["w_spatial"] = args["w_spatial"][0]
    small["b_spatial"] = args["b_spatial"][0]
    return small


def _pack(parts, rows):
    flat = jnp.concatenate([p.reshape(-1).astype(F32) for p in parts])
    return jnp.pad(flat, (0, rows * 128 - flat.shape[0])).reshape(rows, 128)


def _unpack(packed, shapes):
    flat = packed.reshape(-1)
    out, off = [], 0
    for shp in shapes:
        size = math.prod(shp)
        out.append(flat[off:off + size].reshape(shp))
        off += size
    return out


def kernel(x, mem, norm_mix_g, w_in, gm_v_norm_g, w_spatial, b_spatial, head_norm_g, w_out, norm_cross_g, norm_mem_g, w_cq, w_ckv, w_co, norm_ffn_g, w_ff1, w_ff2, norm_final_g, loss_target, m_norm_mix_g, m_w_in, m_gm_v_norm_g, m_w_spatial, m_b_spatial, m_head_norm_g, m_w_out, m_norm_cross_g, m_norm_mem_g, m_w_cq, m_w_ckv, m_w_co, m_norm_ffn_g, m_w_ff1, m_w_ff2, m_norm_final_g, v_norm_mix_g, v_w_in, v_gm_v_norm_g, v_w_spatial, v_b_spatial, v_head_norm_g, v_w_out, v_norm_cross_g, v_norm_mem_g, v_w_cq, v_w_ckv, v_w_co, v_norm_ffn_g, v_w_ff1, v_w_ff2, v_norm_final_g):
    args = dict(locals())
    d = D_MODEL
    nb, s, _ = x.shape
    c_idx = lax.axis_index("c").astype(jnp.int32).reshape(1)
    q_idx = (2 * lax.axis_index("x") + lax.axis_index("y")).astype(jnp.int32).reshape(1)
    rest = BIG[1:]

    shards = {k: args[k][0].astype(BF) for k in BIG}
    big = _full_weights({"w_in": _run_exchange(_GatherExchange([shards["w_in"]]), "all_gather_w_in")[0]})
    gather_rest = _GatherExchange([shards[k] for k in rest])

    reduce = _GradReduce(c_idx)
    loss, grad_x, gsmall, _, parts, last = _local_step(
        x.reshape(nb * s, d), mem.reshape(nb * N_MEM, d), loss_target.reshape(nb * s, d), _small_params(args), big,
        nb, s, gather_rest=gather_rest, reduce=reduce)

    shapes = [args[k].shape for k in SMALL]
    n_small = sum(math.prod(sh) for sh in shapes)
    rows = -(-(n_small + 1) // 1024) * 8
    reduced, (parts["w_in"],) = _small_all_reduce(_pack([gsmall[k] for k in SMALL] + [loss], rows), ride=last)
    halves = [_sum4(reduce.sums[k], parts[k], q_idx, "sum4_" + k) for k in BIG]
    both = _half_exchange(halves)

    out = {"grad_x": grad_x.reshape(nb, s, d)}
    for k, g2 in zip(BIG, both):
        shp = args[k].shape
        g, dl, nm, nv = _adamw(g2.reshape(shp[1], shp[2]), args[k][0], args["m_" + k][0], args["v_" + k][0],
                               "adamw_" + k)
        out["grad_" + k], out["delta_" + k], out["new_m_" + k], out["new_v_" + k] = (
            a.reshape(shp) for a in (g, dl, nm, nv))

    reduced, dl, nm, nv = _adamw(
        reduced, _pack([args[k] for k in SMALL], rows), _pack([args["m_" + k] for k in SMALL], rows),
        _pack([args["v_" + k] for k in SMALL], rows), "adamw_small")
    for name, arr in (("grad_", reduced), ("delta_", dl), ("new_m_", nm), ("new_v_", nv)):
        for k, a in zip(SMALL, _unpack(arr, shapes)):
            out[name + k] = a
    out["loss"] = reduced.reshape(-1)[n_small]

    names = ["norm_mix_g", "w_in", "gm_v_norm_g", "w_spatial", "b_spatial", "head_norm_g", "w_out", "norm_cross_g",
             "norm_mem_g", "w_cq", "w_ckv", "w_co", "norm_ffn_g", "w_ff1", "w_ff2", "norm_final_g"]
    return (out["loss"], out["grad_x"], *[out["grad_" + k] for k in names], *[out["delta_" + k] for k in names],
            *[out["new_m_" + k] for k in names], *[out["new_v_" + k] for k in names])
```

```python
import functools
import math

import jax
import jax.numpy as jnp
from jax import lax
from jax.experimental import pallas as pl
from jax.experimental.pallas import tpu as pltpu

F32 = jnp.float32
BF = jnp.bfloat16

EPS = 1e-6
D_MODEL = 1024
CHUNK = 128
GM_GROUPS = 4
GM_WIDTH = 512
SB_WIDTH = 512
HEAD_LANES = 64
SB_SCALE = 0.125
SB_SKIP = -104.0
X_HEADS = 4
X_HEAD_DIM = 256
N_MEM = 256
D_FF = 4096
IN_COLS = 2560
N_CHIPS = 4
N_DEV = 8

ADAM_LR = 0.001
ADAM_B1 = 0.9
ADAM_B2 = 0.999
ADAM_EPS = 1e-08
ADAM_WD = 0.01
ADAM_STEP = 10

V7X_VMEM_BYTES = 64 * 1024 * 1024
MESH = pl.DeviceIdType.MESH
ANY = pl.BlockSpec(memory_space=pl.ANY)

GELU_C = math.sqrt(2.0 / math.pi)
GELU_A = 0.044715


def _params(vmem_mb, sem=None):
    assert vmem_mb * 1024 * 1024 <= V7X_VMEM_BYTES
    return pltpu.CompilerParams(vmem_limit_bytes=vmem_mb * 1024 * 1024, dimension_semantics=sem)


PIN_MIN_ELEMENTS = 1 << 18


def _in_hbm(*arrays):
    return tuple(pltpu.with_memory_space_constraint(a, pltpu.HBM) if a.size >= PIN_MIN_ELEMENTS else a
                 for a in arrays)


def _dot(a, b):
    return jnp.dot(a, b, preferred_element_type=F32)


def _dot_bt(a, b):
    return lax.dot_general(a, b, (((1,), (1,)), ((), ())), preferred_element_type=F32)


def _dot_at(a, b):
    return lax.dot_general(a, b, (((0,), (0,)), ((), ())), preferred_element_type=F32)


def _gelu(x):
    t = jnp.tanh(GELU_C * (x + GELU_A * x * x * x))
    return 0.5 * x * (1.0 + t)


def _gelu_and_grad(x):
    x2 = x * x
    t = jnp.tanh(GELU_C * (x + GELU_A * x2 * x))
    h = 0.5 * (1.0 + t)
    return x * h, h + 0.5 * x * (1.0 - t * t) * (GELU_C * (1.0 + 3.0 * GELU_A * x2))


def _rs(x):
    return lax.rsqrt(jnp.mean(x * x, axis=-1, keepdims=True) + EPS)


def _rms_bwd(dxn, xhat, r, g):
    dxh = dxn * g
    dx = r * (dxh - xhat * jnp.mean(dxh * xhat, axis=-1, keepdims=True))
    return dx, dxn * xhat


def _norm_matmul(x, g, w, tm, name):
    t, d = x.shape
    n = w.shape[1]
    tm = min(tm, t)

    def body(x_ref, g_ref, w_ref, out_ref, xn_ref):
        xv = x_ref[...]
        xn = (xv * _rs(xv) * g_ref[...]).astype(BF)
        xn_ref[...] = xn
        out_ref[...] = _dot(xn, w_ref[...]).astype(out_ref.dtype)

    return pl.pallas_call(
        body, name=name, grid=(t // tm,),
        in_specs=[pl.BlockSpec((tm, d), lambda i: (i, 0)), pl.BlockSpec((1, d), lambda i: (0, 0)),
                  pl.BlockSpec((d, n), lambda i: (0, 0))],
        out_specs=[pl.BlockSpec((tm, n), lambda i: (i, 0)), pl.BlockSpec((tm, d), lambda i: (i, 0))],
        out_shape=[jax.ShapeDtypeStruct((t, n), BF), jax.ShapeDtypeStruct((t, d), BF)],
        compiler_params=_params(48, ("arbitrary",)),
    )(*_in_hbm(x, g, w))


def _wgrad(a, g, tn, tk, name, square_a=False, col_shards=1, tm=1024):
    t, m = a.shape
    n = g.shape[1]
    tk = min(tk, t)
    tm = min(m, tm)
    ns = n // col_shards
    assert ns % tn == 0 and m % tm == 0
    per = ns // tn
    nk = t // tk

    def body(a_ref, g_ref, o_ref):
        k = pl.program_id(2)

        @pl.when(k == 0)
        def _():
            o_ref[...] = jnp.zeros_like(o_ref)

        av = a_ref[...]
        if square_a:
            af = av.astype(F32)
            av = af * af
        o_ref[...] += _dot_at(av.astype(BF), g_ref[...].astype(BF))

    return pl.pallas_call(
        body, name=name, grid=(m // tm, n // tn, nk),
        in_specs=[pl.BlockSpec((tk, tm), lambda i, j, k: (k, i)), pl.BlockSpec((tk, tn), lambda i, j, k: (k, j))],
        out_specs=pl.BlockSpec((None, tm, tn), lambda i, j, k: (j // per, i, j % per)),
        out_shape=jax.ShapeDtypeStruct((col_shards, m, ns), F32),
        compiler_params=_params(48, ("arbitrary", "arbitrary", "arbitrary")),
    )(*_in_hbm(a, g))


def _wgrad_wide(a, g, tm, tk, name, col_shards):
    t, m = a.shape
    n = g.shape[1]
    tk = min(tk, t)
    tm = min(tm, m)
    ns = n // col_shards

    def body(a_ref, g_ref, o_ref):
        @pl.when(pl.program_id(1) == 0)
        def _():
            o_ref[...] = jnp.zeros_like(o_ref)

        a_t = a_ref[...].astype(BF).T
        for p in range(col_shards):
            o_ref[p] += _dot(a_t, g_ref[:, p * ns:(p + 1) * ns].astype(BF))

    return pl.pallas_call(
        body, name=name, grid=(m // tm, t // tk),
        in_specs=[pl.BlockSpec((tk, tm), lambda i, k: (k, i)), pl.BlockSpec((tk, n), lambda i, k: (k, 0))],
        out_specs=pl.BlockSpec((col_shards, tm, ns), lambda i, k: (0, i, 0)),
        out_shape=jax.ShapeDtypeStruct((col_shards, m, ns), F32),
        compiler_params=_params(48, ("arbitrary", "arbitrary")),
    )(*_in_hbm(a, g))


def _matmul_bt(a, w, tm, name):
    t, n = a.shape
    k = w.shape[0]
    tm = min(tm, t)

    def body(a_ref, w_ref, o_ref):
        o_ref[...] = _dot_bt(a_ref[...].astype(BF), w_ref[...]).astype(o_ref.dtype)

    return pl.pallas_call(
        body, name=name, grid=(t // tm,),
        in_specs=[pl.BlockSpec((tm, n), lambda i: (i, 0)), pl.BlockSpec((k, n), lambda i: (0, 0))],
        out_specs=pl.BlockSpec((tm, k), lambda i: (i, 0)),
        out_shape=jax.ShapeDtypeStruct((t, k), BF),
        compiler_params=_params(32, ("arbitrary",)),
    )(*_in_hbm(a, w))


def _gmlp_fwd(proj, gg, wt, bb, hg, tm):
    t = proj.shape[0]
    tm = min(tm, t)

    def body(u_ref, v_ref, gg_ref, wt_ref, bb_ref, hg_ref, out_ref):
        for cc in range(tm // CHUNK):
            rows = slice(cc * CHUNK, (cc + 1) * CHUNK)
            for g in range(GM_GROUPS):
                cols = slice(g * 128, (g + 1) * 128)
                u = _gelu(u_ref[rows, cols].astype(F32))
                gv = _gelu(v_ref[rows, cols].astype(F32))
                vn = gv * _rs(gv) * gg_ref[:, cols]
                mixed = _dot(wt_ref[g], vn.astype(BF)) + bb_ref[g]
                a = u * mixed
                out_ref[rows, cols] = (a * _rs(a) * hg_ref[:, cols]).astype(BF)

    return pl.pallas_call(
        body, name="gmlp_fwd", grid=(t // tm,),
        in_specs=[pl.BlockSpec((tm, 512), lambda i: (i, 0)), pl.BlockSpec((tm, 512), lambda i: (i, 1)),
                  pl.BlockSpec((1, 512), lambda i: (0, 0)), pl.BlockSpec((4, 128, 128), lambda i: (0, 0, 0)),
                  pl.BlockSpec((4, 128, 128), lambda i: (0, 0, 0)), pl.BlockSpec((1, 512), lambda i: (0, 0))],
        out_specs=pl.BlockSpec((tm, 512), lambda i: (i, 0)),
        out_shape=jax.ShapeDtypeStruct((t, 1024), BF),
        compiler_params=_params(32, ("arbitrary",)),
    )(*_in_hbm(proj, proj, gg, wt, bb, hg))


def _gmlp_bwd(proj, dmerged, gg, wt, wtt, bb, hg, tm, ride=None):
    t = proj.shape[0]
    tm = min(tm, t)
    nsteps = t // tm

    def body(u_ref, v_ref, dm_ref, gg_ref, wt_ref, wtt_ref, bb_ref, hg_ref,
             dp_ref, dw_ref, db_ref, dgg_ref, dhg_ref):
        i = pl.program_id(0)

        @pl.when(i == 0)
        def _():
            dw_ref[...] = jnp.zeros_like(dw_ref)
            db_ref[...] = jnp.zeros_like(db_ref)
            dgg_ref[...] = jnp.zeros_like(dgg_ref)
            dhg_ref[...] = jnp.zeros_like(dhg_ref)

        for cc in range(tm // CHUNK):
            rows = slice(cc * CHUNK, (cc + 1) * CHUNK)
            for g in range(GM_GROUPS):
                cols = slice(g * 128, (g + 1) * 128)
                up = u_ref[rows, cols].astype(F32)
                gp = v_ref[rows, cols].astype(F32)
                u, u_grad = _gelu_and_grad(up)
                gv, gv_grad = _gelu_and_grad(gp)
                rv = _rs(gv)
                gvh = gv * rv
                ggv = gg_ref[:, cols]
                vnb = (gvh * ggv).astype(BF)
                mixed = _dot(wt_ref[g], vnb) + bb_ref[g]
                a = u * mixed
                ra = _rs(a)
                ah = a * ra
                dm = dm_ref[rows, cols].astype(F32)
                dhg_ref[:, cols] += jnp.sum(dm * ah, axis=0, keepdims=True)
                dah = dm * hg_ref[:, cols]
                da = ra * (dah - ah * jnp.mean(dah * ah, axis=-1, keepdims=True))
                du = da * mixed
                dmix = da * u
                db_ref[g] += dmix
                dmb = dmix.astype(BF)
                dw_ref[g] += _dot_bt(dmb, vnb)
                dvn = _dot(wtt_ref[g], dmb)
                dgg_ref[:, cols] += jnp.sum(dvn * gvh, axis=0, keepdims=True)
                dgh = dvn * ggv
                dgv = rv * (dgh - gvh * jnp.mean(dgh * gvh, axis=-1, keepdims=True))
                dp_ref[rows, cols] = (du * u_grad).astype(BF)
                dp_ref[rows, 512 + g * 128:512 + (g + 1) * 128] = (dgv * gv_grad).astype(BF)

        @pl.when(i == nsteps - 1)
        def _():
            r = lax.broadcasted_iota(jnp.int32, (CHUNK, CHUNK), 0)
            c = lax.broadcasted_iota(jnp.int32, (CHUNK, CHUNK), 1)
            for g in range(GM_GROUPS):
                dw_ref[g] = jnp.where(c <= r, dw_ref[g], 0.0)
                db_ref[g] = jnp.broadcast_to(jnp.sum(db_ref[g], axis=-1, keepdims=True), (CHUNK, CHUNK))

    small = lambda shape: pl.BlockSpec(shape, lambda i: (0,) * len(shape))
    res, rode = _ride_call(
        body, "gmlp_bwd", (nsteps,),
        in_specs=[pl.BlockSpec((tm, 512), lambda i: (i, 0)), pl.BlockSpec((tm, 512), lambda i: (i, 1)),
                  pl.BlockSpec((tm, 512), lambda i: (i, 0)), small((1, 512)), small((4, 128, 128)),
                  small((4, 128, 128)), small((4, 128, 128)), small((1, 512))],
        out_specs=[pl.BlockSpec((tm, 1024), lambda i: (i, 0)), small((4, 128, 128)), small((4, 128, 128)),
                   small((1, 512)), small((1, 512))],
        out_shape=[jax.ShapeDtypeStruct((t, IN_COLS), BF), jax.ShapeDtypeStruct((4, 128, 128), F32),
                   jax.ShapeDtypeStruct((4, 128, 128), F32), jax.ShapeDtypeStruct((1, 512), F32),
                   jax.ShapeDtypeStruct((1, 512), F32)],
        scratch_shapes=[], operands=(proj, proj, dmerged, gg, wt, wtt, bb, hg), vmem_mb=32, ride=ride)
    return (*res, rode)


def _other_chips(x, y):
    return ((1 - x, y), (x, 1 - y), (1 - x, 1 - y))


class _GatherExchange:
    def __init__(self, shards):
        n = len(shards)
        self.n = n
        self.in_arrays = list(shards)
        self.out_shape = [jax.ShapeDtypeStruct((N_CHIPS,) + a.shape, a.dtype) for a in shards]
        self.half_rows = [a.shape[0] // 2 for a in shards]
        sems = lambda k: pltpu.SemaphoreType.DMA((k,))
        self.scratch_shapes = [pltpu.VMEM(a.shape, a.dtype) for a in shards] + [
            sems(3 * n), sems(3 * n), sems(3 * n), sems(3 * n), sems(n), sems(n)]

    def _copies(self, ins, outs, scr):
        n = self.n
        stages, (ici_send, ici_recv, d2d_send, d2d_recv, ld_sems, st_sems) = scr[:n], scr[n:]
        x, y, c = lax.axis_index("x"), lax.axis_index("y"), lax.axis_index("c")
        q = 2 * x + y
        loads = [pltpu.make_async_copy(ins[w], stages[w], ld_sems.at[w]) for w in range(n)]
        stores = [pltpu.make_async_copy(stages[w], outs[w].at[q], st_sems.at[w]) for w in range(n)]
        ici, d2d = [], []
        for w in range(n):
            half = pl.ds(c * self.half_rows[w], self.half_rows[w])
            for k, (px, py) in enumerate(_other_chips(x, y)):
                ici.append(pltpu.make_async_remote_copy(
                    src_ref=ins[w].at[half], dst_ref=outs[w].at[q, half], send_sem=ici_send.at[3 * w + k],
                    recv_sem=ici_recv.at[3 * w + k], device_id=(px, py, c), device_id_type=MESH))
                landed = outs[w].at[2 * px + py, half]
                d2d.append(pltpu.make_async_remote_copy(
                    src_ref=landed, dst_ref=landed, send_sem=d2d_send.at[3 * w + k],
                    recv_sem=d2d_recv.at[3 * w + k], device_id=(x, y, 1 - c), device_id_type=MESH))
        return loads, stores, ici, d2d

    def start(self, ins, outs, scr):
        loads, stores, ici, _ = self._copies(ins, outs, scr)
        for cp in loads + ici:
            cp.start()
        for ld, st in zip(loads, stores):
            ld.wait()
            st.start()

    def relay(self, ins, outs, scr):
        _, _, ici, d2d = self._copies(ins, outs, scr)
        for got, fwd in zip(ici, d2d):
            got.wait_recv()
            fwd.start()

    def finish(self, ins, outs, scr):
        _, stores, ici, d2d = self._copies(ins, outs, scr)
        for cp in ici:
            cp.wait_send()
        for cp in d2d + stores:
            cp.wait()


class _SiblingExchange:
    def __init__(self, slabs):
        n = len(slabs)
        self.n = n
        self.in_arrays = list(slabs)
        self.out_shape = [jax.ShapeDtypeStruct((N_CHIPS,) + a.shape[1:], a.dtype) for a in slabs]
        self.scratch_shapes = [pltpu.SemaphoreType.DMA((4 * n,)), pltpu.SemaphoreType.DMA((4 * n,))]

    def _copies(self, ins, outs, scr):
        send_sems, recv_sems = scr
        x, y, c = lax.axis_index("x"), lax.axis_index("y"), lax.axis_index("c")
        return [pltpu.make_async_remote_copy(
            src_ref=ins[w].at[2 * p + (1 - c)], dst_ref=outs[w].at[p], send_sem=send_sems.at[4 * w + p],
            recv_sem=recv_sems.at[4 * w + p], device_id=(x, y, 1 - c), device_id_type=MESH)
            for w in range(self.n) for p in range(N_CHIPS)]

    def start(self, ins, outs, scr):
        for cp in self._copies(ins, outs, scr):
            cp.start()

    def finish(self, ins, outs, scr):
        for cp in self._copies(ins, outs, scr):
            cp.wait()


class _ChipExchange:
    def __init__(self, sums):
        n = len(sums)
        self.n = n
        self.in_arrays = list(sums)
        self.out_shape = [jax.ShapeDtypeStruct(a.shape, a.dtype) for a in sums]
        self.scratch_shapes = [pltpu.SemaphoreType.DMA((3 * n,)), pltpu.SemaphoreType.DMA((3 * n,))]

    def _copies(self, ins, outs, scr):
        send_sems, recv_sems = scr
        x, y, c = lax.axis_index("x"), lax.axis_index("y"), lax.axis_index("c")
        q = 2 * x + y
        return [pltpu.make_async_remote_copy(
            src_ref=ins[w].at[2 * px + py], dst_ref=outs[w].at[q], send_sem=send_sems.at[3 * w + k],
            recv_sem=recv_sems.at[3 * w + k], device_id=(px, py, c), device_id_type=MESH)
            for w in range(self.n) for k, (px, py) in enumerate(_other_chips(x, y))]

    def start(self, ins, outs, scr):
        for cp in self._copies(ins, outs, scr):
            cp.start()

    def finish(self, ins, outs, scr):
        for cp in self._copies(ins, outs, scr):
            cp.wait()


class _NoExchange:
    in_arrays, out_shape, scratch_shapes = (), (), ()

    def start(self, ins, outs, scr):
        pass

    def finish(self, ins, outs, scr):
        pass


def _run_exchange(ex, name):
    n_in, n_out = len(ex.in_arrays), len(ex.out_shape)

    def body(*refs):
        ins, outs, scr = refs[:n_in], refs[n_in:n_in + n_out], refs[n_in + n_out:]
        ex.start(ins, outs, scr)
        if hasattr(ex, "relay"):
            ex.relay(ins, outs, scr)
        ex.finish(ins, outs, scr)

    return pl.pallas_call(
        body, name=name, in_specs=[ANY] * n_in, out_specs=[ANY] * n_out, out_shape=ex.out_shape,
        scratch_shapes=ex.scratch_shapes, compiler_params=_params(24),
    )(*ex.in_arrays)


def _ride_call(body, name, grid, in_specs, out_specs, out_shape, scratch_shapes, operands, vmem_mb, ride=None,
               aliases=None):
    ride = ride or _NoExchange()
    ni, no, ns = len(in_specs), len(out_specs), len(scratch_shapes)
    ri, ro = len(ride.in_arrays), len(ride.out_shape)
    total = math.prod(grid)

    def wrapped(*refs):
        ins, rins = refs[:ni], refs[ni:ni + ri]
        outs, routs = refs[ni + ri:ni + ri + no], refs[ni + ri + no:ni + ri + no + ro]
        scr, rscr = refs[ni + ri + no + ro:ni + ri + no + ro + ns], refs[ni + ri + no + ro + ns:]
        step = pl.program_id(0)
        for ax in range(1, len(grid)):
            step = step * grid[ax] + pl.program_id(ax)

        @pl.when(step == 0)
        def _():
            ride.start(rins, routs, rscr)

        if hasattr(ride, "relay"):
            @pl.when(step == (3 * total) // 4)
            def _():
                ride.relay(rins, routs, rscr)

        body(*ins, *outs, *scr)

        @pl.when(step == total - 1)
        def _():
            ride.finish(rins, routs, rscr)

    res = pl.pallas_call(
        wrapped, name=name, grid=grid, in_specs=list(in_specs) + [ANY] * ri, out_specs=list(out_specs) + [ANY] * ro,
        out_shape=list(out_shape) + list(ride.out_shape),
        scratch_shapes=list(scratch_shapes) + list(ride.scratch_shapes), input_output_aliases=aliases or {},
        compiler_params=_params(vmem_mb, ("arbitrary",) * len(grid)),
    )(*_in_hbm(*operands), *ride.in_arrays)
    return res[:no], res[no:]


def _neg_log_sig(z):
    n = jnp.maximum(z, 0.0) + jnp.log(1.0 + jnp.exp(-jnp.abs(z)))
    return n, z - n


def _running_sums(n, tri2):
    hi = n.astype(BF)
    lo = (n - hi.astype(F32)).astype(BF)
    return _dot(jnp.concatenate([hi, lo], axis=1), tri2)


def _head_sums(x, h0):
    s0 = jnp.sum(jnp.where(h0, x, 0.0), axis=-1, keepdims=True)
    s1 = jnp.sum(jnp.where(h0, 0.0, x), axis=-1, keepdims=True)
    return jnp.where(h0, s0, s1)


SB_BLOCKS_PER_STEP = 4
SB_PAIRS_PER_STEP = 2


def _sb_masks(tq):
    h0 = lax.broadcasted_iota(jnp.int32, (tq, 128), 1) < HEAD_LANES
    r = lax.broadcasted_iota(jnp.int32, (2 * tq, tq), 0)
    c = lax.broadcasted_iota(jnp.int32, (2 * tq, tq), 1)
    return h0, c < jnp.where(r >= tq, r - tq, r)


def _sb_stack(x, h0):
    zero = jnp.zeros_like(x)
    return jnp.concatenate([jnp.where(h0, x, zero), jnp.where(h0, zero, x)], axis=0)


def _tri(tq, op):
    return op(lax.broadcasted_iota(jnp.int32, (tq, tq), 0), lax.broadcasted_iota(jnp.int32, (tq, tq), 1)).astype(BF)


def _sb_fwd(proj, merged, hg, nb, s, tq, ride=None):
    t = nb * s
    tq = min(tq, s)
    nq = s // tq
    per = min(SB_BLOCKS_PER_STEP, nq)
    ns = nq // per
    gp, ng, w = SB_PAIRS_PER_STEP, 4 // SB_PAIRS_PER_STEP, 128 * SB_PAIRS_PER_STEP

    def body(q_ref, k_ref, v_ref, hg_ref, merged_ref, o_ref, tot_ref, mb_ref, nblk_ref, acc, cr, c_min):
        del merged_ref
        h0, causal = _sb_masks(tq)
        tri_gt = _tri(tq, lambda r, c: r > c)
        tri_gt = jnp.concatenate([tri_gt, tri_gt], axis=0)
        lanes = [slice(g * 128, (g + 1) * 128) for g in range(gp)]
        zeros = jnp.zeros((2 * tq, 1), F32)

        def query_block(i, rows):
            qsts = [_sb_stack(q_ref[rows, lanes[g]] * SB_SCALE, h0) for g in range(gp)]

            def block(g, j, masked, c_in):
                start = pl.multiple_of(j * tq, tq)
                kj = k_ref[pl.ds(start, tq), lanes[g]]
                vj = v_ref[pl.ds(start, tq), lanes[g]]
                n, l = _neg_log_sig(_dot_bt(qsts[g], kj))
                if masked:
                    n = jnp.where(causal, n, 0.0)
                a = jnp.exp(l - (_running_sums(n, tri_gt) + c_in))
                if masked:
                    a = jnp.where(causal, a, 0.0)
                return _dot(a.astype(BF), vj), c_in + jnp.sum(n, axis=-1, keepdims=True)

            def keep(parts):
                for g, (p, c) in enumerate(parts):
                    acc[g] = p
                    cr[g] = c
                c_min[0] = jnp.min(functools.reduce(jnp.minimum, [c for _, c in parts]))

            @pl.when(i == 0)
            def _():
                keep([block(g, 0, True, zeros) for g in range(gp)])

            @pl.when(i > 0)
            def _():
                diag = [block(g, i, True, zeros) for g in range(gp)]
                prev = [block(g, i - 1, False, diag[g][1]) for g in range(gp)]
                keep([(diag[g][0] + prev[g][0], prev[g][1]) for g in range(gp)])

            def cond(carry):
                return jnp.logical_and(carry[0] < i, carry[1] < -SB_SKIP)

            def step(carry):
                more = [block(g, i - 1 - carry[0], False, cr[g]) for g in range(gp)]
                for g, (p, c) in enumerate(more):
                    acc[g] += p
                    cr[g] = c
                return carry[0] + 1, jnp.min(functools.reduce(jnp.minimum, [c for _, c in more]))

            walked, _ = lax.while_loop(cond, step, (jnp.minimum(i, 1), c_min[0]))
            return walked

        for u in range(per):
            rows = slice(u * tq, (u + 1) * tq)
            walked = query_block(pl.program_id(2) * per + u, rows)
            for g in range(gp):
                o = jnp.where(h0, acc[g, 0:tq, :], acc[g, tq:2 * tq, :])
                o_ref[rows, lanes[g]] = o
                tot_ref[rows, lanes[g]] = jnp.where(h0, cr[g, 0:tq, :], cr[g, tq:2 * tq, :])
                ro = lax.rsqrt(_head_sums(o * o, h0) * (1.0 / HEAD_LANES) + EPS)
                mb_ref[rows, lanes[g]] = (o * ro * hg_ref[:, lanes[g]]).astype(BF)
            nblk_ref[u * 8:(u + 1) * 8, :] = jnp.full((8, 128), walked.astype(F32))

    blk = lambda col0: pl.BlockSpec((per * tq, w), lambda b, hg_, i: (b * ns + i, col0 + hg_))
    seq = lambda col0: pl.BlockSpec((s, w), lambda b, hg_, i: (b, col0 + hg_))
    first = 1024 // w
    (o, tot, mb, nblk), rode = _ride_call(
        body, "sb_fwd", (nb, ng, ns),
        in_specs=[blk(first), seq(first + ng), seq(first + 2 * ng),
                  pl.BlockSpec((1, w), lambda b, hg_, i: (0, ng + hg_)), ANY],
        out_specs=[blk(0), blk(0), blk(ng),
                   pl.BlockSpec((None, None, per * 8, 128), lambda b, hg_, i: (b, hg_, i, 0))],
        out_shape=[jax.ShapeDtypeStruct((t, 512), F32), jax.ShapeDtypeStruct((t, 512), F32),
                   jax.ShapeDtypeStruct((t, 1024), BF), jax.ShapeDtypeStruct((nb, ng, nq * 8, 128), F32)],
        scratch_shapes=[pltpu.VMEM((gp, 2 * tq, 128), F32), pltpu.VMEM((gp, 2 * tq, 1), F32),
                        pltpu.SMEM((1,), F32)],
        operands=(proj, proj, proj, hg, merged), vmem_mb=40, ride=ride, aliases={4: 2})
    return o, tot, mb, nblk, rode


def _sb_bwd(proj, o_sb, tot, nblk, dmerged, dproj, hg, nb, s, tq, ride=None):
    t = nb * s
    tq = min(tq, s)
    nq = s // tq
    per = min(SB_BLOCKS_PER_STEP, nq)
    ns = nq // per
    gp, ng, w = SB_PAIRS_PER_STEP, 4 // SB_PAIRS_PER_STEP, 128 * SB_PAIRS_PER_STEP

    def body(q_ref, k_ref, v_ref, o_ref, tot_ref, nblk_ref, dm_ref, hg_ref, dproj_ref,
             dq_ref, dk_ref, dv_ref, dhg_ref, dk_acc, dv_acc, dq_acc, cm, cg):
        del dproj_ref
        h0, causal = _sb_masks(tq)
        tri_le = _tri(tq, lambda r, c: r <= c)
        tri_le = jnp.concatenate([tri_le, tri_le], axis=0)
        tri_lt = _tri(tq, lambda r, c: r < c)
        lanes = [slice(g * 128, (g + 1) * 128) for g in range(gp)]

        @pl.when(pl.program_id(2) == 0)
        def _():
            dk_acc[...] = jnp.zeros_like(dk_acc)
            dv_acc[...] = jnp.zeros_like(dv_acc)
            dhg_ref[...] = jnp.zeros_like(dhg_ref)

        def query_block(i, rows):
            for ref in (dq_acc, cm, cg):
                ref[...] = jnp.zeros_like(ref)
            qsts, dosts, tots = [], [], []
            for g in range(gp):
                qsts.append(_sb_stack(q_ref[rows, lanes[g]] * SB_SCALE, h0))
                o = o_ref[rows, lanes[g]]
                ro = lax.rsqrt(_head_sums(o * o, h0) * (1.0 / HEAD_LANES) + EPS)
                oh = o * ro
                dm = dm_ref[rows, lanes[g]].astype(F32)
                dhg_ref[:, lanes[g]] += jnp.sum(dm * oh, axis=0, keepdims=True)
                doh = dm * hg_ref[:, lanes[g]]
                do = ro * (doh - oh * (_head_sums(doh * oh, h0) * (1.0 / HEAD_LANES)))
                dosts.append(_sb_stack(do.astype(BF), h0))
                first = g * 128
                tots.append(jnp.concatenate(
                    [tot_ref[rows, first:first + 1], tot_ref[rows, first + HEAD_LANES:first + HEAD_LANES + 1]], axis=0))
            qsts_t = [q.T for q in qsts]
            dosts_t = [d.T for d in dosts]

            def block(g, j, masked, cm_in, cg_in):
                start = pl.multiple_of(j * tq, tq)
                kj = k_ref[pl.ds(start, tq), lanes[g]]
                vj = v_ref[pl.ds(start, tq), lanes[g]]
                n, l = _neg_log_sig(_dot_bt(qsts[g], kj))
                if masked:
                    n = jnp.where(causal, n, 0.0)
                a = jnp.exp(l - (tots[g] - cm_in - _running_sums(n, tri_le)))
                if masked:
                    a = jnp.where(causal, a, 0.0)
                gm = a * _dot_bt(dosts[g], vj)
                pp = cg_in + _dot(gm.astype(BF), tri_lt)
                dz = gm - jnp.exp(l) * (gm + pp)
                if masked:
                    dz = jnp.where(causal, dz, 0.0)
                dzb = dz.astype(BF)
                dk_acc[g, :, pl.ds(start, tq)] += _dot(qsts_t[g], dzb)
                dv_acc[g, :, pl.ds(start, tq)] += _dot(dosts_t[g], a.astype(BF))
                return (_dot(dzb, kj), cm_in + jnp.sum(n, axis=-1, keepdims=True),
                        cg_in + jnp.sum(gm, axis=-1, keepdims=True))

            def step(j, carry):
                for g in range(gp):
                    dq, cm[g], cg[g] = block(g, j, False, cm[g], cg[g])
                    dq_acc[g] += dq
                return carry

            walked = jnp.clip(nblk_ref[pl.program_id(0), pl.program_id(1), i].astype(jnp.int32),
                              jnp.minimum(i, 1), i)
            lax.fori_loop(i - walked, i - 1, step, 0)

            @pl.when(i == 0)
            def _():
                for g in range(gp):
                    dq_acc[g] = block(g, 0, True, cm[g], cg[g])[0]

            @pl.when(i > 0)
            def _():
                prev = [block(g, i - 1, False, cm[g], cg[g]) for g in range(gp)]
                diag = [block(g, i, True, prev[g][1], prev[g][2]) for g in range(gp)]
                for g in range(gp):
                    dq_acc[g] += prev[g][0] + diag[g][0]

            for g in range(gp):
                dq = jnp.where(h0, dq_acc[g, 0:tq, :], dq_acc[g, tq:2 * tq, :])
                dq_ref[rows, lanes[g]] = (dq * SB_SCALE).astype(BF)

        for u in range(per):
            query_block(pl.program_id(2) * per + u, slice(u * tq, (u + 1) * tq))

        @pl.when(pl.program_id(2) == ns - 1)
        def _():
            for g in range(gp):
                dk_ref[:, lanes[g]] = dk_acc[g].T.astype(BF)
                dv_ref[:, lanes[g]] = dv_acc[g].T.astype(BF)

    blk = lambda col0: pl.BlockSpec((per * tq, w), lambda b, hg_, i: (b * ns + i, col0 + hg_))
    seq = lambda col0: pl.BlockSpec((s, w), lambda b, hg_, i: (b, col0 + hg_))
    first = 1024 // w
    (dq, dk, dv, dhg), rode = _ride_call(
        body, "sb_bwd", (nb, ng, ns),
        in_specs=[blk(first), seq(first + ng), seq(first + 2 * ng), blk(0), blk(0),
                  pl.BlockSpec(memory_space=pltpu.SMEM), blk(ng),
                  pl.BlockSpec((1, w), lambda b, hg_, i: (0, ng + hg_)), ANY],
        out_specs=[blk(first), seq(0), seq(0), pl.BlockSpec((None, 1, w), lambda b, hg_, i: (b, 0, hg_))],
        out_shape=[jax.ShapeDtypeStruct((t, IN_COLS), BF), jax.ShapeDtypeStruct((t, 512), BF),
                   jax.ShapeDtypeStruct((t, 512), BF), jax.ShapeDtypeStruct((nb, 1, 512), F32)],
        scratch_shapes=[pltpu.VMEM((gp, 128, s), F32), pltpu.VMEM((gp, 128, s), F32),
                        pltpu.VMEM((gp, 2 * tq, 128), F32), pltpu.VMEM((gp, 2 * tq, 1), F32),
                        pltpu.VMEM((gp, 2 * tq, 1), F32)],
        operands=(proj, proj, proj, o_sb, tot, nblk.reshape(nb, ng, nq, 8, 128)[:, :, :, 0, 0], dmerged, hg, dproj),
        vmem_mb=48, ride=ride, aliases={8: 0})
    return dq, dk, dv, dhg, rode


def _place(buf, piece, col_block, name):
    t, w = piece.shape
    tm = min(t, 1024)

    def body(piece_ref, buf_ref, out_ref):
        del buf_ref
        out_ref[...] = piece_ref[...]

    return pl.pallas_call(
        body, name=name, grid=(t // tm,),
        in_specs=[pl.BlockSpec((tm, w), lambda i: (i, 0)), ANY],
        out_specs=pl.BlockSpec((tm, w), lambda i: (i, col_block)),
        out_shape=jax.ShapeDtypeStruct(buf.shape, buf.dtype), input_output_aliases={1: 0},
        compiler_params=_params(16, ("arbitrary",)),
    )(piece, buf)


def _softmax_rows(sc):
    e = jnp.exp(sc - jnp.max(sc, axis=-1, keepdims=True))
    return e / jnp.sum(e, axis=-1, keepdims=True)


def _mix_cross_fwd(x, merged, w_out, gc, w_cq, kv, w_co, s, tm):
    t, d = x.shape
    tm = min(tm, s)
    per = s // tm
    inv = 1.0 / math.sqrt(X_HEAD_DIM)

    def body(x_ref, m_ref, wo_ref, gc_ref, wq_ref, kv_ref, wc_ref, h1_ref, h2_ref, hn_ref, qc_ref, oc_ref):
        h1 = x_ref[...] + _dot(m_ref[...], wo_ref[...])
        h1_ref[...] = h1
        hn = (h1 * _rs(h1) * gc_ref[...]).astype(BF)
        hn_ref[...] = hn
        qc = _dot(hn, wq_ref[...]).astype(BF)
        qc_ref[...] = qc
        for h in range(X_HEADS):
            cols = slice(h * X_HEAD_DIM, (h + 1) * X_HEAD_DIM)
            kh = kv_ref[:, h * X_HEAD_DIM:(h + 1) * X_HEAD_DIM]
            vh = kv_ref[:, d + h * X_HEAD_DIM:d + (h + 1) * X_HEAD_DIM]
            p = _softmax_rows(_dot_bt(qc[:, cols], kh) * inv)
            oc_ref[:, cols] = _dot(p.astype(BF), vh).astype(BF)
        h2_ref[...] = h1 + _dot(oc_ref[...], wc_ref[...])

    row = lambda width: pl.BlockSpec((tm, width), lambda i: (i, 0))
    full = lambda a, b: pl.BlockSpec((a, b), lambda i: (0, 0))
    return pl.pallas_call(
        body, name="mix_cross_fwd", grid=(t // tm,),
        in_specs=[row(d), row(d), full(d, d), full(1, d), full(d, d),
                  pl.BlockSpec((N_MEM, 2 * d), lambda i: (i // per, 0)), full(d, d)],
        out_specs=[row(d), row(d), row(d), row(d), row(d)],
        out_shape=[jax.ShapeDtypeStruct((t, d), F32), jax.ShapeDtypeStruct((t, d), F32),
                   jax.ShapeDtypeStruct((t, d), BF), jax.ShapeDtypeStruct((t, d), BF),
                   jax.ShapeDtypeStruct((t, d), BF)],
        compiler_params=_params(48, ("arbitrary",)),
    )(*_in_hbm(x, merged, w_out, gc, w_cq, kv, w_co))


def _cross_bwd(dh2, h1, qc, gc, w_cq, kv, w_co, s, tm):
    t, d = dh2.shape
    tm = min(tm, s)
    per = s // tm
    nb = t // s
    inv = 1.0 / math.sqrt(X_HEAD_DIM)

    def body(dh2_ref, h1_ref, qc_ref, gc_ref, wq_ref, kv_ref, wc_ref, dh1_ref, dqc_ref, dkv_ref, dgc_ref):
        i = pl.program_id(0)

        @pl.when(i == 0)
        def _():
            dgc_ref[...] = jnp.zeros_like(dgc_ref)

        @pl.when(i % per == 0)
        def _():
            dkv_ref[...] = jnp.zeros_like(dkv_ref)

        dh2 = dh2_ref[...]
        h1 = h1_ref[...]
        r = _rs(h1)
        h1h = h1 * r
        gcv = gc_ref[...]
        qc = qc_ref[...]
        do = _dot_bt(dh2.astype(BF), wc_ref[...]).astype(BF)
        for h in range(X_HEADS):
            cols = slice(h * X_HEAD_DIM, (h + 1) * X_HEAD_DIM)
            vcols = slice(d + h * X_HEAD_DIM, d + (h + 1) * X_HEAD_DIM)
            kh = kv_ref[:, cols]
            vh = kv_ref[:, vcols]
            p = _softmax_rows(_dot_bt(qc[:, cols], kh) * inv)
            dp = _dot_bt(do[:, cols], vh)
            ds = (p * (dp - jnp.sum(dp * p, axis=-1, keepdims=True)) * inv).astype(BF)
            dqc_ref[:, cols] = _dot(ds, kh).astype(BF)
            dkv_ref[:, cols] += _dot_at(ds, qc[:, cols])
            dkv_ref[:, vcols] += _dot_at(p.astype(BF), do[:, cols])
        dhn = _dot_bt(dqc_ref[...], wq_ref[...])
        dx, dg = _rms_bwd(dhn, h1h, r, gcv)
        dh1_ref[...] = dh2 + dx
        dgc_ref[...] += jnp.sum(dg, axis=0, keepdims=True)

    row = lambda width: pl.BlockSpec((tm, width), lambda i: (i, 0))
    full = lambda a, b: pl.BlockSpec((a, b), lambda i: (0, 0))
    kvspec = pl.BlockSpec((N_MEM, 2 * d), lambda i: (i // per, 0))
    return pl.pallas_call(
        body, name="cross_bwd", grid=(t // tm,),
        in_specs=[row(d), row(d), row(d), full(1, d), full(d, d), kvspec, full(d, d)],
        out_specs=[row(d), row(d), kvspec, full(1, d)],
        out_shape=[jax.ShapeDtypeStruct((t, d), F32), jax.ShapeDtypeStruct((t, d), BF),
                   jax.ShapeDtypeStruct((nb * N_MEM, 2 * d), F32), jax.ShapeDtypeStruct((1, d), F32)],
        compiler_params=_params(48, ("arbitrary",)),
    )(*_in_hbm(dh2, h1, qc, gc, w_cq, kv, w_co))


def _mem_bwd(mem, gm, dkv, w_ckv, tm):
    t, d = mem.shape
    tm = min(tm, t)

    def body(mem_ref, dkv_ref, w_ref, dg_ref):
        @pl.when(pl.program_id(0) == 0)
        def _():
            dg_ref[...] = jnp.zeros_like(dg_ref)

        mv = mem_ref[...]
        dmn = _dot_bt(dkv_ref[...].astype(BF), w_ref[...])
        dg_ref[...] += jnp.sum(dmn * (mv * _rs(mv)), axis=0, keepdims=True)

    del gm
    return pl.pallas_call(
        body, name="mem_bwd", grid=(t // tm,),
        in_specs=[pl.BlockSpec((tm, d), lambda i: (i, 0)), pl.BlockSpec((tm, 2 * d), lambda i: (i, 0)),
                  pl.BlockSpec((d, 2 * d), lambda i: (0, 0))],
        out_specs=pl.BlockSpec((1, d), lambda i: (0, 0)),
        out_shape=jax.ShapeDtypeStruct((1, d), F32),
        compiler_params=_params(32, ("arbitrary",)),
    )(mem, dkv, w_ckv)


def _ffn_loss_fwd(h2, gf, w1, w2, gl, target, tm):
    t, d = h2.shape
    tm = min(tm, t)

    def body(h2_ref, gf_ref, w1_ref, w2_ref, gl_ref, tg_ref, hn_ref, f_ref, dh3_ref, dgl_ref, loss_ref):
        @pl.when(pl.program_id(0) == 0)
        def _():
            dgl_ref[...] = jnp.zeros_like(dgl_ref)
            loss_ref[...] = jnp.zeros_like(loss_ref)

        h2 = h2_ref[...]
        hn = (h2 * _rs(h2) * gf_ref[...]).astype(BF)
        hn_ref[...] = hn
        h3 = h2
        for c in range(4):
            f = jnp.maximum(_dot(hn, w1_ref[c]), 0.0)
            f_ref[:, c * 1024:(c + 1) * 1024] = f.astype(BF)
            h3 = h3 + _dot((f * f).astype(BF), w2_ref[c])
        r3 = _rs(h3)
        yh = h3 * r3
        glv = gl_ref[...]
        e = yh * glv - tg_ref[...]
        loss_ref[...] += 0.5 * jnp.sum(jnp.sum(e * e, axis=-1, keepdims=True) * (1.0 / d), axis=0, keepdims=True)
        dy = e * (1.0 / d)
        dx, dg = _rms_bwd(dy, yh, r3, glv)
        dh3_ref[...] = dx
        dgl_ref[...] += jnp.sum(dg, axis=0, keepdims=True)

    row = lambda width: pl.BlockSpec((tm, width), lambda i: (i, 0))
    return pl.pallas_call(
        body, name="ffn_loss_fwd", grid=(t // tm,),
        in_specs=[row(d), pl.BlockSpec((1, d), lambda i: (0, 0)), pl.BlockSpec((4, d, 1024), lambda i: (0, 0, 0), pipeline_mode=pl.Buffered(1)),
                  pl.BlockSpec((4, 1024, d), lambda i: (0, 0, 0), pipeline_mode=pl.Buffered(1)),
                  pl.BlockSpec((1, d), lambda i: (0, 0)), row(d)],
        out_specs=[row(d), row(D_FF), row(d), pl.BlockSpec((1, d), lambda i: (0, 0)),
                   pl.BlockSpec((1, 1), lambda i: (0, 0))],
        out_shape=[jax.ShapeDtypeStruct((t, d), BF), jax.ShapeDtypeStruct((t, D_FF), BF),
                   jax.ShapeDtypeStruct((t, d), F32), jax.ShapeDtypeStruct((1, d), F32),
                   jax.ShapeDtypeStruct((1, 1), F32)],
        compiler_params=_params(56, ("arbitrary",)),
    )(*_in_hbm(h2, gf, w1, w2, gl, target))


def _ffn_bwd(dh3, f, h2, gf, w1, w2, tm):
    t, d = h2.shape
    tm = min(tm, t)

    def body(dh3_ref, f_ref, h2_ref, gf_ref, w1_ref, w2_ref, dh2_ref, dpre_ref, dgf_ref):
        @pl.when(pl.program_id(0) == 0)
        def _():
            dgf_ref[...] = jnp.zeros_like(dgf_ref)

        dh3 = dh3_ref[...]
        dh3b = dh3.astype(BF)
        dhn = jnp.zeros((tm, d), F32)
        for c in range(4):
            cols = slice(c * 1024, (c + 1) * 1024)
            dpre = (_dot_bt(dh3b, w2_ref[c]) * (2.0 * f_ref[:, cols].astype(F32))).astype(BF)
            dpre_ref[:, cols] = dpre
            dhn = dhn + _dot_bt(dpre, w1_ref[c])
        h2 = h2_ref[...]
        r = _rs(h2)
        dx, dg = _rms_bwd(dhn, h2 * r, r, gf_ref[...])
        dh2_ref[...] = dh3 + dx
        dgf_ref[...] += jnp.sum(dg, axis=0, keepdims=True)

    row = lambda width: pl.BlockSpec((tm, width), lambda i: (i, 0))
    return pl.pallas_call(
        body, name="ffn_bwd", grid=(t // tm,),
        in_specs=[row(d), row(D_FF), row(d), pl.BlockSpec((1, d), lambda i: (0, 0)),
                  pl.BlockSpec((4, d, 1024), lambda i: (0, 0, 0), pipeline_mode=pl.Buffered(1)),
                  pl.BlockSpec((4, 1024, d), lambda i: (0, 0, 0), pipeline_mode=pl.Buffered(1))],
        out_specs=[row(d), row(D_FF), pl.BlockSpec((1, d), lambda i: (0, 0))],
        out_shape=[jax.ShapeDtypeStruct((t, d), F32), jax.ShapeDtypeStruct((t, D_FF), BF),
                   jax.ShapeDtypeStruct((1, d), F32)],
        compiler_params=_params(56, ("arbitrary",)),
    )(*_in_hbm(dh3, f, h2, gf, w1, w2))


def _in_bwd(dproj, dh1, x, g, w_in, tm, ride=None):
    t, d = x.shape
    n = w_in.shape[1]
    tm = min(tm, t)

    def body(dp_ref, dh1_ref, x_ref, g_ref, w_ref, dx_ref, dg_ref):
        @pl.when(pl.program_id(0) == 0)
        def _():
            dg_ref[...] = jnp.zeros_like(dg_ref)

        dxn = _dot_bt(dp_ref[...], w_ref[...])
        xv = x_ref[...]
        r = _rs(xv)
        dx, dg = _rms_bwd(dxn, xv * r, r, g_ref[...])
        dx_ref[...] = dh1_ref[...] + dx
        dg_ref[...] += jnp.sum(dg, axis=0, keepdims=True)

    row = lambda width: pl.BlockSpec((tm, width), lambda i: (i, 0))
    (dx, dg), rode = _ride_call(
        body, "in_bwd", (t // tm,),
        in_specs=[row(n), row(d), row(d), pl.BlockSpec((1, d), lambda i: (0, 0)),
                  pl.BlockSpec((d, n), lambda i: (0, 0))],
        out_specs=[row(d), pl.BlockSpec((1, d), lambda i: (0, 0))],
        out_shape=[jax.ShapeDtypeStruct((t, d), F32), jax.ShapeDtypeStruct((1, d), F32)],
        scratch_shapes=[], operands=(dproj, dh1, x, g, w_in), vmem_mb=48, ride=ride)
    return dx, dg, rode


class _GradReduce:
    def __init__(self, c_idx):
        self.c_idx = c_idx
        self.sums = {}

    def sibling(self, slabs):
        return _SiblingExchange(slabs)

    def chip(self, names, slabs, recv):
        for k, a, r in zip(names, slabs, recv):
            self.sums[k] = _chip_sum(a, r, self.c_idx, "chip_sum_" + k)
        return _ChipExchange([self.sums[k] for k in names])


def _full_weights(gathered):
    d = D_MODEL
    out = {}
    for k, a in gathered.items():
        if k in ("w_in", "w_ckv", "w_ff1"):
            out[k] = a.transpose(1, 0, 2).reshape(d, -1)
        else:
            out[k] = a.reshape(-1, d)
    return out


def _slabs(a):
    return a.reshape(N_DEV, -1, a.shape[-1])


def _local_step(x, mem, target, small, big, nb, s, tq=256, gather_rest=None, reduce=None):
    d = D_MODEL
    g_mix, g_v, w_sp, b_sp, g_head, g_cross, g_mem, g_ffn, g_fin = (
        small[k] for k in ("norm_mix_g", "gm_v_norm_g", "w_spatial", "b_spatial", "head_norm_g", "norm_cross_g",
                           "norm_mem_g", "norm_ffn_g", "norm_final_g"))
    tri = jnp.tril(jnp.ones((CHUNK, CHUNK), dtype=bool))
    w_sp_m = jnp.where(tri[None], w_sp, 0.0)
    wt = w_sp_m.astype(BF)
    wtt = jnp.swapaxes(w_sp_m, 1, 2).astype(BF)
    bb = jnp.broadcast_to(b_sp[:, :, None], (GM_GROUPS, CHUNK, CHUNK))
    hg_a = g_head[:, :GM_WIDTH]

    proj, xn = _norm_matmul(x, g_mix, big["w_in"], 1024, "in_proj")
    merged = _gmlp_fwd(proj, g_v, wt, bb, hg_a, 512)
    o_sb, tot, merged, nblk, gathered = _sb_fwd(proj, merged, g_head, nb, s, tq, ride=gather_rest)
    if gather_rest is not None:
        big = dict(big, **_full_weights(dict(zip(BIG[1:], gathered))))
    w1c = big["w_ff1"].reshape(d, 4, 1024).transpose(1, 0, 2)
    w2c = big["w_ff2"].reshape(4, 1024, d)
    kv, memn = _norm_matmul(mem, g_mem, big["w_ckv"], 512, "mem_proj")
    h1, h2, hn, qc, oc = _mix_cross_fwd(x, merged, big["w_out"], g_cross, big["w_cq"], kv, big["w_co"], s, 512)
    hn2, f, dh3, d_fin, loss = _ffn_loss_fwd(h2, g_ffn, w1c, w2c, g_fin, target, 512)

    gbig = {}
    dh2, dpre, d_ffn = _ffn_bwd(dh3, f, h2, g_ffn, w1c, w2c, 512)
    gbig["w_ff2"] = _slabs(_wgrad(f, dh3, 1024, 512, "wgrad_ff2", square_a=True, tm=2048))
    gbig["w_ff1"] = _slabs(_wgrad_wide(hn2, dpre, 1024, 512, "wgrad_ff1", col_shards=4))
    dh1, dqc, dkv, d_cross = _cross_bwd(dh2, h1, qc, g_cross, big["w_cq"], kv, big["w_co"], s, 512)
    gbig["w_co"] = _slabs(_wgrad(oc, dh2, 1024, 1024, "wgrad_co"))
    gbig["w_cq"] = _slabs(_wgrad(hn, dqc, 1024, 1024, "wgrad_cq"))
    gbig["w_ckv"] = _slabs(_wgrad(memn, dkv, 512, 1024, "wgrad_ckv", col_shards=4))
    d_mem = _mem_bwd(mem, g_mem, dkv, big["w_ckv"], 512)
    dmerged = _matmul_bt(dh1, big["w_out"], 1024, "out_bwd")
    gbig["w_out"] = _slabs(_wgrad(merged, dh1, 1024, 1024, "wgrad_out"))
    rest = BIG[1:]
    ride = reduce.sibling([gbig[k] for k in rest]) if reduce else None
    dproj, d_wsp, d_bb, d_gv, d_hga, recv = _gmlp_bwd(proj, dmerged, g_v, wt, wtt, bb, hg_a, 1024, ride=ride)
    ride = reduce.chip(rest, [gbig[k] for k in rest], recv) if reduce else None
    dproj, dk, dv, d_hgb, parts_rest = _sb_bwd(proj, o_sb, tot, nblk, dmerged, dproj, g_head, nb, s, tq, ride=ride)
    dproj = _place(_place(dproj, dk, 3, "place_dk"), dv, 4, "place_dv")
    gbig["w_in"] = _slabs(_wgrad_wide(xn, dproj, 512, 1024, "wgrad_in", col_shards=4))
    last = None
    if reduce:
        recv = _run_exchange(reduce.sibling([gbig["w_in"]]), "grad_sibling_exchange_w_in")
        last = reduce.chip(["w_in"], [gbig["w_in"]], recv)
    grad_x, d_mix, _ = _in_bwd(dproj, dh1, x, g_mix, big["w_in"], 512)
    parts = dict(zip(rest, parts_rest))

    gsmall = {
        "norm_mix_g": d_mix, "gm_v_norm_g": d_gv, "w_spatial": d_wsp, "b_spatial": d_bb[:, :, 0],
        "head_norm_g": jnp.concatenate([d_hga, jnp.sum(d_hgb, axis=0)], axis=1), "norm_cross_g": d_cross,
        "norm_mem_g": d_mem, "norm_ffn_g": d_ffn, "norm_final_g": d_fin,
    }
    return loss, grad_x, gsmall, gbig, parts, last


BIG = ("w_in", "w_out", "w_cq", "w_ckv", "w_co", "w_ff1", "w_ff2")
SMALL = ("norm_mix_g", "gm_v_norm_g", "w_spatial", "b_spatial", "head_norm_g", "norm_cross_g", "norm_mem_g",
         "norm_ffn_g", "norm_final_g")


def _local_copies_start(srcs, stages, sems):
    loads = [pltpu.make_async_copy(src, stage, sems.at[w]) for w, (src, stage) in enumerate(zip(srcs, stages))]
    for ld in loads:
        ld.start()
    return loads


def _local_copies_finish(loads, stages, dsts, sems):
    stores = []
    for w, (ld, stage, dst) in enumerate(zip(loads, stages, dsts)):
        ld.wait()
        st = pltpu.make_async_copy(stage, dst, sems.at[w])
        st.start()
        stores.append(st)
    for st in stores:
        st.wait()


def _chip_sum(slabs, recv, c_idx, name):
    _, r, cw = slabs.shape
    tr = min(r, 256)

    def body(c_ref, a_ref, b_ref, o_ref):
        del c_ref
        o_ref[...] = (a_ref[...] + b_ref[...]).astype(BF)

    return pl.pallas_call(
        body, name=name,
        grid_spec=pltpu.PrefetchScalarGridSpec(
            num_scalar_prefetch=1, grid=(N_CHIPS, r // tr),
            in_specs=[pl.BlockSpec((None, tr, cw), lambda p, i, c_ref: (2 * p + c_ref[0], i, 0)),
                      pl.BlockSpec((None, tr, cw), lambda p, i, c_ref: (p, i, 0))],
            out_specs=pl.BlockSpec((None, tr, cw), lambda p, i, c_ref: (p, i, 0))),
        out_shape=jax.ShapeDtypeStruct((N_CHIPS, r, cw), BF),
        compiler_params=_params(32, ("arbitrary", "arbitrary")),
    )(c_idx, *_in_hbm(slabs, recv))


def _sum4(sums, parts, q_idx, name):
    _, r, cw = parts.shape
    tr = min(r, 256)

    def body(q_ref, own_ref, a_ref, b_ref, c_ref, o_ref):
        del q_ref
        o_ref[...] = ((own_ref[...].astype(F32) + a_ref[...].astype(F32)) + b_ref[...].astype(F32)) + c_ref[
            ...].astype(F32)

    spec = lambda k: pl.BlockSpec((None, tr, cw), lambda i, q_ref: ((q_ref[0] + k) % N_CHIPS, i, 0))
    return pl.pallas_call(
        body, name=name,
        grid_spec=pltpu.PrefetchScalarGridSpec(
            num_scalar_prefetch=1, grid=(r // tr,), in_specs=[spec(0), spec(1), spec(2), spec(3)],
            out_specs=pl.BlockSpec((tr, cw), lambda i, q_ref: (i, 0))),
        out_shape=jax.ShapeDtypeStruct((r, cw), F32),
        compiler_params=_params(32, ("arbitrary",)),
    )(q_idx, *_in_hbm(sums, parts, parts, parts))


def _half_exchange(halves):
    n = len(halves)

    def body(*refs):
        ins, outs, stages = refs[:n], refs[n:2 * n], refs[2 * n:3 * n]
        send_sems, recv_sems, ld_sems, st_sems = refs[3 * n:]
        x, y, c = lax.axis_index("x"), lax.axis_index("y"), lax.axis_index("c")
        loads = _local_copies_start(ins, stages, ld_sems)
        copies = []
        for w in range(n):
            cp = pltpu.make_async_remote_copy(
                src_ref=ins[w], dst_ref=outs[w].at[c], send_sem=send_sems.at[w], recv_sem=recv_sems.at[w],
                device_id=(x, y, 1 - c), device_id_type=MESH)
            cp.start()
            copies.append(cp)
        _local_copies_finish(loads, stages, [outs[w].at[c] for w in range(n)], st_sems)
        for cp in copies:
            cp.wait()

    return pl.pallas_call(
        body, name="grad_half_exchange",
        in_specs=[ANY] * n, out_specs=[ANY] * n,
        out_shape=[jax.ShapeDtypeStruct((2,) + a.shape, a.dtype) for a in halves],
        scratch_shapes=[pltpu.VMEM(a.shape, a.dtype) for a in halves] + [
            pltpu.SemaphoreType.DMA((n,)), pltpu.SemaphoreType.DMA((n,)),
            pltpu.SemaphoreType.DMA((n,)), pltpu.SemaphoreType.DMA((n,))],
        compiler_params=_params(24),
    )(*halves)


def _small_all_reduce(packed, ride=None):
    rows = packed.shape[0]
    ride = ride or _NoExchange()
    ri, ro = len(ride.in_arrays), len(ride.out_shape)

    def body(*refs):
        in_ref, rins, out_ref, routs = refs[0], refs[1:1 + ri], refs[1 + ri], refs[2 + ri:2 + ri + ro]
        pair, chip_sum, chips, d2d_send, d2d_recv, ici_send, ici_recv = refs[2 + ri + ro:9 + ri + ro]
        rscr = refs[9 + ri + ro:]
        ride.start(rins, routs, rscr)
        x, y, c = lax.axis_index("x"), lax.axis_index("y"), lax.axis_index("c")
        q = 2 * x + y
        pair[c] = in_ref[...]
        swap = pltpu.make_async_remote_copy(
            src_ref=in_ref, dst_ref=pair.at[c], send_sem=d2d_send, recv_sem=d2d_recv,
            device_id=(x, y, 1 - c), device_id_type=MESH)
        swap.start()
        swap.wait()
        both = pair[0] + pair[1]
        chip_sum[...] = both
        chips[q] = both
        copies = [pltpu.make_async_remote_copy(
            src_ref=chip_sum, dst_ref=chips.at[q], send_sem=ici_send.at[k], recv_sem=ici_recv.at[k],
            device_id=(px, py, c), device_id_type=MESH) for k, (px, py) in enumerate(_other_chips(x, y))]
        for cp in copies:
            cp.start()
        for cp in copies:
            cp.wait()
        out_ref[...] = ((chips[0] + chips[1]) + chips[2]) + chips[3]
        ride.finish(rins, routs, rscr)

    vmem = pl.BlockSpec(memory_space=pltpu.VMEM)
    res = pl.pallas_call(
        body, name="small_all_reduce",
        in_specs=[vmem] + [ANY] * ri, out_specs=[vmem] + [ANY] * ro,
        out_shape=[jax.ShapeDtypeStruct(packed.shape, F32)] + list(ride.out_shape),
        scratch_shapes=[pltpu.VMEM((2, rows, 128), F32), pltpu.VMEM((rows, 128), F32),
                        pltpu.VMEM((N_CHIPS, rows, 128), F32), pltpu.SemaphoreType.DMA, pltpu.SemaphoreType.DMA,
                        pltpu.SemaphoreType.DMA((3,)), pltpu.SemaphoreType.DMA((3,))] + list(ride.scratch_shapes),
        compiler_params=_params(16),
    )(packed, *ride.in_arrays)
    return res[0], res[1:]


def _adamw(g, w, m, v, name):
    r, cw = g.shape
    tr = 256 if r % 256 == 0 else r

    def body(g_ref, w_ref, m_ref, v_ref, d_ref, nm_ref, nv_ref):
        gv = g_ref[...]
        nm = ADAM_B1 * m_ref[...] + (1.0 - ADAM_B1) * gv
        nv = ADAM_B2 * v_ref[...] + (1.0 - ADAM_B2) * (gv * gv)
        m_hat = nm / (1.0 - ADAM_B1 ** ADAM_STEP)
        v_hat = nv / (1.0 - ADAM_B2 ** ADAM_STEP)
        d_ref[...] = -ADAM_LR * (m_hat / (jnp.sqrt(v_hat) + ADAM_EPS) + ADAM_WD * w_ref[...])
        nm_ref[...] = nm
        nv_ref[...] = nv

    spec = pl.BlockSpec((tr, cw), lambda i: (i, 0))
    return pl.pallas_call(
        body, name=name, grid=(r // tr,),
        in_specs=[spec] * 4, out_specs=[spec] * 3,
        out_shape=[jax.ShapeDtypeStruct((r, cw), F32)] * 3,
        compiler_params=_params(32, ("arbitrary",)),
    )(*_in_hbm(g, w, m, v))


def _small_params(args):
    small = {k: args[k].reshape(1, -1) for k in SMALL}
    small["w_spatial"] = args["w_spatial"][0]
    small["b_spatial"] = args["b_spatial"][0]
    return small


def _pack(parts, rows):
    flat = jnp.concatenate([p.reshape(-1).astype(F32) for p in parts])
    return jnp.pad(flat, (0, rows * 128 - flat.shape[0])).reshape(rows, 128)


def _unpack(packed, shapes):
    flat = packed.reshape(-1)
    out, off = [], 0
    for shp in shapes:
        size = math.prod(shp)
        out.append(flat[off:off + size].reshape(shp))
        off += size
    return out


def kernel(x, mem, norm_mix_g, w_in, gm_v_norm_g, w_spatial, b_spatial, head_norm_g, w_out, norm_cross_g, norm_mem_g, w_cq, w_ckv, w_co, norm_ffn_g, w_ff1, w_ff2, norm_final_g, loss_target, m_norm_mix_g, m_w_in, m_gm_v_norm_g, m_w_spatial, m_b_spatial, m_head_norm_g, m_w_out, m_norm_cross_g, m_norm_mem_g, m_w_cq, m_w_ckv, m_w_co, m_norm_ffn_g, m_w_ff1, m_w_ff2, m_norm_final_g, v_norm_mix_g, v_w_in, v_gm_v_norm_g, v_w_spatial, v_b_spatial, v_head_norm_g, v_w_out, v_norm_cross_g, v_norm_mem_g, v_w_cq, v_w_ckv, v_w_co, v_norm_ffn_g, v_w_ff1, v_w_ff2, v_norm_final_g):
    args = dict(locals())
    d = D_MODEL
    nb, s, _ = x.shape
    c_idx = lax.axis_index("c").astype(jnp.int32).reshape(1)
    q_idx = (2 * lax.axis_index("x") + lax.axis_index("y")).astype(jnp.int32).reshape(1)
    rest = BIG[1:]

    shards = {k: args[k][0].astype(BF) for k in BIG}
    big = _full_weights({"w_in": _run_exchange(_GatherExchange([shards["w_in"]]), "all_gather_w_in")[0]})
    gather_rest = _GatherExchange([shards[k] for k in rest])

    reduce = _GradReduce(c_idx)
    loss, grad_x, gsmall, _, parts, last = _local_step(
        x.reshape(nb * s, d), mem.reshape(nb * N_MEM, d), loss_target.reshape(nb * s, d), _small_params(args), big,
        nb, s, gather_rest=gather_rest, reduce=reduce)

    shapes = [args[k].shape for k in SMALL]
    n_small = sum(math.prod(sh) for sh in shapes)
    rows = -(-(n_small + 1) // 1024) * 8
    reduced, (parts["w_in"],) = _small_all_reduce(_pack([gsmall[k] for k in SMALL] + [loss], rows), ride=last)
    halves = [_sum4(reduce.sums[k], parts[k], q_idx, "sum4_" + k) for k in BIG]
    both = _half_exchange(halves)

    out = {"grad_x": grad_x.reshape(nb, s, d)}
    for k, g2 in zip(BIG, both):
        shp = args[k].shape
        g = g2.reshape(shp[1], shp[2])
        dl, nm, nv = _adamw(g, args[k][0], args["m_" + k][0], args["v_" + k][0], "adamw_" + k)
        out["grad_" + k], out["delta_" + k], out["new_m_" + k], out["new_v_" + k] = (
            a.reshape(shp) for a in (g, dl, nm, nv))

    dl, nm, nv = _adamw(reduced, _pack([args[k] for k in SMALL], rows), _pack([args["m_" + k] for k in SMALL], rows),
                        _pack([args["v_" + k] for k in SMALL], rows), "adamw_small")
    for name, arr in (("grad_", reduced), ("delta_", dl), ("new_m_", nm), ("new_v_", nv)):
        for k, a in zip(SMALL, _unpack(arr, shapes)):
            out[name + k] = a
    out["loss"] = reduced.reshape(-1)[n_small]

    names = ["norm_mix_g", "w_in", "gm_v_norm_g", "w_spatial", "b_spatial", "head_norm_g", "w_out", "norm_cross_g",
             "norm_mem_g", "w_cq", "w_ckv", "w_co", "norm_ffn_g", "w_ff1", "w_ff2", "norm_final_g"]
    return (out["loss"], out["grad_x"], *[out["grad_" + k] for k in names], *[out["delta_" + k] for k in names],
            *[out["new_m_" + k] for k in names], *[out["new_v_" + k] for k in names])
```

```python
import functools
import math

import jax
import jax.numpy as jnp
from jax import lax
from jax.experimental import pallas as pl
from jax.experimental.pallas import tpu as pltpu

F32 = jnp.float32
BF = jnp.bfloat16

EPS = 1e-6
D_MODEL = 1024
CHUNK = 128
GM_GROUPS = 4
GM_WIDTH = 512
SB_WIDTH = 512
HEAD_LANES = 64
SB_SCALE = 0.125
SB_SKIP = -104.0
X_HEADS = 4
X_HEAD_DIM = 256
N_MEM = 256
D_FF = 4096
IN_COLS = 2560
N_CHIPS = 4
N_DEV = 8

ADAM_LR = 0.001
ADAM_B1 = 0.9
ADAM_B2 = 0.999
ADAM_EPS = 1e-08
ADAM_WD = 0.01
ADAM_STEP = 10

V7X_VMEM_BYTES = 64 * 1024 * 1024
MESH = pl.DeviceIdType.MESH
ANY = pl.BlockSpec(memory_space=pl.ANY)

GELU_C = math.sqrt(2.0 / math.pi)
GELU_A = 0.044715


def _params(vmem_mb, sem=None):
    assert vmem_mb * 1024 * 1024 <= V7X_VMEM_BYTES
    return pltpu.CompilerParams(vmem_limit_bytes=vmem_mb * 1024 * 1024, dimension_semantics=sem)


PIN_MIN_ELEMENTS = 1 << 18


def _in_hbm(*arrays):
    return tuple(pltpu.with_memory_space_constraint(a, pltpu.HBM) if a.size >= PIN_MIN_ELEMENTS else a
                 for a in arrays)


def _dot(a, b):
    return jnp.dot(a, b, preferred_element_type=F32)


def _dot_bt(a, b):
    return lax.dot_general(a, b, (((1,), (1,)), ((), ())), preferred_element_type=F32)


def _dot_at(a, b):
    return lax.dot_general(a, b, (((0,), (0,)), ((), ())), preferred_element_type=F32)


def _gelu(x):
    t = jnp.tanh(GELU_C * (x + GELU_A * x * x * x))
    return 0.5 * x * (1.0 + t)


def _gelu_and_grad(x):
    x2 = x * x
    t = jnp.tanh(GELU_C * (x + GELU_A * x2 * x))
    h = 0.5 * (1.0 + t)
    return x * h, h + 0.5 * x * (1.0 - t * t) * (GELU_C * (1.0 + 3.0 * GELU_A * x2))


def _rs(x):
    return lax.rsqrt(jnp.mean(x * x, axis=-1, keepdims=True) + EPS)


def _rms_bwd(dxn, xhat, r, g):
    dxh = dxn * g
    dx = r * (dxh - xhat * jnp.mean(dxh * xhat, axis=-1, keepdims=True))
    return dx, dxn * xhat


def _norm_matmul(x, g, w, tm, name):
    t, d = x.shape
    n = w.shape[1]
    tm = min(tm, t)

    def body(x_ref, g_ref, w_ref, out_ref, xn_ref):
        xv = x_ref[...]
        xn = (xv * _rs(xv) * g_ref[...]).astype(BF)
        xn_ref[...] = xn
        out_ref[...] = _dot(xn, w_ref[...]).astype(out_ref.dtype)

    return pl.pallas_call(
        body, name=name, grid=(t // tm,),
        in_specs=[pl.BlockSpec((tm, d), lambda i: (i, 0)), pl.BlockSpec((1, d), lambda i: (0, 0)),
                  pl.BlockSpec((d, n), lambda i: (0, 0))],
        out_specs=[pl.BlockSpec((tm, n), lambda i: (i, 0)), pl.BlockSpec((tm, d), lambda i: (i, 0))],
        out_shape=[jax.ShapeDtypeStruct((t, n), BF), jax.ShapeDtypeStruct((t, d), BF)],
        compiler_params=_params(48, ("arbitrary",)),
    )(*_in_hbm(x, g, w))


def _wgrad(a, g, tn, tk, name, square_a=False, col_shards=1, tm=1024):
    t, m = a.shape
    n = g.shape[1]
    tk = min(tk, t)
    tm = min(m, tm)
    ns = n // col_shards
    assert ns % tn == 0 and m % tm == 0
    per = ns // tn
    nk = t // tk

    def body(a_ref, g_ref, o_ref):
        k = pl.program_id(2)

        @pl.when(k == 0)
        def _():
            o_ref[...] = jnp.zeros_like(o_ref)

        av = a_ref[...]
        if square_a:
            af = av.astype(F32)
            av = af * af
        o_ref[...] += _dot_at(av.astype(BF), g_ref[...].astype(BF))

    return pl.pallas_call(
        body, name=name, grid=(m // tm, n // tn, nk),
        in_specs=[pl.BlockSpec((tk, tm), lambda i, j, k: (k, i)), pl.BlockSpec((tk, tn), lambda i, j, k: (k, j))],
        out_specs=pl.BlockSpec((None, tm, tn), lambda i, j, k: (j // per, i, j % per)),
        out_shape=jax.ShapeDtypeStruct((col_shards, m, ns), F32),
        compiler_params=_params(48, ("arbitrary", "arbitrary", "arbitrary")),
    )(*_in_hbm(a, g))


def _wgrad_wide(a, g, tm, tk, name, col_shards):
    t, m = a.shape
    n = g.shape[1]
    tk = min(tk, t)
    tm = min(tm, m)
    ns = n // col_shards

    def body(a_ref, g_ref, o_ref):
        @pl.when(pl.program_id(1) == 0)
        def _():
            o_ref[...] = jnp.zeros_like(o_ref)

        a_t = a_ref[...].astype(BF).T
        for p in range(col_shards):
            o_ref[p] += _dot(a_t, g_ref[:, p * ns:(p + 1) * ns].astype(BF))

    return pl.pallas_call(
        body, name=name, grid=(m // tm, t // tk),
        in_specs=[pl.BlockSpec((tk, tm), lambda i, k: (k, i)), pl.BlockSpec((tk, n), lambda i, k: (k, 0))],
        out_specs=pl.BlockSpec((col_shards, tm, ns), lambda i, k: (0, i, 0)),
        out_shape=jax.ShapeDtypeStruct((col_shards, m, ns), F32),
        compiler_params=_params(48, ("arbitrary", "arbitrary")),
    )(*_in_hbm(a, g))


def _matmul_bt(a, w, tm, name):
    t, n = a.shape
    k = w.shape[0]
    tm = min(tm, t)

    def body(a_ref, w_ref, o_ref):
        o_ref[...] = _dot_bt(a_ref[...].astype(BF), w_ref[...]).astype(o_ref.dtype)

    return pl.pallas_call(
        body, name=name, grid=(t // tm,),
        in_specs=[pl.BlockSpec((tm, n), lambda i: (i, 0)), pl.BlockSpec((k, n), lambda i: (0, 0))],
        out_specs=pl.BlockSpec((tm, k), lambda i: (i, 0)),
        out_shape=jax.ShapeDtypeStruct((t, k), BF),
        compiler_params=_params(32, ("arbitrary",)),
    )(*_in_hbm(a, w))


def _gmlp_fwd(proj, gg, wt, bb, hg, tm):
    t = proj.shape[0]
    tm = min(tm, t)

    def body(u_ref, v_ref, gg_ref, wt_ref, bb_ref, hg_ref, out_ref):
        for cc in range(tm // CHUNK):
            rows = slice(cc * CHUNK, (cc + 1) * CHUNK)
            for g in range(GM_GROUPS):
                cols = slice(g * 128, (g + 1) * 128)
                u = _gelu(u_ref[rows, cols].astype(F32))
                gv = _gelu(v_ref[rows, cols].astype(F32))
                vn = gv * _rs(gv) * gg_ref[:, cols]
                mixed = _dot(wt_ref[g], vn.astype(BF)) + bb_ref[g]
                a = u * mixed
                out_ref[rows, cols] = (a * _rs(a) * hg_ref[:, cols]).astype(BF)

    return pl.pallas_call(
        body, name="gmlp_fwd", grid=(t // tm,),
        in_specs=[pl.BlockSpec((tm, 512), lambda i: (i, 0)), pl.BlockSpec((tm, 512), lambda i: (i, 1)),
                  pl.BlockSpec((1, 512), lambda i: (0, 0)), pl.BlockSpec((4, 128, 128), lambda i: (0, 0, 0)),
                  pl.BlockSpec((4, 128, 128), lambda i: (0, 0, 0)), pl.BlockSpec((1, 512), lambda i: (0, 0))],
        out_specs=pl.BlockSpec((tm, 512), lambda i: (i, 0)),
        out_shape=jax.ShapeDtypeStruct((t, 1024), BF),
        compiler_params=_params(32, ("arbitrary",)),
    )(*_in_hbm(proj, proj, gg, wt, bb, hg))


def _gmlp_bwd(proj, dmerged, gg, wt, wtt, bb, hg, tm, ride=None):
    t = proj.shape[0]
    tm = min(tm, t)
    nsteps = t // tm

    def body(u_ref, v_ref, dm_ref, gg_ref, wt_ref, wtt_ref, bb_ref, hg_ref,
             dp_ref, dw_ref, db_ref, dgg_ref, dhg_ref):
        i = pl.program_id(0)

        @pl.when(i == 0)
        def _():
            dw_ref[...] = jnp.zeros_like(dw_ref)
            db_ref[...] = jnp.zeros_like(db_ref)
            dgg_ref[...] = jnp.zeros_like(dgg_ref)
            dhg_ref[...] = jnp.zeros_like(dhg_ref)

        for cc in range(tm // CHUNK):
            rows = slice(cc * CHUNK, (cc + 1) * CHUNK)
            for g in range(GM_GROUPS):
                cols = slice(g * 128, (g + 1) * 128)
                up = u_ref[rows, cols].astype(F32)
                gp = v_ref[rows, cols].astype(F32)
                u, u_grad = _gelu_and_grad(up)
                gv, gv_grad = _gelu_and_grad(gp)
                rv = _rs(gv)
                gvh = gv * rv
                ggv = gg_ref[:, cols]
                vnb = (gvh * ggv).astype(BF)
                mixed = _dot(wt_ref[g], vnb) + bb_ref[g]
                a = u * mixed
                ra = _rs(a)
                ah = a * ra
                dm = dm_ref[rows, cols].astype(F32)
                dhg_ref[:, cols] += jnp.sum(dm * ah, axis=0, keepdims=True)
                dah = dm * hg_ref[:, cols]
                da = ra * (dah - ah * jnp.mean(dah * ah, axis=-1, keepdims=True))
                du = da * mixed
                dmix = da * u
                db_ref[g] += dmix
                dmb = dmix.astype(BF)
                dw_ref[g] += _dot_bt(dmb, vnb)
                dvn = _dot(wtt_ref[g], dmb)
                dgg_ref[:, cols] += jnp.sum(dvn * gvh, axis=0, keepdims=True)
                dgh = dvn * ggv
                dgv = rv * (dgh - gvh * jnp.mean(dgh * gvh, axis=-1, keepdims=True))
                dp_ref[rows, cols] = (du * u_grad).astype(BF)
                dp_ref[rows, 512 + g * 128:512 + (g + 1) * 128] = (dgv * gv_grad).astype(BF)

        @pl.when(i == nsteps - 1)
        def _():
            r = lax.broadcasted_iota(jnp.int32, (CHUNK, CHUNK), 0)
            c = lax.broadcasted_iota(jnp.int32, (CHUNK, CHUNK), 1)
            for g in range(GM_GROUPS):
                dw_ref[g] = jnp.where(c <= r, dw_ref[g], 0.0)
                db_ref[g] = jnp.broadcast_to(jnp.sum(db_ref[g], axis=-1, keepdims=True), (CHUNK, CHUNK))

    small = lambda shape: pl.BlockSpec(shape, lambda i: (0,) * len(shape))
    res, rode = _ride_call(
        body, "gmlp_bwd", (nsteps,),
        in_specs=[pl.BlockSpec((tm, 512), lambda i: (i, 0)), pl.BlockSpec((tm, 512), lambda i: (i, 1)),
                  pl.BlockSpec((tm, 512), lambda i: (i, 0)), small((1, 512)), small((4, 128, 128)),
                  small((4, 128, 128)), small((4, 128, 128)), small((1, 512))],
        out_specs=[pl.BlockSpec((tm, 1024), lambda i: (i, 0)), small((4, 128, 128)), small((4, 128, 128)),
                   small((1, 512)), small((1, 512))],
        out_shape=[jax.ShapeDtypeStruct((t, IN_COLS), BF), jax.ShapeDtypeStruct((4, 128, 128), F32),
                   jax.ShapeDtypeStruct((4, 128, 128), F32), jax.ShapeDtypeStruct((1, 512), F32),
                   jax.ShapeDtypeStruct((1, 512), F32)],
        scratch_shapes=[], operands=(proj, proj, dmerged, gg, wt, wtt, bb, hg), vmem_mb=32, ride=ride)
    return (*res, rode)


def _other_chips(x, y):
    return ((1 - x, y), (x, 1 - y), (1 - x, 1 - y))


class _GatherExchange:
    def __init__(self, shards):
        n = len(shards)
        self.n = n
        self.in_arrays = list(shards)
        self.out_shape = [jax.ShapeDtypeStruct((N_CHIPS,) + a.shape, a.dtype) for a in shards]
        self.half_rows = [a.shape[0] // 2 for a in shards]
        sems = lambda k: pltpu.SemaphoreType.DMA((k,))
        self.scratch_shapes = [pltpu.VMEM(a.shape, a.dtype) for a in shards] + [
            sems(3 * n), sems(3 * n), sems(3 * n), sems(3 * n), sems(n), sems(n)]

    def _copies(self, ins, outs, scr):
        n = self.n
        stages, (ici_send, ici_recv, d2d_send, d2d_recv, ld_sems, st_sems) = scr[:n], scr[n:]
        x, y, c = lax.axis_index("x"), lax.axis_index("y"), lax.axis_index("c")
        q = 2 * x + y
        loads = [pltpu.make_async_copy(ins[w], stages[w], ld_sems.at[w]) for w in range(n)]
        stores = [pltpu.make_async_copy(stages[w], outs[w].at[q], st_sems.at[w]) for w in range(n)]
        ici, d2d = [], []
        for w in range(n):
            half = pl.ds(c * self.half_rows[w], self.half_rows[w])
            for k, (px, py) in enumerate(_other_chips(x, y)):
                ici.append(pltpu.make_async_remote_copy(
                    src_ref=ins[w].at[half], dst_ref=outs[w].at[q, half], send_sem=ici_send.at[3 * w + k],
                    recv_sem=ici_recv.at[3 * w + k], device_id=(px, py, c), device_id_type=MESH))
                landed = outs[w].at[2 * px + py, half]
                d2d.append(pltpu.make_async_remote_copy(
                    src_ref=landed, dst_ref=landed, send_sem=d2d_send.at[3 * w + k],
                    recv_sem=d2d_recv.at[3 * w + k], device_id=(x, y, 1 - c), device_id_type=MESH))
        return loads, stores, ici, d2d

    def start(self, ins, outs, scr):
        loads, stores, ici, _ = self._copies(ins, outs, scr)
        for cp in loads + ici:
            cp.start()
        for ld, st in zip(loads, stores):
            ld.wait()
            st.start()

    def relay(self, ins, outs, scr):
        _, _, ici, d2d = self._copies(ins, outs, scr)
        for got, fwd in zip(ici, d2d):
            got.wait_recv()
            fwd.start()

    def finish(self, ins, outs, scr):
        _, stores, ici, d2d = self._copies(ins, outs, scr)
        for cp in ici:
            cp.wait_send()
        for cp in d2d + stores:
            cp.wait()


class _SiblingExchange:
    def __init__(self, slabs):
        n = len(slabs)
        self.n = n
        self.in_arrays = list(slabs)
        self.out_shape = [jax.ShapeDtypeStruct((N_CHIPS,) + a.shape[1:], a.dtype) for a in slabs]
        self.scratch_shapes = [pltpu.SemaphoreType.DMA((4 * n,)), pltpu.SemaphoreType.DMA((4 * n,))]

    def _copies(self, ins, outs, scr):
        send_sems, recv_sems = scr
        x, y, c = lax.axis_index("x"), lax.axis_index("y"), lax.axis_index("c")
        return [pltpu.make_async_remote_copy(
            src_ref=ins[w].at[2 * p + (1 - c)], dst_ref=outs[w].at[p], send_sem=send_sems.at[4 * w + p],
            recv_sem=recv_sems.at[4 * w + p], device_id=(x, y, 1 - c), device_id_type=MESH)
            for w in range(self.n) for p in range(N_CHIPS)]

    def start(self, ins, outs, scr):
        for cp in self._copies(ins, outs, scr):
            cp.start()

    def finish(self, ins, outs, scr):
        for cp in self._copies(ins, outs, scr):
            cp.wait()


class _ChipExchange:
    def __init__(self, sums):
        n = len(sums)
        self.n = n
        self.in_arrays = list(sums)
        self.out_shape = [jax.ShapeDtypeStruct(a.shape, a.dtype) for a in sums]
        self.scratch_shapes = [pltpu.SemaphoreType.DMA((3 * n,)), pltpu.SemaphoreType.DMA((3 * n,))]

    def _copies(self, ins, outs, scr):
        send_sems, recv_sems = scr
        x, y, c = lax.axis_index("x"), lax.axis_index("y"), lax.axis_index("c")
        q = 2 * x + y
        return [pltpu.make_async_remote_copy(
            src_ref=ins[w].at[2 * px + py], dst_ref=outs[w].at[q], send_sem=send_sems.at[3 * w + k],
            recv_sem=recv_sems.at[3 * w + k], device_id=(px, py, c), device_id_type=MESH)
            for w in range(self.n) for k, (px, py) in enumerate(_other_chips(x, y))]

    def start(self, ins, outs, scr):
        for cp in self._copies(ins, outs, scr):
            cp.start()

    def finish(self, ins, outs, scr):
        for cp in self._copies(ins, outs, scr):
            cp.wait()


class _NoExchange:
    in_arrays, out_shape, scratch_shapes = (), (), ()

    def start(self, ins, outs, scr):
        pass

    def finish(self, ins, outs, scr):
        pass


def _run_exchange(ex, name):
    n_in, n_out = len(ex.in_arrays), len(ex.out_shape)

    def body(*refs):
        ins, outs, scr = refs[:n_in], refs[n_in:n_in + n_out], refs[n_in + n_out:]
        ex.start(ins, outs, scr)
        if hasattr(ex, "relay"):
            ex.relay(ins, outs, scr)
        ex.finish(ins, outs, scr)

    return pl.pallas_call(
        body, name=name, in_specs=[ANY] * n_in, out_specs=[ANY] * n_out, out_shape=ex.out_shape,
        scratch_shapes=ex.scratch_shapes, compiler_params=_params(24),
    )(*ex.in_arrays)


def _ride_call(body, name, grid, in_specs, out_specs, out_shape, scratch_shapes, operands, vmem_mb, ride=None,
               aliases=None):
    ride = ride or _NoExchange()
    ni, no, ns = len(in_specs), len(out_specs), len(scratch_shapes)
    ri, ro = len(ride.in_arrays), len(ride.out_shape)
    total = math.prod(grid)

    def wrapped(*refs):
        ins, rins = refs[:ni], refs[ni:ni + ri]
        outs, routs = refs[ni + ri:ni + ri + no], refs[ni + ri + no:ni + ri + no + ro]
        scr, rscr = refs[ni + ri + no + ro:ni + ri + no + ro + ns], refs[ni + ri + no + ro + ns:]
        step = pl.program_id(0)
        for ax in range(1, len(grid)):
            step = step * grid[ax] + pl.program_id(ax)

        @pl.when(step == 0)
        def _():
            ride.start(rins, routs, rscr)

        if hasattr(ride, "relay"):
            @pl.when(step == (3 * total) // 4)
            def _():
                ride.relay(rins, routs, rscr)

        body(*ins, *outs, *scr)

        @pl.when(step == total - 1)
        def _():
            ride.finish(rins, routs, rscr)

    res = pl.pallas_call(
        wrapped, name=name, grid=grid, in_specs=list(in_specs) + [ANY] * ri, out_specs=list(out_specs) + [ANY] * ro,
        out_shape=list(out_shape) + list(ride.out_shape),
        scratch_shapes=list(scratch_shapes) + list(ride.scratch_shapes), input_output_aliases=aliases or {},
        compiler_params=_params(vmem_mb, ("arbitrary",) * len(grid)),
    )(*_in_hbm(*operands), *ride.in_arrays)
    return res[:no], res[no:]


def _neg_log_sig(z):
    n = jnp.maximum(z, 0.0) + jnp.log(1.0 + jnp.exp(-jnp.abs(z)))
    return n, z - n


def _running_sums(n, tri2):
    hi = n.astype(BF)
    lo = (n - hi.astype(F32)).astype(BF)
    return _dot(jnp.concatenate([hi, lo], axis=1), tri2)


def _head_sums(x, h0):
    s0 = jnp.sum(jnp.where(h0, x, 0.0), axis=-1, keepdims=True)
    s1 = jnp.sum(jnp.where(h0, 0.0, x), axis=-1, keepdims=True)
    return jnp.where(h0, s0, s1)


SB_BLOCKS_PER_STEP = 4
SB_PAIRS_PER_STEP = 2


def _sb_masks(tq):
    h0 = lax.broadcasted_iota(jnp.int32, (tq, 128), 1) < HEAD_LANES
    r = lax.broadcasted_iota(jnp.int32, (2 * tq, tq), 0)
    c = lax.broadcasted_iota(jnp.int32, (2 * tq, tq), 1)
    return h0, c < jnp.where(r >= tq, r - tq, r)


def _sb_stack(x, h0):
    zero = jnp.zeros_like(x)
    return jnp.concatenate([jnp.where(h0, x, zero), jnp.where(h0, zero, x)], axis=0)


def _tri(tq, op):
    return op(lax.broadcasted_iota(jnp.int32, (tq, tq), 0), lax.broadcasted_iota(jnp.int32, (tq, tq), 1)).astype(BF)


def _sb_fwd(proj, merged, hg, nb, s, tq, ride=None):
    t = nb * s
    tq = min(tq, s)
    nq = s // tq
    per = min(SB_BLOCKS_PER_STEP, nq)
    ns = nq // per
    gp, ng, w = SB_PAIRS_PER_STEP, 4 // SB_PAIRS_PER_STEP, 128 * SB_PAIRS_PER_STEP

    def body(q_ref, k_ref, v_ref, hg_ref, merged_ref, o_ref, tot_ref, mb_ref, nblk_ref, acc, cr, c_min):
        del merged_ref
        h0, causal = _sb_masks(tq)
        tri_gt = _tri(tq, lambda r, c: r > c)
        tri_gt = jnp.concatenate([tri_gt, tri_gt], axis=0)
        lanes = [slice(g * 128, (g + 1) * 128) for g in range(gp)]
        zeros = jnp.zeros((2 * tq, 1), F32)

        def query_block(i, rows):
            qsts = [_sb_stack(q_ref[rows, lanes[g]] * SB_SCALE, h0) for g in range(gp)]

            def block(g, j, masked, c_in):
                start = pl.multiple_of(j * tq, tq)
                kj = k_ref[pl.ds(start, tq), lanes[g]]
                vj = v_ref[pl.ds(start, tq), lanes[g]]
                n, l = _neg_log_sig(_dot_bt(qsts[g], kj))
                if masked:
                    n = jnp.where(causal, n, 0.0)
                a = jnp.exp(l - (_running_sums(n, tri_gt) + c_in))
                if masked:
                    a = jnp.where(causal, a, 0.0)
                return _dot(a.astype(BF), vj), c_in + jnp.sum(n, axis=-1, keepdims=True)

            def keep(parts):
                for g, (p, c) in enumerate(parts):
                    acc[g] = p
                    cr[g] = c
                c_min[0] = jnp.min(functools.reduce(jnp.minimum, [c for _, c in parts]))

            @pl.when(i == 0)
            def _():
                keep([block(g, 0, True, zeros) for g in range(gp)])

            @pl.when(i > 0)
            def _():
                diag = [block(g, i, True, zeros) for g in range(gp)]
                prev = [block(g, i - 1, False, diag[g][1]) for g in range(gp)]
                keep([(diag[g][0] + prev[g][0], prev[g][1]) for g in range(gp)])

            def cond(carry):
                return jnp.logical_and(carry[0] < i, carry[1] < -SB_SKIP)

            def step(carry):
                more = [block(g, i - 1 - carry[0], False, cr[g]) for g in range(gp)]
                for g, (p, c) in enumerate(more):
                    acc[g] += p
                    cr[g] = c
                return carry[0] + 1, jnp.min(functools.reduce(jnp.minimum, [c for _, c in more]))

            walked, _ = lax.while_loop(cond, step, (jnp.minimum(i, 1), c_min[0]))
            return walked

        for u in range(per):
            rows = slice(u * tq, (u + 1) * tq)
            walked = query_block(pl.program_id(2) * per + u, rows)
            for g in range(gp):
                o = jnp.where(h0, acc[g, 0:tq, :], acc[g, tq:2 * tq, :])
                o_ref[rows, lanes[g]] = o
                tot_ref[rows, lanes[g]] = jnp.where(h0, cr[g, 0:tq, :], cr[g, tq:2 * tq, :])
                ro = lax.rsqrt(_head_sums(o * o, h0) * (1.0 / HEAD_LANES) + EPS)
                mb_ref[rows, lanes[g]] = (o * ro * hg_ref[:, lanes[g]]).astype(BF)
            nblk_ref[u * 8:(u + 1) * 8, :] = jnp.full((8, 128), walked.astype(F32))

    blk = lambda col0: pl.BlockSpec((per * tq, w), lambda b, hg_, i: (b * ns + i, col0 + hg_))
    seq = lambda col0: pl.BlockSpec((s, w), lambda b, hg_, i: (b, col0 + hg_))
    first = 1024 // w
    (o, tot, mb, nblk), rode = _ride_call(
        body, "sb_fwd", (nb, ng, ns),
        in_specs=[blk(first), seq(first + ng), seq(first + 2 * ng),
                  pl.BlockSpec((1, w), lambda b, hg_, i: (0, ng + hg_)), ANY],
        out_specs=[blk(0), blk(0), blk(ng),
                   pl.BlockSpec((None, None, per * 8, 128), lambda b, hg_, i: (b, hg_, i, 0))],
        out_shape=[jax.ShapeDtypeStruct((t, 512), F32), jax.ShapeDtypeStruct((t, 512), F32),
                   jax.ShapeDtypeStruct((t, 1024), BF), jax.ShapeDtypeStruct((nb, ng, nq * 8, 128), F32)],
        scratch_shapes=[pltpu.VMEM((gp, 2 * tq, 128), F32), pltpu.VMEM((gp, 2 * tq, 1), F32),
                        pltpu.SMEM((1,), F32)],
        operands=(proj, proj, proj, hg, merged), vmem_mb=40, ride=ride, aliases={4: 2})
    return o, tot, mb, nblk, rode


def _sb_bwd(proj, o_sb, tot, nblk, dmerged, dproj, hg, nb, s, tq, ride=None):
    t = nb * s
    tq = min(tq, s)
    nq = s // tq
    per = min(SB_BLOCKS_PER_STEP, nq)
    ns = nq // per
    gp, ng, w = SB_PAIRS_PER_STEP, 4 // SB_PAIRS_PER_STEP, 128 * SB_PAIRS_PER_STEP

    def body(q_ref, k_ref, v_ref, o_ref, tot_ref, nblk_ref, dm_ref, hg_ref, dproj_ref,
             dq_ref, dk_ref, dv_ref, dhg_ref, dk_acc, dv_acc, dq_acc, cm, cg):
        del dproj_ref
        h0, causal = _sb_masks(tq)
        tri_le = _tri(tq, lambda r, c: r <= c)
        tri_le = jnp.concatenate([tri_le, tri_le], axis=0)
        tri_lt = _tri(tq, lambda r, c: r < c)
        lanes = [slice(g * 128, (g + 1) * 128) for g in range(gp)]

        @pl.when(pl.program_id(2) == 0)
        def _():
            dk_acc[...] = jnp.zeros_like(dk_acc)
            dv_acc[...] = jnp.zeros_like(dv_acc)
            dhg_ref[...] = jnp.zeros_like(dhg_ref)

        def query_block(i, rows):
            for ref in (dq_acc, cm, cg):
                ref[...] = jnp.zeros_like(ref)
            qsts, dosts, tots = [], [], []
            for g in range(gp):
                qsts.append(_sb_stack(q_ref[rows, lanes[g]] * SB_SCALE, h0))
                o = o_ref[rows, lanes[g]]
                ro = lax.rsqrt(_head_sums(o * o, h0) * (1.0 / HEAD_LANES) + EPS)
                oh = o * ro
                dm = dm_ref[rows, lanes[g]].astype(F32)
                dhg_ref[:, lanes[g]] += jnp.sum(dm * oh, axis=0, keepdims=True)
                doh = dm * hg_ref[:, lanes[g]]
                do = ro * (doh - oh * (_head_sums(doh * oh, h0) * (1.0 / HEAD_LANES)))
                dosts.append(_sb_stack(do.astype(BF), h0))
                first = g * 128
                tots.append(jnp.concatenate(
                    [tot_ref[rows, first:first + 1], tot_ref[rows, first + HEAD_LANES:first + HEAD_LANES + 1]], axis=0))
            qsts_t = [q.T for q in qsts]
            dosts_t = [d.T for d in dosts]

            def block(g, j, masked, cm_in, cg_in):
                start = pl.multiple_of(j * tq, tq)
                kj = k_ref[pl.ds(start, tq), lanes[g]]
                vj = v_ref[pl.ds(start, tq), lanes[g]]
                n, l = _neg_log_sig(_dot_bt(qsts[g], kj))
                if masked:
                    n = jnp.where(causal, n, 0.0)
                a = jnp.exp(l - (tots[g] - cm_in - _running_sums(n, tri_le)))
                if masked:
                    a = jnp.where(causal, a, 0.0)
                gm = a * _dot_bt(dosts[g], vj)
                pp = cg_in + _dot(gm.astype(BF), tri_lt)
                dz = gm - jnp.exp(l) * (gm + pp)
                if masked:
                    dz = jnp.where(causal, dz, 0.0)
                dzb = dz.astype(BF)
                dk_acc[g, :, pl.ds(start, tq)] += _dot(qsts_t[g], dzb)
                dv_acc[g, :, pl.ds(start, tq)] += _dot(dosts_t[g], a.astype(BF))
                return (_dot(dzb, kj), cm_in + jnp.sum(n, axis=-1, keepdims=True),
                        cg_in + jnp.sum(gm, axis=-1, keepdims=True))

            def step(j, carry):
                for g in range(gp):
                    dq, cm[g], cg[g] = block(g, j, False, cm[g], cg[g])
                    dq_acc[g] += dq
                return carry

            walked = jnp.clip(nblk_ref[pl.program_id(0), pl.program_id(1), i].astype(jnp.int32),
                              jnp.minimum(i, 1), i)
            lax.fori_loop(i - walked, i - 1, step, 0)

            @pl.when(i == 0)
            def _():
                for g in range(gp):
                    dq_acc[g] = block(g, 0, True, cm[g], cg[g])[0]

            @pl.when(i > 0)
            def _():
                prev = [block(g, i - 1, False, cm[g], cg[g]) for g in range(gp)]
                diag = [block(g, i, True, prev[g][1], prev[g][2]) for g in range(gp)]
                for g in range(gp):
                    dq_acc[g] += prev[g][0] + diag[g][0]

            for g in range(gp):
                dq = jnp.where(h0, dq_acc[g, 0:tq, :], dq_acc[g, tq:2 * tq, :])
                dq_ref[rows, lanes[g]] = (dq * SB_SCALE).astype(BF)

        for u in range(per):
            query_block(pl.program_id(2) * per + u, slice(u * tq, (u + 1) * tq))

        @pl.when(pl.program_id(2) == ns - 1)
        def _():
            for g in range(gp):
                dk_ref[:, lanes[g]] = dk_acc[g].T.astype(BF)
                dv_ref[:, lanes[g]] = dv_acc[g].T.astype(BF)

    blk = lambda col0: pl.BlockSpec((per * tq, w), lambda b, hg_, i: (b * ns + i, col0 + hg_))
    seq = lambda col0: pl.BlockSpec((s, w), lambda b, hg_, i: (b, col0 + hg_))
    first = 1024 // w
    (dq, dk, dv, dhg), rode = _ride_call(
        body, "sb_bwd", (nb, ng, ns),
        in_specs=[blk(first), seq(first + ng), seq(first + 2 * ng), blk(0), blk(0),
                  pl.BlockSpec(memory_space=pltpu.SMEM), blk(ng),
                  pl.BlockSpec((1, w), lambda b, hg_, i: (0, ng + hg_)), ANY],
        out_specs=[blk(first), seq(0), seq(0), pl.BlockSpec((None, 1, w), lambda b, hg_, i: (b, 0, hg_))],
        out_shape=[jax.ShapeDtypeStruct((t, IN_COLS), BF), jax.ShapeDtypeStruct((t, 512), BF),
                   jax.ShapeDtypeStruct((t, 512), BF), jax.ShapeDtypeStruct((nb, 1, 512), F32)],
        scratch_shapes=[pltpu.VMEM((gp, 128, s), F32), pltpu.VMEM((gp, 128, s), F32),
                        pltpu.VMEM((gp, 2 * tq, 128), F32), pltpu.VMEM((gp, 2 * tq, 1), F32),
                        pltpu.VMEM((gp, 2 * tq, 1), F32)],
        operands=(proj, proj, proj, o_sb, tot, nblk.reshape(nb, ng, nq, 8, 128)[:, :, :, 0, 0], dmerged, hg, dproj),
        vmem_mb=48, ride=ride, aliases={8: 0})
    return dq, dk, dv, dhg, rode


def _place(buf, piece, col_block, name):
    t, w = piece.shape
    tm = min(t, 1024)

    def body(piece_ref, buf_ref, out_ref):
        del buf_ref
        out_ref[...] = piece_ref[...]

    return pl.pallas_call(
        body, name=name, grid=(t // tm,),
        in_specs=[pl.BlockSpec((tm, w), lambda i: (i, 0)), ANY],
        out_specs=pl.BlockSpec((tm, w), lambda i: (i, col_block)),
        out_shape=jax.ShapeDtypeStruct(buf.shape, buf.dtype), input_output_aliases={1: 0},
        compiler_params=_params(16, ("arbitrary",)),
    )(piece, buf)


def _softmax_rows(sc):
    e = jnp.exp(sc - jnp.max(sc, axis=-1, keepdims=True))
    return e / jnp.sum(e, axis=-1, keepdims=True)


def _mix_cross_fwd(x, merged, w_out, gc, w_cq, kv, w_co, s, tm):
    t, d = x.shape
    tm = min(tm, s)
    per = s // tm
    inv = 1.0 / math.sqrt(X_HEAD_DIM)

    def body(x_ref, m_ref, wo_ref, gc_ref, wq_ref, kv_ref, wc_ref, h1_ref, h2_ref, hn_ref, qc_ref, oc_ref):
        h1 = x_ref[...] + _dot(m_ref[...], wo_ref[...])
        h1_ref[...] = h1
        hn = (h1 * _rs(h1) * gc_ref[...]).astype(BF)
        hn_ref[...] = hn
        qc = _dot(hn, wq_ref[...]).astype(BF)
        qc_ref[...] = qc
        for h in range(X_HEADS):
            cols = slice(h * X_HEAD_DIM, (h + 1) * X_HEAD_DIM)
            kh = kv_ref[:, h * X_HEAD_DIM:(h + 1) * X_HEAD_DIM]
            vh = kv_ref[:, d + h * X_HEAD_DIM:d + (h + 1) * X_HEAD_DIM]
            p = _softmax_rows(_dot_bt(qc[:, cols], kh) * inv)
            oc_ref[:, cols] = _dot(p.astype(BF), vh).astype(BF)
        h2_ref[...] = h1 + _dot(oc_ref[...], wc_ref[...])

    row = lambda width: pl.BlockSpec((tm, width), lambda i: (i, 0))
    full = lambda a, b: pl.BlockSpec((a, b), lambda i: (0, 0))
    return pl.pallas_call(
        body, name="mix_cross_fwd", grid=(t // tm,),
        in_specs=[row(d), row(d), full(d, d), full(1, d), full(d, d),
                  pl.BlockSpec((N_MEM, 2 * d), lambda i: (i // per, 0)), full(d, d)],
        out_specs=[row(d), row(d), row(d), row(d), row(d)],
        out_shape=[jax.ShapeDtypeStruct((t, d), F32), jax.ShapeDtypeStruct((t, d), F32),
                   jax.ShapeDtypeStruct((t, d), BF), jax.ShapeDtypeStruct((t, d), BF),
                   jax.ShapeDtypeStruct((t, d), BF)],
        compiler_params=_params(48, ("arbitrary",)),
    )(*_in_hbm(x, merged, w_out, gc, w_cq, kv, w_co))


def _cross_bwd(dh2, h1, qc, gc, w_cq, kv, w_co, s, tm):
    t, d = dh2.shape
    tm = min(tm, s)
    per = s // tm
    nb = t // s
    inv = 1.0 / math.sqrt(X_HEAD_DIM)

    def body(dh2_ref, h1_ref, qc_ref, gc_ref, wq_ref, kv_ref, wc_ref, dh1_ref, dqc_ref, dkv_ref, dgc_ref):
        i = pl.program_id(0)

        @pl.when(i == 0)
        def _():
            dgc_ref[...] = jnp.zeros_like(dgc_ref)

        @pl.when(i % per == 0)
        def _():
            dkv_ref[...] = jnp.zeros_like(dkv_ref)

        dh2 = dh2_ref[...]
        h1 = h1_ref[...]
        r = _rs(h1)
        h1h = h1 * r
        gcv = gc_ref[...]
        qc = qc_ref[...]
        do = _dot_bt(dh2.astype(BF), wc_ref[...]).astype(BF)
        for h in range(X_HEADS):
            cols = slice(h * X_HEAD_DIM, (h + 1) * X_HEAD_DIM)
            vcols = slice(d + h * X_HEAD_DIM, d + (h + 1) * X_HEAD_DIM)
            kh = kv_ref[:, cols]
            vh = kv_ref[:, vcols]
            p = _softmax_rows(_dot_bt(qc[:, cols], kh) * inv)
            dp = _dot_bt(do[:, cols], vh)
            ds = (p * (dp - jnp.sum(dp * p, axis=-1, keepdims=True)) * inv).astype(BF)
            dqc_ref[:, cols] = _dot(ds, kh).astype(BF)
            dkv_ref[:, cols] += _dot_at(ds, qc[:, cols])
            dkv_ref[:, vcols] += _dot_at(p.astype(BF), do[:, cols])
        dhn = _dot_bt(dqc_ref[...], wq_ref[...])
        dx, dg = _rms_bwd(dhn, h1h, r, gcv)
        dh1_ref[...] = dh2 + dx
        dgc_ref[...] += jnp.sum(dg, axis=0, keepdims=True)

    row = lambda width: pl.BlockSpec((tm, width), lambda i: (i, 0))
    full = lambda a, b: pl.BlockSpec((a, b), lambda i: (0, 0))
    kvspec = pl.BlockSpec((N_MEM, 2 * d), lambda i: (i // per, 0))
    return pl.pallas_call(
        body, name="cross_bwd", grid=(t // tm,),
        in_specs=[row(d), row(d), row(d), full(1, d), full(d, d), kvspec, full(d, d)],
        out_specs=[row(d), row(d), kvspec, full(1, d)],
        out_shape=[jax.ShapeDtypeStruct((t, d), F32), jax.ShapeDtypeStruct((t, d), BF),
                   jax.ShapeDtypeStruct((nb * N_MEM, 2 * d), F32), jax.ShapeDtypeStruct((1, d), F32)],
        compiler_params=_params(48, ("arbitrary",)),
    )(*_in_hbm(dh2, h1, qc, gc, w_cq, kv, w_co))


def _mem_bwd(mem, gm, dkv, w_ckv, tm):
    t, d = mem.shape
    tm = min(tm, t)

    def body(mem_ref, dkv_ref, w_ref, dg_ref):
        @pl.when(pl.program_id(0) == 0)
        def _():
            dg_ref[...] = jnp.zeros_like(dg_ref)

        mv = mem_ref[...]
        dmn = _dot_bt(dkv_ref[...].astype(BF), w_ref[...])
        dg_ref[...] += jnp.sum(dmn * (mv * _rs(mv)), axis=0, keepdims=True)

    del gm
    return pl.pallas_call(
        body, name="mem_bwd", grid=(t // tm,),
        in_specs=[pl.BlockSpec((tm, d), lambda i: (i, 0)), pl.BlockSpec((tm, 2 * d), lambda i: (i, 0)),
                  pl.BlockSpec((d, 2 * d), lambda i: (0, 0))],
        out_specs=pl.BlockSpec((1, d), lambda i: (0, 0)),
        out_shape=jax.ShapeDtypeStruct((1, d), F32),
        compiler_params=_params(32, ("arbitrary",)),
    )(mem, dkv, w_ckv)


def _ffn_loss_fwd(h2, gf, w1, w2, gl, target, tm):
    t, d = h2.shape
    tm = min(tm, t)

    def body(h2_ref, gf_ref, w1_ref, w2_ref, gl_ref, tg_ref, hn_ref, f_ref, dh3_ref, dgl_ref, loss_ref):
        @pl.when(pl.program_id(0) == 0)
        def _():
            dgl_ref[...] = jnp.zeros_like(dgl_ref)
            loss_ref[...] = jnp.zeros_like(loss_ref)

        h2 = h2_ref[...]
        hn = (h2 * _rs(h2) * gf_ref[...]).astype(BF)
        hn_ref[...] = hn
        h3 = h2
        for c in range(4):
            f = jnp.maximum(_dot(hn, w1_ref[c]), 0.0)
            f_ref[:, c * 1024:(c + 1) * 1024] = f.astype(BF)
            h3 = h3 + _dot((f * f).astype(BF), w2_ref[c])
        r3 = _rs(h3)
        yh = h3 * r3
        glv = gl_ref[...]
        e = yh * glv - tg_ref[...]
        loss_ref[...] += 0.5 * jnp.sum(jnp.sum(e * e, axis=-1, keepdims=True) * (1.0 / d), axis=0, keepdims=True)
        dy = e * (1.0 / d)
        dx, dg = _rms_bwd(dy, yh, r3, glv)
        dh3_ref[...] = dx
        dgl_ref[...] += jnp.sum(dg, axis=0, keepdims=True)

    row = lambda width: pl.BlockSpec((tm, width), lambda i: (i, 0))
    return pl.pallas_call(
        body, name="ffn_loss_fwd", grid=(t // tm,),
        in_specs=[row(d), pl.BlockSpec((1, d), lambda i: (0, 0)), pl.BlockSpec((4, d, 1024), lambda i: (0, 0, 0), pipeline_mode=pl.Buffered(1)),
                  pl.BlockSpec((4, 1024, d), lambda i: (0, 0, 0), pipeline_mode=pl.Buffered(1)),
                  pl.BlockSpec((1, d), lambda i: (0, 0)), row(d)],
        out_specs=[row(d), row(D_FF), row(d), pl.BlockSpec((1, d), lambda i: (0, 0)),
                   pl.BlockSpec((1, 1), lambda i: (0, 0))],
        out_shape=[jax.ShapeDtypeStruct((t, d), BF), jax.ShapeDtypeStruct((t, D_FF), BF),
                   jax.ShapeDtypeStruct((t, d), F32), jax.ShapeDtypeStruct((1, d), F32),
                   jax.ShapeDtypeStruct((1, 1), F32)],
        compiler_params=_params(56, ("arbitrary",)),
    )(*_in_hbm(h2, gf, w1, w2, gl, target))


def _ffn_bwd(dh3, f, h2, gf, w1, w2, tm):
    t, d = h2.shape
    tm = min(tm, t)

    def body(dh3_ref, f_ref, h2_ref, gf_ref, w1_ref, w2_ref, dh2_ref, dpre_ref, dgf_ref):
        @pl.when(pl.program_id(0) == 0)
        def _():
            dgf_ref[...] = jnp.zeros_like(dgf_ref)

        dh3 = dh3_ref[...]
        dh3b = dh3.astype(BF)
        dhn = jnp.zeros((tm, d), F32)
        for c in range(4):
            cols = slice(c * 1024, (c + 1) * 1024)
            dpre = (_dot_bt(dh3b, w2_ref[c]) * (2.0 * f_ref[:, cols].astype(F32))).astype(BF)
            dpre_ref[:, cols] = dpre
            dhn = dhn + _dot_bt(dpre, w1_ref[c])
        h2 = h2_ref[...]
        r = _rs(h2)
        dx, dg = _rms_bwd(dhn, h2 * r, r, gf_ref[...])
        dh2_ref[...] = dh3 + dx
        dgf_ref[...] += jnp.sum(dg, axis=0, keepdims=True)

    row = lambda width: pl.BlockSpec((tm, width), lambda i: (i, 0))
    return pl.pallas_call(
        body, name="ffn_bwd", grid=(t // tm,),
        in_specs=[row(d), row(D_FF), row(d), pl.BlockSpec((1, d), lambda i: (0, 0)),
                  pl.BlockSpec((4, d, 1024), lambda i: (0, 0, 0), pipeline_mode=pl.Buffered(1)),
                  pl.BlockSpec((4, 1024, d), lambda i: (0, 0, 0), pipeline_mode=pl.Buffered(1))],
        out_specs=[row(d), row(D_FF), pl.BlockSpec((1, d), lambda i: (0, 0))],
        out_shape=[jax.ShapeDtypeStruct((t, d), F32), jax.ShapeDtypeStruct((t, D_FF), BF),
                   jax.ShapeDtypeStruct((1, d), F32)],
        compiler_params=_params(56, ("arbitrary",)),
    )(*_in_hbm(dh3, f, h2, gf, w1, w2))


def _in_bwd(dproj, dh1, x, g, w_in, tm, ride=None):
    t, d = x.shape
    n = w_in.shape[1]
    tm = min(tm, t)

    def body(dp_ref, dh1_ref, x_ref, g_ref, w_ref, dx_ref, dg_ref):
        @pl.when(pl.program_id(0) == 0)
        def _():
            dg_ref[...] = jnp.zeros_like(dg_ref)

        dxn = _dot_bt(dp_ref[...], w_ref[...])
        xv = x_ref[...]
        r = _rs(xv)
        dx, dg = _rms_bwd(dxn, xv * r, r, g_ref[...])
        dx_ref[...] = dh1_ref[...] + dx
        dg_ref[...] += jnp.sum(dg, axis=0, keepdims=True)

    row = lambda width: pl.BlockSpec((tm, width), lambda i: (i, 0))
    (dx, dg), rode = _ride_call(
        body, "in_bwd", (t // tm,),
        in_specs=[row(n), row(d), row(d), pl.BlockSpec((1, d), lambda i: (0, 0)),
                  pl.BlockSpec((d, n), lambda i: (0, 0))],
        out_specs=[row(d), pl.BlockSpec((1, d), lambda i: (0, 0))],
        out_shape=[jax.ShapeDtypeStruct((t, d), F32), jax.ShapeDtypeStruct((1, d), F32)],
        scratch_shapes=[], operands=(dproj, dh1, x, g, w_in), vmem_mb=48, ride=ride)
    return dx, dg, rode


class _GradReduce:
    def __init__(self, c_idx):
        self.c_idx = c_idx
        self.sums = {}

    def sibling(self, slabs):
        return _SiblingExchange(slabs)

    def chip(self, names, slabs, recv):
        for k, a, r in zip(names, slabs, recv):
            self.sums[k] = _chip_sum(a, r, self.c_idx, "chip_sum_" + k)
        return _ChipExchange([self.sums[k] for k in names])


def _full_weights(gathered):
    d = D_MODEL
    out = {}
    for k, a in gathered.items():
        if k in ("w_in", "w_ckv", "w_ff1"):
            out[k] = a.transpose(1, 0, 2).reshape(d, -1)
        else:
            out[k] = a.reshape(-1, d)
    return out


def _slabs(a):
    return a.reshape(N_DEV, -1, a.shape[-1])


def _local_step(x, mem, target, small, big, nb, s, tq=256, gather_rest=None, reduce=None):
    d = D_MODEL
    g_mix, g_v, w_sp, b_sp, g_head, g_cross, g_mem, g_ffn, g_fin = (
        small[k] for k in ("norm_mix_g", "gm_v_norm_g", "w_spatial", "b_spatial", "head_norm_g", "norm_cross_g",
                           "norm_mem_g", "norm_ffn_g", "norm_final_g"))
    tri = jnp.tril(jnp.ones((CHUNK, CHUNK), dtype=bool))
    w_sp_m = jnp.where(tri[None], w_sp, 0.0)
    wt = w_sp_m.astype(BF)
    wtt = jnp.swapaxes(w_sp_m, 1, 2).astype(BF)
    bb = jnp.broadcast_to(b_sp[:, :, None], (GM_GROUPS, CHUNK, CHUNK))
    hg_a = g_head[:, :GM_WIDTH]

    proj, xn = _norm_matmul(x, g_mix, big["w_in"], 1024, "in_proj")
    merged = _gmlp_fwd(proj, g_v, wt, bb, hg_a, 512)
    o_sb, tot, merged, nblk, gathered = _sb_fwd(proj, merged, g_head, nb, s, tq, ride=gather_rest)
    if gather_rest is not None:
        big = dict(big, **_full_weights(dict(zip(BIG[1:], gathered))))
    w1c = big["w_ff1"].reshape(d, 4, 1024).transpose(1, 0, 2)
    w2c = big["w_ff2"].reshape(4, 1024, d)
    kv, memn = _norm_matmul(mem, g_mem, big["w_ckv"], 512, "mem_proj")
    h1, h2, hn, qc, oc = _mix_cross_fwd(x, merged, big["w_out"], g_cross, big["w_cq"], kv, big["w_co"], s, 512)
    hn2, f, dh3, d_fin, loss = _ffn_loss_fwd(h2, g_ffn, w1c, w2c, g_fin, target, 512)

    gbig = {}
    dh2, dpre, d_ffn = _ffn_bwd(dh3, f, h2, g_ffn, w1c, w2c, 512)
    gbig["w_ff2"] = _slabs(_wgrad(f, dh3, 1024, 512, "wgrad_ff2", square_a=True, tm=2048))
    gbig["w_ff1"] = _slabs(_wgrad_wide(hn2, dpre, 1024, 512, "wgrad_ff1", col_shards=4))
    dh1, dqc, dkv, d_cross = _cross_bwd(dh2, h1, qc, g_cross, big["w_cq"], kv, big["w_co"], s, 512)
    gbig["w_co"] = _slabs(_wgrad(oc, dh2, 1024, 1024, "wgrad_co"))
    gbig["w_cq"] = _slabs(_wgrad(hn, dqc, 1024, 1024, "wgrad_cq"))
    gbig["w_ckv"] = _slabs(_wgrad(memn, dkv, 512, 1024, "wgrad_ckv", col_shards=4))
    d_mem = _mem_bwd(mem, g_mem, dkv, big["w_ckv"], 512)
    dmerged = _matmul_bt(dh1, big["w_out"], 1024, "out_bwd")
    gbig["w_out"] = _slabs(_wgrad(merged, dh1, 1024, 1024, "wgrad_out"))
    rest = BIG[1:]
    ride = reduce.sibling([gbig[k] for k in rest]) if reduce else None
    dproj, d_wsp, d_bb, d_gv, d_hga, recv = _gmlp_bwd(proj, dmerged, g_v, wt, wtt, bb, hg_a, 512, ride=ride)
    ride = reduce.chip(rest, [gbig[k] for k in rest], recv) if reduce else None
    dproj, dk, dv, d_hgb, parts_rest = _sb_bwd(proj, o_sb, tot, nblk, dmerged, dproj, g_head, nb, s, tq, ride=ride)
    dproj = _place(_place(dproj, dk, 3, "place_dk"), dv, 4, "place_dv")
    gbig["w_in"] = _slabs(_wgrad_wide(xn, dproj, 512, 1024, "wgrad_in", col_shards=4))
    last = None
    if reduce:
        recv = _run_exchange(reduce.sibling([gbig["w_in"]]), "grad_sibling_exchange_w_in")
        last = reduce.chip(["w_in"], [gbig["w_in"]], recv)
    grad_x, d_mix, _ = _in_bwd(dproj, dh1, x, g_mix, big["w_in"], 512)
    parts = dict(zip(rest, parts_rest))

    gsmall = {
        "norm_mix_g": d_mix, "gm_v_norm_g": d_gv, "w_spatial": d_wsp, "b_spatial": d_bb[:, :, 0],
        "head_norm_g": jnp.concatenate([d_hga, jnp.sum(d_hgb, axis=0)], axis=1), "norm_cross_g": d_cross,
        "norm_mem_g": d_mem, "norm_ffn_g": d_ffn, "norm_final_g": d_fin,
    }
    return loss, grad_x, gsmall, gbig, parts, last


BIG = ("w_in", "w_out", "w_cq", "w_ckv", "w_co", "w_ff1", "w_ff2")
SMALL = ("norm_mix_g", "gm_v_norm_g", "w_spatial", "b_spatial", "head_norm_g", "norm_cross_g", "norm_mem_g",
         "norm_ffn_g", "norm_final_g")


def _local_copies_start(srcs, stages, sems):
    loads = [pltpu.make_async_copy(src, stage, sems.at[w]) for w, (src, stage) in enumerate(zip(srcs, stages))]
    for ld in loads:
        ld.start()
    return loads


def _local_copies_finish(loads, stages, dsts, sems):
    stores = []
    for w, (ld, stage, dst) in enumerate(zip(loads, stages, dsts)):
        ld.wait()
        st = pltpu.make_async_copy(stage, dst, sems.at[w])
        st.start()
        stores.append(st)
    for st in stores:
        st.wait()


def _chip_sum(slabs, recv, c_idx, name):
    _, r, cw = slabs.shape
    tr = min(r, 256)

    def body(c_ref, a_ref, b_ref, o_ref):
        del c_ref
        o_ref[...] = (a_ref[...] + b_ref[...]).astype(BF)

    return pl.pallas_call(
        body, name=name,
        grid_spec=pltpu.PrefetchScalarGridSpec(
            num_scalar_prefetch=1, grid=(N_CHIPS, r // tr),
            in_specs=[pl.BlockSpec((None, tr, cw), lambda p, i, c_ref: (2 * p + c_ref[0], i, 0)),
                      pl.BlockSpec((None, tr, cw), lambda p, i, c_ref: (p, i, 0))],
            out_specs=pl.BlockSpec((None, tr, cw), lambda p, i, c_ref: (p, i, 0))),
        out_shape=jax.ShapeDtypeStruct((N_CHIPS, r, cw), BF),
        compiler_params=_params(32, ("arbitrary", "arbitrary")),
    )(c_idx, *_in_hbm(slabs, recv))


def _sum4(sums, parts, q_idx, name):
    _, r, cw = parts.shape
    tr = min(r, 256)

    def body(q_ref, own_ref, a_ref, b_ref, c_ref, o_ref):
        del q_ref
        o_ref[...] = ((own_ref[...].astype(F32) + a_ref[...].astype(F32)) + b_ref[...].astype(F32)) + c_ref[
            ...].astype(F32)

    spec = lambda k: pl.BlockSpec((None, tr, cw), lambda i, q_ref: ((q_ref[0] + k) % N_CHIPS, i, 0))
    return pl.pallas_call(
        body, name=name,
        grid_spec=pltpu.PrefetchScalarGridSpec(
            num_scalar_prefetch=1, grid=(r // tr,), in_specs=[spec(0), spec(1), spec(2), spec(3)],
            out_specs=pl.BlockSpec((tr, cw), lambda i, q_ref: (i, 0))),
        out_shape=jax.ShapeDtypeStruct((r, cw), F32),
        compiler_params=_params(32, ("arbitrary",)),
    )(q_idx, *_in_hbm(sums, parts, parts, parts))


def _half_exchange(halves):
    n = len(halves)

    def body(*refs):
        ins, outs, stages = refs[:n], refs[n:2 * n], refs[2 * n:3 * n]
        send_sems, recv_sems, ld_sems, st_sems = refs[3 * n:]
        x, y, c = lax.axis_index("x"), lax.axis_index("y"), lax.axis_index("c")
        loads = _local_copies_start(ins, stages, ld_sems)
        copies = []
        for w in range(n):
            cp = pltpu.make_async_remote_copy(
                src_ref=ins[w], dst_ref=outs[w].at[c], send_sem=send_sems.at[w], recv_sem=recv_sems.at[w],
                device_id=(x, y, 1 - c), device_id_type=MESH)
            cp.start()
            copies.append(cp)
        _local_copies_finish(loads, stages, [outs[w].at[c] for w in range(n)], st_sems)
        for cp in copies:
            cp.wait()

    return pl.pallas_call(
        body, name="grad_half_exchange",
        in_specs=[ANY] * n, out_specs=[ANY] * n,
        out_shape=[jax.ShapeDtypeStruct((2,) + a.shape, a.dtype) for a in halves],
        scratch_shapes=[pltpu.VMEM(a.shape, a.dtype) for a in halves] + [
            pltpu.SemaphoreType.DMA((n,)), pltpu.SemaphoreType.DMA((n,)),
            pltpu.SemaphoreType.DMA((n,)), pltpu.SemaphoreType.DMA((n,))],
        compiler_params=_params(24),
    )(*halves)


def _small_all_reduce(packed, ride=None):
    rows = packed.shape[0]
    ride = ride or _NoExchange()
    ri, ro = len(ride.in_arrays), len(ride.out_shape)

    def body(*refs):
        in_ref, rins, out_ref, routs = refs[0], refs[1:1 + ri], refs[1 + ri], refs[2 + ri:2 + ri + ro]
        pair, chip_sum, chips, d2d_send, d2d_recv, ici_send, ici_recv = refs[2 + ri + ro:9 + ri + ro]
        rscr = refs[9 + ri + ro:]
        ride.start(rins, routs, rscr)
        x, y, c = lax.axis_index("x"), lax.axis_index("y"), lax.axis_index("c")
        q = 2 * x + y
        pair[c] = in_ref[...]
        swap = pltpu.make_async_remote_copy(
            src_ref=in_ref, dst_ref=pair.at[c], send_sem=d2d_send, recv_sem=d2d_recv,
            device_id=(x, y, 1 - c), device_id_type=MESH)
        swap.start()
        swap.wait()
        both = pair[0] + pair[1]
        chip_sum[...] = both
        chips[q] = both
        copies = [pltpu.make_async_remote_copy(
            src_ref=chip_sum, dst_ref=chips.at[q], send_sem=ici_send.at[k], recv_sem=ici_recv.at[k],
            device_id=(px, py, c), device_id_type=MESH) for k, (px, py) in enumerate(_other_chips(x, y))]
        for cp in copies:
            cp.start()
        for cp in copies:
            cp.wait()
        out_ref[...] = ((chips[0] + chips[1]) + chips[2]) + chips[3]
        ride.finish(rins, routs, rscr)

    vmem = pl.BlockSpec(memory_space=pltpu.VMEM)
    res = pl.pallas_call(
        body, name="small_all_reduce",
        in_specs=[vmem] + [ANY] * ri, out_specs=[vmem] + [ANY] * ro,
        out_shape=[jax.ShapeDtypeStruct(packed.shape, F32)] + list(ride.out_shape),
        scratch_shapes=[pltpu.VMEM((2, rows, 128), F32), pltpu.VMEM((rows, 128), F32),
                        pltpu.VMEM((N_CHIPS, rows, 128), F32), pltpu.SemaphoreType.DMA, pltpu.SemaphoreType.DMA,
                        pltpu.SemaphoreType.DMA((3,)), pltpu.SemaphoreType.DMA((3,))] + list(ride.scratch_shapes),
        compiler_params=_params(16),
    )(packed, *ride.in_arrays)
    return res[0], res[1:]


def _adamw(g, w, m, v, name):
    r, cw = g.shape
    tr = 256 if r % 256 == 0 else r

    def body(g_ref, w_ref, m_ref, v_ref, d_ref, nm_ref, nv_ref):
        gv = g_ref[...]
        nm = ADAM_B1 * m_ref[...] + (1.0 - ADAM_B1) * gv
        nv = ADAM_B2 * v_ref[...] + (1.0 - ADAM_B2) * (gv * gv)
        m_hat = nm / (1.0 - ADAM_B1 ** ADAM_STEP)
        v_hat = nv / (1.0 - ADAM_B2 ** ADAM_STEP)
        d_ref[...] = -ADAM_LR * (m_hat / (jnp.sqrt(v_hat) + ADAM_EPS) + ADAM_WD * w_ref[...])
        nm_ref[...] = nm
        nv_ref[...] = nv

    spec = pl.BlockSpec((tr, cw), lambda i: (i, 0))
    return pl.pallas_call(
        body, name=name, grid=(r // tr,),
        in_specs=[spec] * 4, out_specs=[spec] * 3,
        out_shape=[jax.ShapeDtypeStruct((r, cw), F32)] * 3,
        compiler_params=_params(32, ("arbitrary",)),
    )(*_in_hbm(g, w, m, v))


def _small_params(args):
    small = {k: args[k].reshape(1, -1) for k in SMALL}
    small["w_spatial"] = args["w_spatial"][0]
    small["b_spatial"] = args["b_spatial"][0]
    return small


def _pack(parts, rows):
    flat = jnp.concatenate([p.reshape(-1).astype(F32) for p in parts])
    return jnp.pad(flat, (0, rows * 128 - flat.shape[0])).reshape(rows, 128)


def _unpack(packed, shapes):
    flat = packed.reshape(-1)
    out, off = [], 0
    for shp in shapes:
        size = math.prod(shp)
        out.append(flat[off:off + size].reshape(shp))
        off += size
    return out


def kernel(x, mem, norm_mix_g, w_in, gm_v_norm_g, w_spatial, b_spatial, head_norm_g, w_out, norm_cross_g, norm_mem_g, w_cq, w_ckv, w_co, norm_ffn_g, w_ff1, w_ff2, norm_final_g, loss_target, m_norm_mix_g, m_w_in, m_gm_v_norm_g, m_w_spatial, m_b_spatial, m_head_norm_g, m_w_out, m_norm_cross_g, m_norm_mem_g, m_w_cq, m_w_ckv, m_w_co, m_norm_ffn_g, m_w_ff1, m_w_ff2, m_norm_final_g, v_norm_mix_g, v_w_in, v_gm_v_norm_g, v_w_spatial, v_b_spatial, v_head_norm_g, v_w_out, v_norm_cross_g, v_norm_mem_g, v_w_cq, v_w_ckv, v_w_co, v_norm_ffn_g, v_w_ff1, v_w_ff2, v_norm_final_g):
    args = dict(locals())
    d = D_MODEL
    nb, s, _ = x.shape
    c_idx = lax.axis_index("c").astype(jnp.int32).reshape(1)
    q_idx = (2 * lax.axis_index("x") + lax.axis_index("y")).astype(jnp.int32).reshape(1)
    rest = BIG[1:]

    shards = {k: args[k][0].astype(BF) for k in BIG}
    big = _full_weights({"w_in": _run_exchange(_GatherExchange([shards["w_in"]]), "all_gather_w_in")[0]})
    gather_rest = _GatherExchange([shards[k] for k in rest])

    reduce = _GradReduce(c_idx)
    loss, grad_x, gsmall, _, parts, last = _local_step(
        x.reshape(nb * s, d), mem.reshape(nb * N_MEM, d), loss_target.reshape(nb * s, d), _small_params(args), big,
        nb, s, gather_rest=gather_rest, reduce=reduce)

    shapes = [args[k].shape for k in SMALL]
    n_small = sum(math.prod(sh) for sh in shapes)
    rows = -(-(n_small + 1) // 1024) * 8
    reduced, (parts["w_in"],) = _small_all_reduce(_pack([gsmall[k] for k in SMALL] + [loss], rows), ride=last)
    halves = [_sum4(reduce.sums[k], parts[k], q_idx, "sum4_" + k) for k in BIG]
    both = _half_exchange(halves)

    out = {"grad_x": grad_x.reshape(nb, s, d)}
    for k, g2 in zip(BIG, both):
        shp = args[k].shape
        g = g2.reshape(shp[1], shp[2])
        dl, nm, nv = _adamw(g, args[k][0], args["m_" + k][0], args["v_" + k][0], "adamw_" + k)
        out["grad_" + k], out["delta_" + k], out["new_m_" + k], out["new_v_" + k] = (
            a.reshape(shp) for a in (g, dl, nm, nv))

    dl, nm, nv = _adamw(reduced, _pack([args[k] for k in SMALL], rows), _pack([args["m_" + k] for k in SMALL], rows),
                        _pack([args["v_" + k] for k in SMALL], rows), "adamw_small")
    for name, arr in (("grad_", reduced), ("delta_", dl), ("new_m_", nm), ("new_v_", nv)):
        for k, a in zip(SMALL, _unpack(arr, shapes)):
            out[name + k] = a
    out["loss"] = reduced.reshape(-1)[n_small]

    names = ["norm_mix_g", "w_in", "gm_v_norm_g", "w_spatial", "b_spatial", "head_norm_g", "w_out", "norm_cross_g",
             "norm_mem_g", "w_cq", "w_ckv", "w_co", "norm_ffn_g", "w_ff1", "w_ff2", "norm_final_g"]
    return (out["loss"], out["grad_x"], *[out["grad_" + k] for k in names], *[out["delta_" + k] for k in names],
            *[out["new_m_" + k] for k in names], *[out["new_v_" + k] for k in names])
```

```python
import functools
import math

import jax
import jax.numpy as jnp
from jax import lax
from jax.experimental import pallas as pl
from jax.experimental.pallas import tpu as pltpu

F32 = jnp.float32
BF = jnp.bfloat16

EPS = 1e-6
D_MODEL = 1024
CHUNK = 128
GM_GROUPS = 4
GM_WIDTH = 512
SB_WIDTH = 512
HEAD_LANES = 64
SB_SCALE = 0.125
SB_SKIP = -104.0
X_HEADS = 4
X_HEAD_DIM = 256
N_MEM = 256
D_FF = 4096
IN_COLS = 2560
N_CHIPS = 4
N_DEV = 8

ADAM_LR = 0.001
ADAM_B1 = 0.9
ADAM_B2 = 0.999
ADAM_EPS = 1e-08
ADAM_WD = 0.01
ADAM_STEP = 10

V7X_VMEM_BYTES = 64 * 1024 * 1024
MESH = pl.DeviceIdType.MESH
ANY = pl.BlockSpec(memory_space=pl.ANY)

GELU_C = math.sqrt(2.0 / math.pi)
GELU_A = 0.044715


def _params(vmem_mb, sem=None):
    assert vmem_mb * 1024 * 1024 <= V7X_VMEM_BYTES
    return pltpu.CompilerParams(vmem_limit_bytes=vmem_mb * 1024 * 1024, dimension_semantics=sem)


PIN_MIN_ELEMENTS = 1 << 18


def _in_hbm(*arrays):
    return tuple(pltpu.with_memory_space_constraint(a, pltpu.HBM) if a.size >= PIN_MIN_ELEMENTS else a
                 for a in arrays)


def _dot(a, b):
    return jnp.dot(a, b, preferred_element_type=F32)


def _dot_bt(a, b):
    return lax.dot_general(a, b, (((1,), (1,)), ((), ())), preferred_element_type=F32)


def _dot_at(a, b):
    return lax.dot_general(a, b, (((0,), (0,)), ((), ())), preferred_element_type=F32)


def _gelu(x):
    t = jnp.tanh(GELU_C * (x + GELU_A * x * x * x))
    return 0.5 * x * (1.0 + t)


def _gelu_and_grad(x):
    x2 = x * x
    t = jnp.tanh(GELU_C * (x + GELU_A * x2 * x))
    h = 0.5 * (1.0 + t)
    return x * h, h + 0.5 * x * (1.0 - t * t) * (GELU_C * (1.0 + 3.0 * GELU_A * x2))


def _rs(x):
    return lax.rsqrt(jnp.mean(x * x, axis=-1, keepdims=True) + EPS)


def _rms_bwd(dxn, xhat, r, g):
    dxh = dxn * g
    dx = r * (dxh - xhat * jnp.mean(dxh * xhat, axis=-1, keepdims=True))
    return dx, dxn * xhat


def _norm_matmul(x, g, w, tm, name):
    t, d = x.shape
    n = w.shape[1]
    tm = min(tm, t)

    def body(x_ref, g_ref, w_ref, out_ref, xn_ref):
        xv = x_ref[...]
        xn = (xv * _rs(xv) * g_ref[...]).astype(BF)
        xn_ref[...] = xn
        out_ref[...] = _dot(xn, w_ref[...]).astype(out_ref.dtype)

    return pl.pallas_call(
        body, name=name, grid=(t // tm,),
        in_specs=[pl.BlockSpec((tm, d), lambda i: (i, 0)), pl.BlockSpec((1, d), lambda i: (0, 0)),
                  pl.BlockSpec((d, n), lambda i: (0, 0))],
        out_specs=[pl.BlockSpec((tm, n), lambda i: (i, 0)), pl.BlockSpec((tm, d), lambda i: (i, 0))],
        out_shape=[jax.ShapeDtypeStruct((t, n), BF), jax.ShapeDtypeStruct((t, d), BF)],
        compiler_params=_params(48, ("arbitrary",)),
    )(*_in_hbm(x, g, w))


def _wgrad(a, g, tn, tk, name, square_a=False, col_shards=1, tm=1024):
    t, m = a.shape
    n = g.shape[1]
    tk = min(tk, t)
    tm = min(m, tm)
    ns = n // col_shards
    assert ns % tn == 0 and m % tm == 0
    per = ns // tn
    nk = t // tk

    def body(a_ref, g_ref, o_ref):
        k = pl.program_id(2)

        @pl.when(k == 0)
        def _():
            o_ref[...] = jnp.zeros_like(o_ref)

        av = a_ref[...]
        if square_a:
            af = av.astype(F32)
            av = af * af
        o_ref[...] += _dot_at(av.astype(BF), g_ref[...].astype(BF))

    return pl.pallas_call(
        body, name=name, grid=(m // tm, n // tn, nk),
        in_specs=[pl.BlockSpec((tk, tm), lambda i, j, k: (k, i)), pl.BlockSpec((tk, tn), lambda i, j, k: (k, j))],
        out_specs=pl.BlockSpec((None, tm, tn), lambda i, j, k: (j // per, i, j % per)),
        out_shape=jax.ShapeDtypeStruct((col_shards, m, ns), F32),
        compiler_params=_params(48, ("arbitrary", "arbitrary", "arbitrary")),
    )(*_in_hbm(a, g))


def _wgrad_wide(a, g, tm, tk, name, col_shards):
    t, m = a.shape
    n = g.shape[1]
    tk = min(tk, t)
    tm = min(tm, m)
    ns = n // col_shards

    def body(a_ref, g_ref, o_ref):
        @pl.when(pl.program_id(1) == 0)
        def _():
            o_ref[...] = jnp.zeros_like(o_ref)

        a_t = a_ref[...].astype(BF).T
        for p in range(col_shards):
            o_ref[p] += _dot(a_t, g_ref[:, p * ns:(p + 1) * ns].astype(BF))

    return pl.pallas_call(
        body, name=name, grid=(m // tm, t // tk),
        in_specs=[pl.BlockSpec((tk, tm), lambda i, k: (k, i)), pl.BlockSpec((tk, n), lambda i, k: (k, 0))],
        out_specs=pl.BlockSpec((col_shards, tm, ns), lambda i, k: (0, i, 0)),
        out_shape=jax.ShapeDtypeStruct((col_shards, m, ns), F32),
        compiler_params=_params(48, ("arbitrary", "arbitrary")),
    )(*_in_hbm(a, g))


def _matmul_bt(a, w, tm, name):
    t, n = a.shape
    k = w.shape[0]
    tm = min(tm, t)

    def body(a_ref, w_ref, o_ref):
        o_ref[...] = _dot_bt(a_ref[...].astype(BF), w_ref[...]).astype(o_ref.dtype)

    return pl.pallas_call(
        body, name=name, grid=(t // tm,),
        in_specs=[pl.BlockSpec((tm, n), lambda i: (i, 0)), pl.BlockSpec((k, n), lambda i: (0, 0))],
        out_specs=pl.BlockSpec((tm, k), lambda i: (i, 0)),
        out_shape=jax.ShapeDtypeStruct((t, k), BF),
        compiler_params=_params(32, ("arbitrary",)),
    )(*_in_hbm(a, w))


def _gmlp_fwd(proj, gg, wt, bb, hg, tm):
    t = proj.shape[0]
    tm = min(tm, t)

    def body(u_ref, v_ref, gg_ref, wt_ref, bb_ref, hg_ref, out_ref):
        for cc in range(tm // CHUNK):
            rows = slice(cc * CHUNK, (cc + 1) * CHUNK)
            for g in range(GM_GROUPS):
                cols = slice(g * 128, (g + 1) * 128)
                u = _gelu(u_ref[rows, cols].astype(F32))
                gv = _gelu(v_ref[rows, cols].astype(F32))
                vn = gv * _rs(gv) * gg_ref[:, cols]
                mixed = _dot(wt_ref[g], vn.astype(BF)) + bb_ref[g]
                a = u * mixed
                out_ref[rows, cols] = (a * _rs(a) * hg_ref[:, cols]).astype(BF)

    return pl.pallas_call(
        body, name="gmlp_fwd", grid=(t // tm,),
        in_specs=[pl.BlockSpec((tm, 512), lambda i: (i, 0)), pl.BlockSpec((tm, 512), lambda i: (i, 1)),
                  pl.BlockSpec((1, 512), lambda i: (0, 0)), pl.BlockSpec((4, 128, 128), lambda i: (0, 0, 0)),
                  pl.BlockSpec((4, 128, 128), lambda i: (0, 0, 0)), pl.BlockSpec((1, 512), lambda i: (0, 0))],
        out_specs=pl.BlockSpec((tm, 512), lambda i: (i, 0)),
        out_shape=jax.ShapeDtypeStruct((t, 1024), BF),
        compiler_params=_params(32, ("arbitrary",)),
    )(*_in_hbm(proj, proj, gg, wt, bb, hg))


def _gmlp_bwd(proj, dmerged, gg, wt, wtt, bb, hg, tm, ride=None):
    t = proj.shape[0]
    tm = min(tm, t)
    nsteps = t // tm

    def body(u_ref, v_ref, dm_ref, gg_ref, wt_ref, wtt_ref, bb_ref, hg_ref,
             dp_ref, dw_ref, db_ref, dgg_ref, dhg_ref):
        i = pl.program_id(0)

        @pl.when(i == 0)
        def _():
            dw_ref[...] = jnp.zeros_like(dw_ref)
            db_ref[...] = jnp.zeros_like(db_ref)
            dgg_ref[...] = jnp.zeros_like(dgg_ref)
            dhg_ref[...] = jnp.zeros_like(dhg_ref)

        for cc in range(tm // CHUNK):
            rows = slice(cc * CHUNK, (cc + 1) * CHUNK)
            for g in range(GM_GROUPS):
                cols = slice(g * 128, (g + 1) * 128)
                up = u_ref[rows, cols].astype(F32)
                gp = v_ref[rows, cols].astype(F32)
                u, u_grad = _gelu_and_grad(up)
                gv, gv_grad = _gelu_and_grad(gp)
                rv = _rs(gv)
                gvh = gv * rv
                ggv = gg_ref[:, cols]
                vnb = (gvh * ggv).astype(BF)
                mixed = _dot(wt_ref[g], vnb) + bb_ref[g]
                a = u * mixed
                ra = _rs(a)
                ah = a * ra
                dm = dm_ref[rows, cols].astype(F32)
                dhg_ref[:, cols] += jnp.sum(dm * ah, axis=0, keepdims=True)
                dah = dm * hg_ref[:, cols]
                da = ra * (dah - ah * jnp.mean(dah * ah, axis=-1, keepdims=True))
                du = da * mixed
                dmix = da * u
                db_ref[g] += dmix
                dmb = dmix.astype(BF)
                dw_ref[g] += _dot_bt(dmb, vnb)
                dvn = _dot(wtt_ref[g], dmb)
                dgg_ref[:, cols] += jnp.sum(dvn * gvh, axis=0, keepdims=True)
                dgh = dvn * ggv
                dgv = rv * (dgh - gvh * jnp.mean(dgh * gvh, axis=-1, keepdims=True))
                dp_ref[rows, cols] = (du * u_grad).astype(BF)
                dp_ref[rows, 512 + g * 128:512 + (g + 1) * 128] = (dgv * gv_grad).astype(BF)

        @pl.when(i == nsteps - 1)
        def _():
            r = lax.broadcasted_iota(jnp.int32, (CHUNK, CHUNK), 0)
            c = lax.broadcasted_iota(jnp.int32, (CHUNK, CHUNK), 1)
            for g in range(GM_GROUPS):
                dw_ref[g] = jnp.where(c <= r, dw_ref[g], 0.0)
                db_ref[g] = jnp.broadcast_to(jnp.sum(db_ref[g], axis=-1, keepdims=True), (CHUNK, CHUNK))

    small = lambda shape: pl.BlockSpec(shape, lambda i: (0,) * len(shape))
    res, rode = _ride_call(
        body, "gmlp_bwd", (nsteps,),
        in_specs=[pl.BlockSpec((tm, 512), lambda i: (i, 0)), pl.BlockSpec((tm, 512), lambda i: (i, 1)),
                  pl.BlockSpec((tm, 512), lambda i: (i, 0)), small((1, 512)), small((4, 128, 128)),
                  small((4, 128, 128)), small((4, 128, 128)), small((1, 512))],
        out_specs=[pl.BlockSpec((tm, 1024), lambda i: (i, 0)), small((4, 128, 128)), small((4, 128, 128)),
                   small((1, 512)), small((1, 512))],
        out_shape=[jax.ShapeDtypeStruct((t, IN_COLS), BF), jax.ShapeDtypeStruct((4, 128, 128), F32),
                   jax.ShapeDtypeStruct((4, 128, 128), F32), jax.ShapeDtypeStruct((1, 512), F32),
                   jax.ShapeDtypeStruct((1, 512), F32)],
        scratch_shapes=[], operands=(proj, proj, dmerged, gg, wt, wtt, bb, hg), vmem_mb=32, ride=ride)
    return (*res, rode)


def _other_chips(x, y):
    return ((1 - x, y), (x, 1 - y), (1 - x, 1 - y))


class _GatherExchange:
    def __init__(self, shards):
        n = len(shards)
        self.n = n
        self.in_arrays = list(shards)
        self.out_shape = [jax.ShapeDtypeStruct((N_CHIPS,) + a.shape, a.dtype) for a in shards]
        self.half_rows = [a.shape[0] // 2 for a in shards]
        sems = lambda k: pltpu.SemaphoreType.DMA((k,))
        self.scratch_shapes = [pltpu.VMEM(a.shape, a.dtype) for a in shards] + [
            sems(3 * n), sems(3 * n), sems(3 * n), sems(3 * n), sems(n), sems(n)]

    def _copies(self, ins, outs, scr):
        n = self.n
        stages, (ici_send, ici_recv, d2d_send, d2d_recv, ld_sems, st_sems) = scr[:n], scr[n:]
        x, y, c = lax.axis_index("x"), lax.axis_index("y"), lax.axis_index("c")
        q = 2 * x + y
        loads = [pltpu.make_async_copy(ins[w], stages[w], ld_sems.at[w]) for w in range(n)]
        stores = [pltpu.make_async_copy(stages[w], outs[w].at[q], st_sems.at[w]) for w in range(n)]
        ici, d2d = [], []
        for w in range(n):
            half = pl.ds(c * self.half_rows[w], self.half_rows[w])
            for k, (px, py) in enumerate(_other_chips(x, y)):
                ici.append(pltpu.make_async_remote_copy(
                    src_ref=ins[w].at[half], dst_ref=outs[w].at[q, half], send_sem=ici_send.at[3 * w + k],
                    recv_sem=ici_recv.at[3 * w + k], device_id=(px, py, c), device_id_type=MESH))
                landed = outs[w].at[2 * px + py, half]
                d2d.append(pltpu.make_async_remote_copy(
                    src_ref=landed, dst_ref=landed, send_sem=d2d_send.at[3 * w + k],
                    recv_sem=d2d_recv.at[3 * w + k], device_id=(x, y, 1 - c), device_id_type=MESH))
        return loads, stores, ici, d2d

    def start(self, ins, outs, scr):
        loads, stores, ici, _ = self._copies(ins, outs, scr)
        for cp in loads + ici:
            cp.start()
        for ld, st in zip(loads, stores):
            ld.wait()
            st.start()

    def relay(self, ins, outs, scr):
        _, _, ici, d2d = self._copies(ins, outs, scr)
        for got, fwd in zip(ici, d2d):
            got.wait_recv()
            fwd.start()

    def finish(self, ins, outs, scr):
        _, stores, ici, d2d = self._copies(ins, outs, scr)
        for cp in ici:
            cp.wait_send()
        for cp in d2d + stores:
            cp.wait()


class _SiblingExchange:
    def __init__(self, slabs):
        n = len(slabs)
        self.n = n
        self.in_arrays = list(slabs)
        self.out_shape = [jax.ShapeDtypeStruct((N_CHIPS,) + a.shape[1:], a.dtype) for a in slabs]
        self.scratch_shapes = [pltpu.SemaphoreType.DMA((4 * n,)), pltpu.SemaphoreType.DMA((4 * n,))]

    def _copies(self, ins, outs, scr):
        send_sems, recv_sems = scr
        x, y, c = lax.axis_index("x"), lax.axis_index("y"), lax.axis_index("c")
        return [pltpu.make_async_remote_copy(
            src_ref=ins[w].at[2 * p + (1 - c)], dst_ref=outs[w].at[p], send_sem=send_sems.at[4 * w + p],
            recv_sem=recv_sems.at[4 * w + p], device_id=(x, y, 1 - c), device_id_type=MESH)
            for w in range(self.n) for p in range(N_CHIPS)]

    def start(self, ins, outs, scr):
        for cp in self._copies(ins, outs, scr):
            cp.start()

    def finish(self, ins, outs, scr):
        for cp in self._copies(ins, outs, scr):
            cp.wait()


class _ChipExchange:
    def __init__(self, sums):
        n = len(sums)
        self.n = n
        self.in_arrays = list(sums)
        self.out_shape = [jax.ShapeDtypeStruct(a.shape, a.dtype) for a in sums]
        self.scratch_shapes = [pltpu.SemaphoreType.DMA((3 * n,)), pltpu.SemaphoreType.DMA((3 * n,))]

    def _copies(self, ins, outs, scr):
        send_sems, recv_sems = scr
        x, y, c = lax.axis_index("x"), lax.axis_index("y"), lax.axis_index("c")
        q = 2 * x + y
        return [pltpu.make_async_remote_copy(
            src_ref=ins[w].at[2 * px + py], dst_ref=outs[w].at[q], send_sem=send_sems.at[3 * w + k],
            recv_sem=recv_sems.at[3 * w + k], device_id=(px, py, c), device_id_type=MESH)
            for w in range(self.n) for k, (px, py) in enumerate(_other_chips(x, y))]

    def start(self, ins, outs, scr):
        for cp in self._copies(ins, outs, scr):
            cp.start()

    def finish(self, ins, outs, scr):
        for cp in self._copies(ins, outs, scr):
            cp.wait()


class _NoExchange:
    in_arrays, out_shape, scratch_shapes = (), (), ()

    def start(self, ins, outs, scr):
        pass

    def finish(self, ins, outs, scr):
        pass


def _run_exchange(ex, name):
    n_in, n_out = len(ex.in_arrays), len(ex.out_shape)

    def body(*refs):
        ins, outs, scr = refs[:n_in], refs[n_in:n_in + n_out], refs[n_in + n_out:]
        ex.start(ins, outs, scr)
        if hasattr(ex, "relay"):
            ex.relay(ins, outs, scr)
        ex.finish(ins, outs, scr)

    return pl.pallas_call(
        body, name=name, in_specs=[ANY] * n_in, out_specs=[ANY] * n_out, out_shape=ex.out_shape,
        scratch_shapes=ex.scratch_shapes, compiler_params=_params(24),
    )(*ex.in_arrays)


def _ride_call(body, name, grid, in_specs, out_specs, out_shape, scratch_shapes, operands, vmem_mb, ride=None,
               aliases=None):
    ride = ride or _NoExchange()
    ni, no, ns = len(in_specs), len(out_specs), len(scratch_shapes)
    ri, ro = len(ride.in_arrays), len(ride.out_shape)
    total = math.prod(grid)

    def wrapped(*refs):
        ins, rins = refs[:ni], refs[ni:ni + ri]
        outs, routs = refs[ni + ri:ni + ri + no], refs[ni + ri + no:ni + ri + no + ro]
        scr, rscr = refs[ni + ri + no + ro:ni + ri + no + ro + ns], refs[ni + ri + no + ro + ns:]
        step = pl.program_id(0)
        for ax in range(1, len(grid)):
            step = step * grid[ax] + pl.program_id(ax)

        @pl.when(step == 0)
        def _():
            ride.start(rins, routs, rscr)

        if hasattr(ride, "relay"):
            @pl.when(step == (3 * total) // 4)
            def _():
                ride.relay(rins, routs, rscr)

        body(*ins, *outs, *scr)

        @pl.when(step == total - 1)
        def _():
            ride.finish(rins, routs, rscr)

    res = pl.pallas_call(
        wrapped, name=name, grid=grid, in_specs=list(in_specs) + [ANY] * ri, out_specs=list(out_specs) + [ANY] * ro,
        out_shape=list(out_shape) + list(ride.out_shape),
        scratch_shapes=list(scratch_shapes) + list(ride.scratch_shapes), input_output_aliases=aliases or {},
        compiler_params=_params(vmem_mb, ("arbitrary",) * len(grid)),
    )(*_in_hbm(*operands), *ride.in_arrays)
    return res[:no], res[no:]


def _neg_log_sig(z):
    n = jnp.maximum(z, 0.0) + jnp.log(1.0 + jnp.exp(-jnp.abs(z)))
    return n, z - n


def _running_sums(n, tri2):
    hi = n.astype(BF)
    lo = (n - hi.astype(F32)).astype(BF)
    return _dot(jnp.concatenate([hi, lo], axis=1), tri2)


def _head_sums(x, h0):
    s0 = jnp.sum(jnp.where(h0, x, 0.0), axis=-1, keepdims=True)
    s1 = jnp.sum(jnp.where(h0, 0.0, x), axis=-1, keepdims=True)
    return jnp.where(h0, s0, s1)


SB_BLOCKS_PER_STEP = 4
SB_PAIRS_PER_STEP = 2


def _sb_masks(tq):
    h0 = lax.broadcasted_iota(jnp.int32, (tq, 128), 1) < HEAD_LANES
    r = lax.broadcasted_iota(jnp.int32, (2 * tq, tq), 0)
    c = lax.broadcasted_iota(jnp.int32, (2 * tq, tq), 1)
    return h0, c < jnp.where(r >= tq, r - tq, r)


def _sb_stack(x, h0):
    zero = jnp.zeros_like(x)
    return jnp.concatenate([jnp.where(h0, x, zero), jnp.where(h0, zero, x)], axis=0)


def _tri(tq, op):
    return op(lax.broadcasted_iota(jnp.int32, (tq, tq), 0), lax.broadcasted_iota(jnp.int32, (tq, tq), 1)).astype(BF)


def _sb_fwd(proj, merged, hg, nb, s, tq, ride=None):
    t = nb * s
    tq = min(tq, s)
    nq = s // tq
    per = min(SB_BLOCKS_PER_STEP, nq)
    ns = nq // per
    gp, ng, w = SB_PAIRS_PER_STEP, 4 // SB_PAIRS_PER_STEP, 128 * SB_PAIRS_PER_STEP

    def body(q_ref, k_ref, v_ref, hg_ref, merged_ref, o_ref, tot_ref, mb_ref, nblk_ref, acc, cr, c_min):
        del merged_ref
        h0, causal = _sb_masks(tq)
        tri_gt = _tri(tq, lambda r, c: r > c)
        tri_gt = jnp.concatenate([tri_gt, tri_gt], axis=0)
        lanes = [slice(g * 128, (g + 1) * 128) for g in range(gp)]
        zeros = jnp.zeros((2 * tq, 1), F32)

        def query_block(i, rows):
            qsts = [_sb_stack(q_ref[rows, lanes[g]] * SB_SCALE, h0) for g in range(gp)]

            def block(g, j, masked, c_in):
                start = pl.multiple_of(j * tq, tq)
                kj = k_ref[pl.ds(start, tq), lanes[g]]
                vj = v_ref[pl.ds(start, tq), lanes[g]]
                n, l = _neg_log_sig(_dot_bt(qsts[g], kj))
                if masked:
                    n = jnp.where(causal, n, 0.0)
                a = jnp.exp(l - (_running_sums(n, tri_gt) + c_in))
                if masked:
                    a = jnp.where(causal, a, 0.0)
                return _dot(a.astype(BF), vj), c_in + jnp.sum(n, axis=-1, keepdims=True)

            def keep(parts):
                for g, (p, c) in enumerate(parts):
                    acc[g] = p
                    cr[g] = c
                c_min[0] = jnp.min(functools.reduce(jnp.minimum, [c for _, c in parts]))

            @pl.when(i == 0)
            def _():
                keep([block(g, 0, True, zeros) for g in range(gp)])

            @pl.when(i > 0)
            def _():
                diag = [block(g, i, True, zeros) for g in range(gp)]
                prev = [block(g, i - 1, False, diag[g][1]) for g in range(gp)]
                keep([(diag[g][0] + prev[g][0], prev[g][1]) for g in range(gp)])

            def cond(carry):
                return jnp.logical_and(carry[0] < i, carry[1] < -SB_SKIP)

            def step(carry):
                more = [block(g, i - 1 - carry[0], False, cr[g]) for g in range(gp)]
                for g, (p, c) in enumerate(more):
                    acc[g] += p
                    cr[g] = c
                return carry[0] + 1, jnp.min(functools.reduce(jnp.minimum, [c for _, c in more]))

            walked, _ = lax.while_loop(cond, step, (jnp.minimum(i, 1), c_min[0]))
            return walked

        for u in range(per):
            rows = slice(u * tq, (u + 1) * tq)
            walked = query_block(pl.program_id(2) * per + u, rows)
            for g in range(gp):
                o = jnp.where(h0, acc[g, 0:tq, :], acc[g, tq:2 * tq, :])
                o_ref[rows, lanes[g]] = o
                tot_ref[rows, lanes[g]] = jnp.where(h0, cr[g, 0:tq, :], cr[g, tq:2 * tq, :])
                ro = lax.rsqrt(_head_sums(o * o, h0) * (1.0 / HEAD_LANES) + EPS)
                mb_ref[rows, lanes[g]] = (o * ro * hg_ref[:, lanes[g]]).astype(BF)
            nblk_ref[u * 8:(u + 1) * 8, :] = jnp.full((8, 128), walked.astype(F32))

    blk = lambda col0: pl.BlockSpec((per * tq, w), lambda b, hg_, i: (b * ns + i, col0 + hg_))
    seq = lambda col0: pl.BlockSpec((s, w), lambda b, hg_, i: (b, col0 + hg_))
    first = 1024 // w
    (o, tot, mb, nblk), rode = _ride_call(
        body, "sb_fwd", (nb, ng, ns),
        in_specs=[blk(first), seq(first + ng), seq(first + 2 * ng),
                  pl.BlockSpec((1, w), lambda b, hg_, i: (0, ng + hg_)), ANY],
        out_specs=[blk(0), blk(0), blk(ng),
                   pl.BlockSpec((None, None, per * 8, 128), lambda b, hg_, i: (b, hg_, i, 0))],
        out_shape=[jax.ShapeDtypeStruct((t, 512), F32), jax.ShapeDtypeStruct((t, 512), F32),
                   jax.ShapeDtypeStruct((t, 1024), BF), jax.ShapeDtypeStruct((nb, ng, nq * 8, 128), F32)],
        scratch_shapes=[pltpu.VMEM((gp, 2 * tq, 128), F32), pltpu.VMEM((gp, 2 * tq, 1), F32),
                        pltpu.SMEM((1,), F32)],
        operands=(proj, proj, proj, hg, merged), vmem_mb=40, ride=ride, aliases={4: 2})
    return o, tot, mb, nblk, rode


def _sb_bwd(proj, o_sb, tot, nblk, dmerged, dproj, hg, nb, s, tq, ride=None):
    t = nb * s
    tq = min(tq, s)
    nq = s // tq
    per = min(SB_BLOCKS_PER_STEP, nq)
    ns = nq // per
    gp, ng, w = SB_PAIRS_PER_STEP, 4 // SB_PAIRS_PER_STEP, 128 * SB_PAIRS_PER_STEP

    def body(q_ref, k_ref, v_ref, o_ref, tot_ref, nblk_ref, dm_ref, hg_ref, dproj_ref,
             dq_ref, dk_ref, dv_ref, dhg_ref, dk_acc, dv_acc, dq_acc, cm, cg):
        del dproj_ref
        h0, causal = _sb_masks(tq)
        tri_le = _tri(tq, lambda r, c: r <= c)
        tri_le = jnp.concatenate([tri_le, tri_le], axis=0)
        tri_lt = _tri(tq, lambda r, c: r < c)
        lanes = [slice(g * 128, (g + 1) * 128) for g in range(gp)]

        @pl.when(pl.program_id(2) == 0)
        def _():
            dk_acc[...] = jnp.zeros_like(dk_acc)
            dv_acc[...] = jnp.zeros_like(dv_acc)
            dhg_ref[...] = jnp.zeros_like(dhg_ref)

        def query_block(i, rows):
            for ref in (dq_acc, cm, cg):
                ref[...] = jnp.zeros_like(ref)
            qsts, dosts, tots = [], [], []
            for g in range(gp):
                qsts.append(_sb_stack(q_ref[rows, lanes[g]] * SB_SCALE, h0))
                o = o_ref[rows, lanes[g]]
                ro = lax.rsqrt(_head_sums(o * o, h0) * (1.0 / HEAD_LANES) + EPS)
                oh = o * ro
                dm = dm_ref[rows, lanes[g]].astype(F32)
                dhg_ref[:, lanes[g]] += jnp.sum(dm * oh, axis=0, keepdims=True)
                doh = dm * hg_ref[:, lanes[g]]
                do = ro * (doh - oh * (_head_sums(doh * oh, h0) * (1.0 / HEAD_LANES)))
                dosts.append(_sb_stack(do.astype(BF), h0))
                first = g * 128
                tots.append(jnp.concatenate(
                    [tot_ref[rows, first:first + 1], tot_ref[rows, first + HEAD_LANES:first + HEAD_LANES + 1]], axis=0))
            qsts_t = [q.T for q in qsts]
            dosts_t = [d.T for d in dosts]

            def block(g, j, masked, cm_in, cg_in):
                start = pl.multiple_of(j * tq, tq)
                kj = k_ref[pl.ds(start, tq), lanes[g]]
                vj = v_ref[pl.ds(start, tq), lanes[g]]
                n, l = _neg_log_sig(_dot_bt(qsts[g], kj))
                if masked:
                    n = jnp.where(causal, n, 0.0)
                a = jnp.exp(l - (tots[g] - cm_in - _running_sums(n, tri_le)))
                if masked:
                    a = jnp.where(causal, a, 0.0)
                gm = a * _dot_bt(dosts[g], vj)
                pp = cg_in + _dot(gm.astype(BF), tri_lt)
                dz = gm - jnp.exp(l) * (gm + pp)
                if masked:
                    dz = jnp.where(causal, dz, 0.0)
                dzb = dz.astype(BF)
                dk_acc[g, :, pl.ds(start, tq)] += _dot(qsts_t[g], dzb)
                dv_acc[g, :, pl.ds(start, tq)] += _dot(dosts_t[g], a.astype(BF))
                return (_dot(dzb, kj), cm_in + jnp.sum(n, axis=-1, keepdims=True),
                        cg_in + jnp.sum(gm, axis=-1, keepdims=True))

            def step(j, carry):
                for g in range(gp):
                    dq, cm[g], cg[g] = block(g, j, False, cm[g], cg[g])
                    dq_acc[g] += dq
                return carry

            walked = jnp.clip(nblk_ref[pl.program_id(0), pl.program_id(1), i].astype(jnp.int32),
                              jnp.minimum(i, 1), i)
            lax.fori_loop(i - walked, i - 1, step, 0)

            @pl.when(i == 0)
            def _():
                for g in range(gp):
                    dq_acc[g] = block(g, 0, True, cm[g], cg[g])[0]

            @pl.when(i > 0)
            def _():
                prev = [block(g, i - 1, False, cm[g], cg[g]) for g in range(gp)]
                diag = [block(g, i, True, prev[g][1], prev[g][2]) for g in range(gp)]
                for g in range(gp):
                    dq_acc[g] += prev[g][0] + diag[g][0]

            for g in range(gp):
                dq = jnp.where(h0, dq_acc[g, 0:tq, :], dq_acc[g, tq:2 * tq, :])
                dq_ref[rows, lanes[g]] = (dq * SB_SCALE).astype(BF)

        for u in range(per):
            query_block(pl.program_id(2) * per + u, slice(u * tq, (u + 1) * tq))

        @pl.when(pl.program_id(2) == ns - 1)
        def _():
            for g in range(gp):
                dk_ref[:, lanes[g]] = dk_acc[g].T.astype(BF)
                dv_ref[:, lanes[g]] = dv_acc[g].T.astype(BF)

    blk = lambda col0: pl.BlockSpec((per * tq, w), lambda b, hg_, i: (b * ns + i, col0 + hg_))
    seq = lambda col0: pl.BlockSpec((s, w), lambda b, hg_, i: (b, col0 + hg_))
    first = 1024 // w
    (dq, dk, dv, dhg), rode = _ride_call(
        body, "sb_bwd", (nb, ng, ns),
        in_specs=[blk(first), seq(first + ng), seq(first + 2 * ng), blk(0), blk(0),
                  pl.BlockSpec(memory_space=pltpu.SMEM), blk(ng),
                  pl.BlockSpec((1, w), lambda b, hg_, i: (0, ng + hg_)), ANY],
        out_specs=[blk(first), seq(0), seq(0), pl.BlockSpec((None, 1, w), lambda b, hg_, i: (b, 0, hg_))],
        out_shape=[jax.ShapeDtypeStruct((t, IN_COLS), BF), jax.ShapeDtypeStruct((t, 512), BF),
                   jax.ShapeDtypeStruct((t, 512), BF), jax.ShapeDtypeStruct((nb, 1, 512), F32)],
        scratch_shapes=[pltpu.VMEM((gp, 128, s), F32), pltpu.VMEM((gp, 128, s), F32),
                        pltpu.VMEM((gp, 2 * tq, 128), F32), pltpu.VMEM((gp, 2 * tq, 1), F32),
                        pltpu.VMEM((gp, 2 * tq, 1), F32)],
        operands=(proj, proj, proj, o_sb, tot, nblk.reshape(nb, ng, nq, 8, 128)[:, :, :, 0, 0], dmerged, hg, dproj),
        vmem_mb=48, ride=ride, aliases={8: 0})
    return dq, dk, dv, dhg, rode


def _place(buf, piece, col_block, name):
    t, w = piece.shape
    tm = min(t, 1024)

    def body(piece_ref, buf_ref, out_ref):
        del buf_ref
        out_ref[...] = piece_ref[...]

    return pl.pallas_call(
        body, name=name, grid=(t // tm,),
        in_specs=[pl.BlockSpec((tm, w), lambda i: (i, 0)), ANY],
        out_specs=pl.BlockSpec((tm, w), lambda i: (i, col_block)),
        out_shape=jax.ShapeDtypeStruct(buf.shape, buf.dtype), input_output_aliases={1: 0},
        compiler_params=_params(16, ("arbitrary",)),
    )(piece, buf)


def _softmax_rows(sc):
    e = jnp.exp(sc - jnp.max(sc, axis=-1, keepdims=True))
    return e / jnp.sum(e, axis=-1, keepdims=True)


def _mix_cross_fwd(x, merged, w_out, gc, w_cq, kv, w_co, s, tm):
    t, d = x.shape
    tm = min(tm, s)
    per = s // tm
    inv = 1.0 / math.sqrt(X_HEAD_DIM)

    def body(x_ref, m_ref, wo_ref, gc_ref, wq_ref, kv_ref, wc_ref, h1_ref, h2_ref, hn_ref, qc_ref, oc_ref):
        h1 = x_ref[...] + _dot(m_ref[...], wo_ref[...])
        h1_ref[...] = h1
        hn = (h1 * _rs(h1) * gc_ref[...]).astype(BF)
        hn_ref[...] = hn
        qc = _dot(hn, wq_ref[...]).astype(BF)
        qc_ref[...] = qc
        for h in range(X_HEADS):
            cols = slice(h * X_HEAD_DIM, (h + 1) * X_HEAD_DIM)
            kh = kv_ref[:, h * X_HEAD_DIM:(h + 1) * X_HEAD_DIM]
            vh = kv_ref[:, d + h * X_HEAD_DIM:d + (h + 1) * X_HEAD_DIM]
            p = _softmax_rows(_dot_bt(qc[:, cols], kh) * inv)
            oc_ref[:, cols] = _dot(p.astype(BF), vh).astype(BF)
        h2_ref[...] = h1 + _dot(oc_ref[...], wc_ref[...])

    row = lambda width: pl.BlockSpec((tm, width), lambda i: (i, 0))
    full = lambda a, b: pl.BlockSpec((a, b), lambda i: (0, 0))
    return pl.pallas_call(
        body, name="mix_cross_fwd", grid=(t // tm,),
        in_specs=[row(d), row(d), full(d, d), full(1, d), full(d, d),
                  pl.BlockSpec((N_MEM, 2 * d), lambda i: (i // per, 0)), full(d, d)],
        out_specs=[row(d), row(d), row(d), row(d), row(d)],
        out_shape=[jax.ShapeDtypeStruct((t, d), F32), jax.ShapeDtypeStruct((t, d), F32),
                   jax.ShapeDtypeStruct((t, d), BF), jax.ShapeDtypeStruct((t, d), BF),
                   jax.ShapeDtypeStruct((t, d), BF)],
        compiler_params=_params(48, ("arbitrary",)),
    )(*_in_hbm(x, merged, w_out, gc, w_cq, kv, w_co))


def _cross_bwd(dh2, h1, qc, gc, w_cq, kv, w_co, s, tm):
    t, d = dh2.shape
    tm = min(tm, s)
    per = s // tm
    nb = t // s
    inv = 1.0 / math.sqrt(X_HEAD_DIM)

    def body(dh2_ref, h1_ref, qc_ref, gc_ref, wq_ref, kv_ref, wc_ref, dh1_ref, dqc_ref, dkv_ref, dgc_ref):
        i = pl.program_id(0)

        @pl.when(i == 0)
        def _():
            dgc_ref[...] = jnp.zeros_like(dgc_ref)

        @pl.when(i % per == 0)
        def _():
            dkv_ref[...] = jnp.zeros_like(dkv_ref)

        dh2 = dh2_ref[...]
        h1 = h1_ref[...]
        r = _rs(h1)
        h1h = h1 * r
        gcv = gc_ref[...]
        qc = qc_ref[...]
        do = _dot_bt(dh2.astype(BF), wc_ref[...]).astype(BF)
        for h in range(X_HEADS):
            cols = slice(h * X_HEAD_DIM, (h + 1) * X_HEAD_DIM)
            vcols = slice(d + h * X_HEAD_DIM, d + (h + 1) * X_HEAD_DIM)
            kh = kv_ref[:, cols]
            vh = kv_ref[:, vcols]
            p = _softmax_rows(_dot_bt(qc[:, cols], kh) * inv)
            dp = _dot_bt(do[:, cols], vh)
            ds = (p * (dp - jnp.sum(dp * p, axis=-1, keepdims=True)) * inv).astype(BF)
            dqc_ref[:, cols] = _dot(ds, kh).astype(BF)
            dkv_ref[:, cols] += _dot_at(ds, qc[:, cols])
            dkv_ref[:, vcols] += _dot_at(p.astype(BF), do[:, cols])
        dhn = _dot_bt(dqc_ref[...], wq_ref[...])
        dx, dg = _rms_bwd(dhn, h1h, r, gcv)
        dh1_ref[...] = dh2 + dx
        dgc_ref[...] += jnp.sum(dg, axis=0, keepdims=True)

    row = lambda width: pl.BlockSpec((tm, width), lambda i: (i, 0))
    full = lambda a, b: pl.BlockSpec((a, b), lambda i: (0, 0))
    kvspec = pl.BlockSpec((N_MEM, 2 * d), lambda i: (i // per, 0))
    return pl.pallas_call(
        body, name="cross_bwd", grid=(t // tm,),
        in_specs=[row(d), row(d), row(d), full(1, d), full(d, d), kvspec, full(d, d)],
        out_specs=[row(d), row(d), kvspec, full(1, d)],
        out_shape=[jax.ShapeDtypeStruct((t, d), F32), jax.ShapeDtypeStruct((t, d), BF),
                   jax.ShapeDtypeStruct((nb * N_MEM, 2 * d), F32), jax.ShapeDtypeStruct((1, d), F32)],
        compiler_params=_params(48, ("arbitrary",)),
    )(*_in_hbm(dh2, h1, qc, gc, w_cq, kv, w_co))


def _mem_bwd(mem, gm, dkv, w_ckv, tm):
    t, d = mem.shape
    tm = min(tm, t)

    def body(mem_ref, dkv_ref, w_ref, dg_ref):
        @pl.when(pl.program_id(0) == 0)
        def _():
            dg_ref[...] = jnp.zeros_like(dg_ref)

        mv = mem_ref[...]
        dmn = _dot_bt(dkv_ref[...].astype(BF), w_ref[...])
        dg_ref[...] += jnp.sum(dmn * (mv * _rs(mv)), axis=0, keepdims=True)

    del gm
    return pl.pallas_call(
        body, name="mem_bwd", grid=(t // tm,),
        in_specs=[pl.BlockSpec((tm, d), lambda i: (i, 0)), pl.BlockSpec((tm, 2 * d), lambda i: (i, 0)),
                  pl.BlockSpec((d, 2 * d), lambda i: (0, 0))],
        out_specs=pl.BlockSpec((1, d), lambda i: (0, 0)),
        out_shape=jax.ShapeDtypeStruct((1, d), F32),
        compiler_params=_params(32, ("arbitrary",)),
    )(mem, dkv, w_ckv)


def _ffn_loss_fwd(h2, gf, w1, w2, gl, target, tm):
    t, d = h2.shape
    tm = min(tm, t)

    def body(h2_ref, gf_ref, w1_ref, w2_ref, gl_ref, tg_ref, hn_ref, f_ref, dh3_ref, dgl_ref, loss_ref):
        @pl.when(pl.program_id(0) == 0)
        def _():
            dgl_ref[...] = jnp.zeros_like(dgl_ref)
            loss_ref[...] = jnp.zeros_like(loss_ref)

        h2 = h2_ref[...]
        hn = (h2 * _rs(h2) * gf_ref[...]).astype(BF)
        hn_ref[...] = hn
        h3 = h2
        for c in range(4):
            f = jnp.maximum(_dot(hn, w1_ref[c]), 0.0)
            f_ref[:, c * 1024:(c + 1) * 1024] = f.astype(BF)
            h3 = h3 + _dot((f * f).astype(BF), w2_ref[c])
        r3 = _rs(h3)
        yh = h3 * r3
        glv = gl_ref[...]
        e = yh * glv - tg_ref[...]
        loss_ref[...] += 0.5 * jnp.sum(jnp.sum(e * e, axis=-1, keepdims=True) * (1.0 / d), axis=0, keepdims=True)
        dy = e * (1.0 / d)
        dx, dg = _rms_bwd(dy, yh, r3, glv)
        dh3_ref[...] = dx
        dgl_ref[...] += jnp.sum(dg, axis=0, keepdims=True)

    row = lambda width: pl.BlockSpec((tm, width), lambda i: (i, 0))
    return pl.pallas_call(
        body, name="ffn_loss_fwd", grid=(t // tm,),
        in_specs=[row(d), pl.BlockSpec((1, d), lambda i: (0, 0)), pl.BlockSpec((4, d, 1024), lambda i: (0, 0, 0), pipeline_mode=pl.Buffered(1)),
                  pl.BlockSpec((4, 1024, d), lambda i: (0, 0, 0), pipeline_mode=pl.Buffered(1)),
                  pl.BlockSpec((1, d), lambda i: (0, 0)), row(d)],
        out_specs=[row(d), row(D_FF), row(d), pl.BlockSpec((1, d), lambda i: (0, 0)),
                   pl.BlockSpec((1, 1), lambda i: (0, 0))],
        out_shape=[jax.ShapeDtypeStruct((t, d), BF), jax.ShapeDtypeStruct((t, D_FF), BF),
                   jax.ShapeDtypeStruct((t, d), F32), jax.ShapeDtypeStruct((1, d), F32),
                   jax.ShapeDtypeStruct((1, 1), F32)],
        compiler_params=_params(56, ("arbitrary",)),
    )(*_in_hbm(h2, gf, w1, w2, gl, target))


def _ffn_bwd(dh3, f, h2, gf, w1, w2, tm):
    t, d = h2.shape
    tm = min(tm, t)

    def body(dh3_ref, f_ref, h2_ref, gf_ref, w1_ref, w2_ref, dh2_ref, dpre_ref, dgf_ref):
        @pl.when(pl.program_id(0) == 0)
        def _():
            dgf_ref[...] = jnp.zeros_like(dgf_ref)

        dh3 = dh3_ref[...]
        dh3b = dh3.astype(BF)
        dhn = jnp.zeros((tm, d), F32)
        for c in range(4):
            cols = slice(c * 1024, (c + 1) * 1024)
            dpre = (_dot_bt(dh3b, w2_ref[c]) * (2.0 * f_ref[:, cols].astype(F32))).astype(BF)
            dpre_ref[:, cols] = dpre
            dhn = dhn + _dot_bt(dpre, w1_ref[c])
        h2 = h2_ref[...]
        r = _rs(h2)
        dx, dg = _rms_bwd(dhn, h2 * r, r, gf_ref[...])
        dh2_ref[...] = dh3 + dx
        dgf_ref[...] += jnp.sum(dg, axis=0, keepdims=True)

    row = lambda width: pl.BlockSpec((tm, width), lambda i: (i, 0))
    return pl.pallas_call(
        body, name="ffn_bwd", grid=(t // tm,),
        in_specs=[row(d), row(D_FF), row(d), pl.BlockSpec((1, d), lambda i: (0, 0)),
                  pl.BlockSpec((4, d, 1024), lambda i: (0, 0, 0), pipeline_mode=pl.Buffered(1)),
                  pl.BlockSpec((4, 1024, d), lambda i: (0, 0, 0), pipeline_mode=pl.Buffered(1))],
        out_specs=[row(d), row(D_FF), pl.BlockSpec((1, d), lambda i: (0, 0))],
        out_shape=[jax.ShapeDtypeStruct((t, d), F32), jax.ShapeDtypeStruct((t, D_FF), BF),
                   jax.ShapeDtypeStruct((1, d), F32)],
        compiler_params=_params(56, ("arbitrary",)),
    )(*_in_hbm(dh3, f, h2, gf, w1, w2))


def _in_bwd(dproj, dh1, x, g, w_in, tm, ride=None):
    t, d = x.shape
    n = w_in.shape[1]
    tm = min(tm, t)

    def body(dp_ref, dh1_ref, x_ref, g_ref, w_ref, dx_ref, dg_ref):
        @pl.when(pl.program_id(0) == 0)
        def _():
            dg_ref[...] = jnp.zeros_like(dg_ref)

        dxn = _dot_bt(dp_ref[...], w_ref[...])
        xv = x_ref[...]
        r = _rs(xv)
        dx, dg = _rms_bwd(dxn, xv * r, r, g_ref[...])
        dx_ref[...] = dh1_ref[...] + dx
        dg_ref[...] += jnp.sum(dg, axis=0, keepdims=True)

    row = lambda width: pl.BlockSpec((tm, width), lambda i: (i, 0))
    (dx, dg), rode = _ride_call(
        body, "in_bwd", (t // tm,),
        in_specs=[row(n), row(d), row(d), pl.BlockSpec((1, d), lambda i: (0, 0)),
                  pl.BlockSpec((d, n), lambda i: (0, 0))],
        out_specs=[row(d), pl.BlockSpec((1, d), lambda i: (0, 0))],
        out_shape=[jax.ShapeDtypeStruct((t, d), F32), jax.ShapeDtypeStruct((1, d), F32)],
        scratch_shapes=[], operands=(dproj, dh1, x, g, w_in), vmem_mb=48, ride=ride)
    return dx, dg, rode


class _GradReduce:
    def __init__(self, c_idx):
        self.c_idx = c_idx
        self.sums = {}

    def sibling(self, slabs):
        return _SiblingExchange(slabs)

    def chip(self, names, slabs, recv):
        for k, a, r in zip(names, slabs, recv):
            self.sums[k] = _chip_sum(a, r, self.c_idx, "chip_sum_" + k)
        return _ChipExchange([self.sums[k] for k in names])


def _full_weights(gathered):
    d = D_MODEL
    out = {}
    for k, a in gathered.items():
        if k in ("w_in", "w_ckv", "w_ff1"):
            out[k] = a.transpose(1, 0, 2).reshape(d, -1)
        else:
            out[k] = a.reshape(-1, d)
    return out


def _slabs(a):
    return a.reshape(N_DEV, -1, a.shape[-1])


def _local_step(x, mem, target, small, big, nb, s, tq=256, gather_rest=None, reduce=None):
    d = D_MODEL
    g_mix, g_v, w_sp, b_sp, g_head, g_cross, g_mem, g_ffn, g_fin = (
        small[k] for k in ("norm_mix_g", "gm_v_norm_g", "w_spatial", "b_spatial", "head_norm_g", "norm_cross_g",
                           "norm_mem_g", "norm_ffn_g", "norm_final_g"))
    tri = jnp.tril(jnp.ones((CHUNK, CHUNK), dtype=bool))
    w_sp_m = jnp.where(tri[None], w_sp, 0.0)
    wt = w_sp_m.astype(BF)
    wtt = jnp.swapaxes(w_sp_m, 1, 2).astype(BF)
    bb = jnp.broadcast_to(b_sp[:, :, None], (GM_GROUPS, CHUNK, CHUNK))
    hg_a = g_head[:, :GM_WIDTH]

    proj, xn = _norm_matmul(x, g_mix, big["w_in"], 1024, "in_proj")
    merged = _gmlp_fwd(proj, g_v, wt, bb, hg_a, 512)
    o_sb, tot, merged, nblk, gathered = _sb_fwd(proj, merged, g_head, nb, s, tq, ride=gather_rest)
    if gather_rest is not None:
        big = dict(big, **_full_weights(dict(zip(BIG[1:], gathered))))
    w1c = big["w_ff1"].reshape(d, 4, 1024).transpose(1, 0, 2)
    w2c = big["w_ff2"].reshape(4, 1024, d)
    kv, memn = _norm_matmul(mem, g_mem, big["w_ckv"], 512, "mem_proj")
    h1, h2, hn, qc, oc = _mix_cross_fwd(x, merged, big["w_out"], g_cross, big["w_cq"], kv, big["w_co"], s, 512)
    hn2, f, dh3, d_fin, loss = _ffn_loss_fwd(h2, g_ffn, w1c, w2c, g_fin, target, 512)

    gbig = {}
    dh2, dpre, d_ffn = _ffn_bwd(dh3, f, h2, g_ffn, w1c, w2c, 512)
    gbig["w_ff2"] = _slabs(_wgrad(f, dh3, 1024, 512, "wgrad_ff2", square_a=True, tm=2048))
    gbig["w_ff1"] = _slabs(_wgrad_wide(hn2, dpre, 1024, 512, "wgrad_ff1", col_shards=4))
    dh1, dqc, dkv, d_cross = _cross_bwd(dh2, h1, qc, g_cross, big["w_cq"], kv, big["w_co"], s, 512)
    gbig["w_co"] = _slabs(_wgrad(oc, dh2, 1024, 1024, "wgrad_co"))
    gbig["w_cq"] = _slabs(_wgrad(hn, dqc, 1024, 1024, "wgrad_cq"))
    gbig["w_ckv"] = _slabs(_wgrad(memn, dkv, 512, 1024, "wgrad_ckv", col_shards=4))
    d_mem = _mem_bwd(mem, g_mem, dkv, big["w_ckv"], 512)
    dmerged = _matmul_bt(dh1, big["w_out"], 1024, "out_bwd")
    gbig["w_out"] = _slabs(_wgrad(merged, dh1, 1024, 1024, "wgrad_out"))
    rest = BIG[1:]
    ride = reduce.sibling([gbig[k] for k in rest]) if reduce else None
    dproj, d_wsp, d_bb, d_gv, d_hga, recv = _gmlp_bwd(proj, dmerged, g_v, wt, wtt, bb, hg_a, 512, ride=ride)
    ride = reduce.chip(rest, [gbig[k] for k in rest], recv) if reduce else None
    dproj, dk, dv, d_hgb, parts_rest = _sb_bwd(proj, o_sb, tot, nblk, dmerged, dproj, g_head, nb, s, tq, ride=ride)
    dproj = _place(_place(dproj, dk, 3, "place_dk"), dv, 4, "place_dv")
    gbig["w_in"] = _slabs(_wgrad_wide(xn, dproj, 1024, 512, "wgrad_in", col_shards=4))
    last = None
    if reduce:
        recv = _run_exchange(reduce.sibling([gbig["w_in"]]), "grad_sibling_exchange_w_in")
        last = reduce.chip(["w_in"], [gbig["w_in"]], recv)
    grad_x, d_mix, _ = _in_bwd(dproj, dh1, x, g_mix, big["w_in"], 512)
    parts = dict(zip(rest, parts_rest))

    gsmall = {
        "norm_mix_g": d_mix, "gm_v_norm_g": d_gv, "w_spatial": d_wsp, "b_spatial": d_bb[:, :, 0],
        "head_norm_g": jnp.concatenate([d_hga, jnp.sum(d_hgb, axis=0)], axis=1), "norm_cross_g": d_cross,
        "norm_mem_g": d_mem, "norm_ffn_g": d_ffn, "norm_final_g": d_fin,
    }
    return loss, grad_x, gsmall, gbig, parts, last


BIG = ("w_in", "w_out", "w_cq", "w_ckv", "w_co", "w_ff1", "w_ff2")
SMALL = ("norm_mix_g", "gm_v_norm_g", "w_spatial", "b_spatial", "head_norm_g", "norm_cross_g", "norm_mem_g",
         "norm_ffn_g", "norm_final_g")


def _local_copies_start(srcs, stages, sems):
    loads = [pltpu.make_async_copy(src, stage, sems.at[w]) for w, (src, stage) in enumerate(zip(srcs, stages))]
    for ld in loads:
        ld.start()
    return loads


def _local_copies_finish(loads, stages, dsts, sems):
    stores = []
    for w, (ld, stage, dst) in enumerate(zip(loads, stages, dsts)):
        ld.wait()
        st = pltpu.make_async_copy(stage, dst, sems.at[w])
        st.start()
        stores.append(st)
    for st in stores:
        st.wait()


def _chip_sum(slabs, recv, c_idx, name):
    _, r, cw = slabs.shape
    tr = min(r, 256)

    def body(c_ref, a_ref, b_ref, o_ref):
        del c_ref
        o_ref[...] = (a_ref[...] + b_ref[...]).astype(BF)

    return pl.pallas_call(
        body, name=name,
        grid_spec=pltpu.PrefetchScalarGridSpec(
            num_scalar_prefetch=1, grid=(N_CHIPS, r // tr),
            in_specs=[pl.BlockSpec((None, tr, cw), lambda p, i, c_ref: (2 * p + c_ref[0], i, 0)),
                      pl.BlockSpec((None, tr, cw), lambda p, i, c_ref: (p, i, 0))],
            out_specs=pl.BlockSpec((None, tr, cw), lambda p, i, c_ref: (p, i, 0))),
        out_shape=jax.ShapeDtypeStruct((N_CHIPS, r, cw), BF),
        compiler_params=_params(32, ("arbitrary", "arbitrary")),
    )(c_idx, *_in_hbm(slabs, recv))


def _sum4(sums, parts, q_idx, name):
    _, r, cw = parts.shape
    tr = min(r, 256)

    def body(q_ref, own_ref, a_ref, b_ref, c_ref, o_ref):
        del q_ref
        o_ref[...] = ((own_ref[...].astype(F32) + a_ref[...].astype(F32)) + b_ref[...].astype(F32)) + c_ref[
            ...].astype(F32)

    spec = lambda k: pl.BlockSpec((None, tr, cw), lambda i, q_ref: ((q_ref[0] + k) % N_CHIPS, i, 0))
    return pl.pallas_call(
        body, name=name,
        grid_spec=pltpu.PrefetchScalarGridSpec(
            num_scalar_prefetch=1, grid=(r // tr,), in_specs=[spec(0), spec(1), spec(2), spec(3)],
            out_specs=pl.BlockSpec((tr, cw), lambda i, q_ref: (i, 0))),
        out_shape=jax.ShapeDtypeStruct((r, cw), F32),
        compiler_params=_params(32, ("arbitrary",)),
    )(q_idx, *_in_hbm(sums, parts, parts, parts))


def _half_exchange(halves):
    n = len(halves)

    def body(*refs):
        ins, outs, stages = refs[:n], refs[n:2 * n], refs[2 * n:3 * n]
        send_sems, recv_sems, ld_sems, st_sems = refs[3 * n:]
        x, y, c = lax.axis_index("x"), lax.axis_index("y"), lax.axis_index("c")
        loads = _local_copies_start(ins, stages, ld_sems)
        copies = []
        for w in range(n):
            cp = pltpu.make_async_remote_copy(
                src_ref=ins[w], dst_ref=outs[w].at[c], send_sem=send_sems.at[w], recv_sem=recv_sems.at[w],
                device_id=(x, y, 1 - c), device_id_type=MESH)
            cp.start()
            copies.append(cp)
        _local_copies_finish(loads, stages, [outs[w].at[c] for w in range(n)], st_sems)
        for cp in copies:
            cp.wait()

    return pl.pallas_call(
        body, name="grad_half_exchange",
        in_specs=[ANY] * n, out_specs=[ANY] * n,
        out_shape=[jax.ShapeDtypeStruct((2,) + a.shape, a.dtype) for a in halves],
        scratch_shapes=[pltpu.VMEM(a.shape, a.dtype) for a in halves] + [
            pltpu.SemaphoreType.DMA((n,)), pltpu.SemaphoreType.DMA((n,)),
            pltpu.SemaphoreType.DMA((n,)), pltpu.SemaphoreType.DMA((n,))],
        compiler_params=_params(24),
    )(*halves)


def _small_all_reduce(packed, ride=None):
    rows = packed.shape[0]
    ride = ride or _NoExchange()
    ri, ro = len(ride.in_arrays), len(ride.out_shape)

    def body(*refs):
        in_ref, rins, out_ref, routs = refs[0], refs[1:1 + ri], refs[1 + ri], refs[2 + ri:2 + ri + ro]
        pair, chip_sum, chips, d2d_send, d2d_recv, ici_send, ici_recv = refs[2 + ri + ro:9 + ri + ro]
        rscr = refs[9 + ri + ro:]
        ride.start(rins, routs, rscr)
        x, y, c = lax.axis_index("x"), lax.axis_index("y"), lax.axis_index("c")
        q = 2 * x + y
        pair[c] = in_ref[...]
        swap = pltpu.make_async_remote_copy(
            src_ref=in_ref, dst_ref=pair.at[c], send_sem=d2d_send, recv_sem=d2d_recv,
            device_id=(x, y, 1 - c), device_id_type=MESH)
        swap.start()
        swap.wait()
        both = pair[0] + pair[1]
        chip_sum[...] = both
        chips[q] = both
        copies = [pltpu.make_async_remote_copy(
            src_ref=chip_sum, dst_ref=chips.at[q], send_sem=ici_send.at[k], recv_sem=ici_recv.at[k],
            device_id=(px, py, c), device_id_type=MESH) for k, (px, py) in enumerate(_other_chips(x, y))]
        for cp in copies:
            cp.start()
        for cp in copies:
            cp.wait()
        out_ref[...] = ((chips[0] + chips[1]) + chips[2]) + chips[3]
        ride.finish(rins, routs, rscr)

    vmem = pl.BlockSpec(memory_space=pltpu.VMEM)
    res = pl.pallas_call(
        body, name="small_all_reduce",
        in_specs=[vmem] + [ANY] * ri, out_specs=[vmem] + [ANY] * ro,
        out_shape=[jax.ShapeDtypeStruct(packed.shape, F32)] + list(ride.out_shape),
        scratch_shapes=[pltpu.VMEM((2, rows, 128), F32), pltpu.VMEM((rows, 128), F32),
                        pltpu.VMEM((N_CHIPS, rows, 128), F32), pltpu.SemaphoreType.DMA, pltpu.SemaphoreType.DMA,
                        pltpu.SemaphoreType.DMA((3,)), pltpu.SemaphoreType.DMA((3,))] + list(ride.scratch_shapes),
        compiler_params=_params(16),
    )(packed, *ride.in_arrays)
    return res[0], res[1:]


def _adamw(g, w, m, v, name):
    r, cw = g.shape
    tr = 256 if r % 256 == 0 else r

    def body(g_ref, w_ref, m_ref, v_ref, d_ref, nm_ref, nv_ref):
        gv = g_ref[...]
        nm = ADAM_B1 * m_ref[...] + (1.0 - ADAM_B1) * gv
        nv = ADAM_B2 * v_ref[...] + (1.0 - ADAM_B2) * (gv * gv)
        m_hat = nm / (1.0 - ADAM_B1 ** ADAM_STEP)
        v_hat = nv / (1.0 - ADAM_B2 ** ADAM_STEP)
        d_ref[...] = -ADAM_LR * (m_hat / (jnp.sqrt(v_hat) + ADAM_EPS) + ADAM_WD * w_ref[...])
        nm_ref[...] = nm
        nv_ref[...] = nv

    spec = pl.BlockSpec((tr, cw), lambda i: (i, 0))
    return pl.pallas_call(
        body, name=name, grid=(r // tr,),
        in_specs=[spec] * 4, out_specs=[spec] * 3,
        out_shape=[jax.ShapeDtypeStruct((r, cw), F32)] * 3,
        compiler_params=_params(32, ("arbitrary",)),
    )(*_in_hbm(g, w, m, v))


def _small_params(args):
    small = {k: args[k].reshape(1, -1) for k in SMALL}
    small["w_spatial"] = args["w_spatial"][0]
    small["b_spatial"] = args["b_spatial"][0]
    return small


def _pack(parts, rows):
    flat = jnp.concatenate([p.reshape(-1).astype(F32) for p in parts])
    return jnp.pad(flat, (0, rows * 128 - flat.shape[0])).reshape(rows, 128)


def _unpack(packed, shapes):
    flat = packed.reshape(-1)
    out, off = [], 0
    for shp in shapes:
        size = math.prod(shp)
        out.append(flat[off:off + size].reshape(shp))
        off += size
    return out


def kernel(x, mem, norm_mix_g, w_in, gm_v_norm_g, w_spatial, b_spatial, head_norm_g, w_out, norm_cross_g, norm_mem_g, w_cq, w_ckv, w_co, norm_ffn_g, w_ff1, w_ff2, norm_final_g, loss_target, m_norm_mix_g, m_w_in, m_gm_v_norm_g, m_w_spatial, m_b_spatial, m_head_norm_g, m_w_out, m_norm_cross_g, m_norm_mem_g, m_w_cq, m_w_ckv, m_w_co, m_norm_ffn_g, m_w_ff1, m_w_ff2, m_norm_final_g, v_norm_mix_g, v_w_in, v_gm_v_norm_g, v_w_spatial, v_b_spatial, v_head_norm_g, v_w_out, v_norm_cross_g, v_norm_mem_g, v_w_cq, v_w_ckv, v_w_co, v_norm_ffn_g, v_w_ff1, v_w_ff2, v_norm_final_g):
    args = dict(locals())
    d = D_MODEL
    nb, s, _ = x.shape
    c_idx = lax.axis_index("c").astype(jnp.int32).reshape(1)
    q_idx = (2 * lax.axis_index("x") + lax.axis_index("y")).astype(jnp.int32).reshape(1)
    rest = BIG[1:]

    shards = {k: args[k][0].astype(BF) for k in BIG}
    big = _full_weights({"w_in": _run_exchange(_GatherExchange([shards["w_in"]]), "all_gather_w_in")[0]})
    gather_rest = _GatherExchange([shards[k] for k in rest])

    reduce = _GradReduce(c_idx)
    loss, grad_x, gsmall, _, parts, last = _local_step(
        x.reshape(nb * s, d), mem.reshape(nb * N_MEM, d), loss_target.reshape(nb * s, d), _small_params(args), big,
        nb, s, gather_rest=gather_rest, reduce=reduce)

    shapes = [args[k].shape for k in SMALL]
    n_small = sum(math.prod(sh) for sh in shapes)
    rows = -(-(n_small + 1) // 1024) * 8
    reduced, (parts["w_in"],) = _small_all_reduce(_pack([gsmall[k] for k in SMALL] + [loss], rows), ride=last)
    halves = [_sum4(reduce.sums[k], parts[k], q_idx, "sum4_" + k) for k in BIG]
    both = _half_exchange(halves)

    out = {"grad_x": grad_x.reshape(nb, s, d)}
    for k, g2 in zip(BIG, both):
        shp = args[k].shape
        g = g2.reshape(shp[1], shp[2])
        dl, nm, nv = _adamw(g, args[k][0], args["m_" + k][0], args["v_" + k][0], "adamw_" + k)
        out["grad_" + k], out["delta_" + k], out["new_m_" + k], out["new_v_" + k] = (
            a.reshape(shp) for a in (g, dl, nm, nv))

    dl, nm, nv = _adamw(reduced, _pack([args[k] for k in SMALL], rows), _pack([args["m_" + k] for k in SMALL], rows),
                        _pack([args["v_" + k] for k in SMALL], rows), "adamw_small")
    for name, arr in (("grad_", reduced), ("delta_", dl), ("new_m_", nm), ("new_v_", nv)):
        for k, a in zip(SMALL, _unpack(arr, shapes)):
            out[name + k] = a
    out["loss"] = reduced.reshape(-1)[n_small]

    names = ["norm_mix_g", "w_in", "gm_v_norm_g", "w_spatial", "b_spatial", "head_norm_g", "w_out", "norm_cross_g",
             "norm_mem_g", "w_cq", "w_ckv", "w_co", "norm_ffn_g", "w_ff1", "w_ff2", "norm_final_g"]
    return (out["loss"], out["grad_x"], *[out["grad_" + k] for k in names], *[out["delta_" + k] for k in names],
            *[out["new_m_" + k] for k in names], *[out["new_v_" + k] for k in names])
```

```python
import functools
import math

import jax
import jax.numpy as jnp
from jax import lax
from jax.experimental import pallas as pl
from jax.experimental.pallas import tpu as pltpu

F32 = jnp.float32
BF = jnp.bfloat16

EPS = 1e-6
D_MODEL = 1024
CHUNK = 128
GM_GROUPS = 4
GM_WIDTH = 512
SB_WIDTH = 512
HEAD_LANES = 64
SB_SCALE = 0.125
SB_SKIP = -104.0
X_HEADS = 4
X_HEAD_DIM = 256
N_MEM = 256
D_FF = 4096
IN_COLS = 2560
N_CHIPS = 4
N_DEV = 8

ADAM_LR = 0.001
ADAM_B1 = 0.9
ADAM_B2 = 0.999
ADAM_EPS = 1e-08
ADAM_WD = 0.01
ADAM_STEP = 10

V7X_VMEM_BYTES = 64 * 1024 * 1024
MESH = pl.DeviceIdType.MESH
ANY = pl.BlockSpec(memory_space=pl.ANY)

GELU_C = math.sqrt(2.0 / math.pi)
GELU_A = 0.044715


def _params(vmem_mb, sem=None):
    assert vmem_mb * 1024 * 1024 <= V7X_VMEM_BYTES
    return pltpu.CompilerParams(vmem_limit_bytes=vmem_mb * 1024 * 1024, dimension_semantics=sem)


PIN_MIN_ELEMENTS = 1 << 18


def _in_hbm(*arrays):
    return tuple(pltpu.with_memory_space_constraint(a, pltpu.HBM) if a.size >= PIN_MIN_ELEMENTS else a
                 for a in arrays)


def _dot(a, b):
    return jnp.dot(a, b, preferred_element_type=F32)


def _dot_bt(a, b):
    return lax.dot_general(a, b, (((1,), (1,)), ((), ())), preferred_element_type=F32)


def _dot_at(a, b):
    return lax.dot_general(a, b, (((0,), (0,)), ((), ())), preferred_element_type=F32)


def _gelu(x):
    t = jnp.tanh(GELU_C * (x + GELU_A * x * x * x))
    return 0.5 * x * (1.0 + t)


def _gelu_and_grad(x):
    x2 = x * x
    t = jnp.tanh(GELU_C * (x + GELU_A * x2 * x))
    h = 0.5 * (1.0 + t)
    return x * h, h + 0.5 * x * (1.0 - t * t) * (GELU_C * (1.0 + 3.0 * GELU_A * x2))


def _rs(x):
    return lax.rsqrt(jnp.mean(x * x, axis=-1, keepdims=True) + EPS)


def _rms_bwd(dxn, xhat, r, g):
    dxh = dxn * g
    dx = r * (dxh - xhat * jnp.mean(dxh * xhat, axis=-1, keepdims=True))
    return dx, dxn * xhat


def _norm_matmul(x, g, w, tm, name):
    t, d = x.shape
    n = w.shape[1]
    tm = min(tm, t)

    def body(x_ref, g_ref, w_ref, out_ref, xn_ref):
        xv = x_ref[...]
        xn = (xv * _rs(xv) * g_ref[...]).astype(BF)
        xn_ref[...] = xn
        out_ref[...] = _dot(xn, w_ref[...]).astype(out_ref.dtype)

    return pl.pallas_call(
        body, name=name, grid=(t // tm,),
        in_specs=[pl.BlockSpec((tm, d), lambda i: (i, 0)), pl.BlockSpec((1, d), lambda i: (0, 0)),
                  pl.BlockSpec((d, n), lambda i: (0, 0))],
        out_specs=[pl.BlockSpec((tm, n), lambda i: (i, 0)), pl.BlockSpec((tm, d), lambda i: (i, 0))],
        out_shape=[jax.ShapeDtypeStruct((t, n), BF), jax.ShapeDtypeStruct((t, d), BF)],
        compiler_params=_params(48, ("arbitrary",)),
    )(*_in_hbm(x, g, w))


def _wgrad(a, g, tn, tk, name, square_a=False, col_shards=1, tm=1024):
    t, m = a.shape
    n = g.shape[1]
    tk = min(tk, t)
    tm = min(m, tm)
    ns = n // col_shards
    assert ns % tn == 0 and m % tm == 0
    per = ns // tn
    nk = t // tk

    def body(a_ref, g_ref, o_ref):
        k = pl.program_id(2)

        @pl.when(k == 0)
        def _():
            o_ref[...] = jnp.zeros_like(o_ref)

        av = a_ref[...]
        if square_a:
            af = av.astype(F32)
            av = af * af
        o_ref[...] += _dot_at(av.astype(BF), g_ref[...].astype(BF))

    return pl.pallas_call(
        body, name=name, grid=(m // tm, n // tn, nk),
        in_specs=[pl.BlockSpec((tk, tm), lambda i, j, k: (k, i)), pl.BlockSpec((tk, tn), lambda i, j, k: (k, j))],
        out_specs=pl.BlockSpec((None, tm, tn), lambda i, j, k: (j // per, i, j % per)),
        out_shape=jax.ShapeDtypeStruct((col_shards, m, ns), F32),
        compiler_params=_params(48, ("arbitrary", "arbitrary", "arbitrary")),
    )(*_in_hbm(a, g))


def _wgrad_wide(a, g, tm, tk, name, col_shards):
    t, m = a.shape
    n = g.shape[1]
    tk = min(tk, t)
    tm = min(tm, m)
    ns = n // col_shards

    def body(a_ref, g_ref, o_ref):
        @pl.when(pl.program_id(1) == 0)
        def _():
            o_ref[...] = jnp.zeros_like(o_ref)

        a_t = a_ref[...].astype(BF).T
        for p in range(col_shards):
            o_ref[p] += _dot(a_t, g_ref[:, p * ns:(p + 1) * ns].astype(BF))

    return pl.pallas_call(
        body, name=name, grid=(m // tm, t // tk),
        in_specs=[pl.BlockSpec((tk, tm), lambda i, k: (k, i)), pl.BlockSpec((tk, n), lambda i, k: (k, 0))],
        out_specs=pl.BlockSpec((col_shards, tm, ns), lambda i, k: (0, i, 0)),
        out_shape=jax.ShapeDtypeStruct((col_shards, m, ns), F32),
        compiler_params=_params(48, ("arbitrary", "arbitrary")),
    )(*_in_hbm(a, g))


def _matmul_bt(a, w, tm, name):
    t, n = a.shape
    k = w.shape[0]
    tm = min(tm, t)

    def body(a_ref, w_ref, o_ref):
        o_ref[...] = _dot_bt(a_ref[...].astype(BF), w_ref[...]).astype(o_ref.dtype)

    return pl.pallas_call(
        body, name=name, grid=(t // tm,),
        in_specs=[pl.BlockSpec((tm, n), lambda i: (i, 0)), pl.BlockSpec((k, n), lambda i: (0, 0))],
        out_specs=pl.BlockSpec((tm, k), lambda i: (i, 0)),
        out_shape=jax.ShapeDtypeStruct((t, k), BF),
        compiler_params=_params(32, ("arbitrary",)),
    )(*_in_hbm(a, w))


def _gmlp_fwd(proj, gg, wt, bb, hg, tm):
    t = proj.shape[0]
    tm = min(tm, t)

    def body(u_ref, v_ref, gg_ref, wt_ref, bb_ref, hg_ref, out_ref):
        for cc in range(tm // CHUNK):
            rows = slice(cc * CHUNK, (cc + 1) * CHUNK)
            for g in range(GM_GROUPS):
                cols = slice(g * 128, (g + 1) * 128)
                u = _gelu(u_ref[rows, cols].astype(F32))
                gv = _gelu(v_ref[rows, cols].astype(F32))
                vn = gv * _rs(gv) * gg_ref[:, cols]
                mixed = _dot(wt_ref[g], vn.astype(BF)) + bb_ref[g]
                a = u * mixed
                out_ref[rows, cols] = (a * _rs(a) * hg_ref[:, cols]).astype(BF)

    return pl.pallas_call(
        body, name="gmlp_fwd", grid=(t // tm,),
        in_specs=[pl.BlockSpec((tm, 512), lambda i: (i, 0)), pl.BlockSpec((tm, 512), lambda i: (i, 1)),
                  pl.BlockSpec((1, 512), lambda i: (0, 0)), pl.BlockSpec((4, 128, 128), lambda i: (0, 0, 0)),
                  pl.BlockSpec((4, 128, 128), lambda i: (0, 0, 0)), pl.BlockSpec((1, 512), lambda i: (0, 0))],
        out_specs=pl.BlockSpec((tm, 512), lambda i: (i, 0)),
        out_shape=jax.ShapeDtypeStruct((t, 1024), BF),
        compiler_params=_params(32, ("arbitrary",)),
    )(*_in_hbm(proj, proj, gg, wt, bb, hg))


def _gmlp_bwd(proj, dmerged, gg, wt, wtt, bb, hg, tm, ride=None):
    t = proj.shape[0]
    tm = min(tm, t)
    nsteps = t // tm

    def body(u_ref, v_ref, dm_ref, gg_ref, wt_ref, wtt_ref, bb_ref, hg_ref,
             dp_ref, dw_ref, db_ref, dgg_ref, dhg_ref):
        i = pl.program_id(0)

        @pl.when(i == 0)
        def _():
            dw_ref[...] = jnp.zeros_like(dw_ref)
            db_ref[...] = jnp.zeros_like(db_ref)
            dgg_ref[...] = jnp.zeros_like(dgg_ref)
            dhg_ref[...] = jnp.zeros_like(dhg_ref)

        for cc in range(tm // CHUNK):
            rows = slice(cc * CHUNK, (cc + 1) * CHUNK)
            for g in range(GM_GROUPS):
                cols = slice(g * 128, (g + 1) * 128)
                up = u_ref[rows, cols].astype(F32)
                gp = v_ref[rows, cols].astype(F32)
                u, u_grad = _gelu_and_grad(up)
                gv, gv_grad = _gelu_and_grad(gp)
                rv = _rs(gv)
                gvh = gv * rv
                ggv = gg_ref[:, cols]
                vnb = (gvh * ggv).astype(BF)
                mixed = _dot(wt_ref[g], vnb) + bb_ref[g]
                a = u * mixed
                ra = _rs(a)
                ah = a * ra
                dm = dm_ref[rows, cols].astype(F32)
                dhg_ref[:, cols] += jnp.sum(dm * ah, axis=0, keepdims=True)
                dah = dm * hg_ref[:, cols]
                da = ra * (dah - ah * jnp.mean(dah * ah, axis=-1, keepdims=True))
                du = da * mixed
                dmix = da * u
                db_ref[g] += dmix
                dmb = dmix.astype(BF)
                dw_ref[g] += _dot_bt(dmb, vnb)
                dvn = _dot(wtt_ref[g], dmb)
                dgg_ref[:, cols] += jnp.sum(dvn * gvh, axis=0, keepdims=True)
                dgh = dvn * ggv
                dgv = rv * (dgh - gvh * jnp.mean(dgh * gvh, axis=-1, keepdims=True))
                dp_ref[rows, cols] = (du * u_grad).astype(BF)
                dp_ref[rows, 512 + g * 128:512 + (g + 1) * 128] = (dgv * gv_grad).astype(BF)

        @pl.when(i == nsteps - 1)
        def _():
            r = lax.broadcasted_iota(jnp.int32, (CHUNK, CHUNK), 0)
            c = lax.broadcasted_iota(jnp.int32, (CHUNK, CHUNK), 1)
            for g in range(GM_GROUPS):
                dw_ref[g] = jnp.where(c <= r, dw_ref[g], 0.0)
                db_ref[g] = jnp.broadcast_to(jnp.sum(db_ref[g], axis=-1, keepdims=True), (CHUNK, CHUNK))

    small = lambda shape: pl.BlockSpec(shape, lambda i: (0,) * len(shape))
    res, rode = _ride_call(
        body, "gmlp_bwd", (nsteps,),
        in_specs=[pl.BlockSpec((tm, 512), lambda i: (i, 0)), pl.BlockSpec((tm, 512), lambda i: (i, 1)),
                  pl.BlockSpec((tm, 512), lambda i: (i, 0)), small((1, 512)), small((4, 128, 128)),
                  small((4, 128, 128)), small((4, 128, 128)), small((1, 512))],
        out_specs=[pl.BlockSpec((tm, 1024), lambda i: (i, 0)), small((4, 128, 128)), small((4, 128, 128)),
                   small((1, 512)), small((1, 512))],
        out_shape=[jax.ShapeDtypeStruct((t, IN_COLS), BF), jax.ShapeDtypeStruct((4, 128, 128), F32),
                   jax.ShapeDtypeStruct((4, 128, 128), F32), jax.ShapeDtypeStruct((1, 512), F32),
                   jax.ShapeDtypeStruct((1, 512), F32)],
        scratch_shapes=[], operands=(proj, proj, dmerged, gg, wt, wtt, bb, hg), vmem_mb=32, ride=ride)
    return (*res, rode)


def _other_chips(x, y):
    return ((1 - x, y), (x, 1 - y), (1 - x, 1 - y))


class _GatherExchange:
    def __init__(self, shards):
        n = len(shards)
        self.n = n
        self.in_arrays = list(shards)
        self.out_shape = [jax.ShapeDtypeStruct((N_CHIPS,) + a.shape, a.dtype) for a in shards]
        self.half_rows = [a.shape[0] // 2 for a in shards]
        sems = lambda k: pltpu.SemaphoreType.DMA((k,))
        self.scratch_shapes = [pltpu.VMEM(a.shape, a.dtype) for a in shards] + [
            sems(3 * n), sems(3 * n), sems(3 * n), sems(3 * n), sems(n), sems(n)]

    def _copies(self, ins, outs, scr):
        n = self.n
        stages, (ici_send, ici_recv, d2d_send, d2d_recv, ld_sems, st_sems) = scr[:n], scr[n:]
        x, y, c = lax.axis_index("x"), lax.axis_index("y"), lax.axis_index("c")
        q = 2 * x + y
        loads = [pltpu.make_async_copy(ins[w], stages[w], ld_sems.at[w]) for w in range(n)]
        stores = [pltpu.make_async_copy(stages[w], outs[w].at[q], st_sems.at[w]) for w in range(n)]
        ici, d2d = [], []
        for w in range(n):
            half = pl.ds(c * self.half_rows[w], self.half_rows[w])
            for k, (px, py) in enumerate(_other_chips(x, y)):
                ici.append(pltpu.make_async_remote_copy(
                    src_ref=ins[w].at[half], dst_ref=outs[w].at[q, half], send_sem=ici_send.at[3 * w + k],
                    recv_sem=ici_recv.at[3 * w + k], device_id=(px, py, c), device_id_type=MESH))
                landed = outs[w].at[2 * px + py, half]
                d2d.append(pltpu.make_async_remote_copy(
                    src_ref=landed, dst_ref=landed, send_sem=d2d_send.at[3 * w + k],
                    recv_sem=d2d_recv.at[3 * w + k], device_id=(x, y, 1 - c), device_id_type=MESH))
        return loads, stores, ici, d2d

    def start(self, ins, outs, scr):
        loads, stores, ici, _ = self._copies(ins, outs, scr)
        for cp in loads + ici:
            cp.start()
        for ld, st in zip(loads, stores):
            ld.wait()
            st.start()

    def relay(self, ins, outs, scr):
        _, _, ici, d2d = self._copies(ins, outs, scr)
        for got, fwd in zip(ici, d2d):
            got.wait_recv()
            fwd.start()

    def finish(self, ins, outs, scr):
        _, stores, ici, d2d = self._copies(ins, outs, scr)
        for cp in ici:
            cp.wait_send()
        for cp in d2d + stores:
            cp.wait()


class _SiblingExchange:
    def __init__(self, slabs):
        n = len(slabs)
        self.n = n
        self.in_arrays = list(slabs)
        self.out_shape = [jax.ShapeDtypeStruct((N_CHIPS,) + a.shape[1:], a.dtype) for a in slabs]
        self.scratch_shapes = [pltpu.SemaphoreType.DMA((4 * n,)), pltpu.SemaphoreType.DMA((4 * n,))]

    def _copies(self, ins, outs, scr):
        send_sems, recv_sems = scr
        x, y, c = lax.axis_index("x"), lax.axis_index("y"), lax.axis_index("c")
        return [pltpu.make_async_remote_copy(
            src_ref=ins[w].at[2 * p + (1 - c)], dst_ref=outs[w].at[p], send_sem=send_sems.at[4 * w + p],
            recv_sem=recv_sems.at[4 * w + p], device_id=(x, y, 1 - c), device_id_type=MESH)
            for w in range(self.n) for p in range(N_CHIPS)]

    def start(self, ins, outs, scr):
        for cp in self._copies(ins, outs, scr):
            cp.start()

    def finish(self, ins, outs, scr):
        for cp in self._copies(ins, outs, scr):
            cp.wait()


class _ChipExchange:
    def __init__(self, sums):
        n = len(sums)
        self.n = n
        self.in_arrays = list(sums)
        self.out_shape = [jax.ShapeDtypeStruct(a.shape, a.dtype) for a in sums]
        self.scratch_shapes = [pltpu.SemaphoreType.DMA((3 * n,)), pltpu.SemaphoreType.DMA((3 * n,))]

    def _copies(self, ins, outs, scr):
        send_sems, recv_sems = scr
        x, y, c = lax.axis_index("x"), lax.axis_index("y"), lax.axis_index("c")
        q = 2 * x + y
        return [pltpu.make_async_remote_copy(
            src_ref=ins[w].at[2 * px + py], dst_ref=outs[w].at[q], send_sem=send_sems.at[3 * w + k],
            recv_sem=recv_sems.at[3 * w + k], device_id=(px, py, c), device_id_type=MESH)
            for w in range(self.n) for k, (px, py) in enumerate(_other_chips(x, y))]

    def start(self, ins, outs, scr):
        for cp in self._copies(ins, outs, scr):
            cp.start()

    def finish(self, ins, outs, scr):
        for cp in self._copies(ins, outs, scr):
            cp.wait()


class _NoExchange:
    in_arrays, out_shape, scratch_shapes = (), (), ()

    def start(self, ins, outs, scr):
        pass

    def finish(self, ins, outs, scr):
        pass


def _run_exchange(ex, name):
    n_in, n_out = len(ex.in_arrays), len(ex.out_shape)

    def body(*refs):
        ins, outs, scr = refs[:n_in], refs[n_in:n_in + n_out], refs[n_in + n_out:]
        ex.start(ins, outs, scr)
        if hasattr(ex, "relay"):
            ex.relay(ins, outs, scr)
        ex.finish(ins, outs, scr)

    return pl.pallas_call(
        body, name=name, in_specs=[ANY] * n_in, out_specs=[ANY] * n_out, out_shape=ex.out_shape,
        scratch_shapes=ex.scratch_shapes, compiler_params=_params(24),
    )(*ex.in_arrays)


def _ride_call(body, name, grid, in_specs, out_specs, out_shape, scratch_shapes, operands, vmem_mb, ride=None,
               aliases=None):
    ride = ride or _NoExchange()
    ni, no, ns = len(in_specs), len(out_specs), len(scratch_shapes)
    ri, ro = len(ride.in_arrays), len(ride.out_shape)
    total = math.prod(grid)

    def wrapped(*refs):
        ins, rins = refs[:ni], refs[ni:ni + ri]
        outs, routs = refs[ni + ri:ni + ri + no], refs[ni + ri + no:ni + ri + no + ro]
        scr, rscr = refs[ni + ri + no + ro:ni + ri + no + ro + ns], refs[ni + ri + no + ro + ns:]
        step = pl.program_id(0)
        for ax in range(1, len(grid)):
            step = step * grid[ax] + pl.program_id(ax)

        @pl.when(step == 0)
        def _():
            ride.start(rins, routs, rscr)

        if hasattr(ride, "relay"):
            @pl.when(step == (3 * total) // 4)
            def _():
                ride.relay(rins, routs, rscr)

        body(*ins, *outs, *scr)

        @pl.when(step == total - 1)
        def _():
            ride.finish(rins, routs, rscr)

    res = pl.pallas_call(
        wrapped, name=name, grid=grid, in_specs=list(in_specs) + [ANY] * ri, out_specs=list(out_specs) + [ANY] * ro,
        out_shape=list(out_shape) + list(ride.out_shape),
        scratch_shapes=list(scratch_shapes) + list(ride.scratch_shapes), input_output_aliases=aliases or {},
        compiler_params=_params(vmem_mb, ("arbitrary",) * len(grid)),
    )(*_in_hbm(*operands), *ride.in_arrays)
    return res[:no], res[no:]


def _neg_log_sig(z):
    n = jnp.maximum(z, 0.0) + jnp.log(1.0 + jnp.exp(-jnp.abs(z)))
    return n, z - n


def _running_sums(n, tri2):
    hi = n.astype(BF)
    lo = (n - hi.astype(F32)).astype(BF)
    return _dot(jnp.concatenate([hi, lo], axis=1), tri2)


def _head_sums(x, h0):
    s0 = jnp.sum(jnp.where(h0, x, 0.0), axis=-1, keepdims=True)
    s1 = jnp.sum(jnp.where(h0, 0.0, x), axis=-1, keepdims=True)
    return jnp.where(h0, s0, s1)


SB_BLOCKS_PER_STEP = 4
SB_PAIRS_PER_STEP = 2


def _sb_masks(tq):
    h0 = lax.broadcasted_iota(jnp.int32, (tq, 128), 1) < HEAD_LANES
    r = lax.broadcasted_iota(jnp.int32, (2 * tq, tq), 0)
    c = lax.broadcasted_iota(jnp.int32, (2 * tq, tq), 1)
    return h0, c < jnp.where(r >= tq, r - tq, r)


def _sb_stack(x, h0):
    zero = jnp.zeros_like(x)
    return jnp.concatenate([jnp.where(h0, x, zero), jnp.where(h0, zero, x)], axis=0)


def _tri(tq, op):
    return op(lax.broadcasted_iota(jnp.int32, (tq, tq), 0), lax.broadcasted_iota(jnp.int32, (tq, tq), 1)).astype(BF)


def _sb_fwd(proj, merged, hg, nb, s, tq, ride=None):
    t = nb * s
    tq = min(tq, s)
    nq = s // tq
    per = min(SB_BLOCKS_PER_STEP, nq)
    ns = nq // per
    gp, ng, w = SB_PAIRS_PER_STEP, 4 // SB_PAIRS_PER_STEP, 128 * SB_PAIRS_PER_STEP

    def body(q_ref, k_ref, v_ref, hg_ref, merged_ref, o_ref, tot_ref, mb_ref, nblk_ref, acc, cr, c_min):
        del merged_ref
        h0, causal = _sb_masks(tq)
        tri_gt = _tri(tq, lambda r, c: r > c)
        tri_gt = jnp.concatenate([tri_gt, tri_gt], axis=0)
        lanes = [slice(g * 128, (g + 1) * 128) for g in range(gp)]
        zeros = jnp.zeros((2 * tq, 1), F32)

        def query_block(i, rows):
            qsts = [_sb_stack(q_ref[rows, lanes[g]] * SB_SCALE, h0) for g in range(gp)]

            def block(g, j, masked, c_in):
                start = pl.multiple_of(j * tq, tq)
                kj = k_ref[pl.ds(start, tq), lanes[g]]
                vj = v_ref[pl.ds(start, tq), lanes[g]]
                n, l = _neg_log_sig(_dot_bt(qsts[g], kj))
                if masked:
                    n = jnp.where(causal, n, 0.0)
                a = jnp.exp(l - (_running_sums(n, tri_gt) + c_in))
                if masked:
                    a = jnp.where(causal, a, 0.0)
                return _dot(a.astype(BF), vj), c_in + jnp.sum(n, axis=-1, keepdims=True)

            def keep(parts):
                for g, (p, c) in enumerate(parts):
                    acc[g] = p
                    cr[g] = c
                c_min[0] = jnp.min(functools.reduce(jnp.minimum, [c for _, c in parts]))

            @pl.when(i == 0)
            def _():
                keep([block(g, 0, True, zeros) for g in range(gp)])

            @pl.when(i > 0)
            def _():
                diag = [block(g, i, True, zeros) for g in range(gp)]
                prev = [block(g, i - 1, False, diag[g][1]) for g in range(gp)]
                keep([(diag[g][0] + prev[g][0], prev[g][1]) for g in range(gp)])

            def cond(carry):
                return jnp.logical_and(carry[0] < i, carry[1] < -SB_SKIP)

            def step(carry):
                more = [block(g, i - 1 - carry[0], False, cr[g]) for g in range(gp)]
                for g, (p, c) in enumerate(more):
                    acc[g] += p
                    cr[g] = c
                return carry[0] + 1, jnp.min(functools.reduce(jnp.minimum, [c for _, c in more]))

            walked, _ = lax.while_loop(cond, step, (jnp.minimum(i, 1), c_min[0]))
            return walked

        for u in range(per):
            rows = slice(u * tq, (u + 1) * tq)
            walked = query_block(pl.program_id(2) * per + u, rows)
            for g in range(gp):
                o = jnp.where(h0, acc[g, 0:tq, :], acc[g, tq:2 * tq, :])
                o_ref[rows, lanes[g]] = o
                tot_ref[rows, lanes[g]] = jnp.where(h0, cr[g, 0:tq, :], cr[g, tq:2 * tq, :])
                ro = lax.rsqrt(_head_sums(o * o, h0) * (1.0 / HEAD_LANES) + EPS)
                mb_ref[rows, lanes[g]] = (o * ro * hg_ref[:, lanes[g]]).astype(BF)
            nblk_ref[u * 8:(u + 1) * 8, :] = jnp.full((8, 128), walked.astype(F32))

    blk = lambda col0: pl.BlockSpec((per * tq, w), lambda b, hg_, i: (b * ns + i, col0 + hg_))
    seq = lambda col0: pl.BlockSpec((s, w), lambda b, hg_, i: (b, col0 + hg_))
    first = 1024 // w
    (o, tot, mb, nblk), rode = _ride_call(
        body, "sb_fwd", (nb, ng, ns),
        in_specs=[blk(first), seq(first + ng), seq(first + 2 * ng),
                  pl.BlockSpec((1, w), lambda b, hg_, i: (0, ng + hg_)), ANY],
        out_specs=[blk(0), blk(0), blk(ng),
                   pl.BlockSpec((None, None, per * 8, 128), lambda b, hg_, i: (b, hg_, i, 0))],
        out_shape=[jax.ShapeDtypeStruct((t, 512), F32), jax.ShapeDtypeStruct((t, 512), F32),
                   jax.ShapeDtypeStruct((t, 1024), BF), jax.ShapeDtypeStruct((nb, ng, nq * 8, 128), F32)],
        scratch_shapes=[pltpu.VMEM((gp, 2 * tq, 128), F32), pltpu.VMEM((gp, 2 * tq, 1), F32),
                        pltpu.SMEM((1,), F32)],
        operands=(proj, proj, proj, hg, merged), vmem_mb=40, ride=ride, aliases={4: 2})
    return o, tot, mb, nblk, rode


def _sb_bwd(proj, o_sb, tot, nblk, dmerged, dproj, hg, nb, s, tq, ride=None):
    t = nb * s
    tq = min(tq, s)
    nq = s // tq
    per = min(SB_BLOCKS_PER_STEP, nq)
    ns = nq // per
    gp, ng, w = SB_PAIRS_PER_STEP, 4 // SB_PAIRS_PER_STEP, 128 * SB_PAIRS_PER_STEP

    def body(q_ref, k_ref, v_ref, o_ref, tot_ref, nblk_ref, dm_ref, hg_ref, dproj_ref,
             dq_ref, dk_ref, dv_ref, dhg_ref, dk_acc, dv_acc, dq_acc, cm, cg):
        del dproj_ref
        h0, causal = _sb_masks(tq)
        tri_le = _tri(tq, lambda r, c: r <= c)
        tri_le = jnp.concatenate([tri_le, tri_le], axis=0)
        tri_lt = _tri(tq, lambda r, c: r < c)
        lanes = [slice(g * 128, (g + 1) * 128) for g in range(gp)]

        @pl.when(pl.program_id(2) == 0)
        def _():
            dk_acc[...] = jnp.zeros_like(dk_acc)
            dv_acc[...] = jnp.zeros_like(dv_acc)
            dhg_ref[...] = jnp.zeros_like(dhg_ref)

        def query_block(i, rows):
            for ref in (dq_acc, cm, cg):
                ref[...] = jnp.zeros_like(ref)
            qsts, dosts, tots = [], [], []
            for g in range(gp):
                qsts.append(_sb_stack(q_ref[rows, lanes[g]] * SB_SCALE, h0))
                o = o_ref[rows, lanes[g]]
                ro = lax.rsqrt(_head_sums(o * o, h0) * (1.0 / HEAD_LANES) + EPS)
                oh = o * ro
                dm = dm_ref[rows, lanes[g]].astype(F32)
                dhg_ref[:, lanes[g]] += jnp.sum(dm * oh, axis=0, keepdims=True)
                doh = dm * hg_ref[:, lanes[g]]
                do = ro * (doh - oh * (_head_sums(doh * oh, h0) * (1.0 / HEAD_LANES)))
                dosts.append(_sb_stack(do.astype(BF), h0))
                first = g * 128
                tots.append(jnp.concatenate(
                    [tot_ref[rows, first:first + 1], tot_ref[rows, first + HEAD_LANES:first + HEAD_LANES + 1]], axis=0))
            qsts_t = [q.T for q in qsts]
            dosts_t = [d.T for d in dosts]

            def block(g, j, masked, cm_in, cg_in):
                start = pl.multiple_of(j * tq, tq)
                kj = k_ref[pl.ds(start, tq), lanes[g]]
                vj = v_ref[pl.ds(start, tq), lanes[g]]
                n, l = _neg_log_sig(_dot_bt(qsts[g], kj))
                if masked:
                    n = jnp.where(causal, n, 0.0)
                a = jnp.exp(l - (tots[g] - cm_in - _running_sums(n, tri_le)))
                if masked:
                    a = jnp.where(causal, a, 0.0)
                gm = a * _dot_bt(dosts[g], vj)
                pp = cg_in + _dot(gm.astype(BF), tri_lt)
                dz = gm - jnp.exp(l) * (gm + pp)
                if masked:
                    dz = jnp.where(causal, dz, 0.0)
                dzb = dz.astype(BF)
                dk_acc[g, :, pl.ds(start, tq)] += _dot(qsts_t[g], dzb)
                dv_acc[g, :, pl.ds(start, tq)] += _dot(dosts_t[g], a.astype(BF))
                return (_dot(dzb, kj), cm_in + jnp.sum(n, axis=-1, keepdims=True),
                        cg_in + jnp.sum(gm, axis=-1, keepdims=True))

            def step(j, carry):
                for g in range(gp):
                    dq, cm[g], cg[g] = block(g, j, False, cm[g], cg[g])
                    dq_acc[g] += dq
                return carry

            walked = jnp.clip(nblk_ref[pl.program_id(0), pl.program_id(1), i].astype(jnp.int32),
                              jnp.minimum(i, 1), i)
            lax.fori_loop(i - walked, i - 1, step, 0)

            @pl.when(i == 0)
            def _():
                for g in range(gp):
                    dq_acc[g] = block(g, 0, True, cm[g], cg[g])[0]

            @pl.when(i > 0)
            def _():
                prev = [block(g, i - 1, False, cm[g], cg[g]) for g in range(gp)]
                diag = [block(g, i, True, prev[g][1], prev[g][2]) for g in range(gp)]
                for g in range(gp):
                    dq_acc[g] += prev[g][0] + diag[g][0]

            for g in range(gp):
                dq = jnp.where(h0, dq_acc[g, 0:tq, :], dq_acc[g, tq:2 * tq, :])
                dq_ref[rows, lanes[g]] = (dq * SB_SCALE).astype(BF)

        for u in range(per):
            query_block(pl.program_id(2) * per + u, slice(u * tq, (u + 1) * tq))

        @pl.when(pl.program_id(2) == ns - 1)
        def _():
            for g in range(gp):
                dk_ref[:, lanes[g]] = dk_acc[g].T.astype(BF)
                dv_ref[:, lanes[g]] = dv_acc[g].T.astype(BF)

    blk = lambda col0: pl.BlockSpec((per * tq, w), lambda b, hg_, i: (b * ns + i, col0 + hg_))
    seq = lambda col0: pl.BlockSpec((s, w), lambda b, hg_, i: (b, col0 + hg_))
    first = 1024 // w
    (dq, dk, dv, dhg), rode = _ride_call(
        body, "sb_bwd", (nb, ng, ns),
        in_specs=[blk(first), seq(first + ng), seq(first + 2 * ng), blk(0), blk(0),
                  pl.BlockSpec(memory_space=pltpu.SMEM), blk(ng),
                  pl.BlockSpec((1, w), lambda b, hg_, i: (0, ng + hg_)), ANY],
        out_specs=[blk(first), seq(0), seq(0), pl.BlockSpec((None, 1, w), lambda b, hg_, i: (b, 0, hg_))],
        out_shape=[jax.ShapeDtypeStruct((t, IN_COLS), BF), jax.ShapeDtypeStruct((t, 512), BF),
                   jax.ShapeDtypeStruct((t, 512), BF), jax.ShapeDtypeStruct((nb, 1, 512), F32)],
        scratch_shapes=[pltpu.VMEM((gp, 128, s), F32), pltpu.VMEM((gp, 128, s), F32),
                        pltpu.VMEM((gp, 2 * tq, 128), F32), pltpu.VMEM((gp, 2 * tq, 1), F32),
                        pltpu.VMEM((gp, 2 * tq, 1), F32)],
        operands=(proj, proj, proj, o_sb, tot, nblk.reshape(nb, ng, nq, 8, 128)[:, :, :, 0, 0], dmerged, hg, dproj),
        vmem_mb=48, ride=ride, aliases={8: 0})
    return dq, dk, dv, dhg, rode


def _place(buf, piece, col_block, name):
    t, w = piece.shape
    tm = min(t, 1024)

    def body(piece_ref, buf_ref, out_ref):
        del buf_ref
        out_ref[...] = piece_ref[...]

    return pl.pallas_call(
        body, name=name, grid=(t // tm,),
        in_specs=[pl.BlockSpec((tm, w), lambda i: (i, 0)), ANY],
        out_specs=pl.BlockSpec((tm, w), lambda i: (i, col_block)),
        out_shape=jax.ShapeDtypeStruct(buf.shape, buf.dtype), input_output_aliases={1: 0},
        compiler_params=_params(16, ("arbitrary",)),
    )(piece, buf)


def _softmax_rows(sc):
    e = jnp.exp(sc - jnp.max(sc, axis=-1, keepdims=True))
    return e / jnp.sum(e, axis=-1, keepdims=True)


def _mix_cross_fwd(x, merged, w_out, gc, w_cq, kv, w_co, s, tm):
    t, d = x.shape
    tm = min(tm, s)
    per = s // tm
    inv = 1.0 / math.sqrt(X_HEAD_DIM)

    def body(x_ref, m_ref, wo_ref, gc_ref, wq_ref, kv_ref, wc_ref, h1_ref, h2_ref, hn_ref, qc_ref, oc_ref):
        h1 = x_ref[...] + _dot(m_ref[...], wo_ref[...])
        h1_ref[...] = h1
        hn = (h1 * _rs(h1) * gc_ref[...]).astype(BF)
        hn_ref[...] = hn
        qc = _dot(hn, wq_ref[...]).astype(BF)
        qc_ref[...] = qc
        for h in range(X_HEADS):
            cols = slice(h * X_HEAD_DIM, (h + 1) * X_HEAD_DIM)
            kh = kv_ref[:, h * X_HEAD_DIM:(h + 1) * X_HEAD_DIM]
            vh = kv_ref[:, d + h * X_HEAD_DIM:d + (h + 1) * X_HEAD_DIM]
            p = _softmax_rows(_dot_bt(qc[:, cols], kh) * inv)
            oc_ref[:, cols] = _dot(p.astype(BF), vh).astype(BF)
        h2_ref[...] = h1 + _dot(oc_ref[...], wc_ref[...])

    row = lambda width: pl.BlockSpec((tm, width), lambda i: (i, 0))
    full = lambda a, b: pl.BlockSpec((a, b), lambda i: (0, 0))
    return pl.pallas_call(
        body, name="mix_cross_fwd", grid=(t // tm,),
        in_specs=[row(d), row(d), full(d, d), full(1, d), full(d, d),
                  pl.BlockSpec((N_MEM, 2 * d), lambda i: (i // per, 0)), full(d, d)],
        out_specs=[row(d), row(d), row(d), row(d), row(d)],
        out_shape=[jax.ShapeDtypeStruct((t, d), F32), jax.ShapeDtypeStruct((t, d), F32),
                   jax.ShapeDtypeStruct((t, d), BF), jax.ShapeDtypeStruct((t, d), BF),
                   jax.ShapeDtypeStruct((t, d), BF)],
        compiler_params=_params(48, ("arbitrary",)),
    )(*_in_hbm(x, merged, w_out, gc, w_cq, kv, w_co))


def _cross_bwd(dh2, h1, qc, gc, w_cq, kv, w_co, s, tm):
    t, d = dh2.shape
    tm = min(tm, s)
    per = s // tm
    nb = t // s
    inv = 1.0 / math.sqrt(X_HEAD_DIM)

    def body(dh2_ref, h1_ref, qc_ref, gc_ref, wq_ref, kv_ref, wc_ref, dh1_ref, dqc_ref, dkv_ref, dgc_ref):
        i = pl.program_id(0)

        @pl.when(i == 0)
        def _():
            dgc_ref[...] = jnp.zeros_like(dgc_ref)

        @pl.when(i % per == 0)
        def _():
            dkv_ref[...] = jnp.zeros_like(dkv_ref)

        dh2 = dh2_ref[...]
        h1 = h1_ref[...]
        r = _rs(h1)
        h1h = h1 * r
        gcv = gc_ref[...]
        qc = qc_ref[...]
        do = _dot_bt(dh2.astype(BF), wc_ref[...]).astype(BF)
        for h in range(X_HEADS):
            cols = slice(h * X_HEAD_DIM, (h + 1) * X_HEAD_DIM)
            vcols = slice(d + h * X_HEAD_DIM, d + (h + 1) * X_HEAD_DIM)
            kh = kv_ref[:, cols]
            vh = kv_ref[:, vcols]
            p = _softmax_rows(_dot_bt(qc[:, cols], kh) * inv)
            dp = _dot_bt(do[:, cols], vh)
            ds = (p * (dp - jnp.sum(dp * p, axis=-1, keepdims=True)) * inv).astype(BF)
            dqc_ref[:, cols] = _dot(ds, kh).astype(BF)
            dkv_ref[:, cols] += _dot_at(ds, qc[:, cols])
            dkv_ref[:, vcols] += _dot_at(p.astype(BF), do[:, cols])
        dhn = _dot_bt(dqc_ref[...], wq_ref[...])
        dx, dg = _rms_bwd(dhn, h1h, r, gcv)
        dh1_ref[...] = dh2 + dx
        dgc_ref[...] += jnp.sum(dg, axis=0, keepdims=True)

    row = lambda width: pl.BlockSpec((tm, width), lambda i: (i, 0))
    full = lambda a, b: pl.BlockSpec((a, b), lambda i: (0, 0))
    kvspec = pl.BlockSpec((N_MEM, 2 * d), lambda i: (i // per, 0))
    return pl.pallas_call(
        body, name="cross_bwd", grid=(t // tm,),
        in_specs=[row(d), row(d), row(d), full(1, d), full(d, d), kvspec, full(d, d)],
        out_specs=[row(d), row(d), kvspec, full(1, d)],
        out_shape=[jax.ShapeDtypeStruct((t, d), F32), jax.ShapeDtypeStruct((t, d), BF),
                   jax.ShapeDtypeStruct((nb * N_MEM, 2 * d), F32), jax.ShapeDtypeStruct((1, d), F32)],
        compiler_params=_params(48, ("arbitrary",)),
    )(*_in_hbm(dh2, h1, qc, gc, w_cq, kv, w_co))


def _mem_bwd(mem, gm, dkv, w_ckv, tm):
    t, d = mem.shape
    tm = min(tm, t)

    def body(mem_ref, dkv_ref, w_ref, dg_ref):
        @pl.when(pl.program_id(0) == 0)
        def _():
            dg_ref[...] = jnp.zeros_like(dg_ref)

        mv = mem_ref[...]
        dmn = _dot_bt(dkv_ref[...].astype(BF), w_ref[...])
        dg_ref[...] += jnp.sum(dmn * (mv * _rs(mv)), axis=0, keepdims=True)

    del gm
    return pl.pallas_call(
        body, name="mem_bwd", grid=(t // tm,),
        in_specs=[pl.BlockSpec((tm, d), lambda i: (i, 0)), pl.BlockSpec((tm, 2 * d), lambda i: (i, 0)),
                  pl.BlockSpec((d, 2 * d), lambda i: (0, 0))],
        out_specs=pl.BlockSpec((1, d), lambda i: (0, 0)),
        out_shape=jax.ShapeDtypeStruct((1, d), F32),
        compiler_params=_params(32, ("arbitrary",)),
    )(mem, dkv, w_ckv)


def _ffn_loss_fwd(h2, gf, w1, w2, gl, target, tm):
    t, d = h2.shape
    tm = min(tm, t)

    def body(h2_ref, gf_ref, w1_ref, w2_ref, gl_ref, tg_ref, hn_ref, f_ref, dh3_ref, dgl_ref, loss_ref):
        @pl.when(pl.program_id(0) == 0)
        def _():
            dgl_ref[...] = jnp.zeros_like(dgl_ref)
            loss_ref[...] = jnp.zeros_like(loss_ref)

        h2 = h2_ref[...]
        hn = (h2 * _rs(h2) * gf_ref[...]).astype(BF)
        hn_ref[...] = hn
        h3 = h2
        for c in range(4):
            f = jnp.maximum(_dot(hn, w1_ref[c]), 0.0)
            f_ref[:, c * 1024:(c + 1) * 1024] = f.astype(BF)
            h3 = h3 + _dot((f * f).astype(BF), w2_ref[c])
        r3 = _rs(h3)
        yh = h3 * r3
        glv = gl_ref[...]
        e = yh * glv - tg_ref[...]
        loss_ref[...] += 0.5 * jnp.sum(jnp.sum(e * e, axis=-1, keepdims=True) * (1.0 / d), axis=0, keepdims=True)
        dy = e * (1.0 / d)
        dx, dg = _rms_bwd(dy, yh, r3, glv)
        dh3_ref[...] = dx
        dgl_ref[...] += jnp.sum(dg, axis=0, keepdims=True)

    row = lambda width: pl.BlockSpec((tm, width), lambda i: (i, 0))
    return pl.pallas_call(
        body, name="ffn_loss_fwd", grid=(t // tm,),
        in_specs=[row(d), pl.BlockSpec((1, d), lambda i: (0, 0)), pl.BlockSpec((4, d, 1024), lambda i: (0, 0, 0), pipeline_mode=pl.Buffered(1)),
                  pl.BlockSpec((4, 1024, d), lambda i: (0, 0, 0), pipeline_mode=pl.Buffered(1)),
                  pl.BlockSpec((1, d), lambda i: (0, 0)), row(d)],
        out_specs=[row(d), row(D_FF), row(d), pl.BlockSpec((1, d), lambda i: (0, 0)),
                   pl.BlockSpec((1, 1), lambda i: (0, 0))],
        out_shape=[jax.ShapeDtypeStruct((t, d), BF), jax.ShapeDtypeStruct((t, D_FF), BF),
                   jax.ShapeDtypeStruct((t, d), F32), jax.ShapeDtypeStruct((1, d), F32),
                   jax.ShapeDtypeStruct((1, 1), F32)],
        compiler_params=_params(56, ("arbitrary",)),
    )(*_in_hbm(h2, gf, w1, w2, gl, target))


def _ffn_bwd(dh3, f, h2, gf, w1, w2, tm):
    t, d = h2.shape
    tm = min(tm, t)

    def body(dh3_ref, f_ref, h2_ref, gf_ref, w1_ref, w2_ref, dh2_ref, dpre_ref, dgf_ref):
        @pl.when(pl.program_id(0) == 0)
        def _():
            dgf_ref[...] = jnp.zeros_like(dgf_ref)

        dh3 = dh3_ref[...]
        dh3b = dh3.astype(BF)
        dhn = jnp.zeros((tm, d), F32)
        for c in range(4):
            cols = slice(c * 1024, (c + 1) * 1024)
            dpre = (_dot_bt(dh3b, w2_ref[c]) * (2.0 * f_ref[:, cols].astype(F32))).astype(BF)
            dpre_ref[:, cols] = dpre
            dhn = dhn + _dot_bt(dpre, w1_ref[c])
        h2 = h2_ref[...]
        r = _rs(h2)
        dx, dg = _rms_bwd(dhn, h2 * r, r, gf_ref[...])
        dh2_ref[...] = dh3 + dx
        dgf_ref[...] += jnp.sum(dg, axis=0, keepdims=True)

    row = lambda width: pl.BlockSpec((tm, width), lambda i: (i, 0))
    return pl.pallas_call(
        body, name="ffn_bwd", grid=(t // tm,),
        in_specs=[row(d), row(D_FF), row(d), pl.BlockSpec((1, d), lambda i: (0, 0)),
                  pl.BlockSpec((4, d, 1024), lambda i: (0, 0, 0), pipeline_mode=pl.Buffered(1)),
                  pl.BlockSpec((4, 1024, d), lambda i: (0, 0, 0), pipeline_mode=pl.Buffered(1))],
        out_specs=[row(d), row(D_FF), pl.BlockSpec((1, d), lambda i: (0, 0))],
        out_shape=[jax.ShapeDtypeStruct((t, d), F32), jax.ShapeDtypeStruct((t, D_FF), BF),
                   jax.ShapeDtypeStruct((1, d), F32)],
        compiler_params=_params(56, ("arbitrary",)),
    )(*_in_hbm(dh3, f, h2, gf, w1, w2))


def _in_bwd(dproj, dh1, x, g, w_in, tm, ride=None):
    t, d = x.shape
    n = w_in.shape[1]
    tm = min(tm, t)

    def body(dp_ref, dh1_ref, x_ref, g_ref, w_ref, dx_ref, dg_ref):
        @pl.when(pl.program_id(0) == 0)
        def _():
            dg_ref[...] = jnp.zeros_like(dg_ref)

        dxn = _dot_bt(dp_ref[...], w_ref[...])
        xv = x_ref[...]
        r = _rs(xv)
        dx, dg = _rms_bwd(dxn, xv * r, r, g_ref[...])
        dx_ref[...] = dh1_ref[...] + dx
        dg_ref[...] += jnp.sum(dg, axis=0, keepdims=True)

    row = lambda width: pl.BlockSpec((tm, width), lambda i: (i, 0))
    (dx, dg), rode = _ride_call(
        body, "in_bwd", (t // tm,),
        in_specs=[row(n), row(d), row(d), pl.BlockSpec((1, d), lambda i: (0, 0)),
                  pl.BlockSpec((d, n), lambda i: (0, 0), pipeline_mode=pl.Buffered(1))],
        out_specs=[row(d), pl.BlockSpec((1, d), lambda i: (0, 0))],
        out_shape=[jax.ShapeDtypeStruct((t, d), F32), jax.ShapeDtypeStruct((1, d), F32)],
        scratch_shapes=[], operands=(dproj, dh1, x, g, w_in), vmem_mb=56, ride=ride)
    return dx, dg, rode


class _GradReduce:
    def __init__(self, c_idx):
        self.c_idx = c_idx
        self.sums = {}

    def sibling(self, slabs):
        return _SiblingExchange(slabs)

    def chip(self, names, slabs, recv):
        for k, a, r in zip(names, slabs, recv):
            self.sums[k] = _chip_sum(a, r, self.c_idx, "chip_sum_" + k)
        return _ChipExchange([self.sums[k] for k in names])


def _full_weights(gathered):
    d = D_MODEL
    out = {}
    for k, a in gathered.items():
        if k in ("w_in", "w_ckv", "w_ff1"):
            out[k] = a.transpose(1, 0, 2).reshape(d, -1)
        else:
            out[k] = a.reshape(-1, d)
    return out


def _slabs(a):
    return a.reshape(N_DEV, -1, a.shape[-1])


def _local_step(x, mem, target, small, big, nb, s, tq=256, gather_rest=None, reduce=None):
    d = D_MODEL
    g_mix, g_v, w_sp, b_sp, g_head, g_cross, g_mem, g_ffn, g_fin = (
        small[k] for k in ("norm_mix_g", "gm_v_norm_g", "w_spatial", "b_spatial", "head_norm_g", "norm_cross_g",
                           "norm_mem_g", "norm_ffn_g", "norm_final_g"))
    tri = jnp.tril(jnp.ones((CHUNK, CHUNK), dtype=bool))
    w_sp_m = jnp.where(tri[None], w_sp, 0.0)
    wt = w_sp_m.astype(BF)
    wtt = jnp.swapaxes(w_sp_m, 1, 2).astype(BF)
    bb = jnp.broadcast_to(b_sp[:, :, None], (GM_GROUPS, CHUNK, CHUNK))
    hg_a = g_head[:, :GM_WIDTH]

    proj, xn = _norm_matmul(x, g_mix, big["w_in"], 1024, "in_proj")
    merged = _gmlp_fwd(proj, g_v, wt, bb, hg_a, 512)
    o_sb, tot, merged, nblk, gathered = _sb_fwd(proj, merged, g_head, nb, s, tq, ride=gather_rest)
    if gather_rest is not None:
        big = dict(big, **_full_weights(dict(zip(BIG[1:], gathered))))
    w1c = big["w_ff1"].reshape(d, 4, 1024).transpose(1, 0, 2)
    w2c = big["w_ff2"].reshape(4, 1024, d)
    kv, memn = _norm_matmul(mem, g_mem, big["w_ckv"], 512, "mem_proj")
    h1, h2, hn, qc, oc = _mix_cross_fwd(x, merged, big["w_out"], g_cross, big["w_cq"], kv, big["w_co"], s, 512)
    hn2, f, dh3, d_fin, loss = _ffn_loss_fwd(h2, g_ffn, w1c, w2c, g_fin, target, 512)

    gbig = {}
    dh2, dpre, d_ffn = _ffn_bwd(dh3, f, h2, g_ffn, w1c, w2c, 512)
    gbig["w_ff2"] = _slabs(_wgrad(f, dh3, 1024, 512, "wgrad_ff2", square_a=True, tm=2048))
    gbig["w_ff1"] = _slabs(_wgrad_wide(hn2, dpre, 1024, 512, "wgrad_ff1", col_shards=4))
    dh1, dqc, dkv, d_cross = _cross_bwd(dh2, h1, qc, g_cross, big["w_cq"], kv, big["w_co"], s, 512)
    gbig["w_co"] = _slabs(_wgrad(oc, dh2, 1024, 1024, "wgrad_co"))
    gbig["w_cq"] = _slabs(_wgrad(hn, dqc, 1024, 1024, "wgrad_cq"))
    gbig["w_ckv"] = _slabs(_wgrad(memn, dkv, 512, 1024, "wgrad_ckv", col_shards=4))
    d_mem = _mem_bwd(mem, g_mem, dkv, big["w_ckv"], 512)
    dmerged = _matmul_bt(dh1, big["w_out"], 1024, "out_bwd")
    gbig["w_out"] = _slabs(_wgrad(merged, dh1, 1024, 1024, "wgrad_out"))
    rest = BIG[1:]
    ride = reduce.sibling([gbig[k] for k in rest]) if reduce else None
    dproj, d_wsp, d_bb, d_gv, d_hga, recv = _gmlp_bwd(proj, dmerged, g_v, wt, wtt, bb, hg_a, 512, ride=ride)
    ride = reduce.chip(rest, [gbig[k] for k in rest], recv) if reduce else None
    dproj, dk, dv, d_hgb, parts_rest = _sb_bwd(proj, o_sb, tot, nblk, dmerged, dproj, g_head, nb, s, tq, ride=ride)
    dproj = _place(_place(dproj, dk, 3, "place_dk"), dv, 4, "place_dv")
    gbig["w_in"] = _slabs(_wgrad_wide(xn, dproj, 1024, 512, "wgrad_in", col_shards=4))
    last = None
    if reduce:
        recv = _run_exchange(reduce.sibling([gbig["w_in"]]), "grad_sibling_exchange_w_in")
        last = reduce.chip(["w_in"], [gbig["w_in"]], recv)
    grad_x, d_mix, _ = _in_bwd(dproj, dh1, x, g_mix, big["w_in"], 1024)
    parts = dict(zip(rest, parts_rest))

    gsmall = {
        "norm_mix_g": d_mix, "gm_v_norm_g": d_gv, "w_spatial": d_wsp, "b_spatial": d_bb[:, :, 0],
        "head_norm_g": jnp.concatenate([d_hga, jnp.sum(d_hgb, axis=0)], axis=1), "norm_cross_g": d_cross,
        "norm_mem_g": d_mem, "norm_ffn_g": d_ffn, "norm_final_g": d_fin,
    }
    return loss, grad_x, gsmall, gbig, parts, last


BIG = ("w_in", "w_out", "w_cq", "w_ckv", "w_co", "w_ff1", "w_ff2")
SMALL = ("norm_mix_g", "gm_v_norm_g", "w_spatial", "b_spatial", "head_norm_g", "norm_cross_g", "norm_mem_g",
         "norm_ffn_g", "norm_final_g")


def _local_copies_start(srcs, stages, sems):
    loads = [pltpu.make_async_copy(src, stage, sems.at[w]) for w, (src, stage) in enumerate(zip(srcs, stages))]
    for ld in loads:
        ld.start()
    return loads


def _local_copies_finish(loads, stages, dsts, sems):
    stores = []
    for w, (ld, stage, dst) in enumerate(zip(loads, stages, dsts)):
        ld.wait()
        st = pltpu.make_async_copy(stage, dst, sems.at[w])
        st.start()
        stores.append(st)
    for st in stores:
        st.wait()


def _chip_sum(slabs, recv, c_idx, name):
    _, r, cw = slabs.shape
    tr = min(r, 256)

    def body(c_ref, a_ref, b_ref, o_ref):
        del c_ref
        o_ref[...] = (a_ref[...] + b_ref[...]).astype(BF)

    return pl.pallas_call(
        body, name=name,
        grid_spec=pltpu.PrefetchScalarGridSpec(
            num_scalar_prefetch=1, grid=(N_CHIPS, r // tr),
            in_specs=[pl.BlockSpec((None, tr, cw), lambda p, i, c_ref: (2 * p + c_ref[0], i, 0)),
                      pl.BlockSpec((None, tr, cw), lambda p, i, c_ref: (p, i, 0))],
            out_specs=pl.BlockSpec((None, tr, cw), lambda p, i, c_ref: (p, i, 0))),
        out_shape=jax.ShapeDtypeStruct((N_CHIPS, r, cw), BF),
        compiler_params=_params(32, ("arbitrary", "arbitrary")),
    )(c_idx, *_in_hbm(slabs, recv))


def _sum4(sums, parts, q_idx, name):
    _, r, cw = parts.shape
    tr = min(r, 256)

    def body(q_ref, own_ref, a_ref, b_ref, c_ref, o_ref):
        del q_ref
        o_ref[...] = ((own_ref[...].astype(F32) + a_ref[...].astype(F32)) + b_ref[...].astype(F32)) + c_ref[
            ...].astype(F32)

    spec = lambda k: pl.BlockSpec((None, tr, cw), lambda i, q_ref: ((q_ref[0] + k) % N_CHIPS, i, 0))
    return pl.pallas_call(
        body, name=name,
        grid_spec=pltpu.PrefetchScalarGridSpec(
            num_scalar_prefetch=1, grid=(r // tr,), in_specs=[spec(0), spec(1), spec(2), spec(3)],
            out_specs=pl.BlockSpec((tr, cw), lambda i, q_ref: (i, 0))),
        out_shape=jax.ShapeDtypeStruct((r, cw), F32),
        compiler_params=_params(32, ("arbitrary",)),
    )(q_idx, *_in_hbm(sums, parts, parts, parts))


def _half_exchange(halves):
    n = len(halves)

    def body(*refs):
        ins, outs, stages = refs[:n], refs[n:2 * n], refs[2 * n:3 * n]
        send_sems, recv_sems, ld_sems, st_sems = refs[3 * n:]
        x, y, c = lax.axis_index("x"), lax.axis_index("y"), lax.axis_index("c")
        loads = _local_copies_start(ins, stages, ld_sems)
        copies = []
        for w in range(n):
            cp = pltpu.make_async_remote_copy(
                src_ref=ins[w], dst_ref=outs[w].at[c], send_sem=send_sems.at[w], recv_sem=recv_sems.at[w],
                device_id=(x, y, 1 - c), device_id_type=MESH)
            cp.start()
            copies.append(cp)
        _local_copies_finish(loads, stages, [outs[w].at[c] for w in range(n)], st_sems)
        for cp in copies:
            cp.wait()

    return pl.pallas_call(
        body, name="grad_half_exchange",
        in_specs=[ANY] * n, out_specs=[ANY] * n,
        out_shape=[jax.ShapeDtypeStruct((2,) + a.shape, a.dtype) for a in halves],
        scratch_shapes=[pltpu.VMEM(a.shape, a.dtype) for a in halves] + [
            pltpu.SemaphoreType.DMA((n,)), pltpu.SemaphoreType.DMA((n,)),
            pltpu.SemaphoreType.DMA((n,)), pltpu.SemaphoreType.DMA((n,))],
        compiler_params=_params(24),
    )(*halves)


def _small_all_reduce(packed, ride=None):
    rows = packed.shape[0]
    ride = ride or _NoExchange()
    ri, ro = len(ride.in_arrays), len(ride.out_shape)

    def body(*refs):
        in_ref, rins, out_ref, routs = refs[0], refs[1:1 + ri], refs[1 + ri], refs[2 + ri:2 + ri + ro]
        pair, chip_sum, chips, d2d_send, d2d_recv, ici_send, ici_recv = refs[2 + ri + ro:9 + ri + ro]
        rscr = refs[9 + ri + ro:]
        ride.start(rins, routs, rscr)
        x, y, c = lax.axis_index("x"), lax.axis_index("y"), lax.axis_index("c")
        q = 2 * x + y
        pair[c] = in_ref[...]
        swap = pltpu.make_async_remote_copy(
            src_ref=in_ref, dst_ref=pair.at[c], send_sem=d2d_send, recv_sem=d2d_recv,
            device_id=(x, y, 1 - c), device_id_type=MESH)
        swap.start()
        swap.wait()
        both = pair[0] + pair[1]
        chip_sum[...] = both
        chips[q] = both
        copies = [pltpu.make_async_remote_copy(
            src_ref=chip_sum, dst_ref=chips.at[q], send_sem=ici_send.at[k], recv_sem=ici_recv.at[k],
            device_id=(px, py, c), device_id_type=MESH) for k, (px, py) in enumerate(_other_chips(x, y))]
        for cp in copies:
            cp.start()
        for cp in copies:
            cp.wait()
        out_ref[...] = ((chips[0] + chips[1]) + chips[2]) + chips[3]
        ride.finish(rins, routs, rscr)

    vmem = pl.BlockSpec(memory_space=pltpu.VMEM)
    res = pl.pallas_call(
        body, name="small_all_reduce",
        in_specs=[vmem] + [ANY] * ri, out_specs=[vmem] + [ANY] * ro,
        out_shape=[jax.ShapeDtypeStruct(packed.shape, F32)] + list(ride.out_shape),
        scratch_shapes=[pltpu.VMEM((2, rows, 128), F32), pltpu.VMEM((rows, 128), F32),
                        pltpu.VMEM((N_CHIPS, rows, 128), F32), pltpu.SemaphoreType.DMA, pltpu.SemaphoreType.DMA,
                        pltpu.SemaphoreType.DMA((3,)), pltpu.SemaphoreType.DMA((3,))] + list(ride.scratch_shapes),
        compiler_params=_params(16),
    )(packed, *ride.in_arrays)
    return res[0], res[1:]


def _adamw(g, w, m, v, name):
    r, cw = g.shape
    tr = 256 if r % 256 == 0 else r

    def body(g_ref, w_ref, m_ref, v_ref, d_ref, nm_ref, nv_ref):
        gv = g_ref[...]
        nm = ADAM_B1 * m_ref[...] + (1.0 - ADAM_B1) * gv
        nv = ADAM_B2 * v_ref[...] + (1.0 - ADAM_B2) * (gv * gv)
        m_hat = nm / (1.0 - ADAM_B1 ** ADAM_STEP)
        v_hat = nv / (1.0 - ADAM_B2 ** ADAM_STEP)
        d_ref[...] = -ADAM_LR * (m_hat / (jnp.sqrt(v_hat) + ADAM_EPS) + ADAM_WD * w_ref[...])
        nm_ref[...] = nm
        nv_ref[...] = nv

    spec = pl.BlockSpec((tr, cw), lambda i: (i, 0))
    return pl.pallas_call(
        body, name=name, grid=(r // tr,),
        in_specs=[spec] * 4, out_specs=[spec] * 3,
        out_shape=[jax.ShapeDtypeStruct((r, cw), F32)] * 3,
        compiler_params=_params(32, ("arbitrary",)),
    )(*_in_hbm(g, w, m, v))


def _small_params(args):
    small = {k: args[k].reshape(1, -1) for k in SMALL}
    small["w_spatial"] = args["w_spatial"][0]
    small["b_spatial"] = args["b_spatial"][0]
    return small


def _pack(parts, rows):
    flat = jnp.concatenate([p.reshape(-1).astype(F32) for p in parts])
    return jnp.pad(flat, (0, rows * 128 - flat.shape[0])).reshape(rows, 128)


def _unpack(packed, shapes):
    flat = packed.reshape(-1)
    out, off = [], 0
    for shp in shapes:
        size = math.prod(shp)
        out.append(flat[off:off + size].reshape(shp))
        off += size
    return out


def kernel(x, mem, norm_mix_g, w_in, gm_v_norm_g, w_spatial, b_spatial, head_norm_g, w_out, norm_cross_g, norm_mem_g, w_cq, w_ckv, w_co, norm_ffn_g, w_ff1, w_ff2, norm_final_g, loss_target, m_norm_mix_g, m_w_in, m_gm_v_norm_g, m_w_spatial, m_b_spatial, m_head_norm_g, m_w_out, m_norm_cross_g, m_norm_mem_g, m_w_cq, m_w_ckv, m_w_co, m_norm_ffn_g, m_w_ff1, m_w_ff2, m_norm_final_g, v_norm_mix_g, v_w_in, v_gm_v_norm_g, v_w_spatial, v_b_spatial, v_head_norm_g, v_w_out, v_norm_cross_g, v_norm_mem_g, v_w_cq, v_w_ckv, v_w_co, v_norm_ffn_g, v_w_ff1, v_w_ff2, v_norm_final_g):
    args = dict(locals())
    d = D_MODEL
    nb, s, _ = x.shape
    c_idx = lax.axis_index("c").astype(jnp.int32).reshape(1)
    q_idx = (2 * lax.axis_index("x") + lax.axis_index("y")).astype(jnp.int32).reshape(1)
    rest = BIG[1:]

    shards = {k: args[k][0].astype(BF) for k in BIG}
    big = _full_weights({"w_in": _run_exchange(_GatherExchange([shards["w_in"]]), "all_gather_w_in")[0]})
    gather_rest = _GatherExchange([shards[k] for k in rest])

    reduce = _GradReduce(c_idx)
    loss, grad_x, gsmall, _, parts, last = _local_step(
        x.reshape(nb * s, d), mem.reshape(nb * N_MEM, d), loss_target.reshape(nb * s, d), _small_params(args), big,
        nb, s, gather_rest=gather_rest, reduce=reduce)

    shapes = [args[k].shape for k in SMALL]
    n_small = sum(math.prod(sh) for sh in shapes)
    rows = -(-(n_small + 1) // 1024) * 8
    reduced, (parts["w_in"],) = _small_all_reduce(_pack([gsmall[k] for k in SMALL] + [loss], rows), ride=last)
    halves = [_sum4(reduce.sums[k], parts[k], q_idx, "sum4_" + k) for k in BIG]
    both = _half_exchange(halves)

    out = {"grad_x": grad_x.reshape(nb, s, d)}
    for k, g2 in zip(BIG, both):
        shp = args[k].shape
        g = g2.reshape(shp[1], shp[2])
        dl, nm, nv = _adamw(g, args[k][0], args["m_" + k][0], args["v_" + k][0], "adamw_" + k)
        out["grad_" + k], out["delta_" + k], out["new_m_" + k], out["new_v_" + k] = (
            a.reshape(shp) for a in (g, dl, nm, nv))

    dl, nm, nv = _adamw(reduced, _pack([args[k] for k in SMALL], rows), _pack([args["m_" + k] for k in SMALL], rows),
                        _pack([args["v_" + k] for k in SMALL], rows), "adamw_small")
    for name, arr in (("grad_", reduced), ("delta_", dl), ("new_m_", nm), ("new_v_", nv)):
        for k, a in zip(SMALL, _unpack(arr, shapes)):
            out[name + k] = a
    out["loss"] = reduced.reshape(-1)[n_small]

    names = ["norm_mix_g", "w_in", "gm_v_norm_g", "w_spatial", "b_spatial", "head_norm_g", "w_out", "norm_cross_g",
             "norm_mem_g", "w_cq", "w_ckv", "w_co", "norm_ffn_g", "w_ff1", "w_ff2", "norm_final_g"]
    return (out["loss"], out["grad_x"], *[out["grad_" + k] for k in names], *[out["delta_" + k] for k in names],
            *[out["new_m_" + k] for k in names], *[out["new_v_" + k] for k in names])
```

```python
import functools
import math

import jax
import jax.numpy as jnp
from jax import lax
from jax.experimental import pallas as pl
from jax.experimental.pallas import tpu as pltpu

F32 = jnp.float32
BF = jnp.bfloat16

EPS = 1e-6
D_MODEL = 1024
CHUNK = 128
GM_GROUPS = 4
GM_WIDTH = 512
SB_WIDTH = 512
HEAD_LANES = 64
SB_SCALE = 0.125
SB_SKIP = -104.0
X_HEADS = 4
X_HEAD_DIM = 256
N_MEM = 256
D_FF = 4096
IN_COLS = 2560
N_CHIPS = 4
N_DEV = 8

ADAM_LR = 0.001
ADAM_B1 = 0.9
ADAM_B2 = 0.999
ADAM_EPS = 1e-08
ADAM_WD = 0.01
ADAM_STEP = 10

V7X_VMEM_BYTES = 64 * 1024 * 1024
MESH = pl.DeviceIdType.MESH
ANY = pl.BlockSpec(memory_space=pl.ANY)

GELU_C = math.sqrt(2.0 / math.pi)
GELU_A = 0.044715


def _params(vmem_mb, sem=None):
    assert vmem_mb * 1024 * 1024 <= V7X_VMEM_BYTES
    return pltpu.CompilerParams(vmem_limit_bytes=vmem_mb * 1024 * 1024, dimension_semantics=sem)


PIN_MIN_ELEMENTS = 1 << 18


def _in_hbm(*arrays):
    return tuple(pltpu.with_memory_space_constraint(a, pltpu.HBM) if a.size >= PIN_MIN_ELEMENTS else a
                 for a in arrays)


def _dot(a, b):
    return jnp.dot(a, b, preferred_element_type=F32)


def _dot_bt(a, b):
    return lax.dot_general(a, b, (((1,), (1,)), ((), ())), preferred_element_type=F32)


def _dot_at(a, b):
    return lax.dot_general(a, b, (((0,), (0,)), ((), ())), preferred_element_type=F32)


def _gelu(x):
    t = jnp.tanh(GELU_C * (x + GELU_A * x * x * x))
    return 0.5 * x * (1.0 + t)


def _gelu_and_grad(x):
    x2 = x * x
    t = jnp.tanh(GELU_C * (x + GELU_A * x2 * x))
    h = 0.5 * (1.0 + t)
    return x * h, h + 0.5 * x * (1.0 - t * t) * (GELU_C * (1.0 + 3.0 * GELU_A * x2))


def _rs(x):
    return lax.rsqrt(jnp.mean(x * x, axis=-1, keepdims=True) + EPS)


def _rms_bwd(dxn, xhat, r, g):
    dxh = dxn * g
    dx = r * (dxh - xhat * jnp.mean(dxh * xhat, axis=-1, keepdims=True))
    return dx, dxn * xhat


def _norm_matmul(x, g, w, tm, name):
    t, d = x.shape
    n = w.shape[1]
    tm = min(tm, t)

    def body(x_ref, g_ref, w_ref, out_ref, xn_ref):
        xv = x_ref[...]
        xn = (xv * _rs(xv) * g_ref[...]).astype(BF)
        xn_ref[...] = xn
        out_ref[...] = _dot(xn, w_ref[...]).astype(out_ref.dtype)

    return pl.pallas_call(
        body, name=name, grid=(t // tm,),
        in_specs=[pl.BlockSpec((tm, d), lambda i: (i, 0)), pl.BlockSpec((1, d), lambda i: (0, 0)),
                  pl.BlockSpec((d, n), lambda i: (0, 0))],
        out_specs=[pl.BlockSpec((tm, n), lambda i: (i, 0)), pl.BlockSpec((tm, d), lambda i: (i, 0))],
        out_shape=[jax.ShapeDtypeStruct((t, n), BF), jax.ShapeDtypeStruct((t, d), BF)],
        compiler_params=_params(48, ("parallel",)),
    )(*_in_hbm(x, g, w))


def _wgrad(a, g, tn, tk, name, square_a=False, col_shards=1, tm=1024):
    t, m = a.shape
    n = g.shape[1]
    tk = min(tk, t)
    tm = min(m, tm)
    ns = n // col_shards
    assert ns % tn == 0 and m % tm == 0
    per = ns // tn
    nk = t // tk

    def body(a_ref, g_ref, o_ref):
        k = pl.program_id(2)

        @pl.when(k == 0)
        def _():
            o_ref[...] = jnp.zeros_like(o_ref)

        av = a_ref[...]
        if square_a:
            af = av.astype(F32)
            av = af * af
        o_ref[...] += _dot_at(av.astype(BF), g_ref[...].astype(BF))

    return pl.pallas_call(
        body, name=name, grid=(m // tm, n // tn, nk),
        in_specs=[pl.BlockSpec((tk, tm), lambda i, j, k: (k, i)), pl.BlockSpec((tk, tn), lambda i, j, k: (k, j))],
        out_specs=pl.BlockSpec((None, tm, tn), lambda i, j, k: (j // per, i, j % per)),
        out_shape=jax.ShapeDtypeStruct((col_shards, m, ns), F32),
        compiler_params=_params(48, ("arbitrary", "arbitrary", "arbitrary")),
    )(*_in_hbm(a, g))


def _wgrad_wide(a, g, tm, tk, name, col_shards):
    t, m = a.shape
    n = g.shape[1]
    tk = min(tk, t)
    tm = min(tm, m)
    ns = n // col_shards

    def body(a_ref, g_ref, o_ref):
        @pl.when(pl.program_id(1) == 0)
        def _():
            o_ref[...] = jnp.zeros_like(o_ref)

        a_t = a_ref[...].astype(BF).T
        for p in range(col_shards):
            o_ref[p] += _dot(a_t, g_ref[:, p * ns:(p + 1) * ns].astype(BF))

    return pl.pallas_call(
        body, name=name, grid=(m // tm, t // tk),
        in_specs=[pl.BlockSpec((tk, tm), lambda i, k: (k, i)), pl.BlockSpec((tk, n), lambda i, k: (k, 0))],
        out_specs=pl.BlockSpec((col_shards, tm, ns), lambda i, k: (0, i, 0)),
        out_shape=jax.ShapeDtypeStruct((col_shards, m, ns), F32),
        compiler_params=_params(48, ("arbitrary", "arbitrary")),
    )(*_in_hbm(a, g))


def _matmul_bt(a, w, tm, name):
    t, n = a.shape
    k = w.shape[0]
    tm = min(tm, t)

    def body(a_ref, w_ref, o_ref):
        o_ref[...] = _dot_bt(a_ref[...].astype(BF), w_ref[...]).astype(o_ref.dtype)

    return pl.pallas_call(
        body, name=name, grid=(t // tm,),
        in_specs=[pl.BlockSpec((tm, n), lambda i: (i, 0)), pl.BlockSpec((k, n), lambda i: (0, 0))],
        out_specs=pl.BlockSpec((tm, k), lambda i: (i, 0)),
        out_shape=jax.ShapeDtypeStruct((t, k), BF),
        compiler_params=_params(32, ("parallel",)),
    )(*_in_hbm(a, w))


def _gmlp_fwd(proj, gg, wt, bb, hg, tm):
    t = proj.shape[0]
    tm = min(tm, t)

    def body(u_ref, v_ref, gg_ref, wt_ref, bb_ref, hg_ref, out_ref):
        for cc in range(tm // CHUNK):
            rows = slice(cc * CHUNK, (cc + 1) * CHUNK)
            for g in range(GM_GROUPS):
                cols = slice(g * 128, (g + 1) * 128)
                u = _gelu(u_ref[rows, cols].astype(F32))
                gv = _gelu(v_ref[rows, cols].astype(F32))
                vn = gv * _rs(gv) * gg_ref[:, cols]
                mixed = _dot(wt_ref[g], vn.astype(BF)) + bb_ref[g]
                a = u * mixed
                out_ref[rows, cols] = (a * _rs(a) * hg_ref[:, cols]).astype(BF)

    return pl.pallas_call(
        body, name="gmlp_fwd", grid=(t // tm,),
        in_specs=[pl.BlockSpec((tm, 512), lambda i: (i, 0)), pl.BlockSpec((tm, 512), lambda i: (i, 1)),
                  pl.BlockSpec((1, 512), lambda i: (0, 0)), pl.BlockSpec((4, 128, 128), lambda i: (0, 0, 0)),
                  pl.BlockSpec((4, 128, 128), lambda i: (0, 0, 0)), pl.BlockSpec((1, 512), lambda i: (0, 0))],
        out_specs=pl.BlockSpec((tm, 512), lambda i: (i, 0)),
        out_shape=jax.ShapeDtypeStruct((t, 1024), BF),
        compiler_params=_params(32, ("arbitrary",)),
    )(*_in_hbm(proj, proj, gg, wt, bb, hg))


def _gmlp_bwd(proj, dmerged, gg, wt, wtt, bb, hg, tm, ride=None):
    t = proj.shape[0]
    tm = min(tm, t)
    nsteps = t // tm

    def body(u_ref, v_ref, dm_ref, gg_ref, wt_ref, wtt_ref, bb_ref, hg_ref,
             dp_ref, dw_ref, db_ref, dgg_ref, dhg_ref):
        i = pl.program_id(0)

        @pl.when(i == 0)
        def _():
            dw_ref[...] = jnp.zeros_like(dw_ref)
            db_ref[...] = jnp.zeros_like(db_ref)
            dgg_ref[...] = jnp.zeros_like(dgg_ref)
            dhg_ref[...] = jnp.zeros_like(dhg_ref)

        for cc in range(tm // CHUNK):
            rows = slice(cc * CHUNK, (cc + 1) * CHUNK)
            for g in range(GM_GROUPS):
                cols = slice(g * 128, (g + 1) * 128)
                up = u_ref[rows, cols].astype(F32)
                gp = v_ref[rows, cols].astype(F32)
                u, u_grad = _gelu_and_grad(up)
                gv, gv_grad = _gelu_and_grad(gp)
                rv = _rs(gv)
                gvh = gv * rv
                ggv = gg_ref[:, cols]
                vnb = (gvh * ggv).astype(BF)
                mixed = _dot(wt_ref[g], vnb) + bb_ref[g]
                a = u * mixed
                ra = _rs(a)
                ah = a * ra
                dm = dm_ref[rows, cols].astype(F32)
                dhg_ref[:, cols] += jnp.sum(dm * ah, axis=0, keepdims=True)
                dah = dm * hg_ref[:, cols]
                da = ra * (dah - ah * jnp.mean(dah * ah, axis=-1, keepdims=True))
                du = da * mixed
                dmix = da * u
                db_ref[g] += dmix
                dmb = dmix.astype(BF)
                dw_ref[g] += _dot_bt(dmb, vnb)
                dvn = _dot(wtt_ref[g], dmb)
                dgg_ref[:, cols] += jnp.sum(dvn * gvh, axis=0, keepdims=True)
                dgh = dvn * ggv
                dgv = rv * (dgh - gvh * jnp.mean(dgh * gvh, axis=-1, keepdims=True))
                dp_ref[rows, cols] = (du * u_grad).astype(BF)
                dp_ref[rows, 512 + g * 128:512 + (g + 1) * 128] = (dgv * gv_grad).astype(BF)

        @pl.when(i == nsteps - 1)
        def _():
            r = lax.broadcasted_iota(jnp.int32, (CHUNK, CHUNK), 0)
            c = lax.broadcasted_iota(jnp.int32, (CHUNK, CHUNK), 1)
            for g in range(GM_GROUPS):
                dw_ref[g] = jnp.where(c <= r, dw_ref[g], 0.0)
                db_ref[g] = jnp.broadcast_to(jnp.sum(db_ref[g], axis=-1, keepdims=True), (CHUNK, CHUNK))

    small = lambda shape: pl.BlockSpec(shape, lambda i: (0,) * len(shape))
    res, rode = _ride_call(
        body, "gmlp_bwd", (nsteps,),
        in_specs=[pl.BlockSpec((tm, 512), lambda i: (i, 0)), pl.BlockSpec((tm, 512), lambda i: (i, 1)),
                  pl.BlockSpec((tm, 512), lambda i: (i, 0)), small((1, 512)), small((4, 128, 128)),
                  small((4, 128, 128)), small((4, 128, 128)), small((1, 512))],
        out_specs=[pl.BlockSpec((tm, 1024), lambda i: (i, 0)), small((4, 128, 128)), small((4, 128, 128)),
                   small((1, 512)), small((1, 512))],
        out_shape=[jax.ShapeDtypeStruct((t, IN_COLS), BF), jax.ShapeDtypeStruct((4, 128, 128), F32),
                   jax.ShapeDtypeStruct((4, 128, 128), F32), jax.ShapeDtypeStruct((1, 512), F32),
                   jax.ShapeDtypeStruct((1, 512), F32)],
        scratch_shapes=[], operands=(proj, proj, dmerged, gg, wt, wtt, bb, hg), vmem_mb=32, ride=ride)
    return (*res, rode)


def _other_chips(x, y):
    return ((1 - x, y), (x, 1 - y), (1 - x, 1 - y))


class _GatherExchange:
    def __init__(self, shards):
        n = len(shards)
        self.n = n
        self.in_arrays = list(shards)
        self.out_shape = [jax.ShapeDtypeStruct((N_CHIPS,) + a.shape, a.dtype) for a in shards]
        self.half_rows = [a.shape[0] // 2 for a in shards]
        sems = lambda k: pltpu.SemaphoreType.DMA((k,))
        self.scratch_shapes = [pltpu.VMEM(a.shape, a.dtype) for a in shards] + [
            sems(3 * n), sems(3 * n), sems(3 * n), sems(3 * n), sems(n), sems(n)]

    def _copies(self, ins, outs, scr):
        n = self.n
        stages, (ici_send, ici_recv, d2d_send, d2d_recv, ld_sems, st_sems) = scr[:n], scr[n:]
        x, y, c = lax.axis_index("x"), lax.axis_index("y"), lax.axis_index("c")
        q = 2 * x + y
        loads = [pltpu.make_async_copy(ins[w], stages[w], ld_sems.at[w]) for w in range(n)]
        stores = [pltpu.make_async_copy(stages[w], outs[w].at[q], st_sems.at[w]) for w in range(n)]
        ici, d2d = [], []
        for w in range(n):
            half = pl.ds(c * self.half_rows[w], self.half_rows[w])
            for k, (px, py) in enumerate(_other_chips(x, y)):
                ici.append(pltpu.make_async_remote_copy(
                    src_ref=ins[w].at[half], dst_ref=outs[w].at[q, half], send_sem=ici_send.at[3 * w + k],
                    recv_sem=ici_recv.at[3 * w + k], device_id=(px, py, c), device_id_type=MESH))
                landed = outs[w].at[2 * px + py, half]
                d2d.append(pltpu.make_async_remote_copy(
                    src_ref=landed, dst_ref=landed, send_sem=d2d_send.at[3 * w + k],
                    recv_sem=d2d_recv.at[3 * w + k], device_id=(x, y, 1 - c), device_id_type=MESH))
        return loads, stores, ici, d2d

    def start(self, ins, outs, scr):
        loads, stores, ici, _ = self._copies(ins, outs, scr)
        for cp in loads + ici:
            cp.start()
        for ld, st in zip(loads, stores):
            ld.wait()
            st.start()

    def relay(self, ins, outs, scr):
        _, _, ici, d2d = self._copies(ins, outs, scr)
        for got, fwd in zip(ici, d2d):
            got.wait_recv()
            fwd.start()

    def finish(self, ins, outs, scr):
        _, stores, ici, d2d = self._copies(ins, outs, scr)
        for cp in ici:
            cp.wait_send()
        for cp in d2d + stores:
            cp.wait()


class _SiblingExchange:
    def __init__(self, slabs):
        n = len(slabs)
        self.n = n
        self.in_arrays = list(slabs)
        self.out_shape = [jax.ShapeDtypeStruct((N_CHIPS,) + a.shape[1:], a.dtype) for a in slabs]
        self.scratch_shapes = [pltpu.SemaphoreType.DMA((4 * n,)), pltpu.SemaphoreType.DMA((4 * n,))]

    def _copies(self, ins, outs, scr):
        send_sems, recv_sems = scr
        x, y, c = lax.axis_index("x"), lax.axis_index("y"), lax.axis_index("c")
        return [pltpu.make_async_remote_copy(
            src_ref=ins[w].at[2 * p + (1 - c)], dst_ref=outs[w].at[p], send_sem=send_sems.at[4 * w + p],
            recv_sem=recv_sems.at[4 * w + p], device_id=(x, y, 1 - c), device_id_type=MESH)
            for w in range(self.n) for p in range(N_CHIPS)]

    def start(self, ins, outs, scr):
        for cp in self._copies(ins, outs, scr):
            cp.start()

    def finish(self, ins, outs, scr):
        for cp in self._copies(ins, outs, scr):
            cp.wait()


class _ChipExchange:
    def __init__(self, sums):
        n = len(sums)
        self.n = n
        self.in_arrays = list(sums)
        self.out_shape = [jax.ShapeDtypeStruct(a.shape, a.dtype) for a in sums]
        self.scratch_shapes = [pltpu.SemaphoreType.DMA((3 * n,)), pltpu.SemaphoreType.DMA((3 * n,))]

    def _copies(self, ins, outs, scr):
        send_sems, recv_sems = scr
        x, y, c = lax.axis_index("x"), lax.axis_index("y"), lax.axis_index("c")
        q = 2 * x + y
        return [pltpu.make_async_remote_copy(
            src_ref=ins[w].at[2 * px + py], dst_ref=outs[w].at[q], send_sem=send_sems.at[3 * w + k],
            recv_sem=recv_sems.at[3 * w + k], device_id=(px, py, c), device_id_type=MESH)
            for w in range(self.n) for k, (px, py) in enumerate(_other_chips(x, y))]

    def start(self, ins, outs, scr):
        for cp in self._copies(ins, outs, scr):
            cp.start()

    def finish(self, ins, outs, scr):
        for cp in self._copies(ins, outs, scr):
            cp.wait()


class _NoExchange:
    in_arrays, out_shape, scratch_shapes = (), (), ()

    def start(self, ins, outs, scr):
        pass

    def finish(self, ins, outs, scr):
        pass


def _run_exchange(ex, name):
    n_in, n_out = len(ex.in_arrays), len(ex.out_shape)

    def body(*refs):
        ins, outs, scr = refs[:n_in], refs[n_in:n_in + n_out], refs[n_in + n_out:]
        ex.start(ins, outs, scr)
        if hasattr(ex, "relay"):
            ex.relay(ins, outs, scr)
        ex.finish(ins, outs, scr)

    return pl.pallas_call(
        body, name=name, in_specs=[ANY] * n_in, out_specs=[ANY] * n_out, out_shape=ex.out_shape,
        scratch_shapes=ex.scratch_shapes, compiler_params=_params(24),
    )(*ex.in_arrays)


def _ride_call(body, name, grid, in_specs, out_specs, out_shape, scratch_shapes, operands, vmem_mb, ride=None,
               aliases=None):
    ride = ride or _NoExchange()
    ni, no, ns = len(in_specs), len(out_specs), len(scratch_shapes)
    ri, ro = len(ride.in_arrays), len(ride.out_shape)
    total = math.prod(grid)

    def wrapped(*refs):
        ins, rins = refs[:ni], refs[ni:ni + ri]
        outs, routs = refs[ni + ri:ni + ri + no], refs[ni + ri + no:ni + ri + no + ro]
        scr, rscr = refs[ni + ri + no + ro:ni + ri + no + ro + ns], refs[ni + ri + no + ro + ns:]
        step = pl.program_id(0)
        for ax in range(1, len(grid)):
            step = step * grid[ax] + pl.program_id(ax)

        @pl.when(step == 0)
        def _():
            ride.start(rins, routs, rscr)

        if hasattr(ride, "relay"):
            @pl.when(step == (3 * total) // 4)
            def _():
                ride.relay(rins, routs, rscr)

        body(*ins, *outs, *scr)

        @pl.when(step == total - 1)
        def _():
            ride.finish(rins, routs, rscr)

    res = pl.pallas_call(
        wrapped, name=name, grid=grid, in_specs=list(in_specs) + [ANY] * ri, out_specs=list(out_specs) + [ANY] * ro,
        out_shape=list(out_shape) + list(ride.out_shape),
        scratch_shapes=list(scratch_shapes) + list(ride.scratch_shapes), input_output_aliases=aliases or {},
        compiler_params=_params(vmem_mb, ("arbitrary",) * len(grid)),
    )(*_in_hbm(*operands), *ride.in_arrays)
    return res[:no], res[no:]


def _neg_log_sig(z):
    n = jnp.maximum(z, 0.0) + jnp.log(1.0 + jnp.exp(-jnp.abs(z)))
    return n, z - n


def _running_sums(n, tri2):
    hi = n.astype(BF)
    lo = (n - hi.astype(F32)).astype(BF)
    return _dot(jnp.concatenate([hi, lo], axis=1), tri2)


def _head_sums(x, h0):
    s0 = jnp.sum(jnp.where(h0, x, 0.0), axis=-1, keepdims=True)
    s1 = jnp.sum(jnp.where(h0, 0.0, x), axis=-1, keepdims=True)
    return jnp.where(h0, s0, s1)


SB_BLOCKS_PER_STEP = 4
SB_PAIRS_PER_STEP = 2


def _sb_masks(tq):
    h0 = lax.broadcasted_iota(jnp.int32, (tq, 128), 1) < HEAD_LANES
    r = lax.broadcasted_iota(jnp.int32, (2 * tq, tq), 0)
    c = lax.broadcasted_iota(jnp.int32, (2 * tq, tq), 1)
    return h0, c < jnp.where(r >= tq, r - tq, r)


def _sb_stack(x, h0):
    zero = jnp.zeros_like(x)
    return jnp.concatenate([jnp.where(h0, x, zero), jnp.where(h0, zero, x)], axis=0)


def _tri(tq, op):
    return op(lax.broadcasted_iota(jnp.int32, (tq, tq), 0), lax.broadcasted_iota(jnp.int32, (tq, tq), 1)).astype(BF)


def _sb_fwd(proj, merged, hg, nb, s, tq, ride=None):
    t = nb * s
    tq = min(tq, s)
    nq = s // tq
    per = min(SB_BLOCKS_PER_STEP, nq)
    ns = nq // per
    gp, ng, w = SB_PAIRS_PER_STEP, 4 // SB_PAIRS_PER_STEP, 128 * SB_PAIRS_PER_STEP

    def body(q_ref, k_ref, v_ref, hg_ref, merged_ref, o_ref, tot_ref, mb_ref, nblk_ref, acc, cr, c_min):
        del merged_ref
        h0, causal = _sb_masks(tq)
        tri_gt = _tri(tq, lambda r, c: r > c)
        tri_gt = jnp.concatenate([tri_gt, tri_gt], axis=0)
        lanes = [slice(g * 128, (g + 1) * 128) for g in range(gp)]
        zeros = jnp.zeros((2 * tq, 1), F32)

        def query_block(i, rows):
            qsts = [_sb_stack(q_ref[rows, lanes[g]] * SB_SCALE, h0) for g in range(gp)]

            def block(g, j, masked, c_in):
                start = pl.multiple_of(j * tq, tq)
                kj = k_ref[pl.ds(start, tq), lanes[g]]
                vj = v_ref[pl.ds(start, tq), lanes[g]]
                n, l = _neg_log_sig(_dot_bt(qsts[g], kj))
                if masked:
                    n = jnp.where(causal, n, 0.0)
                a = jnp.exp(l - (_running_sums(n, tri_gt) + c_in))
                if masked:
                    a = jnp.where(causal, a, 0.0)
                return _dot(a.astype(BF), vj), c_in + jnp.sum(n, axis=-1, keepdims=True)

            def keep(parts):
                for g, (p, c) in enumerate(parts):
                    acc[g] = p
                    cr[g] = c
                c_min[0] = jnp.min(functools.reduce(jnp.minimum, [c for _, c in parts]))

            @pl.when(i == 0)
            def _():
                keep([block(g, 0, True, zeros) for g in range(gp)])

            @pl.when(i > 0)
            def _():
                diag = [block(g, i, True, zeros) for g in range(gp)]
                prev = [block(g, i - 1, False, diag[g][1]) for g in range(gp)]
                keep([(diag[g][0] + prev[g][0], prev[g][1]) for g in range(gp)])

            def cond(carry):
                return jnp.logical_and(carry[0] < i, carry[1] < -SB_SKIP)

            def step(carry):
                more = [block(g, i - 1 - carry[0], False, cr[g]) for g in range(gp)]
                for g, (p, c) in enumerate(more):
                    acc[g] += p
                    cr[g] = c
                return carry[0] + 1, jnp.min(functools.reduce(jnp.minimum, [c for _, c in more]))

            walked, _ = lax.while_loop(cond, step, (jnp.minimum(i, 1), c_min[0]))
            return walked

        for u in range(per):
            rows = slice(u * tq, (u + 1) * tq)
            walked = query_block(pl.program_id(2) * per + u, rows)
            for g in range(gp):
                o = jnp.where(h0, acc[g, 0:tq, :], acc[g, tq:2 * tq, :])
                o_ref[rows, lanes[g]] = o
                tot_ref[rows, lanes[g]] = jnp.where(h0, cr[g, 0:tq, :], cr[g, tq:2 * tq, :])
                ro = lax.rsqrt(_head_sums(o * o, h0) * (1.0 / HEAD_LANES) + EPS)
                mb_ref[rows, lanes[g]] = (o * ro * hg_ref[:, lanes[g]]).astype(BF)
            nblk_ref[u * 8:(u + 1) * 8, :] = jnp.full((8, 128), walked.astype(F32))

    blk = lambda col0: pl.BlockSpec((per * tq, w), lambda b, hg_, i: (b * ns + i, col0 + hg_))
    seq = lambda col0: pl.BlockSpec((s, w), lambda b, hg_, i: (b, col0 + hg_))
    first = 1024 // w
    (o, tot, mb, nblk), rode = _ride_call(
        body, "sb_fwd", (nb, ng, ns),
        in_specs=[blk(first), seq(first + ng), seq(first + 2 * ng),
                  pl.BlockSpec((1, w), lambda b, hg_, i: (0, ng + hg_)), ANY],
        out_specs=[blk(0), blk(0), blk(ng),
                   pl.BlockSpec((None, None, per * 8, 128), lambda b, hg_, i: (b, hg_, i, 0))],
        out_shape=[jax.ShapeDtypeStruct((t, 512), F32), jax.ShapeDtypeStruct((t, 512), F32),
                   jax.ShapeDtypeStruct((t, 1024), BF), jax.ShapeDtypeStruct((nb, ng, nq * 8, 128), F32)],
        scratch_shapes=[pltpu.VMEM((gp, 2 * tq, 128), F32), pltpu.VMEM((gp, 2 * tq, 1), F32),
                        pltpu.SMEM((1,), F32)],
        operands=(proj, proj, proj, hg, merged), vmem_mb=40, ride=ride, aliases={4: 2})
    return o, tot, mb, nblk, rode


def _sb_bwd(proj, o_sb, tot, nblk, dmerged, dproj, hg, nb, s, tq, ride=None):
    t = nb * s
    tq = min(tq, s)
    nq = s // tq
    per = min(SB_BLOCKS_PER_STEP, nq)
    ns = nq // per
    gp, ng, w = SB_PAIRS_PER_STEP, 4 // SB_PAIRS_PER_STEP, 128 * SB_PAIRS_PER_STEP

    def body(q_ref, k_ref, v_ref, o_ref, tot_ref, nblk_ref, dm_ref, hg_ref, dproj_ref,
             dq_ref, dk_ref, dv_ref, dhg_ref, dk_acc, dv_acc, dq_acc, cm, cg):
        del dproj_ref
        h0, causal = _sb_masks(tq)
        tri_le = _tri(tq, lambda r, c: r <= c)
        tri_le = jnp.concatenate([tri_le, tri_le], axis=0)
        tri_lt = _tri(tq, lambda r, c: r < c)
        lanes = [slice(g * 128, (g + 1) * 128) for g in range(gp)]

        @pl.when(pl.program_id(2) == 0)
        def _():
            dk_acc[...] = jnp.zeros_like(dk_acc)
            dv_acc[...] = jnp.zeros_like(dv_acc)
            dhg_ref[...] = jnp.zeros_like(dhg_ref)

        def query_block(i, rows):
            for ref in (dq_acc, cm, cg):
                ref[...] = jnp.zeros_like(ref)
            qsts, dosts, tots = [], [], []
            for g in range(gp):
                qsts.append(_sb_stack(q_ref[rows, lanes[g]] * SB_SCALE, h0))
                o = o_ref[rows, lanes[g]]
                ro = lax.rsqrt(_head_sums(o * o, h0) * (1.0 / HEAD_LANES) + EPS)
                oh = o * ro
                dm = dm_ref[rows, lanes[g]].astype(F32)
                dhg_ref[:, lanes[g]] += jnp.sum(dm * oh, axis=0, keepdims=True)
                doh = dm * hg_ref[:, lanes[g]]
                do = ro * (doh - oh * (_head_sums(doh * oh, h0) * (1.0 / HEAD_LANES)))
                dosts.append(_sb_stack(do.astype(BF), h0))
                first = g * 128
                tots.append(jnp.concatenate(
                    [tot_ref[rows, first:first + 1], tot_ref[rows, first + HEAD_LANES:first + HEAD_LANES + 1]], axis=0))
            qsts_t = [q.T for q in qsts]
            dosts_t = [d.T for d in dosts]

            def block(g, j, masked, cm_in, cg_in):
                start = pl.multiple_of(j * tq, tq)
                kj = k_ref[pl.ds(start, tq), lanes[g]]
                vj = v_ref[pl.ds(start, tq), lanes[g]]
                n, l = _neg_log_sig(_dot_bt(qsts[g], kj))
                if masked:
                    n = jnp.where(causal, n, 0.0)
                a = jnp.exp(l - (tots[g] - cm_in - _running_sums(n, tri_le)))
                if masked:
                    a = jnp.where(causal, a, 0.0)
                gm = a * _dot_bt(dosts[g], vj)
                pp = cg_in + _dot(gm.astype(BF), tri_lt)
                dz = gm - jnp.exp(l) * (gm + pp)
                if masked:
                    dz = jnp.where(causal, dz, 0.0)
                dzb = dz.astype(BF)
                dk_acc[g, :, pl.ds(start, tq)] += _dot(qsts_t[g], dzb)
                dv_acc[g, :, pl.ds(start, tq)] += _dot(dosts_t[g], a.astype(BF))
                return (_dot(dzb, kj), cm_in + jnp.sum(n, axis=-1, keepdims=True),
                        cg_in + jnp.sum(gm, axis=-1, keepdims=True))

            def step(j, carry):
                for g in range(gp):
                    dq, cm[g], cg[g] = block(g, j, False, cm[g], cg[g])
                    dq_acc[g] += dq
                return carry

            walked = jnp.clip(nblk_ref[pl.program_id(0), pl.program_id(1), i].astype(jnp.int32),
                              jnp.minimum(i, 1), i)
            lax.fori_loop(i - walked, i - 1, step, 0)

            @pl.when(i == 0)
            def _():
                for g in range(gp):
                    dq_acc[g] = block(g, 0, True, cm[g], cg[g])[0]

            @pl.when(i > 0)
            def _():
                prev = [block(g, i - 1, False, cm[g], cg[g]) for g in range(gp)]
                diag = [block(g, i, True, prev[g][1], prev[g][2]) for g in range(gp)]
                for g in range(gp):
                    dq_acc[g] += prev[g][0] + diag[g][0]

            for g in range(gp):
                dq = jnp.where(h0, dq_acc[g, 0:tq, :], dq_acc[g, tq:2 * tq, :])
                dq_ref[rows, lanes[g]] = (dq * SB_SCALE).astype(BF)

        for u in range(per):
            query_block(pl.program_id(2) * per + u, slice(u * tq, (u + 1) * tq))

        @pl.when(pl.program_id(2) == ns - 1)
        def _():
            for g in range(gp):
                dk_ref[:, lanes[g]] = dk_acc[g].T.astype(BF)
                dv_ref[:, lanes[g]] = dv_acc[g].T.astype(BF)

    blk = lambda col0: pl.BlockSpec((per * tq, w), lambda b, hg_, i: (b * ns + i, col0 + hg_))
    seq = lambda col0: pl.BlockSpec((s, w), lambda b, hg_, i: (b, col0 + hg_))
    first = 1024 // w
    (dq, dk, dv, dhg), rode = _ride_call(
        body, "sb_bwd", (nb, ng, ns),
        in_specs=[blk(first), seq(first + ng), seq(first + 2 * ng), blk(0), blk(0),
                  pl.BlockSpec(memory_space=pltpu.SMEM), blk(ng),
                  pl.BlockSpec((1, w), lambda b, hg_, i: (0, ng + hg_)), ANY],
        out_specs=[blk(first), seq(0), seq(0), pl.BlockSpec((None, 1, w), lambda b, hg_, i: (b, 0, hg_))],
        out_shape=[jax.ShapeDtypeStruct((t, IN_COLS), BF), jax.ShapeDtypeStruct((t, 512), BF),
                   jax.ShapeDtypeStruct((t, 512), BF), jax.ShapeDtypeStruct((nb, 1, 512), F32)],
        scratch_shapes=[pltpu.VMEM((gp, 128, s), F32), pltpu.VMEM((gp, 128, s), F32),
                        pltpu.VMEM((gp, 2 * tq, 128), F32), pltpu.VMEM((gp, 2 * tq, 1), F32),
                        pltpu.VMEM((gp, 2 * tq, 1), F32)],
        operands=(proj, proj, proj, o_sb, tot, nblk.reshape(nb, ng, nq, 8, 128)[:, :, :, 0, 0], dmerged, hg, dproj),
        vmem_mb=48, ride=ride, aliases={8: 0})
    return dq, dk, dv, dhg, rode


def _place(buf, piece, col_block, name):
    t, w = piece.shape
    tm = min(t, 1024)

    def body(piece_ref, buf_ref, out_ref):
        del buf_ref
        out_ref[...] = piece_ref[...]

    return pl.pallas_call(
        body, name=name, grid=(t // tm,),
        in_specs=[pl.BlockSpec((tm, w), lambda i: (i, 0)), ANY],
        out_specs=pl.BlockSpec((tm, w), lambda i: (i, col_block)),
        out_shape=jax.ShapeDtypeStruct(buf.shape, buf.dtype), input_output_aliases={1: 0},
        compiler_params=_params(16, ("arbitrary",)),
    )(piece, buf)


def _softmax_rows(sc):
    e = jnp.exp(sc - jnp.max(sc, axis=-1, keepdims=True))
    return e / jnp.sum(e, axis=-1, keepdims=True)


def _mix_cross_fwd(x, merged, w_out, gc, w_cq, kv, w_co, s, tm):
    t, d = x.shape
    tm = min(tm, s)
    per = s // tm
    inv = 1.0 / math.sqrt(X_HEAD_DIM)

    def body(x_ref, m_ref, wo_ref, gc_ref, wq_ref, kv_ref, wc_ref, h1_ref, h2_ref, hn_ref, qc_ref, oc_ref):
        h1 = x_ref[...] + _dot(m_ref[...], wo_ref[...])
        h1_ref[...] = h1
        hn = (h1 * _rs(h1) * gc_ref[...]).astype(BF)
        hn_ref[...] = hn
        qc = _dot(hn, wq_ref[...]).astype(BF)
        qc_ref[...] = qc
        for h in range(X_HEADS):
            cols = slice(h * X_HEAD_DIM, (h + 1) * X_HEAD_DIM)
            kh = kv_ref[:, h * X_HEAD_DIM:(h + 1) * X_HEAD_DIM]
            vh = kv_ref[:, d + h * X_HEAD_DIM:d + (h + 1) * X_HEAD_DIM]
            p = _softmax_rows(_dot_bt(qc[:, cols], kh) * inv)
            oc_ref[:, cols] = _dot(p.astype(BF), vh).astype(BF)
        h2_ref[...] = h1 + _dot(oc_ref[...], wc_ref[...])

    row = lambda width: pl.BlockSpec((tm, width), lambda i: (i, 0))
    full = lambda a, b: pl.BlockSpec((a, b), lambda i: (0, 0))
    return pl.pallas_call(
        body, name="mix_cross_fwd", grid=(t // tm,),
        in_specs=[row(d), row(d), full(d, d), full(1, d), full(d, d),
                  pl.BlockSpec((N_MEM, 2 * d), lambda i: (i // per, 0)), full(d, d)],
        out_specs=[row(d), row(d), row(d), row(d), row(d)],
        out_shape=[jax.ShapeDtypeStruct((t, d), F32), jax.ShapeDtypeStruct((t, d), F32),
                   jax.ShapeDtypeStruct((t, d), BF), jax.ShapeDtypeStruct((t, d), BF),
                   jax.ShapeDtypeStruct((t, d), BF)],
        compiler_params=_params(48, ("parallel",)),
    )(*_in_hbm(x, merged, w_out, gc, w_cq, kv, w_co))


def _cross_bwd(dh2, h1, qc, gc, w_cq, kv, w_co, s, tm):
    t, d = dh2.shape
    tm = min(tm, s)
    per = s // tm
    nb = t // s
    inv = 1.0 / math.sqrt(X_HEAD_DIM)

    def body(dh2_ref, h1_ref, qc_ref, gc_ref, wq_ref, kv_ref, wc_ref, dh1_ref, dqc_ref, dkv_ref, dgc_ref):
        i = pl.program_id(0)

        @pl.when(i == 0)
        def _():
            dgc_ref[...] = jnp.zeros_like(dgc_ref)

        @pl.when(i % per == 0)
        def _():
            dkv_ref[...] = jnp.zeros_like(dkv_ref)

        dh2 = dh2_ref[...]
        h1 = h1_ref[...]
        r = _rs(h1)
        h1h = h1 * r
        gcv = gc_ref[...]
        qc = qc_ref[...]
        do = _dot_bt(dh2.astype(BF), wc_ref[...]).astype(BF)
        for h in range(X_HEADS):
            cols = slice(h * X_HEAD_DIM, (h + 1) * X_HEAD_DIM)
            vcols = slice(d + h * X_HEAD_DIM, d + (h + 1) * X_HEAD_DIM)
            kh = kv_ref[:, cols]
            vh = kv_ref[:, vcols]
            p = _softmax_rows(_dot_bt(qc[:, cols], kh) * inv)
            dp = _dot_bt(do[:, cols], vh)
            ds = (p * (dp - jnp.sum(dp * p, axis=-1, keepdims=True)) * inv).astype(BF)
            dqc_ref[:, cols] = _dot(ds, kh).astype(BF)
            dkv_ref[:, cols] += _dot_at(ds, qc[:, cols])
            dkv_ref[:, vcols] += _dot_at(p.astype(BF), do[:, cols])
        dhn = _dot_bt(dqc_ref[...], wq_ref[...])
        dx, dg = _rms_bwd(dhn, h1h, r, gcv)
        dh1_ref[...] = dh2 + dx
        dgc_ref[...] += jnp.sum(dg, axis=0, keepdims=True)

    row = lambda width: pl.BlockSpec((tm, width), lambda i: (i, 0))
    full = lambda a, b: pl.BlockSpec((a, b), lambda i: (0, 0))
    kvspec = pl.BlockSpec((N_MEM, 2 * d), lambda i: (i // per, 0))
    return pl.pallas_call(
        body, name="cross_bwd", grid=(t // tm,),
        in_specs=[row(d), row(d), row(d), full(1, d), full(d, d), kvspec, full(d, d)],
        out_specs=[row(d), row(d), kvspec, full(1, d)],
        out_shape=[jax.ShapeDtypeStruct((t, d), F32), jax.ShapeDtypeStruct((t, d), BF),
                   jax.ShapeDtypeStruct((nb * N_MEM, 2 * d), F32), jax.ShapeDtypeStruct((1, d), F32)],
        compiler_params=_params(48, ("arbitrary",)),
    )(*_in_hbm(dh2, h1, qc, gc, w_cq, kv, w_co))


def _mem_bwd(mem, gm, dkv, w_ckv, tm):
    t, d = mem.shape
    tm = min(tm, t)

    def body(mem_ref, dkv_ref, w_ref, dg_ref):
        @pl.when(pl.program_id(0) == 0)
        def _():
            dg_ref[...] = jnp.zeros_like(dg_ref)

        mv = mem_ref[...]
        dmn = _dot_bt(dkv_ref[...].astype(BF), w_ref[...])
        dg_ref[...] += jnp.sum(dmn * (mv * _rs(mv)), axis=0, keepdims=True)

    del gm
    return pl.pallas_call(
        body, name="mem_bwd", grid=(t // tm,),
        in_specs=[pl.BlockSpec((tm, d), lambda i: (i, 0)), pl.BlockSpec((tm, 2 * d), lambda i: (i, 0)),
                  pl.BlockSpec((d, 2 * d), lambda i: (0, 0))],
        out_specs=pl.BlockSpec((1, d), lambda i: (0, 0)),
        out_shape=jax.ShapeDtypeStruct((1, d), F32),
        compiler_params=_params(32, ("arbitrary",)),
    )(mem, dkv, w_ckv)


def _ffn_loss_fwd(h2, gf, w1, w2, gl, target, tm):
    t, d = h2.shape
    tm = min(tm, t)

    def body(h2_ref, gf_ref, w1_ref, w2_ref, gl_ref, tg_ref, hn_ref, f_ref, dh3_ref, dgl_ref, loss_ref):
        @pl.when(pl.program_id(0) == 0)
        def _():
            dgl_ref[...] = jnp.zeros_like(dgl_ref)
            loss_ref[...] = jnp.zeros_like(loss_ref)

        h2 = h2_ref[...]
        hn = (h2 * _rs(h2) * gf_ref[...]).astype(BF)
        hn_ref[...] = hn
        h3 = h2
        for c in range(4):
            f = jnp.maximum(_dot(hn, w1_ref[c]), 0.0)
            f_ref[:, c * 1024:(c + 1) * 1024] = f.astype(BF)
            h3 = h3 + _dot((f * f).astype(BF), w2_ref[c])
        r3 = _rs(h3)
        yh = h3 * r3
        glv = gl_ref[...]
        e = yh * glv - tg_ref[...]
        loss_ref[...] += 0.5 * jnp.sum(jnp.sum(e * e, axis=-1, keepdims=True) * (1.0 / d), axis=0, keepdims=True)
        dy = e * (1.0 / d)
        dx, dg = _rms_bwd(dy, yh, r3, glv)
        dh3_ref[...] = dx
        dgl_ref[...] += jnp.sum(dg, axis=0, keepdims=True)

    row = lambda width: pl.BlockSpec((tm, width), lambda i: (i, 0))
    return pl.pallas_call(
        body, name="ffn_loss_fwd", grid=(t // tm,),
        in_specs=[row(d), pl.BlockSpec((1, d), lambda i: (0, 0)), pl.BlockSpec((4, d, 1024), lambda i: (0, 0, 0), pipeline_mode=pl.Buffered(1)),
                  pl.BlockSpec((4, 1024, d), lambda i: (0, 0, 0), pipeline_mode=pl.Buffered(1)),
                  pl.BlockSpec((1, d), lambda i: (0, 0)), row(d)],
        out_specs=[row(d), row(D_FF), row(d), pl.BlockSpec((1, d), lambda i: (0, 0)),
                   pl.BlockSpec((1, 1), lambda i: (0, 0))],
        out_shape=[jax.ShapeDtypeStruct((t, d), BF), jax.ShapeDtypeStruct((t, D_FF), BF),
                   jax.ShapeDtypeStruct((t, d), F32), jax.ShapeDtypeStruct((1, d), F32),
                   jax.ShapeDtypeStruct((1, 1), F32)],
        compiler_params=_params(56, ("arbitrary",)),
    )(*_in_hbm(h2, gf, w1, w2, gl, target))


def _ffn_bwd(dh3, f, h2, gf, w1, w2, tm):
    t, d = h2.shape
    tm = min(tm, t)

    def body(dh3_ref, f_ref, h2_ref, gf_ref, w1_ref, w2_ref, dh2_ref, dpre_ref, dgf_ref):
        @pl.when(pl.program_id(0) == 0)
        def _():
            dgf_ref[...] = jnp.zeros_like(dgf_ref)

        dh3 = dh3_ref[...]
        dh3b = dh3.astype(BF)
        dhn = jnp.zeros((tm, d), F32)
        for c in range(4):
            cols = slice(c * 1024, (c + 1) * 1024)
            dpre = (_dot_bt(dh3b, w2_ref[c]) * (2.0 * f_ref[:, cols].astype(F32))).astype(BF)
            dpre_ref[:, cols] = dpre
            dhn = dhn + _dot_bt(dpre, w1_ref[c])
        h2 = h2_ref[...]
        r = _rs(h2)
        dx, dg = _rms_bwd(dhn, h2 * r, r, gf_ref[...])
        dh2_ref[...] = dh3 + dx
        dgf_ref[...] += jnp.sum(dg, axis=0, keepdims=True)

    row = lambda width: pl.BlockSpec((tm, width), lambda i: (i, 0))
    return pl.pallas_call(
        body, name="ffn_bwd", grid=(t // tm,),
        in_specs=[row(d), row(D_FF), row(d), pl.BlockSpec((1, d), lambda i: (0, 0)),
                  pl.BlockSpec((4, d, 1024), lambda i: (0, 0, 0), pipeline_mode=pl.Buffered(1)),
                  pl.BlockSpec((4, 1024, d), lambda i: (0, 0, 0), pipeline_mode=pl.Buffered(1))],
        out_specs=[row(d), row(D_FF), pl.BlockSpec((1, d), lambda i: (0, 0))],
        out_shape=[jax.ShapeDtypeStruct((t, d), F32), jax.ShapeDtypeStruct((t, D_FF), BF),
                   jax.ShapeDtypeStruct((1, d), F32)],
        compiler_params=_params(56, ("arbitrary",)),
    )(*_in_hbm(dh3, f, h2, gf, w1, w2))


def _in_bwd(dproj, dh1, x, g, w_in, tm, ride=None):
    t, d = x.shape
    n = w_in.shape[1]
    tm = min(tm, t)

    def body(dp_ref, dh1_ref, x_ref, g_ref, w_ref, dx_ref, dg_ref):
        @pl.when(pl.program_id(0) == 0)
        def _():
            dg_ref[...] = jnp.zeros_like(dg_ref)

        dxn = _dot_bt(dp_ref[...], w_ref[...])
        xv = x_ref[...]
        r = _rs(xv)
        dx, dg = _rms_bwd(dxn, xv * r, r, g_ref[...])
        dx_ref[...] = dh1_ref[...] + dx
        dg_ref[...] += jnp.sum(dg, axis=0, keepdims=True)

    row = lambda width: pl.BlockSpec((tm, width), lambda i: (i, 0))
    (dx, dg), rode = _ride_call(
        body, "in_bwd", (t // tm,),
        in_specs=[row(n), row(d), row(d), pl.BlockSpec((1, d), lambda i: (0, 0)),
                  pl.BlockSpec((d, n), lambda i: (0, 0))],
        out_specs=[row(d), pl.BlockSpec((1, d), lambda i: (0, 0))],
        out_shape=[jax.ShapeDtypeStruct((t, d), F32), jax.ShapeDtypeStruct((1, d), F32)],
        scratch_shapes=[], operands=(dproj, dh1, x, g, w_in), vmem_mb=48, ride=ride)
    return dx, dg, rode


class _GradReduce:
    def __init__(self, c_idx):
        self.c_idx = c_idx
        self.sums = {}

    def sibling(self, slabs):
        return _SiblingExchange(slabs)

    def chip(self, names, slabs, recv):
        for k, a, r in zip(names, slabs, recv):
            self.sums[k] = _chip_sum(a, r, self.c_idx, "chip_sum_" + k)
        return _ChipExchange([self.sums[k] for k in names])


def _full_weights(gathered):
    d = D_MODEL
    out = {}
    for k, a in gathered.items():
        if k in ("w_in", "w_ckv", "w_ff1"):
            out[k] = a.transpose(1, 0, 2).reshape(d, -1)
        else:
            out[k] = a.reshape(-1, d)
    return out


def _slabs(a):
    return a.reshape(N_DEV, -1, a.shape[-1])


def _local_step(x, mem, target, small, big, nb, s, tq=256, gather_rest=None, reduce=None):
    d = D_MODEL
    g_mix, g_v, w_sp, b_sp, g_head, g_cross, g_mem, g_ffn, g_fin = (
        small[k] for k in ("norm_mix_g", "gm_v_norm_g", "w_spatial", "b_spatial", "head_norm_g", "norm_cross_g",
                           "norm_mem_g", "norm_ffn_g", "norm_final_g"))
    tri = jnp.tril(jnp.ones((CHUNK, CHUNK), dtype=bool))
    w_sp_m = jnp.where(tri[None], w_sp, 0.0)
    wt = w_sp_m.astype(BF)
    wtt = jnp.swapaxes(w_sp_m, 1, 2).astype(BF)
    bb = jnp.broadcast_to(b_sp[:, :, None], (GM_GROUPS, CHUNK, CHUNK))
    hg_a = g_head[:, :GM_WIDTH]

    proj, xn = _norm_matmul(x, g_mix, big["w_in"], 1024, "in_proj")
    merged = _gmlp_fwd(proj, g_v, wt, bb, hg_a, 512)
    o_sb, tot, merged, nblk, gathered = _sb_fwd(proj, merged, g_head, nb, s, tq, ride=gather_rest)
    if gather_rest is not None:
        big = dict(big, **_full_weights(dict(zip(BIG[1:], gathered))))
    w1c = big["w_ff1"].reshape(d, 4, 1024).transpose(1, 0, 2)
    w2c = big["w_ff2"].reshape(4, 1024, d)
    kv, memn = _norm_matmul(mem, g_mem, big["w_ckv"], 512, "mem_proj")
    h1, h2, hn, qc, oc = _mix_cross_fwd(x, merged, big["w_out"], g_cross, big["w_cq"], kv, big["w_co"], s, 512)
    hn2, f, dh3, d_fin, loss = _ffn_loss_fwd(h2, g_ffn, w1c, w2c, g_fin, target, 512)

    gbig = {}
    dh2, dpre, d_ffn = _ffn_bwd(dh3, f, h2, g_ffn, w1c, w2c, 512)
    gbig["w_ff2"] = _slabs(_wgrad(f, dh3, 1024, 512, "wgrad_ff2", square_a=True, tm=2048))
    gbig["w_ff1"] = _slabs(_wgrad_wide(hn2, dpre, 1024, 512, "wgrad_ff1", col_shards=4))
    dh1, dqc, dkv, d_cross = _cross_bwd(dh2, h1, qc, g_cross, big["w_cq"], kv, big["w_co"], s, 512)
    gbig["w_co"] = _slabs(_wgrad(oc, dh2, 1024, 1024, "wgrad_co"))
    gbig["w_cq"] = _slabs(_wgrad(hn, dqc, 1024, 1024, "wgrad_cq"))
    gbig["w_ckv"] = _slabs(_wgrad(memn, dkv, 512, 1024, "wgrad_ckv", col_shards=4))
    d_mem = _mem_bwd(mem, g_mem, dkv, big["w_ckv"], 512)
    dmerged = _matmul_bt(dh1, big["w_out"], 1024, "out_bwd")
    gbig["w_out"] = _slabs(_wgrad(merged, dh1, 1024, 1024, "wgrad_out"))
    rest = BIG[1:]
    ride = reduce.sibling([gbig[k] for k in rest]) if reduce else None
    dproj, d_wsp, d_bb, d_gv, d_hga, recv = _gmlp_bwd(proj, dmerged, g_v, wt, wtt, bb, hg_a, 512, ride=ride)
    ride = reduce.chip(rest, [gbig[k] for k in rest], recv) if reduce else None
    dproj, dk, dv, d_hgb, parts_rest = _sb_bwd(proj, o_sb, tot, nblk, dmerged, dproj, g_head, nb, s, tq, ride=ride)
    dproj = _place(_place(dproj, dk, 3, "place_dk"), dv, 4, "place_dv")
    gbig["w_in"] = _slabs(_wgrad_wide(xn, dproj, 1024, 512, "wgrad_in", col_shards=4))
    last = None
    if reduce:
        recv = _run_exchange(reduce.sibling([gbig["w_in"]]), "grad_sibling_exchange_w_in")
        last = reduce.chip(["w_in"], [gbig["w_in"]], recv)
    grad_x, d_mix, _ = _in_bwd(dproj, dh1, x, g_mix, big["w_in"], 512)
    parts = dict(zip(rest, parts_rest))

    gsmall = {
        "norm_mix_g": d_mix, "gm_v_norm_g": d_gv, "w_spatial": d_wsp, "b_spatial": d_bb[:, :, 0],
        "head_norm_g": jnp.concatenate([d_hga, jnp.sum(d_hgb, axis=0)], axis=1), "norm_cross_g": d_cross,
        "norm_mem_g": d_mem, "norm_ffn_g": d_ffn, "norm_final_g": d_fin,
    }
    return loss, grad_x, gsmall, gbig, parts, last


BIG = ("w_in", "w_out", "w_cq", "w_ckv", "w_co", "w_ff1", "w_ff2")
SMALL = ("norm_mix_g", "gm_v_norm_g", "w_spatial", "b_spatial", "head_norm_g", "norm_cross_g", "norm_mem_g",
         "norm_ffn_g", "norm_final_g")


def _local_copies_start(srcs, stages, sems):
    loads = [pltpu.make_async_copy(src, stage, sems.at[w]) for w, (src, stage) in enumerate(zip(srcs, stages))]
    for ld in loads:
        ld.start()
    return loads


def _local_copies_finish(loads, stages, dsts, sems):
    stores = []
    for w, (ld, stage, dst) in enumerate(zip(loads, stages, dsts)):
        ld.wait()
        st = pltpu.make_async_copy(stage, dst, sems.at[w])
        st.start()
        stores.append(st)
    for st in stores:
        st.wait()


def _chip_sum(slabs, recv, c_idx, name):
    _, r, cw = slabs.shape
    tr = min(r, 256)

    def body(c_ref, a_ref, b_ref, o_ref):
        del c_ref
        o_ref[...] = (a_ref[...] + b_ref[...]).astype(BF)

    return pl.pallas_call(
        body, name=name,
        grid_spec=pltpu.PrefetchScalarGridSpec(
            num_scalar_prefetch=1, grid=(N_CHIPS, r // tr),
            in_specs=[pl.BlockSpec((None, tr, cw), lambda p, i, c_ref: (2 * p + c_ref[0], i, 0)),
                      pl.BlockSpec((None, tr, cw), lambda p, i, c_ref: (p, i, 0))],
            out_specs=pl.BlockSpec((None, tr, cw), lambda p, i, c_ref: (p, i, 0))),
        out_shape=jax.ShapeDtypeStruct((N_CHIPS, r, cw), BF),
        compiler_params=_params(32, ("arbitrary", "arbitrary")),
    )(c_idx, *_in_hbm(slabs, recv))


def _sum4(sums, parts, q_idx, name):
    _, r, cw = parts.shape
    tr = min(r, 256)

    def body(q_ref, own_ref, a_ref, b_ref, c_ref, o_ref):
        del q_ref
        o_ref[...] = ((own_ref[...].astype(F32) + a_ref[...].astype(F32)) + b_ref[...].astype(F32)) + c_ref[
            ...].astype(F32)

    spec = lambda k: pl.BlockSpec((None, tr, cw), lambda i, q_ref: ((q_ref[0] + k) % N_CHIPS, i, 0))
    return pl.pallas_call(
        body, name=name,
        grid_spec=pltpu.PrefetchScalarGridSpec(
            num_scalar_prefetch=1, grid=(r // tr,), in_specs=[spec(0), spec(1), spec(2), spec(3)],
            out_specs=pl.BlockSpec((tr, cw), lambda i, q_ref: (i, 0))),
        out_shape=jax.ShapeDtypeStruct((r, cw), F32),
        compiler_params=_params(32, ("arbitrary",)),
    )(q_idx, *_in_hbm(sums, parts, parts, parts))


def _half_exchange(halves):
    n = len(halves)

    def body(*refs):
        ins, outs, stages = refs[:n], refs[n:2 * n], refs[2 * n:3 * n]
        send_sems, recv_sems, ld_sems, st_sems = refs[3 * n:]
        x, y, c = lax.axis_index("x"), lax.axis_index("y"), lax.axis_index("c")
        loads = _local_copies_start(ins, stages, ld_sems)
        copies = []
        for w in range(n):
            cp = pltpu.make_async_remote_copy(
                src_ref=ins[w], dst_ref=outs[w].at[c], send_sem=send_sems.at[w], recv_sem=recv_sems.at[w],
                device_id=(x, y, 1 - c), device_id_type=MESH)
            cp.start()
            copies.append(cp)
        _local_copies_finish(loads, stages, [outs[w].at[c] for w in range(n)], st_sems)
        for cp in copies:
            cp.wait()

    return pl.pallas_call(
        body, name="grad_half_exchange",
        in_specs=[ANY] * n, out_specs=[ANY] * n,
        out_shape=[jax.ShapeDtypeStruct((2,) + a.shape, a.dtype) for a in halves],
        scratch_shapes=[pltpu.VMEM(a.shape, a.dtype) for a in halves] + [
            pltpu.SemaphoreType.DMA((n,)), pltpu.SemaphoreType.DMA((n,)),
            pltpu.SemaphoreType.DMA((n,)), pltpu.SemaphoreType.DMA((n,))],
        compiler_params=_params(24),
    )(*halves)


def _small_all_reduce(packed, ride=None):
    rows = packed.shape[0]
    ride = ride or _NoExchange()
    ri, ro = len(ride.in_arrays), len(ride.out_shape)

    def body(*refs):
        in_ref, rins, out_ref, routs = refs[0], refs[1:1 + ri], refs[1 + ri], refs[2 + ri:2 + ri + ro]
        pair, chip_sum, chips, d2d_send, d2d_recv, ici_send, ici_recv = refs[2 + ri + ro:9 + ri + ro]
        rscr = refs[9 + ri + ro:]
        ride.start(rins, routs, rscr)
        x, y, c = lax.axis_index("x"), lax.axis_index("y"), lax.axis_index("c")
        q = 2 * x + y
        pair[c] = in_ref[...]
        swap = pltpu.make_async_remote_copy(
            src_ref=in_ref, dst_ref=pair.at[c], send_sem=d2d_send, recv_sem=d2d_recv,
            device_id=(x, y, 1 - c), device_id_type=MESH)
        swap.start()
        swap.wait()
        both = pair[0] + pair[1]
        chip_sum[...] = both
        chips[q] = both
        copies = [pltpu.make_async_remote_copy(
            src_ref=chip_sum, dst_ref=chips.at[q], send_sem=ici_send.at[k], recv_sem=ici_recv.at[k],
            device_id=(px, py, c), device_id_type=MESH) for k, (px, py) in enumerate(_other_chips(x, y))]
        for cp in copies:
            cp.start()
        for cp in copies:
            cp.wait()
        out_ref[...] = ((chips[0] + chips[1]) + chips[2]) + chips[3]
        ride.finish(rins, routs, rscr)

    vmem = pl.BlockSpec(memory_space=pltpu.VMEM)
    res = pl.pallas_call(
        body, name="small_all_reduce",
        in_specs=[vmem] + [ANY] * ri, out_specs=[vmem] + [ANY] * ro,
        out_shape=[jax.ShapeDtypeStruct(packed.shape, F32)] + list(ride.out_shape),
        scratch_shapes=[pltpu.VMEM((2, rows, 128), F32), pltpu.VMEM((rows, 128), F32),
                        pltpu.VMEM((N_CHIPS, rows, 128), F32), pltpu.SemaphoreType.DMA, pltpu.SemaphoreType.DMA,
                        pltpu.SemaphoreType.DMA((3,)), pltpu.SemaphoreType.DMA((3,))] + list(ride.scratch_shapes),
        compiler_params=_params(16),
    )(packed, *ride.in_arrays)
    return res[0], res[1:]


def _adamw(g, w, m, v, name):
    r, cw = g.shape
    tr = 256 if r % 256 == 0 else r

    def body(g_ref, w_ref, m_ref, v_ref, d_ref, nm_ref, nv_ref):
        gv = g_ref[...]
        nm = ADAM_B1 * m_ref[...] + (1.0 - ADAM_B1) * gv
        nv = ADAM_B2 * v_ref[...] + (1.0 - ADAM_B2) * (gv * gv)
        m_hat = nm / (1.0 - ADAM_B1 ** ADAM_STEP)
        v_hat = nv / (1.0 - ADAM_B2 ** ADAM_STEP)
        d_ref[...] = -ADAM_LR * (m_hat / (jnp.sqrt(v_hat) + ADAM_EPS) + ADAM_WD * w_ref[...])
        nm_ref[...] = nm
        nv_ref[...] = nv

    spec = pl.BlockSpec((tr, cw), lambda i: (i, 0))
    return pl.pallas_call(
        body, name=name, grid=(r // tr,),
        in_specs=[spec] * 4, out_specs=[spec] * 3,
        out_shape=[jax.ShapeDtypeStruct((r, cw), F32)] * 3,
        compiler_params=_params(32, ("parallel",)),
    )(*_in_hbm(g, w, m, v))


def _small_params(args):
    small = {k: args[k].reshape(1, -1) for k in SMALL}
    small["w_spatial"] = args["w_spatial"][0]
    small["b_spatial"] = args["b_spatial"][0]
    return small


def _pack(parts, rows):
    flat = jnp.concatenate([p.reshape(-1).astype(F32) for p in parts])
    return jnp.pad(flat, (0, rows * 128 - flat.shape[0])).reshape(rows, 128)


def _unpack(packed, shapes):
    flat = packed.reshape(-1)
    out, off = [], 0
    for shp in shapes:
        size = math.prod(shp)
        out.append(flat[off:off + size].reshape(shp))
        off += size
    return out


def kernel(x, mem, norm_mix_g, w_in, gm_v_norm_g, w_spatial, b_spatial, head_norm_g, w_out, norm_cross_g, norm_mem_g, w_cq, w_ckv, w_co, norm_ffn_g, w_ff1, w_ff2, norm_final_g, loss_target, m_norm_mix_g, m_w_in, m_gm_v_norm_g, m_w_spatial, m_b_spatial, m_head_norm_g, m_w_out, m_norm_cross_g, m_norm_mem_g, m_w_cq, m_w_ckv, m_w_co, m_norm_ffn_g, m_w_ff1, m_w_ff2, m_norm_final_g, v_norm_mix_g, v_w_in, v_gm_v_norm_g, v_w_spatial, v_b_spatial, v_head_norm_g, v_w_out, v_norm_cross_g, v_norm_mem_g, v_w_cq, v_w_ckv, v_w_co, v_norm_ffn_g, v_w_ff1, v_w_ff2, v_norm_final_g):
    args = dict(locals())
    d = D_MODEL
    nb, s, _ = x.shape
    c_idx = lax.axis_index("c").astype(jnp.int32).reshape(1)
    q_idx = (2 * lax.axis_index("x") + lax.axis_index("y")).astype(jnp.int32).reshape(1)
    rest = BIG[1:]

    shards = {k: args[k][0].astype(BF) for k in BIG}
    big = _full_weights({"w_in": _run_exchange(_GatherExchange([shards["w_in"]]), "all_gather_w_in")[0]})
    gather_rest = _GatherExchange([shards[k] for k in rest])

    reduce = _GradReduce(c_idx)
    loss, grad_x, gsmall, _, parts, last = _local_step(
        x.reshape(nb * s, d), mem.reshape(nb * N_MEM, d), loss_target.reshape(nb * s, d), _small_params(args), big,
        nb, s, gather_rest=gather_rest, reduce=reduce)

    shapes = [args[k].shape for k in SMALL]
    n_small = sum(math.prod(sh) for sh in shapes)
    rows = -(-(n_small + 1) // 1024) * 8
    reduced, (parts["w_in"],) = _small_all_reduce(_pack([gsmall[k] for k in SMALL] + [loss], rows), ride=last)
    halves = [_sum4(reduce.sums[k], parts[k], q_idx, "sum4_" + k) for k in BIG]
    both = _half_exchange(halves)

    out = {"grad_x": grad_x.reshape(nb, s, d)}
    for k, g2 in zip(BIG, both):
        shp = args[k].shape
        g = g2.reshape(shp[1], shp[2])
        dl, nm, nv = _adamw(g, args[k][0], args["m_" + k][0], args["v_" + k][0], "adamw_" + k)
        out["grad_" + k], out["delta_" + k], out["new_m_" + k], out["new_v_" + k] = (
            a.reshape(shp) for a in (g, dl, nm, nv))

    dl, nm, nv = _adamw(reduced, _pack([args[k] for k in SMALL], rows), _pack([args["m_" + k] for k in SMALL], rows),
                        _pack([args["v_" + k] for k in SMALL], rows), "adamw_small")
    for name, arr in (("grad_", reduced), ("delta_", dl), ("new_m_", nm), ("new_v_", nv)):
        for k, a in zip(SMALL, _unpack(arr, shapes)):
            out[name + k] = a
    out["loss"] = reduced.reshape(-1)[n_small]

    names = ["norm_mix_g", "w_in", "gm_v_norm_g", "w_spatial", "b_spatial", "head_norm_g", "w_out", "norm_cross_g",
             "norm_mem_g", "w_cq", "w_ckv", "w_co", "norm_ffn_g", "w_ff1", "w_ff2", "norm_final_g"]
    return (out["loss"], out["grad_x"], *[out["grad_" + k] for k in names], *[out["delta_" + k] for k in names],
            *[out["new_m_" + k] for k in names], *[out["new_v_" + k] for k in names])
```
